```python
import math
import jax
import jax.numpy as jnp
from jax import lax
import numpy as np

D_MODEL = 1024
BATCH = 8
SEQ = 4096
DEPTH = 1

ATT_HEAD_DIM = 64
ATT_HEADS = 8
ATT_GROUPS = ((128, 1), (512, 4), (2048, 16))
N_ATT_GROUPS = 3
ATT_QK_WIDTH = N_ATT_GROUPS * ATT_HEADS * ATT_HEAD_DIM
ATT_OUT_WIDTH = ATT_HEADS * ATT_HEAD_DIM
ATT_BLOCK = 128
NEG_INF = -1e30

HG_HEADS = 8
HG_EXPAND = 128
HG_WIDTH = HG_HEADS * HG_EXPAND
HG_HEAD_V = HG_WIDTH // HG_HEADS
HG_CHUNK = 32
RMS_EPS = 1e-6

SPLIT_SIZES = (ATT_QK_WIDTH,) * 3 + (HG_WIDTH,) * 4 + (D_MODEL,) * 2
N_IN = 3 * ATT_QK_WIDTH + 4 * HG_WIDTH + 2 * D_MODEL

N_EXPERTS = 64
TOP_K = 8
N_EXPERT_GROUPS = 8
TOPK_GROUPS = 4
EXPERT_FF = 256
SHARED_FF = 256
ROUTED_SCALE = 2.5
MOE_BLOCK = 512

PLE_DIM = 256
LN_EPS = 1e-5
DEEPNORM_ALPHA = (2.0 * DEPTH) ** 0.25
DEEPNORM_BETA = (8.0 * DEPTH) ** -0.25

kernel_name = 'hybrid_dilated_attn_hgrn2_moe_block'


def layer_norm(x, w, b):
    xf = x.astype(jnp.float32)
    mu = xf.mean(-1, keepdims=True)
    var = jnp.mean(jnp.square(xf - mu), -1, keepdims=True)
    return ((xf - mu) * lax.rsqrt(var + LN_EPS) * w.astype(jnp.float32) + b.astype(jnp.float32)).astype(x.dtype)


def alibi_slopes():
    return jnp.asarray(np.array([2.0 ** (-8.0 * (h + 1) / ATT_HEADS) for h in range(ATT_HEADS)], np.float32))


def dilated_window_group(q, k, v, window, dilation):
    B, S, H, Dh = q.shape
    w_sub = window // dilation
    L = S // dilation
    nb = -(-L // ATT_BLOCK)
    pad = nb * ATT_BLOCK - L

    def to_blocks(t):
        t = t.astype(jnp.float32).reshape(B, L, dilation, H, Dh)
        t = jnp.pad(t, ((0, 0), (0, pad), (0, 0), (0, 0), (0, 0)))
        return t.reshape(B, nb, ATT_BLOCK, dilation, H, Dh)

    def with_previous_block(t):
        prev = jnp.pad(t[:, :-1], ((0, 0), (1, 0), (0, 0), (0, 0), (0, 0), (0, 0)))
        return jnp.concatenate([prev, t], axis=2)

    qb = to_blocks(q) * (Dh ** -0.5)
    kc = with_previous_block(to_blocks(k))
    vc = with_previous_block(to_blocks(v))

    qi = jnp.arange(ATT_BLOCK)[:, None]
    ki = jnp.arange(2 * ATT_BLOCK)[None, :]
    steps = qi + ATT_BLOCK - ki
    key_idx = jnp.arange(nb)[:, None] * ATT_BLOCK - ATT_BLOCK + jnp.arange(2 * ATT_BLOCK)[None, :]
    valid = ((steps >= 0) & (steps <= w_sub))[None] & (key_idx >= 0)[:, None, :]
    bias = -alibi_slopes()[:, None, None] * (steps * dilation).astype(jnp.float32)[None]

    s = jnp.einsum('bnqrhd,bnkrhd->bnrhqk', qb, kc) + bias[None, None, None]
    s = jnp.where(valid[None, :, None, None], s, NEG_INF)
    m = s.max(-1)
    pexp = jnp.exp(s - m[..., None])
    den = pexp.sum(-1)
    num = jnp.einsum('bnrhqk,bnkrhd->bnqrhd', pexp, vc)

    def back(t):
        t = t.reshape((B, nb * ATT_BLOCK, dilation) + t.shape[4:])[:, :L]
        return t.reshape((B, S) + t.shape[3:])

    return back(num), back(m.transpose(0, 1, 4, 2, 3)), back(den.transpose(0, 1, 4, 2, 3))


def dilated_attention(q, k, v):
    parts = [dilated_window_group(q[:, :, g], k[:, :, g], v[:, :, g], win, dil)
             for g, (win, dil) in enumerate(ATT_GROUPS)]
    m_all = jnp.stack([pt[1] for pt in parts])
    scale = jnp.exp(m_all - m_all.max(0))
    num = sum(scale[g][..., None] * parts[g][0] for g in range(N_ATT_GROUPS))
    den = sum(scale[g] * parts[g][2] for g in range(N_ATT_GROUPS))
    return num / den[..., None]


def hgrn2_recurrence(q, f_logit, i, lb):
    B, S, _ = q.shape
    n_chunks = S // HG_CHUNK
    forget = lb + (1.0 - lb) * jax.nn.sigmoid(f_logit.astype(jnp.float32))
    log_f = jnp.log(forget)
    key = 1.0 - forget

    def chunked(t):
        t = t.astype(jnp.float32).reshape(B, n_chunks, HG_CHUNK, HG_HEADS, -1)
        return t.transpose(1, 0, 3, 2, 4)

    causal = jnp.tril(jnp.ones((HG_CHUNK, HG_CHUNK), dtype=bool))

    def step(state, inp):
        qc, kc, vc, gc = inp
        b = jnp.cumsum(gc, axis=2)
        b_last = b[:, :, -1:]
        q_dec = qc * jnp.exp(b)
        k_inv = kc * jnp.exp(-b)
        k_end = kc * jnp.exp(b_last - b)
        a = jnp.where(causal, jnp.einsum('bhtk,bhsk->bhts', q_dec, k_inv), 0.0)
        o = jnp.einsum('bhts,bhsv->bhtv', a, vc) + jnp.einsum('bhtk,bhkv->bhtv', q_dec, state)
        state = jnp.exp(b_last)[:, :, 0, :, None] * state + jnp.einsum('bhsk,bhsv->bhkv', k_end, vc)
        return state, o

    state0 = jnp.zeros((B, HG_HEADS, HG_EXPAND, HG_HEAD_V), jnp.float32)
    _, o = lax.scan(step, state0, (chunked(q), chunked(key), chunked(i), chunked(log_f)))
    return o.transpose(1, 0, 3, 2, 4).reshape(B, S, HG_HEADS, HG_HEAD_V)


def rms_norm_heads(o, gain):
    B, S, H, dv = o.shape
    o = o * lax.rsqrt(jnp.mean(jnp.square(o), -1, keepdims=True) + RMS_EPS)
    return (o * gain.astype(jnp.float32).reshape(H, dv)).reshape(B, S, H * dv)


def swiglu(h, w_gate, w_up, w_down):
    return (jax.nn.silu(h @ w_gate) * (h @ w_up)) @ w_down


def route(h, w_router, bias):
    T = h.shape[0]
    s = jax.nn.sigmoid(h.astype(jnp.float32) @ w_router.astype(jnp.float32))
    sel = s + bias.astype(jnp.float32)
    grp = sel.reshape(T, N_EXPERT_GROUPS, N_EXPERTS // N_EXPERT_GROUPS)
    grp_score = lax.top_k(grp, 2)[0].sum(-1)
    _, top_grp = lax.top_k(grp_score, TOPK_GROUPS)
    grp_mask = jnp.any(top_grp[..., :, None] == jnp.arange(N_EXPERT_GROUPS), axis=-2)
    expert_mask = jnp.repeat(grp_mask, N_EXPERTS // N_EXPERT_GROUPS, axis=-1)
    _, idx = lax.top_k(jnp.where(expert_mask, sel, -jnp.inf), TOP_K)
    g = jnp.take_along_axis(s, idx, axis=-1)
    g = g / g.sum(-1, keepdims=True) * ROUTED_SCALE
    return idx, g


def routed_experts(h, idx, gate, w_gate, w_up, w_down):
    T, D = h.shape
    A = T * TOP_K
    e_flat = idx.reshape(A).astype(jnp.int32)
    tok_flat = (jnp.arange(A, dtype=jnp.int32) // TOP_K)
    g_flat = gate.reshape(A)
    order = jnp.argsort(e_flat)
    e_s, tok_s, g_s = e_flat[order], tok_flat[order], g_flat[order]
    counts = jnp.bincount(e_flat, length=N_EXPERTS).astype(jnp.int32)
    padded = ((counts + MOE_BLOCK - 1) // MOE_BLOCK) * MOE_BLOCK
    start = jnp.cumsum(counts) - counts
    pend = jnp.cumsum(padded)
    pstart = pend - padded
    dest = pstart[e_s] + (jnp.arange(A, dtype=jnp.int32) - start[e_s])
    n_blocks = -(-(A + N_EXPERTS * (MOE_BLOCK - 1)) // MOE_BLOCK)
    P = n_blocks * MOE_BLOCK
    buf_tok = jnp.full((P,), T, jnp.int32).at[dest].set(tok_s)
    buf_g = jnp.zeros((P,), h.dtype).at[dest].set(g_s.astype(h.dtype))
    block_expert = jnp.minimum(
        jnp.searchsorted(pend, jnp.arange(n_blocks, dtype=jnp.int32) * MOE_BLOCK, side='right'),
        N_EXPERTS - 1).astype(jnp.int32)
    h_pad = jnp.concatenate([h, jnp.zeros((1, D), h.dtype)], axis=0)

    def step(acc, blk):
        tok_b, g_b, e = blk
        xb = h_pad[tok_b]
        a = jax.nn.silu(xb @ w_gate[e]) * (xb @ w_up[e])
        out = (a @ w_down[e]) * g_b[:, None]
        return acc.at[tok_b].add(out.astype(acc.dtype)), None

    acc, _ = lax.scan(step, jnp.zeros((T + 1, D), h.dtype),
                      (buf_tok.reshape(n_blocks, MOE_BLOCK), buf_g.reshape(n_blocks, MOE_BLOCK), block_expert))
    return acc[:T]


def setup_inputs(seed: int = 0) -> dict:
    key = jax.random.key(seed)
    ks = jax.random.split(key, 24)
    f32 = jnp.float32
    beta = DEEPNORM_BETA

    def nrm(k, shape, fan_in, scale=1.0):
        return jax.random.normal(k, shape, f32) * (fan_in ** -0.5 * scale)

    col_scale = np.ones((N_IN,), np.float32)
    v0 = 2 * ATT_QK_WIDTH
    col_scale[v0:v0 + ATT_QK_WIDTH] = beta
    i0 = 3 * ATT_QK_WIDTH + 2 * HG_WIDTH
    col_scale[i0:i0 + HG_WIDTH] = beta

    return {
        'x': jax.random.normal(ks[0], (BATCH, SEQ, D_MODEL), f32),
        'p': jax.random.normal(ks[1], (DEPTH, BATCH, SEQ, PLE_DIM), f32),
        'w_in': nrm(ks[2], (DEPTH, D_MODEL, N_IN), D_MODEL) * jnp.asarray(col_scale),
        'hgrn_lb_logits': 0.1 * jax.random.normal(ks[3], (DEPTH + 1, HG_WIDTH), f32),
        'hgrn_norm_w': 1.0 + 0.05 * jax.random.normal(ks[4], (DEPTH, HG_WIDTH), f32),
        'w_branch_att': nrm(ks[5], (DEPTH, ATT_OUT_WIDTH, D_MODEL), ATT_OUT_WIDTH, beta),
        'w_branch_hgrn': nrm(ks[6], (DEPTH, HG_WIDTH, D_MODEL), HG_WIDTH, beta),
        'w_out': nrm(ks[7], (DEPTH, D_MODEL, D_MODEL), D_MODEL, beta),
        'ln1_w': 1.0 + 0.05 * jax.random.normal(ks[8], (DEPTH, D_MODEL), f32),
        'ln1_b': 0.02 * jax.random.normal(ks[9], (DEPTH, D_MODEL), f32),
        'router_w': nrm(ks[10], (DEPTH, D_MODEL, N_EXPERTS), D_MODEL),
        'router_bias': 0.01 * jax.random.normal(ks[11], (DEPTH, N_EXPERTS), f32),
        'expert_w_gate': nrm(ks[12], (DEPTH, N_EXPERTS, D_MODEL, EXPERT_FF), D_MODEL),
        'expert_w_up': nrm(ks[13], (DEPTH, N_EXPERTS, D_MODEL, EXPERT_FF), D_MODEL, beta),
        'expert_w_down': nrm(ks[14], (DEPTH, N_EXPERTS, EXPERT_FF, D_MODEL), EXPERT_FF, beta),
        'shared_w_gate': nrm(ks[15], (DEPTH, D_MODEL, SHARED_FF), D_MODEL),
        'shared_w_up': nrm(ks[16], (DEPTH, D_MODEL, SHARED_FF), D_MODEL, beta),
        'shared_w_down': nrm(ks[17], (DEPTH, SHARED_FF, D_MODEL), SHARED_FF, beta),
        'ple_gate_w': nrm(ks[18], (DEPTH, D_MODEL, D_MODEL), D_MODEL),
        'ple_proj_w': nrm(ks[19], (DEPTH, PLE_DIM, D_MODEL), PLE_DIM, beta),
        'ln2_w': 1.0 + 0.05 * jax.random.normal(ks[20], (DEPTH, D_MODEL), f32),
        'ln2_b': 0.02 * jax.random.normal(ks[21], (DEPTH, D_MODEL), f32),
    }


def reference(x, p, w_in, hgrn_lb_logits, hgrn_norm_w, w_branch_att, w_branch_hgrn, w_out,
              ln1_w, ln1_b, router_w, router_bias, expert_w_gate, expert_w_up, expert_w_down,
              shared_w_gate, shared_w_up, shared_w_down, ple_gate_w, ple_proj_w, ln2_w, ln2_b):
    B, S, D = x.shape
    split_points = np.cumsum(SPLIT_SIZES)[:-1].tolist()
    lower_bounds = jnp.cumsum(jax.nn.softmax(hgrn_lb_logits.astype(jnp.float32), axis=0), axis=0)

    def att_heads(t):
        return t.reshape(B, S, N_ATT_GROUPS, ATT_HEADS, ATT_HEAD_DIM)

    for l in range(DEPTH):
        u = x @ w_in[l]
        q_a, k_a, v_a, q_h, f_h, i_h, g_h, gate_a, gate_h = jnp.split(u, split_points, axis=-1)
        y_att = dilated_attention(att_heads(q_a), att_heads(k_a), att_heads(v_a))
        y_att = y_att.reshape(B, S, ATT_OUT_WIDTH).astype(x.dtype)
        o_h = hgrn2_recurrence(q_h, f_h, jax.nn.silu(i_h), lower_bounds[l])
        y_hg = (rms_norm_heads(o_h, hgrn_norm_w[l]) * jax.nn.silu(g_h.astype(jnp.float32))).astype(x.dtype)
        merged = (jax.nn.sigmoid(gate_a) * (y_att @ w_branch_att[l])
                  + jax.nn.sigmoid(gate_h) * (y_hg @ w_branch_hgrn[l]))
        x1 = layer_norm(DEEPNORM_ALPHA * x + merged @ w_out[l], ln1_w[l], ln1_b[l])

        h = x1.reshape(B * S, D)
        idx, gate = route(h, router_w[l], router_bias[l])
        y_moe = (routed_experts(h, idx, gate, expert_w_gate[l], expert_w_up[l], expert_w_down[l])
                 + swiglu(h, shared_w_gate[l], shared_w_up[l], shared_w_down[l]))
        ple = jax.nn.sigmoid(x1 @ ple_gate_w[l]) * (p[l] @ ple_proj_w[l])
        x = layer_norm(DEEPNORM_ALPHA * x1 + y_moe.reshape(B, S, D) + ple, ln2_w[l], ln2_b[l])
    return x
```

```python
import functools

import jax
import jax.numpy as jnp
import numpy as np
from jax import lax
from jax.experimental import pallas as pl
from jax.experimental.pallas import tpu as pltpu

F32 = jnp.float32
BF16 = jnp.bfloat16

D_MODEL = 1024
ATT_HEAD_DIM = 64
ATT_HEADS = 8
ATT_DILATIONS = (1, 4, 16)
ATT_BLOCK = 128
ATT_WIDTH = ATT_HEADS * ATT_HEAD_DIM
ATT_TILE = ATT_BLOCK * max(ATT_DILATIONS)
NEG_INF = -1e30

HG_HEADS = 8
HG_DIM = 128
HG_WIDTH = HG_HEADS * HG_DIM
HG_CHUNK = 32
HG_TILE = 256
RMS_EPS = 1e-6

N_EXPERTS = 64
TOP_K = 8
N_GROUPS = 8
GROUP_SIZE = N_EXPERTS // N_GROUPS
TOPK_GROUPS = 4
EXPERT_FF = 256
ROUTED_SCALE = 2.5
MOE_BLOCK = 512
PLE_DIM = 256
LN_EPS = 1e-5
DEPTH = 1
DEEPNORM_ALPHA = (2.0 * DEPTH) ** 0.25

LANES = 128
LANE_CHUNKS = D_MODEL // LANES
PROJ_ROWS = 512
MIX_ROWS = 512
DISPATCH_ROWS = 256
COMBINE_ROWS = 128
V7X_VMEM_LIMIT = 56 * 1024 * 1024


def _cparams(*sem):
    return pltpu.CompilerParams(dimension_semantics=sem, vmem_limit_bytes=V7X_VMEM_LIMIT)


def _proj_att_kernel(*refs, dil):
    x_refs, w_ref, o_ref = refs[:LANE_CHUNKS], refs[LANE_CHUNKS], refs[LANE_CHUNKS + 1]
    n = PROJ_ROWS // dil

    def rows(ref):
        if dil == 1:
            return ref[...]
        return jnp.concatenate([ref[pl.ds(r, n, stride=dil), :] for r in range(dil)], axis=0)

    xp = jnp.concatenate([rows(ref).astype(BF16) for ref in x_refs], axis=1)
    y = jnp.dot(xp, w_ref[...], preferred_element_type=F32)
    o_ref[...] = y.astype(BF16).reshape(dil, n, 3 * ATT_WIDTH)


def _proj_att(x2d, w, dil):
    T = x2d.shape[0]
    per = ATT_TILE // PROJ_ROWS
    n = PROJ_ROWS // dil
    out = pl.pallas_call(
        functools.partial(_proj_att_kernel, dil=dil),
        grid=(T // PROJ_ROWS,),
        in_specs=[pl.BlockSpec((PROJ_ROWS, LANES), functools.partial(lambda i, c: (i, c), c=c))
                  for c in range(LANE_CHUNKS)]
                 + [pl.BlockSpec((D_MODEL, 3 * ATT_WIDTH), lambda i: (0, 0))],
        out_specs=pl.BlockSpec((None, dil, None, n, 3 * ATT_WIDTH), lambda i: (i // per, 0, i % per, 0, 0)),
        out_shape=jax.ShapeDtypeStruct((T // ATT_TILE, dil, per, n, 3 * ATT_WIDTH), BF16),
        compiler_params=_cparams("parallel"),
        name=f"proj_att_d{dil}",
    )(*([x2d] * LANE_CHUNKS), w)
    return out.reshape(T // ATT_TILE, dil, ATT_TILE // dil, 3 * ATT_WIDTH)


def _att_pair(q2, kp, kc, vp, vc, bias_ref, g, first):
    def head0_lanes(rows, dtype):
        lane = lax.broadcasted_iota(jnp.int32, (rows, 2 * ATT_HEAD_DIM), 1)
        return lane.astype(F32).astype(dtype) < ATT_HEAD_DIM

    lo_q = head0_lanes(ATT_BLOCK, BF16)
    lo_v = head0_lanes(2 * ATT_BLOCK, BF16)
    k2 = jnp.concatenate([kp, kc], axis=0)
    v2 = jnp.concatenate([vp, vc], axis=0)
    zero = jnp.zeros_like(q2)
    ps, ms = [], []
    for hh in range(2):
        qm = jnp.where(lo_q, q2, zero) if hh == 0 else jnp.where(lo_q, zero, q2)
        s = lax.dot_general(qm, k2, (((1,), (1,)), ((), ())), preferred_element_type=F32)
        s = s + bias_ref[g, hh, first]
        m = jnp.max(s, axis=-1, keepdims=True)
        ps.append(jnp.exp(s - m).astype(BF16))
        ms.append(m)
    pcat = jnp.concatenate(ps, axis=1)
    zero_v, one_v = jnp.zeros_like(v2), jnp.ones_like(v2)
    rhs = jnp.concatenate([
        jnp.concatenate([jnp.where(lo_v, v2, zero_v), jnp.where(lo_v, one_v, zero_v)], axis=1),
        jnp.concatenate([jnp.where(lo_v, zero_v, v2), jnp.where(lo_v, zero_v, one_v)], axis=1)], axis=0)
    nd = jnp.dot(pcat, rhs, preferred_element_type=F32)
    m2 = jnp.where(head0_lanes(ATT_BLOCK, F32), ms[0], ms[1])
    return nd[:, :2 * ATT_HEAD_DIM], m2, nd[:, 2 * ATT_HEAD_DIM:]


def _att_kernel(*refs):
    (q0, kc0, vc0, kp0, vp0, q1, kc1, vc1, kp1, vp1, q2, kc2, vc2, kp2, vp2,
     bias_ref, o_ref) = refs[:17]
    ng = len(ATT_DILATIONS)
    num_s, m_s, den_s = refs[17:17 + ng], refs[17 + ng:17 + 2 * ng], refs[17 + 2 * ng:]
    first_tile = (pl.program_id(2) == 0).astype(jnp.int32)
    groups = ((q0, kc0, vc0, kp0, vp0), (q1, kc1, vc1, kp1, vp1), (q2, kc2, vc2, kp2, vp2))
    for g, dil in enumerate(ATT_DILATIONS):
        q_ref, kc_ref, vc_ref, kp_ref, vp_ref = groups[g]
        nb = ATT_TILE // dil // ATT_BLOCK
        for r in range(dil):
            for n in range(nb):
                rows = pl.ds(n * ATT_BLOCK, ATT_BLOCK)
                if n == 0:
                    prev = pl.ds((nb - 1) * ATT_BLOCK, ATT_BLOCK)
                    kp, vp, first = kp_ref[r, prev, :], vp_ref[r, prev, :], first_tile
                else:
                    prev = pl.ds((n - 1) * ATT_BLOCK, ATT_BLOCK)
                    kp, vp, first = kc_ref[r, prev, :], vc_ref[r, prev, :], 0
                num, m, den = _att_pair(q_ref[r, rows, :], kp, kc_ref[r, rows, :], vp, vc_ref[r, rows, :],
                                        bias_ref, g, first)
                if dil == 1:
                    dst = rows
                else:
                    dst = pl.ds(n * ATT_BLOCK * dil + r, ATT_BLOCK, stride=dil)
                num_s[g][dst, :] = num
                m_s[g][dst, :] = m
                den_s[g][dst, :] = den
    m_all = jnp.maximum(jnp.maximum(m_s[0][...], m_s[1][...]), m_s[2][...])
    num = jnp.zeros((ATT_TILE, 2 * ATT_HEAD_DIM), F32)
    den = jnp.zeros((ATT_TILE, 2 * ATT_HEAD_DIM), F32)
    for g in range(ng):
        sc = jnp.exp(m_s[g][...] - m_all)
        num = num + sc * num_s[g][...]
        den = den + sc * den_s[g][...]
    o_ref[...] = (num / den).astype(o_ref.dtype)


def _att_bias_table():
    qi = np.arange(ATT_BLOCK)[:, None]
    ki = np.arange(2 * ATT_BLOCK)[None, :]
    steps = qi + ATT_BLOCK - ki
    valid = (steps >= 0) & (steps <= ATT_BLOCK)
    slopes = np.array([2.0 ** (-8.0 * (h + 1) / ATT_HEADS) for h in range(ATT_HEADS)], np.float32)
    tab = np.empty((len(ATT_DILATIONS), ATT_HEADS, 2, ATT_BLOCK, 2 * ATT_BLOCK), np.float32)
    for g, dil in enumerate(ATT_DILATIONS):
        bias = -slopes[:, None, None] * (steps * dil).astype(np.float32)[None]
        tab[g, :, 0] = np.where(valid[None], bias, NEG_INF)
        tab[g, :, 1] = np.where((valid & (ki >= ATT_BLOCK))[None], bias, NEG_INF)
    return jnp.asarray(tab)


def _attention(qkv, B, S):
    tiles = S // ATT_TILE
    pair = 2 * ATT_HEAD_DIM
    npair = ATT_WIDTH // pair
    in_specs, args = [], []
    for g, dil in enumerate(ATT_DILATIONS):
        blk = (None, dil, ATT_TILE // dil, pair)
        cur = lambda b, hp, t, off: (b * tiles + t, 0, 0, off * npair + hp)
        prv = lambda b, hp, t, off: (b * tiles + jnp.maximum(t - 1, 0), 0, 0, off * npair + hp)
        in_specs += [pl.BlockSpec(blk, functools.partial(cur, off=0)),
                     pl.BlockSpec(blk, functools.partial(cur, off=1)),
                     pl.BlockSpec(blk, functools.partial(cur, off=2)),
                     pl.BlockSpec(blk, functools.partial(prv, off=1)),
                     pl.BlockSpec(blk, functools.partial(prv, off=2))]
        args += [qkv[g]] * 5
    in_specs.append(pl.BlockSpec((len(ATT_DILATIONS), 2, 2, ATT_BLOCK, 2 * ATT_BLOCK),
                                 lambda b, hp, t: (0, hp, 0, 0, 0)))
    args.append(_att_bias_table())
    scratch = [pltpu.VMEM((ATT_TILE, pair), F32) for _ in range(3 * len(ATT_DILATIONS))]
    return pl.pallas_call(
        _att_kernel,
        grid=(B, npair, tiles),
        in_specs=in_specs,
        out_specs=pl.BlockSpec((ATT_TILE, pair), lambda b, hp, t: (b * tiles + t, hp)),
        out_shape=jax.ShapeDtypeStruct((B * S, ATT_WIDTH), BF16),
        scratch_shapes=scratch,
        compiler_params=_cparams("parallel", "parallel", "arbitrary"),
        name="dilated_attention",
    )(*args)


def _att_weights(w_in_l):
    out = []
    width = len(ATT_DILATIONS) * ATT_WIDTH
    for g in range(len(ATT_DILATIONS)):
        cols = [w_in_l[:, part * width + g * ATT_WIDTH: part * width + (g + 1) * ATT_WIDTH] for part in range(3)]
        cols[0] = cols[0] * (ATT_HEAD_DIM ** -0.5)
        out.append(jnp.concatenate(cols, axis=1).astype(BF16))
    return out


def _proj_kernel(x_ref, w_ref, o_ref):
    o_ref[...] = jnp.dot(x_ref[...].astype(BF16), w_ref[...], preferred_element_type=F32).astype(o_ref.dtype)


def _proj(x2d, w, col_tile):
    T, N = x2d.shape[0], w.shape[1]
    return pl.pallas_call(
        _proj_kernel,
        grid=(T // PROJ_ROWS, N // col_tile),
        in_specs=[pl.BlockSpec((PROJ_ROWS, D_MODEL), lambda i, j: (i, 0)),
                  pl.BlockSpec((D_MODEL, col_tile), lambda i, j: (0, j))],
        out_specs=pl.BlockSpec((PROJ_ROWS, col_tile), lambda i, j: (i, j)),
        out_shape=jax.ShapeDtypeStruct((T, N), BF16),
        compiler_params=_cparams("parallel", "arbitrary"),
        name="proj_hgrn_gates",
    )(x2d, w)


def _split3(v):
    a = v.astype(BF16)
    r = v - a.astype(F32)
    b = r.astype(BF16)
    c = (r - b.astype(F32)).astype(BF16)
    return a, b, c


def _hgrn_kernel(q_ref, f_ref, i_ref, g_ref, lbl_ref, gain_ref, o_ref, state_ref):
    @pl.when(pl.program_id(1) == 0)
    def _():
        state_ref[...] = jnp.zeros_like(state_ref)

    lbl = lbl_ref[...]
    e = jnp.exp(lbl - jnp.max(lbl, axis=0, keepdims=True))
    lb = e[0:1] / jnp.sum(e, axis=0, keepdims=True)
    forget = lb + (1.0 - lb) * jax.nn.sigmoid(f_ref[...].astype(F32))
    log_f = jnp.log(forget)
    key = 1.0 - forget

    row = lax.broadcasted_iota(jnp.int32, (HG_TILE, HG_TILE), 0)
    col = lax.broadcasted_iota(jnp.int32, (HG_TILE, HG_TILE), 1)
    causal = (row >= col) & ((row // HG_CHUNK) == (col // HG_CHUNK))
    tri = jnp.where(causal, 1.0, 0.0).astype(BF16)
    b = sum(jnp.dot(tri, t, preferred_element_type=F32) for t in _split3(log_f))
    eb = jnp.exp(b)
    q_dec = (q_ref[...].astype(F32) * eb).astype(BF16)
    k_inv = key * jnp.exp(-b)
    xi = i_ref[...].astype(F32)
    val = (xi * jax.nn.sigmoid(xi)).astype(BF16)
    k_inv_b = k_inv.astype(BF16)

    n_chunks = HG_TILE // HG_CHUNK
    outs = []
    for h in range(HG_HEADS):
        cols = slice(h * HG_DIM, (h + 1) * HG_DIM)
        qd, ki, vv = q_dec[:, cols], k_inv_b[:, cols], val[:, cols]
        a = lax.dot_general(qd, ki, (((1,), (1,)), ((), ())), preferred_element_type=F32)
        a = jnp.where(causal, a, 0.0).astype(BF16)
        o_intra = jnp.dot(a, vv, preferred_element_type=F32)
        st = state_ref[h]
        pieces = []
        for c in range(n_chunks):
            rows = slice(c * HG_CHUNK, (c + 1) * HG_CHUNK)
            dec = eb[(c + 1) * HG_CHUNK - 1:(c + 1) * HG_CHUNK, cols]
            o_inter = lax.dot_general(qd[rows], st.astype(BF16), (((1,), (1,)), ((), ())),
                                      preferred_element_type=F32)
            pieces.append(o_intra[rows] + o_inter)
            k_end = (k_inv[rows, cols] * dec).astype(BF16)
            st = st * dec + lax.dot_general(vv[rows], k_end, (((0,), (0,)), ((), ())),
                                            preferred_element_type=F32)
        state_ref[h] = st
        o = jnp.concatenate(pieces, axis=0)
        o = o * lax.rsqrt(jnp.mean(jnp.square(o), axis=-1, keepdims=True) + RMS_EPS)
        outs.append(o)
    o = jnp.concatenate(outs, axis=1) * gain_ref[...]
    gg = g_ref[...].astype(F32)
    o_ref[...] = (o * (gg * jax.nn.sigmoid(gg))).astype(o_ref.dtype)


def _hgrn(u_hg, lb_logits, gain, B, S):
    tiles = S // HG_TILE
    col = lambda j: pl.BlockSpec((HG_TILE, HG_WIDTH), functools.partial(lambda b, t, j: (b * tiles + t, j), j=j))
    return pl.pallas_call(
        _hgrn_kernel,
        grid=(B, tiles),
        in_specs=[col(0), col(1), col(2), col(3),
                  pl.BlockSpec((2, HG_WIDTH), lambda b, t: (0, 0)),
                  pl.BlockSpec((1, HG_WIDTH), lambda b, t: (0, 0))],
        out_specs=pl.BlockSpec((HG_TILE, HG_WIDTH), lambda b, t: (b * tiles + t, 0)),
        out_shape=jax.ShapeDtypeStruct((B * S, HG_WIDTH), BF16),
        scratch_shapes=[pltpu.VMEM((HG_HEADS, HG_DIM, HG_DIM), F32)],
        compiler_params=_cparams("parallel", "arbitrary"),
        name="hgrn2",
    )(u_hg, u_hg, u_hg, u_hg, lb_logits, gain)


def _load_row_tiles(ref, n, start=0):
    return jnp.concatenate([ref[pl.ds(start + c, n, stride=LANE_CHUNKS), :] for c in range(LANE_CHUNKS)], axis=1)


def _store_row_tiles(ref, val, n):
    for c in range(LANE_CHUNKS):
        ref[pl.ds(c, n, stride=LANE_CHUNKS), :] = val[:, c * LANES:(c + 1) * LANES]


def _layer_norm(z, w, b):
    mu = jnp.mean(z, axis=-1, keepdims=True)
    zc = z - mu
    var = jnp.mean(jnp.square(zc), axis=-1, keepdims=True)
    return zc * lax.rsqrt(var + LN_EPS) * w + b


def _merge_kernel(ya_ref, yh_ref, ga_ref, gh_ref, x_ref, wa_ref, wh_ref, wo_ref, lw_ref, lb_ref, o_ref):
    ma = jnp.dot(ya_ref[...], wa_ref[...], preferred_element_type=F32)
    mh = jnp.dot(yh_ref[...], wh_ref[...], preferred_element_type=F32)
    merged = (jax.nn.sigmoid(ga_ref[...].astype(F32)) * ma + jax.nn.sigmoid(gh_ref[...].astype(F32)) * mh)
    z = DEEPNORM_ALPHA * x_ref[...] + jnp.dot(merged.astype(BF16), wo_ref[...], preferred_element_type=F32)
    _store_row_tiles(o_ref, _layer_norm(z, lw_ref[...], lb_ref[...]), MIX_ROWS)


def _merge(y_att, y_hg, u_hg, x2d, w_a, w_h, w_o, ln_w, ln_b):
    T = x2d.shape[0]
    rows = lambda width, j=0: pl.BlockSpec((MIX_ROWS, width), functools.partial(lambda i, j: (i, j), j=j))
    full = lambda a: pl.BlockSpec(a.shape, lambda i: (0, 0))
    return pl.pallas_call(
        _merge_kernel,
        grid=(T // MIX_ROWS,),
        in_specs=[rows(ATT_WIDTH), rows(HG_WIDTH), rows(D_MODEL, 4), rows(D_MODEL, 5), rows(D_MODEL),
                  full(w_a), full(w_h), full(w_o), full(ln_w), full(ln_b)],
        out_specs=pl.BlockSpec((MIX_ROWS * LANE_CHUNKS, LANES), lambda i: (i, 0)),
        out_shape=jax.ShapeDtypeStruct((T * LANE_CHUNKS, LANES), F32),
        compiler_params=_cparams("parallel"),
        name="merge_ln1",
    )(y_att, y_hg, u_hg, u_hg, x2d, w_a, w_h, w_o, ln_w, ln_b)


def _first_argmax(v, ids, n):
    mx = jnp.max(v, axis=0, keepdims=True)
    return mx, jnp.min(jnp.where(v == mx, ids, n), axis=0, keepdims=True)


def _route_kernel(x1_ref, p_ref, wrt_ref, rb_ref, wsg_ref, wsu_ref, wsd_ref, wpg_ref, wpp_ref,
                  base_ref, idx_ref, gate_ref, rank_ref, cnt_ref, carry_ref):
    @pl.when(pl.program_id(0) == 0)
    def _():
        carry_ref[...] = jnp.zeros_like(carry_ref)

    x1 = _load_row_tiles(x1_ref, MIX_ROWS)
    x1b = x1.astype(BF16)
    logits = lax.dot_general(wrt_ref[...], x1, (((1,), (1,)), ((), ())), preferred_element_type=F32,
                             precision=lax.Precision.HIGHEST)
    s = jax.nn.sigmoid(logits)
    sel = s + rb_ref[...]
    eid = lax.broadcasted_iota(jnp.int32, (N_EXPERTS, MIX_ROWS), 0)
    neg = -jnp.inf

    grp = sel.reshape(N_GROUPS, GROUP_SIZE, MIX_ROWS)
    mid = lax.broadcasted_iota(jnp.int32, grp.shape, 1)
    m1 = jnp.max(grp, axis=1, keepdims=True)
    i1 = jnp.min(jnp.where(grp == m1, mid, GROUP_SIZE), axis=1, keepdims=True)
    m2 = jnp.max(jnp.where(mid == i1, neg, grp), axis=1, keepdims=True)
    gscore = (m1 + m2).reshape(N_GROUPS, MIX_ROWS)
    gid = lax.broadcasted_iota(jnp.int32, (N_GROUPS, MIX_ROWS), 0)
    gsel = jnp.zeros((N_GROUPS, MIX_ROWS), jnp.bool_)
    for _ in range(TOPK_GROUPS):
        _, gi = _first_argmax(gscore, gid, N_GROUPS)
        hit = gid == gi
        gsel = gsel | hit
        gscore = jnp.where(hit, neg, gscore)
    emask = jnp.broadcast_to(gsel.reshape(N_GROUPS, 1, MIX_ROWS), grp.shape).reshape(N_EXPERTS, MIX_ROWS)
    cand = jnp.where(emask, sel, neg)

    idxs, gates = [], []
    chosen = jnp.zeros((N_EXPERTS, MIX_ROWS), jnp.bool_)
    for _ in range(TOP_K):
        _, ei = _first_argmax(cand, eid, N_EXPERTS)
        hit = eid == ei
        idxs.append(ei)
        gates.append(jnp.sum(jnp.where(hit, s, 0.0), axis=0, keepdims=True))
        chosen = chosen | hit
        cand = jnp.where(hit, neg, cand)
    g = jnp.concatenate(gates, axis=0)
    g = g / jnp.sum(g, axis=0, keepdims=True) * ROUTED_SCALE
    idx_ref[...] = jnp.concatenate(idxs, axis=0)
    gate_ref[...] = g

    onehot = jnp.where(chosen, 1.0, 0.0)
    tr = lax.broadcasted_iota(jnp.int32, (MIX_ROWS, MIX_ROWS), 0)
    tc = lax.broadcasted_iota(jnp.int32, (MIX_ROWS, MIX_ROWS), 1)
    before = jnp.where(tr < tc, 1.0, 0.0).astype(BF16)
    prefix = jnp.dot(onehot.astype(BF16), before, preferred_element_type=F32)
    rankfull = (carry_ref[:, 0:1] + prefix).astype(jnp.int32)
    rank_ref[...] = jnp.concatenate(
        [jnp.sum(jnp.where(eid == ei, rankfull, 0), axis=0, keepdims=True) for ei in idxs], axis=0)
    total = carry_ref[...] + jnp.sum(onehot, axis=1, keepdims=True)
    carry_ref[...] = total
    cnt_ref[...] = total.astype(jnp.int32)

    hg = jnp.dot(x1b, wsg_ref[...], preferred_element_type=F32)
    hu = jnp.dot(x1b, wsu_ref[...], preferred_element_type=F32)
    shared = jnp.dot((hg * jax.nn.sigmoid(hg) * hu).astype(BF16), wsd_ref[...], preferred_element_type=F32)
    ple = (jax.nn.sigmoid(jnp.dot(x1b, wpg_ref[...], preferred_element_type=F32))
           * jnp.dot(p_ref[...].astype(BF16), wpp_ref[...], preferred_element_type=F32))
    base_ref[...] = DEEPNORM_ALPHA * x1 + shared + ple


def _route(x1, p2d, wr_t, rbias, wsg, wsu, wsd, wpg, wpp):
    T = x1.shape[0] // LANE_CHUNKS
    full = lambda a: pl.BlockSpec(a.shape, lambda i: (0, 0))
    tok = pl.BlockSpec((TOP_K, MIX_ROWS), lambda i: (0, i))
    return pl.pallas_call(
        _route_kernel,
        grid=(T // MIX_ROWS,),
        in_specs=[pl.BlockSpec((MIX_ROWS * LANE_CHUNKS, LANES), lambda i: (i, 0)),
                  pl.BlockSpec((MIX_ROWS, PLE_DIM), lambda i: (i, 0)),
                  full(wr_t), full(rbias), full(wsg), full(wsu), full(wsd), full(wpg), full(wpp)],
        out_specs=[pl.BlockSpec((MIX_ROWS, D_MODEL), lambda i: (i, 0)), tok, tok, tok,
                   pl.BlockSpec((N_EXPERTS, LANES), lambda i: (0, 0))],
        out_shape=[jax.ShapeDtypeStruct((T, D_MODEL), F32),
                   jax.ShapeDtypeStruct((TOP_K, T), jnp.int32),
                   jax.ShapeDtypeStruct((TOP_K, T), F32),
                   jax.ShapeDtypeStruct((TOP_K, T), jnp.int32),
                   jax.ShapeDtypeStruct((N_EXPERTS, LANES), jnp.int32)],
        scratch_shapes=[pltpu.VMEM((N_EXPERTS, LANES), F32)],
        compiler_params=_cparams("arbitrary"),
        name="route_shared_ple",
    )(x1, p2d, wr_t, rbias, wsg, wsu, wsd, wpg, wpp)


def _dispatch_kernel(pend_ref, dest_ref, x_ref, xs_ref, zero_ref, sem):
    tile = lambda ref, r, n=1: ref.at[pl.ds(pl.multiple_of(r * LANE_CHUNKS, LANE_CHUNKS), n * LANE_CHUNKS), :]
    row_copy = lambda t, d: pltpu.make_async_copy(tile(x_ref, t), tile(xs_ref, d), sem)

    @pl.when(pl.program_id(0) == 0)
    def _():
        zero_ref[...] = jnp.zeros_like(zero_ref)
        fill = lambda e: pltpu.make_async_copy(
            zero_ref, tile(xs_ref, jnp.maximum(pend_ref[e] - MOE_BLOCK, 0), MOE_BLOCK), sem)

        def start(e, c):
            fill(e).start()
            return c

        def wait(e, c):
            fill(e).wait()
            return c

        lax.fori_loop(0, N_EXPERTS, start, 0)
        lax.fori_loop(0, N_EXPERTS, wait, 0)

    def start_row(t, c):
        for k in range(TOP_K):
            row_copy(t, dest_ref[k, t]).start()
        return c

    def wait_row(t, c):
        for k in range(TOP_K):
            row_copy(t, 0).wait()
        return c

    lax.fori_loop(0, DISPATCH_ROWS, start_row, 0)
    lax.fori_loop(0, DISPATCH_ROWS, wait_row, 0)


def _dispatch(x1, dest, pend, n_rows):
    T = x1.shape[0] // LANE_CHUNKS
    return pl.pallas_call(
        _dispatch_kernel,
        grid_spec=pltpu.PrefetchScalarGridSpec(
            num_scalar_prefetch=1,
            grid=(T // DISPATCH_ROWS,),
            in_specs=[pl.BlockSpec((TOP_K, DISPATCH_ROWS), lambda i, pend: (0, i), memory_space=pltpu.SMEM),
                      pl.BlockSpec((DISPATCH_ROWS * LANE_CHUNKS, LANES), lambda i, pend: (i, 0))],
            out_specs=pl.BlockSpec(memory_space=pl.ANY),
            scratch_shapes=[pltpu.VMEM((MOE_BLOCK * LANE_CHUNKS, LANES), F32), pltpu.SemaphoreType.DMA(())],
        ),
        out_shape=jax.ShapeDtypeStruct((n_rows * LANE_CHUNKS, LANES), F32),
        compiler_params=_cparams("arbitrary"),
        name="moe_dispatch",
    )(pend, dest, x1)


def _expert_kernel(be_ref, nused_ref, x_ref, wg_ref, wu_ref, wd_ref, o_ref):
    @pl.when(pl.program_id(0) < nused_ref[0])
    def _():
        xb = _load_row_tiles(x_ref, MOE_BLOCK).astype(BF16)
        hg = jnp.dot(xb, wg_ref[...], preferred_element_type=F32)
        hu = jnp.dot(xb, wu_ref[...], preferred_element_type=F32)
        act = (hg * jax.nn.sigmoid(hg) * hu).astype(BF16)
        _store_row_tiles(o_ref, jnp.dot(act, wd_ref[...], preferred_element_type=F32), MOE_BLOCK)


def _experts(xs, block_expert, n_used, wg, wu, wd):
    n_blocks = xs.shape[0] // (MOE_BLOCK * LANE_CHUNKS)
    live = lambda i, be, nu: jnp.minimum(i, nu[0] - 1)
    return pl.pallas_call(
        _expert_kernel,
        grid_spec=pltpu.PrefetchScalarGridSpec(
            num_scalar_prefetch=2,
            grid=(n_blocks,),
            in_specs=[pl.BlockSpec((MOE_BLOCK * LANE_CHUNKS, LANES), lambda i, be, nu: (live(i, be, nu), 0)),
                      pl.BlockSpec((None, D_MODEL, EXPERT_FF), lambda i, be, nu: (be[live(i, be, nu)], 0, 0)),
                      pl.BlockSpec((None, D_MODEL, EXPERT_FF), lambda i, be, nu: (be[live(i, be, nu)], 0, 0)),
                      pl.BlockSpec((None, EXPERT_FF, D_MODEL), lambda i, be, nu: (be[live(i, be, nu)], 0, 0))],
            out_specs=pl.BlockSpec((MOE_BLOCK * LANE_CHUNKS, LANES), lambda i, be, nu: (live(i, be, nu), 0)),
        ),
        out_shape=jax.ShapeDtypeStruct(xs.shape, F32),
        compiler_params=_cparams("arbitrary"),
        name="moe_experts",
    )(block_expert, n_used, xs, wg, wu, wd)


def _combine_kernel(dest_ref, gate_ref, base_ref, lw_ref, lb_ref, ys_ref, o_ref, buf_ref, sem):
    tile = lambda ref, r: ref.at[pl.ds(pl.multiple_of(r * LANE_CHUNKS, LANE_CHUNKS), LANE_CHUNKS), :]
    row_copy = lambda k, t, d: pltpu.make_async_copy(tile(ys_ref, d), tile(buf_ref, k * COMBINE_ROWS + t), sem)

    def start_row(t, c):
        for k in range(TOP_K):
            row_copy(k, t, dest_ref[k, t]).start()
        return c

    def wait_row(t, c):
        for k in range(TOP_K):
            row_copy(k, t, 0).wait()
        return c

    lax.fori_loop(0, COMBINE_ROWS, start_row, 0)
    lax.fori_loop(0, COMBINE_ROWS, wait_row, 0)
    gate = gate_ref[...]
    z = base_ref[...]
    for k in range(TOP_K):
        z = z + gate[:, k:k + 1] * _load_row_tiles(buf_ref, COMBINE_ROWS, k * COMBINE_ROWS * LANE_CHUNKS)
    o_ref[...] = _layer_norm(z, lw_ref[...], lb_ref[...])


def _combine(ys, dest, gate_t, base, ln_w, ln_b):
    T = base.shape[0]
    return pl.pallas_call(
        _combine_kernel,
        grid=(T // COMBINE_ROWS,),
        in_specs=[pl.BlockSpec((TOP_K, COMBINE_ROWS), lambda i: (0, i), memory_space=pltpu.SMEM),
                  pl.BlockSpec((COMBINE_ROWS, TOP_K), lambda i: (i, 0)),
                  pl.BlockSpec((COMBINE_ROWS, D_MODEL), lambda i: (i, 0)),
                  pl.BlockSpec((1, D_MODEL), lambda i: (0, 0)),
                  pl.BlockSpec((1, D_MODEL), lambda i: (0, 0)),
                  pl.BlockSpec(memory_space=pl.ANY)],
        out_specs=pl.BlockSpec((COMBINE_ROWS, D_MODEL), lambda i: (i, 0)),
        out_shape=jax.ShapeDtypeStruct((T, D_MODEL), F32),
        scratch_shapes=[pltpu.VMEM((TOP_K * COMBINE_ROWS * LANE_CHUNKS, LANES), F32), pltpu.SemaphoreType.DMA(())],
        compiler_params=_cparams("arbitrary"),
        name="moe_combine_ln2",
    )(dest, gate_t, base, ln_w, ln_b, ys)


def kernel(x, p, w_in, hgrn_lb_logits, hgrn_norm_w, w_branch_att, w_branch_hgrn, w_out, ln1_w, ln1_b, router_w, router_bias, expert_w_gate, expert_w_up, expert_w_down, shared_w_gate, shared_w_up, shared_w_down, ple_gate_w, ple_proj_w, ln2_w, ln2_b):
    B, S, D = x.shape
    T = B * S
    l = 0
    x2d = x.reshape(T, D)
    bf = lambda a: a.astype(BF16)

    ws = _att_weights(w_in[l])
    qkv = [_proj_att(x2d, ws[g], d) for g, d in enumerate(ATT_DILATIONS)]
    y_att = _attention(qkv, B, S)
    u_hg = _proj(x2d, bf(w_in[l][:, 3 * len(ATT_DILATIONS) * ATT_WIDTH:]), 1536)
    y_hg = _hgrn(u_hg, hgrn_lb_logits, hgrn_norm_w[l:l + 1], B, S)
    x1 = _merge(y_att, y_hg, u_hg, x2d, bf(w_branch_att[l]), bf(w_branch_hgrn[l]), bf(w_out[l]),
                ln1_w[l:l + 1], ln1_b[l:l + 1])

    base, idx, gate, rank, counts = _route(
        x1, p[l].reshape(T, PLE_DIM), router_w[l].T, router_bias[l].reshape(N_EXPERTS, 1),
        bf(shared_w_gate[l]), bf(shared_w_up[l]), bf(shared_w_down[l]), bf(ple_gate_w[l]), bf(ple_proj_w[l]))
    counts = counts[:, 0]
    padded = (counts + MOE_BLOCK - 1) // MOE_BLOCK * MOE_BLOCK
    pend = jnp.cumsum(padded)
    n_blocks = -(-(T * TOP_K + N_EXPERTS * (MOE_BLOCK - 1)) // MOE_BLOCK)
    dest = (pend - padded)[idx] + rank
    n_used = (pend[-1:] // MOE_BLOCK).astype(jnp.int32)
    block_expert = jnp.minimum(
        jnp.searchsorted(pend, jnp.arange(n_blocks, dtype=jnp.int32) * MOE_BLOCK, side='right'),
        N_EXPERTS - 1).astype(jnp.int32)
    xs = _dispatch(x1, dest, pend.astype(jnp.int32), n_blocks * MOE_BLOCK)
    ys = _experts(xs, block_expert, n_used, bf(expert_w_gate[l]), bf(expert_w_up[l]), bf(expert_w_down[l]))
    out = _combine(ys, dest, gate.T, base, ln2_w[l:l + 1], ln2_b[l:l + 1])
    return out.reshape(B, S, D)
```

```python
import functools

import jax
import jax.numpy as jnp
import numpy as np
from jax import lax
from jax.experimental import pallas as pl
from jax.experimental.pallas import tpu as pltpu

F32 = jnp.float32
BF16 = jnp.bfloat16

D_MODEL = 1024
ATT_HEAD_DIM = 64
ATT_HEADS = 8
ATT_DILATIONS = (1, 4, 16)
ATT_BLOCK = 128
ATT_WIDTH = ATT_HEADS * ATT_HEAD_DIM
ATT_TILE = ATT_BLOCK * max(ATT_DILATIONS)
NEG_INF = -1e30

HG_HEADS = 8
HG_DIM = 128
HG_WIDTH = HG_HEADS * HG_DIM
HG_CHUNK = 32
HG_TILE = 256
RMS_EPS = 1e-6

N_EXPERTS = 64
TOP_K = 8
N_GROUPS = 8
GROUP_SIZE = N_EXPERTS // N_GROUPS
TOPK_GROUPS = 4
EXPERT_FF = 256
ROUTED_SCALE = 2.5
MOE_BLOCK = 512
PLE_DIM = 256
LN_EPS = 1e-5
DEPTH = 1
DEEPNORM_ALPHA = (2.0 * DEPTH) ** 0.25

LANES = 128
LANE_CHUNKS = D_MODEL // LANES
PROJ_ROWS = 512
MIX_ROWS = 512
DISPATCH_ROWS = 256
COMBINE_ROWS = 128
V7X_VMEM_LIMIT = 56 * 1024 * 1024


def _cparams(*sem):
    return pltpu.CompilerParams(dimension_semantics=sem, vmem_limit_bytes=V7X_VMEM_LIMIT)


def _proj_att_kernel(*refs, dil):
    x_refs, w_ref, o_ref = refs[:LANE_CHUNKS], refs[LANE_CHUNKS], refs[LANE_CHUNKS + 1]
    n = PROJ_ROWS // dil

    def rows(ref):
        if dil == 1:
            return ref[...]
        return jnp.concatenate([ref[pl.ds(r, n, stride=dil), :] for r in range(dil)], axis=0)

    xp = jnp.concatenate([rows(ref).astype(BF16) for ref in x_refs], axis=1)
    y = jnp.dot(xp, w_ref[...], preferred_element_type=F32)
    o_ref[...] = y.astype(BF16).reshape(dil, n, 3 * ATT_WIDTH)


def _proj_att(x2d, w, dil):
    T = x2d.shape[0]
    per = ATT_TILE // PROJ_ROWS
    n = PROJ_ROWS // dil
    out = pl.pallas_call(
        functools.partial(_proj_att_kernel, dil=dil),
        grid=(T // PROJ_ROWS,),
        in_specs=[pl.BlockSpec((PROJ_ROWS, LANES), functools.partial(lambda i, c: (i, c), c=c))
                  for c in range(LANE_CHUNKS)]
                 + [pl.BlockSpec((D_MODEL, 3 * ATT_WIDTH), lambda i: (0, 0))],
        out_specs=pl.BlockSpec((None, dil, None, n, 3 * ATT_WIDTH), lambda i: (i // per, 0, i % per, 0, 0)),
        out_shape=jax.ShapeDtypeStruct((T // ATT_TILE, dil, per, n, 3 * ATT_WIDTH), BF16),
        compiler_params=_cparams("parallel"),
        name=f"proj_att_d{dil}",
    )(*([x2d] * LANE_CHUNKS), w)
    return out.reshape(T // ATT_TILE, dil, ATT_TILE // dil, 3 * ATT_WIDTH)


def _att_pair(q2, kp, kc, vp, vc, bias_ref, g, first):
    def head0_lanes(rows, dtype):
        lane = lax.broadcasted_iota(jnp.int32, (rows, 2 * ATT_HEAD_DIM), 1)
        return lane.astype(F32).astype(dtype) < ATT_HEAD_DIM

    lo_q = head0_lanes(ATT_BLOCK, BF16)
    lo_v = head0_lanes(2 * ATT_BLOCK, BF16)
    k2 = jnp.concatenate([kp, kc], axis=0)
    v2 = jnp.concatenate([vp, vc], axis=0)
    zero = jnp.zeros_like(q2)
    ps, ms = [], []
    for hh in range(2):
        qm = jnp.where(lo_q, q2, zero) if hh == 0 else jnp.where(lo_q, zero, q2)
        s = lax.dot_general(qm, k2, (((1,), (1,)), ((), ())), preferred_element_type=F32)
        s = s + bias_ref[g, hh, first]
        m = jnp.max(s, axis=-1, keepdims=True)
        ps.append(jnp.exp(s - m).astype(BF16))
        ms.append(m)
    pcat = jnp.concatenate(ps, axis=1)
    zero_v, one_v = jnp.zeros_like(v2), jnp.ones_like(v2)
    rhs = jnp.concatenate([
        jnp.concatenate([jnp.where(lo_v, v2, zero_v), jnp.where(lo_v, one_v, zero_v)], axis=1),
        jnp.concatenate([jnp.where(lo_v, zero_v, v2), jnp.where(lo_v, zero_v, one_v)], axis=1)], axis=0)
    nd = jnp.dot(pcat, rhs, preferred_element_type=F32)
    m2 = jnp.where(head0_lanes(ATT_BLOCK, F32), ms[0], ms[1])
    return nd[:, :2 * ATT_HEAD_DIM], m2, nd[:, 2 * ATT_HEAD_DIM:]


def _att_kernel(*refs):
    (q0, kc0, vc0, kp0, vp0, q1, kc1, vc1, kp1, vp1, q2, kc2, vc2, kp2, vp2,
     bias_ref, o_ref) = refs[:17]
    ng = len(ATT_DILATIONS)
    num_s, m_s, den_s = refs[17:17 + ng], refs[17 + ng:17 + 2 * ng], refs[17 + 2 * ng:]
    first_tile = (pl.program_id(2) == 0).astype(jnp.int32)
    groups = ((q0, kc0, vc0, kp0, vp0), (q1, kc1, vc1, kp1, vp1), (q2, kc2, vc2, kp2, vp2))
    for g, dil in enumerate(ATT_DILATIONS):
        q_ref, kc_ref, vc_ref, kp_ref, vp_ref = groups[g]
        nb = ATT_TILE // dil // ATT_BLOCK
        for r in range(dil):
            for n in range(nb):
                rows = pl.ds(n * ATT_BLOCK, ATT_BLOCK)
                if n == 0:
                    prev = pl.ds((nb - 1) * ATT_BLOCK, ATT_BLOCK)
                    kp, vp, first = kp_ref[r, prev, :], vp_ref[r, prev, :], first_tile
                else:
                    prev = pl.ds((n - 1) * ATT_BLOCK, ATT_BLOCK)
                    kp, vp, first = kc_ref[r, prev, :], vc_ref[r, prev, :], 0
                num, m, den = _att_pair(q_ref[r, rows, :], kp, kc_ref[r, rows, :], vp, vc_ref[r, rows, :],
                                        bias_ref, g, first)
                if dil == 1:
                    dst = rows
                else:
                    dst = pl.ds(n * ATT_BLOCK * dil + r, ATT_BLOCK, stride=dil)
                num_s[g][dst, :] = num
                m_s[g][dst, :] = m
                den_s[g][dst, :] = den
    m_all = jnp.maximum(jnp.maximum(m_s[0][...], m_s[1][...]), m_s[2][...])
    num = jnp.zeros((ATT_TILE, 2 * ATT_HEAD_DIM), F32)
    den = jnp.zeros((ATT_TILE, 2 * ATT_HEAD_DIM), F32)
    for g in range(ng):
        sc = jnp.exp(m_s[g][...] - m_all)
        num = num + sc * num_s[g][...]
        den = den + sc * den_s[g][...]
    o_ref[...] = (num / den).astype(o_ref.dtype)


def _att_bias_table():
    qi = np.arange(ATT_BLOCK)[:, None]
    ki = np.arange(2 * ATT_BLOCK)[None, :]
    steps = qi + ATT_BLOCK - ki
    valid = (steps >= 0) & (steps <= ATT_BLOCK)
    slopes = np.array([2.0 ** (-8.0 * (h + 1) / ATT_HEADS) for h in range(ATT_HEADS)], np.float32)
    tab = np.empty((len(ATT_DILATIONS), ATT_HEADS, 2, ATT_BLOCK, 2 * ATT_BLOCK), np.float32)
    for g, dil in enumerate(ATT_DILATIONS):
        bias = -slopes[:, None, None] * (steps * dil).astype(np.float32)[None]
        tab[g, :, 0] = np.where(valid[None], bias, NEG_INF)
        tab[g, :, 1] = np.where((valid & (ki >= ATT_BLOCK))[None], bias, NEG_INF)
    return jnp.asarray(tab)


def _attention(qkv, B, S):
    tiles = S // ATT_TILE
    pair = 2 * ATT_HEAD_DIM
    npair = ATT_WIDTH // pair
    in_specs, args = [], []
    for g, dil in enumerate(ATT_DILATIONS):
        blk = (None, dil, ATT_TILE // dil, pair)
        cur = lambda b, hp, t, off: (b * tiles + t, 0, 0, off * npair + hp)
        prv = lambda b, hp, t, off: (b * tiles + jnp.maximum(t - 1, 0), 0, 0, off * npair + hp)
        in_specs += [pl.BlockSpec(blk, functools.partial(cur, off=0)),
                     pl.BlockSpec(blk, functools.partial(cur, off=1)),
                     pl.BlockSpec(blk, functools.partial(cur, off=2)),
                     pl.BlockSpec(blk, functools.partial(prv, off=1)),
                     pl.BlockSpec(blk, functools.partial(prv, off=2))]
        args += [qkv[g]] * 5
    in_specs.append(pl.BlockSpec((len(ATT_DILATIONS), 2, 2, ATT_BLOCK, 2 * ATT_BLOCK),
                                 lambda b, hp, t: (0, hp, 0, 0, 0)))
    args.append(_att_bias_table())
    scratch = [pltpu.VMEM((ATT_TILE, pair), F32) for _ in range(3 * len(ATT_DILATIONS))]
    return pl.pallas_call(
        _att_kernel,
        grid=(B, npair, tiles),
        in_specs=in_specs,
        out_specs=pl.BlockSpec((ATT_TILE, pair), lambda b, hp, t: (b * tiles + t, hp)),
        out_shape=jax.ShapeDtypeStruct((B * S, ATT_WIDTH), BF16),
        scratch_shapes=scratch,
        compiler_params=_cparams("parallel", "parallel", "arbitrary"),
        name="dilated_attention",
    )(*args)


def _att_weights(w_in_l):
    out = []
    width = len(ATT_DILATIONS) * ATT_WIDTH
    for g in range(len(ATT_DILATIONS)):
        cols = [w_in_l[:, part * width + g * ATT_WIDTH: part * width + (g + 1) * ATT_WIDTH] for part in range(3)]
        cols[0] = cols[0] * (ATT_HEAD_DIM ** -0.5)
        out.append(jnp.concatenate(cols, axis=1).astype(BF16))
    return out


def _proj_kernel(x_ref, w_ref, o_ref, *, col_tile):
    xb = x_ref[...].astype(BF16)
    for c in range(w_ref.shape[1] // col_tile):
        cols = slice(c * col_tile, (c + 1) * col_tile)
        o_ref[:, cols] = jnp.dot(xb, w_ref[:, cols], preferred_element_type=F32).astype(o_ref.dtype)


def _proj(x2d, w, col_tile):
    T, N = x2d.shape[0], w.shape[1]
    return pl.pallas_call(
        functools.partial(_proj_kernel, col_tile=col_tile),
        grid=(T // PROJ_ROWS,),
        in_specs=[pl.BlockSpec((PROJ_ROWS, D_MODEL), lambda i: (i, 0)),
                  pl.BlockSpec((D_MODEL, N), lambda i: (0, 0))],
        out_specs=pl.BlockSpec((PROJ_ROWS, N), lambda i: (i, 0)),
        out_shape=jax.ShapeDtypeStruct((T, N), BF16),
        compiler_params=_cparams("parallel"),
        name="proj_hgrn_gates",
    )(x2d, w)


def _split3(v):
    a = v.astype(BF16)
    r = v - a.astype(F32)
    b = r.astype(BF16)
    c = (r - b.astype(F32)).astype(BF16)
    return a, b, c


def _hgrn_kernel(q_ref, f_ref, i_ref, g_ref, lbl_ref, gain_ref, o_ref, state_ref):
    @pl.when(pl.program_id(1) == 0)
    def _():
        state_ref[...] = jnp.zeros_like(state_ref)

    lbl = lbl_ref[...]
    e = jnp.exp(lbl - jnp.max(lbl, axis=0, keepdims=True))
    lb = e[0:1] / jnp.sum(e, axis=0, keepdims=True)
    forget = lb + (1.0 - lb) * jax.nn.sigmoid(f_ref[...].astype(F32))
    log_f = jnp.log(forget)
    key = 1.0 - forget

    row = lax.broadcasted_iota(jnp.int32, (HG_TILE, HG_TILE), 0)
    col = lax.broadcasted_iota(jnp.int32, (HG_TILE, HG_TILE), 1)
    causal = (row >= col) & ((row // HG_CHUNK) == (col // HG_CHUNK))
    tri = jnp.where(causal, 1.0, 0.0).astype(BF16)
    b = sum(jnp.dot(tri, t, preferred_element_type=F32) for t in _split3(log_f))
    eb = jnp.exp(b)
    q_dec = (q_ref[...].astype(F32) * eb).astype(BF16)
    k_inv = key * jnp.exp(-b)
    xi = i_ref[...].astype(F32)
    val = (xi * jax.nn.sigmoid(xi)).astype(BF16)
    k_inv_b = k_inv.astype(BF16)

    n_chunks = HG_TILE // HG_CHUNK
    outs = []
    for h in range(HG_HEADS):
        cols = slice(h * HG_DIM, (h + 1) * HG_DIM)
        qd, ki, vv = q_dec[:, cols], k_inv_b[:, cols], val[:, cols]
        a = lax.dot_general(qd, ki, (((1,), (1,)), ((), ())), preferred_element_type=F32)
        a = jnp.where(causal, a, 0.0).astype(BF16)
        o_intra = jnp.dot(a, vv, preferred_element_type=F32)
        st = state_ref[h]
        pieces = []
        for c in range(n_chunks):
            rows = slice(c * HG_CHUNK, (c + 1) * HG_CHUNK)
            dec = eb[(c + 1) * HG_CHUNK - 1:(c + 1) * HG_CHUNK, cols]
            o_inter = lax.dot_general(qd[rows], st.astype(BF16), (((1,), (1,)), ((), ())),
                                      preferred_element_type=F32)
            pieces.append(o_intra[rows] + o_inter)
            k_end = (k_inv[rows, cols] * dec).astype(BF16)
            st = st * dec + lax.dot_general(vv[rows], k_end, (((0,), (0,)), ((), ())),
                                            preferred_element_type=F32)
        state_ref[h] = st
        o = jnp.concatenate(pieces, axis=0)
        o = o * lax.rsqrt(jnp.mean(jnp.square(o), axis=-1, keepdims=True) + RMS_EPS)
        outs.append(o)
    o = jnp.concatenate(outs, axis=1) * gain_ref[...]
    gg = g_ref[...].astype(F32)
    o_ref[...] = (o * (gg * jax.nn.sigmoid(gg))).astype(o_ref.dtype)


def _hgrn(u_hg, lb_logits, gain, B, S):
    tiles = S // HG_TILE
    col = lambda j: pl.BlockSpec((HG_TILE, HG_WIDTH), functools.partial(lambda b, t, j: (b * tiles + t, j), j=j))
    return pl.pallas_call(
        _hgrn_kernel,
        grid=(B, tiles),
        in_specs=[col(0), col(1), col(2), col(3),
                  pl.BlockSpec((2, HG_WIDTH), lambda b, t: (0, 0)),
                  pl.BlockSpec((1, HG_WIDTH), lambda b, t: (0, 0))],
        out_specs=pl.BlockSpec((HG_TILE, HG_WIDTH), lambda b, t: (b * tiles + t, 0)),
        out_shape=jax.ShapeDtypeStruct((B * S, HG_WIDTH), BF16),
        scratch_shapes=[pltpu.VMEM((HG_HEADS, HG_DIM, HG_DIM), F32)],
        compiler_params=_cparams("parallel", "arbitrary"),
        name="hgrn2",
    )(u_hg, u_hg, u_hg, u_hg, lb_logits, gain)


def _load_row_tiles(ref, n, start=0):
    return jnp.concatenate([ref[pl.ds(start + c, n, stride=LANE_CHUNKS), :] for c in range(LANE_CHUNKS)], axis=1)


def _store_row_tiles(ref, val, n):
    for c in range(LANE_CHUNKS):
        ref[pl.ds(c, n, stride=LANE_CHUNKS), :] = val[:, c * LANES:(c + 1) * LANES]


def _layer_norm(z, w, b):
    mu = jnp.mean(z, axis=-1, keepdims=True)
    zc = z - mu
    var = jnp.mean(jnp.square(zc), axis=-1, keepdims=True)
    return zc * lax.rsqrt(var + LN_EPS) * w + b


def _merge_kernel(ya_ref, yh_ref, ga_ref, gh_ref, x_ref, wa_ref, wh_ref, wo_ref, lw_ref, lb_ref, o_ref):
    ma = jnp.dot(ya_ref[...], wa_ref[...], preferred_element_type=F32)
    mh = jnp.dot(yh_ref[...], wh_ref[...], preferred_element_type=F32)
    merged = (jax.nn.sigmoid(ga_ref[...].astype(F32)) * ma + jax.nn.sigmoid(gh_ref[...].astype(F32)) * mh)
    z = DEEPNORM_ALPHA * x_ref[...] + jnp.dot(merged.astype(BF16), wo_ref[...], preferred_element_type=F32)
    _store_row_tiles(o_ref, _layer_norm(z, lw_ref[...], lb_ref[...]), MIX_ROWS)


def _merge(y_att, y_hg, u_hg, x2d, w_a, w_h, w_o, ln_w, ln_b):
    T = x2d.shape[0]
    rows = lambda width, j=0: pl.BlockSpec((MIX_ROWS, width), functools.partial(lambda i, j: (i, j), j=j))
    full = lambda a: pl.BlockSpec(a.shape, lambda i: (0, 0))
    return pl.pallas_call(
        _merge_kernel,
        grid=(T // MIX_ROWS,),
        in_specs=[rows(ATT_WIDTH), rows(HG_WIDTH), rows(D_MODEL, 4), rows(D_MODEL, 5), rows(D_MODEL),
                  full(w_a), full(w_h), full(w_o), full(ln_w), full(ln_b)],
        out_specs=pl.BlockSpec((MIX_ROWS * LANE_CHUNKS, LANES), lambda i: (i, 0)),
        out_shape=jax.ShapeDtypeStruct((T * LANE_CHUNKS, LANES), F32),
        compiler_params=_cparams("parallel"),
        name="merge_ln1",
    )(y_att, y_hg, u_hg, u_hg, x2d, w_a, w_h, w_o, ln_w, ln_b)


def _first_argmax(v, ids, n):
    mx = jnp.max(v, axis=0, keepdims=True)
    return mx, jnp.min(jnp.where(v == mx, ids, n), axis=0, keepdims=True)


def _route_kernel(x1_ref, p_ref, wrt_ref, rb_ref, wsg_ref, wsu_ref, wsd_ref, wpg_ref, wpp_ref,
                  base_ref, idx_ref, gate_ref, rank_ref, cnt_ref, carry_ref):
    @pl.when(pl.program_id(0) == 0)
    def _():
        carry_ref[...] = jnp.zeros_like(carry_ref)

    x1 = _load_row_tiles(x1_ref, MIX_ROWS)
    x1b = x1.astype(BF16)
    logits = lax.dot_general(wrt_ref[...], x1, (((1,), (1,)), ((), ())), preferred_element_type=F32,
                             precision=lax.Precision.HIGHEST)
    s = jax.nn.sigmoid(logits)
    sel = s + rb_ref[...]
    eid = lax.broadcasted_iota(jnp.int32, (N_EXPERTS, MIX_ROWS), 0)
    neg = -jnp.inf

    grp = sel.reshape(N_GROUPS, GROUP_SIZE, MIX_ROWS)
    mid = lax.broadcasted_iota(jnp.int32, grp.shape, 1)
    m1 = jnp.max(grp, axis=1, keepdims=True)
    i1 = jnp.min(jnp.where(grp == m1, mid, GROUP_SIZE), axis=1, keepdims=True)
    m2 = jnp.max(jnp.where(mid == i1, neg, grp), axis=1, keepdims=True)
    gscore = (m1 + m2).reshape(N_GROUPS, MIX_ROWS)
    gid = lax.broadcasted_iota(jnp.int32, (N_GROUPS, MIX_ROWS), 0)
    gsel = jnp.zeros((N_GROUPS, MIX_ROWS), jnp.bool_)
    for _ in range(TOPK_GROUPS):
        _, gi = _first_argmax(gscore, gid, N_GROUPS)
        hit = gid == gi
        gsel = gsel | hit
        gscore = jnp.where(hit, neg, gscore)
    emask = jnp.broadcast_to(gsel.reshape(N_GROUPS, 1, MIX_ROWS), grp.shape).reshape(N_EXPERTS, MIX_ROWS)
    cand = jnp.where(emask, sel, neg)

    idxs, gates = [], []
    chosen = jnp.zeros((N_EXPERTS, MIX_ROWS), jnp.bool_)
    for _ in range(TOP_K):
        _, ei = _first_argmax(cand, eid, N_EXPERTS)
        hit = eid == ei
        idxs.append(ei)
        gates.append(jnp.sum(jnp.where(hit, s, 0.0), axis=0, keepdims=True))
        chosen = chosen | hit
        cand = jnp.where(hit, neg, cand)
    g = jnp.concatenate(gates, axis=0)
    g = g / jnp.sum(g, axis=0, keepdims=True) * ROUTED_SCALE
    idx_ref[...] = jnp.concatenate(idxs, axis=0)
    gate_ref[...] = g

    onehot = jnp.where(chosen, 1.0, 0.0)
    tr = lax.broadcasted_iota(jnp.int32, (MIX_ROWS, MIX_ROWS), 0)
    tc = lax.broadcasted_iota(jnp.int32, (MIX_ROWS, MIX_ROWS), 1)
    before = jnp.where(tr < tc, 1.0, 0.0).astype(BF16)
    prefix = jnp.dot(onehot.astype(BF16), before, preferred_element_type=F32)
    rankfull = (carry_ref[:, 0:1] + prefix).astype(jnp.int32)
    rank_ref[...] = jnp.concatenate(
        [jnp.sum(jnp.where(eid == ei, rankfull, 0), axis=0, keepdims=True) for ei in idxs], axis=0)
    total = carry_ref[...] + jnp.sum(onehot, axis=1, keepdims=True)
    carry_ref[...] = total
    cnt_ref[...] = total.astype(jnp.int32)

    hg = jnp.dot(x1b, wsg_ref[...], preferred_element_type=F32)
    hu = jnp.dot(x1b, wsu_ref[...], preferred_element_type=F32)
    shared = jnp.dot((hg * jax.nn.sigmoid(hg) * hu).astype(BF16), wsd_ref[...], preferred_element_type=F32)
    ple = (jax.nn.sigmoid(jnp.dot(x1b, wpg_ref[...], preferred_element_type=F32))
           * jnp.dot(p_ref[...].astype(BF16), wpp_ref[...], preferred_element_type=F32))
    base_ref[...] = DEEPNORM_ALPHA * x1 + shared + ple


def _route(x1, p2d, wr_t, rbias, wsg, wsu, wsd, wpg, wpp):
    T = x1.shape[0] // LANE_CHUNKS
    full = lambda a: pl.BlockSpec(a.shape, lambda i: (0, 0))
    tok = pl.BlockSpec((TOP_K, MIX_ROWS), lambda i: (0, i))
    return pl.pallas_call(
        _route_kernel,
        grid=(T // MIX_ROWS,),
        in_specs=[pl.BlockSpec((MIX_ROWS * LANE_CHUNKS, LANES), lambda i: (i, 0)),
                  pl.BlockSpec((MIX_ROWS, PLE_DIM), lambda i: (i, 0)),
                  full(wr_t), full(rbias), full(wsg), full(wsu), full(wsd), full(wpg), full(wpp)],
        out_specs=[pl.BlockSpec((MIX_ROWS, D_MODEL), lambda i: (i, 0)), tok, tok, tok,
                   pl.BlockSpec((N_EXPERTS, LANES), lambda i: (0, 0))],
        out_shape=[jax.ShapeDtypeStruct((T, D_MODEL), F32),
                   jax.ShapeDtypeStruct((TOP_K, T), jnp.int32),
                   jax.ShapeDtypeStruct((TOP_K, T), F32),
                   jax.ShapeDtypeStruct((TOP_K, T), jnp.int32),
                   jax.ShapeDtypeStruct((N_EXPERTS, LANES), jnp.int32)],
        scratch_shapes=[pltpu.VMEM((N_EXPERTS, LANES), F32)],
        compiler_params=_cparams("arbitrary"),
        name="route_shared_ple",
    )(x1, p2d, wr_t, rbias, wsg, wsu, wsd, wpg, wpp)


def _dest_kernel(pstart_ref, idx_ref, rank_ref, dest_ref):
    idx = idx_ref[...]

    def body(e, acc):
        return acc + jnp.where(idx == e, pstart_ref[e], 0)

    dest_ref[...] = lax.fori_loop(0, N_EXPERTS, body, rank_ref[...])


def _dest(pstart, idx, rank):
    spec = pl.BlockSpec(idx.shape, lambda i, ps: (0, 0))
    return pl.pallas_call(
        _dest_kernel,
        grid_spec=pltpu.PrefetchScalarGridSpec(num_scalar_prefetch=1, grid=(1,), in_specs=[spec, spec],
                                               out_specs=spec),
        out_shape=jax.ShapeDtypeStruct(idx.shape, jnp.int32),
        compiler_params=_cparams("arbitrary"),
        name="moe_dest",
    )(pstart, idx, rank)


def _dispatch_kernel(pend_ref, dest_ref, x_ref, xs_ref, zero_ref, sem):
    tile = lambda ref, r, n=1: ref.at[pl.ds(pl.multiple_of(r * LANE_CHUNKS, LANE_CHUNKS), n * LANE_CHUNKS), :]
    row_copy = lambda t, d: pltpu.make_async_copy(tile(x_ref, t), tile(xs_ref, d), sem)

    @pl.when(pl.program_id(0) == 0)
    def _():
        zero_ref[...] = jnp.zeros_like(zero_ref)
        fill = lambda e: pltpu.make_async_copy(
            zero_ref, tile(xs_ref, jnp.maximum(pend_ref[e] - MOE_BLOCK, 0), MOE_BLOCK), sem)

        def start(e, c):
            fill(e).start()
            return c

        def wait(e, c):
            fill(e).wait()
            return c

        lax.fori_loop(0, N_EXPERTS, start, 0)
        lax.fori_loop(0, N_EXPERTS, wait, 0)

    def start_row(t, c):
        for k in range(TOP_K):
            row_copy(t, dest_ref[k, t]).start()
        return c

    def wait_row(t, c):
        for k in range(TOP_K):
            row_copy(t, 0).wait()
        return c

    lax.fori_loop(0, DISPATCH_ROWS, start_row, 0)
    lax.fori_loop(0, DISPATCH_ROWS, wait_row, 0)


def _dispatch(x1, dest, pend, n_rows):
    T = x1.shape[0] // LANE_CHUNKS
    return pl.pallas_call(
        _dispatch_kernel,
        grid_spec=pltpu.PrefetchScalarGridSpec(
            num_scalar_prefetch=1,
            grid=(T // DISPATCH_ROWS,),
            in_specs=[pl.BlockSpec((TOP_K, DISPATCH_ROWS), lambda i, pend: (0, i), memory_space=pltpu.SMEM),
                      pl.BlockSpec((DISPATCH_ROWS * LANE_CHUNKS, LANES), lambda i, pend: (i, 0))],
            out_specs=pl.BlockSpec(memory_space=pl.ANY),
            scratch_shapes=[pltpu.VMEM((MOE_BLOCK * LANE_CHUNKS, LANES), F32), pltpu.SemaphoreType.DMA(())],
        ),
        out_shape=jax.ShapeDtypeStruct((n_rows * LANE_CHUNKS, LANES), F32),
        compiler_params=_cparams("arbitrary"),
        name="moe_dispatch",
    )(pend, dest, x1)


def _expert_kernel(be_ref, nused_ref, x_ref, wg_ref, wu_ref, wd_ref, o_ref, wg_b, wu_b, wd_b):
    i = pl.program_id(0)

    @pl.when(i < nused_ref[0])
    def _():
        @pl.when((i == 0) | (be_ref[i] != be_ref[jnp.maximum(i - 1, 0)]))
        def _():
            wg_b[...] = wg_ref[...].astype(BF16)
            wu_b[...] = wu_ref[...].astype(BF16)
            wd_b[...] = wd_ref[...].astype(BF16)

        xb = _load_row_tiles(x_ref, MOE_BLOCK).astype(BF16)
        hg = jnp.dot(xb, wg_b[...], preferred_element_type=F32)
        hu = jnp.dot(xb, wu_b[...], preferred_element_type=F32)
        act = (hg * jax.nn.sigmoid(hg) * hu).astype(BF16)
        _store_row_tiles(o_ref, jnp.dot(act, wd_b[...], preferred_element_type=F32), MOE_BLOCK)


def _experts(xs, block_expert, n_used, wg, wu, wd):
    n_blocks = xs.shape[0] // (MOE_BLOCK * LANE_CHUNKS)
    live = lambda i, be, nu: jnp.minimum(i, nu[0] - 1)
    return pl.pallas_call(
        _expert_kernel,
        grid_spec=pltpu.PrefetchScalarGridSpec(
            num_scalar_prefetch=2,
            grid=(n_blocks,),
            in_specs=[pl.BlockSpec((MOE_BLOCK * LANE_CHUNKS, LANES), lambda i, be, nu: (live(i, be, nu), 0)),
                      pl.BlockSpec((None, D_MODEL, EXPERT_FF), lambda i, be, nu: (be[live(i, be, nu)], 0, 0)),
                      pl.BlockSpec((None, D_MODEL, EXPERT_FF), lambda i, be, nu: (be[live(i, be, nu)], 0, 0)),
                      pl.BlockSpec((None, EXPERT_FF, D_MODEL), lambda i, be, nu: (be[live(i, be, nu)], 0, 0))],
            out_specs=pl.BlockSpec((MOE_BLOCK * LANE_CHUNKS, LANES), lambda i, be, nu: (live(i, be, nu), 0)),
            scratch_shapes=[pltpu.VMEM((D_MODEL, EXPERT_FF), BF16), pltpu.VMEM((D_MODEL, EXPERT_FF), BF16),
                            pltpu.VMEM((EXPERT_FF, D_MODEL), BF16)],
        ),
        out_shape=jax.ShapeDtypeStruct(xs.shape, F32),
        compiler_params=_cparams("arbitrary"),
        name="moe_experts",
    )(block_expert, n_used, xs, wg, wu, wd)


def _combine_kernel(dest_ref, gate_ref, base_ref, lw_ref, lb_ref, ys_ref, o_ref, buf_ref, sem):
    tile = lambda ref, r: ref.at[pl.ds(pl.multiple_of(r * LANE_CHUNKS, LANE_CHUNKS), LANE_CHUNKS), :]
    row_copy = lambda k, t, d: pltpu.make_async_copy(tile(ys_ref, d), tile(buf_ref, k * COMBINE_ROWS + t), sem)

    def start_row(t, c):
        for k in range(TOP_K):
            row_copy(k, t, dest_ref[k, t]).start()
        return c

    def wait_row(t, c):
        for k in range(TOP_K):
            row_copy(k, t, 0).wait()
        return c

    lax.fori_loop(0, COMBINE_ROWS, start_row, 0)
    lax.fori_loop(0, COMBINE_ROWS, wait_row, 0)
    gate = gate_ref[...]
    z = base_ref[...]
    for k in range(TOP_K):
        z = z + gate[:, k:k + 1] * _load_row_tiles(buf_ref, COMBINE_ROWS, k * COMBINE_ROWS * LANE_CHUNKS)
    o_ref[...] = _layer_norm(z, lw_ref[...], lb_ref[...])


def _combine(ys, dest, gate_t, base, ln_w, ln_b):
    T = base.shape[0]
    return pl.pallas_call(
        _combine_kernel,
        grid=(T // COMBINE_ROWS,),
        in_specs=[pl.BlockSpec((TOP_K, COMBINE_ROWS), lambda i: (0, i), memory_space=pltpu.SMEM),
                  pl.BlockSpec((COMBINE_ROWS, TOP_K), lambda i: (i, 0)),
                  pl.BlockSpec((COMBINE_ROWS, D_MODEL), lambda i: (i, 0)),
                  pl.BlockSpec((1, D_MODEL), lambda i: (0, 0)),
                  pl.BlockSpec((1, D_MODEL), lambda i: (0, 0)),
                  pl.BlockSpec(memory_space=pl.ANY)],
        out_specs=pl.BlockSpec((COMBINE_ROWS, D_MODEL), lambda i: (i, 0)),
        out_shape=jax.ShapeDtypeStruct((T, D_MODEL), F32),
        scratch_shapes=[pltpu.VMEM((TOP_K * COMBINE_ROWS * LANE_CHUNKS, LANES), F32), pltpu.SemaphoreType.DMA(())],
        compiler_params=_cparams("arbitrary"),
        name="moe_combine_ln2",
    )(dest, gate_t, base, ln_w, ln_b, ys)


def kernel(x, p, w_in, hgrn_lb_logits, hgrn_norm_w, w_branch_att, w_branch_hgrn, w_out, ln1_w, ln1_b, router_w, router_bias, expert_w_gate, expert_w_up, expert_w_down, shared_w_gate, shared_w_up, shared_w_down, ple_gate_w, ple_proj_w, ln2_w, ln2_b):
    B, S, D = x.shape
    T = B * S
    l = 0
    x2d = x.reshape(T, D)
    bf = lambda a: a.astype(BF16)

    ws = _att_weights(w_in[l])
    qkv = [_proj_att(x2d, ws[g], d) for g, d in enumerate(ATT_DILATIONS)]
    y_att = _attention(qkv, B, S)
    u_hg = _proj(x2d, bf(w_in[l][:, 3 * len(ATT_DILATIONS) * ATT_WIDTH:]), 1536)
    y_hg = _hgrn(u_hg, hgrn_lb_logits, hgrn_norm_w[l:l + 1], B, S)
    x1 = _merge(y_att, y_hg, u_hg, x2d, bf(w_branch_att[l]), bf(w_branch_hgrn[l]), bf(w_out[l]),
                ln1_w[l:l + 1], ln1_b[l:l + 1])

    base, idx, gate, rank, counts = _route(
        x1, p[l].reshape(T, PLE_DIM), router_w[l].T, router_bias[l].reshape(N_EXPERTS, 1),
        bf(shared_w_gate[l]), bf(shared_w_up[l]), bf(shared_w_down[l]), bf(ple_gate_w[l]), bf(ple_proj_w[l]))
    counts = counts[:, 0]
    padded = (counts + MOE_BLOCK - 1) // MOE_BLOCK * MOE_BLOCK
    pend = jnp.cumsum(padded)
    n_blocks = -(-(T * TOP_K + N_EXPERTS * (MOE_BLOCK - 1)) // MOE_BLOCK)
    dest = _dest((pend - padded).astype(jnp.int32), idx, rank)
    n_used = (pend[-1:] // MOE_BLOCK).astype(jnp.int32)
    block_start = jnp.arange(n_blocks, dtype=jnp.int32) * MOE_BLOCK
    block_expert = jnp.minimum(jnp.sum(pend[None, :] <= block_start[:, None], axis=1), N_EXPERTS - 1)
    block_expert = block_expert.astype(jnp.int32)
    xs = _dispatch(x1, dest, pend.astype(jnp.int32), n_blocks * MOE_BLOCK)
    ys = _experts(xs, block_expert, n_used, expert_w_gate[l], expert_w_up[l], expert_w_down[l])
    out = _combine(ys, dest, gate.T, base, ln2_w[l:l + 1], ln2_b[l:l + 1])
    return out.reshape(B, S, D)
```

```python
import functools

import jax
import jax.numpy as jnp
import numpy as np
from jax import lax
from jax.experimental import pallas as pl
from jax.experimental.pallas import tpu as pltpu

F32 = jnp.float32
BF16 = jnp.bfloat16

D_MODEL = 1024
ATT_HEAD_DIM = 64
ATT_HEADS = 8
ATT_DILATIONS = (1, 4, 16)
ATT_BLOCK = 128
ATT_WIDTH = ATT_HEADS * ATT_HEAD_DIM
ATT_TILE = ATT_BLOCK * max(ATT_DILATIONS)
NEG_INF = -1e30

HG_HEADS = 8
HG_DIM = 128
HG_WIDTH = HG_HEADS * HG_DIM
HG_CHUNK = 32
HG_TILE = 256
RMS_EPS = 1e-6

N_EXPERTS = 64
TOP_K = 8
TOP_K_BITS = 3
N_GROUPS = 8
GROUP_SIZE = N_EXPERTS // N_GROUPS
TOPK_GROUPS = 4
EXPERT_FF = 256
ROUTED_SCALE = 2.5
PLE_DIM = 256
LN_EPS = 1e-5
DEPTH = 1
DEEPNORM_ALPHA = (2.0 * DEPTH) ** 0.25

LANES = 128
LANE_CHUNKS = D_MODEL // LANES
PROJ_ROWS = 512
MIX_ROWS = 512
MOE_TILE = 4096
MOE_CHUNK = 576
PLAN_ROWS = 512
LIST_PAD = 1024
LIST_LEN = MOE_TILE * TOP_K + LIST_PAD
GATHER_GROUP = 8
SCATTER_GROUP = 8
V7X_VMEM_LIMIT = 56 * 1024 * 1024


def _cparams(*sem):
    return pltpu.CompilerParams(dimension_semantics=sem, vmem_limit_bytes=V7X_VMEM_LIMIT)


def _proj_att_kernel(*refs, dil):
    x_refs, w_ref, o_ref = refs[:LANE_CHUNKS], refs[LANE_CHUNKS], refs[LANE_CHUNKS + 1]
    n = PROJ_ROWS // dil

    def rows(ref):
        if dil == 1:
            return ref[...]
        return jnp.concatenate([ref[pl.ds(r, n, stride=dil), :] for r in range(dil)], axis=0)

    xp = jnp.concatenate([rows(ref).astype(BF16) for ref in x_refs], axis=1)
    y = jnp.dot(xp, w_ref[...], preferred_element_type=F32)
    o_ref[...] = y.astype(BF16).reshape(dil, n, 3 * ATT_WIDTH)


def _proj_att(x2d, w, dil):
    T = x2d.shape[0]
    per = ATT_TILE // PROJ_ROWS
    n = PROJ_ROWS // dil
    out = pl.pallas_call(
        functools.partial(_proj_att_kernel, dil=dil),
        grid=(T // PROJ_ROWS,),
        in_specs=[pl.BlockSpec((PROJ_ROWS, LANES), functools.partial(lambda i, c: (i, c), c=c))
                  for c in range(LANE_CHUNKS)]
                 + [pl.BlockSpec((D_MODEL, 3 * ATT_WIDTH), lambda i: (0, 0))],
        out_specs=pl.BlockSpec((None, dil, None, n, 3 * ATT_WIDTH), lambda i: (i // per, 0, i % per, 0, 0)),
        out_shape=jax.ShapeDtypeStruct((T // ATT_TILE, dil, per, n, 3 * ATT_WIDTH), BF16),
        compiler_params=_cparams("parallel"),
        name=f"proj_att_d{dil}",
    )(*([x2d] * LANE_CHUNKS), w)
    return out.reshape(T // ATT_TILE, dil, ATT_TILE // dil, 3 * ATT_WIDTH)


def _att_pair(q2, kp, kc, vp, vc, bias_ref, g, first):
    def head0_lanes(rows, dtype):
        lane = lax.broadcasted_iota(jnp.int32, (rows, 2 * ATT_HEAD_DIM), 1)
        return lane.astype(F32).astype(dtype) < ATT_HEAD_DIM

    lo_q = head0_lanes(ATT_BLOCK, BF16)
    lo_v = head0_lanes(2 * ATT_BLOCK, BF16)
    k2 = jnp.concatenate([kp, kc], axis=0)
    v2 = jnp.concatenate([vp, vc], axis=0)
    zero = jnp.zeros_like(q2)
    ps, ms = [], []
    for hh in range(2):
        qm = jnp.where(lo_q, q2, zero) if hh == 0 else jnp.where(lo_q, zero, q2)
        s = lax.dot_general(qm, k2, (((1,), (1,)), ((), ())), preferred_element_type=F32)
        s = s + bias_ref[g, hh, first]
        m = jnp.max(s, axis=-1, keepdims=True)
        ps.append(jnp.exp(s - m).astype(BF16))
        ms.append(m)
    pcat = jnp.concatenate(ps, axis=1)
    zero_v, one_v = jnp.zeros_like(v2), jnp.ones_like(v2)
    rhs = jnp.concatenate([
        jnp.concatenate([jnp.where(lo_v, v2, zero_v), jnp.where(lo_v, one_v, zero_v)], axis=1),
        jnp.concatenate([jnp.where(lo_v, zero_v, v2), jnp.where(lo_v, zero_v, one_v)], axis=1)], axis=0)
    nd = jnp.dot(pcat, rhs, preferred_element_type=F32)
    m2 = jnp.where(head0_lanes(ATT_BLOCK, F32), ms[0], ms[1])
    return nd[:, :2 * ATT_HEAD_DIM], m2, nd[:, 2 * ATT_HEAD_DIM:]


def _att_kernel(*refs):
    (q0, kc0, vc0, kp0, vp0, q1, kc1, vc1, kp1, vp1, q2, kc2, vc2, kp2, vp2,
     bias_ref, o_ref) = refs[:17]
    ng = len(ATT_DILATIONS)
    num_s, m_s, den_s = refs[17:17 + ng], refs[17 + ng:17 + 2 * ng], refs[17 + 2 * ng:]
    first_tile = (pl.program_id(2) == 0).astype(jnp.int32)
    groups = ((q0, kc0, vc0, kp0, vp0), (q1, kc1, vc1, kp1, vp1), (q2, kc2, vc2, kp2, vp2))
    for g, dil in enumerate(ATT_DILATIONS):
        q_ref, kc_ref, vc_ref, kp_ref, vp_ref = groups[g]
        nb = ATT_TILE // dil // ATT_BLOCK
        for r in range(dil):
            for n in range(nb):
                rows = pl.ds(n * ATT_BLOCK, ATT_BLOCK)
                if n == 0:
                    prev = pl.ds((nb - 1) * ATT_BLOCK, ATT_BLOCK)
                    kp, vp, first = kp_ref[r, prev, :], vp_ref[r, prev, :], first_tile
                else:
                    prev = pl.ds((n - 1) * ATT_BLOCK, ATT_BLOCK)
                    kp, vp, first = kc_ref[r, prev, :], vc_ref[r, prev, :], 0
                num, m, den = _att_pair(q_ref[r, rows, :], kp, kc_ref[r, rows, :], vp, vc_ref[r, rows, :],
                                        bias_ref, g, first)
                if dil == 1:
                    dst = rows
                else:
                    dst = pl.ds(n * ATT_BLOCK * dil + r, ATT_BLOCK, stride=dil)
                num_s[g][dst, :] = num
                m_s[g][dst, :] = m
                den_s[g][dst, :] = den
    m_all = jnp.maximum(jnp.maximum(m_s[0][...], m_s[1][...]), m_s[2][...])
    num = jnp.zeros((ATT_TILE, 2 * ATT_HEAD_DIM), F32)
    den = jnp.zeros((ATT_TILE, 2 * ATT_HEAD_DIM), F32)
    for g in range(ng):
        sc = jnp.exp(m_s[g][...] - m_all)
        num = num + sc * num_s[g][...]
        den = den + sc * den_s[g][...]
    o_ref[...] = (num / den).astype(o_ref.dtype)


def _att_bias_table():
    qi = np.arange(ATT_BLOCK)[:, None]
    ki = np.arange(2 * ATT_BLOCK)[None, :]
    steps = qi + ATT_BLOCK - ki
    valid = (steps >= 0) & (steps <= ATT_BLOCK)
    slopes = np.array([2.0 ** (-8.0 * (h + 1) / ATT_HEADS) for h in range(ATT_HEADS)], np.float32)
    tab = np.empty((len(ATT_DILATIONS), ATT_HEADS, 2, ATT_BLOCK, 2 * ATT_BLOCK), np.float32)
    for g, dil in enumerate(ATT_DILATIONS):
        bias = -slopes[:, None, None] * (steps * dil).astype(np.float32)[None]
        tab[g, :, 0] = np.where(valid[None], bias, NEG_INF)
        tab[g, :, 1] = np.where((valid & (ki >= ATT_BLOCK))[None], bias, NEG_INF)
    return jnp.asarray(tab)


def _attention(qkv, B, S):
    tiles = S // ATT_TILE
    pair = 2 * ATT_HEAD_DIM
    npair = ATT_WIDTH // pair
    in_specs, args = [], []
    for g, dil in enumerate(ATT_DILATIONS):
        blk = (None, dil, ATT_TILE // dil, pair)
        cur = lambda b, hp, t, off: (b * tiles + t, 0, 0, off * npair + hp)
        prv = lambda b, hp, t, off: (b * tiles + jnp.maximum(t - 1, 0), 0, 0, off * npair + hp)
        in_specs += [pl.BlockSpec(blk, functools.partial(cur, off=0)),
                     pl.BlockSpec(blk, functools.partial(cur, off=1)),
                     pl.BlockSpec(blk, functools.partial(cur, off=2)),
                     pl.BlockSpec(blk, functools.partial(prv, off=1)),
                     pl.BlockSpec(blk, functools.partial(prv, off=2))]
        args += [qkv[g]] * 5
    in_specs.append(pl.BlockSpec((len(ATT_DILATIONS), 2, 2, ATT_BLOCK, 2 * ATT_BLOCK),
                                 lambda b, hp, t: (0, hp, 0, 0, 0)))
    args.append(_att_bias_table())
    scratch = [pltpu.VMEM((ATT_TILE, pair), F32) for _ in range(3 * len(ATT_DILATIONS))]
    return pl.pallas_call(
        _att_kernel,
        grid=(B, npair, tiles),
        in_specs=in_specs,
        out_specs=pl.BlockSpec((ATT_TILE, pair), lambda b, hp, t: (b * tiles + t, hp)),
        out_shape=jax.ShapeDtypeStruct((B * S, ATT_WIDTH), BF16),
        scratch_shapes=scratch,
        compiler_params=_cparams("parallel", "parallel", "arbitrary"),
        name="dilated_attention",
    )(*args)


def _att_weights(w_in_l):
    out = []
    width = len(ATT_DILATIONS) * ATT_WIDTH
    for g in range(len(ATT_DILATIONS)):
        cols = [w_in_l[:, part * width + g * ATT_WIDTH: part * width + (g + 1) * ATT_WIDTH] for part in range(3)]
        cols[0] = cols[0] * (ATT_HEAD_DIM ** -0.5)
        out.append(jnp.concatenate(cols, axis=1).astype(BF16))
    return out


def _proj_kernel(x_ref, w_ref, o_ref, *, col_tile):
    xb = x_ref[...].astype(BF16)
    for c in range(w_ref.shape[1] // col_tile):
        cols = slice(c * col_tile, (c + 1) * col_tile)
        o_ref[:, cols] = jnp.dot(xb, w_ref[:, cols], preferred_element_type=F32).astype(o_ref.dtype)


def _proj(x2d, w, col_tile):
    T, N = x2d.shape[0], w.shape[1]
    return pl.pallas_call(
        functools.partial(_proj_kernel, col_tile=col_tile),
        grid=(T // PROJ_ROWS,),
        in_specs=[pl.BlockSpec((PROJ_ROWS, D_MODEL), lambda i: (i, 0)),
                  pl.BlockSpec((D_MODEL, N), lambda i: (0, 0))],
        out_specs=pl.BlockSpec((PROJ_ROWS, N), lambda i: (i, 0)),
        out_shape=jax.ShapeDtypeStruct((T, N), BF16),
        compiler_params=_cparams("parallel"),
        name="proj_hgrn_gates",
    )(x2d, w)


def _split3(v):
    a = v.astype(BF16)
    r = v - a.astype(F32)
    b = r.astype(BF16)
    c = (r - b.astype(F32)).astype(BF16)
    return a, b, c


def _hgrn_kernel(q_ref, f_ref, i_ref, g_ref, lbl_ref, gain_ref, o_ref, state_ref):
    @pl.when(pl.program_id(1) == 0)
    def _():
        state_ref[...] = jnp.zeros_like(state_ref)

    lbl = lbl_ref[...]
    e = jnp.exp(lbl - jnp.max(lbl, axis=0, keepdims=True))
    lb = e[0:1] / jnp.sum(e, axis=0, keepdims=True)
    forget = lb + (1.0 - lb) * jax.nn.sigmoid(f_ref[...].astype(F32))
    log_f = jnp.log(forget)
    key = 1.0 - forget

    row = lax.broadcasted_iota(jnp.int32, (HG_TILE, HG_TILE), 0)
    col = lax.broadcasted_iota(jnp.int32, (HG_TILE, HG_TILE), 1)
    causal = (row >= col) & ((row // HG_CHUNK) == (col // HG_CHUNK))
    tri = jnp.where(causal, 1.0, 0.0).astype(BF16)
    b = sum(jnp.dot(tri, t, preferred_element_type=F32) for t in _split3(log_f))
    eb = jnp.exp(b)
    q_dec = (q_ref[...].astype(F32) * eb).astype(BF16)
    k_inv = key * jnp.exp(-b)
    xi = i_ref[...].astype(F32)
    val = (xi * jax.nn.sigmoid(xi)).astype(BF16)
    k_inv_b = k_inv.astype(BF16)

    n_chunks = HG_TILE // HG_CHUNK
    outs = []
    for h in range(HG_HEADS):
        cols = slice(h * HG_DIM, (h + 1) * HG_DIM)
        qd, ki, vv = q_dec[:, cols], k_inv_b[:, cols], val[:, cols]
        a = lax.dot_general(qd, ki, (((1,), (1,)), ((), ())), preferred_element_type=F32)
        a = jnp.where(causal, a, 0.0).astype(BF16)
        o_intra = jnp.dot(a, vv, preferred_element_type=F32)
        st = state_ref[h]
        pieces = []
        for c in range(n_chunks):
            rows = slice(c * HG_CHUNK, (c + 1) * HG_CHUNK)
            dec = eb[(c + 1) * HG_CHUNK - 1:(c + 1) * HG_CHUNK, cols]
            o_inter = lax.dot_general(qd[rows], st.astype(BF16), (((1,), (1,)), ((), ())),
                                      preferred_element_type=F32)
            pieces.append(o_intra[rows] + o_inter)
            k_end = (k_inv[rows, cols] * dec).astype(BF16)
            st = st * dec + lax.dot_general(vv[rows], k_end, (((0,), (0,)), ((), ())),
                                            preferred_element_type=F32)
        state_ref[h] = st
        o = jnp.concatenate(pieces, axis=0)
        o = o * lax.rsqrt(jnp.mean(jnp.square(o), axis=-1, keepdims=True) + RMS_EPS)
        outs.append(o)
    o = jnp.concatenate(outs, axis=1) * gain_ref[...]
    gg = g_ref[...].astype(F32)
    o_ref[...] = (o * (gg * jax.nn.sigmoid(gg))).astype(o_ref.dtype)


def _hgrn(u_hg, lb_logits, gain, B, S):
    tiles = S // HG_TILE
    col = lambda j: pl.BlockSpec((HG_TILE, HG_WIDTH), functools.partial(lambda b, t, j: (b * tiles + t, j), j=j))
    return pl.pallas_call(
        _hgrn_kernel,
        grid=(B, tiles),
        in_specs=[col(0), col(1), col(2), col(3),
                  pl.BlockSpec((2, HG_WIDTH), lambda b, t: (0, 0)),
                  pl.BlockSpec((1, HG_WIDTH), lambda b, t: (0, 0))],
        out_specs=pl.BlockSpec((HG_TILE, HG_WIDTH), lambda b, t: (b * tiles + t, 0)),
        out_shape=jax.ShapeDtypeStruct((B * S, HG_WIDTH), BF16),
        scratch_shapes=[pltpu.VMEM((HG_HEADS, HG_DIM, HG_DIM), F32)],
        compiler_params=_cparams("parallel", "arbitrary"),
        name="hgrn2",
    )(u_hg, u_hg, u_hg, u_hg, lb_logits, gain)


def _load_row_tiles(ref, n, start=0):
    return jnp.concatenate([ref[pl.ds(start + c, n, stride=LANE_CHUNKS), :] for c in range(LANE_CHUNKS)], axis=1)


def _store_row_tiles(ref, val, n):
    for c in range(LANE_CHUNKS):
        ref[pl.ds(c, n, stride=LANE_CHUNKS), :] = val[:, c * LANES:(c + 1) * LANES]


def _layer_norm(z, w, b):
    mu = jnp.mean(z, axis=-1, keepdims=True)
    zc = z - mu
    var = jnp.mean(jnp.square(zc), axis=-1, keepdims=True)
    return zc * lax.rsqrt(var + LN_EPS) * w + b


def _merge_kernel(ya_ref, yh_ref, ga_ref, gh_ref, x_ref, wa_ref, wh_ref, wo_ref, lw_ref, lb_ref, o_ref):
    ma = jnp.dot(ya_ref[...], wa_ref[...], preferred_element_type=F32)
    mh = jnp.dot(yh_ref[...], wh_ref[...], preferred_element_type=F32)
    merged = (jax.nn.sigmoid(ga_ref[...].astype(F32)) * ma + jax.nn.sigmoid(gh_ref[...].astype(F32)) * mh)
    z = DEEPNORM_ALPHA * x_ref[...] + jnp.dot(merged.astype(BF16), wo_ref[...], preferred_element_type=F32)
    _store_row_tiles(o_ref, _layer_norm(z, lw_ref[...], lb_ref[...]), MIX_ROWS)


def _merge(y_att, y_hg, u_hg, x2d, w_a, w_h, w_o, ln_w, ln_b):
    T = x2d.shape[0]
    rows = lambda width, j=0: pl.BlockSpec((MIX_ROWS, width), functools.partial(lambda i, j: (i, j), j=j))
    full = lambda a: pl.BlockSpec(a.shape, lambda i: (0, 0))
    return pl.pallas_call(
        _merge_kernel,
        grid=(T // MIX_ROWS,),
        in_specs=[rows(ATT_WIDTH), rows(HG_WIDTH), rows(D_MODEL, 4), rows(D_MODEL, 5), rows(D_MODEL),
                  full(w_a), full(w_h), full(w_o), full(ln_w), full(ln_b)],
        out_specs=pl.BlockSpec((MIX_ROWS * LANE_CHUNKS, LANES), lambda i: (i, 0)),
        out_shape=jax.ShapeDtypeStruct((T * LANE_CHUNKS, LANES), F32),
        compiler_params=_cparams("parallel"),
        name="merge_ln1",
    )(y_att, y_hg, u_hg, u_hg, x2d, w_a, w_h, w_o, ln_w, ln_b)


def _first_argmax(v, ids, n):
    mx = jnp.max(v, axis=0, keepdims=True)
    return mx, jnp.min(jnp.where(v == mx, ids, n), axis=0, keepdims=True)


def _route_kernel(x1_ref, p_ref, wrt_ref, rb_ref, wsg_ref, wsu_ref, wsd_ref, wpg_ref, wpp_ref,
                  base_ref, idx_ref, gate_ref, rank_ref, cnt_ref, carry_ref):
    @pl.when(pl.program_id(0) % (MOE_TILE // MIX_ROWS) == 0)
    def _():
        carry_ref[...] = jnp.zeros_like(carry_ref)

    x1 = _load_row_tiles(x1_ref, MIX_ROWS)
    x1b = x1.astype(BF16)
    logits = lax.dot_general(wrt_ref[...], x1, (((1,), (1,)), ((), ())), preferred_element_type=F32,
                             precision=lax.Precision.HIGHEST)
    s = jax.nn.sigmoid(logits)
    sel = s + rb_ref[...]
    eid = lax.broadcasted_iota(jnp.int32, (N_EXPERTS, MIX_ROWS), 0)
    neg = -jnp.inf

    grp = sel.reshape(N_GROUPS, GROUP_SIZE, MIX_ROWS)
    mid = lax.broadcasted_iota(jnp.int32, grp.shape, 1)
    m1 = jnp.max(grp, axis=1, keepdims=True)
    i1 = jnp.min(jnp.where(grp == m1, mid, GROUP_SIZE), axis=1, keepdims=True)
    m2 = jnp.max(jnp.where(mid == i1, neg, grp), axis=1, keepdims=True)
    gscore = (m1 + m2).reshape(N_GROUPS, MIX_ROWS)
    gid = lax.broadcasted_iota(jnp.int32, (N_GROUPS, MIX_ROWS), 0)
    gsel = jnp.zeros((N_GROUPS, MIX_ROWS), jnp.bool_)
    for _ in range(TOPK_GROUPS):
        _, gi = _first_argmax(gscore, gid, N_GROUPS)
        hit = gid == gi
        gsel = gsel | hit
        gscore = jnp.where(hit, neg, gscore)
    emask = jnp.broadcast_to(gsel.reshape(N_GROUPS, 1, MIX_ROWS), grp.shape).reshape(N_EXPERTS, MIX_ROWS)
    cand = jnp.where(emask, sel, neg)

    idxs, gates = [], []
    chosen = jnp.zeros((N_EXPERTS, MIX_ROWS), jnp.bool_)
    for _ in range(TOP_K):
        _, ei = _first_argmax(cand, eid, N_EXPERTS)
        hit = eid == ei
        idxs.append(ei)
        gates.append(jnp.sum(jnp.where(hit, s, 0.0), axis=0, keepdims=True))
        chosen = chosen | hit
        cand = jnp.where(hit, neg, cand)
    g = jnp.concatenate(gates, axis=0)
    g = g / jnp.sum(g, axis=0, keepdims=True) * ROUTED_SCALE
    idx_ref[...] = jnp.concatenate(idxs, axis=0)
    gate_ref[...] = g

    onehot = jnp.where(chosen, 1.0, 0.0)
    tr = lax.broadcasted_iota(jnp.int32, (MIX_ROWS, MIX_ROWS), 0)
    tc = lax.broadcasted_iota(jnp.int32, (MIX_ROWS, MIX_ROWS), 1)
    before = jnp.where(tr < tc, 1.0, 0.0).astype(BF16)
    prefix = jnp.dot(onehot.astype(BF16), before, preferred_element_type=F32)
    rankfull = (carry_ref[:, 0:1] + prefix).astype(jnp.int32)
    rank_ref[...] = jnp.concatenate(
        [jnp.sum(jnp.where(eid == ei, rankfull, 0), axis=0, keepdims=True) for ei in idxs], axis=0)
    total = carry_ref[...] + jnp.sum(onehot, axis=1, keepdims=True)
    carry_ref[...] = total
    cnt_ref[...] = total.astype(jnp.int32)

    hg = jnp.dot(x1b, wsg_ref[...], preferred_element_type=F32)
    hu = jnp.dot(x1b, wsu_ref[...], preferred_element_type=F32)
    shared = jnp.dot((hg * jax.nn.sigmoid(hg) * hu).astype(BF16), wsd_ref[...], preferred_element_type=F32)
    ple = (jax.nn.sigmoid(jnp.dot(x1b, wpg_ref[...], preferred_element_type=F32))
           * jnp.dot(p_ref[...].astype(BF16), wpp_ref[...], preferred_element_type=F32))
    _store_row_tiles(base_ref, DEEPNORM_ALPHA * x1 + shared + ple, MIX_ROWS)


def _route(x1, p2d, wr_t, rbias, wsg, wsu, wsd, wpg, wpp):
    T = x1.shape[0] // LANE_CHUNKS
    per_tile = MOE_TILE // MIX_ROWS
    full = lambda a: pl.BlockSpec(a.shape, lambda i: (0, 0))
    tok = pl.BlockSpec((TOP_K, MIX_ROWS), lambda i: (0, i))
    row_tiles = pl.BlockSpec((MIX_ROWS * LANE_CHUNKS, LANES), lambda i: (i, 0))
    return pl.pallas_call(
        _route_kernel,
        grid=(T // MIX_ROWS,),
        in_specs=[row_tiles,
                  pl.BlockSpec((MIX_ROWS, PLE_DIM), lambda i: (i, 0)),
                  full(wr_t), full(rbias), full(wsg), full(wsu), full(wsd), full(wpg), full(wpp)],
        out_specs=[row_tiles, tok, tok, tok,
                   pl.BlockSpec((None, N_EXPERTS, LANES), lambda i: (i // per_tile, 0, 0))],
        out_shape=[jax.ShapeDtypeStruct((T * LANE_CHUNKS, LANES), F32),
                   jax.ShapeDtypeStruct((TOP_K, T), jnp.int32),
                   jax.ShapeDtypeStruct((TOP_K, T), F32),
                   jax.ShapeDtypeStruct((TOP_K, T), jnp.int32),
                   jax.ShapeDtypeStruct((T // MOE_TILE, N_EXPERTS, LANES), jnp.int32)],
        scratch_shapes=[pltpu.VMEM((N_EXPERTS, LANES), F32)],
        compiler_params=_cparams("arbitrary"),
        name="route_shared_ple",
    )(x1, p2d, wr_t, rbias, wsg, wsu, wsd, wpg, wpp)


def _dest_kernel(off_ref, idx_ref, rank_ref, dest_ref):
    idx = idx_ref[...]
    base = pl.program_id(0) * N_EXPERTS

    def body(e, acc):
        return acc + jnp.where(idx == e, off_ref[base + e], 0)

    dest_ref[...] = lax.fori_loop(0, N_EXPERTS, body, rank_ref[...])


def _dest(off, idx, rank):
    T = idx.shape[1]
    spec = pl.BlockSpec((TOP_K, MOE_TILE), lambda i, off: (0, i))
    return pl.pallas_call(
        _dest_kernel,
        grid_spec=pltpu.PrefetchScalarGridSpec(num_scalar_prefetch=1, grid=(T // MOE_TILE,),
                                               in_specs=[spec, spec], out_specs=spec),
        out_shape=jax.ShapeDtypeStruct(idx.shape, jnp.int32),
        compiler_params=_cparams("arbitrary"),
        name="moe_dest",
    )(off, idx, rank)


def _plan_kernel(dest_ref, list_ref):
    first = pl.program_id(1) * PLAN_ROWS

    @pl.when(pl.program_id(1) == 0)
    def _():
        def pad(u, c):
            list_ref[MOE_TILE * TOP_K + u] = 0
            return c

        lax.fori_loop(0, LIST_PAD, pad, 0, unroll=8)

    def body(t, c):
        for k in range(TOP_K):
            list_ref[dest_ref[k, t]] = (first + t) * TOP_K + k
        return c

    lax.fori_loop(0, PLAN_ROWS, body, 0, unroll=4)


def _plan(dest):
    T = dest.shape[1]
    per_tile = MOE_TILE // PLAN_ROWS
    return pl.pallas_call(
        _plan_kernel,
        grid=(T // MOE_TILE, per_tile),
        in_specs=[pl.BlockSpec((TOP_K, PLAN_ROWS), lambda i, j: (0, i * per_tile + j), memory_space=pltpu.SMEM)],
        out_specs=pl.BlockSpec((LIST_LEN,), lambda i, j: (i,), memory_space=pltpu.SMEM),
        out_shape=jax.ShapeDtypeStruct((T // MOE_TILE * LIST_LEN,), jnp.int32),
        compiler_params=_cparams("arbitrary", "arbitrary"),
        name="moe_plan",
    )(dest)


def _moe_tile_kernel(cnt_ref, off_ref, list_ref, gate_ref, wg_ref, wu_ref, wd_ref, lw_ref, lb_ref, x_hbm, base_hbm,
                     o_hbm, x_s, acc_s, xg_s, y_s, stage_s, wg_b, wu_b, wd_b):
    tile, e = pl.program_id(0), pl.program_id(1)
    rows_of = lambda ref, r, n: ref.at[pl.ds(pl.multiple_of(r * LANE_CHUNKS, LANE_CHUNKS), n * LANE_CHUNKS), :]
    tile_rows = pl.ds(pl.multiple_of(tile * (MOE_TILE * LANE_CHUNKS), LANE_CHUNKS), MOE_TILE * LANE_CHUNKS)

    @pl.when(e == 0)
    def _():
        pltpu.sync_copy(x_hbm.at[tile_rows, :], x_s)
        pltpu.sync_copy(base_hbm.at[tile_rows, :], acc_s)

    wg_b[...] = wg_ref[...].astype(BF16)
    wu_b[...] = wu_ref[...].astype(BF16)
    wd_b[...] = wd_ref[...].astype(BF16)
    n = cnt_ref[tile * N_EXPERTS + e]
    off = off_ref[tile * N_EXPERTS + e]
    tile_of = lambda ref, code: ref.at[pl.ds(pl.multiple_of(code & -LANE_CHUNKS, LANE_CHUNKS), LANE_CHUNKS), :]

    def chunk(c, carry):
        first = off + c * MOE_CHUNK

        def gather(jb, cc):
            at = first + jb * GATHER_GROUP
            rows = [tile_of(x_s, list_ref[at + u])[...] for u in range(GATHER_GROUP)]
            rows_of(xg_s, jb * GATHER_GROUP, GATHER_GROUP)[...] = jnp.concatenate(rows, axis=0)
            return cc

        lax.fori_loop(0, MOE_CHUNK // GATHER_GROUP, gather, 0)
        xb = _load_row_tiles(xg_s, MOE_CHUNK).astype(BF16)
        hg = jnp.dot(xb, wg_b[...], preferred_element_type=F32)
        hu = jnp.dot(xb, wu_b[...], preferred_element_type=F32)
        act = (hg * jax.nn.sigmoid(hg) * hu).astype(BF16)
        _store_row_tiles(y_s, jnp.dot(act, wd_b[...], preferred_element_type=F32), MOE_CHUNK)

        m = jnp.minimum(MOE_CHUNK, n - c * MOE_CHUNK)

        def scatter_group(j0, live):
            at = first + j0
            codes = [list_ref[at + u] for u in range(live)]
            y = rows_of(y_s, j0, live)[...]
            vals = [tile_of(acc_s, code)[...] + gate_ref[code] * y[u * LANE_CHUNKS:(u + 1) * LANE_CHUNKS]
                    for u, code in enumerate(codes)]
            for code, val in zip(codes, vals):
                tile_of(acc_s, code)[...] = val

        def scatter(jg, cc):
            scatter_group(jg * SCATTER_GROUP, SCATTER_GROUP)
            return cc

        lax.fori_loop(0, m // SCATTER_GROUP, scatter, 0)
        for live in range(1, SCATTER_GROUP):
            @pl.when(m % SCATTER_GROUP == live)
            def _(live=live):
                scatter_group(m - live, live)
        return carry

    lax.fori_loop(0, (n + MOE_CHUNK - 1) // MOE_CHUNK, chunk, 0)

    @pl.when(e == N_EXPERTS - 1)
    def _():
        for c in range(MOE_TILE // MIX_ROWS):
            z = _load_row_tiles(acc_s, MIX_ROWS, c * MIX_ROWS * LANE_CHUNKS)
            stage_s[...] = _layer_norm(z, lw_ref[...], lb_ref[...])
            pltpu.sync_copy(stage_s, o_hbm.at[pl.ds(tile * MOE_TILE + c * MIX_ROWS, MIX_ROWS), :])


def _moe_tiles(x1, base, tok_list, gate, cnt, off, wg, wu, wd, ln_w, ln_b):
    T = x1.shape[0] // LANE_CHUNKS
    w_spec = lambda shape: pl.BlockSpec((None,) + shape, lambda i, e, cnt, off: (e, 0, 0))
    vec = pl.BlockSpec((1, D_MODEL), lambda i, e, cnt, off: (0, 0))
    hbm = pl.BlockSpec(memory_space=pl.ANY)
    tile_rows = MOE_TILE * LANE_CHUNKS
    return pl.pallas_call(
        _moe_tile_kernel,
        grid_spec=pltpu.PrefetchScalarGridSpec(
            num_scalar_prefetch=2,
            grid=(T // MOE_TILE, N_EXPERTS),
            in_specs=[pl.BlockSpec((LIST_LEN,), lambda i, e, cnt, off: (i,), memory_space=pltpu.SMEM),
                      pl.BlockSpec((MOE_TILE * TOP_K,), lambda i, e, cnt, off: (i,), memory_space=pltpu.SMEM),
                      w_spec((D_MODEL, EXPERT_FF)), w_spec((D_MODEL, EXPERT_FF)), w_spec((EXPERT_FF, D_MODEL)),
                      vec, vec, hbm, hbm],
            out_specs=hbm,
            scratch_shapes=[pltpu.VMEM((tile_rows, LANES), F32),
                            pltpu.VMEM((tile_rows, LANES), F32),
                            pltpu.VMEM((MOE_CHUNK * LANE_CHUNKS, LANES), F32),
                            pltpu.VMEM((MOE_CHUNK * LANE_CHUNKS, LANES), F32),
                            pltpu.VMEM((MIX_ROWS, D_MODEL), F32),
                            pltpu.VMEM((D_MODEL, EXPERT_FF), BF16), pltpu.VMEM((D_MODEL, EXPERT_FF), BF16),
                            pltpu.VMEM((EXPERT_FF, D_MODEL), BF16)],
        ),
        out_shape=jax.ShapeDtypeStruct((T, D_MODEL), F32),
        compiler_params=_cparams("arbitrary", "arbitrary"),
        name="moe_tiles_ln2",
    )(cnt, off, tok_list, gate, wg, wu, wd, ln_w, ln_b, x1, base)


def kernel(x, p, w_in, hgrn_lb_logits, hgrn_norm_w, w_branch_att, w_branch_hgrn, w_out, ln1_w, ln1_b, router_w, router_bias, expert_w_gate, expert_w_up, expert_w_down, shared_w_gate, shared_w_up, shared_w_down, ple_gate_w, ple_proj_w, ln2_w, ln2_b):
    B, S, D = x.shape
    T = B * S
    l = 0
    x2d = x.reshape(T, D)
    bf = lambda a: a.astype(BF16)

    ws = _att_weights(w_in[l])
    qkv = [_proj_att(x2d, ws[g], d) for g, d in enumerate(ATT_DILATIONS)]
    y_att = _attention(qkv, B, S)
    u_hg = _proj(x2d, bf(w_in[l][:, 3 * len(ATT_DILATIONS) * ATT_WIDTH:]), 1536)
    y_hg = _hgrn(u_hg, hgrn_lb_logits, hgrn_norm_w[l:l + 1], B, S)
    x1 = _merge(y_att, y_hg, u_hg, x2d, bf(w_branch_att[l]), bf(w_branch_hgrn[l]), bf(w_out[l]),
                ln1_w[l:l + 1], ln1_b[l:l + 1])

    base, idx, gate, rank, counts = _route(
        x1, p[l].reshape(T, PLE_DIM), router_w[l].T, router_bias[l].reshape(N_EXPERTS, 1),
        bf(shared_w_gate[l]), bf(shared_w_up[l]), bf(shared_w_down[l]), bf(ple_gate_w[l]), bf(ple_proj_w[l]))
    cnt = counts[:, :, 0]
    off = jnp.cumsum(cnt, axis=1) - cnt
    cnt, off = cnt.reshape(-1), off.reshape(-1)
    tok_list = _plan(_dest(off, idx, rank))
    gate_list = gate.T.reshape(-1)
    out = _moe_tiles(x1, base, tok_list, gate_list, cnt, off, expert_w_gate[l], expert_w_up[l], expert_w_down[l],
                     ln2_w[l:l + 1], ln2_b[l:l + 1])
    return out.reshape(B, S, D)
```

```python
import functools

import jax
import jax.numpy as jnp
import numpy as np
from jax import lax
from jax.experimental import pallas as pl
from jax.experimental.pallas import tpu as pltpu

F32 = jnp.float32
BF16 = jnp.bfloat16

D_MODEL = 1024
ATT_HEAD_DIM = 64
ATT_HEADS = 8
ATT_DILATIONS = (1, 4, 16)
ATT_BLOCK = 128
ATT_WIDTH = ATT_HEADS * ATT_HEAD_DIM
ATT_TILE = ATT_BLOCK * max(ATT_DILATIONS)
NEG_INF = -1e30

HG_HEADS = 8
HG_DIM = 128
HG_WIDTH = HG_HEADS * HG_DIM
HG_CHUNK = 32
HG_TILE = 256
RMS_EPS = 1e-6

N_EXPERTS = 64
TOP_K = 8
TOP_K_BITS = 3
N_GROUPS = 8
GROUP_SIZE = N_EXPERTS // N_GROUPS
TOPK_GROUPS = 4
EXPERT_FF = 256
ROUTED_SCALE = 2.5
PLE_DIM = 256
LN_EPS = 1e-5
DEPTH = 1
DEEPNORM_ALPHA = (2.0 * DEPTH) ** 0.25

LANES = 128
LANE_CHUNKS = D_MODEL // LANES
PROJ_ROWS = 512
MIX_ROWS = 512
MOE_TILE = 4096
MOE_CHUNK = 576
PLAN_CODES = 4096
PLAN_GROUP = 16
LIST_PAD = 1024
LIST_LEN = MOE_TILE * TOP_K + LIST_PAD
GATHER_GROUP = 8
SCATTER_GROUP = 8
V7X_VMEM_LIMIT = 56 * 1024 * 1024


def _cparams(*sem):
    return pltpu.CompilerParams(dimension_semantics=sem, vmem_limit_bytes=V7X_VMEM_LIMIT)


def _proj_att_kernel(*refs, dil):
    x_refs, w_ref, o_ref = refs[:LANE_CHUNKS], refs[LANE_CHUNKS], refs[LANE_CHUNKS + 1]
    n = PROJ_ROWS // dil

    def rows(ref):
        if dil == 1:
            return ref[...]
        return jnp.concatenate([ref[pl.ds(r, n, stride=dil), :] for r in range(dil)], axis=0)

    xp = jnp.concatenate([rows(ref).astype(BF16) for ref in x_refs], axis=1)
    y = jnp.dot(xp, w_ref[...], preferred_element_type=F32)
    o_ref[...] = y.astype(BF16).reshape(dil, n, 3 * ATT_WIDTH)


def _proj_att(x2d, w, dil):
    T = x2d.shape[0]
    per = ATT_TILE // PROJ_ROWS
    n = PROJ_ROWS // dil
    out = pl.pallas_call(
        functools.partial(_proj_att_kernel, dil=dil),
        grid=(T // PROJ_ROWS,),
        in_specs=[pl.BlockSpec((PROJ_ROWS, LANES), functools.partial(lambda i, c: (i, c), c=c))
                  for c in range(LANE_CHUNKS)]
                 + [pl.BlockSpec((D_MODEL, 3 * ATT_WIDTH), lambda i: (0, 0))],
        out_specs=pl.BlockSpec((None, dil, None, n, 3 * ATT_WIDTH), lambda i: (i // per, 0, i % per, 0, 0)),
        out_shape=jax.ShapeDtypeStruct((T // ATT_TILE, dil, per, n, 3 * ATT_WIDTH), BF16),
        compiler_params=_cparams("parallel"),
        name=f"proj_att_d{dil}",
    )(*([x2d] * LANE_CHUNKS), w)
    return out.reshape(T // ATT_TILE, dil, ATT_TILE // dil, 3 * ATT_WIDTH)


def _att_pair(q2, kp, kc, vp, vc, bias_ref, g, first):
    def head0_lanes(rows, dtype):
        lane = lax.broadcasted_iota(jnp.int32, (rows, 2 * ATT_HEAD_DIM), 1)
        return lane.astype(F32).astype(dtype) < ATT_HEAD_DIM

    lo_q = head0_lanes(ATT_BLOCK, BF16)
    lo_v = head0_lanes(2 * ATT_BLOCK, BF16)
    k2 = jnp.concatenate([kp, kc], axis=0)
    v2 = jnp.concatenate([vp, vc], axis=0)
    zero = jnp.zeros_like(q2)
    ps, ms = [], []
    for hh in range(2):
        qm = jnp.where(lo_q, q2, zero) if hh == 0 else jnp.where(lo_q, zero, q2)
        s = lax.dot_general(qm, k2, (((1,), (1,)), ((), ())), preferred_element_type=F32)
        s = s + bias_ref[g, hh, first]
        m = jnp.max(s, axis=-1, keepdims=True)
        ps.append(jnp.exp(s - m).astype(BF16))
        ms.append(m)
    pcat = jnp.concatenate(ps, axis=1)
    zero_v, one_v = jnp.zeros_like(v2), jnp.ones_like(v2)
    rhs = jnp.concatenate([
        jnp.concatenate([jnp.where(lo_v, v2, zero_v), jnp.where(lo_v, one_v, zero_v)], axis=1),
        jnp.concatenate([jnp.where(lo_v, zero_v, v2), jnp.where(lo_v, zero_v, one_v)], axis=1)], axis=0)
    nd = jnp.dot(pcat, rhs, preferred_element_type=F32)
    m2 = jnp.where(head0_lanes(ATT_BLOCK, F32), ms[0], ms[1])
    return nd[:, :2 * ATT_HEAD_DIM], m2, nd[:, 2 * ATT_HEAD_DIM:]


def _att_kernel(*refs):
    (q0, kc0, vc0, kp0, vp0, q1, kc1, vc1, kp1, vp1, q2, kc2, vc2, kp2, vp2,
     bias_ref, o_ref) = refs[:17]
    ng = len(ATT_DILATIONS)
    num_s, m_s, den_s = refs[17:17 + ng], refs[17 + ng:17 + 2 * ng], refs[17 + 2 * ng:]
    first_tile = (pl.program_id(2) == 0).astype(jnp.int32)
    groups = ((q0, kc0, vc0, kp0, vp0), (q1, kc1, vc1, kp1, vp1), (q2, kc2, vc2, kp2, vp2))
    for g, dil in enumerate(ATT_DILATIONS):
        q_ref, kc_ref, vc_ref, kp_ref, vp_ref = groups[g]
        nb = ATT_TILE // dil // ATT_BLOCK
        for r in range(dil):
            for n in range(nb):
                rows = pl.ds(n * ATT_BLOCK, ATT_BLOCK)
                if n == 0:
                    prev = pl.ds((nb - 1) * ATT_BLOCK, ATT_BLOCK)
                    kp, vp, first = kp_ref[r, prev, :], vp_ref[r, prev, :], first_tile
                else:
                    prev = pl.ds((n - 1) * ATT_BLOCK, ATT_BLOCK)
                    kp, vp, first = kc_ref[r, prev, :], vc_ref[r, prev, :], 0
                num, m, den = _att_pair(q_ref[r, rows, :], kp, kc_ref[r, rows, :], vp, vc_ref[r, rows, :],
                                        bias_ref, g, first)
                if dil == 1:
                    dst = rows
                else:
                    dst = pl.ds(n * ATT_BLOCK * dil + r, ATT_BLOCK, stride=dil)
                num_s[g][dst, :] = num
                m_s[g][dst, :] = m
                den_s[g][dst, :] = den
    m_all = jnp.maximum(jnp.maximum(m_s[0][...], m_s[1][...]), m_s[2][...])
    num = jnp.zeros((ATT_TILE, 2 * ATT_HEAD_DIM), F32)
    den = jnp.zeros((ATT_TILE, 2 * ATT_HEAD_DIM), F32)
    for g in range(ng):
        sc = jnp.exp(m_s[g][...] - m_all)
        num = num + sc * num_s[g][...]
        den = den + sc * den_s[g][...]
    o_ref[...] = (num / den).astype(o_ref.dtype)


def _att_bias_table():
    qi = np.arange(ATT_BLOCK)[:, None]
    ki = np.arange(2 * ATT_BLOCK)[None, :]
    steps = qi + ATT_BLOCK - ki
    valid = (steps >= 0) & (steps <= ATT_BLOCK)
    slopes = np.array([2.0 ** (-8.0 * (h + 1) / ATT_HEADS) for h in range(ATT_HEADS)], np.float32)
    tab = np.empty((len(ATT_DILATIONS), ATT_HEADS, 2, ATT_BLOCK, 2 * ATT_BLOCK), np.float32)
    for g, dil in enumerate(ATT_DILATIONS):
        bias = -slopes[:, None, None] * (steps * dil).astype(np.float32)[None]
        tab[g, :, 0] = np.where(valid[None], bias, NEG_INF)
        tab[g, :, 1] = np.where((valid & (ki >= ATT_BLOCK))[None], bias, NEG_INF)
    return jnp.asarray(tab)


def _attention(qkv, B, S):
    tiles = S // ATT_TILE
    pair = 2 * ATT_HEAD_DIM
    npair = ATT_WIDTH // pair
    in_specs, args = [], []
    for g, dil in enumerate(ATT_DILATIONS):
        blk = (None, dil, ATT_TILE // dil, pair)
        cur = lambda b, hp, t, off: (b * tiles + t, 0, 0, off * npair + hp)
        prv = lambda b, hp, t, off: (b * tiles + jnp.maximum(t - 1, 0), 0, 0, off * npair + hp)
        in_specs += [pl.BlockSpec(blk, functools.partial(cur, off=0)),
                     pl.BlockSpec(blk, functools.partial(cur, off=1)),
                     pl.BlockSpec(blk, functools.partial(cur, off=2)),
                     pl.BlockSpec(blk, functools.partial(prv, off=1)),
                     pl.BlockSpec(blk, functools.partial(prv, off=2))]
        args += [qkv[g]] * 5
    in_specs.append(pl.BlockSpec((len(ATT_DILATIONS), 2, 2, ATT_BLOCK, 2 * ATT_BLOCK),
                                 lambda b, hp, t: (0, hp, 0, 0, 0)))
    args.append(_att_bias_table())
    scratch = [pltpu.VMEM((ATT_TILE, pair), F32) for _ in range(3 * len(ATT_DILATIONS))]
    return pl.pallas_call(
        _att_kernel,
        grid=(B, npair, tiles),
        in_specs=in_specs,
        out_specs=pl.BlockSpec((ATT_TILE, pair), lambda b, hp, t: (b * tiles + t, hp)),
        out_shape=jax.ShapeDtypeStruct((B * S, ATT_WIDTH), BF16),
        scratch_shapes=scratch,
        compiler_params=_cparams("parallel", "parallel", "arbitrary"),
        name="dilated_attention",
    )(*args)


def _att_weights(w_in_l):
    out = []
    width = len(ATT_DILATIONS) * ATT_WIDTH
    for g in range(len(ATT_DILATIONS)):
        cols = [w_in_l[:, part * width + g * ATT_WIDTH: part * width + (g + 1) * ATT_WIDTH] for part in range(3)]
        cols[0] = cols[0] * (ATT_HEAD_DIM ** -0.5)
        out.append(jnp.concatenate(cols, axis=1).astype(BF16))
    return out


def _proj_kernel(x_ref, w_ref, o_ref, *, col_tile):
    xb = x_ref[...].astype(BF16)
    for c in range(w_ref.shape[1] // col_tile):
        cols = slice(c * col_tile, (c + 1) * col_tile)
        o_ref[:, cols] = jnp.dot(xb, w_ref[:, cols], preferred_element_type=F32).astype(o_ref.dtype)


def _proj(x2d, w, col_tile):
    T, N = x2d.shape[0], w.shape[1]
    return pl.pallas_call(
        functools.partial(_proj_kernel, col_tile=col_tile),
        grid=(T // PROJ_ROWS,),
        in_specs=[pl.BlockSpec((PROJ_ROWS, D_MODEL), lambda i: (i, 0)),
                  pl.BlockSpec((D_MODEL, N), lambda i: (0, 0))],
        out_specs=pl.BlockSpec((PROJ_ROWS, N), lambda i: (i, 0)),
        out_shape=jax.ShapeDtypeStruct((T, N), BF16),
        compiler_params=_cparams("parallel"),
        name="proj_hgrn_gates",
    )(x2d, w)


def _split3(v):
    a = v.astype(BF16)
    r = v - a.astype(F32)
    b = r.astype(BF16)
    c = (r - b.astype(F32)).astype(BF16)
    return a, b, c


def _hgrn_kernel(q_ref, f_ref, i_ref, g_ref, lbl_ref, gain_ref, o_ref, state_ref):
    @pl.when(pl.program_id(1) == 0)
    def _():
        state_ref[...] = jnp.zeros_like(state_ref)

    lbl = lbl_ref[...]
    e = jnp.exp(lbl - jnp.max(lbl, axis=0, keepdims=True))
    lb = e[0:1] / jnp.sum(e, axis=0, keepdims=True)
    forget = lb + (1.0 - lb) * jax.nn.sigmoid(f_ref[...].astype(F32))
    log_f = jnp.log(forget)
    key = 1.0 - forget

    row = lax.broadcasted_iota(jnp.int32, (HG_TILE, HG_TILE), 0)
    col = lax.broadcasted_iota(jnp.int32, (HG_TILE, HG_TILE), 1)
    causal = (row >= col) & ((row // HG_CHUNK) == (col // HG_CHUNK))
    tri = jnp.where(causal, 1.0, 0.0).astype(BF16)
    b = sum(jnp.dot(tri, t, preferred_element_type=F32) for t in _split3(log_f))
    eb = jnp.exp(b)
    q_dec = (q_ref[...].astype(F32) * eb).astype(BF16)
    k_inv = key * jnp.exp(-b)
    xi = i_ref[...].astype(F32)
    val = (xi * jax.nn.sigmoid(xi)).astype(BF16)
    k_inv_b = k_inv.astype(BF16)

    n_chunks = HG_TILE // HG_CHUNK
    outs = []
    for h in range(HG_HEADS):
        cols = slice(h * HG_DIM, (h + 1) * HG_DIM)
        qd, ki, vv = q_dec[:, cols], k_inv_b[:, cols], val[:, cols]
        a = lax.dot_general(qd, ki, (((1,), (1,)), ((), ())), preferred_element_type=F32)
        a = jnp.where(causal, a, 0.0).astype(BF16)
        o_intra = jnp.dot(a, vv, preferred_element_type=F32)
        st = state_ref[h]
        pieces = []
        for c in range(n_chunks):
            rows = slice(c * HG_CHUNK, (c + 1) * HG_CHUNK)
            dec = eb[(c + 1) * HG_CHUNK - 1:(c + 1) * HG_CHUNK, cols]
            o_inter = lax.dot_general(qd[rows], st.astype(BF16), (((1,), (1,)), ((), ())),
                                      preferred_element_type=F32)
            pieces.append(o_intra[rows] + o_inter)
            k_end = (k_inv[rows, cols] * dec).astype(BF16)
            st = st * dec + lax.dot_general(vv[rows], k_end, (((0,), (0,)), ((), ())),
                                            preferred_element_type=F32)
        state_ref[h] = st
        o = jnp.concatenate(pieces, axis=0)
        o = o * lax.rsqrt(jnp.mean(jnp.square(o), axis=-1, keepdims=True) + RMS_EPS)
        outs.append(o)
    o = jnp.concatenate(outs, axis=1) * gain_ref[...]
    gg = g_ref[...].astype(F32)
    o_ref[...] = (o * (gg * jax.nn.sigmoid(gg))).astype(o_ref.dtype)


def _hgrn(u_hg, lb_logits, gain, B, S):
    tiles = S // HG_TILE
    col = lambda j: pl.BlockSpec((HG_TILE, HG_WIDTH), functools.partial(lambda b, t, j: (b * tiles + t, j), j=j))
    return pl.pallas_call(
        _hgrn_kernel,
        grid=(B, tiles),
        in_specs=[col(0), col(1), col(2), col(3),
                  pl.BlockSpec((2, HG_WIDTH), lambda b, t: (0, 0)),
                  pl.BlockSpec((1, HG_WIDTH), lambda b, t: (0, 0))],
        out_specs=pl.BlockSpec((HG_TILE, HG_WIDTH), lambda b, t: (b * tiles + t, 0)),
        out_shape=jax.ShapeDtypeStruct((B * S, HG_WIDTH), BF16),
        scratch_shapes=[pltpu.VMEM((HG_HEADS, HG_DIM, HG_DIM), F32)],
        compiler_params=_cparams("parallel", "arbitrary"),
        name="hgrn2",
    )(u_hg, u_hg, u_hg, u_hg, lb_logits, gain)


def _load_row_tiles(ref, n, start=0):
    return jnp.concatenate([ref[pl.ds(start + c, n, stride=LANE_CHUNKS), :] for c in range(LANE_CHUNKS)], axis=1)


def _store_row_tiles(ref, val, n):
    for c in range(LANE_CHUNKS):
        ref[pl.ds(c, n, stride=LANE_CHUNKS), :] = val[:, c * LANES:(c + 1) * LANES]


def _layer_norm(z, w, b):
    mu = jnp.mean(z, axis=-1, keepdims=True)
    zc = z - mu
    var = jnp.mean(jnp.square(zc), axis=-1, keepdims=True)
    return zc * lax.rsqrt(var + LN_EPS) * w + b


def _merge_kernel(ya_ref, yh_ref, ga_ref, gh_ref, x_ref, wa_ref, wh_ref, wo_ref, lw_ref, lb_ref, o_ref):
    ma = jnp.dot(ya_ref[...], wa_ref[...], preferred_element_type=F32)
    mh = jnp.dot(yh_ref[...], wh_ref[...], preferred_element_type=F32)
    merged = (jax.nn.sigmoid(ga_ref[...].astype(F32)) * ma + jax.nn.sigmoid(gh_ref[...].astype(F32)) * mh)
    z = DEEPNORM_ALPHA * x_ref[...] + jnp.dot(merged.astype(BF16), wo_ref[...], preferred_element_type=F32)
    _store_row_tiles(o_ref, _layer_norm(z, lw_ref[...], lb_ref[...]), MIX_ROWS)


def _merge(y_att, y_hg, u_hg, x2d, w_a, w_h, w_o, ln_w, ln_b):
    T = x2d.shape[0]
    rows = lambda width, j=0: pl.BlockSpec((MIX_ROWS, width), functools.partial(lambda i, j: (i, j), j=j))
    full = lambda a: pl.BlockSpec(a.shape, lambda i: (0, 0))
    return pl.pallas_call(
        _merge_kernel,
        grid=(T // MIX_ROWS,),
        in_specs=[rows(ATT_WIDTH), rows(HG_WIDTH), rows(D_MODEL, 4), rows(D_MODEL, 5), rows(D_MODEL),
                  full(w_a), full(w_h), full(w_o), full(ln_w), full(ln_b)],
        out_specs=pl.BlockSpec((MIX_ROWS * LANE_CHUNKS, LANES), lambda i: (i, 0)),
        out_shape=jax.ShapeDtypeStruct((T * LANE_CHUNKS, LANES), F32),
        compiler_params=_cparams("parallel"),
        name="merge_ln1",
    )(y_att, y_hg, u_hg, u_hg, x2d, w_a, w_h, w_o, ln_w, ln_b)


def _first_argmax(v, ids, n):
    mx = jnp.max(v, axis=0, keepdims=True)
    return mx, jnp.min(jnp.where(v == mx, ids, n), axis=0, keepdims=True)


def _route_kernel(x1_ref, p_ref, wrt_ref, rb_ref, wsg_ref, wsu_ref, wsd_ref, wpg_ref, wpp_ref,
                  base_ref, idx_ref, gate_ref, rank_ref, cnt_ref, carry_ref):
    @pl.when(pl.program_id(0) % (MOE_TILE // MIX_ROWS) == 0)
    def _():
        carry_ref[...] = jnp.zeros_like(carry_ref)

    x1 = _load_row_tiles(x1_ref, MIX_ROWS)
    x1b = x1.astype(BF16)
    logits = lax.dot_general(wrt_ref[...], x1, (((1,), (1,)), ((), ())), preferred_element_type=F32,
                             precision=lax.Precision.HIGHEST)
    s = jax.nn.sigmoid(logits)
    sel = s + rb_ref[...]
    eid = lax.broadcasted_iota(jnp.int32, (N_EXPERTS, MIX_ROWS), 0)
    neg = -jnp.inf

    grp = sel.reshape(N_GROUPS, GROUP_SIZE, MIX_ROWS)
    mid = lax.broadcasted_iota(jnp.int32, grp.shape, 1)
    m1 = jnp.max(grp, axis=1, keepdims=True)
    i1 = jnp.min(jnp.where(grp == m1, mid, GROUP_SIZE), axis=1, keepdims=True)
    m2 = jnp.max(jnp.where(mid == i1, neg, grp), axis=1, keepdims=True)
    gscore = (m1 + m2).reshape(N_GROUPS, MIX_ROWS)
    gid = lax.broadcasted_iota(jnp.int32, (N_GROUPS, MIX_ROWS), 0)
    gsel = jnp.zeros((N_GROUPS, MIX_ROWS), jnp.bool_)
    for _ in range(TOPK_GROUPS):
        _, gi = _first_argmax(gscore, gid, N_GROUPS)
        hit = gid == gi
        gsel = gsel | hit
        gscore = jnp.where(hit, neg, gscore)
    emask = jnp.broadcast_to(gsel.reshape(N_GROUPS, 1, MIX_ROWS), grp.shape).reshape(N_EXPERTS, MIX_ROWS)
    cand = jnp.where(emask, sel, neg)

    idxs, gates = [], []
    chosen = jnp.zeros((N_EXPERTS, MIX_ROWS), jnp.bool_)
    for _ in range(TOP_K):
        _, ei = _first_argmax(cand, eid, N_EXPERTS)
        hit = eid == ei
        idxs.append(ei)
        gates.append(jnp.sum(jnp.where(hit, s, 0.0), axis=0, keepdims=True))
        chosen = chosen | hit
        cand = jnp.where(hit, neg, cand)
    g = jnp.concatenate(gates, axis=0)
    g = g / jnp.sum(g, axis=0, keepdims=True) * ROUTED_SCALE
    idx_ref[...] = jnp.concatenate(idxs, axis=0)
    gate_ref[...] = g

    onehot = jnp.where(chosen, 1.0, 0.0)
    tr = lax.broadcasted_iota(jnp.int32, (MIX_ROWS, MIX_ROWS), 0)
    tc = lax.broadcasted_iota(jnp.int32, (MIX_ROWS, MIX_ROWS), 1)
    before = jnp.where(tr < tc, 1.0, 0.0).astype(BF16)
    prefix = jnp.dot(onehot.astype(BF16), before, preferred_element_type=F32)
    rankfull = (carry_ref[:, 0:1] + prefix).astype(jnp.int32)
    rank_ref[...] = jnp.concatenate(
        [jnp.sum(jnp.where(eid == ei, rankfull, 0), axis=0, keepdims=True) for ei in idxs], axis=0)
    total = carry_ref[...] + jnp.sum(onehot, axis=1, keepdims=True)
    carry_ref[...] = total
    cnt_ref[...] = total.astype(jnp.int32)

    hg = jnp.dot(x1b, wsg_ref[...], preferred_element_type=F32)
    hu = jnp.dot(x1b, wsu_ref[...], preferred_element_type=F32)
    shared = jnp.dot((hg * jax.nn.sigmoid(hg) * hu).astype(BF16), wsd_ref[...], preferred_element_type=F32)
    ple = (jax.nn.sigmoid(jnp.dot(x1b, wpg_ref[...], preferred_element_type=F32))
           * jnp.dot(p_ref[...].astype(BF16), wpp_ref[...], preferred_element_type=F32))
    _store_row_tiles(base_ref, DEEPNORM_ALPHA * x1 + shared + ple, MIX_ROWS)


def _route(x1, p2d, wr_t, rbias, wsg, wsu, wsd, wpg, wpp):
    T = x1.shape[0] // LANE_CHUNKS
    per_tile = MOE_TILE // MIX_ROWS
    full = lambda a: pl.BlockSpec(a.shape, lambda i: (0, 0))
    tok = pl.BlockSpec((TOP_K, MIX_ROWS), lambda i: (0, i))
    row_tiles = pl.BlockSpec((MIX_ROWS * LANE_CHUNKS, LANES), lambda i: (i, 0))
    return pl.pallas_call(
        _route_kernel,
        grid=(T // MIX_ROWS,),
        in_specs=[row_tiles,
                  pl.BlockSpec((MIX_ROWS, PLE_DIM), lambda i: (i, 0)),
                  full(wr_t), full(rbias), full(wsg), full(wsu), full(wsd), full(wpg), full(wpp)],
        out_specs=[row_tiles, tok, tok, tok,
                   pl.BlockSpec((None, N_EXPERTS, LANES), lambda i: (i // per_tile, 0, 0))],
        out_shape=[jax.ShapeDtypeStruct((T * LANE_CHUNKS, LANES), F32),
                   jax.ShapeDtypeStruct((TOP_K, T), jnp.int32),
                   jax.ShapeDtypeStruct((TOP_K, T), F32),
                   jax.ShapeDtypeStruct((TOP_K, T), jnp.int32),
                   jax.ShapeDtypeStruct((T // MOE_TILE, N_EXPERTS, LANES), jnp.int32)],
        scratch_shapes=[pltpu.VMEM((N_EXPERTS, LANES), F32)],
        compiler_params=_cparams("arbitrary"),
        name="route_shared_ple",
    )(x1, p2d, wr_t, rbias, wsg, wsu, wsd, wpg, wpp)


def _dest_kernel(off_ref, idx_ref, rank_ref, dest_ref):
    idx = idx_ref[...]
    base = pl.program_id(0) * N_EXPERTS

    def body(e, acc):
        return acc + jnp.where(idx == e, off_ref[base + e], 0)

    dest_ref[...] = lax.fori_loop(0, N_EXPERTS, body, rank_ref[...])


def _dest(off, idx, rank):
    T = idx.shape[1]
    spec = pl.BlockSpec((TOP_K, MOE_TILE), lambda i, off: (0, i))
    return pl.pallas_call(
        _dest_kernel,
        grid_spec=pltpu.PrefetchScalarGridSpec(num_scalar_prefetch=1, grid=(T // MOE_TILE,),
                                               in_specs=[spec, spec], out_specs=spec),
        out_shape=jax.ShapeDtypeStruct(idx.shape, jnp.int32),
        compiler_params=_cparams("arbitrary"),
        name="moe_dest",
    )(off, idx, rank)


def _plan_kernel(dest_ref, list_ref):
    first = pl.program_id(1) * PLAN_CODES

    @pl.when(pl.program_id(1) == 0)
    def _():
        def pad(u, c):
            list_ref[MOE_TILE * TOP_K + u] = 0
            return c

        lax.fori_loop(0, LIST_PAD, pad, 0, unroll=8)

    def body(i, c):
        a = i * PLAN_GROUP
        dests = [dest_ref[a + u] for u in range(PLAN_GROUP)]
        for u, d in enumerate(dests):
            list_ref[d] = first + a + u
        return c

    lax.fori_loop(0, PLAN_CODES // PLAN_GROUP, body, 0)


def _plan(dest):
    n_tiles = dest.shape[0] // (MOE_TILE * TOP_K)
    per_tile = MOE_TILE * TOP_K // PLAN_CODES
    return pl.pallas_call(
        _plan_kernel,
        grid=(n_tiles, per_tile),
        in_specs=[pl.BlockSpec((PLAN_CODES,), lambda i, j: (i * per_tile + j,), memory_space=pltpu.SMEM)],
        out_specs=pl.BlockSpec((LIST_LEN,), lambda i, j: (i,), memory_space=pltpu.SMEM),
        out_shape=jax.ShapeDtypeStruct((n_tiles * LIST_LEN,), jnp.int32),
        compiler_params=_cparams("arbitrary", "arbitrary"),
        name="moe_plan",
    )(dest)


def _moe_tile_kernel(cnt_ref, off_ref, list_ref, gate_ref, wg_ref, wu_ref, wd_ref, lw_ref, lb_ref, x_hbm, base_hbm,
                     o_hbm, x_s, acc_s, xg_s, y_s, stage_s, wg_b, wu_b, wd_b):
    tile, e = pl.program_id(0), pl.program_id(1)
    rows_of = lambda ref, r, n: ref.at[pl.ds(pl.multiple_of(r * LANE_CHUNKS, LANE_CHUNKS), n * LANE_CHUNKS), :]
    tile_rows = pl.ds(pl.multiple_of(tile * (MOE_TILE * LANE_CHUNKS), LANE_CHUNKS), MOE_TILE * LANE_CHUNKS)

    pair = tile * N_EXPERTS + e
    n, off = cnt_ref[pair], off_ref[pair]
    tile_of = lambda ref, code: ref.at[pl.ds(pl.multiple_of(code & -LANE_CHUNKS, LANE_CHUNKS), LANE_CHUNKS), :]

    def gather_group(first, dst_row, jb):
        at = first + jb * GATHER_GROUP
        rows = [tile_of(x_s, list_ref[at + u])[...] for u in range(GATHER_GROUP)]
        rows_of(xg_s, dst_row + jb * GATHER_GROUP, GATHER_GROUP)[...] = jnp.concatenate(rows, axis=0)

    def gather_loop(first, dst_row):
        def body(jb, cc):
            gather_group(first, dst_row, jb)
            return cc

        lax.fori_loop(0, MOE_CHUNK // GATHER_GROUP, body, 0)

    def swiglu(src_row):
        xb = _load_row_tiles(xg_s, MOE_CHUNK, src_row * LANE_CHUNKS).astype(BF16)
        hg = jnp.dot(xb, wg_b[...], preferred_element_type=F32)
        hu = jnp.dot(xb, wu_b[...], preferred_element_type=F32)
        act = (hg * jax.nn.sigmoid(hg) * hu).astype(BF16)
        _store_row_tiles(y_s, jnp.dot(act, wd_b[...], preferred_element_type=F32), MOE_CHUNK)

    cur_row = (e % 2) * MOE_CHUNK
    nxt_row = MOE_CHUNK - cur_row

    @pl.when(e == 0)
    def _():
        pltpu.sync_copy(x_hbm.at[tile_rows, :], x_s)
        pltpu.sync_copy(base_hbm.at[tile_rows, :], acc_s)
        gather_loop(off, cur_row)

    wg_b[...] = wg_ref[...].astype(BF16)
    wu_b[...] = wu_ref[...].astype(BF16)
    wd_b[...] = wd_ref[...].astype(BF16)

    def scatter_chunk(c):
        first = off + c * MOE_CHUNK
        m = jnp.minimum(MOE_CHUNK, n - c * MOE_CHUNK)

        def codes_and_gates(j0, live):
            at = first + j0
            codes = [list_ref[at + u] for u in range(live)]
            return codes, [gate_ref[code] for code in codes]

        def scatter_group(j0, codes, gates):
            y = rows_of(y_s, j0, len(codes))[...]
            vals = [tile_of(acc_s, code)[...] + g * y[u * LANE_CHUNKS:(u + 1) * LANE_CHUNKS]
                    for u, (code, g) in enumerate(zip(codes, gates))]
            for code, val in zip(codes, vals):
                tile_of(acc_s, code)[...] = val

        def scatter(jg, cc):
            scatter_group(jg * SCATTER_GROUP, *codes_and_gates(jg * SCATTER_GROUP, SCATTER_GROUP))
            return cc

        lax.fori_loop(0, m // SCATTER_GROUP, scatter, 0)
        for live in range(1, SCATTER_GROUP):
            @pl.when(m % SCATTER_GROUP == live)
            def _(live=live):
                scatter_group(m - live, *codes_and_gates(m - live, live))

    swiglu(cur_row)
    next_off = off_ref[jnp.minimum(pair + 1, pl.num_programs(0) * N_EXPERTS - 1)]
    for jb in range(MOE_CHUNK // GATHER_GROUP):
        gather_group(next_off, nxt_row, jb)
    scatter_chunk(0)

    def extra_chunk(c, carry):
        gather_loop(off + c * MOE_CHUNK, 2 * MOE_CHUNK)
        swiglu(2 * MOE_CHUNK)
        scatter_chunk(c)
        return carry

    lax.fori_loop(1, (n + MOE_CHUNK - 1) // MOE_CHUNK, extra_chunk, 0)

    @pl.when(e == N_EXPERTS - 1)
    def _():
        for c in range(MOE_TILE // MIX_ROWS):
            z = _load_row_tiles(acc_s, MIX_ROWS, c * MIX_ROWS * LANE_CHUNKS)
            stage_s[...] = _layer_norm(z, lw_ref[...], lb_ref[...])
            pltpu.sync_copy(stage_s, o_hbm.at[pl.ds(tile * MOE_TILE + c * MIX_ROWS, MIX_ROWS), :])


def _moe_tiles(x1, base, tok_list, gate, cnt, off, wg, wu, wd, ln_w, ln_b):
    T = x1.shape[0] // LANE_CHUNKS
    w_spec = lambda shape: pl.BlockSpec((None,) + shape, lambda i, e, cnt, off: (e, 0, 0))
    vec = pl.BlockSpec((1, D_MODEL), lambda i, e, cnt, off: (0, 0))
    hbm = pl.BlockSpec(memory_space=pl.ANY)
    tile_rows = MOE_TILE * LANE_CHUNKS
    return pl.pallas_call(
        _moe_tile_kernel,
        grid_spec=pltpu.PrefetchScalarGridSpec(
            num_scalar_prefetch=2,
            grid=(T // MOE_TILE, N_EXPERTS),
            in_specs=[pl.BlockSpec((LIST_LEN,), lambda i, e, cnt, off: (i,), memory_space=pltpu.SMEM),
                      pl.BlockSpec((MOE_TILE * TOP_K,), lambda i, e, cnt, off: (i,), memory_space=pltpu.SMEM),
                      w_spec((D_MODEL, EXPERT_FF)), w_spec((D_MODEL, EXPERT_FF)), w_spec((EXPERT_FF, D_MODEL)),
                      vec, vec, hbm, hbm],
            out_specs=hbm,
            scratch_shapes=[pltpu.VMEM((tile_rows, LANES), F32),
                            pltpu.VMEM((tile_rows, LANES), F32),
                            pltpu.VMEM((3 * MOE_CHUNK * LANE_CHUNKS, LANES), F32),
                            pltpu.VMEM((MOE_CHUNK * LANE_CHUNKS, LANES), F32),
                            pltpu.VMEM((MIX_ROWS, D_MODEL), F32),
                            pltpu.VMEM((D_MODEL, EXPERT_FF), BF16), pltpu.VMEM((D_MODEL, EXPERT_FF), BF16),
                            pltpu.VMEM((EXPERT_FF, D_MODEL), BF16)],
        ),
        out_shape=jax.ShapeDtypeStruct((T, D_MODEL), F32),
        compiler_params=_cparams("arbitrary", "arbitrary"),
        name="moe_tiles_ln2",
    )(cnt, off, tok_list, gate, wg, wu, wd, ln_w, ln_b, x1, base)


def kernel(x, p, w_in, hgrn_lb_logits, hgrn_norm_w, w_branch_att, w_branch_hgrn, w_out, ln1_w, ln1_b, router_w, router_bias, expert_w_gate, expert_w_up, expert_w_down, shared_w_gate, shared_w_up, shared_w_down, ple_gate_w, ple_proj_w, ln2_w, ln2_b):
    B, S, D = x.shape
    T = B * S
    l = 0
    x2d = x.reshape(T, D)
    bf = lambda a: a.astype(BF16)

    ws = _att_weights(w_in[l])
    qkv = [_proj_att(x2d, ws[g], d) for g, d in enumerate(ATT_DILATIONS)]
    y_att = _attention(qkv, B, S)
    u_hg = _proj(x2d, bf(w_in[l][:, 3 * len(ATT_DILATIONS) * ATT_WIDTH:]), 1536)
    y_hg = _hgrn(u_hg, hgrn_lb_logits, hgrn_norm_w[l:l + 1], B, S)
    x1 = _merge(y_att, y_hg, u_hg, x2d, bf(w_branch_att[l]), bf(w_branch_hgrn[l]), bf(w_out[l]),
                ln1_w[l:l + 1], ln1_b[l:l + 1])

    base, idx, gate, rank, counts = _route(
        x1, p[l].reshape(T, PLE_DIM), router_w[l].T, router_bias[l].reshape(N_EXPERTS, 1),
        bf(shared_w_gate[l]), bf(shared_w_up[l]), bf(shared_w_down[l]), bf(ple_gate_w[l]), bf(ple_proj_w[l]))
    cnt = counts[:, :, 0]
    off = jnp.cumsum(cnt, axis=1) - cnt
    cnt, off = cnt.reshape(-1), off.reshape(-1)
    tok_list = _plan(_dest(off, idx, rank).T.reshape(-1))
    gate_list = gate.T.reshape(-1)
    out = _moe_tiles(x1, base, tok_list, gate_list, cnt, off, expert_w_gate[l], expert_w_up[l], expert_w_down[l],
                     ln2_w[l:l + 1], ln2_b[l:l + 1])
    return out.reshape(B, S, D)
```

```python
import functools

import jax
import jax.numpy as jnp
import numpy as np
from jax import lax
from jax.experimental import pallas as pl
from jax.experimental.pallas import tpu as pltpu

F32 = jnp.float32
BF16 = jnp.bfloat16

D_MODEL = 1024
ATT_HEAD_DIM = 64
ATT_HEADS = 8
ATT_DILATIONS = (1, 4, 16)
ATT_BLOCK = 128
ATT_WIDTH = ATT_HEADS * ATT_HEAD_DIM
ATT_TILE = ATT_BLOCK * max(ATT_DILATIONS)
NEG_INF = -1e30

HG_HEADS = 8
HG_DIM = 128
HG_WIDTH = HG_HEADS * HG_DIM
HG_CHUNK = 32
HG_TILE = 256
RMS_EPS = 1e-6

N_EXPERTS = 64
TOP_K = 8
TOP_K_BITS = 3
N_GROUPS = 8
GROUP_SIZE = N_EXPERTS // N_GROUPS
TOPK_GROUPS = 4
EXPERT_FF = 256
ROUTED_SCALE = 2.5
PLE_DIM = 256
LN_EPS = 1e-5
DEPTH = 1
DEEPNORM_ALPHA = (2.0 * DEPTH) ** 0.25

LANES = 128
LANE_CHUNKS = D_MODEL // LANES
PROJ_ROWS = 512
MIX_ROWS = 512
MOE_TILE = 4096
MOE_CHUNK = 576
PLAN_CODES = 4096
PLAN_GROUP = 16
LIST_PAD = 1024
LIST_LEN = MOE_TILE * TOP_K + LIST_PAD
GATHER_GROUP = 8
SCATTER_GROUP = 8
V7X_VMEM_LIMIT = 56 * 1024 * 1024


def _cparams(*sem):
    return pltpu.CompilerParams(dimension_semantics=sem, vmem_limit_bytes=V7X_VMEM_LIMIT)


def _proj_att_kernel(*refs, dil):
    x_refs, w_ref, o_ref = refs[:LANE_CHUNKS], refs[LANE_CHUNKS], refs[LANE_CHUNKS + 1]
    n = PROJ_ROWS // dil

    def rows(ref):
        if dil == 1:
            return ref[...]
        return jnp.concatenate([ref[pl.ds(r, n, stride=dil), :] for r in range(dil)], axis=0)

    xp = jnp.concatenate([rows(ref).astype(BF16) for ref in x_refs], axis=1)
    y = jnp.dot(xp, w_ref[...], preferred_element_type=F32)
    o_ref[...] = y.astype(BF16).reshape(dil, n, 3 * ATT_WIDTH)


def _proj_att(x2d, w, dil):
    T = x2d.shape[0]
    per = ATT_TILE // PROJ_ROWS
    n = PROJ_ROWS // dil
    out = pl.pallas_call(
        functools.partial(_proj_att_kernel, dil=dil),
        grid=(T // PROJ_ROWS,),
        in_specs=[pl.BlockSpec((PROJ_ROWS, LANES), functools.partial(lambda i, c: (i, c), c=c))
                  for c in range(LANE_CHUNKS)]
                 + [pl.BlockSpec((D_MODEL, 3 * ATT_WIDTH), lambda i: (0, 0))],
        out_specs=pl.BlockSpec((None, dil, None, n, 3 * ATT_WIDTH), lambda i: (i // per, 0, i % per, 0, 0)),
        out_shape=jax.ShapeDtypeStruct((T // ATT_TILE, dil, per, n, 3 * ATT_WIDTH), BF16),
        compiler_params=_cparams("parallel"),
        name=f"proj_att_d{dil}",
    )(*([x2d] * LANE_CHUNKS), w)
    return out.reshape(T // ATT_TILE, dil, ATT_TILE // dil, 3 * ATT_WIDTH)


def _att_pair(q2, kp, kc, vp, vc, bias_ref, g, first):
    def head0_lanes(rows, dtype):
        lane = lax.broadcasted_iota(jnp.int32, (rows, 2 * ATT_HEAD_DIM), 1)
        return lane.astype(F32).astype(dtype) < ATT_HEAD_DIM

    lo_q = head0_lanes(ATT_BLOCK, BF16)
    lo_v = head0_lanes(2 * ATT_BLOCK, BF16)
    k2 = jnp.concatenate([kp, kc], axis=0)
    v2 = jnp.concatenate([vp, vc], axis=0)
    zero = jnp.zeros_like(q2)
    ps, ms = [], []
    for hh in range(2):
        qm = jnp.where(lo_q, q2, zero) if hh == 0 else jnp.where(lo_q, zero, q2)
        s = lax.dot_general(qm, k2, (((1,), (1,)), ((), ())), preferred_element_type=F32)
        s = s + bias_ref[g, hh, first]
        m = jnp.max(s, axis=-1, keepdims=True)
        ps.append(jnp.exp(s - m).astype(BF16))
        ms.append(m)
    pcat = jnp.concatenate(ps, axis=1)
    zero_v, one_v = jnp.zeros_like(v2), jnp.ones_like(v2)
    rhs = jnp.concatenate([
        jnp.concatenate([jnp.where(lo_v, v2, zero_v), jnp.where(lo_v, one_v, zero_v)], axis=1),
        jnp.concatenate([jnp.where(lo_v, zero_v, v2), jnp.where(lo_v, zero_v, one_v)], axis=1)], axis=0)
    nd = jnp.dot(pcat, rhs, preferred_element_type=F32)
    m2 = jnp.where(head0_lanes(ATT_BLOCK, F32), ms[0], ms[1])
    return nd[:, :2 * ATT_HEAD_DIM], m2, nd[:, 2 * ATT_HEAD_DIM:]


def _att_kernel(*refs):
    (q0, kc0, vc0, kp0, vp0, q1, kc1, vc1, kp1, vp1, q2, kc2, vc2, kp2, vp2,
     bias_ref, o_ref) = refs[:17]
    ng = len(ATT_DILATIONS)
    num_s, m_s, den_s = refs[17:17 + ng], refs[17 + ng:17 + 2 * ng], refs[17 + 2 * ng:]
    first_tile = (pl.program_id(2) == 0).astype(jnp.int32)
    groups = ((q0, kc0, vc0, kp0, vp0), (q1, kc1, vc1, kp1, vp1), (q2, kc2, vc2, kp2, vp2))
    for g, dil in enumerate(ATT_DILATIONS):
        q_ref, kc_ref, vc_ref, kp_ref, vp_ref = groups[g]
        nb = ATT_TILE // dil // ATT_BLOCK
        for r in range(dil):
            for n in range(nb):
                rows = pl.ds(n * ATT_BLOCK, ATT_BLOCK)
                if n == 0:
                    prev = pl.ds((nb - 1) * ATT_BLOCK, ATT_BLOCK)
                    kp, vp, first = kp_ref[r, prev, :], vp_ref[r, prev, :], first_tile
                else:
                    prev = pl.ds((n - 1) * ATT_BLOCK, ATT_BLOCK)
                    kp, vp, first = kc_ref[r, prev, :], vc_ref[r, prev, :], 0
                num, m, den = _att_pair(q_ref[r, rows, :], kp, kc_ref[r, rows, :], vp, vc_ref[r, rows, :],
                                        bias_ref, g, first)
                if dil == 1:
                    dst = rows
                else:
                    dst = pl.ds(n * ATT_BLOCK * dil + r, ATT_BLOCK, stride=dil)
                num_s[g][dst, :] = num
                m_s[g][dst, :] = m
                den_s[g][dst, :] = den
    m_all = jnp.maximum(jnp.maximum(m_s[0][...], m_s[1][...]), m_s[2][...])
    num = jnp.zeros((ATT_TILE, 2 * ATT_HEAD_DIM), F32)
    den = jnp.zeros((ATT_TILE, 2 * ATT_HEAD_DIM), F32)
    for g in range(ng):
        sc = jnp.exp(m_s[g][...] - m_all)
        num = num + sc * num_s[g][...]
        den = den + sc * den_s[g][...]
    o_ref[...] = (num / den).astype(o_ref.dtype)


def _att_bias_table():
    qi = np.arange(ATT_BLOCK)[:, None]
    ki = np.arange(2 * ATT_BLOCK)[None, :]
    steps = qi + ATT_BLOCK - ki
    valid = (steps >= 0) & (steps <= ATT_BLOCK)
    slopes = np.array([2.0 ** (-8.0 * (h + 1) / ATT_HEADS) for h in range(ATT_HEADS)], np.float32)
    tab = np.empty((len(ATT_DILATIONS), ATT_HEADS, 2, ATT_BLOCK, 2 * ATT_BLOCK), np.float32)
    for g, dil in enumerate(ATT_DILATIONS):
        bias = -slopes[:, None, None] * (steps * dil).astype(np.float32)[None]
        tab[g, :, 0] = np.where(valid[None], bias, NEG_INF)
        tab[g, :, 1] = np.where((valid & (ki >= ATT_BLOCK))[None], bias, NEG_INF)
    return jnp.asarray(tab)


def _attention(qkv, B, S):
    tiles = S // ATT_TILE
    pair = 2 * ATT_HEAD_DIM
    npair = ATT_WIDTH // pair
    in_specs, args = [], []
    for g, dil in enumerate(ATT_DILATIONS):
        blk = (None, dil, ATT_TILE // dil, pair)
        cur = lambda b, hp, t, off: (b * tiles + t, 0, 0, off * npair + hp)
        prv = lambda b, hp, t, off: (b * tiles + jnp.maximum(t - 1, 0), 0, 0, off * npair + hp)
        in_specs += [pl.BlockSpec(blk, functools.partial(cur, off=0)),
                     pl.BlockSpec(blk, functools.partial(cur, off=1)),
                     pl.BlockSpec(blk, functools.partial(cur, off=2)),
                     pl.BlockSpec(blk, functools.partial(prv, off=1)),
                     pl.BlockSpec(blk, functools.partial(prv, off=2))]
        args += [qkv[g]] * 5
    in_specs.append(pl.BlockSpec((len(ATT_DILATIONS), 2, 2, ATT_BLOCK, 2 * ATT_BLOCK),
                                 lambda b, hp, t: (0, hp, 0, 0, 0)))
    args.append(_att_bias_table())
    scratch = [pltpu.VMEM((ATT_TILE, pair), F32) for _ in range(3 * len(ATT_DILATIONS))]
    return pl.pallas_call(
        _att_kernel,
        grid=(B, npair, tiles),
        in_specs=in_specs,
        out_specs=pl.BlockSpec((ATT_TILE, pair), lambda b, hp, t: (b * tiles + t, hp)),
        out_shape=jax.ShapeDtypeStruct((B * S, ATT_WIDTH), BF16),
        scratch_shapes=scratch,
        compiler_params=_cparams("parallel", "parallel", "arbitrary"),
        name="dilated_attention",
    )(*args)


def _att_weights(w_in_l):
    out = []
    width = len(ATT_DILATIONS) * ATT_WIDTH
    for g in range(len(ATT_DILATIONS)):
        cols = [w_in_l[:, part * width + g * ATT_WIDTH: part * width + (g + 1) * ATT_WIDTH] for part in range(3)]
        cols[0] = cols[0] * (ATT_HEAD_DIM ** -0.5)
        out.append(jnp.concatenate(cols, axis=1).astype(BF16))
    return out


def _proj_kernel(x_ref, w_ref, o_ref, *, col_tile):
    xb = x_ref[...].astype(BF16)
    for c in range(w_ref.shape[1] // col_tile):
        cols = slice(c * col_tile, (c + 1) * col_tile)
        o_ref[:, cols] = jnp.dot(xb, w_ref[:, cols], preferred_element_type=F32).astype(o_ref.dtype)


def _proj(x2d, w, col_tile):
    T, N = x2d.shape[0], w.shape[1]
    return pl.pallas_call(
        functools.partial(_proj_kernel, col_tile=col_tile),
        grid=(T // PROJ_ROWS,),
        in_specs=[pl.BlockSpec((PROJ_ROWS, D_MODEL), lambda i: (i, 0)),
                  pl.BlockSpec((D_MODEL, N), lambda i: (0, 0))],
        out_specs=pl.BlockSpec((PROJ_ROWS, N), lambda i: (i, 0)),
        out_shape=jax.ShapeDtypeStruct((T, N), BF16),
        compiler_params=_cparams("parallel"),
        name="proj_hgrn_gates",
    )(x2d, w)


def _split3(v):
    a = v.astype(BF16)
    r = v - a.astype(F32)
    b = r.astype(BF16)
    c = (r - b.astype(F32)).astype(BF16)
    return a, b, c


def _hgrn_kernel(q_ref, f_ref, i_ref, g_ref, lbl_ref, gain_ref, o_ref, state_ref):
    @pl.when(pl.program_id(1) == 0)
    def _():
        state_ref[...] = jnp.zeros_like(state_ref)

    lbl = lbl_ref[...]
    e = jnp.exp(lbl - jnp.max(lbl, axis=0, keepdims=True))
    lb = e[0:1] / jnp.sum(e, axis=0, keepdims=True)
    forget = lb + (1.0 - lb) * jax.nn.sigmoid(f_ref[...].astype(F32))
    log_f = jnp.log(forget)
    key = 1.0 - forget

    row = lax.broadcasted_iota(jnp.int32, (HG_TILE, HG_TILE), 0)
    col = lax.broadcasted_iota(jnp.int32, (HG_TILE, HG_TILE), 1)
    causal = (row >= col) & ((row // HG_CHUNK) == (col // HG_CHUNK))
    tri = jnp.where(causal, 1.0, 0.0).astype(BF16)
    b = sum(jnp.dot(tri, t, preferred_element_type=F32) for t in _split3(log_f))
    eb = jnp.exp(b)
    q_dec = (q_ref[...].astype(F32) * eb).astype(BF16)
    k_inv = key * jnp.exp(-b)
    xi = i_ref[...].astype(F32)
    val = (xi * jax.nn.sigmoid(xi)).astype(BF16)
    k_inv_b = k_inv.astype(BF16)

    n_chunks = HG_TILE // HG_CHUNK
    last_rows = [eb[(c + 1) * HG_CHUNK - 1:(c + 1) * HG_CHUNK, :] for c in range(n_chunks)]
    dec_rows = jnp.concatenate([jnp.broadcast_to(r, (HG_CHUNK, HG_WIDTH)) for r in last_rows], axis=0)
    k_end = (k_inv * dec_rows).astype(BF16)
    chunk_of_row = (lax.broadcasted_iota(jnp.int32, (HG_TILE, HG_DIM), 0) // HG_CHUNK).astype(F32).astype(BF16)
    in_chunk = [chunk_of_row == c for c in range(n_chunks)]
    zero = jnp.zeros((HG_TILE, HG_DIM), BF16)

    def per_chunk_columns(t):
        return jnp.concatenate([jnp.where(m, t, zero) for m in in_chunk], axis=1)

    outs = []
    for h in range(HG_HEADS):
        cols = slice(h * HG_DIM, (h + 1) * HG_DIM)
        qd, ki, vv = q_dec[:, cols], k_inv_b[:, cols], val[:, cols]
        a = lax.dot_general(qd, ki, (((1,), (1,)), ((), ())), preferred_element_type=F32)
        a = jnp.where(causal, a, 0.0).astype(BF16)
        o_intra = jnp.dot(a, vv, preferred_element_type=F32)
        upd = lax.dot_general(vv, per_chunk_columns(k_end[:, cols]), (((0,), (0,)), ((), ())),
                              preferred_element_type=F32)
        st = state_ref[h]
        entering = []
        for c in range(n_chunks):
            entering.append(st.astype(BF16))
            st = st * last_rows[c][:, cols] + upd[:, c * HG_DIM:(c + 1) * HG_DIM]
        state_ref[h] = st
        o_inter = lax.dot_general(per_chunk_columns(qd), jnp.concatenate(entering, axis=1),
                                  (((1,), (1,)), ((), ())), preferred_element_type=F32)
        o = o_intra + o_inter
        o = o * lax.rsqrt(jnp.mean(jnp.square(o), axis=-1, keepdims=True) + RMS_EPS)
        outs.append(o)
    o = jnp.concatenate(outs, axis=1) * gain_ref[...]
    gg = g_ref[...].astype(F32)
    o_ref[...] = (o * (gg * jax.nn.sigmoid(gg))).astype(o_ref.dtype)


def _hgrn(u_hg, lb_logits, gain, B, S):
    tiles = S // HG_TILE
    col = lambda j: pl.BlockSpec((HG_TILE, HG_WIDTH), functools.partial(lambda b, t, j: (b * tiles + t, j), j=j))
    return pl.pallas_call(
        _hgrn_kernel,
        grid=(B, tiles),
        in_specs=[col(0), col(1), col(2), col(3),
                  pl.BlockSpec((2, HG_WIDTH), lambda b, t: (0, 0)),
                  pl.BlockSpec((1, HG_WIDTH), lambda b, t: (0, 0))],
        out_specs=pl.BlockSpec((HG_TILE, HG_WIDTH), lambda b, t: (b * tiles + t, 0)),
        out_shape=jax.ShapeDtypeStruct((B * S, HG_WIDTH), BF16),
        scratch_shapes=[pltpu.VMEM((HG_HEADS, HG_DIM, HG_DIM), F32)],
        compiler_params=_cparams("parallel", "arbitrary"),
        name="hgrn2",
    )(u_hg, u_hg, u_hg, u_hg, lb_logits, gain)


def _load_row_tiles(ref, n, start=0):
    return jnp.concatenate([ref[pl.ds(start + c, n, stride=LANE_CHUNKS), :] for c in range(LANE_CHUNKS)], axis=1)


def _store_row_tiles(ref, val, n):
    for c in range(LANE_CHUNKS):
        ref[pl.ds(c, n, stride=LANE_CHUNKS), :] = val[:, c * LANES:(c + 1) * LANES]


def _layer_norm(z, w, b):
    mu = jnp.mean(z, axis=-1, keepdims=True)
    zc = z - mu
    var = jnp.mean(jnp.square(zc), axis=-1, keepdims=True)
    return zc * lax.rsqrt(var + LN_EPS) * w + b


def _merge_kernel(ya_ref, yh_ref, ga_ref, gh_ref, x_ref, wa_ref, wh_ref, wo_ref, lw_ref, lb_ref, o_ref):
    ma = jnp.dot(ya_ref[...], wa_ref[...], preferred_element_type=F32)
    mh = jnp.dot(yh_ref[...], wh_ref[...], preferred_element_type=F32)
    merged = (jax.nn.sigmoid(ga_ref[...].astype(F32)) * ma + jax.nn.sigmoid(gh_ref[...].astype(F32)) * mh)
    z = DEEPNORM_ALPHA * x_ref[...] + jnp.dot(merged.astype(BF16), wo_ref[...], preferred_element_type=F32)
    _store_row_tiles(o_ref, _layer_norm(z, lw_ref[...], lb_ref[...]), MIX_ROWS)


def _merge(y_att, y_hg, u_hg, x2d, w_a, w_h, w_o, ln_w, ln_b):
    T = x2d.shape[0]
    rows = lambda width, j=0: pl.BlockSpec((MIX_ROWS, width), functools.partial(lambda i, j: (i, j), j=j))
    full = lambda a: pl.BlockSpec(a.shape, lambda i: (0, 0))
    return pl.pallas_call(
        _merge_kernel,
        grid=(T // MIX_ROWS,),
        in_specs=[rows(ATT_WIDTH), rows(HG_WIDTH), rows(D_MODEL, 4), rows(D_MODEL, 5), rows(D_MODEL),
                  full(w_a), full(w_h), full(w_o), full(ln_w), full(ln_b)],
        out_specs=pl.BlockSpec((MIX_ROWS * LANE_CHUNKS, LANES), lambda i: (i, 0)),
        out_shape=jax.ShapeDtypeStruct((T * LANE_CHUNKS, LANES), F32),
        compiler_params=_cparams("parallel"),
        name="merge_ln1",
    )(y_att, y_hg, u_hg, u_hg, x2d, w_a, w_h, w_o, ln_w, ln_b)


def _first_argmax(v, ids, n):
    mx = jnp.max(v, axis=0, keepdims=True)
    return mx, jnp.min(jnp.where(v == mx, ids, n), axis=0, keepdims=True)


def _route_kernel(x1_ref, p_ref, wrt_ref, rb_ref, wsg_ref, wsu_ref, wsd_ref, wpg_ref, wpp_ref,
                  base_ref, idx_ref, gate_ref, rank_ref, cnt_ref, carry_ref):
    @pl.when(pl.program_id(0) % (MOE_TILE // MIX_ROWS) == 0)
    def _():
        carry_ref[...] = jnp.zeros_like(carry_ref)

    x1 = _load_row_tiles(x1_ref, MIX_ROWS)
    x1b = x1.astype(BF16)
    logits = lax.dot_general(wrt_ref[...], x1, (((1,), (1,)), ((), ())), preferred_element_type=F32,
                             precision=lax.Precision.HIGHEST)
    s = jax.nn.sigmoid(logits)
    sel = s + rb_ref[...]
    eid = lax.broadcasted_iota(jnp.int32, (N_EXPERTS, MIX_ROWS), 0)
    neg = -jnp.inf

    grp = sel.reshape(N_GROUPS, GROUP_SIZE, MIX_ROWS)
    mid = lax.broadcasted_iota(jnp.int32, grp.shape, 1)
    m1 = jnp.max(grp, axis=1, keepdims=True)
    i1 = jnp.min(jnp.where(grp == m1, mid, GROUP_SIZE), axis=1, keepdims=True)
    m2 = jnp.max(jnp.where(mid == i1, neg, grp), axis=1, keepdims=True)
    gscore = (m1 + m2).reshape(N_GROUPS, MIX_ROWS)
    gid = lax.broadcasted_iota(jnp.int32, (N_GROUPS, MIX_ROWS), 0)
    gsel = jnp.zeros((N_GROUPS, MIX_ROWS), jnp.bool_)
    for _ in range(TOPK_GROUPS):
        _, gi = _first_argmax(gscore, gid, N_GROUPS)
        hit = gid == gi
        gsel = gsel | hit
        gscore = jnp.where(hit, neg, gscore)
    emask = jnp.broadcast_to(gsel.reshape(N_GROUPS, 1, MIX_ROWS), grp.shape).reshape(N_EXPERTS, MIX_ROWS)
    cand = jnp.where(emask, sel, neg)

    idxs, gates = [], []
    chosen = jnp.zeros((N_EXPERTS, MIX_ROWS), jnp.bool_)
    for _ in range(TOP_K):
        _, ei = _first_argmax(cand, eid, N_EXPERTS)
        hit = eid == ei
        idxs.append(ei)
        gates.append(jnp.sum(jnp.where(hit, s, 0.0), axis=0, keepdims=True))
        chosen = chosen | hit
        cand = jnp.where(hit, neg, cand)
    g = jnp.concatenate(gates, axis=0)
    g = g / jnp.sum(g, axis=0, keepdims=True) * ROUTED_SCALE
    idx_ref[...] = jnp.concatenate(idxs, axis=0)
    gate_ref[...] = g

    onehot = jnp.where(chosen, 1.0, 0.0)
    tr = lax.broadcasted_iota(jnp.int32, (MIX_ROWS, MIX_ROWS), 0)
    tc = lax.broadcasted_iota(jnp.int32, (MIX_ROWS, MIX_ROWS), 1)
    before = jnp.where(tr < tc, 1.0, 0.0).astype(BF16)
    prefix = jnp.dot(onehot.astype(BF16), before, preferred_element_type=F32)
    rankfull = (carry_ref[:, 0:1] + prefix).astype(jnp.int32)
    rank_ref[...] = jnp.concatenate(
        [jnp.sum(jnp.where(eid == ei, rankfull, 0), axis=0, keepdims=True) for ei in idxs], axis=0)
    total = carry_ref[...] + jnp.sum(onehot, axis=1, keepdims=True)
    carry_ref[...] = total
    cnt_ref[...] = total.astype(jnp.int32)

    hg = jnp.dot(x1b, wsg_ref[...], preferred_element_type=F32)
    hu = jnp.dot(x1b, wsu_ref[...], preferred_element_type=F32)
    shared = jnp.dot((hg * jax.nn.sigmoid(hg) * hu).astype(BF16), wsd_ref[...], preferred_element_type=F32)
    ple = (jax.nn.sigmoid(jnp.dot(x1b, wpg_ref[...], preferred_element_type=F32))
           * jnp.dot(p_ref[...].astype(BF16), wpp_ref[...], preferred_element_type=F32))
    _store_row_tiles(base_ref, DEEPNORM_ALPHA * x1 + shared + ple, MIX_ROWS)


def _route(x1, p2d, wr_t, rbias, wsg, wsu, wsd, wpg, wpp):
    T = x1.shape[0] // LANE_CHUNKS
    per_tile = MOE_TILE // MIX_ROWS
    full = lambda a: pl.BlockSpec(a.shape, lambda i: (0, 0))
    tok = pl.BlockSpec((TOP_K, MIX_ROWS), lambda i: (0, i))
    row_tiles = pl.BlockSpec((MIX_ROWS * LANE_CHUNKS, LANES), lambda i: (i, 0))
    return pl.pallas_call(
        _route_kernel,
        grid=(T // MIX_ROWS,),
        in_specs=[row_tiles,
                  pl.BlockSpec((MIX_ROWS, PLE_DIM), lambda i: (i, 0)),
                  full(wr_t), full(rbias), full(wsg), full(wsu), full(wsd), full(wpg), full(wpp)],
        out_specs=[row_tiles, tok, tok, tok,
                   pl.BlockSpec((None, N_EXPERTS, LANES), lambda i: (i // per_tile, 0, 0))],
        out_shape=[jax.ShapeDtypeStruct((T * LANE_CHUNKS, LANES), F32),
                   jax.ShapeDtypeStruct((TOP_K, T), jnp.int32),
                   jax.ShapeDtypeStruct((TOP_K, T), F32),
                   jax.ShapeDtypeStruct((TOP_K, T), jnp.int32),
                   jax.ShapeDtypeStruct((T // MOE_TILE, N_EXPERTS, LANES), jnp.int32)],
        scratch_shapes=[pltpu.VMEM((N_EXPERTS, LANES), F32)],
        compiler_params=_cparams("arbitrary"),
        name="route_shared_ple",
    )(x1, p2d, wr_t, rbias, wsg, wsu, wsd, wpg, wpp)


def _dest_kernel(off_ref, idx_ref, rank_ref, dest_ref):
    idx = idx_ref[...]
    base = pl.program_id(0) * N_EXPERTS

    def body(e, acc):
        return acc + jnp.where(idx == e, off_ref[base + e], 0)

    dest_ref[...] = lax.fori_loop(0, N_EXPERTS, body, rank_ref[...])


def _dest(off, idx, rank):
    T = idx.shape[1]
    spec = pl.BlockSpec((TOP_K, MOE_TILE), lambda i, off: (0, i))
    return pl.pallas_call(
        _dest_kernel,
        grid_spec=pltpu.PrefetchScalarGridSpec(num_scalar_prefetch=1, grid=(T // MOE_TILE,),
                                               in_specs=[spec, spec], out_specs=spec),
        out_shape=jax.ShapeDtypeStruct(idx.shape, jnp.int32),
        compiler_params=_cparams("arbitrary"),
        name="moe_dest",
    )(off, idx, rank)


def _plan_kernel(dest_ref, list_ref):
    first = pl.program_id(1) * PLAN_CODES

    @pl.when(pl.program_id(1) == 0)
    def _():
        def pad(u, c):
            list_ref[MOE_TILE * TOP_K + u] = 0
            return c

        lax.fori_loop(0, LIST_PAD, pad, 0, unroll=8)

    def body(i, c):
        a = i * PLAN_GROUP
        dests = [dest_ref[a + u] for u in range(PLAN_GROUP)]
        for u, d in enumerate(dests):
            list_ref[d] = first + a + u
        return c

    lax.fori_loop(0, PLAN_CODES // PLAN_GROUP, body, 0)


def _plan(dest):
    n_tiles = dest.shape[0] // (MOE_TILE * TOP_K)
    per_tile = MOE_TILE * TOP_K // PLAN_CODES
    return pl.pallas_call(
        _plan_kernel,
        grid=(n_tiles, per_tile),
        in_specs=[pl.BlockSpec((PLAN_CODES,), lambda i, j: (i * per_tile + j,), memory_space=pltpu.SMEM)],
        out_specs=pl.BlockSpec((LIST_LEN,), lambda i, j: (i,), memory_space=pltpu.SMEM),
        out_shape=jax.ShapeDtypeStruct((n_tiles * LIST_LEN,), jnp.int32),
        compiler_params=_cparams("arbitrary", "arbitrary"),
        name="moe_plan",
    )(dest)


def _moe_tile_kernel(cnt_ref, off_ref, list_ref, gate_ref, wg_ref, wu_ref, wd_ref, lw_ref, lb_ref, x_hbm, base_hbm,
                     o_hbm, x_s, acc_s, xg_s, y_s, stage_s, wg_b, wu_b, wd_b):
    tile, e = pl.program_id(0), pl.program_id(1)
    rows_of = lambda ref, r, n: ref.at[pl.ds(pl.multiple_of(r * LANE_CHUNKS, LANE_CHUNKS), n * LANE_CHUNKS), :]
    tile_rows = pl.ds(pl.multiple_of(tile * (MOE_TILE * LANE_CHUNKS), LANE_CHUNKS), MOE_TILE * LANE_CHUNKS)

    pair = tile * N_EXPERTS + e
    n, off = cnt_ref[pair], off_ref[pair]
    tile_of = lambda ref, code: ref.at[pl.ds(pl.multiple_of(code & -LANE_CHUNKS, LANE_CHUNKS), LANE_CHUNKS), :]

    def gather_group(first, dst_row, jb):
        at = first + jb * GATHER_GROUP
        rows = [tile_of(x_s, list_ref[at + u])[...] for u in range(GATHER_GROUP)]
        rows_of(xg_s, dst_row + jb * GATHER_GROUP, GATHER_GROUP)[...] = jnp.concatenate(rows, axis=0)

    def gather_loop(first, dst_row):
        def body(jb, cc):
            gather_group(first, dst_row, jb)
            return cc

        lax.fori_loop(0, MOE_CHUNK // GATHER_GROUP, body, 0)

    def swiglu(src_row):
        xb = _load_row_tiles(xg_s, MOE_CHUNK, src_row * LANE_CHUNKS).astype(BF16)
        hg = jnp.dot(xb, wg_b[...], preferred_element_type=F32)
        hu = jnp.dot(xb, wu_b[...], preferred_element_type=F32)
        act = (hg * jax.nn.sigmoid(hg) * hu).astype(BF16)
        _store_row_tiles(y_s, jnp.dot(act, wd_b[...], preferred_element_type=F32), MOE_CHUNK)

    cur_row = (e % 2) * MOE_CHUNK
    nxt_row = MOE_CHUNK - cur_row

    @pl.when(e == 0)
    def _():
        pltpu.sync_copy(x_hbm.at[tile_rows, :], x_s)
        pltpu.sync_copy(base_hbm.at[tile_rows, :], acc_s)
        gather_loop(off, cur_row)

    wg_b[...] = wg_ref[...].astype(BF16)
    wu_b[...] = wu_ref[...].astype(BF16)
    wd_b[...] = wd_ref[...].astype(BF16)

    def scatter_chunk(c):
        first = off + c * MOE_CHUNK
        m = jnp.minimum(MOE_CHUNK, n - c * MOE_CHUNK)

        def codes_and_gates(j0, live):
            at = first + j0
            codes = [list_ref[at + u] for u in range(live)]
            return codes, [gate_ref[code] for code in codes]

        def scatter_group(j0, codes, gates):
            y = rows_of(y_s, j0, len(codes))[...]
            vals = [tile_of(acc_s, code)[...] + g * y[u * LANE_CHUNKS:(u + 1) * LANE_CHUNKS]
                    for u, (code, g) in enumerate(zip(codes, gates))]
            for code, val in zip(codes, vals):
                tile_of(acc_s, code)[...] = val

        def scatter(jg, cc):
            scatter_group(jg * SCATTER_GROUP, *codes_and_gates(jg * SCATTER_GROUP, SCATTER_GROUP))
            return cc

        lax.fori_loop(0, m // SCATTER_GROUP, scatter, 0)
        for live in range(1, SCATTER_GROUP):
            @pl.when(m % SCATTER_GROUP == live)
            def _(live=live):
                scatter_group(m - live, *codes_and_gates(m - live, live))

    swiglu(cur_row)
    next_off = off_ref[jnp.minimum(pair + 1, pl.num_programs(0) * N_EXPERTS - 1)]
    for jb in range(MOE_CHUNK // GATHER_GROUP):
        gather_group(next_off, nxt_row, jb)
    scatter_chunk(0)

    def extra_chunk(c, carry):
        gather_loop(off + c * MOE_CHUNK, 2 * MOE_CHUNK)
        swiglu(2 * MOE_CHUNK)
        scatter_chunk(c)
        return carry

    lax.fori_loop(1, (n + MOE_CHUNK - 1) // MOE_CHUNK, extra_chunk, 0)

    @pl.when(e == N_EXPERTS - 1)
    def _():
        for c in range(MOE_TILE // MIX_ROWS):
            z = _load_row_tiles(acc_s, MIX_ROWS, c * MIX_ROWS * LANE_CHUNKS)
            stage_s[...] = _layer_norm(z, lw_ref[...], lb_ref[...])
            pltpu.sync_copy(stage_s, o_hbm.at[pl.ds(tile * MOE_TILE + c * MIX_ROWS, MIX_ROWS), :])


def _moe_tiles(x1, base, tok_list, gate, cnt, off, wg, wu, wd, ln_w, ln_b):
    T = x1.shape[0] // LANE_CHUNKS
    w_spec = lambda shape: pl.BlockSpec((None,) + shape, lambda i, e, cnt, off: (e, 0, 0))
    vec = pl.BlockSpec((1, D_MODEL), lambda i, e, cnt, off: (0, 0))
    hbm = pl.BlockSpec(memory_space=pl.ANY)
    tile_rows = MOE_TILE * LANE_CHUNKS
    return pl.pallas_call(
        _moe_tile_kernel,
        grid_spec=pltpu.PrefetchScalarGridSpec(
            num_scalar_prefetch=2,
            grid=(T // MOE_TILE, N_EXPERTS),
            in_specs=[pl.BlockSpec((LIST_LEN,), lambda i, e, cnt, off: (i,), memory_space=pltpu.SMEM),
                      pl.BlockSpec((MOE_TILE * TOP_K,), lambda i, e, cnt, off: (i,), memory_space=pltpu.SMEM),
                      w_spec((D_MODEL, EXPERT_FF)), w_spec((D_MODEL, EXPERT_FF)), w_spec((EXPERT_FF, D_MODEL)),
                      vec, vec, hbm, hbm],
            out_specs=hbm,
            scratch_shapes=[pltpu.VMEM((tile_rows, LANES), F32),
                            pltpu.VMEM((tile_rows, LANES), F32),
                            pltpu.VMEM((3 * MOE_CHUNK * LANE_CHUNKS, LANES), F32),
                            pltpu.VMEM((MOE_CHUNK * LANE_CHUNKS, LANES), F32),
                            pltpu.VMEM((MIX_ROWS, D_MODEL), F32),
                            pltpu.VMEM((D_MODEL, EXPERT_FF), BF16), pltpu.VMEM((D_MODEL, EXPERT_FF), BF16),
                            pltpu.VMEM((EXPERT_FF, D_MODEL), BF16)],
        ),
        out_shape=jax.ShapeDtypeStruct((T, D_MODEL), F32),
        compiler_params=_cparams("arbitrary", "arbitrary"),
        name="moe_tiles_ln2",
    )(cnt, off, tok_list, gate, wg, wu, wd, ln_w, ln_b, x1, base)


def kernel(x, p, w_in, hgrn_lb_logits, hgrn_norm_w, w_branch_att, w_branch_hgrn, w_out, ln1_w, ln1_b, router_w, router_bias, expert_w_gate, expert_w_up, expert_w_down, shared_w_gate, shared_w_up, shared_w_down, ple_gate_w, ple_proj_w, ln2_w, ln2_b):
    B, S, D = x.shape
    T = B * S
    l = 0
    x2d = x.reshape(T, D)
    bf = lambda a: a.astype(BF16)

    ws = _att_weights(w_in[l])
    qkv = [_proj_att(x2d, ws[g], d) for g, d in enumerate(ATT_DILATIONS)]
    y_att = _attention(qkv, B, S)
    u_hg = _proj(x2d, bf(w_in[l][:, 3 * len(ATT_DILATIONS) * ATT_WIDTH:]), 1536)
    y_hg = _hgrn(u_hg, hgrn_lb_logits, hgrn_norm_w[l:l + 1], B, S)
    x1 = _merge(y_att, y_hg, u_hg, x2d, bf(w_branch_att[l]), bf(w_branch_hgrn[l]), bf(w_out[l]),
                ln1_w[l:l + 1], ln1_b[l:l + 1])

    base, idx, gate, rank, counts = _route(
        x1, p[l].reshape(T, PLE_DIM), router_w[l].T, router_bias[l].reshape(N_EXPERTS, 1),
        bf(shared_w_gate[l]), bf(shared_w_up[l]), bf(shared_w_down[l]), bf(ple_gate_w[l]), bf(ple_proj_w[l]))
    cnt = counts[:, :, 0]
    off = jnp.cumsum(cnt, axis=1) - cnt
    cnt, off = cnt.reshape(-1), off.reshape(-1)
    tok_list = _plan(_dest(off, idx, rank).T.reshape(-1))
    gate_list = gate.T.reshape(-1)
    out = _moe_tiles(x1, base, tok_list, gate_list, cnt, off, expert_w_gate[l], expert_w_up[l], expert_w_down[l],
                     ln2_w[l:l + 1], ln2_b[l:l + 1])
    return out.reshape(B, S, D)
```

```python
import functools

import jax
import jax.numpy as jnp
import numpy as np
from jax import lax
from jax.experimental import pallas as pl
from jax.experimental.pallas import tpu as pltpu

F32 = jnp.float32
BF16 = jnp.bfloat16

D_MODEL = 1024
ATT_HEAD_DIM = 64
ATT_HEADS = 8
ATT_DILATIONS = (1, 4, 16)
ATT_BLOCK = 128
ATT_WIDTH = ATT_HEADS * ATT_HEAD_DIM
ATT_TILE = ATT_BLOCK * max(ATT_DILATIONS)
NEG_INF = -1e30

HG_HEADS = 8
HG_DIM = 128
HG_WIDTH = HG_HEADS * HG_DIM
HG_CHUNK = 32
HG_TILE = 256
RMS_EPS = 1e-6

N_EXPERTS = 64
TOP_K = 8
TOP_K_BITS = 3
N_GROUPS = 8
GROUP_SIZE = N_EXPERTS // N_GROUPS
TOPK_GROUPS = 4
EXPERT_FF = 256
ROUTED_SCALE = 2.5
PLE_DIM = 256
LN_EPS = 1e-5
DEPTH = 1
DEEPNORM_ALPHA = (2.0 * DEPTH) ** 0.25

LANES = 128
LANE_CHUNKS = D_MODEL // LANES
PROJ_ROWS = 512
MIX_ROWS = 512
MOE_TILE = 4096
MOE_CHUNK = 576
PLAN_CODES = 4096
PLAN_GROUP = 16
LIST_PAD = 1024
LIST_LEN = MOE_TILE * TOP_K + LIST_PAD
GATHER_GROUP = 8
SCATTER_GROUP = 8
V7X_VMEM_LIMIT = 56 * 1024 * 1024


def _cparams(*sem):
    return pltpu.CompilerParams(dimension_semantics=sem, vmem_limit_bytes=V7X_VMEM_LIMIT)


def _proj_att_kernel(*refs, dil):
    x_refs, w_ref, o_ref = refs[:LANE_CHUNKS], refs[LANE_CHUNKS], refs[LANE_CHUNKS + 1]
    n = PROJ_ROWS // dil

    def rows(ref):
        if dil == 1:
            return ref[...]
        return jnp.concatenate([ref[pl.ds(r, n, stride=dil), :] for r in range(dil)], axis=0)

    xp = jnp.concatenate([rows(ref).astype(BF16) for ref in x_refs], axis=1)
    y = jnp.dot(xp, w_ref[...], preferred_element_type=F32)
    o_ref[...] = y.astype(BF16).reshape(dil, n, 3 * ATT_WIDTH)


def _proj_att(x2d, w, dil):
    T = x2d.shape[0]
    per = ATT_TILE // PROJ_ROWS
    n = PROJ_ROWS // dil
    out = pl.pallas_call(
        functools.partial(_proj_att_kernel, dil=dil),
        grid=(T // PROJ_ROWS,),
        in_specs=[pl.BlockSpec((PROJ_ROWS, LANES), functools.partial(lambda i, c: (i, c), c=c))
                  for c in range(LANE_CHUNKS)]
                 + [pl.BlockSpec((D_MODEL, 3 * ATT_WIDTH), lambda i: (0, 0))],
        out_specs=pl.BlockSpec((None, dil, None, n, 3 * ATT_WIDTH), lambda i: (i // per, 0, i % per, 0, 0)),
        out_shape=jax.ShapeDtypeStruct((T // ATT_TILE, dil, per, n, 3 * ATT_WIDTH), BF16),
        compiler_params=_cparams("parallel"),
        name=f"proj_att_d{dil}",
    )(*([x2d] * LANE_CHUNKS), w)
    return out.reshape(T // ATT_TILE, dil, ATT_TILE // dil, 3 * ATT_WIDTH)


def _att_pair(q2, kp, kc, vp, vc, bias_ref, g, first):
    def head0_lanes(rows, dtype):
        lane = lax.broadcasted_iota(jnp.int32, (rows, 2 * ATT_HEAD_DIM), 1)
        return lane.astype(F32).astype(dtype) < ATT_HEAD_DIM

    lo_q = head0_lanes(ATT_BLOCK, BF16)
    lo_v = head0_lanes(2 * ATT_BLOCK, BF16)
    k2 = jnp.concatenate([kp, kc], axis=0)
    v2 = jnp.concatenate([vp, vc], axis=0)
    zero = jnp.zeros_like(q2)
    ps, ms = [], []
    for hh in range(2):
        qm = jnp.where(lo_q, q2, zero) if hh == 0 else jnp.where(lo_q, zero, q2)
        s = lax.dot_general(qm, k2, (((1,), (1,)), ((), ())), preferred_element_type=F32)
        s = s + bias_ref[g, hh, first]
        m = jnp.max(s, axis=-1, keepdims=True)
        ps.append(jnp.exp(s - m).astype(BF16))
        ms.append(m)
    pcat = jnp.concatenate(ps, axis=1)
    zero_v, one_v = jnp.zeros_like(v2), jnp.ones_like(v2)
    rhs = jnp.concatenate([
        jnp.concatenate([jnp.where(lo_v, v2, zero_v), jnp.where(lo_v, one_v, zero_v)], axis=1),
        jnp.concatenate([jnp.where(lo_v, zero_v, v2), jnp.where(lo_v, zero_v, one_v)], axis=1)], axis=0)
    nd = jnp.dot(pcat, rhs, preferred_element_type=F32)
    m2 = jnp.where(head0_lanes(ATT_BLOCK, F32), ms[0], ms[1])
    return nd[:, :2 * ATT_HEAD_DIM], m2, nd[:, 2 * ATT_HEAD_DIM:]


def _att_kernel(*refs):
    (q0, kc0, vc0, kp0, vp0, q1, kc1, vc1, kp1, vp1, q2, kc2, vc2, kp2, vp2,
     bias_ref, o_ref) = refs[:17]
    ng = len(ATT_DILATIONS)
    num_s, m_s, den_s = refs[17:17 + ng], refs[17 + ng:17 + 2 * ng], refs[17 + 2 * ng:]
    first_tile = (pl.program_id(2) == 0).astype(jnp.int32)
    groups = ((q0, kc0, vc0, kp0, vp0), (q1, kc1, vc1, kp1, vp1), (q2, kc2, vc2, kp2, vp2))
    for g, dil in enumerate(ATT_DILATIONS):
        q_ref, kc_ref, vc_ref, kp_ref, vp_ref = groups[g]
        nb = ATT_TILE // dil // ATT_BLOCK
        for r in range(dil):
            for n in range(nb):
                rows = pl.ds(n * ATT_BLOCK, ATT_BLOCK)
                if n == 0:
                    prev = pl.ds((nb - 1) * ATT_BLOCK, ATT_BLOCK)
                    kp, vp, first = kp_ref[r, prev, :], vp_ref[r, prev, :], first_tile
                else:
                    prev = pl.ds((n - 1) * ATT_BLOCK, ATT_BLOCK)
                    kp, vp, first = kc_ref[r, prev, :], vc_ref[r, prev, :], 0
                num, m, den = _att_pair(q_ref[r, rows, :], kp, kc_ref[r, rows, :], vp, vc_ref[r, rows, :],
                                        bias_ref, g, first)
                if dil == 1:
                    dst = rows
                else:
                    dst = pl.ds(n * ATT_BLOCK * dil + r, ATT_BLOCK, stride=dil)
                num_s[g][dst, :] = num
                m_s[g][dst, :] = m
                den_s[g][dst, :] = den
    m_all = jnp.maximum(jnp.maximum(m_s[0][...], m_s[1][...]), m_s[2][...])
    num = jnp.zeros((ATT_TILE, 2 * ATT_HEAD_DIM), F32)
    den = jnp.zeros((ATT_TILE, 2 * ATT_HEAD_DIM), F32)
    for g in range(ng):
        sc = jnp.exp(m_s[g][...] - m_all)
        num = num + sc * num_s[g][...]
        den = den + sc * den_s[g][...]
    o_ref[...] = (num / den).astype(o_ref.dtype)


def _att_bias_table():
    qi = np.arange(ATT_BLOCK)[:, None]
    ki = np.arange(2 * ATT_BLOCK)[None, :]
    steps = qi + ATT_BLOCK - ki
    valid = (steps >= 0) & (steps <= ATT_BLOCK)
    slopes = np.array([2.0 ** (-8.0 * (h + 1) / ATT_HEADS) for h in range(ATT_HEADS)], np.float32)
    tab = np.empty((len(ATT_DILATIONS), ATT_HEADS, 2, ATT_BLOCK, 2 * ATT_BLOCK), np.float32)
    for g, dil in enumerate(ATT_DILATIONS):
        bias = -slopes[:, None, None] * (steps * dil).astype(np.float32)[None]
        tab[g, :, 0] = np.where(valid[None], bias, NEG_INF)
        tab[g, :, 1] = np.where((valid & (ki >= ATT_BLOCK))[None], bias, NEG_INF)
    return jnp.asarray(tab)


def _attention(qkv, B, S):
    tiles = S // ATT_TILE
    pair = 2 * ATT_HEAD_DIM
    npair = ATT_WIDTH // pair
    in_specs, args = [], []
    for g, dil in enumerate(ATT_DILATIONS):
        blk = (None, dil, ATT_TILE // dil, pair)
        cur = lambda b, hp, t, off: (b * tiles + t, 0, 0, off * npair + hp)
        prv = lambda b, hp, t, off: (b * tiles + jnp.maximum(t - 1, 0), 0, 0, off * npair + hp)
        in_specs += [pl.BlockSpec(blk, functools.partial(cur, off=0)),
                     pl.BlockSpec(blk, functools.partial(cur, off=1)),
                     pl.BlockSpec(blk, functools.partial(cur, off=2)),
                     pl.BlockSpec(blk, functools.partial(prv, off=1)),
                     pl.BlockSpec(blk, functools.partial(prv, off=2))]
        args += [qkv[g]] * 5
    in_specs.append(pl.BlockSpec((len(ATT_DILATIONS), 2, 2, ATT_BLOCK, 2 * ATT_BLOCK),
                                 lambda b, hp, t: (0, hp, 0, 0, 0)))
    args.append(_att_bias_table())
    scratch = [pltpu.VMEM((ATT_TILE, pair), F32) for _ in range(3 * len(ATT_DILATIONS))]
    return pl.pallas_call(
        _att_kernel,
        grid=(B, npair, tiles),
        in_specs=in_specs,
        out_specs=pl.BlockSpec((ATT_TILE, pair), lambda b, hp, t: (b * tiles + t, hp)),
        out_shape=jax.ShapeDtypeStruct((B * S, ATT_WIDTH), BF16),
        scratch_shapes=scratch,
        compiler_params=_cparams("parallel", "parallel", "arbitrary"),
        name="dilated_attention",
    )(*args)


def _att_weights(w_in_l):
    out = []
    width = len(ATT_DILATIONS) * ATT_WIDTH
    for g in range(len(ATT_DILATIONS)):
        cols = [w_in_l[:, part * width + g * ATT_WIDTH: part * width + (g + 1) * ATT_WIDTH] for part in range(3)]
        cols[0] = cols[0] * (ATT_HEAD_DIM ** -0.5)
        out.append(jnp.concatenate(cols, axis=1).astype(BF16))
    return out


def _proj_kernel(x_ref, w_ref, o_ref, *, col_tile):
    xb = x_ref[...].astype(BF16)
    for c in range(w_ref.shape[1] // col_tile):
        cols = slice(c * col_tile, (c + 1) * col_tile)
        o_ref[:, cols] = jnp.dot(xb, w_ref[:, cols], preferred_element_type=F32).astype(o_ref.dtype)


def _proj(x2d, w, col_tile):
    T, N = x2d.shape[0], w.shape[1]
    return pl.pallas_call(
        functools.partial(_proj_kernel, col_tile=col_tile),
        grid=(T // PROJ_ROWS,),
        in_specs=[pl.BlockSpec((PROJ_ROWS, D_MODEL), lambda i: (i, 0)),
                  pl.BlockSpec((D_MODEL, N), lambda i: (0, 0))],
        out_specs=pl.BlockSpec((PROJ_ROWS, N), lambda i: (i, 0)),
        out_shape=jax.ShapeDtypeStruct((T, N), BF16),
        compiler_params=_cparams("parallel"),
        name="proj_hgrn_gates",
    )(x2d, w)


def _split3(v):
    a = v.astype(BF16)
    r = v - a.astype(F32)
    b = r.astype(BF16)
    c = (r - b.astype(F32)).astype(BF16)
    return a, b, c


def _hgrn_kernel(q_ref, f_ref, i_ref, g_ref, lbl_ref, gain_ref, o_ref, state_ref):
    @pl.when(pl.program_id(1) == 0)
    def _():
        state_ref[...] = jnp.zeros_like(state_ref)

    lbl = lbl_ref[...]
    e = jnp.exp(lbl - jnp.max(lbl, axis=0, keepdims=True))
    lb = e[0:1] / jnp.sum(e, axis=0, keepdims=True)
    forget = lb + (1.0 - lb) * jax.nn.sigmoid(f_ref[...].astype(F32))
    log_f = jnp.log(forget)
    key = 1.0 - forget

    row = lax.broadcasted_iota(jnp.int32, (HG_TILE, HG_TILE), 0)
    col = lax.broadcasted_iota(jnp.int32, (HG_TILE, HG_TILE), 1)
    causal = (row >= col) & ((row // HG_CHUNK) == (col // HG_CHUNK))
    tri = jnp.where(causal, 1.0, 0.0).astype(BF16)
    b = sum(jnp.dot(tri, t, preferred_element_type=F32) for t in _split3(log_f))
    eb = jnp.exp(b)
    q_dec = (q_ref[...].astype(F32) * eb).astype(BF16)
    k_inv = key * jnp.exp(-b)
    xi = i_ref[...].astype(F32)
    val = (xi * jax.nn.sigmoid(xi)).astype(BF16)
    k_inv_b = k_inv.astype(BF16)

    n_chunks = HG_TILE // HG_CHUNK
    last_rows = [eb[(c + 1) * HG_CHUNK - 1:(c + 1) * HG_CHUNK, :] for c in range(n_chunks)]
    dec_rows = jnp.concatenate([jnp.broadcast_to(r, (HG_CHUNK, HG_WIDTH)) for r in last_rows], axis=0)
    k_end = (k_inv * dec_rows).astype(BF16)
    chunk_of_row = (lax.broadcasted_iota(jnp.int32, (HG_TILE, HG_DIM), 0) // HG_CHUNK).astype(F32).astype(BF16)
    in_chunk = [chunk_of_row == c for c in range(n_chunks)]
    zero = jnp.zeros((HG_TILE, HG_DIM), BF16)

    def per_chunk_columns(t):
        return jnp.concatenate([jnp.where(m, t, zero) for m in in_chunk], axis=1)

    outs = []
    for h in range(HG_HEADS):
        cols = slice(h * HG_DIM, (h + 1) * HG_DIM)
        qd, ki, vv = q_dec[:, cols], k_inv_b[:, cols], val[:, cols]
        a = lax.dot_general(qd, ki, (((1,), (1,)), ((), ())), preferred_element_type=F32)
        a = jnp.where(causal, a, 0.0).astype(BF16)
        o_intra = jnp.dot(a, vv, preferred_element_type=F32)
        upd = lax.dot_general(vv, per_chunk_columns(k_end[:, cols]), (((0,), (0,)), ((), ())),
                              preferred_element_type=F32)
        st = state_ref[h]
        entering = []
        for c in range(n_chunks):
            entering.append(st.astype(BF16))
            st = st * last_rows[c][:, cols] + upd[:, c * HG_DIM:(c + 1) * HG_DIM]
        state_ref[h] = st
        o_inter = lax.dot_general(per_chunk_columns(qd), jnp.concatenate(entering, axis=1),
                                  (((1,), (1,)), ((), ())), preferred_element_type=F32)
        o = o_intra + o_inter
        o = o * lax.rsqrt(jnp.mean(jnp.square(o), axis=-1, keepdims=True) + RMS_EPS)
        outs.append(o)
    o = jnp.concatenate(outs, axis=1) * gain_ref[...]
    gg = g_ref[...].astype(F32)
    o_ref[...] = (o * (gg * jax.nn.sigmoid(gg))).astype(o_ref.dtype)


def _hgrn(u_hg, lb_logits, gain, B, S):
    tiles = S // HG_TILE
    col = lambda j: pl.BlockSpec((HG_TILE, HG_WIDTH), functools.partial(lambda b, t, j: (b * tiles + t, j), j=j))
    return pl.pallas_call(
        _hgrn_kernel,
        grid=(B, tiles),
        in_specs=[col(0), col(1), col(2), col(3),
                  pl.BlockSpec((2, HG_WIDTH), lambda b, t: (0, 0)),
                  pl.BlockSpec((1, HG_WIDTH), lambda b, t: (0, 0))],
        out_specs=pl.BlockSpec((HG_TILE, HG_WIDTH), lambda b, t: (b * tiles + t, 0)),
        out_shape=jax.ShapeDtypeStruct((B * S, HG_WIDTH), BF16),
        scratch_shapes=[pltpu.VMEM((HG_HEADS, HG_DIM, HG_DIM), F32)],
        compiler_params=_cparams("parallel", "arbitrary"),
        name="hgrn2",
    )(u_hg, u_hg, u_hg, u_hg, lb_logits, gain)


def _load_row_tiles(ref, n, start=0):
    return jnp.concatenate([ref[pl.ds(start + c, n, stride=LANE_CHUNKS), :] for c in range(LANE_CHUNKS)], axis=1)


def _store_row_tiles(ref, val, n):
    for c in range(LANE_CHUNKS):
        ref[pl.ds(c, n, stride=LANE_CHUNKS), :] = val[:, c * LANES:(c + 1) * LANES]


def _layer_norm(z, w, b):
    mu = jnp.mean(z, axis=-1, keepdims=True)
    zc = z - mu
    var = jnp.mean(jnp.square(zc), axis=-1, keepdims=True)
    return zc * lax.rsqrt(var + LN_EPS) * w + b


def _merge_kernel(ya_ref, yh_ref, ga_ref, gh_ref, x_ref, wa_ref, wh_ref, wo_ref, lw_ref, lb_ref, o_ref):
    ma = jnp.dot(ya_ref[...], wa_ref[...], preferred_element_type=F32)
    mh = jnp.dot(yh_ref[...], wh_ref[...], preferred_element_type=F32)
    merged = (jax.nn.sigmoid(ga_ref[...].astype(F32)) * ma + jax.nn.sigmoid(gh_ref[...].astype(F32)) * mh)
    z = DEEPNORM_ALPHA * x_ref[...] + jnp.dot(merged.astype(BF16), wo_ref[...], preferred_element_type=F32)
    _store_row_tiles(o_ref, _layer_norm(z, lw_ref[...], lb_ref[...]), MIX_ROWS)


def _merge(y_att, y_hg, u_hg, x2d, w_a, w_h, w_o, ln_w, ln_b):
    T = x2d.shape[0]
    rows = lambda width, j=0: pl.BlockSpec((MIX_ROWS, width), functools.partial(lambda i, j: (i, j), j=j))
    full = lambda a: pl.BlockSpec(a.shape, lambda i: (0, 0))
    return pl.pallas_call(
        _merge_kernel,
        grid=(T // MIX_ROWS,),
        in_specs=[rows(ATT_WIDTH), rows(HG_WIDTH), rows(D_MODEL, 4), rows(D_MODEL, 5), rows(D_MODEL),
                  full(w_a), full(w_h), full(w_o), full(ln_w), full(ln_b)],
        out_specs=pl.BlockSpec((MIX_ROWS * LANE_CHUNKS, LANES), lambda i: (i, 0)),
        out_shape=jax.ShapeDtypeStruct((T * LANE_CHUNKS, LANES), F32),
        compiler_params=_cparams("parallel"),
        name="merge_ln1",
    )(y_att, y_hg, u_hg, u_hg, x2d, w_a, w_h, w_o, ln_w, ln_b)


def _first_argmax(v, ids, n):
    mx = jnp.max(v, axis=0, keepdims=True)
    return mx, jnp.min(jnp.where(v == mx, ids, n), axis=0, keepdims=True)


def _route_kernel(x1_ref, p_ref, wrt_ref, rb_ref, wsg_ref, wsu_ref, wsd_ref, wpg_ref, wpp_ref,
                  base_ref, idx_ref, gate_ref, rank_ref, cnt_ref, carry_ref):
    @pl.when(pl.program_id(0) % (MOE_TILE // MIX_ROWS) == 0)
    def _():
        carry_ref[...] = jnp.zeros_like(carry_ref)

    x1 = _load_row_tiles(x1_ref, MIX_ROWS)
    x1b = x1.astype(BF16)
    logits = lax.dot_general(wrt_ref[...], x1, (((1,), (1,)), ((), ())), preferred_element_type=F32,
                             precision=lax.Precision.HIGHEST)
    s = jax.nn.sigmoid(logits)
    sel = s + rb_ref[...]
    eid = lax.broadcasted_iota(jnp.int32, (N_EXPERTS, MIX_ROWS), 0)
    neg = -jnp.inf

    grp = sel.reshape(N_GROUPS, GROUP_SIZE, MIX_ROWS)
    mid = lax.broadcasted_iota(jnp.int32, grp.shape, 1)
    m1 = jnp.max(grp, axis=1, keepdims=True)
    i1 = jnp.min(jnp.where(grp == m1, mid, GROUP_SIZE), axis=1, keepdims=True)
    m2 = jnp.max(jnp.where(mid == i1, neg, grp), axis=1, keepdims=True)
    gscore = (m1 + m2).reshape(N_GROUPS, MIX_ROWS)
    gid = lax.broadcasted_iota(jnp.int32, (N_GROUPS, MIX_ROWS), 0)
    gsel = jnp.zeros((N_GROUPS, MIX_ROWS), jnp.bool_)
    for _ in range(TOPK_GROUPS):
        _, gi = _first_argmax(gscore, gid, N_GROUPS)
        hit = gid == gi
        gsel = gsel | hit
        gscore = jnp.where(hit, neg, gscore)
    emask = jnp.broadcast_to(gsel.reshape(N_GROUPS, 1, MIX_ROWS), grp.shape).reshape(N_EXPERTS, MIX_ROWS)
    cand = jnp.where(emask, sel, neg)

    idxs, gates = [], []
    chosen = jnp.zeros((N_EXPERTS, MIX_ROWS), jnp.bool_)
    for _ in range(TOP_K):
        _, ei = _first_argmax(cand, eid, N_EXPERTS)
        hit = eid == ei
        idxs.append(ei)
        gates.append(jnp.sum(jnp.where(hit, s, 0.0), axis=0, keepdims=True))
        chosen = chosen | hit
        cand = jnp.where(hit, neg, cand)
    g = jnp.concatenate(gates, axis=0)
    g = g / jnp.sum(g, axis=0, keepdims=True) * ROUTED_SCALE
    idx_ref[...] = jnp.concatenate(idxs, axis=0)
    gate_ref[...] = g

    onehot = jnp.where(chosen, 1.0, 0.0)
    tr = lax.broadcasted_iota(jnp.int32, (MIX_ROWS, MIX_ROWS), 0)
    tc = lax.broadcasted_iota(jnp.int32, (MIX_ROWS, MIX_ROWS), 1)
    before = jnp.where(tr < tc, 1.0, 0.0).astype(BF16)
    prefix = jnp.dot(onehot.astype(BF16), before, preferred_element_type=F32)
    rankfull = (carry_ref[:, 0:1] + prefix).astype(jnp.int32)
    rank_ref[...] = jnp.concatenate(
        [jnp.sum(jnp.where(eid == ei, rankfull, 0), axis=0, keepdims=True) for ei in idxs], axis=0)
    total = carry_ref[...] + jnp.sum(onehot, axis=1, keepdims=True)
    carry_ref[...] = total
    cnt_ref[...] = total.astype(jnp.int32)

    hg = jnp.dot(x1b, wsg_ref[...], preferred_element_type=F32)
    hu = jnp.dot(x1b, wsu_ref[...], preferred_element_type=F32)
    shared = jnp.dot((hg * jax.nn.sigmoid(hg) * hu).astype(BF16), wsd_ref[...], preferred_element_type=F32)
    ple = (jax.nn.sigmoid(jnp.dot(x1b, wpg_ref[...], preferred_element_type=F32))
           * jnp.dot(p_ref[...].astype(BF16), wpp_ref[...], preferred_element_type=F32))
    _store_row_tiles(base_ref, DEEPNORM_ALPHA * x1 + shared + ple, MIX_ROWS)


def _route(x1, p2d, wr_t, rbias, wsg, wsu, wsd, wpg, wpp):
    T = x1.shape[0] // LANE_CHUNKS
    per_tile = MOE_TILE // MIX_ROWS
    full = lambda a: pl.BlockSpec(a.shape, lambda i: (0, 0))
    tok = pl.BlockSpec((TOP_K, MIX_ROWS), lambda i: (0, i))
    row_tiles = pl.BlockSpec((MIX_ROWS * LANE_CHUNKS, LANES), lambda i: (i, 0))
    return pl.pallas_call(
        _route_kernel,
        grid=(T // MIX_ROWS,),
        in_specs=[row_tiles,
                  pl.BlockSpec((MIX_ROWS, PLE_DIM), lambda i: (i, 0)),
                  full(wr_t), full(rbias), full(wsg), full(wsu), full(wsd), full(wpg), full(wpp)],
        out_specs=[row_tiles, tok, tok, tok,
                   pl.BlockSpec((None, N_EXPERTS, LANES), lambda i: (i // per_tile, 0, 0))],
        out_shape=[jax.ShapeDtypeStruct((T * LANE_CHUNKS, LANES), F32),
                   jax.ShapeDtypeStruct((TOP_K, T), jnp.int32),
                   jax.ShapeDtypeStruct((TOP_K, T), F32),
                   jax.ShapeDtypeStruct((TOP_K, T), jnp.int32),
                   jax.ShapeDtypeStruct((T // MOE_TILE, N_EXPERTS, LANES), jnp.int32)],
        scratch_shapes=[pltpu.VMEM((N_EXPERTS, LANES), F32)],
        compiler_params=_cparams("arbitrary"),
        name="route_shared_ple",
    )(x1, p2d, wr_t, rbias, wsg, wsu, wsd, wpg, wpp)


def _dest_kernel(off_ref, idx_ref, rank_ref, dest_ref):
    idx = idx_ref[...]
    base = pl.program_id(0) * N_EXPERTS

    def body(e, acc):
        return acc + jnp.where(idx == e, off_ref[base + e], 0)

    dest_ref[...] = lax.fori_loop(0, N_EXPERTS, body, rank_ref[...])


def _dest(off, idx, rank):
    T = idx.shape[1]
    spec = pl.BlockSpec((TOP_K, MOE_TILE), lambda i, off: (0, i))
    return pl.pallas_call(
        _dest_kernel,
        grid_spec=pltpu.PrefetchScalarGridSpec(num_scalar_prefetch=1, grid=(T // MOE_TILE,),
                                               in_specs=[spec, spec], out_specs=spec),
        out_shape=jax.ShapeDtypeStruct(idx.shape, jnp.int32),
        compiler_params=_cparams("arbitrary"),
        name="moe_dest",
    )(off, idx, rank)


def _plan_kernel(dest_ref, list_ref):
    first = pl.program_id(1) * PLAN_CODES

    @pl.when(pl.program_id(1) == 0)
    def _():
        def pad(u, c):
            list_ref[MOE_TILE * TOP_K + u] = 0
            return c

        lax.fori_loop(0, LIST_PAD, pad, 0, unroll=8)

    def body(i, c):
        a = i * PLAN_GROUP
        dests = [dest_ref[a + u] for u in range(PLAN_GROUP)]
        for u, d in enumerate(dests):
            list_ref[d] = first + a + u
        return c

    lax.fori_loop(0, PLAN_CODES // PLAN_GROUP, body, 0)


def _plan(dest):
    n_tiles = dest.shape[0] // (MOE_TILE * TOP_K)
    per_tile = MOE_TILE * TOP_K // PLAN_CODES
    return pl.pallas_call(
        _plan_kernel,
        grid=(n_tiles, per_tile),
        in_specs=[pl.BlockSpec((PLAN_CODES,), lambda i, j: (i * per_tile + j,), memory_space=pltpu.SMEM)],
        out_specs=pl.BlockSpec((LIST_LEN,), lambda i, j: (i,), memory_space=pltpu.SMEM),
        out_shape=jax.ShapeDtypeStruct((n_tiles * LIST_LEN,), jnp.int32),
        compiler_params=_cparams("arbitrary", "arbitrary"),
        name="moe_plan",
    )(dest)


def _moe_tile_kernel(cnt_ref, off_ref, list_ref, gate_ref, wg_ref, wu_ref, wd_ref, lw_ref, lb_ref, x_hbm, base_hbm,
                     o_hbm, x_s, acc_s, xg_a, xg_b, xg_c, y_a, y_b, y_c, stage_s):
    tile, e = pl.program_id(0), pl.program_id(1)
    rows_of = lambda ref, r, n: ref.at[pl.ds(pl.multiple_of(r * LANE_CHUNKS, LANE_CHUNKS), n * LANE_CHUNKS), :]
    tile_rows = pl.ds(pl.multiple_of(tile * (MOE_TILE * LANE_CHUNKS), LANE_CHUNKS), MOE_TILE * LANE_CHUNKS)

    pair = tile * N_EXPERTS + e
    last_pair = pl.num_programs(0) * N_EXPERTS - 1
    n, off = cnt_ref[pair], off_ref[pair]
    first_row = lambda code: pl.multiple_of(code & -LANE_CHUNKS, LANE_CHUNKS)
    tile_at = lambda ref, r: ref.at[pl.ds(r, LANE_CHUNKS), :]

    def gather_group(xg, first, jb):
        at = first + jb * GATHER_GROUP
        rows = [tile_at(x_s, first_row(list_ref[at + u]))[...] for u in range(GATHER_GROUP)]
        rows_of(xg, jb * GATHER_GROUP, GATHER_GROUP)[...] = jnp.concatenate(rows, axis=0)

    def gather_loop(xg, first):
        def body(jb, cc):
            gather_group(xg, first, jb)
            return cc

        lax.fori_loop(0, MOE_CHUNK // GATHER_GROUP, body, 0)

    def swiglu(xg, y):
        xb = _load_row_tiles(xg, MOE_CHUNK).astype(BF16)
        hg = jnp.dot(xb, wg_ref[...], preferred_element_type=F32)
        hu = jnp.dot(xb, wu_ref[...], preferred_element_type=F32)
        act = (hg * jax.nn.sigmoid(hg) * hu).astype(BF16)
        _store_row_tiles(y, jnp.dot(act, wd_ref[...], preferred_element_type=F32), MOE_CHUNK)

    def scatter_group(y, first, j0, live):
        codes = [list_ref[first + j0 + u] for u in range(live)]
        gates = [gate_ref[code] for code in codes]
        dsts = [first_row(code) for code in codes]
        yv = rows_of(y, j0, live)[...]
        vals = [tile_at(acc_s, d)[...] + g * yv[u * LANE_CHUNKS:(u + 1) * LANE_CHUNKS]
                for u, (d, g) in enumerate(zip(dsts, gates))]
        for d, val in reversed(list(zip(dsts, vals))):
            tile_at(acc_s, d)[...] = val

    def scatter_loop(y, first, m):
        def body(jg, cc):
            scatter_group(y, first, jg * SCATTER_GROUP, SCATTER_GROUP)
            return cc

        lax.fori_loop(0, m // SCATTER_GROUP, body, 0)
        for live in range(1, SCATTER_GROUP):
            @pl.when(m % SCATTER_GROUP == live)
            def _(live=live):
                scatter_group(y, first, m - live, live)

    @pl.when(e == 0)
    def _():
        pltpu.sync_copy(x_hbm.at[tile_rows, :], x_s)
        pltpu.sync_copy(base_hbm.at[tile_rows, :], acc_s)
        y_b[...] = jnp.zeros_like(y_b)
        gather_loop(xg_a, off)

    prev_off = off_ref[jnp.maximum(pair - 1, 0)]
    next_off = off_ref[jnp.minimum(pair + 1, last_pair)]
    live_row = lax.broadcasted_iota(jnp.int32, (MOE_CHUNK, 2 * LANES), 0) < n

    def pipeline_step(xg_cur, y_cur, xg_nxt, y_prv):
        gathers = [functools.partial(gather_group, xg_nxt, next_off, jb) for jb in range(MOE_CHUNK // GATHER_GROUP)]
        scatters = [functools.partial(scatter_group, y_prv, prev_off, jg * SCATTER_GROUP, SCATTER_GROUP)
                    for jg in range(MOE_CHUNK // SCATTER_GROUP)]
        side = [s for both in zip(gathers, scatters) for s in both]
        pieces = 2 + LANE_CHUNKS // 2
        per_piece = -(-len(side) // pieces)

        def side_work(i):
            for s in side[i * per_piece:(i + 1) * per_piece]:
                s()

        xb = _load_row_tiles(xg_cur, MOE_CHUNK).astype(BF16)
        hg = jnp.dot(xb, wg_ref[...], preferred_element_type=F32)
        side_work(0)
        hu = jnp.dot(xb, wu_ref[...], preferred_element_type=F32)
        side_work(1)
        act = (hg * jax.nn.sigmoid(hg) * hu).astype(BF16)
        for q in range(LANE_CHUNKS // 2):
            out = jnp.dot(act, wd_ref[:, q * 2 * LANES:(q + 1) * 2 * LANES], preferred_element_type=F32)
            out = jnp.where(live_row, out, 0.0)
            for c in range(2):
                y_cur[pl.ds(2 * q + c, MOE_CHUNK, stride=LANE_CHUNKS), :] = out[:, c * LANES:(c + 1) * LANES]
            side_work(2 + q)

    @pl.when(e % 2 == 0)
    def _():
        pipeline_step(xg_a, y_a, xg_b, y_b)

    @pl.when(e % 2 == 1)
    def _():
        pipeline_step(xg_b, y_b, xg_a, y_a)

    def extra_chunk(c, carry):
        first = off + c * MOE_CHUNK
        gather_loop(xg_c, first)
        swiglu(xg_c, y_c)
        scatter_loop(y_c, first, jnp.minimum(MOE_CHUNK, n - c * MOE_CHUNK))
        return carry

    lax.fori_loop(1, (n + MOE_CHUNK - 1) // MOE_CHUNK, extra_chunk, 0)

    @pl.when(e == N_EXPERTS - 1)
    def _():
        scatter_loop(y_b, off, jnp.minimum(MOE_CHUNK, n))
        for c in range(MOE_TILE // MIX_ROWS):
            z = _load_row_tiles(acc_s, MIX_ROWS, c * MIX_ROWS * LANE_CHUNKS)
            stage_s[...] = _layer_norm(z, lw_ref[...], lb_ref[...])
            pltpu.sync_copy(stage_s, o_hbm.at[pl.ds(tile * MOE_TILE + c * MIX_ROWS, MIX_ROWS), :])


def _moe_tiles(x1, base, tok_list, gate, cnt, off, wg, wu, wd, ln_w, ln_b):
    T = x1.shape[0] // LANE_CHUNKS
    w_spec = lambda shape: pl.BlockSpec((None,) + shape, lambda i, e, cnt, off: (e, 0, 0))
    vec = pl.BlockSpec((1, D_MODEL), lambda i, e, cnt, off: (0, 0))
    hbm = pl.BlockSpec(memory_space=pl.ANY)
    tile_rows = MOE_TILE * LANE_CHUNKS
    return pl.pallas_call(
        _moe_tile_kernel,
        grid_spec=pltpu.PrefetchScalarGridSpec(
            num_scalar_prefetch=2,
            grid=(T // MOE_TILE, N_EXPERTS),
            in_specs=[pl.BlockSpec((LIST_LEN,), lambda i, e, cnt, off: (i,), memory_space=pltpu.SMEM),
                      pl.BlockSpec((MOE_TILE * TOP_K,), lambda i, e, cnt, off: (i,), memory_space=pltpu.SMEM),
                      w_spec((D_MODEL, EXPERT_FF)), w_spec((D_MODEL, EXPERT_FF)), w_spec((EXPERT_FF, D_MODEL)),
                      vec, vec, hbm, hbm],
            out_specs=hbm,
            scratch_shapes=[pltpu.VMEM((tile_rows, LANES), F32),
                            pltpu.VMEM((tile_rows, LANES), F32)]
                           + [pltpu.VMEM((MOE_CHUNK * LANE_CHUNKS, LANES), F32)] * 6
                           + [pltpu.VMEM((MIX_ROWS, D_MODEL), F32)],
        ),
        out_shape=jax.ShapeDtypeStruct((T, D_MODEL), F32),
        compiler_params=_cparams("arbitrary", "arbitrary"),
        name="moe_tiles_ln2",
    )(cnt, off, tok_list, gate, wg, wu, wd, ln_w, ln_b, x1, base)


def kernel(x, p, w_in, hgrn_lb_logits, hgrn_norm_w, w_branch_att, w_branch_hgrn, w_out, ln1_w, ln1_b, router_w, router_bias, expert_w_gate, expert_w_up, expert_w_down, shared_w_gate, shared_w_up, shared_w_down, ple_gate_w, ple_proj_w, ln2_w, ln2_b):
    B, S, D = x.shape
    T = B * S
    l = 0
    x2d = x.reshape(T, D)
    bf = lambda a: a.astype(BF16)

    ws = _att_weights(w_in[l])
    qkv = [_proj_att(x2d, ws[g], d) for g, d in enumerate(ATT_DILATIONS)]
    y_att = _attention(qkv, B, S)
    u_hg = _proj(x2d, bf(w_in[l][:, 3 * len(ATT_DILATIONS) * ATT_WIDTH:]), 1536)
    y_hg = _hgrn(u_hg, hgrn_lb_logits, hgrn_norm_w[l:l + 1], B, S)
    x1 = _merge(y_att, y_hg, u_hg, x2d, bf(w_branch_att[l]), bf(w_branch_hgrn[l]), bf(w_out[l]),
                ln1_w[l:l + 1], ln1_b[l:l + 1])

    base, idx, gate, rank, counts = _route(
        x1, p[l].reshape(T, PLE_DIM), router_w[l].T, router_bias[l].reshape(N_EXPERTS, 1),
        bf(shared_w_gate[l]), bf(shared_w_up[l]), bf(shared_w_down[l]), bf(ple_gate_w[l]), bf(ple_proj_w[l]))
    cnt = counts[:, :, 0]
    off = jnp.cumsum(cnt, axis=1) - cnt
    cnt, off = cnt.reshape(-1), off.reshape(-1)
    tok_list = _plan(_dest(off, idx, rank).T.reshape(-1))
    gate_list = gate.T.reshape(-1)
    out = _moe_tiles(x1, base, tok_list, gate_list, cnt, off, bf(expert_w_gate[l]), bf(expert_w_up[l]),
                     bf(expert_w_down[l]), ln2_w[l:l + 1], ln2_b[l:l + 1])
    return out.reshape(B, S, D)
```

```python
import functools

import jax
import jax.numpy as jnp
import numpy as np
from jax import lax
from jax.experimental import pallas as pl
from jax.experimental.pallas import tpu as pltpu

F32 = jnp.float32
BF16 = jnp.bfloat16

D_MODEL = 1024
ATT_HEAD_DIM = 64
ATT_HEADS = 8
ATT_DILATIONS = (1, 4, 16)
ATT_BLOCK = 128
ATT_WIDTH = ATT_HEADS * ATT_HEAD_DIM
ATT_TILE = ATT_BLOCK * max(ATT_DILATIONS)
NEG_INF = -1e30

HG_HEADS = 8
HG_DIM = 128
HG_WIDTH = HG_HEADS * HG_DIM
HG_CHUNK = 32
HG_TILE = 256
RMS_EPS = 1e-6

N_EXPERTS = 64
TOP_K = 8
TOP_K_BITS = 3
N_GROUPS = 8
GROUP_SIZE = N_EXPERTS // N_GROUPS
TOPK_GROUPS = 4
EXPERT_FF = 256
ROUTED_SCALE = 2.5
PLE_DIM = 256
LN_EPS = 1e-5
DEPTH = 1
DEEPNORM_ALPHA = (2.0 * DEPTH) ** 0.25

LANES = 128
LANE_CHUNKS = D_MODEL // LANES
PROJ_ROWS = 512
ATT_PROJ_ROWS = 1024
MIX_ROWS = 512
MOE_TILE = 4096
MOE_CHUNK = 576
PLAN_CODES = MOE_TILE * TOP_K
PLAN_GROUP = 16
LIST_PAD = 1024
LIST_LEN = MOE_TILE * TOP_K + LIST_PAD
GATHER_GROUP = 8
SCATTER_GROUP = 8
V7X_VMEM_LIMIT = 56 * 1024 * 1024


def _cparams(*sem):
    return pltpu.CompilerParams(dimension_semantics=sem, vmem_limit_bytes=V7X_VMEM_LIMIT)


def _proj_att_kernel(*refs, dil):
    x_refs, w_ref, o_ref = refs[:LANE_CHUNKS], refs[LANE_CHUNKS], refs[LANE_CHUNKS + 1]
    n = ATT_PROJ_ROWS // dil

    def rows(ref):
        if dil == 1:
            return ref[...]
        return jnp.concatenate([ref[pl.ds(r, n, stride=dil), :] for r in range(dil)], axis=0)

    xp = jnp.concatenate([rows(ref).astype(BF16) for ref in x_refs], axis=1)
    y = jnp.dot(xp, w_ref[...], preferred_element_type=F32)
    o_ref[...] = y.astype(BF16).reshape(dil, n, 3 * ATT_WIDTH)


def _proj_att(x2d, w, dil):
    T = x2d.shape[0]
    per = ATT_TILE // ATT_PROJ_ROWS
    n = ATT_PROJ_ROWS // dil
    out = pl.pallas_call(
        functools.partial(_proj_att_kernel, dil=dil),
        grid=(T // ATT_PROJ_ROWS,),
        in_specs=[pl.BlockSpec((ATT_PROJ_ROWS, LANES), functools.partial(lambda i, c: (i, c), c=c))
                  for c in range(LANE_CHUNKS)]
                 + [pl.BlockSpec((D_MODEL, 3 * ATT_WIDTH), lambda i: (0, 0))],
        out_specs=pl.BlockSpec((None, dil, None, n, 3 * ATT_WIDTH), lambda i: (i // per, 0, i % per, 0, 0)),
        out_shape=jax.ShapeDtypeStruct((T // ATT_TILE, dil, per, n, 3 * ATT_WIDTH), BF16),
        compiler_params=_cparams("parallel"),
        name=f"proj_att_d{dil}",
    )(*([x2d] * LANE_CHUNKS), w)
    return out.reshape(T // ATT_TILE, dil, ATT_TILE // dil, 3 * ATT_WIDTH)


def _att_pair(q2, kp, kc, vp, vc, bias_ref, g, first):
    def head0_lanes(rows, dtype):
        lane = lax.broadcasted_iota(jnp.int32, (rows, 2 * ATT_HEAD_DIM), 1)
        return lane.astype(F32).astype(dtype) < ATT_HEAD_DIM

    lo_q = head0_lanes(ATT_BLOCK, BF16)
    lo_v = head0_lanes(2 * ATT_BLOCK, BF16)
    k2 = jnp.concatenate([kp, kc], axis=0)
    v2 = jnp.concatenate([vp, vc], axis=0)
    zero = jnp.zeros_like(q2)
    ps, ms = [], []
    for hh in range(2):
        qm = jnp.where(lo_q, q2, zero) if hh == 0 else jnp.where(lo_q, zero, q2)
        s = lax.dot_general(qm, k2, (((1,), (1,)), ((), ())), preferred_element_type=F32)
        s = s + bias_ref[g, hh, first]
        m = jnp.max(s, axis=-1, keepdims=True)
        ps.append(jnp.exp(s - m).astype(BF16))
        ms.append(m)
    pcat = jnp.concatenate(ps, axis=1)
    zero_v, one_v = jnp.zeros_like(v2), jnp.ones_like(v2)
    rhs = jnp.concatenate([
        jnp.concatenate([jnp.where(lo_v, v2, zero_v), jnp.where(lo_v, one_v, zero_v)], axis=1),
        jnp.concatenate([jnp.where(lo_v, zero_v, v2), jnp.where(lo_v, zero_v, one_v)], axis=1)], axis=0)
    nd = jnp.dot(pcat, rhs, preferred_element_type=F32)
    m2 = jnp.where(head0_lanes(ATT_BLOCK, F32), ms[0], ms[1])
    return nd[:, :2 * ATT_HEAD_DIM], m2, nd[:, 2 * ATT_HEAD_DIM:]


def _att_kernel(*refs):
    (q0, kc0, vc0, kp0, vp0, q1, kc1, vc1, kp1, vp1, q2, kc2, vc2, kp2, vp2,
     bias_ref, o_ref) = refs[:17]
    ng = len(ATT_DILATIONS)
    num_s, m_s, den_s = refs[17:17 + ng], refs[17 + ng:17 + 2 * ng], refs[17 + 2 * ng:]
    first_tile = (pl.program_id(2) == 0).astype(jnp.int32)
    groups = ((q0, kc0, vc0, kp0, vp0), (q1, kc1, vc1, kp1, vp1), (q2, kc2, vc2, kp2, vp2))
    for g, dil in enumerate(ATT_DILATIONS):
        q_ref, kc_ref, vc_ref, kp_ref, vp_ref = groups[g]
        nb = ATT_TILE // dil // ATT_BLOCK
        for r in range(dil):
            for n in range(nb):
                rows = pl.ds(n * ATT_BLOCK, ATT_BLOCK)
                if n == 0:
                    prev = pl.ds((nb - 1) * ATT_BLOCK, ATT_BLOCK)
                    kp, vp, first = kp_ref[r, prev, :], vp_ref[r, prev, :], first_tile
                else:
                    prev = pl.ds((n - 1) * ATT_BLOCK, ATT_BLOCK)
                    kp, vp, first = kc_ref[r, prev, :], vc_ref[r, prev, :], 0
                num, m, den = _att_pair(q_ref[r, rows, :], kp, kc_ref[r, rows, :], vp, vc_ref[r, rows, :],
                                        bias_ref, g, first)
                if dil == 1:
                    dst = rows
                else:
                    dst = pl.ds(n * ATT_BLOCK * dil + r, ATT_BLOCK, stride=dil)
                num_s[g][dst, :] = num
                m_s[g][dst, :] = m
                den_s[g][dst, :] = den
    m_all = jnp.maximum(jnp.maximum(m_s[0][...], m_s[1][...]), m_s[2][...])
    num = jnp.zeros((ATT_TILE, 2 * ATT_HEAD_DIM), F32)
    den = jnp.zeros((ATT_TILE, 2 * ATT_HEAD_DIM), F32)
    for g in range(ng):
        sc = jnp.exp(m_s[g][...] - m_all)
        num = num + sc * num_s[g][...]
        den = den + sc * den_s[g][...]
    o_ref[...] = (num / den).astype(o_ref.dtype)


def _att_bias_table():
    qi = np.arange(ATT_BLOCK)[:, None]
    ki = np.arange(2 * ATT_BLOCK)[None, :]
    steps = qi + ATT_BLOCK - ki
    valid = (steps >= 0) & (steps <= ATT_BLOCK)
    slopes = np.array([2.0 ** (-8.0 * (h + 1) / ATT_HEADS) for h in range(ATT_HEADS)], np.float32)
    tab = np.empty((len(ATT_DILATIONS), ATT_HEADS, 2, ATT_BLOCK, 2 * ATT_BLOCK), np.float32)
    for g, dil in enumerate(ATT_DILATIONS):
        bias = -slopes[:, None, None] * (steps * dil).astype(np.float32)[None]
        tab[g, :, 0] = np.where(valid[None], bias, NEG_INF)
        tab[g, :, 1] = np.where((valid & (ki >= ATT_BLOCK))[None], bias, NEG_INF)
    return jnp.asarray(tab)


def _attention(qkv, B, S):
    tiles = S // ATT_TILE
    pair = 2 * ATT_HEAD_DIM
    npair = ATT_WIDTH // pair
    in_specs, args = [], []
    for g, dil in enumerate(ATT_DILATIONS):
        blk = (None, dil, ATT_TILE // dil, pair)
        cur = lambda b, hp, t, off: (b * tiles + t, 0, 0, off * npair + hp)
        prv = lambda b, hp, t, off: (b * tiles + jnp.maximum(t - 1, 0), 0, 0, off * npair + hp)
        in_specs += [pl.BlockSpec(blk, functools.partial(cur, off=0)),
                     pl.BlockSpec(blk, functools.partial(cur, off=1)),
                     pl.BlockSpec(blk, functools.partial(cur, off=2)),
                     pl.BlockSpec(blk, functools.partial(prv, off=1)),
                     pl.BlockSpec(blk, functools.partial(prv, off=2))]
        args += [qkv[g]] * 5
    in_specs.append(pl.BlockSpec((len(ATT_DILATIONS), 2, 2, ATT_BLOCK, 2 * ATT_BLOCK),
                                 lambda b, hp, t: (0, hp, 0, 0, 0)))
    args.append(_att_bias_table())
    scratch = [pltpu.VMEM((ATT_TILE, pair), F32) for _ in range(3 * len(ATT_DILATIONS))]
    return pl.pallas_call(
        _att_kernel,
        grid=(B, npair, tiles),
        in_specs=in_specs,
        out_specs=pl.BlockSpec((ATT_TILE, pair), lambda b, hp, t: (b * tiles + t, hp)),
        out_shape=jax.ShapeDtypeStruct((B * S, ATT_WIDTH), BF16),
        scratch_shapes=scratch,
        compiler_params=_cparams("parallel", "parallel", "arbitrary"),
        name="dilated_attention",
    )(*args)


def _att_weights(w_in_l):
    out = []
    width = len(ATT_DILATIONS) * ATT_WIDTH
    for g in range(len(ATT_DILATIONS)):
        cols = [w_in_l[:, part * width + g * ATT_WIDTH: part * width + (g + 1) * ATT_WIDTH] for part in range(3)]
        cols[0] = cols[0] * (ATT_HEAD_DIM ** -0.5)
        out.append(jnp.concatenate(cols, axis=1).astype(BF16))
    return out


def _proj_kernel(x_ref, w_ref, o_ref, *, col_tile):
    xb = x_ref[...].astype(BF16)
    for c in range(w_ref.shape[1] // col_tile):
        cols = slice(c * col_tile, (c + 1) * col_tile)
        o_ref[:, cols] = jnp.dot(xb, w_ref[:, cols], preferred_element_type=F32).astype(o_ref.dtype)


def _proj(x2d, w, col_tile):
    T, N = x2d.shape[0], w.shape[1]
    return pl.pallas_call(
        functools.partial(_proj_kernel, col_tile=col_tile),
        grid=(T // PROJ_ROWS,),
        in_specs=[pl.BlockSpec((PROJ_ROWS, D_MODEL), lambda i: (i, 0)),
                  pl.BlockSpec((D_MODEL, N), lambda i: (0, 0))],
        out_specs=pl.BlockSpec((PROJ_ROWS, N), lambda i: (i, 0)),
        out_shape=jax.ShapeDtypeStruct((T, N), BF16),
        compiler_params=_cparams("parallel"),
        name="proj_hgrn_gates",
    )(x2d, w)


def _split3(v):
    a = v.astype(BF16)
    r = v - a.astype(F32)
    b = r.astype(BF16)
    c = (r - b.astype(F32)).astype(BF16)
    return a, b, c


def _hgrn_kernel(q_ref, f_ref, i_ref, g_ref, lbl_ref, gain_ref, o_ref, state_ref):
    @pl.when(pl.program_id(1) == 0)
    def _():
        state_ref[...] = jnp.zeros_like(state_ref)

    lbl = lbl_ref[...]
    e = jnp.exp(lbl - jnp.max(lbl, axis=0, keepdims=True))
    lb = e[0:1] / jnp.sum(e, axis=0, keepdims=True)
    forget = lb + (1.0 - lb) * jax.nn.sigmoid(f_ref[...].astype(F32))
    log_f = jnp.log(forget)
    key = 1.0 - forget

    row = lax.broadcasted_iota(jnp.int32, (HG_TILE, HG_TILE), 0)
    col = lax.broadcasted_iota(jnp.int32, (HG_TILE, HG_TILE), 1)
    causal = (row >= col) & ((row // HG_CHUNK) == (col // HG_CHUNK))
    tri = jnp.where(causal, 1.0, 0.0).astype(BF16)
    b = sum(jnp.dot(tri, t, preferred_element_type=F32) for t in _split3(log_f))
    eb = jnp.exp(b)
    q_dec = (q_ref[...].astype(F32) * eb).astype(BF16)
    k_inv = key * jnp.exp(-b)
    xi = i_ref[...].astype(F32)
    val = (xi * jax.nn.sigmoid(xi)).astype(BF16)
    k_inv_b = k_inv.astype(BF16)

    n_chunks = HG_TILE // HG_CHUNK
    last_rows = [eb[(c + 1) * HG_CHUNK - 1:(c + 1) * HG_CHUNK, :] for c in range(n_chunks)]
    dec_rows = jnp.concatenate([jnp.broadcast_to(r, (HG_CHUNK, HG_WIDTH)) for r in last_rows], axis=0)
    k_end = (k_inv * dec_rows).astype(BF16)
    chunk_of_row = (lax.broadcasted_iota(jnp.int32, (HG_TILE, HG_DIM), 0) // HG_CHUNK).astype(F32).astype(BF16)
    in_chunk = [chunk_of_row == c for c in range(n_chunks)]
    zero = jnp.zeros((HG_TILE, HG_DIM), BF16)

    def per_chunk_columns(t):
        return jnp.concatenate([jnp.where(m, t, zero) for m in in_chunk], axis=1)

    outs = []
    for h in range(HG_HEADS):
        cols = slice(h * HG_DIM, (h + 1) * HG_DIM)
        qd, ki, vv = q_dec[:, cols], k_inv_b[:, cols], val[:, cols]
        a = lax.dot_general(qd, ki, (((1,), (1,)), ((), ())), preferred_element_type=F32)
        a = jnp.where(causal, a, 0.0).astype(BF16)
        o_intra = jnp.dot(a, vv, preferred_element_type=F32)
        upd = lax.dot_general(vv, per_chunk_columns(k_end[:, cols]), (((0,), (0,)), ((), ())),
                              preferred_element_type=F32)
        st = state_ref[h]
        entering = []
        for c in range(n_chunks):
            entering.append(st.astype(BF16))
            st = st * last_rows[c][:, cols] + upd[:, c * HG_DIM:(c + 1) * HG_DIM]
        state_ref[h] = st
        o_inter = lax.dot_general(per_chunk_columns(qd), jnp.concatenate(entering, axis=1),
                                  (((1,), (1,)), ((), ())), preferred_element_type=F32)
        o = o_intra + o_inter
        o = o * lax.rsqrt(jnp.mean(jnp.square(o), axis=-1, keepdims=True) + RMS_EPS)
        outs.append(o)
    o = jnp.concatenate(outs, axis=1) * gain_ref[...]
    gg = g_ref[...].astype(F32)
    o_ref[...] = (o * (gg * jax.nn.sigmoid(gg))).astype(o_ref.dtype)


def _hgrn(u_hg, lb_logits, gain, B, S):
    tiles = S // HG_TILE
    col = lambda j: pl.BlockSpec((HG_TILE, HG_WIDTH), functools.partial(lambda b, t, j: (b * tiles + t, j), j=j))
    return pl.pallas_call(
        _hgrn_kernel,
        grid=(B, tiles),
        in_specs=[col(0), col(1), col(2), col(3),
                  pl.BlockSpec((2, HG_WIDTH), lambda b, t: (0, 0)),
                  pl.BlockSpec((1, HG_WIDTH), lambda b, t: (0, 0))],
        out_specs=pl.BlockSpec((HG_TILE, HG_WIDTH), lambda b, t: (b * tiles + t, 0)),
        out_shape=jax.ShapeDtypeStruct((B * S, HG_WIDTH), BF16),
        scratch_shapes=[pltpu.VMEM((HG_HEADS, HG_DIM, HG_DIM), F32)],
        compiler_params=_cparams("parallel", "arbitrary"),
        name="hgrn2",
    )(u_hg, u_hg, u_hg, u_hg, lb_logits, gain)


def _load_row_tiles(ref, n, start=0):
    return jnp.concatenate([ref[pl.ds(start + c, n, stride=LANE_CHUNKS), :] for c in range(LANE_CHUNKS)], axis=1)


def _store_row_tiles(ref, val, n):
    for c in range(LANE_CHUNKS):
        ref[pl.ds(c, n, stride=LANE_CHUNKS), :] = val[:, c * LANES:(c + 1) * LANES]


def _layer_norm(z, w, b):
    mu = jnp.mean(z, axis=-1, keepdims=True)
    zc = z - mu
    var = jnp.mean(jnp.square(zc), axis=-1, keepdims=True)
    return zc * lax.rsqrt(var + LN_EPS) * w + b


def _merge_kernel(ya_ref, yh_ref, ga_ref, gh_ref, x_ref, wa_ref, wh_ref, wo_ref, lw_ref, lb_ref, o_ref):
    ma = jnp.dot(ya_ref[...], wa_ref[...], preferred_element_type=F32)
    mh = jnp.dot(yh_ref[...], wh_ref[...], preferred_element_type=F32)
    merged = (jax.nn.sigmoid(ga_ref[...].astype(F32)) * ma + jax.nn.sigmoid(gh_ref[...].astype(F32)) * mh)
    z = DEEPNORM_ALPHA * x_ref[...] + jnp.dot(merged.astype(BF16), wo_ref[...], preferred_element_type=F32)
    _store_row_tiles(o_ref, _layer_norm(z, lw_ref[...], lb_ref[...]), MIX_ROWS)


def _merge(y_att, y_hg, u_hg, x2d, w_a, w_h, w_o, ln_w, ln_b):
    T = x2d.shape[0]
    rows = lambda width, j=0: pl.BlockSpec((MIX_ROWS, width), functools.partial(lambda i, j: (i, j), j=j))
    full = lambda a: pl.BlockSpec(a.shape, lambda i: (0, 0))
    return pl.pallas_call(
        _merge_kernel,
        grid=(T // MIX_ROWS,),
        in_specs=[rows(ATT_WIDTH), rows(HG_WIDTH), rows(D_MODEL, 4), rows(D_MODEL, 5), rows(D_MODEL),
                  full(w_a), full(w_h), full(w_o), full(ln_w), full(ln_b)],
        out_specs=pl.BlockSpec((MIX_ROWS * LANE_CHUNKS, LANES), lambda i: (i, 0)),
        out_shape=jax.ShapeDtypeStruct((T * LANE_CHUNKS, LANES), F32),
        compiler_params=_cparams("parallel"),
        name="merge_ln1",
    )(y_att, y_hg, u_hg, u_hg, x2d, w_a, w_h, w_o, ln_w, ln_b)


def _first_argmax(v, ids, n):
    mx = jnp.max(v, axis=0, keepdims=True)
    return mx, jnp.min(jnp.where(v == mx, ids, n), axis=0, keepdims=True)


def _route_kernel(x1_ref, p_ref, wrt_ref, rb_ref, wsg_ref, wsu_ref, wsd_ref, wpg_ref, wpp_ref,
                  base_ref, idx_ref, gate_ref, rank_ref, cnt_ref, carry_ref):
    @pl.when(pl.program_id(0) % (MOE_TILE // MIX_ROWS) == 0)
    def _():
        carry_ref[...] = jnp.zeros_like(carry_ref)

    x1 = _load_row_tiles(x1_ref, MIX_ROWS)
    x1b = x1.astype(BF16)
    logits = lax.dot_general(wrt_ref[...], x1, (((1,), (1,)), ((), ())), preferred_element_type=F32,
                             precision=lax.Precision.HIGHEST)
    s = jax.nn.sigmoid(logits)
    sel = s + rb_ref[...]
    eid = lax.broadcasted_iota(jnp.int32, (N_EXPERTS, MIX_ROWS), 0)
    neg = -jnp.inf

    grp = sel.reshape(N_GROUPS, GROUP_SIZE, MIX_ROWS)
    mid = lax.broadcasted_iota(jnp.int32, grp.shape, 1)
    m1 = jnp.max(grp, axis=1, keepdims=True)
    i1 = jnp.min(jnp.where(grp == m1, mid, GROUP_SIZE), axis=1, keepdims=True)
    m2 = jnp.max(jnp.where(mid == i1, neg, grp), axis=1, keepdims=True)
    gscore = (m1 + m2).reshape(N_GROUPS, MIX_ROWS)
    gid = lax.broadcasted_iota(jnp.int32, (N_GROUPS, MIX_ROWS), 0)
    gsel = jnp.zeros((N_GROUPS, MIX_ROWS), jnp.bool_)
    for _ in range(TOPK_GROUPS):
        _, gi = _first_argmax(gscore, gid, N_GROUPS)
        hit = gid == gi
        gsel = gsel | hit
        gscore = jnp.where(hit, neg, gscore)
    emask = jnp.broadcast_to(gsel.reshape(N_GROUPS, 1, MIX_ROWS), grp.shape).reshape(N_EXPERTS, MIX_ROWS)
    cand = jnp.where(emask, sel, neg)

    idxs, gates = [], []
    chosen = jnp.zeros((N_EXPERTS, MIX_ROWS), jnp.bool_)
    for _ in range(TOP_K):
        _, ei = _first_argmax(cand, eid, N_EXPERTS)
        hit = eid == ei
        idxs.append(ei)
        gates.append(jnp.sum(jnp.where(hit, s, 0.0), axis=0, keepdims=True))
        chosen = chosen | hit
        cand = jnp.where(hit, neg, cand)
    g = jnp.concatenate(gates, axis=0)
    g = g / jnp.sum(g, axis=0, keepdims=True) * ROUTED_SCALE
    idx_ref[...] = jnp.concatenate(idxs, axis=0)
    gate_ref[...] = g

    onehot = jnp.where(chosen, 1.0, 0.0)
    tr = lax.broadcasted_iota(jnp.int32, (MIX_ROWS, MIX_ROWS), 0)
    tc = lax.broadcasted_iota(jnp.int32, (MIX_ROWS, MIX_ROWS), 1)
    before = jnp.where(tr < tc, 1.0, 0.0).astype(BF16)
    prefix = jnp.dot(onehot.astype(BF16), before, preferred_element_type=F32)
    rankfull = (carry_ref[:, 0:1] + prefix).astype(jnp.int32)
    rank_ref[...] = jnp.concatenate(
        [jnp.sum(jnp.where(eid == ei, rankfull, 0), axis=0, keepdims=True) for ei in idxs], axis=0)
    total = carry_ref[...] + jnp.sum(onehot, axis=1, keepdims=True)
    carry_ref[...] = total
    cnt_ref[...] = total.astype(jnp.int32)

    hg = jnp.dot(x1b, wsg_ref[...], preferred_element_type=F32)
    hu = jnp.dot(x1b, wsu_ref[...], preferred_element_type=F32)
    shared = jnp.dot((hg * jax.nn.sigmoid(hg) * hu).astype(BF16), wsd_ref[...], preferred_element_type=F32)
    ple = (jax.nn.sigmoid(jnp.dot(x1b, wpg_ref[...], preferred_element_type=F32))
           * jnp.dot(p_ref[...].astype(BF16), wpp_ref[...], preferred_element_type=F32))
    _store_row_tiles(base_ref, DEEPNORM_ALPHA * x1 + shared + ple, MIX_ROWS)


def _route(x1, p2d, wr_t, rbias, wsg, wsu, wsd, wpg, wpp):
    T = x1.shape[0] // LANE_CHUNKS
    per_tile = MOE_TILE // MIX_ROWS
    full = lambda a: pl.BlockSpec(a.shape, lambda i: (0, 0))
    tok = pl.BlockSpec((TOP_K, MIX_ROWS), lambda i: (0, i))
    row_tiles = pl.BlockSpec((MIX_ROWS * LANE_CHUNKS, LANES), lambda i: (i, 0))
    return pl.pallas_call(
        _route_kernel,
        grid=(T // MIX_ROWS,),
        in_specs=[row_tiles,
                  pl.BlockSpec((MIX_ROWS, PLE_DIM), lambda i: (i, 0)),
                  full(wr_t), full(rbias), full(wsg), full(wsu), full(wsd), full(wpg), full(wpp)],
        out_specs=[row_tiles, tok, tok, tok,
                   pl.BlockSpec((None, N_EXPERTS, LANES), lambda i: (i // per_tile, 0, 0))],
        out_shape=[jax.ShapeDtypeStruct((T * LANE_CHUNKS, LANES), F32),
                   jax.ShapeDtypeStruct((TOP_K, T), jnp.int32),
                   jax.ShapeDtypeStruct((TOP_K, T), F32),
                   jax.ShapeDtypeStruct((TOP_K, T), jnp.int32),
                   jax.ShapeDtypeStruct((T // MOE_TILE, N_EXPERTS, LANES), jnp.int32)],
        scratch_shapes=[pltpu.VMEM((N_EXPERTS, LANES), F32)],
        compiler_params=_cparams("arbitrary"),
        name="route_shared_ple",
    )(x1, p2d, wr_t, rbias, wsg, wsu, wsd, wpg, wpp)


def _dest_kernel(off_ref, idx_ref, rank_ref, dest_ref):
    idx = idx_ref[...]
    base = pl.program_id(0) * N_EXPERTS

    def body(e, acc):
        return acc + jnp.where(idx == e, off_ref[base + e], 0)

    dest_ref[...] = lax.fori_loop(0, N_EXPERTS, body, rank_ref[...])


def _dest(off, idx, rank):
    T = idx.shape[1]
    spec = pl.BlockSpec((TOP_K, MOE_TILE), lambda i, off: (0, i))
    return pl.pallas_call(
        _dest_kernel,
        grid_spec=pltpu.PrefetchScalarGridSpec(num_scalar_prefetch=1, grid=(T // MOE_TILE,),
                                               in_specs=[spec, spec], out_specs=spec),
        out_shape=jax.ShapeDtypeStruct(idx.shape, jnp.int32),
        compiler_params=_cparams("arbitrary"),
        name="moe_dest",
    )(off, idx, rank)


def _plan_kernel(dest_ref, list_ref):
    first = pl.program_id(1) * PLAN_CODES

    @pl.when(pl.program_id(1) == 0)
    def _():
        def pad(u, c):
            list_ref[MOE_TILE * TOP_K + u] = 0
            return c

        lax.fori_loop(0, LIST_PAD, pad, 0, unroll=8)

    def body(i, c):
        a = i * PLAN_GROUP
        dests = [dest_ref[a + u] for u in range(PLAN_GROUP)]
        for u, d in enumerate(dests):
            list_ref[d] = first + a + u
        return c

    lax.fori_loop(0, PLAN_CODES // PLAN_GROUP, body, 0)


def _plan(dest):
    n_tiles = dest.shape[0] // (MOE_TILE * TOP_K)
    per_tile = MOE_TILE * TOP_K // PLAN_CODES
    return pl.pallas_call(
        _plan_kernel,
        grid=(n_tiles, per_tile),
        in_specs=[pl.BlockSpec((PLAN_CODES,), lambda i, j: (i * per_tile + j,), memory_space=pltpu.SMEM)],
        out_specs=pl.BlockSpec((LIST_LEN,), lambda i, j: (i,), memory_space=pltpu.SMEM),
        out_shape=jax.ShapeDtypeStruct((n_tiles * LIST_LEN,), jnp.int32),
        compiler_params=_cparams("arbitrary", "arbitrary"),
        name="moe_plan",
    )(dest)


def _moe_tile_kernel(cnt_ref, off_ref, list_ref, gate_ref, wg_ref, wu_ref, wd_ref, lw_ref, lb_ref, x_hbm, base_hbm,
                     o_hbm, x_s, acc_s, xg_a, xg_b, y_a, y_b, stage_s):
    tile, step = pl.program_id(0), pl.program_id(1)
    rows_of = lambda ref, r, n: ref.at[pl.ds(pl.multiple_of(r * LANE_CHUNKS, LANE_CHUNKS), n * LANE_CHUNKS), :]
    tile_rows = pl.ds(pl.multiple_of(tile * (MOE_TILE * LANE_CHUNKS), LANE_CHUNKS), MOE_TILE * LANE_CHUNKS)
    last_pair = pl.num_programs(0) * N_EXPERTS - 1
    first_row = lambda code: pl.multiple_of(code & -LANE_CHUNKS, LANE_CHUNKS)
    tile_at = lambda ref, r: ref.at[pl.ds(r, LANE_CHUNKS), :]

    def gather_group(xg, first, jb):
        at = first + jb * GATHER_GROUP
        rows = [tile_at(x_s, first_row(list_ref[at + u]))[...] for u in range(GATHER_GROUP)]
        rows_of(xg, jb * GATHER_GROUP, GATHER_GROUP)[...] = jnp.concatenate(rows, axis=0)

    def gather_loop(xg, first):
        def body(jb, cc):
            gather_group(xg, first, jb)
            return cc

        lax.fori_loop(0, MOE_CHUNK // GATHER_GROUP, body, 0)

    def swiglu(xg, y, half):
        xb = _load_row_tiles(xg, MOE_CHUNK).astype(BF16)
        hg = jnp.dot(xb, wg_ref[half], preferred_element_type=F32)
        hu = jnp.dot(xb, wu_ref[half], preferred_element_type=F32)
        act = (hg * jax.nn.sigmoid(hg) * hu).astype(BF16)
        _store_row_tiles(y, jnp.dot(act, wd_ref[half], preferred_element_type=F32), MOE_CHUNK)

    def scatter_group(y, first, j0, live):
        codes = [list_ref[first + j0 + u] for u in range(live)]
        gates = [gate_ref[code] for code in codes]
        dsts = [first_row(code) for code in codes]
        yv = rows_of(y, j0, live)[...]
        vals = [tile_at(acc_s, d)[...] + g * yv[u * LANE_CHUNKS:(u + 1) * LANE_CHUNKS]
                for u, (d, g) in enumerate(zip(dsts, gates))]
        for d, val in reversed(list(zip(dsts, vals))):
            tile_at(acc_s, d)[...] = val

    def scatter_loop(y, first, m):
        def body(jg, cc):
            scatter_group(y, first, jg * SCATTER_GROUP, SCATTER_GROUP)
            return cc

        lax.fori_loop(0, m // SCATTER_GROUP, body, 0)
        for live in range(1, SCATTER_GROUP):
            @pl.when(m % SCATTER_GROUP == live)
            def _(live=live):
                scatter_group(y, first, m - live, live)

    @pl.when(step == 0)
    def _():
        pltpu.sync_copy(x_hbm.at[tile_rows, :], x_s)
        pltpu.sync_copy(base_hbm.at[tile_rows, :], acc_s)
        y_b[...] = jnp.zeros_like(y_b)
        gather_loop(xg_a, off_ref[tile * N_EXPERTS])

    def run_expert(half, xg_cur, y_cur, xg_nxt, y_prv):
        pair = tile * N_EXPERTS + 2 * step + half
        n, off = cnt_ref[pair], off_ref[pair]
        prev_off = off_ref[jnp.maximum(pair - 1, 0)]
        next_off = off_ref[jnp.minimum(pair + 1, last_pair)]
        live_row = lax.broadcasted_iota(jnp.int32, (MOE_CHUNK, 2 * LANES), 0) < n

        gathers = [functools.partial(gather_group, xg_nxt, next_off, jb) for jb in range(MOE_CHUNK // GATHER_GROUP)]
        scatters = [functools.partial(scatter_group, y_prv, prev_off, jg * SCATTER_GROUP, SCATTER_GROUP)
                    for jg in range(MOE_CHUNK // SCATTER_GROUP)]
        side = [s for both in zip(gathers, scatters) for s in both]
        pieces = 2 + LANE_CHUNKS // 2
        per_piece = -(-len(side) // pieces)

        def side_work(i):
            for s in side[i * per_piece:(i + 1) * per_piece]:
                s()

        xb = _load_row_tiles(xg_cur, MOE_CHUNK).astype(BF16)
        hg = jnp.dot(xb, wg_ref[half], preferred_element_type=F32)
        side_work(0)
        hu = jnp.dot(xb, wu_ref[half], preferred_element_type=F32)
        side_work(1)
        act = (hg * jax.nn.sigmoid(hg) * hu).astype(BF16)
        for q in range(LANE_CHUNKS // 2):
            out = jnp.dot(act, wd_ref[half, :, q * 2 * LANES:(q + 1) * 2 * LANES], preferred_element_type=F32)
            out = jnp.where(live_row, out, 0.0)
            for c in range(2):
                y_cur[pl.ds(2 * q + c, MOE_CHUNK, stride=LANE_CHUNKS), :] = out[:, c * LANES:(c + 1) * LANES]
            side_work(2 + q)

        def extra_chunk(c, carry):
            first = off + c * MOE_CHUNK
            gather_loop(xg_cur, first)
            swiglu(xg_cur, y_prv, half)
            scatter_loop(y_prv, first, jnp.minimum(MOE_CHUNK, n - c * MOE_CHUNK))
            return carry

        lax.fori_loop(1, (n + MOE_CHUNK - 1) // MOE_CHUNK, extra_chunk, 0)
        return n, off

    run_expert(0, xg_a, y_a, xg_b, y_b)
    n_last, off_last = run_expert(1, xg_b, y_b, xg_a, y_a)

    @pl.when(step == pl.num_programs(1) - 1)
    def _():
        scatter_loop(y_b, off_last, jnp.minimum(MOE_CHUNK, n_last))
        for c in range(MOE_TILE // MIX_ROWS):
            z = _load_row_tiles(acc_s, MIX_ROWS, c * MIX_ROWS * LANE_CHUNKS)
            stage_s[...] = _layer_norm(z, lw_ref[...], lb_ref[...])
            pltpu.sync_copy(stage_s, o_hbm.at[pl.ds(tile * MOE_TILE + c * MIX_ROWS, MIX_ROWS), :])


def _moe_tiles(x1, base, tok_list, gate, cnt, off, wg, wu, wd, ln_w, ln_b):
    T = x1.shape[0] // LANE_CHUNKS
    w_spec = lambda shape: pl.BlockSpec((2,) + shape, lambda i, e, cnt, off: (e, 0, 0))
    vec = pl.BlockSpec((1, D_MODEL), lambda i, e, cnt, off: (0, 0))
    hbm = pl.BlockSpec(memory_space=pl.ANY)
    tile_rows = MOE_TILE * LANE_CHUNKS
    return pl.pallas_call(
        _moe_tile_kernel,
        grid_spec=pltpu.PrefetchScalarGridSpec(
            num_scalar_prefetch=2,
            grid=(T // MOE_TILE, N_EXPERTS // 2),
            in_specs=[pl.BlockSpec((LIST_LEN,), lambda i, e, cnt, off: (i,), memory_space=pltpu.SMEM),
                      pl.BlockSpec((MOE_TILE * TOP_K,), lambda i, e, cnt, off: (i,), memory_space=pltpu.SMEM),
                      w_spec((D_MODEL, EXPERT_FF)), w_spec((D_MODEL, EXPERT_FF)), w_spec((EXPERT_FF, D_MODEL)),
                      vec, vec, hbm, hbm],
            out_specs=hbm,
            scratch_shapes=[pltpu.VMEM((tile_rows, LANES), F32),
                            pltpu.VMEM((tile_rows, LANES), F32)]
                           + [pltpu.VMEM((MOE_CHUNK * LANE_CHUNKS, LANES), F32)] * 4
                           + [pltpu.VMEM((MIX_ROWS, D_MODEL), F32)],
        ),
        out_shape=jax.ShapeDtypeStruct((T, D_MODEL), F32),
        compiler_params=_cparams("arbitrary", "arbitrary"),
        name="moe_tiles_ln2",
    )(cnt, off, tok_list, gate, wg, wu, wd, ln_w, ln_b, x1, base)


def kernel(x, p, w_in, hgrn_lb_logits, hgrn_norm_w, w_branch_att, w_branch_hgrn, w_out, ln1_w, ln1_b, router_w, router_bias, expert_w_gate, expert_w_up, expert_w_down, shared_w_gate, shared_w_up, shared_w_down, ple_gate_w, ple_proj_w, ln2_w, ln2_b):
    B, S, D = x.shape
    T = B * S
    l = 0
    x2d = x.reshape(T, D)
    bf = lambda a: a.astype(BF16)

    ws = _att_weights(w_in[l])
    qkv = [_proj_att(x2d, ws[g], d) for g, d in enumerate(ATT_DILATIONS)]
    y_att = _attention(qkv, B, S)
    u_hg = _proj(x2d, bf(w_in[l][:, 3 * len(ATT_DILATIONS) * ATT_WIDTH:]), 1536)
    y_hg = _hgrn(u_hg, hgrn_lb_logits, hgrn_norm_w[l:l + 1], B, S)
    x1 = _merge(y_att, y_hg, u_hg, x2d, bf(w_branch_att[l]), bf(w_branch_hgrn[l]), bf(w_out[l]),
                ln1_w[l:l + 1], ln1_b[l:l + 1])

    base, idx, gate, rank, counts = _route(
        x1, p[l].reshape(T, PLE_DIM), router_w[l].T, router_bias[l].reshape(N_EXPERTS, 1),
        bf(shared_w_gate[l]), bf(shared_w_up[l]), bf(shared_w_down[l]), bf(ple_gate_w[l]), bf(ple_proj_w[l]))
    cnt = counts[:, :, 0]
    off = jnp.cumsum(cnt, axis=1) - cnt
    cnt, off = cnt.reshape(-1), off.reshape(-1)
    tok_list = _plan(_dest(off, idx, rank).T.reshape(-1))
    gate_list = gate.T.reshape(-1)
    out = _moe_tiles(x1, base, tok_list, gate_list, cnt, off, bf(expert_w_gate[l]), bf(expert_w_up[l]),
                     bf(expert_w_down[l]), ln2_w[l:l + 1], ln2_b[l:l + 1])
    return out.reshape(B, S, D)
```

```python
import functools

import jax
import jax.numpy as jnp
import numpy as np
from jax import lax
from jax.experimental import pallas as pl
from jax.experimental.pallas import tpu as pltpu

F32 = jnp.float32
BF16 = jnp.bfloat16

D_MODEL = 1024
ATT_HEAD_DIM = 64
ATT_HEADS = 8
ATT_DILATIONS = (1, 4, 16)
ATT_BLOCK = 128
ATT_WIDTH = ATT_HEADS * ATT_HEAD_DIM
ATT_TILE = ATT_BLOCK * max(ATT_DILATIONS)
NEG_INF = -1e30

HG_HEADS = 8
HG_DIM = 128
HG_WIDTH = HG_HEADS * HG_DIM
HG_CHUNK = 32
HG_TILE = 256
RMS_EPS = 1e-6

N_EXPERTS = 64
TOP_K = 8
TOP_K_BITS = 3
N_GROUPS = 8
GROUP_SIZE = N_EXPERTS // N_GROUPS
TOPK_GROUPS = 4
EXPERT_FF = 256
ROUTED_SCALE = 2.5
PLE_DIM = 256
LN_EPS = 1e-5
DEPTH = 1
DEEPNORM_ALPHA = (2.0 * DEPTH) ** 0.25

LANES = 128
LANE_CHUNKS = D_MODEL // LANES
PROJ_ROWS = 512
ATT_PROJ_ROWS = 1024
MIX_ROWS = 512
MOE_TILE = 4096
MOE_CHUNK = 576
PLAN_CODES = MOE_TILE * TOP_K
PLAN_GROUP = 16
LIST_PAD = 1024
LIST_LEN = MOE_TILE * TOP_K + LIST_PAD
GATHER_GROUP = 8
SCATTER_GROUP = 8
V7X_VMEM_LIMIT = 56 * 1024 * 1024


def _cparams(*sem):
    return pltpu.CompilerParams(dimension_semantics=sem, vmem_limit_bytes=V7X_VMEM_LIMIT)


def _proj_att_kernel(*refs, dil):
    x_refs, w_ref, o_ref = refs[:LANE_CHUNKS], refs[LANE_CHUNKS], refs[LANE_CHUNKS + 1]
    n = ATT_PROJ_ROWS // dil

    def rows(ref):
        if dil == 1:
            return ref[...]
        return jnp.concatenate([ref[pl.ds(r, n, stride=dil), :] for r in range(dil)], axis=0)

    xp = jnp.concatenate([rows(ref).astype(BF16) for ref in x_refs], axis=1)
    y = jnp.dot(xp, w_ref[...], preferred_element_type=F32)
    o_ref[...] = y.astype(BF16).reshape(dil, n, 3 * ATT_WIDTH)


def _proj_att(x2d, w, dil):
    T = x2d.shape[0]
    per = ATT_TILE // ATT_PROJ_ROWS
    n = ATT_PROJ_ROWS // dil
    out = pl.pallas_call(
        functools.partial(_proj_att_kernel, dil=dil),
        grid=(T // ATT_PROJ_ROWS,),
        in_specs=[pl.BlockSpec((ATT_PROJ_ROWS, LANES), functools.partial(lambda i, c: (i, c), c=c))
                  for c in range(LANE_CHUNKS)]
                 + [pl.BlockSpec((D_MODEL, 3 * ATT_WIDTH), lambda i: (0, 0))],
        out_specs=pl.BlockSpec((None, dil, None, n, 3 * ATT_WIDTH), lambda i: (i // per, 0, i % per, 0, 0)),
        out_shape=jax.ShapeDtypeStruct((T // ATT_TILE, dil, per, n, 3 * ATT_WIDTH), BF16),
        compiler_params=_cparams("parallel"),
        name=f"proj_att_d{dil}",
    )(*([x2d] * LANE_CHUNKS), w)
    return out.reshape(T // ATT_TILE, dil, ATT_TILE // dil, 3 * ATT_WIDTH)


def _att_pair(q2, kp, kc, vp, vc, bias_ref, g, first):
    def head0_lanes(rows, dtype):
        lane = lax.broadcasted_iota(jnp.int32, (rows, 2 * ATT_HEAD_DIM), 1)
        return lane.astype(F32).astype(dtype) < ATT_HEAD_DIM

    lo_q = head0_lanes(ATT_BLOCK, BF16)
    lo_v = head0_lanes(2 * ATT_BLOCK, BF16)
    k2 = jnp.concatenate([kp, kc], axis=0)
    v2 = jnp.concatenate([vp, vc], axis=0)
    zero = jnp.zeros_like(q2)
    ps, ms = [], []
    for hh in range(2):
        qm = jnp.where(lo_q, q2, zero) if hh == 0 else jnp.where(lo_q, zero, q2)
        s = lax.dot_general(qm, k2, (((1,), (1,)), ((), ())), preferred_element_type=F32)
        s = s + bias_ref[g, hh, first]
        m = jnp.max(s, axis=-1, keepdims=True)
        ps.append(jnp.exp(s - m).astype(BF16))
        ms.append(m)
    pcat = jnp.concatenate(ps, axis=1)
    zero_v, one_v = jnp.zeros_like(v2), jnp.ones_like(v2)
    rhs = jnp.concatenate([
        jnp.concatenate([jnp.where(lo_v, v2, zero_v), jnp.where(lo_v, one_v, zero_v)], axis=1),
        jnp.concatenate([jnp.where(lo_v, zero_v, v2), jnp.where(lo_v, zero_v, one_v)], axis=1)], axis=0)
    nd = jnp.dot(pcat, rhs, preferred_element_type=F32)
    m2 = jnp.where(head0_lanes(ATT_BLOCK, F32), ms[0], ms[1])
    return nd[:, :2 * ATT_HEAD_DIM], m2, nd[:, 2 * ATT_HEAD_DIM:]


def _att_kernel(*refs):
    (q0, kc0, vc0, kp0, vp0, q1, kc1, vc1, kp1, vp1, q2, kc2, vc2, kp2, vp2,
     bias_ref, o_ref) = refs[:17]
    ng = len(ATT_DILATIONS)
    num_s, m_s, den_s = refs[17:17 + ng], refs[17 + ng:17 + 2 * ng], refs[17 + 2 * ng:]
    first_tile = (pl.program_id(2) == 0).astype(jnp.int32)
    groups = ((q0, kc0, vc0, kp0, vp0), (q1, kc1, vc1, kp1, vp1), (q2, kc2, vc2, kp2, vp2))
    for g, dil in enumerate(ATT_DILATIONS):
        q_ref, kc_ref, vc_ref, kp_ref, vp_ref = groups[g]
        nb = ATT_TILE // dil // ATT_BLOCK
        for r in range(dil):
            for n in range(nb):
                rows = pl.ds(n * ATT_BLOCK, ATT_BLOCK)
                if n == 0:
                    prev = pl.ds((nb - 1) * ATT_BLOCK, ATT_BLOCK)
                    kp, vp, first = kp_ref[r, prev, :], vp_ref[r, prev, :], first_tile
                else:
                    prev = pl.ds((n - 1) * ATT_BLOCK, ATT_BLOCK)
                    kp, vp, first = kc_ref[r, prev, :], vc_ref[r, prev, :], 0
                num, m, den = _att_pair(q_ref[r, rows, :], kp, kc_ref[r, rows, :], vp, vc_ref[r, rows, :],
                                        bias_ref, g, first)
                if dil == 1:
                    dst = rows
                else:
                    dst = pl.ds(n * ATT_BLOCK * dil + r, ATT_BLOCK, stride=dil)
                num_s[g][dst, :] = num
                m_s[g][dst, :] = m
                den_s[g][dst, :] = den
    m_all = jnp.maximum(jnp.maximum(m_s[0][...], m_s[1][...]), m_s[2][...])
    num = jnp.zeros((ATT_TILE, 2 * ATT_HEAD_DIM), F32)
    den = jnp.zeros((ATT_TILE, 2 * ATT_HEAD_DIM), F32)
    for g in range(ng):
        sc = jnp.exp(m_s[g][...] - m_all)
        num = num + sc * num_s[g][...]
        den = den + sc * den_s[g][...]
    o_ref[...] = (num / den).astype(o_ref.dtype)


def _att_bias_table():
    qi = np.arange(ATT_BLOCK)[:, None]
    ki = np.arange(2 * ATT_BLOCK)[None, :]
    steps = qi + ATT_BLOCK - ki
    valid = (steps >= 0) & (steps <= ATT_BLOCK)
    slopes = np.array([2.0 ** (-8.0 * (h + 1) / ATT_HEADS) for h in range(ATT_HEADS)], np.float32)
    tab = np.empty((len(ATT_DILATIONS), ATT_HEADS, 2, ATT_BLOCK, 2 * ATT_BLOCK), np.float32)
    for g, dil in enumerate(ATT_DILATIONS):
        bias = -slopes[:, None, None] * (steps * dil).astype(np.float32)[None]
        tab[g, :, 0] = np.where(valid[None], bias, NEG_INF)
        tab[g, :, 1] = np.where((valid & (ki >= ATT_BLOCK))[None], bias, NEG_INF)
    return jnp.asarray(tab)


def _attention(qkv, B, S):
    tiles = S // ATT_TILE
    pair = 2 * ATT_HEAD_DIM
    npair = ATT_WIDTH // pair
    in_specs, args = [], []
    for g, dil in enumerate(ATT_DILATIONS):
        blk = (None, dil, ATT_TILE // dil, pair)
        cur = lambda b, hp, t, off: (b * tiles + t, 0, 0, off * npair + hp)
        prv = lambda b, hp, t, off: (b * tiles + jnp.maximum(t - 1, 0), 0, 0, off * npair + hp)
        in_specs += [pl.BlockSpec(blk, functools.partial(cur, off=0)),
                     pl.BlockSpec(blk, functools.partial(cur, off=1)),
                     pl.BlockSpec(blk, functools.partial(cur, off=2)),
                     pl.BlockSpec(blk, functools.partial(prv, off=1)),
                     pl.BlockSpec(blk, functools.partial(prv, off=2))]
        args += [qkv[g]] * 5
    in_specs.append(pl.BlockSpec((len(ATT_DILATIONS), 2, 2, ATT_BLOCK, 2 * ATT_BLOCK),
                                 lambda b, hp, t: (0, hp, 0, 0, 0)))
    args.append(_att_bias_table())
    scratch = [pltpu.VMEM((ATT_TILE, pair), F32) for _ in range(3 * len(ATT_DILATIONS))]
    return pl.pallas_call(
        _att_kernel,
        grid=(B, npair, tiles),
        in_specs=in_specs,
        out_specs=pl.BlockSpec((ATT_TILE, pair), lambda b, hp, t: (b * tiles + t, hp)),
        out_shape=jax.ShapeDtypeStruct((B * S, ATT_WIDTH), BF16),
        scratch_shapes=scratch,
        compiler_params=_cparams("parallel", "parallel", "arbitrary"),
        name="dilated_attention",
    )(*args)


def _att_weights(w_in_l):
    out = []
    width = len(ATT_DILATIONS) * ATT_WIDTH
    for g in range(len(ATT_DILATIONS)):
        cols = [w_in_l[:, part * width + g * ATT_WIDTH: part * width + (g + 1) * ATT_WIDTH] for part in range(3)]
        cols[0] = cols[0] * (ATT_HEAD_DIM ** -0.5)
        out.append(jnp.concatenate(cols, axis=1).astype(BF16))
    return out


def _proj_kernel(x_ref, w_ref, o_ref, *, col_tile):
    xb = x_ref[...].astype(BF16)
    for c in range(w_ref.shape[1] // col_tile):
        cols = slice(c * col_tile, (c + 1) * col_tile)
        o_ref[:, cols] = jnp.dot(xb, w_ref[:, cols], preferred_element_type=F32).astype(o_ref.dtype)


def _proj(x2d, w, col_tile):
    T, N = x2d.shape[0], w.shape[1]
    return pl.pallas_call(
        functools.partial(_proj_kernel, col_tile=col_tile),
        grid=(T // PROJ_ROWS,),
        in_specs=[pl.BlockSpec((PROJ_ROWS, D_MODEL), lambda i: (i, 0)),
                  pl.BlockSpec((D_MODEL, N), lambda i: (0, 0))],
        out_specs=pl.BlockSpec((PROJ_ROWS, N), lambda i: (i, 0)),
        out_shape=jax.ShapeDtypeStruct((T, N), BF16),
        compiler_params=_cparams("parallel"),
        name="proj_hgrn_gates",
    )(x2d, w)


def _split3(v):
    a = v.astype(BF16)
    r = v - a.astype(F32)
    b = r.astype(BF16)
    c = (r - b.astype(F32)).astype(BF16)
    return a, b, c


def _hgrn_kernel(q_ref, f_ref, i_ref, g_ref, lbl_ref, gain_ref, o_ref, state_ref):
    @pl.when(pl.program_id(1) == 0)
    def _():
        state_ref[...] = jnp.zeros_like(state_ref)

    lbl = lbl_ref[...]
    e = jnp.exp(lbl - jnp.max(lbl, axis=0, keepdims=True))
    lb = e[0:1] / jnp.sum(e, axis=0, keepdims=True)
    forget = lb + (1.0 - lb) * jax.nn.sigmoid(f_ref[...].astype(F32))
    log_f = jnp.log(forget)
    key = 1.0 - forget

    row = lax.broadcasted_iota(jnp.int32, (HG_TILE, HG_TILE), 0)
    col = lax.broadcasted_iota(jnp.int32, (HG_TILE, HG_TILE), 1)
    causal = (row >= col) & ((row // HG_CHUNK) == (col // HG_CHUNK))
    tri = jnp.where(causal, 1.0, 0.0).astype(BF16)
    b = sum(jnp.dot(tri, t, preferred_element_type=F32) for t in _split3(log_f))
    eb = jnp.exp(b)
    q_dec = (q_ref[...].astype(F32) * eb).astype(BF16)
    k_inv = key * jnp.exp(-b)
    xi = i_ref[...].astype(F32)
    val = (xi * jax.nn.sigmoid(xi)).astype(BF16)
    k_inv_b = k_inv.astype(BF16)

    n_chunks = HG_TILE // HG_CHUNK
    last_rows = [eb[(c + 1) * HG_CHUNK - 1:(c + 1) * HG_CHUNK, :] for c in range(n_chunks)]
    dec_rows = jnp.concatenate([jnp.broadcast_to(r, (HG_CHUNK, HG_WIDTH)) for r in last_rows], axis=0)
    k_end = (k_inv * dec_rows).astype(BF16)
    chunk_of_row = (lax.broadcasted_iota(jnp.int32, (HG_TILE, HG_DIM), 0) // HG_CHUNK).astype(F32).astype(BF16)
    in_chunk = [chunk_of_row == c for c in range(n_chunks)]
    zero = jnp.zeros((HG_TILE, HG_DIM), BF16)

    def per_chunk_columns(t):
        return jnp.concatenate([jnp.where(m, t, zero) for m in in_chunk], axis=1)

    outs = []
    for h in range(HG_HEADS):
        cols = slice(h * HG_DIM, (h + 1) * HG_DIM)
        qd, ki, vv = q_dec[:, cols], k_inv_b[:, cols], val[:, cols]
        a = lax.dot_general(qd, ki, (((1,), (1,)), ((), ())), preferred_element_type=F32)
        a = jnp.where(causal, a, 0.0).astype(BF16)
        o_intra = jnp.dot(a, vv, preferred_element_type=F32)
        upd = lax.dot_general(vv, per_chunk_columns(k_end[:, cols]), (((0,), (0,)), ((), ())),
                              preferred_element_type=F32)
        st = state_ref[h]
        entering = []
        for c in range(n_chunks):
            entering.append(st.astype(BF16))
            st = st * last_rows[c][:, cols] + upd[:, c * HG_DIM:(c + 1) * HG_DIM]
        state_ref[h] = st
        o_inter = lax.dot_general(per_chunk_columns(qd), jnp.concatenate(entering, axis=1),
                                  (((1,), (1,)), ((), ())), preferred_element_type=F32)
        o = o_intra + o_inter
        o = o * lax.rsqrt(jnp.mean(jnp.square(o), axis=-1, keepdims=True) + RMS_EPS)
        outs.append(o)
    o = jnp.concatenate(outs, axis=1) * gain_ref[...]
    gg = g_ref[...].astype(F32)
    o_ref[...] = (o * (gg * jax.nn.sigmoid(gg))).astype(o_ref.dtype)


def _hgrn(u_hg, lb_logits, gain, B, S):
    tiles = S // HG_TILE
    col = lambda j: pl.BlockSpec((HG_TILE, HG_WIDTH), functools.partial(lambda b, t, j: (b * tiles + t, j), j=j))
    return pl.pallas_call(
        _hgrn_kernel,
        grid=(B, tiles),
        in_specs=[col(0), col(1), col(2), col(3),
                  pl.BlockSpec((2, HG_WIDTH), lambda b, t: (0, 0)),
                  pl.BlockSpec((1, HG_WIDTH), lambda b, t: (0, 0))],
        out_specs=pl.BlockSpec((HG_TILE, HG_WIDTH), lambda b, t: (b * tiles + t, 0)),
        out_shape=jax.ShapeDtypeStruct((B * S, HG_WIDTH), BF16),
        scratch_shapes=[pltpu.VMEM((HG_HEADS, HG_DIM, HG_DIM), F32)],
        compiler_params=_cparams("parallel", "arbitrary"),
        name="hgrn2",
    )(u_hg, u_hg, u_hg, u_hg, lb_logits, gain)


def _load_row_tiles(ref, n, start=0):
    return jnp.concatenate([ref[pl.ds(start + c, n, stride=LANE_CHUNKS), :] for c in range(LANE_CHUNKS)], axis=1)


def _store_row_tiles(ref, val, n):
    for c in range(LANE_CHUNKS):
        ref[pl.ds(c, n, stride=LANE_CHUNKS), :] = val[:, c * LANES:(c + 1) * LANES]


def _layer_norm(z, w, b):
    mu = jnp.mean(z, axis=-1, keepdims=True)
    zc = z - mu
    var = jnp.mean(jnp.square(zc), axis=-1, keepdims=True)
    return zc * lax.rsqrt(var + LN_EPS) * w + b


def _merge_kernel(ya_ref, yh_ref, ga_ref, gh_ref, x_ref, wa_ref, wh_ref, wo_ref, lw_ref, lb_ref, o_ref):
    ma = jnp.dot(ya_ref[...], wa_ref[...], preferred_element_type=F32)
    mh = jnp.dot(yh_ref[...], wh_ref[...], preferred_element_type=F32)
    merged = (jax.nn.sigmoid(ga_ref[...].astype(F32)) * ma + jax.nn.sigmoid(gh_ref[...].astype(F32)) * mh)
    z = DEEPNORM_ALPHA * x_ref[...] + jnp.dot(merged.astype(BF16), wo_ref[...], preferred_element_type=F32)
    _store_row_tiles(o_ref, _layer_norm(z, lw_ref[...], lb_ref[...]), MIX_ROWS)


def _merge(y_att, y_hg, u_hg, x2d, w_a, w_h, w_o, ln_w, ln_b):
    T = x2d.shape[0]
    rows = lambda width, j=0: pl.BlockSpec((MIX_ROWS, width), functools.partial(lambda i, j: (i, j), j=j))
    full = lambda a: pl.BlockSpec(a.shape, lambda i: (0, 0))
    return pl.pallas_call(
        _merge_kernel,
        grid=(T // MIX_ROWS,),
        in_specs=[rows(ATT_WIDTH), rows(HG_WIDTH), rows(D_MODEL, 4), rows(D_MODEL, 5), rows(D_MODEL),
                  full(w_a), full(w_h), full(w_o), full(ln_w), full(ln_b)],
        out_specs=pl.BlockSpec((MIX_ROWS * LANE_CHUNKS, LANES), lambda i: (i, 0)),
        out_shape=jax.ShapeDtypeStruct((T * LANE_CHUNKS, LANES), F32),
        compiler_params=_cparams("parallel"),
        name="merge_ln1",
    )(y_att, y_hg, u_hg, u_hg, x2d, w_a, w_h, w_o, ln_w, ln_b)


def _first_argmax(v, ids, n):
    mx = jnp.max(v, axis=0, keepdims=True)
    return mx, jnp.min(jnp.where(v == mx, ids, n), axis=0, keepdims=True)


def _route_kernel(x1_ref, p_ref, wrt_ref, rb_ref, wsg_ref, wsu_ref, wsd_ref, wpg_ref, wpp_ref,
                  base_ref, idx_ref, gate_ref, rank_ref, cnt_ref, carry_ref):
    @pl.when(pl.program_id(0) % (MOE_TILE // MIX_ROWS) == 0)
    def _():
        carry_ref[...] = jnp.zeros_like(carry_ref)

    x1 = _load_row_tiles(x1_ref, MIX_ROWS)
    x1b = x1.astype(BF16)
    logits = lax.dot_general(wrt_ref[...], x1, (((1,), (1,)), ((), ())), preferred_element_type=F32,
                             precision=lax.Precision.HIGHEST)
    s = jax.nn.sigmoid(logits)
    sel = s + rb_ref[...]
    eid = lax.broadcasted_iota(jnp.int32, (N_EXPERTS, MIX_ROWS), 0)
    neg = -jnp.inf

    grp = sel.reshape(N_GROUPS, GROUP_SIZE, MIX_ROWS)
    mid = lax.broadcasted_iota(jnp.int32, grp.shape, 1)
    m1 = jnp.max(grp, axis=1, keepdims=True)
    i1 = jnp.min(jnp.where(grp == m1, mid, GROUP_SIZE), axis=1, keepdims=True)
    m2 = jnp.max(jnp.where(mid == i1, neg, grp), axis=1, keepdims=True)
    gscore = (m1 + m2).reshape(N_GROUPS, MIX_ROWS)
    gid = lax.broadcasted_iota(jnp.int32, (N_GROUPS, MIX_ROWS), 0)
    gsel = jnp.zeros((N_GROUPS, MIX_ROWS), jnp.bool_)
    for _ in range(TOPK_GROUPS):
        _, gi = _first_argmax(gscore, gid, N_GROUPS)
        hit = gid == gi
        gsel = gsel | hit
        gscore = jnp.where(hit, neg, gscore)
    emask = jnp.broadcast_to(gsel.reshape(N_GROUPS, 1, MIX_ROWS), grp.shape).reshape(N_EXPERTS, MIX_ROWS)
    cand = jnp.where(emask, sel, neg)

    idxs, gates = [], []
    chosen = jnp.zeros((N_EXPERTS, MIX_ROWS), jnp.bool_)
    for _ in range(TOP_K):
        _, ei = _first_argmax(cand, eid, N_EXPERTS)
        hit = eid == ei
        idxs.append(ei)
        gates.append(jnp.sum(jnp.where(hit, s, 0.0), axis=0, keepdims=True))
        chosen = chosen | hit
        cand = jnp.where(hit, neg, cand)
    g = jnp.concatenate(gates, axis=0)
    g = g / jnp.sum(g, axis=0, keepdims=True) * ROUTED_SCALE
    idx_ref[...] = jnp.concatenate(idxs, axis=0)
    gate_ref[...] = g

    onehot = jnp.where(chosen, 1.0, 0.0)
    tr = lax.broadcasted_iota(jnp.int32, (MIX_ROWS, MIX_ROWS), 0)
    tc = lax.broadcasted_iota(jnp.int32, (MIX_ROWS, MIX_ROWS), 1)
    before = jnp.where(tr < tc, 1.0, 0.0).astype(BF16)
    prefix = jnp.dot(onehot.astype(BF16), before, preferred_element_type=F32)
    rankfull = (carry_ref[:, 0:1] + prefix).astype(jnp.int32)
    rank_ref[...] = jnp.concatenate(
        [jnp.sum(jnp.where(eid == ei, rankfull, 0), axis=0, keepdims=True) for ei in idxs], axis=0)
    total = carry_ref[...] + jnp.sum(onehot, axis=1, keepdims=True)
    carry_ref[...] = total
    cnt_ref[...] = total.astype(jnp.int32)

    hg = jnp.dot(x1b, wsg_ref[...], preferred_element_type=F32)
    hu = jnp.dot(x1b, wsu_ref[...], preferred_element_type=F32)
    shared = jnp.dot((hg * jax.nn.sigmoid(hg) * hu).astype(BF16), wsd_ref[...], preferred_element_type=F32)
    ple = (jax.nn.sigmoid(jnp.dot(x1b, wpg_ref[...], preferred_element_type=F32))
           * jnp.dot(p_ref[...].astype(BF16), wpp_ref[...], preferred_element_type=F32))
    _store_row_tiles(base_ref, DEEPNORM_ALPHA * x1 + shared + ple, MIX_ROWS)


def _route(x1, p2d, wr_t, rbias, wsg, wsu, wsd, wpg, wpp):
    T = x1.shape[0] // LANE_CHUNKS
    per_tile = MOE_TILE // MIX_ROWS
    full = lambda a: pl.BlockSpec(a.shape, lambda i: (0, 0))
    tok = pl.BlockSpec((TOP_K, MIX_ROWS), lambda i: (0, i))
    row_tiles = pl.BlockSpec((MIX_ROWS * LANE_CHUNKS, LANES), lambda i: (i, 0))
    return pl.pallas_call(
        _route_kernel,
        grid=(T // MIX_ROWS,),
        in_specs=[row_tiles,
                  pl.BlockSpec((MIX_ROWS, PLE_DIM), lambda i: (i, 0)),
                  full(wr_t), full(rbias), full(wsg), full(wsu), full(wsd), full(wpg), full(wpp)],
        out_specs=[row_tiles, tok, tok, tok,
                   pl.BlockSpec((None, N_EXPERTS, LANES), lambda i: (i // per_tile, 0, 0))],
        out_shape=[jax.ShapeDtypeStruct((T * LANE_CHUNKS, LANES), F32),
                   jax.ShapeDtypeStruct((TOP_K, T), jnp.int32),
                   jax.ShapeDtypeStruct((TOP_K, T), F32),
                   jax.ShapeDtypeStruct((TOP_K, T), jnp.int32),
                   jax.ShapeDtypeStruct((T // MOE_TILE, N_EXPERTS, LANES), jnp.int32)],
        scratch_shapes=[pltpu.VMEM((N_EXPERTS, LANES), F32)],
        compiler_params=_cparams("arbitrary"),
        name="route_shared_ple",
    )(x1, p2d, wr_t, rbias, wsg, wsu, wsd, wpg, wpp)


def _dest_kernel(off_ref, idx_ref, rank_ref, dest_ref):
    idx = idx_ref[...]
    base = pl.program_id(0) * N_EXPERTS

    def body(e, acc):
        return acc + jnp.where(idx == e, off_ref[base + e], 0)

    dest_ref[...] = lax.fori_loop(0, N_EXPERTS, body, rank_ref[...])


def _dest(off, idx, rank):
    T = idx.shape[1]
    spec = pl.BlockSpec((TOP_K, MOE_TILE), lambda i, off: (0, i))
    return pl.pallas_call(
        _dest_kernel,
        grid_spec=pltpu.PrefetchScalarGridSpec(num_scalar_prefetch=1, grid=(T // MOE_TILE,),
                                               in_specs=[spec, spec], out_specs=spec),
        out_shape=jax.ShapeDtypeStruct(idx.shape, jnp.int32),
        compiler_params=_cparams("arbitrary"),
        name="moe_dest",
    )(off, idx, rank)


def _plan_kernel(dest_ref, list_ref):
    first = pl.program_id(1) * PLAN_CODES

    @pl.when(pl.program_id(1) == 0)
    def _():
        def pad(u, c):
            list_ref[MOE_TILE * TOP_K + u] = 0
            return c

        lax.fori_loop(0, LIST_PAD, pad, 0, unroll=8)

    def body(i, c):
        a = i * PLAN_GROUP
        dests = [dest_ref[a + u] for u in range(PLAN_GROUP)]
        for u, d in enumerate(dests):
            list_ref[d] = first + a + u
        return c

    lax.fori_loop(0, PLAN_CODES // PLAN_GROUP, body, 0)


def _plan(dest):
    n_tiles = dest.shape[0] // (MOE_TILE * TOP_K)
    per_tile = MOE_TILE * TOP_K // PLAN_CODES
    return pl.pallas_call(
        _plan_kernel,
        grid=(n_tiles, per_tile),
        in_specs=[pl.BlockSpec((PLAN_CODES,), lambda i, j: (i * per_tile + j,), memory_space=pltpu.SMEM)],
        out_specs=pl.BlockSpec((LIST_LEN,), lambda i, j: (i,), memory_space=pltpu.SMEM),
        out_shape=jax.ShapeDtypeStruct((n_tiles * LIST_LEN,), jnp.int32),
        compiler_params=_cparams("arbitrary", "arbitrary"),
        name="moe_plan",
    )(dest)


def _moe_tile_kernel(cnt_ref, off_ref, list_ref, gate_ref, wg_ref, wu_ref, wd_ref, lw_ref, lb_ref, x_hbm, base_hbm,
                     o_hbm, x_s, acc_s, xg_a, xg_b, xg_c, y_a, y_b, y_c, stage_s):
    tile, e = pl.program_id(0), pl.program_id(1)
    rows_of = lambda ref, r, n: ref.at[pl.ds(pl.multiple_of(r * LANE_CHUNKS, LANE_CHUNKS), n * LANE_CHUNKS), :]
    tile_rows = pl.ds(pl.multiple_of(tile * (MOE_TILE * LANE_CHUNKS), LANE_CHUNKS), MOE_TILE * LANE_CHUNKS)

    pair = tile * N_EXPERTS + e
    last_pair = pl.num_programs(0) * N_EXPERTS - 1
    n, off = cnt_ref[pair], off_ref[pair]
    first_row = lambda code: pl.multiple_of(code & -LANE_CHUNKS, LANE_CHUNKS)
    tile_at = lambda ref, r: ref.at[pl.ds(r, LANE_CHUNKS), :]

    def gather_group(xg, first, jb):
        at = first + jb * GATHER_GROUP
        rows = [tile_at(x_s, first_row(list_ref[at + u]))[...] for u in range(GATHER_GROUP)]
        rows_of(xg, jb * GATHER_GROUP, GATHER_GROUP)[...] = jnp.concatenate(rows, axis=0)

    def gather_loop(xg, first):
        def body(jb, cc):
            gather_group(xg, first, jb)
            return cc

        lax.fori_loop(0, MOE_CHUNK // GATHER_GROUP, body, 0)

    def swiglu(xg, y):
        xb = _load_row_tiles(xg, MOE_CHUNK).astype(BF16)
        hg = jnp.dot(xb, wg_ref[...], preferred_element_type=F32)
        hu = jnp.dot(xb, wu_ref[...], preferred_element_type=F32)
        act = (hg * jax.nn.sigmoid(hg) * hu).astype(BF16)
        _store_row_tiles(y, jnp.dot(act, wd_ref[...], preferred_element_type=F32), MOE_CHUNK)

    def scatter_group(y, first, j0, live):
        codes = [list_ref[first + j0 + u] for u in range(live)]
        gates = [gate_ref[code] for code in codes]
        dsts = [first_row(code) for code in codes]
        yv = rows_of(y, j0, live)[...]
        vals = [tile_at(acc_s, d)[...] + g * yv[u * LANE_CHUNKS:(u + 1) * LANE_CHUNKS]
                for u, (d, g) in enumerate(zip(dsts, gates))]
        for d, val in reversed(list(zip(dsts, vals))):
            tile_at(acc_s, d)[...] = val

    def scatter_loop(y, first, m):
        def body(jg, cc):
            scatter_group(y, first, jg * SCATTER_GROUP, SCATTER_GROUP)
            return cc

        lax.fori_loop(0, m // SCATTER_GROUP, body, 0)
        for live in range(1, SCATTER_GROUP):
            @pl.when(m % SCATTER_GROUP == live)
            def _(live=live):
                scatter_group(y, first, m - live, live)

    @pl.when(e == 0)
    def _():
        pltpu.sync_copy(x_hbm.at[tile_rows, :], x_s)
        pltpu.sync_copy(base_hbm.at[tile_rows, :], acc_s)
        y_b[...] = jnp.zeros_like(y_b)
        gather_loop(xg_a, off)

    prev_off = off_ref[jnp.maximum(pair - 1, 0)]
    next_off = off_ref[jnp.minimum(pair + 1, last_pair)]
    live_row = lax.broadcasted_iota(jnp.int32, (MOE_CHUNK, 2 * LANES), 0) < n

    def run_expert(xg_cur, y_cur, xg_nxt, y_prv):
        gathers = [functools.partial(gather_group, xg_nxt, next_off, jb) for jb in range(MOE_CHUNK // GATHER_GROUP)]
        scatters = [functools.partial(scatter_group, y_prv, prev_off, jg * SCATTER_GROUP, SCATTER_GROUP)
                    for jg in range(MOE_CHUNK // SCATTER_GROUP)]
        side = [s for both in zip(gathers, scatters) for s in both]
        pieces = 2 + LANE_CHUNKS // 2
        per_piece = -(-len(side) // pieces)

        def side_work(i):
            for s in side[i * per_piece:(i + 1) * per_piece]:
                s()

        xb = _load_row_tiles(xg_cur, MOE_CHUNK).astype(BF16)
        hg = jnp.dot(xb, wg_ref[...], preferred_element_type=F32)
        side_work(0)
        hu = jnp.dot(xb, wu_ref[...], preferred_element_type=F32)
        side_work(1)
        act = (hg * jax.nn.sigmoid(hg) * hu).astype(BF16)
        for q in range(LANE_CHUNKS // 2):
            out = jnp.dot(act, wd_ref[:, q * 2 * LANES:(q + 1) * 2 * LANES], preferred_element_type=F32)
            out = jnp.where(live_row, out, 0.0)
            for c in range(2):
                y_cur[pl.ds(2 * q + c, MOE_CHUNK, stride=LANE_CHUNKS), :] = out[:, c * LANES:(c + 1) * LANES]
            side_work(2 + q)

    @pl.when(e % 2 == 0)
    def _():
        run_expert(xg_a, y_a, xg_b, y_b)

    @pl.when(e % 2 == 1)
    def _():
        run_expert(xg_b, y_b, xg_a, y_a)

    def extra_chunk(c, carry):
        first = off + c * MOE_CHUNK
        gather_loop(xg_c, first)
        swiglu(xg_c, y_c)
        scatter_loop(y_c, first, jnp.minimum(MOE_CHUNK, n - c * MOE_CHUNK))
        return carry

    lax.fori_loop(1, (n + MOE_CHUNK - 1) // MOE_CHUNK, extra_chunk, 0)

    @pl.when(e == N_EXPERTS - 1)
    def _():
        scatter_loop(y_b, off, jnp.minimum(MOE_CHUNK, n))
        for c in range(MOE_TILE // MIX_ROWS):
            z = _load_row_tiles(acc_s, MIX_ROWS, c * MIX_ROWS * LANE_CHUNKS)
            stage_s[...] = _layer_norm(z, lw_ref[...], lb_ref[...])
            pltpu.sync_copy(stage_s, o_hbm.at[pl.ds(tile * MOE_TILE + c * MIX_ROWS, MIX_ROWS), :])


def _moe_tiles(x1, base, tok_list, gate, cnt, off, wg, wu, wd, ln_w, ln_b):
    T = x1.shape[0] // LANE_CHUNKS
    w_spec = lambda shape: pl.BlockSpec((None,) + shape, lambda i, e, cnt, off: (e, 0, 0))
    vec = pl.BlockSpec((1, D_MODEL), lambda i, e, cnt, off: (0, 0))
    hbm = pl.BlockSpec(memory_space=pl.ANY)
    tile_rows = MOE_TILE * LANE_CHUNKS
    return pl.pallas_call(
        _moe_tile_kernel,
        grid_spec=pltpu.PrefetchScalarGridSpec(
            num_scalar_prefetch=2,
            grid=(T // MOE_TILE, N_EXPERTS),
            in_specs=[pl.BlockSpec((LIST_LEN,), lambda i, e, cnt, off: (i,), memory_space=pltpu.SMEM),
                      pl.BlockSpec((MOE_TILE * TOP_K,), lambda i, e, cnt, off: (i,), memory_space=pltpu.SMEM),
                      w_spec((D_MODEL, EXPERT_FF)), w_spec((D_MODEL, EXPERT_FF)), w_spec((EXPERT_FF, D_MODEL)),
                      vec, vec, hbm, hbm],
            out_specs=hbm,
            scratch_shapes=[pltpu.VMEM((tile_rows, LANES), F32),
                            pltpu.VMEM((tile_rows, LANES), F32)]
                           + [pltpu.VMEM((MOE_CHUNK * LANE_CHUNKS, LANES), F32)] * 6
                           + [pltpu.VMEM((MIX_ROWS, D_MODEL), F32)],
        ),
        out_shape=jax.ShapeDtypeStruct((T, D_MODEL), F32),
        compiler_params=_cparams("arbitrary", "arbitrary"),
        name="moe_tiles_ln2",
    )(cnt, off, tok_list, gate, wg, wu, wd, ln_w, ln_b, x1, base)


def kernel(x, p, w_in, hgrn_lb_logits, hgrn_norm_w, w_branch_att, w_branch_hgrn, w_out, ln1_w, ln1_b, router_w, router_bias, expert_w_gate, expert_w_up, expert_w_down, shared_w_gate, shared_w_up, shared_w_down, ple_gate_w, ple_proj_w, ln2_w, ln2_b):
    B, S, D = x.shape
    T = B * S
    l = 0
    x2d = x.reshape(T, D)
    bf = lambda a: a.astype(BF16)

    ws = _att_weights(w_in[l])
    qkv = [_proj_att(x2d, ws[g], d) for g, d in enumerate(ATT_DILATIONS)]
    y_att = _attention(qkv, B, S)
    u_hg = _proj(x2d, bf(w_in[l][:, 3 * len(ATT_DILATIONS) * ATT_WIDTH:]), 1536)
    y_hg = _hgrn(u_hg, hgrn_lb_logits, hgrn_norm_w[l:l + 1], B, S)
    x1 = _merge(y_att, y_hg, u_hg, x2d, bf(w_branch_att[l]), bf(w_branch_hgrn[l]), bf(w_out[l]),
                ln1_w[l:l + 1], ln1_b[l:l + 1])

    base, idx, gate, rank, counts = _route(
        x1, p[l].reshape(T, PLE_DIM), router_w[l].T, router_bias[l].reshape(N_EXPERTS, 1),
        bf(shared_w_gate[l]), bf(shared_w_up[l]), bf(shared_w_down[l]), bf(ple_gate_w[l]), bf(ple_proj_w[l]))
    cnt = counts[:, :, 0]
    off = jnp.cumsum(cnt, axis=1) - cnt
    cnt, off = cnt.reshape(-1), off.reshape(-1)
    tok_list = _plan(_dest(off, idx, rank).T.reshape(-1))
    gate_list = gate.T.reshape(-1)
    out = _moe_tiles(x1, base, tok_list, gate_list, cnt, off, bf(expert_w_gate[l]), bf(expert_w_up[l]),
                     bf(expert_w_down[l]), ln2_w[l:l + 1], ln2_b[l:l + 1])
    return out.reshape(B, S, D)
```

```python
import functools

import jax
import jax.numpy as jnp
import numpy as np
from jax import lax
from jax.experimental import pallas as pl
from jax.experimental.pallas import tpu as pltpu
from jax.experimental.pallas import tpu_sc as plsc

F32 = jnp.float32
BF16 = jnp.bfloat16

D_MODEL = 1024
ATT_HEAD_DIM = 64
ATT_HEADS = 8
ATT_DILATIONS = (1, 4, 16)
ATT_BLOCK = 128
ATT_WIDTH = ATT_HEADS * ATT_HEAD_DIM
ATT_TILE = ATT_BLOCK * max(ATT_DILATIONS)
NEG_INF = -1e30

HG_HEADS = 8
HG_DIM = 128
HG_WIDTH = HG_HEADS * HG_DIM
HG_CHUNK = 32
HG_TILE = 256
RMS_EPS = 1e-6

N_EXPERTS = 64
TOP_K = 8
TOP_K_BITS = 3
N_GROUPS = 8
GROUP_SIZE = N_EXPERTS // N_GROUPS
TOPK_GROUPS = 4
EXPERT_FF = 256
ROUTED_SCALE = 2.5
PLE_DIM = 256
LN_EPS = 1e-5
DEPTH = 1
DEEPNORM_ALPHA = (2.0 * DEPTH) ** 0.25

LANES = 128
LANE_CHUNKS = D_MODEL // LANES
PROJ_ROWS = 512
ATT_PROJ_ROWS = 1024
MIX_ROWS = 512
MOE_TILE = 4096
MOE_CHUNK = 576
PLAN_CODES = MOE_TILE * TOP_K
PLAN_GROUP = 16
LIST_PAD = 1024
LIST_LEN = MOE_TILE * TOP_K + LIST_PAD
GATHER_GROUP = 8
SCATTER_GROUP = 8
V7X_VMEM_LIMIT = 56 * 1024 * 1024
SC_CORES, SC_SUBCORES, SC_LANES = 2, 16, 16


def _cparams(*sem):
    return pltpu.CompilerParams(dimension_semantics=sem, vmem_limit_bytes=V7X_VMEM_LIMIT)


def _proj_att_kernel(*refs, dil):
    x_refs, w_ref, o_ref = refs[:LANE_CHUNKS], refs[LANE_CHUNKS], refs[LANE_CHUNKS + 1]
    n = ATT_PROJ_ROWS // dil

    def rows(ref):
        if dil == 1:
            return ref[...]
        return jnp.concatenate([ref[pl.ds(r, n, stride=dil), :] for r in range(dil)], axis=0)

    xp = jnp.concatenate([rows(ref).astype(BF16) for ref in x_refs], axis=1)
    y = jnp.dot(xp, w_ref[...], preferred_element_type=F32)
    o_ref[...] = y.astype(BF16).reshape(dil, n, 3 * ATT_WIDTH)


def _proj_att(x2d, w, dil):
    T = x2d.shape[0]
    per = ATT_TILE // ATT_PROJ_ROWS
    n = ATT_PROJ_ROWS // dil
    out = pl.pallas_call(
        functools.partial(_proj_att_kernel, dil=dil),
        grid=(T // ATT_PROJ_ROWS,),
        in_specs=[pl.BlockSpec((ATT_PROJ_ROWS, LANES), functools.partial(lambda i, c: (i, c), c=c))
                  for c in range(LANE_CHUNKS)]
                 + [pl.BlockSpec((D_MODEL, 3 * ATT_WIDTH), lambda i: (0, 0))],
        out_specs=pl.BlockSpec((None, dil, None, n, 3 * ATT_WIDTH), lambda i: (i // per, 0, i % per, 0, 0)),
        out_shape=jax.ShapeDtypeStruct((T // ATT_TILE, dil, per, n, 3 * ATT_WIDTH), BF16),
        compiler_params=_cparams("parallel"),
        name=f"proj_att_d{dil}",
    )(*([x2d] * LANE_CHUNKS), w)
    return out.reshape(T // ATT_TILE, dil, ATT_TILE // dil, 3 * ATT_WIDTH)


def _att_pair(q2, kp, kc, vp, vc, bias_ref, g, first):
    def head0_lanes(rows, dtype):
        lane = lax.broadcasted_iota(jnp.int32, (rows, 2 * ATT_HEAD_DIM), 1)
        return lane.astype(F32).astype(dtype) < ATT_HEAD_DIM

    lo_q = head0_lanes(ATT_BLOCK, BF16)
    lo_v = head0_lanes(2 * ATT_BLOCK, BF16)
    k2 = jnp.concatenate([kp, kc], axis=0)
    v2 = jnp.concatenate([vp, vc], axis=0)
    zero = jnp.zeros_like(q2)
    ps, ms = [], []
    for hh in range(2):
        qm = jnp.where(lo_q, q2, zero) if hh == 0 else jnp.where(lo_q, zero, q2)
        s = lax.dot_general(qm, k2, (((1,), (1,)), ((), ())), preferred_element_type=F32)
        s = s + bias_ref[g, hh, first]
        m = jnp.max(s, axis=-1, keepdims=True)
        ps.append(jnp.exp(s - m).astype(BF16))
        ms.append(m)
    pcat = jnp.concatenate(ps, axis=1)
    zero_v, one_v = jnp.zeros_like(v2), jnp.ones_like(v2)
    rhs = jnp.concatenate([
        jnp.concatenate([jnp.where(lo_v, v2, zero_v), jnp.where(lo_v, one_v, zero_v)], axis=1),
        jnp.concatenate([jnp.where(lo_v, zero_v, v2), jnp.where(lo_v, zero_v, one_v)], axis=1)], axis=0)
    nd = jnp.dot(pcat, rhs, preferred_element_type=F32)
    m2 = jnp.where(head0_lanes(ATT_BLOCK, F32), ms[0], ms[1])
    return nd[:, :2 * ATT_HEAD_DIM], m2, nd[:, 2 * ATT_HEAD_DIM:]


def _att_kernel(*refs):
    (q0, kc0, vc0, kp0, vp0, q1, kc1, vc1, kp1, vp1, q2, kc2, vc2, kp2, vp2,
     bias_ref, o_ref) = refs[:17]
    ng = len(ATT_DILATIONS)
    num_s, m_s, den_s = refs[17:17 + ng], refs[17 + ng:17 + 2 * ng], refs[17 + 2 * ng:]
    first_tile = (pl.program_id(2) == 0).astype(jnp.int32)
    groups = ((q0, kc0, vc0, kp0, vp0), (q1, kc1, vc1, kp1, vp1), (q2, kc2, vc2, kp2, vp2))
    for g, dil in enumerate(ATT_DILATIONS):
        q_ref, kc_ref, vc_ref, kp_ref, vp_ref = groups[g]
        nb = ATT_TILE // dil // ATT_BLOCK
        for r in range(dil):
            for n in range(nb):
                rows = pl.ds(n * ATT_BLOCK, ATT_BLOCK)
                if n == 0:
                    prev = pl.ds((nb - 1) * ATT_BLOCK, ATT_BLOCK)
                    kp, vp, first = kp_ref[r, prev, :], vp_ref[r, prev, :], first_tile
                else:
                    prev = pl.ds((n - 1) * ATT_BLOCK, ATT_BLOCK)
                    kp, vp, first = kc_ref[r, prev, :], vc_ref[r, prev, :], 0
                num, m, den = _att_pair(q_ref[r, rows, :], kp, kc_ref[r, rows, :], vp, vc_ref[r, rows, :],
                                        bias_ref, g, first)
                if dil == 1:
                    dst = rows
                else:
                    dst = pl.ds(n * ATT_BLOCK * dil + r, ATT_BLOCK, stride=dil)
                num_s[g][dst, :] = num
                m_s[g][dst, :] = m
                den_s[g][dst, :] = den
    m_all = jnp.maximum(jnp.maximum(m_s[0][...], m_s[1][...]), m_s[2][...])
    num = jnp.zeros((ATT_TILE, 2 * ATT_HEAD_DIM), F32)
    den = jnp.zeros((ATT_TILE, 2 * ATT_HEAD_DIM), F32)
    for g in range(ng):
        sc = jnp.exp(m_s[g][...] - m_all)
        num = num + sc * num_s[g][...]
        den = den + sc * den_s[g][...]
    o_ref[...] = (num / den).astype(o_ref.dtype)


def _att_bias_table():
    qi = np.arange(ATT_BLOCK)[:, None]
    ki = np.arange(2 * ATT_BLOCK)[None, :]
    steps = qi + ATT_BLOCK - ki
    valid = (steps >= 0) & (steps <= ATT_BLOCK)
    slopes = np.array([2.0 ** (-8.0 * (h + 1) / ATT_HEADS) for h in range(ATT_HEADS)], np.float32)
    tab = np.empty((len(ATT_DILATIONS), ATT_HEADS, 2, ATT_BLOCK, 2 * ATT_BLOCK), np.float32)
    for g, dil in enumerate(ATT_DILATIONS):
        bias = -slopes[:, None, None] * (steps * dil).astype(np.float32)[None]
        tab[g, :, 0] = np.where(valid[None], bias, NEG_INF)
        tab[g, :, 1] = np.where((valid & (ki >= ATT_BLOCK))[None], bias, NEG_INF)
    return jnp.asarray(tab)


def _attention(qkv, B, S):
    tiles = S // ATT_TILE
    pair = 2 * ATT_HEAD_DIM
    npair = ATT_WIDTH // pair
    in_specs, args = [], []
    for g, dil in enumerate(ATT_DILATIONS):
        blk = (None, dil, ATT_TILE // dil, pair)
        cur = lambda b, hp, t, off: (b * tiles + t, 0, 0, off * npair + hp)
        prv = lambda b, hp, t, off: (b * tiles + jnp.maximum(t - 1, 0), 0, 0, off * npair + hp)
        in_specs += [pl.BlockSpec(blk, functools.partial(cur, off=0)),
                     pl.BlockSpec(blk, functools.partial(cur, off=1)),
                     pl.BlockSpec(blk, functools.partial(cur, off=2)),
                     pl.BlockSpec(blk, functools.partial(prv, off=1)),
                     pl.BlockSpec(blk, functools.partial(prv, off=2))]
        args += [qkv[g]] * 5
    in_specs.append(pl.BlockSpec((len(ATT_DILATIONS), 2, 2, ATT_BLOCK, 2 * ATT_BLOCK),
                                 lambda b, hp, t: (0, hp, 0, 0, 0)))
    args.append(_att_bias_table())
    scratch = [pltpu.VMEM((ATT_TILE, pair), F32) for _ in range(3 * len(ATT_DILATIONS))]
    return pl.pallas_call(
        _att_kernel,
        grid=(B, npair, tiles),
        in_specs=in_specs,
        out_specs=pl.BlockSpec((ATT_TILE, pair), lambda b, hp, t: (b * tiles + t, hp)),
        out_shape=jax.ShapeDtypeStruct((B * S, ATT_WIDTH), BF16),
        scratch_shapes=scratch,
        compiler_params=_cparams("parallel", "parallel", "arbitrary"),
        name="dilated_attention",
    )(*args)


def _att_weights(w_in_l):
    out = []
    width = len(ATT_DILATIONS) * ATT_WIDTH
    for g in range(len(ATT_DILATIONS)):
        cols = [w_in_l[:, part * width + g * ATT_WIDTH: part * width + (g + 1) * ATT_WIDTH] for part in range(3)]
        cols[0] = cols[0] * (ATT_HEAD_DIM ** -0.5)
        out.append(jnp.concatenate(cols, axis=1).astype(BF16))
    return out


def _proj_kernel(x_ref, w_ref, o_ref, *, col_tile):
    xb = x_ref[...].astype(BF16)
    for c in range(w_ref.shape[1] // col_tile):
        cols = slice(c * col_tile, (c + 1) * col_tile)
        o_ref[:, cols] = jnp.dot(xb, w_ref[:, cols], preferred_element_type=F32).astype(o_ref.dtype)


def _proj(x2d, w, col_tile):
    T, N = x2d.shape[0], w.shape[1]
    return pl.pallas_call(
        functools.partial(_proj_kernel, col_tile=col_tile),
        grid=(T // PROJ_ROWS,),
        in_specs=[pl.BlockSpec((PROJ_ROWS, D_MODEL), lambda i: (i, 0)),
                  pl.BlockSpec((D_MODEL, N), lambda i: (0, 0))],
        out_specs=pl.BlockSpec((PROJ_ROWS, N), lambda i: (i, 0)),
        out_shape=jax.ShapeDtypeStruct((T, N), BF16),
        compiler_params=_cparams("parallel"),
        name="proj_hgrn_gates",
    )(x2d, w)


def _split3(v):
    a = v.astype(BF16)
    r = v - a.astype(F32)
    b = r.astype(BF16)
    c = (r - b.astype(F32)).astype(BF16)
    return a, b, c


def _hgrn_kernel(q_ref, f_ref, i_ref, g_ref, lbl_ref, gain_ref, o_ref, state_ref):
    @pl.when(pl.program_id(1) == 0)
    def _():
        state_ref[...] = jnp.zeros_like(state_ref)

    lbl = lbl_ref[...]
    e = jnp.exp(lbl - jnp.max(lbl, axis=0, keepdims=True))
    lb = e[0:1] / jnp.sum(e, axis=0, keepdims=True)
    forget = lb + (1.0 - lb) * jax.nn.sigmoid(f_ref[...].astype(F32))
    log_f = jnp.log(forget)
    key = 1.0 - forget

    row = lax.broadcasted_iota(jnp.int32, (HG_TILE, HG_TILE), 0)
    col = lax.broadcasted_iota(jnp.int32, (HG_TILE, HG_TILE), 1)
    causal = (row >= col) & ((row // HG_CHUNK) == (col // HG_CHUNK))
    tri = jnp.where(causal, 1.0, 0.0).astype(BF16)
    b = sum(jnp.dot(tri, t, preferred_element_type=F32) for t in _split3(log_f))
    eb = jnp.exp(b)
    q_dec = (q_ref[...].astype(F32) * eb).astype(BF16)
    k_inv = key * jnp.exp(-b)
    xi = i_ref[...].astype(F32)
    val = (xi * jax.nn.sigmoid(xi)).astype(BF16)
    k_inv_b = k_inv.astype(BF16)

    n_chunks = HG_TILE // HG_CHUNK
    last_rows = [eb[(c + 1) * HG_CHUNK - 1:(c + 1) * HG_CHUNK, :] for c in range(n_chunks)]
    dec_rows = jnp.concatenate([jnp.broadcast_to(r, (HG_CHUNK, HG_WIDTH)) for r in last_rows], axis=0)
    k_end = (k_inv * dec_rows).astype(BF16)
    chunk_of_row = (lax.broadcasted_iota(jnp.int32, (HG_TILE, HG_DIM), 0) // HG_CHUNK).astype(F32).astype(BF16)
    in_chunk = [chunk_of_row == c for c in range(n_chunks)]
    zero = jnp.zeros((HG_TILE, HG_DIM), BF16)

    def per_chunk_columns(t):
        return jnp.concatenate([jnp.where(m, t, zero) for m in in_chunk], axis=1)

    outs = []
    for h in range(HG_HEADS):
        cols = slice(h * HG_DIM, (h + 1) * HG_DIM)
        qd, ki, vv = q_dec[:, cols], k_inv_b[:, cols], val[:, cols]
        a = lax.dot_general(qd, ki, (((1,), (1,)), ((), ())), preferred_element_type=F32)
        a = jnp.where(causal, a, 0.0).astype(BF16)
        o_intra = jnp.dot(a, vv, preferred_element_type=F32)
        upd = lax.dot_general(vv, per_chunk_columns(k_end[:, cols]), (((0,), (0,)), ((), ())),
                              preferred_element_type=F32)
        st = state_ref[h]
        entering = []
        for c in range(n_chunks):
            entering.append(st.astype(BF16))
            st = st * last_rows[c][:, cols] + upd[:, c * HG_DIM:(c + 1) * HG_DIM]
        state_ref[h] = st
        o_inter = lax.dot_general(per_chunk_columns(qd), jnp.concatenate(entering, axis=1),
                                  (((1,), (1,)), ((), ())), preferred_element_type=F32)
        o = o_intra + o_inter
        o = o * lax.rsqrt(jnp.mean(jnp.square(o), axis=-1, keepdims=True) + RMS_EPS)
        outs.append(o)
    o = jnp.concatenate(outs, axis=1) * gain_ref[...]
    gg = g_ref[...].astype(F32)
    o_ref[...] = (o * (gg * jax.nn.sigmoid(gg))).astype(o_ref.dtype)


def _hgrn(u_hg, lb_logits, gain, B, S):
    tiles = S // HG_TILE
    col = lambda j: pl.BlockSpec((HG_TILE, HG_WIDTH), functools.partial(lambda b, t, j: (b * tiles + t, j), j=j))
    return pl.pallas_call(
        _hgrn_kernel,
        grid=(B, tiles),
        in_specs=[col(0), col(1), col(2), col(3),
                  pl.BlockSpec((2, HG_WIDTH), lambda b, t: (0, 0)),
                  pl.BlockSpec((1, HG_WIDTH), lambda b, t: (0, 0))],
        out_specs=pl.BlockSpec((HG_TILE, HG_WIDTH), lambda b, t: (b * tiles + t, 0)),
        out_shape=jax.ShapeDtypeStruct((B * S, HG_WIDTH), BF16),
        scratch_shapes=[pltpu.VMEM((HG_HEADS, HG_DIM, HG_DIM), F32)],
        compiler_params=_cparams("parallel", "arbitrary"),
        name="hgrn2",
    )(u_hg, u_hg, u_hg, u_hg, lb_logits, gain)


def _load_row_tiles(ref, n, start=0):
    return jnp.concatenate([ref[pl.ds(start + c, n, stride=LANE_CHUNKS), :] for c in range(LANE_CHUNKS)], axis=1)


def _store_row_tiles(ref, val, n):
    for c in range(LANE_CHUNKS):
        ref[pl.ds(c, n, stride=LANE_CHUNKS), :] = val[:, c * LANES:(c + 1) * LANES]


def _layer_norm(z, w, b):
    mu = jnp.mean(z, axis=-1, keepdims=True)
    zc = z - mu
    var = jnp.mean(jnp.square(zc), axis=-1, keepdims=True)
    return zc * lax.rsqrt(var + LN_EPS) * w + b


def _merge_kernel(ya_ref, yh_ref, ga_ref, gh_ref, x_ref, wa_ref, wh_ref, wo_ref, lw_ref, lb_ref, o_ref):
    ma = jnp.dot(ya_ref[...], wa_ref[...], preferred_element_type=F32)
    mh = jnp.dot(yh_ref[...], wh_ref[...], preferred_element_type=F32)
    merged = (jax.nn.sigmoid(ga_ref[...].astype(F32)) * ma + jax.nn.sigmoid(gh_ref[...].astype(F32)) * mh)
    z = DEEPNORM_ALPHA * x_ref[...] + jnp.dot(merged.astype(BF16), wo_ref[...], preferred_element_type=F32)
    _store_row_tiles(o_ref, _layer_norm(z, lw_ref[...], lb_ref[...]), MIX_ROWS)


def _merge(y_att, y_hg, u_hg, x2d, w_a, w_h, w_o, ln_w, ln_b):
    T = x2d.shape[0]
    rows = lambda width, j=0: pl.BlockSpec((MIX_ROWS, width), functools.partial(lambda i, j: (i, j), j=j))
    full = lambda a: pl.BlockSpec(a.shape, lambda i: (0, 0))
    return pl.pallas_call(
        _merge_kernel,
        grid=(T // MIX_ROWS,),
        in_specs=[rows(ATT_WIDTH), rows(HG_WIDTH), rows(D_MODEL, 4), rows(D_MODEL, 5), rows(D_MODEL),
                  full(w_a), full(w_h), full(w_o), full(ln_w), full(ln_b)],
        out_specs=pl.BlockSpec((MIX_ROWS * LANE_CHUNKS, LANES), lambda i: (i, 0)),
        out_shape=jax.ShapeDtypeStruct((T * LANE_CHUNKS, LANES), F32),
        compiler_params=_cparams("parallel"),
        name="merge_ln1",
    )(y_att, y_hg, u_hg, u_hg, x2d, w_a, w_h, w_o, ln_w, ln_b)


def _first_argmax(v, ids, n):
    mx = jnp.max(v, axis=0, keepdims=True)
    return mx, jnp.min(jnp.where(v == mx, ids, n), axis=0, keepdims=True)


def _route_kernel(x1_ref, p_ref, wrt_ref, rb_ref, wsg_ref, wsu_ref, wsd_ref, wpg_ref, wpp_ref,
                  base_ref, idx_ref, gate_ref, rank_ref, cnt_ref, carry_ref):
    @pl.when(pl.program_id(0) % (MOE_TILE // MIX_ROWS) == 0)
    def _():
        carry_ref[...] = jnp.zeros_like(carry_ref)

    x1 = _load_row_tiles(x1_ref, MIX_ROWS)
    x1b = x1.astype(BF16)
    logits = lax.dot_general(wrt_ref[...], x1, (((1,), (1,)), ((), ())), preferred_element_type=F32,
                             precision=lax.Precision.HIGHEST)
    s = jax.nn.sigmoid(logits)
    sel = s + rb_ref[...]
    eid = lax.broadcasted_iota(jnp.int32, (N_EXPERTS, MIX_ROWS), 0)
    neg = -jnp.inf

    grp = sel.reshape(N_GROUPS, GROUP_SIZE, MIX_ROWS)
    mid = lax.broadcasted_iota(jnp.int32, grp.shape, 1)
    m1 = jnp.max(grp, axis=1, keepdims=True)
    i1 = jnp.min(jnp.where(grp == m1, mid, GROUP_SIZE), axis=1, keepdims=True)
    m2 = jnp.max(jnp.where(mid == i1, neg, grp), axis=1, keepdims=True)
    gscore = (m1 + m2).reshape(N_GROUPS, MIX_ROWS)
    gid = lax.broadcasted_iota(jnp.int32, (N_GROUPS, MIX_ROWS), 0)
    gsel = jnp.zeros((N_GROUPS, MIX_ROWS), jnp.bool_)
    for _ in range(TOPK_GROUPS):
        _, gi = _first_argmax(gscore, gid, N_GROUPS)
        hit = gid == gi
        gsel = gsel | hit
        gscore = jnp.where(hit, neg, gscore)
    emask = jnp.broadcast_to(gsel.reshape(N_GROUPS, 1, MIX_ROWS), grp.shape).reshape(N_EXPERTS, MIX_ROWS)
    cand = jnp.where(emask, sel, neg)

    idxs, gates = [], []
    chosen = jnp.zeros((N_EXPERTS, MIX_ROWS), jnp.bool_)
    for _ in range(TOP_K):
        _, ei = _first_argmax(cand, eid, N_EXPERTS)
        hit = eid == ei
        idxs.append(ei)
        gates.append(jnp.sum(jnp.where(hit, s, 0.0), axis=0, keepdims=True))
        chosen = chosen | hit
        cand = jnp.where(hit, neg, cand)
    g = jnp.concatenate(gates, axis=0)
    g = g / jnp.sum(g, axis=0, keepdims=True) * ROUTED_SCALE
    idx_ref[...] = jnp.concatenate(idxs, axis=0)
    gate_ref[...] = g

    onehot = jnp.where(chosen, 1.0, 0.0)
    tr = lax.broadcasted_iota(jnp.int32, (MIX_ROWS, MIX_ROWS), 0)
    tc = lax.broadcasted_iota(jnp.int32, (MIX_ROWS, MIX_ROWS), 1)
    before = jnp.where(tr < tc, 1.0, 0.0).astype(BF16)
    prefix = jnp.dot(onehot.astype(BF16), before, preferred_element_type=F32)
    rankfull = (carry_ref[:, 0:1] + prefix).astype(jnp.int32)
    rank_ref[...] = jnp.concatenate(
        [jnp.sum(jnp.where(eid == ei, rankfull, 0), axis=0, keepdims=True) for ei in idxs], axis=0)
    total = carry_ref[...] + jnp.sum(onehot, axis=1, keepdims=True)
    carry_ref[...] = total
    cnt_ref[...] = total.astype(jnp.int32)

    hg = jnp.dot(x1b, wsg_ref[...], preferred_element_type=F32)
    hu = jnp.dot(x1b, wsu_ref[...], preferred_element_type=F32)
    shared = jnp.dot((hg * jax.nn.sigmoid(hg) * hu).astype(BF16), wsd_ref[...], preferred_element_type=F32)
    ple = (jax.nn.sigmoid(jnp.dot(x1b, wpg_ref[...], preferred_element_type=F32))
           * jnp.dot(p_ref[...].astype(BF16), wpp_ref[...], preferred_element_type=F32))
    _store_row_tiles(base_ref, DEEPNORM_ALPHA * x1 + shared + ple, MIX_ROWS)


def _route(x1, p2d, wr_t, rbias, wsg, wsu, wsd, wpg, wpp):
    T = x1.shape[0] // LANE_CHUNKS
    per_tile = MOE_TILE // MIX_ROWS
    full = lambda a: pl.BlockSpec(a.shape, lambda i: (0, 0))
    tok = pl.BlockSpec((TOP_K, MIX_ROWS), lambda i: (0, i))
    row_tiles = pl.BlockSpec((MIX_ROWS * LANE_CHUNKS, LANES), lambda i: (i, 0))
    return pl.pallas_call(
        _route_kernel,
        grid=(T // MIX_ROWS,),
        in_specs=[row_tiles,
                  pl.BlockSpec((MIX_ROWS, PLE_DIM), lambda i: (i, 0)),
                  full(wr_t), full(rbias), full(wsg), full(wsu), full(wsd), full(wpg), full(wpp)],
        out_specs=[row_tiles, tok, tok, tok,
                   pl.BlockSpec((None, N_EXPERTS, LANES), lambda i: (i // per_tile, 0, 0))],
        out_shape=[jax.ShapeDtypeStruct((T * LANE_CHUNKS, LANES), F32),
                   jax.ShapeDtypeStruct((TOP_K, T), jnp.int32),
                   jax.ShapeDtypeStruct((TOP_K, T), F32),
                   jax.ShapeDtypeStruct((TOP_K, T), jnp.int32),
                   jax.ShapeDtypeStruct((T // MOE_TILE, N_EXPERTS, LANES), jnp.int32)],
        scratch_shapes=[pltpu.VMEM((N_EXPERTS, LANES), F32)],
        compiler_params=_cparams("arbitrary"),
        name="route_shared_ple",
    )(x1, p2d, wr_t, rbias, wsg, wsu, wsd, wpg, wpp)


def _dest_kernel(off_ref, idx_ref, rank_ref, dest_ref):
    idx = idx_ref[...]
    base = pl.program_id(0) * N_EXPERTS

    def body(e, acc):
        return acc + jnp.where(idx == e, off_ref[base + e], 0)

    dest_ref[...] = lax.fori_loop(0, N_EXPERTS, body, rank_ref[...])


def _dest(off, idx, rank):
    T = idx.shape[1]
    spec = pl.BlockSpec((TOP_K, MOE_TILE), lambda i, off: (0, i))
    return pl.pallas_call(
        _dest_kernel,
        grid_spec=pltpu.PrefetchScalarGridSpec(num_scalar_prefetch=1, grid=(T // MOE_TILE,),
                                               in_specs=[spec, spec], out_specs=spec),
        out_shape=jax.ShapeDtypeStruct(idx.shape, jnp.int32),
        compiler_params=_cparams("arbitrary"),
        name="moe_dest",
    )(off, idx, rank)


def _plan_kernel(dest_ref, list_ref):
    first = pl.program_id(1) * PLAN_CODES

    @pl.when(pl.program_id(1) == 0)
    def _():
        def pad(u, c):
            list_ref[MOE_TILE * TOP_K + u] = 0
            return c

        lax.fori_loop(0, LIST_PAD, pad, 0, unroll=8)

    def body(i, c):
        a = i * PLAN_GROUP
        dests = [dest_ref[a + u] for u in range(PLAN_GROUP)]
        for u, d in enumerate(dests):
            list_ref[d] = first + a + u
        return c

    lax.fori_loop(0, PLAN_CODES // PLAN_GROUP, body, 0)


def _plan(dest):
    n_tiles = dest.shape[0] // (MOE_TILE * TOP_K)
    per_tile = MOE_TILE * TOP_K // PLAN_CODES
    return pl.pallas_call(
        _plan_kernel,
        grid=(n_tiles, per_tile),
        in_specs=[pl.BlockSpec((PLAN_CODES,), lambda i, j: (i * per_tile + j,), memory_space=pltpu.SMEM)],
        out_specs=pl.BlockSpec((LIST_LEN,), lambda i, j: (i,), memory_space=pltpu.SMEM),
        out_shape=jax.ShapeDtypeStruct((n_tiles * LIST_LEN,), jnp.int32),
        compiler_params=_cparams("arbitrary", "arbitrary"),
        name="moe_plan",
    )(dest)


def _plan_sc_kernel(dest_hbm, list_hbm, dest_v, list_v):
    codes = MOE_TILE * TOP_K
    worker = lax.axis_index("subcore") * SC_CORES + lax.axis_index("core")

    @pl.when(worker < dest_hbm.shape[0] // codes)
    def _():
        pltpu.sync_copy(dest_hbm.at[pl.ds(worker * codes, codes)], dest_v)
        lane = lax.iota(jnp.int32, SC_LANES)

        @pl.loop(0, codes // SC_LANES)
        def _(i):
            at = i * SC_LANES
            plsc.store_scatter(list_v, [dest_v[pl.ds(at, SC_LANES)]], lane + at)

        @pl.loop(0, LIST_PAD // SC_LANES)
        def _(i):
            list_v[pl.ds(codes + i * SC_LANES, SC_LANES)] = jnp.zeros((SC_LANES,), jnp.int32)

        pltpu.sync_copy(list_v, list_hbm.at[pl.ds(worker * LIST_LEN, LIST_LEN)])


def _plan_sc(dest):
    n_tiles = dest.shape[0] // (MOE_TILE * TOP_K)
    assert n_tiles <= SC_CORES * SC_SUBCORES
    mesh = plsc.VectorSubcoreMesh(core_axis_name="core", subcore_axis_name="subcore",
                                  num_cores=SC_CORES, num_subcores=SC_SUBCORES)
    return pl.kernel(
        _plan_sc_kernel,
        out_type=jax.ShapeDtypeStruct((n_tiles * LIST_LEN,), jnp.int32),
        mesh=mesh,
        scratch_types=[pltpu.VMEM((MOE_TILE * TOP_K,), jnp.int32), pltpu.VMEM((LIST_LEN,), jnp.int32)],
        compiler_params=pltpu.CompilerParams(needs_layout_passes=False),
        name="moe_plan_sc",
    )(dest)


def _moe_tile_kernel(cnt_ref, off_ref, list_ref, gate_ref, wg_ref, wu_ref, wd_ref, lw_ref, lb_ref, x_hbm, base_hbm,
                     o_hbm, x_s, acc_s, xg_a, xg_b, xg_c, y_a, y_b, y_c, stage_s):
    tile, e = pl.program_id(0), pl.program_id(1)
    rows_of = lambda ref, r, n: ref.at[pl.ds(pl.multiple_of(r * LANE_CHUNKS, LANE_CHUNKS), n * LANE_CHUNKS), :]
    tile_rows = pl.ds(pl.multiple_of(tile * (MOE_TILE * LANE_CHUNKS), LANE_CHUNKS), MOE_TILE * LANE_CHUNKS)

    pair = tile * N_EXPERTS + e
    last_pair = pl.num_programs(0) * N_EXPERTS - 1
    n, off = cnt_ref[pair], off_ref[pair]
    first_row = lambda code: pl.multiple_of(code & -LANE_CHUNKS, LANE_CHUNKS)
    tile_at = lambda ref, r: ref.at[pl.ds(r, LANE_CHUNKS), :]

    def gather_group(xg, first, jb):
        at = first + jb * GATHER_GROUP
        rows = [tile_at(x_s, first_row(list_ref[at + u]))[...] for u in range(GATHER_GROUP)]
        rows_of(xg, jb * GATHER_GROUP, GATHER_GROUP)[...] = jnp.concatenate(rows, axis=0)

    def gather_loop(xg, first):
        def body(jb, cc):
            gather_group(xg, first, jb)
            return cc

        lax.fori_loop(0, MOE_CHUNK // GATHER_GROUP, body, 0)

    def swiglu(xg, y):
        xb = _load_row_tiles(xg, MOE_CHUNK).astype(BF16)
        hg = jnp.dot(xb, wg_ref[...], preferred_element_type=F32)
        hu = jnp.dot(xb, wu_ref[...], preferred_element_type=F32)
        act = (hg * jax.nn.sigmoid(hg) * hu).astype(BF16)
        _store_row_tiles(y, jnp.dot(act, wd_ref[...], preferred_element_type=F32), MOE_CHUNK)

    def scatter_group(y, first, j0, live):
        codes = [list_ref[first + j0 + u] for u in range(live)]
        gates = [gate_ref[code] for code in codes]
        dsts = [first_row(code) for code in codes]
        yv = rows_of(y, j0, live)[...]
        vals = [tile_at(acc_s, d)[...] + g * yv[u * LANE_CHUNKS:(u + 1) * LANE_CHUNKS]
                for u, (d, g) in enumerate(zip(dsts, gates))]
        for d, val in reversed(list(zip(dsts, vals))):
            tile_at(acc_s, d)[...] = val

    def scatter_loop(y, first, m):
        def body(jg, cc):
            scatter_group(y, first, jg * SCATTER_GROUP, SCATTER_GROUP)
            return cc

        lax.fori_loop(0, m // SCATTER_GROUP, body, 0)
        for live in range(1, SCATTER_GROUP):
            @pl.when(m % SCATTER_GROUP == live)
            def _(live=live):
                scatter_group(y, first, m - live, live)

    @pl.when(e == 0)
    def _():
        pltpu.sync_copy(x_hbm.at[tile_rows, :], x_s)
        pltpu.sync_copy(base_hbm.at[tile_rows, :], acc_s)
        y_b[...] = jnp.zeros_like(y_b)
        gather_loop(xg_a, off)

    prev_off = off_ref[jnp.maximum(pair - 1, 0)]
    next_off = off_ref[jnp.minimum(pair + 1, last_pair)]
    live_row = lax.broadcasted_iota(jnp.int32, (MOE_CHUNK, 2 * LANES), 0) < n

    def run_expert(xg_cur, y_cur, xg_nxt, y_prv):
        gathers = [functools.partial(gather_group, xg_nxt, next_off, jb) for jb in range(MOE_CHUNK // GATHER_GROUP)]
        scatters = [functools.partial(scatter_group, y_prv, prev_off, jg * SCATTER_GROUP, SCATTER_GROUP)
                    for jg in range(MOE_CHUNK // SCATTER_GROUP)]
        side = [s for both in zip(gathers, scatters) for s in both]
        pieces = 2 + LANE_CHUNKS // 2
        per_piece = -(-len(side) // pieces)

        def side_work(i):
            for s in side[i * per_piece:(i + 1) * per_piece]:
                s()

        xb = _load_row_tiles(xg_cur, MOE_CHUNK).astype(BF16)
        hg = jnp.dot(xb, wg_ref[...], preferred_element_type=F32)
        side_work(0)
        hu = jnp.dot(xb, wu_ref[...], preferred_element_type=F32)
        side_work(1)
        act = (hg * jax.nn.sigmoid(hg) * hu).astype(BF16)
        for q in range(LANE_CHUNKS // 2):
            out = jnp.dot(act, wd_ref[:, q * 2 * LANES:(q + 1) * 2 * LANES], preferred_element_type=F32)
            out = jnp.where(live_row, out, 0.0)
            for c in range(2):
                y_cur[pl.ds(2 * q + c, MOE_CHUNK, stride=LANE_CHUNKS), :] = out[:, c * LANES:(c + 1) * LANES]
            side_work(2 + q)

    @pl.when(e % 2 == 0)
    def _():
        run_expert(xg_a, y_a, xg_b, y_b)

    @pl.when(e % 2 == 1)
    def _():
        run_expert(xg_b, y_b, xg_a, y_a)

    def extra_chunk(c, carry):
        first = off + c * MOE_CHUNK
        gather_loop(xg_c, first)
        swiglu(xg_c, y_c)
        scatter_loop(y_c, first, jnp.minimum(MOE_CHUNK, n - c * MOE_CHUNK))
        return carry

    lax.fori_loop(1, (n + MOE_CHUNK - 1) // MOE_CHUNK, extra_chunk, 0)

    @pl.when(e == N_EXPERTS - 1)
    def _():
        scatter_loop(y_b, off, jnp.minimum(MOE_CHUNK, n))
        for c in range(MOE_TILE // MIX_ROWS):
            z = _load_row_tiles(acc_s, MIX_ROWS, c * MIX_ROWS * LANE_CHUNKS)
            stage_s[...] = _layer_norm(z, lw_ref[...], lb_ref[...])
            pltpu.sync_copy(stage_s, o_hbm.at[pl.ds(tile * MOE_TILE + c * MIX_ROWS, MIX_ROWS), :])


def _moe_tiles(x1, base, tok_list, gate, cnt, off, wg, wu, wd, ln_w, ln_b):
    T = x1.shape[0] // LANE_CHUNKS
    w_spec = lambda shape: pl.BlockSpec((None,) + shape, lambda i, e, cnt, off: (e, 0, 0))
    vec = pl.BlockSpec((1, D_MODEL), lambda i, e, cnt, off: (0, 0))
    hbm = pl.BlockSpec(memory_space=pl.ANY)
    tile_rows = MOE_TILE * LANE_CHUNKS
    return pl.pallas_call(
        _moe_tile_kernel,
        grid_spec=pltpu.PrefetchScalarGridSpec(
            num_scalar_prefetch=2,
            grid=(T // MOE_TILE, N_EXPERTS),
            in_specs=[pl.BlockSpec((LIST_LEN,), lambda i, e, cnt, off: (i,), memory_space=pltpu.SMEM),
                      pl.BlockSpec((MOE_TILE * TOP_K,), lambda i, e, cnt, off: (i,), memory_space=pltpu.SMEM),
                      w_spec((D_MODEL, EXPERT_FF)), w_spec((D_MODEL, EXPERT_FF)), w_spec((EXPERT_FF, D_MODEL)),
                      vec, vec, hbm, hbm],
            out_specs=hbm,
            scratch_shapes=[pltpu.VMEM((tile_rows, LANES), F32),
                            pltpu.VMEM((tile_rows, LANES), F32)]
                           + [pltpu.VMEM((MOE_CHUNK * LANE_CHUNKS, LANES), F32)] * 6
                           + [pltpu.VMEM((MIX_ROWS, D_MODEL), F32)],
        ),
        out_shape=jax.ShapeDtypeStruct((T, D_MODEL), F32),
        compiler_params=_cparams("arbitrary", "arbitrary"),
        name="moe_tiles_ln2",
    )(cnt, off, tok_list, gate, wg, wu, wd, ln_w, ln_b, x1, base)


def kernel(x, p, w_in, hgrn_lb_logits, hgrn_norm_w, w_branch_att, w_branch_hgrn, w_out, ln1_w, ln1_b, router_w, router_bias, expert_w_gate, expert_w_up, expert_w_down, shared_w_gate, shared_w_up, shared_w_down, ple_gate_w, ple_proj_w, ln2_w, ln2_b):
    B, S, D = x.shape
    T = B * S
    l = 0
    x2d = x.reshape(T, D)
    bf = lambda a: a.astype(BF16)

    ws = _att_weights(w_in[l])
    qkv = [_proj_att(x2d, ws[g], d) for g, d in enumerate(ATT_DILATIONS)]
    y_att = _attention(qkv, B, S)
    u_hg = _proj(x2d, bf(w_in[l][:, 3 * len(ATT_DILATIONS) * ATT_WIDTH:]), 1536)
    y_hg = _hgrn(u_hg, hgrn_lb_logits, hgrn_norm_w[l:l + 1], B, S)
    x1 = _merge(y_att, y_hg, u_hg, x2d, bf(w_branch_att[l]), bf(w_branch_hgrn[l]), bf(w_out[l]),
                ln1_w[l:l + 1], ln1_b[l:l + 1])

    base, idx, gate, rank, counts = _route(
        x1, p[l].reshape(T, PLE_DIM), router_w[l].T, router_bias[l].reshape(N_EXPERTS, 1),
        bf(shared_w_gate[l]), bf(shared_w_up[l]), bf(shared_w_down[l]), bf(ple_gate_w[l]), bf(ple_proj_w[l]))
    cnt = counts[:, :, 0]
    off = jnp.cumsum(cnt, axis=1) - cnt
    cnt, off = cnt.reshape(-1), off.reshape(-1)
    tok_list = _plan_sc(_dest(off, idx, rank).T.reshape(-1))
    gate_list = gate.T.reshape(-1)
    out = _moe_tiles(x1, base, tok_list, gate_list, cnt, off, bf(expert_w_gate[l]), bf(expert_w_up[l]),
                     bf(expert_w_down[l]), ln2_w[l:l + 1], ln2_b[l:l + 1])
    return out.reshape(B, S, D)
```

```python
import functools

import jax
import jax.numpy as jnp
import numpy as np
from jax import lax
from jax.experimental import pallas as pl
from jax.experimental.pallas import tpu as pltpu
from jax.experimental.pallas import tpu_sc as plsc

F32 = jnp.float32
BF16 = jnp.bfloat16

D_MODEL = 1024
ATT_HEAD_DIM = 64
ATT_HEADS = 8
ATT_DILATIONS = (1, 4, 16)
ATT_BLOCK = 128
ATT_WIDTH = ATT_HEADS * ATT_HEAD_DIM
ATT_TILE = ATT_BLOCK * max(ATT_DILATIONS)
NEG_INF = -1e30

HG_HEADS = 8
HG_DIM = 128
HG_WIDTH = HG_HEADS * HG_DIM
HG_CHUNK = 32
HG_TILE = 256
RMS_EPS = 1e-6

N_EXPERTS = 64
TOP_K = 8
TOP_K_BITS = 3
N_GROUPS = 8
GROUP_SIZE = N_EXPERTS // N_GROUPS
TOPK_GROUPS = 4
EXPERT_FF = 256
ROUTED_SCALE = 2.5
PLE_DIM = 256
LN_EPS = 1e-5
DEPTH = 1
DEEPNORM_ALPHA = (2.0 * DEPTH) ** 0.25

LANES = 128
LANE_CHUNKS = D_MODEL // LANES
PROJ_ROWS = 512
ATT_PROJ_ROWS = 1024
MIX_ROWS = 512
MOE_TILE = 4096
MOE_CHUNK = 576
PLAN_CODES = MOE_TILE * TOP_K
PLAN_GROUP = 16
LIST_PAD = 1024
LIST_LEN = MOE_TILE * TOP_K + LIST_PAD
GATHER_GROUP = 8
SCATTER_GROUP = 8
V7X_VMEM_LIMIT = 56 * 1024 * 1024
SC_CORES, SC_SUBCORES, SC_LANES = 2, 16, 16


def _cparams(*sem):
    return pltpu.CompilerParams(dimension_semantics=sem, vmem_limit_bytes=V7X_VMEM_LIMIT)


def _proj_att_kernel(*refs, dil):
    x_refs, w_ref, o_ref = refs[:LANE_CHUNKS], refs[LANE_CHUNKS], refs[LANE_CHUNKS + 1]
    n = ATT_PROJ_ROWS // dil

    def rows(ref):
        if dil == 1:
            return ref[...]
        return jnp.concatenate([ref[pl.ds(r, n, stride=dil), :] for r in range(dil)], axis=0)

    xp = jnp.concatenate([rows(ref).astype(BF16) for ref in x_refs], axis=1)
    y = jnp.dot(xp, w_ref[...], preferred_element_type=F32)
    o_ref[...] = y.astype(BF16).reshape(dil, n, 3 * ATT_WIDTH)


def _proj_att(x2d, w, dil):
    T = x2d.shape[0]
    per = ATT_TILE // ATT_PROJ_ROWS
    n = ATT_PROJ_ROWS // dil
    out = pl.pallas_call(
        functools.partial(_proj_att_kernel, dil=dil),
        grid=(T // ATT_PROJ_ROWS,),
        in_specs=[pl.BlockSpec((ATT_PROJ_ROWS, LANES), functools.partial(lambda i, c: (i, c), c=c))
                  for c in range(LANE_CHUNKS)]
                 + [pl.BlockSpec((D_MODEL, 3 * ATT_WIDTH), lambda i: (0, 0))],
        out_specs=pl.BlockSpec((None, dil, None, n, 3 * ATT_WIDTH), lambda i: (i // per, 0, i % per, 0, 0)),
        out_shape=jax.ShapeDtypeStruct((T // ATT_TILE, dil, per, n, 3 * ATT_WIDTH), BF16),
        compiler_params=_cparams("parallel"),
        name=f"proj_att_d{dil}",
    )(*([x2d] * LANE_CHUNKS), w)
    return out.reshape(T // ATT_TILE, dil, ATT_TILE // dil, 3 * ATT_WIDTH)


def _att_pair(q2, kp, kc, vp, vc, bias_ref, g, first):
    def head0_lanes(rows, dtype):
        lane = lax.broadcasted_iota(jnp.int32, (rows, 2 * ATT_HEAD_DIM), 1)
        return lane.astype(F32).astype(dtype) < ATT_HEAD_DIM

    lo_q = head0_lanes(ATT_BLOCK, BF16)
    lo_v = head0_lanes(2 * ATT_BLOCK, BF16)
    k2 = jnp.concatenate([kp, kc], axis=0)
    v2 = jnp.concatenate([vp, vc], axis=0)
    zero = jnp.zeros_like(q2)
    ps, ms = [], []
    for hh in range(2):
        qm = jnp.where(lo_q, q2, zero) if hh == 0 else jnp.where(lo_q, zero, q2)
        s = lax.dot_general(qm, k2, (((1,), (1,)), ((), ())), preferred_element_type=F32)
        s = s + bias_ref[g, hh, first]
        m = jnp.max(s, axis=-1, keepdims=True)
        ps.append(jnp.exp(s - m).astype(BF16))
        ms.append(m)
    pcat = jnp.concatenate(ps, axis=1)
    zero_v, one_v = jnp.zeros_like(v2), jnp.ones_like(v2)
    rhs = jnp.concatenate([
        jnp.concatenate([jnp.where(lo_v, v2, zero_v), jnp.where(lo_v, one_v, zero_v)], axis=1),
        jnp.concatenate([jnp.where(lo_v, zero_v, v2), jnp.where(lo_v, zero_v, one_v)], axis=1)], axis=0)
    nd = jnp.dot(pcat, rhs, preferred_element_type=F32)
    m2 = jnp.where(head0_lanes(ATT_BLOCK, F32), ms[0], ms[1])
    return nd[:, :2 * ATT_HEAD_DIM], m2, nd[:, 2 * ATT_HEAD_DIM:]


def _att_kernel(*refs):
    (q0, kc0, vc0, kp0, vp0, q1, kc1, vc1, kp1, vp1, q2, kc2, vc2, kp2, vp2,
     bias_ref, o_ref) = refs[:17]
    ng = len(ATT_DILATIONS)
    num_s, m_s, den_s = refs[17:17 + ng], refs[17 + ng:17 + 2 * ng], refs[17 + 2 * ng:]
    first_tile = (pl.program_id(2) == 0).astype(jnp.int32)
    groups = ((q0, kc0, vc0, kp0, vp0), (q1, kc1, vc1, kp1, vp1), (q2, kc2, vc2, kp2, vp2))
    for g, dil in enumerate(ATT_DILATIONS):
        q_ref, kc_ref, vc_ref, kp_ref, vp_ref = groups[g]
        nb = ATT_TILE // dil // ATT_BLOCK
        for r in range(dil):
            for n in range(nb):
                rows = pl.ds(n * ATT_BLOCK, ATT_BLOCK)
                if n == 0:
                    prev = pl.ds((nb - 1) * ATT_BLOCK, ATT_BLOCK)
                    kp, vp, first = kp_ref[r, prev, :], vp_ref[r, prev, :], first_tile
                else:
                    prev = pl.ds((n - 1) * ATT_BLOCK, ATT_BLOCK)
                    kp, vp, first = kc_ref[r, prev, :], vc_ref[r, prev, :], 0
                num, m, den = _att_pair(q_ref[r, rows, :], kp, kc_ref[r, rows, :], vp, vc_ref[r, rows, :],
                                        bias_ref, g, first)
                if dil == 1:
                    dst = rows
                else:
                    dst = pl.ds(n * ATT_BLOCK * dil + r, ATT_BLOCK, stride=dil)
                num_s[g][dst, :] = num
                m_s[g][dst, :] = m
                den_s[g][dst, :] = den
    m_all = jnp.maximum(jnp.maximum(m_s[0][...], m_s[1][...]), m_s[2][...])
    num = jnp.zeros((ATT_TILE, 2 * ATT_HEAD_DIM), F32)
    den = jnp.zeros((ATT_TILE, 2 * ATT_HEAD_DIM), F32)
    for g in range(ng):
        sc = jnp.exp(m_s[g][...] - m_all)
        num = num + sc * num_s[g][...]
        den = den + sc * den_s[g][...]
    o_ref[...] = (num / den).astype(o_ref.dtype)


def _att_bias_table():
    qi = np.arange(ATT_BLOCK)[:, None]
    ki = np.arange(2 * ATT_BLOCK)[None, :]
    steps = qi + ATT_BLOCK - ki
    valid = (steps >= 0) & (steps <= ATT_BLOCK)
    slopes = np.array([2.0 ** (-8.0 * (h + 1) / ATT_HEADS) for h in range(ATT_HEADS)], np.float32)
    tab = np.empty((len(ATT_DILATIONS), ATT_HEADS, 2, ATT_BLOCK, 2 * ATT_BLOCK), np.float32)
    for g, dil in enumerate(ATT_DILATIONS):
        bias = -slopes[:, None, None] * (steps * dil).astype(np.float32)[None]
        tab[g, :, 0] = np.where(valid[None], bias, NEG_INF)
        tab[g, :, 1] = np.where((valid & (ki >= ATT_BLOCK))[None], bias, NEG_INF)
    return jnp.asarray(tab)


def _attention(qkv, B, S):
    tiles = S // ATT_TILE
    pair = 2 * ATT_HEAD_DIM
    npair = ATT_WIDTH // pair
    in_specs, args = [], []
    for g, dil in enumerate(ATT_DILATIONS):
        blk = (None, dil, ATT_TILE // dil, pair)
        cur = lambda b, hp, t, off: (b * tiles + t, 0, 0, off * npair + hp)
        prv = lambda b, hp, t, off: (b * tiles + jnp.maximum(t - 1, 0), 0, 0, off * npair + hp)
        in_specs += [pl.BlockSpec(blk, functools.partial(cur, off=0)),
                     pl.BlockSpec(blk, functools.partial(cur, off=1)),
                     pl.BlockSpec(blk, functools.partial(cur, off=2)),
                     pl.BlockSpec(blk, functools.partial(prv, off=1)),
                     pl.BlockSpec(blk, functools.partial(prv, off=2))]
        args += [qkv[g]] * 5
    in_specs.append(pl.BlockSpec((len(ATT_DILATIONS), 2, 2, ATT_BLOCK, 2 * ATT_BLOCK),
                                 lambda b, hp, t: (0, hp, 0, 0, 0)))
    args.append(_att_bias_table())
    scratch = [pltpu.VMEM((ATT_TILE, pair), F32) for _ in range(3 * len(ATT_DILATIONS))]
    return pl.pallas_call(
        _att_kernel,
        grid=(B, npair, tiles),
        in_specs=in_specs,
        out_specs=pl.BlockSpec((ATT_TILE, pair), lambda b, hp, t: (b * tiles + t, hp)),
        out_shape=jax.ShapeDtypeStruct((B * S, ATT_WIDTH), BF16),
        scratch_shapes=scratch,
        compiler_params=_cparams("parallel", "parallel", "arbitrary"),
        name="dilated_attention",
    )(*args)


def _att_weights(w_in_l):
    out = []
    width = len(ATT_DILATIONS) * ATT_WIDTH
    for g in range(len(ATT_DILATIONS)):
        cols = [w_in_l[:, part * width + g * ATT_WIDTH: part * width + (g + 1) * ATT_WIDTH] for part in range(3)]
        cols[0] = cols[0] * (ATT_HEAD_DIM ** -0.5)
        out.append(jnp.concatenate(cols, axis=1).astype(BF16))
    return out


HG_Q, HG_LOGF_HI, HG_LOGF_LO, HG_KEY, HG_VAL, HG_SWISH_G, HG_GATE_A, HG_GATE_H = range(8)


def _proj_kernel(x_ref, w_ref, lbl_ref, o_ref):
    xb = x_ref[...].astype(BF16)
    lbl = lbl_ref[...]
    e = jnp.exp(lbl - jnp.max(lbl, axis=0, keepdims=True))
    lb = e[0:1] / jnp.sum(e, axis=0, keepdims=True)

    def put(block, v):
        o_ref[:, block * HG_WIDTH:(block + 1) * HG_WIDTH] = v.astype(o_ref.dtype)

    def u(j):
        return jnp.dot(xb, w_ref[:, j * HG_WIDTH:(j + 1) * HG_WIDTH], preferred_element_type=F32)

    def forget_terms(uf):
        forget = lb + (1.0 - lb) * jax.nn.sigmoid(uf)
        log_f = jnp.log(forget)
        hi = log_f.astype(BF16)
        put(HG_LOGF_HI, hi)
        put(HG_LOGF_LO, log_f - hi.astype(F32))
        put(HG_KEY, 1.0 - forget)

    finish = [lambda v: put(HG_Q, v), forget_terms,
              lambda v: put(HG_VAL, v * jax.nn.sigmoid(v)), lambda v: put(HG_SWISH_G, v * jax.nn.sigmoid(v)),
              lambda v: put(HG_GATE_A, v), lambda v: put(HG_GATE_H, v)]
    pending = u(0)
    for j in range(1, len(finish)):
        nxt = u(j)
        finish[j - 1](pending)
        pending = nxt
    finish[-1](pending)


def _proj(x2d, w, lb_logits):
    T, N = x2d.shape[0], w.shape[1]
    return pl.pallas_call(
        _proj_kernel,
        grid=(T // PROJ_ROWS,),
        in_specs=[pl.BlockSpec((PROJ_ROWS, D_MODEL), lambda i: (i, 0)),
                  pl.BlockSpec((D_MODEL, N), lambda i: (0, 0)),
                  pl.BlockSpec((2, HG_WIDTH), lambda i: (0, 0))],
        out_specs=pl.BlockSpec((PROJ_ROWS, 8 * HG_WIDTH), lambda i: (i, 0)),
        out_shape=jax.ShapeDtypeStruct((T, 8 * HG_WIDTH), BF16),
        compiler_params=_cparams("parallel"),
        name="proj_hgrn_gates",
    )(x2d, w, lb_logits)


def _hgrn_kernel(q_ref, lf_hi_ref, lf_lo_ref, key_ref, val_ref, sg_ref, gain_ref, o_ref, state_ref):
    @pl.when(pl.program_id(1) == 0)
    def _():
        state_ref[...] = jnp.zeros_like(state_ref)

    row = lax.broadcasted_iota(jnp.int32, (HG_TILE, HG_TILE), 0)
    col = lax.broadcasted_iota(jnp.int32, (HG_TILE, HG_TILE), 1)
    causal = (row >= col) & ((row // HG_CHUNK) == (col // HG_CHUNK))
    tri = jnp.where(causal, 1.0, 0.0).astype(BF16)
    b = (jnp.dot(tri, lf_hi_ref[...], preferred_element_type=F32)
         + jnp.dot(tri, lf_lo_ref[...], preferred_element_type=F32))
    eb = jnp.exp(b)
    q_dec = (q_ref[...].astype(F32) * eb).astype(BF16)
    k_inv = key_ref[...].astype(F32) * jnp.exp(-b)
    val = val_ref[...]
    k_inv_b = k_inv.astype(BF16)

    n_chunks = HG_TILE // HG_CHUNK
    last_rows = [eb[(c + 1) * HG_CHUNK - 1:(c + 1) * HG_CHUNK, :] for c in range(n_chunks)]
    dec_rows = jnp.concatenate([jnp.broadcast_to(r, (HG_CHUNK, HG_WIDTH)) for r in last_rows], axis=0)
    k_end = (k_inv * dec_rows).astype(BF16)
    chunk_of_row = (lax.broadcasted_iota(jnp.int32, (HG_TILE, HG_DIM), 0) // HG_CHUNK).astype(F32).astype(BF16)
    in_chunk = [chunk_of_row == c for c in range(n_chunks)]
    zero = jnp.zeros((HG_TILE, HG_DIM), BF16)

    def per_chunk_columns(t):
        return jnp.concatenate([jnp.where(m, t, zero) for m in in_chunk], axis=1)

    outs = []
    for h in range(HG_HEADS):
        cols = slice(h * HG_DIM, (h + 1) * HG_DIM)
        qd, ki, vv = q_dec[:, cols], k_inv_b[:, cols], val[:, cols]
        a = lax.dot_general(qd, ki, (((1,), (1,)), ((), ())), preferred_element_type=F32)
        a = jnp.where(causal, a, 0.0).astype(BF16)
        o_intra = jnp.dot(a, vv, preferred_element_type=F32)
        upd = lax.dot_general(vv, per_chunk_columns(k_end[:, cols]), (((0,), (0,)), ((), ())),
                              preferred_element_type=F32)
        st = state_ref[h]
        entering = []
        for c in range(n_chunks):
            entering.append(st.astype(BF16))
            st = st * last_rows[c][:, cols] + upd[:, c * HG_DIM:(c + 1) * HG_DIM]
        state_ref[h] = st
        o_inter = lax.dot_general(per_chunk_columns(qd), jnp.concatenate(entering, axis=1),
                                  (((1,), (1,)), ((), ())), preferred_element_type=F32)
        o = o_intra + o_inter
        o = o * lax.rsqrt(jnp.mean(jnp.square(o), axis=-1, keepdims=True) + RMS_EPS)
        outs.append(o)
    o = jnp.concatenate(outs, axis=1) * gain_ref[...]
    o_ref[...] = (o * sg_ref[...].astype(F32)).astype(o_ref.dtype)


def _hgrn(u_hg, gain, B, S):
    tiles = S // HG_TILE
    col = lambda j: pl.BlockSpec((HG_TILE, HG_WIDTH), functools.partial(lambda b, t, j: (b * tiles + t, j), j=j))
    blocks = (HG_Q, HG_LOGF_HI, HG_LOGF_LO, HG_KEY, HG_VAL, HG_SWISH_G)
    return pl.pallas_call(
        _hgrn_kernel,
        grid=(B, tiles),
        in_specs=[col(j) for j in blocks] + [pl.BlockSpec((1, HG_WIDTH), lambda b, t: (0, 0))],
        out_specs=pl.BlockSpec((HG_TILE, HG_WIDTH), lambda b, t: (b * tiles + t, 0)),
        out_shape=jax.ShapeDtypeStruct((B * S, HG_WIDTH), BF16),
        scratch_shapes=[pltpu.VMEM((HG_HEADS, HG_DIM, HG_DIM), F32)],
        compiler_params=_cparams("parallel", "arbitrary"),
        name="hgrn2",
    )(*([u_hg] * len(blocks)), gain)


def _load_row_tiles(ref, n, start=0):
    return jnp.concatenate([ref[pl.ds(start + c, n, stride=LANE_CHUNKS), :] for c in range(LANE_CHUNKS)], axis=1)


def _store_row_tiles(ref, val, n):
    for c in range(LANE_CHUNKS):
        ref[pl.ds(c, n, stride=LANE_CHUNKS), :] = val[:, c * LANES:(c + 1) * LANES]


def _layer_norm(z, w, b):
    mu = jnp.mean(z, axis=-1, keepdims=True)
    zc = z - mu
    var = jnp.mean(jnp.square(zc), axis=-1, keepdims=True)
    return zc * lax.rsqrt(var + LN_EPS) * w + b


def _merge_kernel(ya_ref, yh_ref, ga_ref, gh_ref, x_ref, wa_ref, wh_ref, wo_ref, lw_ref, lb_ref, o_ref):
    ma = jnp.dot(ya_ref[...], wa_ref[...], preferred_element_type=F32)
    mh = jnp.dot(yh_ref[...], wh_ref[...], preferred_element_type=F32)
    merged = (jax.nn.sigmoid(ga_ref[...].astype(F32)) * ma + jax.nn.sigmoid(gh_ref[...].astype(F32)) * mh)
    z = DEEPNORM_ALPHA * x_ref[...] + jnp.dot(merged.astype(BF16), wo_ref[...], preferred_element_type=F32)
    _store_row_tiles(o_ref, _layer_norm(z, lw_ref[...], lb_ref[...]), MIX_ROWS)


def _merge(y_att, y_hg, u_hg, x2d, w_a, w_h, w_o, ln_w, ln_b):
    T = x2d.shape[0]
    rows = lambda width, j=0: pl.BlockSpec((MIX_ROWS, width), functools.partial(lambda i, j: (i, j), j=j))
    full = lambda a: pl.BlockSpec(a.shape, lambda i: (0, 0))
    return pl.pallas_call(
        _merge_kernel,
        grid=(T // MIX_ROWS,),
        in_specs=[rows(ATT_WIDTH), rows(HG_WIDTH), rows(D_MODEL, HG_GATE_A), rows(D_MODEL, HG_GATE_H), rows(D_MODEL),
                  full(w_a), full(w_h), full(w_o), full(ln_w), full(ln_b)],
        out_specs=pl.BlockSpec((MIX_ROWS * LANE_CHUNKS, LANES), lambda i: (i, 0)),
        out_shape=jax.ShapeDtypeStruct((T * LANE_CHUNKS, LANES), F32),
        compiler_params=_cparams("parallel"),
        name="merge_ln1",
    )(y_att, y_hg, u_hg, u_hg, x2d, w_a, w_h, w_o, ln_w, ln_b)


def _first_argmax(v, ids, n):
    mx = jnp.max(v, axis=0, keepdims=True)
    return mx, jnp.min(jnp.where(v == mx, ids, n), axis=0, keepdims=True)


def _route_kernel(x1_ref, p_ref, wrt_ref, rb_ref, wsg_ref, wsu_ref, wsd_ref, wpg_ref, wpp_ref,
                  base_ref, idx_ref, gate_ref, rank_ref, cnt_ref, carry_ref):
    @pl.when(pl.program_id(0) % (MOE_TILE // MIX_ROWS) == 0)
    def _():
        carry_ref[...] = jnp.zeros_like(carry_ref)

    x1 = _load_row_tiles(x1_ref, MIX_ROWS)
    x1b = x1.astype(BF16)
    logits = lax.dot_general(wrt_ref[...], x1, (((1,), (1,)), ((), ())), preferred_element_type=F32,
                             precision=lax.Precision.HIGHEST)
    s = jax.nn.sigmoid(logits)
    sel = s + rb_ref[...]
    eid = lax.broadcasted_iota(jnp.int32, (N_EXPERTS, MIX_ROWS), 0)
    neg = -jnp.inf

    grp = sel.reshape(N_GROUPS, GROUP_SIZE, MIX_ROWS)
    mid = lax.broadcasted_iota(jnp.int32, grp.shape, 1)
    m1 = jnp.max(grp, axis=1, keepdims=True)
    i1 = jnp.min(jnp.where(grp == m1, mid, GROUP_SIZE), axis=1, keepdims=True)
    m2 = jnp.max(jnp.where(mid == i1, neg, grp), axis=1, keepdims=True)
    gscore = (m1 + m2).reshape(N_GROUPS, MIX_ROWS)
    gid = lax.broadcasted_iota(jnp.int32, (N_GROUPS, MIX_ROWS), 0)
    gsel = jnp.zeros((N_GROUPS, MIX_ROWS), jnp.bool_)
    for _ in range(TOPK_GROUPS):
        _, gi = _first_argmax(gscore, gid, N_GROUPS)
        hit = gid == gi
        gsel = gsel | hit
        gscore = jnp.where(hit, neg, gscore)
    emask = jnp.broadcast_to(gsel.reshape(N_GROUPS, 1, MIX_ROWS), grp.shape).reshape(N_EXPERTS, MIX_ROWS)
    cand = jnp.where(emask, sel, neg)

    idxs, gates = [], []
    chosen = jnp.zeros((N_EXPERTS, MIX_ROWS), jnp.bool_)
    for _ in range(TOP_K):
        _, ei = _first_argmax(cand, eid, N_EXPERTS)
        hit = eid == ei
        idxs.append(ei)
        gates.append(jnp.sum(jnp.where(hit, s, 0.0), axis=0, keepdims=True))
        chosen = chosen | hit
        cand = jnp.where(hit, neg, cand)
    g = jnp.concatenate(gates, axis=0)
    g = g / jnp.sum(g, axis=0, keepdims=True) * ROUTED_SCALE
    idx_ref[...] = jnp.concatenate(idxs, axis=0)
    gate_ref[...] = g

    onehot = jnp.where(chosen, 1.0, 0.0)
    tr = lax.broadcasted_iota(jnp.int32, (MIX_ROWS, MIX_ROWS), 0)
    tc = lax.broadcasted_iota(jnp.int32, (MIX_ROWS, MIX_ROWS), 1)
    before = jnp.where(tr < tc, 1.0, 0.0).astype(BF16)
    prefix = jnp.dot(onehot.astype(BF16), before, preferred_element_type=F32)
    rankfull = (carry_ref[:, 0:1] + prefix).astype(jnp.int32)
    rank_ref[...] = jnp.concatenate(
        [jnp.sum(jnp.where(eid == ei, rankfull, 0), axis=0, keepdims=True) for ei in idxs], axis=0)
    total = carry_ref[...] + jnp.sum(onehot, axis=1, keepdims=True)
    carry_ref[...] = total
    cnt_ref[...] = total.astype(jnp.int32)

    hg = jnp.dot(x1b, wsg_ref[...], preferred_element_type=F32)
    hu = jnp.dot(x1b, wsu_ref[...], preferred_element_type=F32)
    shared = jnp.dot((hg * jax.nn.sigmoid(hg) * hu).astype(BF16), wsd_ref[...], preferred_element_type=F32)
    ple = (jax.nn.sigmoid(jnp.dot(x1b, wpg_ref[...], preferred_element_type=F32))
           * jnp.dot(p_ref[...].astype(BF16), wpp_ref[...], preferred_element_type=F32))
    _store_row_tiles(base_ref, DEEPNORM_ALPHA * x1 + shared + ple, MIX_ROWS)


def _route(x1, p2d, wr_t, rbias, wsg, wsu, wsd, wpg, wpp):
    T = x1.shape[0] // LANE_CHUNKS
    per_tile = MOE_TILE // MIX_ROWS
    full = lambda a: pl.BlockSpec(a.shape, lambda i: (0, 0))
    tok = pl.BlockSpec((TOP_K, MIX_ROWS), lambda i: (0, i))
    row_tiles = pl.BlockSpec((MIX_ROWS * LANE_CHUNKS, LANES), lambda i: (i, 0))
    return pl.pallas_call(
        _route_kernel,
        grid=(T // MIX_ROWS,),
        in_specs=[row_tiles,
                  pl.BlockSpec((MIX_ROWS, PLE_DIM), lambda i: (i, 0)),
                  full(wr_t), full(rbias), full(wsg), full(wsu), full(wsd), full(wpg), full(wpp)],
        out_specs=[row_tiles, tok, tok, tok,
                   pl.BlockSpec((None, N_EXPERTS, LANES), lambda i: (i // per_tile, 0, 0))],
        out_shape=[jax.ShapeDtypeStruct((T * LANE_CHUNKS, LANES), F32),
                   jax.ShapeDtypeStruct((TOP_K, T), jnp.int32),
                   jax.ShapeDtypeStruct((TOP_K, T), F32),
                   jax.ShapeDtypeStruct((TOP_K, T), jnp.int32),
                   jax.ShapeDtypeStruct((T // MOE_TILE, N_EXPERTS, LANES), jnp.int32)],
        scratch_shapes=[pltpu.VMEM((N_EXPERTS, LANES), F32)],
        compiler_params=_cparams("arbitrary"),
        name="route_shared_ple",
    )(x1, p2d, wr_t, rbias, wsg, wsu, wsd, wpg, wpp)


def _dest_kernel(off_ref, idx_ref, rank_ref, dest_ref):
    idx = idx_ref[...]
    base = pl.program_id(0) * N_EXPERTS

    def body(e, acc):
        return acc + jnp.where(idx == e, off_ref[base + e], 0)

    dest_ref[...] = lax.fori_loop(0, N_EXPERTS, body, rank_ref[...])


def _dest(off, idx, rank):
    T = idx.shape[1]
    spec = pl.BlockSpec((TOP_K, MOE_TILE), lambda i, off: (0, i))
    return pl.pallas_call(
        _dest_kernel,
        grid_spec=pltpu.PrefetchScalarGridSpec(num_scalar_prefetch=1, grid=(T // MOE_TILE,),
                                               in_specs=[spec, spec], out_specs=spec),
        out_shape=jax.ShapeDtypeStruct(idx.shape, jnp.int32),
        compiler_params=_cparams("arbitrary"),
        name="moe_dest",
    )(off, idx, rank)


def _plan_kernel(dest_ref, list_ref):
    first = pl.program_id(1) * PLAN_CODES

    @pl.when(pl.program_id(1) == 0)
    def _():
        def pad(u, c):
            list_ref[MOE_TILE * TOP_K + u] = 0
            return c

        lax.fori_loop(0, LIST_PAD, pad, 0, unroll=8)

    def body(i, c):
        a = i * PLAN_GROUP
        dests = [dest_ref[a + u] for u in range(PLAN_GROUP)]
        for u, d in enumerate(dests):
            list_ref[d] = first + a + u
        return c

    lax.fori_loop(0, PLAN_CODES // PLAN_GROUP, body, 0)


def _plan(dest):
    n_tiles = dest.shape[0] // (MOE_TILE * TOP_K)
    per_tile = MOE_TILE * TOP_K // PLAN_CODES
    return pl.pallas_call(
        _plan_kernel,
        grid=(n_tiles, per_tile),
        in_specs=[pl.BlockSpec((PLAN_CODES,), lambda i, j: (i * per_tile + j,), memory_space=pltpu.SMEM)],
        out_specs=pl.BlockSpec((LIST_LEN,), lambda i, j: (i,), memory_space=pltpu.SMEM),
        out_shape=jax.ShapeDtypeStruct((n_tiles * LIST_LEN,), jnp.int32),
        compiler_params=_cparams("arbitrary", "arbitrary"),
        name="moe_plan",
    )(dest)


def _plan_sc_kernel(dest_hbm, list_hbm, dest_v, list_v):
    codes = MOE_TILE * TOP_K
    worker = lax.axis_index("subcore") * SC_CORES + lax.axis_index("core")

    @pl.when(worker < dest_hbm.shape[0] // codes)
    def _():
        pltpu.sync_copy(dest_hbm.at[pl.ds(worker * codes, codes)], dest_v)
        lane = lax.iota(jnp.int32, SC_LANES)

        @pl.loop(0, codes // SC_LANES)
        def _(i):
            at = i * SC_LANES
            plsc.store_scatter(list_v, [dest_v[pl.ds(at, SC_LANES)]], lane + at)

        @pl.loop(0, LIST_PAD // SC_LANES)
        def _(i):
            list_v[pl.ds(codes + i * SC_LANES, SC_LANES)] = jnp.zeros((SC_LANES,), jnp.int32)

        pltpu.sync_copy(list_v, list_hbm.at[pl.ds(worker * LIST_LEN, LIST_LEN)])


def _plan_sc(dest):
    n_tiles = dest.shape[0] // (MOE_TILE * TOP_K)
    assert n_tiles <= SC_CORES * SC_SUBCORES
    mesh = plsc.VectorSubcoreMesh(core_axis_name="core", subcore_axis_name="subcore",
                                  num_cores=SC_CORES, num_subcores=SC_SUBCORES)
    return pl.kernel(
        _plan_sc_kernel,
        out_type=jax.ShapeDtypeStruct((n_tiles * LIST_LEN,), jnp.int32),
        mesh=mesh,
        scratch_types=[pltpu.VMEM((MOE_TILE * TOP_K,), jnp.int32), pltpu.VMEM((LIST_LEN,), jnp.int32)],
        compiler_params=pltpu.CompilerParams(needs_layout_passes=False),
        name="moe_plan_sc",
    )(dest)


def _moe_tile_kernel(cnt_ref, off_ref, list_ref, gate_ref, wg_ref, wu_ref, wd_ref, lw_ref, lb_ref, x_hbm, base_hbm,
                     o_hbm, x_s, acc_s, xg_a, xg_b, xg_c, y_a, y_b, y_c, stage_s):
    tile, e = pl.program_id(0), pl.program_id(1)
    rows_of = lambda ref, r, n: ref.at[pl.ds(pl.multiple_of(r * LANE_CHUNKS, LANE_CHUNKS), n * LANE_CHUNKS), :]
    tile_rows = pl.ds(pl.multiple_of(tile * (MOE_TILE * LANE_CHUNKS), LANE_CHUNKS), MOE_TILE * LANE_CHUNKS)

    pair = tile * N_EXPERTS + e
    last_pair = pl.num_programs(0) * N_EXPERTS - 1
    n, off = cnt_ref[pair], off_ref[pair]
    first_row = lambda code: pl.multiple_of(code & -LANE_CHUNKS, LANE_CHUNKS)
    tile_at = lambda ref, r: ref.at[pl.ds(r, LANE_CHUNKS), :]

    def gather_group(xg, first, jb):
        at = first + jb * GATHER_GROUP
        rows = [tile_at(x_s, first_row(list_ref[at + u]))[...] for u in range(GATHER_GROUP)]
        rows_of(xg, jb * GATHER_GROUP, GATHER_GROUP)[...] = jnp.concatenate(rows, axis=0)

    def gather_loop(xg, first):
        def body(jb, cc):
            gather_group(xg, first, jb)
            return cc

        lax.fori_loop(0, MOE_CHUNK // GATHER_GROUP, body, 0)

    def swiglu(xg, y):
        xb = _load_row_tiles(xg, MOE_CHUNK).astype(BF16)
        hg = jnp.dot(xb, wg_ref[...], preferred_element_type=F32)
        hu = jnp.dot(xb, wu_ref[...], preferred_element_type=F32)
        act = (hg * jax.nn.sigmoid(hg) * hu).astype(BF16)
        _store_row_tiles(y, jnp.dot(act, wd_ref[...], preferred_element_type=F32), MOE_CHUNK)

    def scatter_group(y, first, j0, live):
        codes = [list_ref[first + j0 + u] for u in range(live)]
        gates = [gate_ref[code] for code in codes]
        dsts = [first_row(code) for code in codes]
        yv = rows_of(y, j0, live)[...]
        vals = [tile_at(acc_s, d)[...] + g * yv[u * LANE_CHUNKS:(u + 1) * LANE_CHUNKS]
                for u, (d, g) in enumerate(zip(dsts, gates))]
        for d, val in reversed(list(zip(dsts, vals))):
            tile_at(acc_s, d)[...] = val

    def scatter_loop(y, first, m):
        def body(jg, cc):
            scatter_group(y, first, jg * SCATTER_GROUP, SCATTER_GROUP)
            return cc

        lax.fori_loop(0, m // SCATTER_GROUP, body, 0)
        for live in range(1, SCATTER_GROUP):
            @pl.when(m % SCATTER_GROUP == live)
            def _(live=live):
                scatter_group(y, first, m - live, live)

    @pl.when(e == 0)
    def _():
        pltpu.sync_copy(x_hbm.at[tile_rows, :], x_s)
        pltpu.sync_copy(base_hbm.at[tile_rows, :], acc_s)
        y_b[...] = jnp.zeros_like(y_b)
        gather_loop(xg_a, off)

    prev_off = off_ref[jnp.maximum(pair - 1, 0)]
    next_off = off_ref[jnp.minimum(pair + 1, last_pair)]
    live_row = lax.broadcasted_iota(jnp.int32, (MOE_CHUNK, 2 * LANES), 0) < n

    def run_expert(xg_cur, y_cur, xg_nxt, y_prv):
        gathers = [functools.partial(gather_group, xg_nxt, next_off, jb) for jb in range(MOE_CHUNK // GATHER_GROUP)]
        scatters = [functools.partial(scatter_group, y_prv, prev_off, jg * SCATTER_GROUP, SCATTER_GROUP)
                    for jg in range(MOE_CHUNK // SCATTER_GROUP)]
        side = [s for both in zip(gathers, scatters) for s in both]
        pieces = 2 + LANE_CHUNKS // 2
        per_piece = -(-len(side) // pieces)

        def side_work(i):
            for s in side[i * per_piece:(i + 1) * per_piece]:
                s()

        xb = _load_row_tiles(xg_cur, MOE_CHUNK).astype(BF16)
        hg = jnp.dot(xb, wg_ref[...], preferred_element_type=F32)
        side_work(0)
        hu = jnp.dot(xb, wu_ref[...], preferred_element_type=F32)
        side_work(1)
        act = (hg * jax.nn.sigmoid(hg) * hu).astype(BF16)
        for q in range(LANE_CHUNKS // 2):
            out = jnp.dot(act, wd_ref[:, q * 2 * LANES:(q + 1) * 2 * LANES], preferred_element_type=F32)
            out = jnp.where(live_row, out, 0.0)
            for c in range(2):
                y_cur[pl.ds(2 * q + c, MOE_CHUNK, stride=LANE_CHUNKS), :] = out[:, c * LANES:(c + 1) * LANES]
            side_work(2 + q)

    @pl.when(e % 2 == 0)
    def _():
        run_expert(xg_a, y_a, xg_b, y_b)

    @pl.when(e % 2 == 1)
    def _():
        run_expert(xg_b, y_b, xg_a, y_a)

    def extra_chunk(c, carry):
        first = off + c * MOE_CHUNK
        gather_loop(xg_c, first)
        swiglu(xg_c, y_c)
        scatter_loop(y_c, first, jnp.minimum(MOE_CHUNK, n - c * MOE_CHUNK))
        return carry

    lax.fori_loop(1, (n + MOE_CHUNK - 1) // MOE_CHUNK, extra_chunk, 0)

    @pl.when(e == N_EXPERTS - 1)
    def _():
        scatter_loop(y_b, off, jnp.minimum(MOE_CHUNK, n))
        for c in range(MOE_TILE // MIX_ROWS):
            z = _load_row_tiles(acc_s, MIX_ROWS, c * MIX_ROWS * LANE_CHUNKS)
            stage_s[...] = _layer_norm(z, lw_ref[...], lb_ref[...])
            pltpu.sync_copy(stage_s, o_hbm.at[pl.ds(tile * MOE_TILE + c * MIX_ROWS, MIX_ROWS), :])


def _moe_tiles(x1, base, tok_list, gate, cnt, off, wg, wu, wd, ln_w, ln_b):
    T = x1.shape[0] // LANE_CHUNKS
    w_spec = lambda shape: pl.BlockSpec((None,) + shape, lambda i, e, cnt, off: (e, 0, 0))
    vec = pl.BlockSpec((1, D_MODEL), lambda i, e, cnt, off: (0, 0))
    hbm = pl.BlockSpec(memory_space=pl.ANY)
    tile_rows = MOE_TILE * LANE_CHUNKS
    return pl.pallas_call(
        _moe_tile_kernel,
        grid_spec=pltpu.PrefetchScalarGridSpec(
            num_scalar_prefetch=2,
            grid=(T // MOE_TILE, N_EXPERTS),
            in_specs=[pl.BlockSpec((LIST_LEN,), lambda i, e, cnt, off: (i,), memory_space=pltpu.SMEM),
                      pl.BlockSpec((MOE_TILE * TOP_K,), lambda i, e, cnt, off: (i,), memory_space=pltpu.SMEM),
                      w_spec((D_MODEL, EXPERT_FF)), w_spec((D_MODEL, EXPERT_FF)), w_spec((EXPERT_FF, D_MODEL)),
                      vec, vec, hbm, hbm],
            out_specs=hbm,
            scratch_shapes=[pltpu.VMEM((tile_rows, LANES), F32),
                            pltpu.VMEM((tile_rows, LANES), F32)]
                           + [pltpu.VMEM((MOE_CHUNK * LANE_CHUNKS, LANES), F32)] * 6
                           + [pltpu.VMEM((MIX_ROWS, D_MODEL), F32)],
        ),
        out_shape=jax.ShapeDtypeStruct((T, D_MODEL), F32),
        compiler_params=_cparams("arbitrary", "arbitrary"),
        name="moe_tiles_ln2",
    )(cnt, off, tok_list, gate, wg, wu, wd, ln_w, ln_b, x1, base)


def kernel(x, p, w_in, hgrn_lb_logits, hgrn_norm_w, w_branch_att, w_branch_hgrn, w_out, ln1_w, ln1_b, router_w, router_bias, expert_w_gate, expert_w_up, expert_w_down, shared_w_gate, shared_w_up, shared_w_down, ple_gate_w, ple_proj_w, ln2_w, ln2_b):
    B, S, D = x.shape
    T = B * S
    l = 0
    x2d = x.reshape(T, D)
    bf = lambda a: a.astype(BF16)

    ws = _att_weights(w_in[l])
    qkv = [_proj_att(x2d, ws[g], d) for g, d in enumerate(ATT_DILATIONS)]
    y_att = _attention(qkv, B, S)
    u_hg = _proj(x2d, bf(w_in[l][:, 3 * len(ATT_DILATIONS) * ATT_WIDTH:]), hgrn_lb_logits)
    y_hg = _hgrn(u_hg, hgrn_norm_w[l:l + 1], B, S)
    x1 = _merge(y_att, y_hg, u_hg, x2d, bf(w_branch_att[l]), bf(w_branch_hgrn[l]), bf(w_out[l]),
                ln1_w[l:l + 1], ln1_b[l:l + 1])

    base, idx, gate, rank, counts = _route(
        x1, p[l].reshape(T, PLE_DIM), router_w[l].T, router_bias[l].reshape(N_EXPERTS, 1),
        bf(shared_w_gate[l]), bf(shared_w_up[l]), bf(shared_w_down[l]), bf(ple_gate_w[l]), bf(ple_proj_w[l]))
    cnt = counts[:, :, 0]
    off = jnp.cumsum(cnt, axis=1) - cnt
    cnt, off = cnt.reshape(-1), off.reshape(-1)
    tok_list = _plan_sc(_dest(off, idx, rank).T.reshape(-1))
    gate_list = gate.T.reshape(-1)
    out = _moe_tiles(x1, base, tok_list, gate_list, cnt, off, bf(expert_w_gate[l]), bf(expert_w_up[l]),
                     bf(expert_w_down[l]), ln2_w[l:l + 1], ln2_b[l:l + 1])
    return out.reshape(B, S, D)
```

```python
import functools

import jax
import jax.numpy as jnp
import numpy as np
from jax import lax
from jax.experimental import pallas as pl
from jax.experimental.pallas import tpu as pltpu
from jax.experimental.pallas import tpu_sc as plsc

F32 = jnp.float32
BF16 = jnp.bfloat16

D_MODEL = 1024
ATT_HEAD_DIM = 64
ATT_HEADS = 8
ATT_DILATIONS = (1, 4, 16)
ATT_BLOCK = 128
ATT_WIDTH = ATT_HEADS * ATT_HEAD_DIM
ATT_TILE = ATT_BLOCK * max(ATT_DILATIONS)
NEG_INF = -1e30

HG_HEADS = 8
HG_DIM = 128
HG_WIDTH = HG_HEADS * HG_DIM
HG_CHUNK = 32
HG_TILE = 256
RMS_EPS = 1e-6

N_EXPERTS = 64
TOP_K = 8
TOP_K_BITS = 3
N_GROUPS = 8
GROUP_SIZE = N_EXPERTS // N_GROUPS
TOPK_GROUPS = 4
EXPERT_FF = 256
ROUTED_SCALE = 2.5
PLE_DIM = 256
LN_EPS = 1e-5
DEPTH = 1
DEEPNORM_ALPHA = (2.0 * DEPTH) ** 0.25

LANES = 128
LANE_CHUNKS = D_MODEL // LANES
PROJ_ROWS = 512
ATT_PROJ_ROWS = 1024
MIX_ROWS = 512
MOE_TILE = 4096
MOE_CHUNK = 576
LIST_PAD = 1024
LIST_LEN = MOE_TILE * TOP_K + LIST_PAD
GATHER_GROUP = 8
SCATTER_GROUP = 8
V7X_VMEM_LIMIT = 56 * 1024 * 1024
SC_CORES, SC_SUBCORES, SC_LANES = 2, 16, 16


def _cparams(*sem):
    return pltpu.CompilerParams(dimension_semantics=sem, vmem_limit_bytes=V7X_VMEM_LIMIT)


def _proj_att_kernel(*refs, dil):
    x_refs, w_ref, o_ref = refs[:LANE_CHUNKS], refs[LANE_CHUNKS], refs[LANE_CHUNKS + 1]
    n = ATT_PROJ_ROWS // dil

    def rows(ref):
        if dil == 1:
            return ref[...]
        return jnp.concatenate([ref[pl.ds(r, n, stride=dil), :] for r in range(dil)], axis=0)

    xp = jnp.concatenate([rows(ref).astype(BF16) for ref in x_refs], axis=1)
    y = jnp.dot(xp, w_ref[...], preferred_element_type=F32)
    o_ref[...] = y.astype(BF16).reshape(dil, n, 3 * ATT_WIDTH)


def _proj_att(x2d, w, dil):
    T = x2d.shape[0]
    per = ATT_TILE // ATT_PROJ_ROWS
    n = ATT_PROJ_ROWS // dil
    out = pl.pallas_call(
        functools.partial(_proj_att_kernel, dil=dil),
        grid=(T // ATT_PROJ_ROWS,),
        in_specs=[pl.BlockSpec((ATT_PROJ_ROWS, LANES), functools.partial(lambda i, c: (i, c), c=c))
                  for c in range(LANE_CHUNKS)]
                 + [pl.BlockSpec((D_MODEL, 3 * ATT_WIDTH), lambda i: (0, 0))],
        out_specs=pl.BlockSpec((None, dil, None, n, 3 * ATT_WIDTH), lambda i: (i // per, 0, i % per, 0, 0)),
        out_shape=jax.ShapeDtypeStruct((T // ATT_TILE, dil, per, n, 3 * ATT_WIDTH), BF16),
        compiler_params=_cparams("parallel"),
        name=f"proj_att_d{dil}",
    )(*([x2d] * LANE_CHUNKS), w)
    return out.reshape(T // ATT_TILE, dil, ATT_TILE // dil, 3 * ATT_WIDTH)


def _att_pair(q2, kp, kc, vp, vc, bias_ref, g, first):
    def head0_lanes(rows, dtype):
        lane = lax.broadcasted_iota(jnp.int32, (rows, 2 * ATT_HEAD_DIM), 1)
        return lane.astype(F32).astype(dtype) < ATT_HEAD_DIM

    lo_q = head0_lanes(ATT_BLOCK, BF16)
    lo_v = head0_lanes(2 * ATT_BLOCK, BF16)
    k2 = jnp.concatenate([kp, kc], axis=0)
    v2 = jnp.concatenate([vp, vc], axis=0)
    zero = jnp.zeros_like(q2)
    ps, ms = [], []
    for hh in range(2):
        qm = jnp.where(lo_q, q2, zero) if hh == 0 else jnp.where(lo_q, zero, q2)
        s = lax.dot_general(qm, k2, (((1,), (1,)), ((), ())), preferred_element_type=F32)
        s = s + bias_ref[g, hh, first]
        m = jnp.max(s, axis=-1, keepdims=True)
        ps.append(jnp.exp(s - m).astype(BF16))
        ms.append(m)
    pcat = jnp.concatenate(ps, axis=1)
    zero_v, one_v = jnp.zeros_like(v2), jnp.ones_like(v2)
    rhs = jnp.concatenate([
        jnp.concatenate([jnp.where(lo_v, v2, zero_v), jnp.where(lo_v, one_v, zero_v)], axis=1),
        jnp.concatenate([jnp.where(lo_v, zero_v, v2), jnp.where(lo_v, zero_v, one_v)], axis=1)], axis=0)
    nd = jnp.dot(pcat, rhs, preferred_element_type=F32)
    m2 = jnp.where(head0_lanes(ATT_BLOCK, F32), ms[0], ms[1])
    return nd[:, :2 * ATT_HEAD_DIM], m2, nd[:, 2 * ATT_HEAD_DIM:]


def _att_kernel(*refs):
    (q0, kc0, vc0, kp0, vp0, q1, kc1, vc1, kp1, vp1, q2, kc2, vc2, kp2, vp2,
     bias_ref, o_ref) = refs[:17]
    ng = len(ATT_DILATIONS)
    num_s, m_s, den_s = refs[17:17 + ng], refs[17 + ng:17 + 2 * ng], refs[17 + 2 * ng:]
    first_tile = (pl.program_id(2) == 0).astype(jnp.int32)
    groups = ((q0, kc0, vc0, kp0, vp0), (q1, kc1, vc1, kp1, vp1), (q2, kc2, vc2, kp2, vp2))
    for g, dil in enumerate(ATT_DILATIONS):
        q_ref, kc_ref, vc_ref, kp_ref, vp_ref = groups[g]
        nb = ATT_TILE // dil // ATT_BLOCK
        for r in range(dil):
            for n in range(nb):
                rows = pl.ds(n * ATT_BLOCK, ATT_BLOCK)
                if n == 0:
                    prev = pl.ds((nb - 1) * ATT_BLOCK, ATT_BLOCK)
                    kp, vp, first = kp_ref[r, prev, :], vp_ref[r, prev, :], first_tile
                else:
                    prev = pl.ds((n - 1) * ATT_BLOCK, ATT_BLOCK)
                    kp, vp, first = kc_ref[r, prev, :], vc_ref[r, prev, :], 0
                num, m, den = _att_pair(q_ref[r, rows, :], kp, kc_ref[r, rows, :], vp, vc_ref[r, rows, :],
                                        bias_ref, g, first)
                if dil == 1:
                    dst = rows
                else:
                    dst = pl.ds(n * ATT_BLOCK * dil + r, ATT_BLOCK, stride=dil)
                num_s[g][dst, :] = num
                m_s[g][dst, :] = m
                den_s[g][dst, :] = den
    m_all = jnp.maximum(jnp.maximum(m_s[0][...], m_s[1][...]), m_s[2][...])
    num = jnp.zeros((ATT_TILE, 2 * ATT_HEAD_DIM), F32)
    den = jnp.zeros((ATT_TILE, 2 * ATT_HEAD_DIM), F32)
    for g in range(ng):
        sc = jnp.exp(m_s[g][...] - m_all)
        num = num + sc * num_s[g][...]
        den = den + sc * den_s[g][...]
    o_ref[...] = (num / den).astype(o_ref.dtype)


def _att_bias_table():
    qi = np.arange(ATT_BLOCK)[:, None]
    ki = np.arange(2 * ATT_BLOCK)[None, :]
    steps = qi + ATT_BLOCK - ki
    valid = (steps >= 0) & (steps <= ATT_BLOCK)
    slopes = np.array([2.0 ** (-8.0 * (h + 1) / ATT_HEADS) for h in range(ATT_HEADS)], np.float32)
    tab = np.empty((len(ATT_DILATIONS), ATT_HEADS, 2, ATT_BLOCK, 2 * ATT_BLOCK), np.float32)
    for g, dil in enumerate(ATT_DILATIONS):
        bias = -slopes[:, None, None] * (steps * dil).astype(np.float32)[None]
        tab[g, :, 0] = np.where(valid[None], bias, NEG_INF)
        tab[g, :, 1] = np.where((valid & (ki >= ATT_BLOCK))[None], bias, NEG_INF)
    return jnp.asarray(tab)


def _attention(qkv, B, S):
    tiles = S // ATT_TILE
    pair = 2 * ATT_HEAD_DIM
    npair = ATT_WIDTH // pair
    in_specs, args = [], []
    for g, dil in enumerate(ATT_DILATIONS):
        blk = (None, dil, ATT_TILE // dil, pair)
        cur = lambda b, hp, t, off: (b * tiles + t, 0, 0, off * npair + hp)
        prv = lambda b, hp, t, off: (b * tiles + jnp.maximum(t - 1, 0), 0, 0, off * npair + hp)
        in_specs += [pl.BlockSpec(blk, functools.partial(cur, off=0)),
                     pl.BlockSpec(blk, functools.partial(cur, off=1)),
                     pl.BlockSpec(blk, functools.partial(cur, off=2)),
                     pl.BlockSpec(blk, functools.partial(prv, off=1)),
                     pl.BlockSpec(blk, functools.partial(prv, off=2))]
        args += [qkv[g]] * 5
    in_specs.append(pl.BlockSpec((len(ATT_DILATIONS), 2, 2, ATT_BLOCK, 2 * ATT_BLOCK),
                                 lambda b, hp, t: (0, hp, 0, 0, 0)))
    args.append(_att_bias_table())
    scratch = [pltpu.VMEM((ATT_TILE, pair), F32) for _ in range(3 * len(ATT_DILATIONS))]
    return pl.pallas_call(
        _att_kernel,
        grid=(B, npair, tiles),
        in_specs=in_specs,
        out_specs=pl.BlockSpec((ATT_TILE, pair), lambda b, hp, t: (b * tiles + t, hp)),
        out_shape=jax.ShapeDtypeStruct((B * S, ATT_WIDTH), BF16),
        scratch_shapes=scratch,
        compiler_params=_cparams("parallel", "parallel", "arbitrary"),
        name="dilated_attention",
    )(*args)


def _att_weights(w_in_l):
    out = []
    width = len(ATT_DILATIONS) * ATT_WIDTH
    for g in range(len(ATT_DILATIONS)):
        cols = [w_in_l[:, part * width + g * ATT_WIDTH: part * width + (g + 1) * ATT_WIDTH] for part in range(3)]
        cols[0] = cols[0] * (ATT_HEAD_DIM ** -0.5)
        out.append(jnp.concatenate(cols, axis=1).astype(BF16))
    return out


def _proj_kernel(x_ref, w_ref, o_ref, *, col_tile):
    xb = x_ref[...].astype(BF16)
    for c in range(w_ref.shape[1] // col_tile):
        cols = slice(c * col_tile, (c + 1) * col_tile)
        o_ref[:, cols] = jnp.dot(xb, w_ref[:, cols], preferred_element_type=F32).astype(o_ref.dtype)


def _proj(x2d, w, col_tile):
    T, N = x2d.shape[0], w.shape[1]
    return pl.pallas_call(
        functools.partial(_proj_kernel, col_tile=col_tile),
        grid=(T // PROJ_ROWS,),
        in_specs=[pl.BlockSpec((PROJ_ROWS, D_MODEL), lambda i: (i, 0)),
                  pl.BlockSpec((D_MODEL, N), lambda i: (0, 0))],
        out_specs=pl.BlockSpec((PROJ_ROWS, N), lambda i: (i, 0)),
        out_shape=jax.ShapeDtypeStruct((T, N), BF16),
        compiler_params=_cparams("parallel"),
        name="proj_hgrn_gates",
    )(x2d, w)


def _split3(v):
    a = v.astype(BF16)
    r = v - a.astype(F32)
    b = r.astype(BF16)
    c = (r - b.astype(F32)).astype(BF16)
    return a, b, c


def _hgrn_kernel(q_ref, f_ref, i_ref, g_ref, lbl_ref, gain_ref, o_ref, state_ref):
    @pl.when(pl.program_id(1) == 0)
    def _():
        state_ref[...] = jnp.zeros_like(state_ref)

    lbl = lbl_ref[...]
    e = jnp.exp(lbl - jnp.max(lbl, axis=0, keepdims=True))
    lb = e[0:1] / jnp.sum(e, axis=0, keepdims=True)
    forget = lb + (1.0 - lb) * jax.nn.sigmoid(f_ref[...].astype(F32))
    log_f = jnp.log(forget)
    key = 1.0 - forget

    row = lax.broadcasted_iota(jnp.int32, (HG_TILE, HG_TILE), 0)
    col = lax.broadcasted_iota(jnp.int32, (HG_TILE, HG_TILE), 1)
    causal = (row >= col) & ((row // HG_CHUNK) == (col // HG_CHUNK))
    tri = jnp.where(causal, 1.0, 0.0).astype(BF16)
    b = sum(jnp.dot(tri, t, preferred_element_type=F32) for t in _split3(log_f))
    eb = jnp.exp(b)
    q_dec = (q_ref[...].astype(F32) * eb).astype(BF16)
    k_inv = key * jnp.exp(-b)
    xi = i_ref[...].astype(F32)
    val = (xi * jax.nn.sigmoid(xi)).astype(BF16)
    k_inv_b = k_inv.astype(BF16)

    n_chunks = HG_TILE // HG_CHUNK
    last_rows = [eb[(c + 1) * HG_CHUNK - 1:(c + 1) * HG_CHUNK, :] for c in range(n_chunks)]
    dec_rows = jnp.concatenate([jnp.broadcast_to(r, (HG_CHUNK, HG_WIDTH)) for r in last_rows], axis=0)
    k_end = (k_inv * dec_rows).astype(BF16)
    chunk_of_row = (lax.broadcasted_iota(jnp.int32, (HG_TILE, HG_DIM), 0) // HG_CHUNK).astype(F32).astype(BF16)
    in_chunk = [chunk_of_row == c for c in range(n_chunks)]
    zero = jnp.zeros((HG_TILE, HG_DIM), BF16)

    def per_chunk_columns(t):
        return jnp.concatenate([jnp.where(m, t, zero) for m in in_chunk], axis=1)

    outs = []
    for h in range(HG_HEADS):
        cols = slice(h * HG_DIM, (h + 1) * HG_DIM)
        qd, ki, vv = q_dec[:, cols], k_inv_b[:, cols], val[:, cols]
        a = lax.dot_general(qd, ki, (((1,), (1,)), ((), ())), preferred_element_type=F32)
        a = jnp.where(causal, a, 0.0).astype(BF16)
        o_intra = jnp.dot(a, vv, preferred_element_type=F32)
        upd = lax.dot_general(vv, per_chunk_columns(k_end[:, cols]), (((0,), (0,)), ((), ())),
                              preferred_element_type=F32)
        st = state_ref[h]
        entering = []
        for c in range(n_chunks):
            entering.append(st.astype(BF16))
            st = st * last_rows[c][:, cols] + upd[:, c * HG_DIM:(c + 1) * HG_DIM]
        state_ref[h] = st
        o_inter = lax.dot_general(per_chunk_columns(qd), jnp.concatenate(entering, axis=1),
                                  (((1,), (1,)), ((), ())), preferred_element_type=F32)
        o = o_intra + o_inter
        o = o * lax.rsqrt(jnp.mean(jnp.square(o), axis=-1, keepdims=True) + RMS_EPS)
        outs.append(o)
    o = jnp.concatenate(outs, axis=1) * gain_ref[...]
    gg = g_ref[...].astype(F32)
    o_ref[...] = (o * (gg * jax.nn.sigmoid(gg))).astype(o_ref.dtype)


def _hgrn(u_hg, lb_logits, gain, B, S):
    tiles = S // HG_TILE
    col = lambda j: pl.BlockSpec((HG_TILE, HG_WIDTH), functools.partial(lambda b, t, j: (b * tiles + t, j), j=j))
    return pl.pallas_call(
        _hgrn_kernel,
        grid=(B, tiles),
        in_specs=[col(0), col(1), col(2), col(3),
                  pl.BlockSpec((2, HG_WIDTH), lambda b, t: (0, 0)),
                  pl.BlockSpec((1, HG_WIDTH), lambda b, t: (0, 0))],
        out_specs=pl.BlockSpec((HG_TILE, HG_WIDTH), lambda b, t: (b * tiles + t, 0)),
        out_shape=jax.ShapeDtypeStruct((B * S, HG_WIDTH), BF16),
        scratch_shapes=[pltpu.VMEM((HG_HEADS, HG_DIM, HG_DIM), F32)],
        compiler_params=_cparams("parallel", "arbitrary"),
        name="hgrn2",
    )(u_hg, u_hg, u_hg, u_hg, lb_logits, gain)


def _load_row_tiles(ref, n, start=0):
    return jnp.concatenate([ref[pl.ds(start + c, n, stride=LANE_CHUNKS), :] for c in range(LANE_CHUNKS)], axis=1)


def _store_row_tiles(ref, val, n):
    for c in range(LANE_CHUNKS):
        ref[pl.ds(c, n, stride=LANE_CHUNKS), :] = val[:, c * LANES:(c + 1) * LANES]


def _layer_norm(z, w, b):
    mu = jnp.mean(z, axis=-1, keepdims=True)
    zc = z - mu
    var = jnp.mean(jnp.square(zc), axis=-1, keepdims=True)
    return zc * lax.rsqrt(var + LN_EPS) * w + b


def _merge_kernel(ya_ref, yh_ref, ga_ref, gh_ref, x_ref, wa_ref, wh_ref, wo_ref, lw_ref, lb_ref, o_ref):
    ma = jnp.dot(ya_ref[...], wa_ref[...], preferred_element_type=F32)
    mh = jnp.dot(yh_ref[...], wh_ref[...], preferred_element_type=F32)
    merged = (jax.nn.sigmoid(ga_ref[...].astype(F32)) * ma + jax.nn.sigmoid(gh_ref[...].astype(F32)) * mh)
    z = DEEPNORM_ALPHA * x_ref[...] + jnp.dot(merged.astype(BF16), wo_ref[...], preferred_element_type=F32)
    _store_row_tiles(o_ref, _layer_norm(z, lw_ref[...], lb_ref[...]), MIX_ROWS)


def _merge(y_att, y_hg, u_hg, x2d, w_a, w_h, w_o, ln_w, ln_b):
    T = x2d.shape[0]
    rows = lambda width, j=0: pl.BlockSpec((MIX_ROWS, width), functools.partial(lambda i, j: (i, j), j=j))
    full = lambda a: pl.BlockSpec(a.shape, lambda i: (0, 0))
    return pl.pallas_call(
        _merge_kernel,
        grid=(T // MIX_ROWS,),
        in_specs=[rows(ATT_WIDTH), rows(HG_WIDTH), rows(D_MODEL, 4), rows(D_MODEL, 5), rows(D_MODEL),
                  full(w_a), full(w_h), full(w_o), full(ln_w), full(ln_b)],
        out_specs=pl.BlockSpec((MIX_ROWS * LANE_CHUNKS, LANES), lambda i: (i, 0)),
        out_shape=jax.ShapeDtypeStruct((T * LANE_CHUNKS, LANES), F32),
        compiler_params=_cparams("parallel"),
        name="merge_ln1",
    )(y_att, y_hg, u_hg, u_hg, x2d, w_a, w_h, w_o, ln_w, ln_b)


def _first_argmax(v, ids, n):
    mx = jnp.max(v, axis=0, keepdims=True)
    return mx, jnp.min(jnp.where(v == mx, ids, n), axis=0, keepdims=True)


def _route_kernel(x1_ref, p_ref, wrt_ref, rb_ref, wsg_ref, wsu_ref, wsd_ref, wpg_ref, wpp_ref,
                  base_ref, idx_ref, gate_ref, rank_ref, cnt_ref, carry_ref):
    @pl.when(pl.program_id(0) % (MOE_TILE // MIX_ROWS) == 0)
    def _():
        carry_ref[...] = jnp.zeros_like(carry_ref)

    x1 = _load_row_tiles(x1_ref, MIX_ROWS)
    x1b = x1.astype(BF16)
    logits = lax.dot_general(wrt_ref[...], x1, (((1,), (1,)), ((), ())), preferred_element_type=F32,
                             precision=lax.Precision.HIGHEST)
    s = jax.nn.sigmoid(logits)
    sel = s + rb_ref[...]
    eid = lax.broadcasted_iota(jnp.int32, (N_EXPERTS, MIX_ROWS), 0)
    neg = -jnp.inf

    grp = sel.reshape(N_GROUPS, GROUP_SIZE, MIX_ROWS)
    mid = lax.broadcasted_iota(jnp.int32, grp.shape, 1)
    m1 = jnp.max(grp, axis=1, keepdims=True)
    i1 = jnp.min(jnp.where(grp == m1, mid, GROUP_SIZE), axis=1, keepdims=True)
    m2 = jnp.max(jnp.where(mid == i1, neg, grp), axis=1, keepdims=True)
    gscore = (m1 + m2).reshape(N_GROUPS, MIX_ROWS)
    gid = lax.broadcasted_iota(jnp.int32, (N_GROUPS, MIX_ROWS), 0)
    gsel = jnp.zeros((N_GROUPS, MIX_ROWS), jnp.bool_)
    for _ in range(TOPK_GROUPS):
        _, gi = _first_argmax(gscore, gid, N_GROUPS)
        hit = gid == gi
        gsel = gsel | hit
        gscore = jnp.where(hit, neg, gscore)
    emask = jnp.broadcast_to(gsel.reshape(N_GROUPS, 1, MIX_ROWS), grp.shape).reshape(N_EXPERTS, MIX_ROWS)
    cand = jnp.where(emask, sel, neg)

    idxs, gates = [], []
    chosen = jnp.zeros((N_EXPERTS, MIX_ROWS), jnp.bool_)
    for _ in range(TOP_K):
        _, ei = _first_argmax(cand, eid, N_EXPERTS)
        hit = eid == ei
        idxs.append(ei)
        gates.append(jnp.sum(jnp.where(hit, s, 0.0), axis=0, keepdims=True))
        chosen = chosen | hit
        cand = jnp.where(hit, neg, cand)
    g = jnp.concatenate(gates, axis=0)
    g = g / jnp.sum(g, axis=0, keepdims=True) * ROUTED_SCALE
    idx_ref[...] = jnp.concatenate(idxs, axis=0)
    gate_ref[...] = g

    onehot = jnp.where(chosen, 1.0, 0.0)
    tr = lax.broadcasted_iota(jnp.int32, (MIX_ROWS, MIX_ROWS), 0)
    tc = lax.broadcasted_iota(jnp.int32, (MIX_ROWS, MIX_ROWS), 1)
    before = jnp.where(tr < tc, 1.0, 0.0).astype(BF16)
    prefix = jnp.dot(onehot.astype(BF16), before, preferred_element_type=F32)
    rankfull = (carry_ref[:, 0:1] + prefix).astype(jnp.int32)
    rank_ref[...] = jnp.concatenate(
        [jnp.sum(jnp.where(eid == ei, rankfull, 0), axis=0, keepdims=True) for ei in idxs], axis=0)
    total = carry_ref[...] + jnp.sum(onehot, axis=1, keepdims=True)
    carry_ref[...] = total
    cnt_ref[...] = total.astype(jnp.int32)

    hg = jnp.dot(x1b, wsg_ref[...], preferred_element_type=F32)
    hu = jnp.dot(x1b, wsu_ref[...], preferred_element_type=F32)
    shared = jnp.dot((hg * jax.nn.sigmoid(hg) * hu).astype(BF16), wsd_ref[...], preferred_element_type=F32)
    ple = (jax.nn.sigmoid(jnp.dot(x1b, wpg_ref[...], preferred_element_type=F32))
           * jnp.dot(p_ref[...].astype(BF16), wpp_ref[...], preferred_element_type=F32))
    _store_row_tiles(base_ref, DEEPNORM_ALPHA * x1 + shared + ple, MIX_ROWS)


def _route(x1, p2d, wr_t, rbias, wsg, wsu, wsd, wpg, wpp):
    T = x1.shape[0] // LANE_CHUNKS
    per_tile = MOE_TILE // MIX_ROWS
    full = lambda a: pl.BlockSpec(a.shape, lambda i: (0, 0))
    tok = pl.BlockSpec((TOP_K, MIX_ROWS), lambda i: (0, i))
    row_tiles = pl.BlockSpec((MIX_ROWS * LANE_CHUNKS, LANES), lambda i: (i, 0))
    return pl.pallas_call(
        _route_kernel,
        grid=(T // MIX_ROWS,),
        in_specs=[row_tiles,
                  pl.BlockSpec((MIX_ROWS, PLE_DIM), lambda i: (i, 0)),
                  full(wr_t), full(rbias), full(wsg), full(wsu), full(wsd), full(wpg), full(wpp)],
        out_specs=[row_tiles, tok, tok, tok,
                   pl.BlockSpec((None, N_EXPERTS, LANES), lambda i: (i // per_tile, 0, 0))],
        out_shape=[jax.ShapeDtypeStruct((T * LANE_CHUNKS, LANES), F32),
                   jax.ShapeDtypeStruct((TOP_K, T), jnp.int32),
                   jax.ShapeDtypeStruct((TOP_K, T), F32),
                   jax.ShapeDtypeStruct((TOP_K, T), jnp.int32),
                   jax.ShapeDtypeStruct((T // MOE_TILE, N_EXPERTS, LANES), jnp.int32)],
        scratch_shapes=[pltpu.VMEM((N_EXPERTS, LANES), F32)],
        compiler_params=_cparams("arbitrary"),
        name="route_shared_ple",
    )(x1, p2d, wr_t, rbias, wsg, wsu, wsd, wpg, wpp)


def _plan_sc_kernel(off_hbm, idx_hbm, rank_hbm, list_hbm, off_v, idx_v, rank_v, list_v):
    n_tokens = idx_hbm.shape[0] // TOP_K
    worker = lax.axis_index("subcore") * SC_CORES + lax.axis_index("core")

    @pl.when(worker < n_tokens // MOE_TILE)
    def _():
        pltpu.sync_copy(off_hbm.at[pl.ds(worker * N_EXPERTS, N_EXPERTS)], off_v)
        lane = lax.iota(jnp.int32, SC_LANES)
        for k in range(TOP_K):
            row = pl.ds(k * n_tokens + worker * MOE_TILE, MOE_TILE)
            pltpu.sync_copy(idx_hbm.at[row], idx_v)
            pltpu.sync_copy(rank_hbm.at[row], rank_v)

            @pl.loop(0, MOE_TILE // SC_LANES)
            def _(i):
                at = i * SC_LANES
                pos = plsc.load_gather(off_v, [idx_v[pl.ds(at, SC_LANES)]]) + rank_v[pl.ds(at, SC_LANES)]
                plsc.store_scatter(list_v, [pos], (lane + at) * TOP_K + k)

        @pl.loop(0, LIST_PAD // SC_LANES)
        def _(i):
            list_v[pl.ds(MOE_TILE * TOP_K + i * SC_LANES, SC_LANES)] = jnp.zeros((SC_LANES,), jnp.int32)

        pltpu.sync_copy(list_v, list_hbm.at[pl.ds(worker * LIST_LEN, LIST_LEN)])


def _plan_sc(off, idx, rank):
    n_tiles = idx.shape[1] // MOE_TILE
    assert n_tiles <= SC_CORES * SC_SUBCORES
    mesh = plsc.VectorSubcoreMesh(core_axis_name="core", subcore_axis_name="subcore",
                                  num_cores=SC_CORES, num_subcores=SC_SUBCORES)
    return pl.kernel(
        _plan_sc_kernel,
        out_type=jax.ShapeDtypeStruct((n_tiles * LIST_LEN,), jnp.int32),
        mesh=mesh,
        scratch_types=[pltpu.VMEM((N_EXPERTS,), jnp.int32), pltpu.VMEM((MOE_TILE,), jnp.int32),
                       pltpu.VMEM((MOE_TILE,), jnp.int32), pltpu.VMEM((LIST_LEN,), jnp.int32)],
        compiler_params=pltpu.CompilerParams(needs_layout_passes=False),
        name="moe_plan_sc",
    )(off, idx.reshape(-1), rank.reshape(-1))


def _moe_tile_kernel(cnt_ref, off_ref, list_ref, gate_ref, wg_ref, wu_ref, wd_ref, lw_ref, lb_ref, x_hbm, base_hbm,
                     o_hbm, x_s, acc_s, xg_a, xg_b, xg_c, y_a, y_b, y_c, stage_s):
    tile, e = pl.program_id(0), pl.program_id(1)
    rows_of = lambda ref, r, n: ref.at[pl.ds(pl.multiple_of(r * LANE_CHUNKS, LANE_CHUNKS), n * LANE_CHUNKS), :]
    tile_rows = pl.ds(pl.multiple_of(tile * (MOE_TILE * LANE_CHUNKS), LANE_CHUNKS), MOE_TILE * LANE_CHUNKS)

    pair = tile * N_EXPERTS + e
    last_pair = pl.num_programs(0) * N_EXPERTS - 1
    n, off = cnt_ref[pair], off_ref[pair]
    first_row = lambda code: pl.multiple_of(code & -LANE_CHUNKS, LANE_CHUNKS)
    tile_at = lambda ref, r: ref.at[pl.ds(r, LANE_CHUNKS), :]

    def gather_group(xg, first, jb):
        at = first + jb * GATHER_GROUP
        rows = [tile_at(x_s, first_row(list_ref[at + u]))[...] for u in range(GATHER_GROUP)]
        rows_of(xg, jb * GATHER_GROUP, GATHER_GROUP)[...] = jnp.concatenate(rows, axis=0)

    def gather_loop(xg, first):
        def body(jb, cc):
            gather_group(xg, first, jb)
            return cc

        lax.fori_loop(0, MOE_CHUNK // GATHER_GROUP, body, 0)

    def swiglu(xg, y):
        xb = _load_row_tiles(xg, MOE_CHUNK).astype(BF16)
        hg = jnp.dot(xb, wg_ref[...], preferred_element_type=F32)
        hu = jnp.dot(xb, wu_ref[...], preferred_element_type=F32)
        act = (hg * jax.nn.sigmoid(hg) * hu).astype(BF16)
        _store_row_tiles(y, jnp.dot(act, wd_ref[...], preferred_element_type=F32), MOE_CHUNK)

    def scatter_group(y, first, j0, live):
        codes = [list_ref[first + j0 + u] for u in range(live)]
        gates = [gate_ref[code] for code in codes]
        dsts = [first_row(code) for code in codes]
        yv = rows_of(y, j0, live)[...]
        vals = [tile_at(acc_s, d)[...] + g * yv[u * LANE_CHUNKS:(u + 1) * LANE_CHUNKS]
                for u, (d, g) in enumerate(zip(dsts, gates))]
        for d, val in reversed(list(zip(dsts, vals))):
            tile_at(acc_s, d)[...] = val

    def scatter_loop(y, first, m):
        def body(jg, cc):
            scatter_group(y, first, jg * SCATTER_GROUP, SCATTER_GROUP)
            return cc

        lax.fori_loop(0, m // SCATTER_GROUP, body, 0)
        for live in range(1, SCATTER_GROUP):
            @pl.when(m % SCATTER_GROUP == live)
            def _(live=live):
                scatter_group(y, first, m - live, live)

    @pl.when(e == 0)
    def _():
        pltpu.sync_copy(x_hbm.at[tile_rows, :], x_s)
        pltpu.sync_copy(base_hbm.at[tile_rows, :], acc_s)
        y_b[...] = jnp.zeros_like(y_b)
        gather_loop(xg_a, off)

    prev_off = off_ref[jnp.maximum(pair - 1, 0)]
    next_off = off_ref[jnp.minimum(pair + 1, last_pair)]
    live_row = lax.broadcasted_iota(jnp.int32, (MOE_CHUNK, 2 * LANES), 0) < n

    def run_expert(xg_cur, y_cur, xg_nxt, y_prv):
        gathers = [functools.partial(gather_group, xg_nxt, next_off, jb) for jb in range(MOE_CHUNK // GATHER_GROUP)]
        scatters = [functools.partial(scatter_group, y_prv, prev_off, jg * SCATTER_GROUP, SCATTER_GROUP)
                    for jg in range(MOE_CHUNK // SCATTER_GROUP)]
        side = [s for both in zip(gathers, scatters) for s in both]
        pieces = 2 + LANE_CHUNKS // 2
        per_piece = -(-len(side) // pieces)

        def side_work(i):
            for s in side[i * per_piece:(i + 1) * per_piece]:
                s()

        xb = _load_row_tiles(xg_cur, MOE_CHUNK).astype(BF16)
        hg = jnp.dot(xb, wg_ref[...], preferred_element_type=F32)
        side_work(0)
        hu = jnp.dot(xb, wu_ref[...], preferred_element_type=F32)
        side_work(1)
        act = (hg * jax.nn.sigmoid(hg) * hu).astype(BF16)
        for q in range(LANE_CHUNKS // 2):
            out = jnp.dot(act, wd_ref[:, q * 2 * LANES:(q + 1) * 2 * LANES], preferred_element_type=F32)
            out = jnp.where(live_row, out, 0.0)
            for c in range(2):
                y_cur[pl.ds(2 * q + c, MOE_CHUNK, stride=LANE_CHUNKS), :] = out[:, c * LANES:(c + 1) * LANES]
            side_work(2 + q)

    @pl.when(e % 2 == 0)
    def _():
        run_expert(xg_a, y_a, xg_b, y_b)

    @pl.when(e % 2 == 1)
    def _():
        run_expert(xg_b, y_b, xg_a, y_a)

    def extra_chunk(c, carry):
        first = off + c * MOE_CHUNK
        gather_loop(xg_c, first)
        swiglu(xg_c, y_c)
        scatter_loop(y_c, first, jnp.minimum(MOE_CHUNK, n - c * MOE_CHUNK))
        return carry

    lax.fori_loop(1, (n + MOE_CHUNK - 1) // MOE_CHUNK, extra_chunk, 0)

    @pl.when(e == N_EXPERTS - 1)
    def _():
        scatter_loop(y_b, off, jnp.minimum(MOE_CHUNK, n))
        for c in range(MOE_TILE // MIX_ROWS):
            z = _load_row_tiles(acc_s, MIX_ROWS, c * MIX_ROWS * LANE_CHUNKS)
            stage_s[...] = _layer_norm(z, lw_ref[...], lb_ref[...])
            pltpu.sync_copy(stage_s, o_hbm.at[pl.ds(tile * MOE_TILE + c * MIX_ROWS, MIX_ROWS), :])


def _moe_tiles(x1, base, tok_list, gate, cnt, off, wg, wu, wd, ln_w, ln_b):
    T = x1.shape[0] // LANE_CHUNKS
    w_spec = lambda shape: pl.BlockSpec((None,) + shape, lambda i, e, cnt, off: (e, 0, 0))
    vec = pl.BlockSpec((1, D_MODEL), lambda i, e, cnt, off: (0, 0))
    hbm = pl.BlockSpec(memory_space=pl.ANY)
    tile_rows = MOE_TILE * LANE_CHUNKS
    return pl.pallas_call(
        _moe_tile_kernel,
        grid_spec=pltpu.PrefetchScalarGridSpec(
            num_scalar_prefetch=2,
            grid=(T // MOE_TILE, N_EXPERTS),
            in_specs=[pl.BlockSpec((LIST_LEN,), lambda i, e, cnt, off: (i,), memory_space=pltpu.SMEM),
                      pl.BlockSpec((MOE_TILE * TOP_K,), lambda i, e, cnt, off: (i,), memory_space=pltpu.SMEM),
                      w_spec((D_MODEL, EXPERT_FF)), w_spec((D_MODEL, EXPERT_FF)), w_spec((EXPERT_FF, D_MODEL)),
                      vec, vec, hbm, hbm],
            out_specs=hbm,
            scratch_shapes=[pltpu.VMEM((tile_rows, LANES), F32),
                            pltpu.VMEM((tile_rows, LANES), F32)]
                           + [pltpu.VMEM((MOE_CHUNK * LANE_CHUNKS, LANES), F32)] * 6
                           + [pltpu.VMEM((MIX_ROWS, D_MODEL), F32)],
        ),
        out_shape=jax.ShapeDtypeStruct((T, D_MODEL), F32),
        compiler_params=_cparams("arbitrary", "arbitrary"),
        name="moe_tiles_ln2",
    )(cnt, off, tok_list, gate, wg, wu, wd, ln_w, ln_b, x1, base)


def kernel(x, p, w_in, hgrn_lb_logits, hgrn_norm_w, w_branch_att, w_branch_hgrn, w_out, ln1_w, ln1_b, router_w, router_bias, expert_w_gate, expert_w_up, expert_w_down, shared_w_gate, shared_w_up, shared_w_down, ple_gate_w, ple_proj_w, ln2_w, ln2_b):
    B, S, D = x.shape
    T = B * S
    l = 0
    x2d = x.reshape(T, D)
    bf = lambda a: a.astype(BF16)

    ws = _att_weights(w_in[l])
    qkv = [_proj_att(x2d, ws[g], d) for g, d in enumerate(ATT_DILATIONS)]
    y_att = _attention(qkv, B, S)
    u_hg = _proj(x2d, bf(w_in[l][:, 3 * len(ATT_DILATIONS) * ATT_WIDTH:]), 1536)
    y_hg = _hgrn(u_hg, hgrn_lb_logits, hgrn_norm_w[l:l + 1], B, S)
    x1 = _merge(y_att, y_hg, u_hg, x2d, bf(w_branch_att[l]), bf(w_branch_hgrn[l]), bf(w_out[l]),
                ln1_w[l:l + 1], ln1_b[l:l + 1])

    base, idx, gate, rank, counts = _route(
        x1, p[l].reshape(T, PLE_DIM), router_w[l].T, router_bias[l].reshape(N_EXPERTS, 1),
        bf(shared_w_gate[l]), bf(shared_w_up[l]), bf(shared_w_down[l]), bf(ple_gate_w[l]), bf(ple_proj_w[l]))
    cnt = counts[:, :, 0]
    off = jnp.cumsum(cnt, axis=1) - cnt
    cnt, off = cnt.reshape(-1), off.reshape(-1)
    tok_list = _plan_sc(off, idx, rank)
    gate_list = gate.T.reshape(-1)
    out = _moe_tiles(x1, base, tok_list, gate_list, cnt, off, bf(expert_w_gate[l]), bf(expert_w_up[l]),
                     bf(expert_w_down[l]), ln2_w[l:l + 1], ln2_b[l:l + 1])
    return out.reshape(B, S, D)
```

```python
import functools

import jax
import jax.numpy as jnp
import numpy as np
from jax import lax
from jax.experimental import pallas as pl
from jax.experimental.pallas import tpu as pltpu
from jax.experimental.pallas import tpu_sc as plsc

F32 = jnp.float32
BF16 = jnp.bfloat16

D_MODEL = 1024
ATT_HEAD_DIM = 64
ATT_HEADS = 8
ATT_DILATIONS = (1, 4, 16)
ATT_BLOCK = 128
ATT_WIDTH = ATT_HEADS * ATT_HEAD_DIM
ATT_TILE = ATT_BLOCK * max(ATT_DILATIONS)
NEG_INF = -1e30

HG_HEADS = 8
HG_DIM = 128
HG_WIDTH = HG_HEADS * HG_DIM
HG_CHUNK = 32
HG_TILE = 256
RMS_EPS = 1e-6

N_EXPERTS = 64
TOP_K = 8
TOP_K_BITS = 3
N_GROUPS = 8
GROUP_SIZE = N_EXPERTS // N_GROUPS
TOPK_GROUPS = 4
EXPERT_FF = 256
ROUTED_SCALE = 2.5
PLE_DIM = 256
LN_EPS = 1e-5
DEPTH = 1
DEEPNORM_ALPHA = (2.0 * DEPTH) ** 0.25

LANES = 128
LANE_CHUNKS = D_MODEL // LANES
PROJ_ROWS = 512
ATT_PROJ_ROWS = 1024
MIX_ROWS = 512
MOE_TILE = 4096
MOE_CHUNK = 576
LN_ROWS = 256
LIST_PAD = 1024
LIST_LEN = MOE_TILE * TOP_K + LIST_PAD
GATHER_GROUP = 8
SCATTER_GROUP = 8
V7X_VMEM_LIMIT = 56 * 1024 * 1024
SC_CORES, SC_SUBCORES, SC_LANES = 2, 16, 16


def _cparams(*sem):
    return pltpu.CompilerParams(dimension_semantics=sem, vmem_limit_bytes=V7X_VMEM_LIMIT)


def _proj_att_kernel(*refs, dil):
    x_refs, w_ref, o_ref = refs[:LANE_CHUNKS], refs[LANE_CHUNKS], refs[LANE_CHUNKS + 1]
    n = ATT_PROJ_ROWS // dil

    def rows(ref):
        if dil == 1:
            return ref[...]
        return jnp.concatenate([ref[pl.ds(r, n, stride=dil), :] for r in range(dil)], axis=0)

    xp = jnp.concatenate([rows(ref).astype(BF16) for ref in x_refs], axis=1)
    y = jnp.dot(xp, w_ref[...], preferred_element_type=F32)
    o_ref[...] = y.astype(BF16).reshape(dil, n, 3 * ATT_WIDTH)


def _proj_att(x2d, w, dil):
    T = x2d.shape[0]
    per = ATT_TILE // ATT_PROJ_ROWS
    n = ATT_PROJ_ROWS // dil
    out = pl.pallas_call(
        functools.partial(_proj_att_kernel, dil=dil),
        grid=(T // ATT_PROJ_ROWS,),
        in_specs=[pl.BlockSpec((ATT_PROJ_ROWS, LANES), functools.partial(lambda i, c: (i, c), c=c))
                  for c in range(LANE_CHUNKS)]
                 + [pl.BlockSpec((D_MODEL, 3 * ATT_WIDTH), lambda i: (0, 0))],
        out_specs=pl.BlockSpec((None, dil, None, n, 3 * ATT_WIDTH), lambda i: (i // per, 0, i % per, 0, 0)),
        out_shape=jax.ShapeDtypeStruct((T // ATT_TILE, dil, per, n, 3 * ATT_WIDTH), BF16),
        compiler_params=_cparams("parallel"),
        name=f"proj_att_d{dil}",
    )(*([x2d] * LANE_CHUNKS), w)
    return out.reshape(T // ATT_TILE, dil, ATT_TILE // dil, 3 * ATT_WIDTH)


def _att_pair(q2, kp, kc, vp, vc, bias_ref, g, first):
    def head0_lanes(rows, dtype):
        lane = lax.broadcasted_iota(jnp.int32, (rows, 2 * ATT_HEAD_DIM), 1)
        return lane.astype(F32).astype(dtype) < ATT_HEAD_DIM

    lo_q = head0_lanes(ATT_BLOCK, BF16)
    lo_v = head0_lanes(2 * ATT_BLOCK, BF16)
    k2 = jnp.concatenate([kp, kc], axis=0)
    v2 = jnp.concatenate([vp, vc], axis=0)
    zero = jnp.zeros_like(q2)
    ps, ms = [], []
    for hh in range(2):
        qm = jnp.where(lo_q, q2, zero) if hh == 0 else jnp.where(lo_q, zero, q2)
        s = lax.dot_general(qm, k2, (((1,), (1,)), ((), ())), preferred_element_type=F32)
        s = s + bias_ref[g, hh, first]
        m = jnp.max(s, axis=-1, keepdims=True)
        ps.append(jnp.exp(s - m).astype(BF16))
        ms.append(m)
    pcat = jnp.concatenate(ps, axis=1)
    zero_v, one_v = jnp.zeros_like(v2), jnp.ones_like(v2)
    rhs = jnp.concatenate([
        jnp.concatenate([jnp.where(lo_v, v2, zero_v), jnp.where(lo_v, one_v, zero_v)], axis=1),
        jnp.concatenate([jnp.where(lo_v, zero_v, v2), jnp.where(lo_v, zero_v, one_v)], axis=1)], axis=0)
    nd = jnp.dot(pcat, rhs, preferred_element_type=F32)
    m2 = jnp.where(head0_lanes(ATT_BLOCK, F32), ms[0], ms[1])
    return nd[:, :2 * ATT_HEAD_DIM], m2, nd[:, 2 * ATT_HEAD_DIM:]


def _att_kernel(*refs):
    (q0, kc0, vc0, kp0, vp0, q1, kc1, vc1, kp1, vp1, q2, kc2, vc2, kp2, vp2,
     bias_ref, o_ref) = refs[:17]
    ng = len(ATT_DILATIONS)
    num_s, m_s, den_s = refs[17:17 + ng], refs[17 + ng:17 + 2 * ng], refs[17 + 2 * ng:]
    first_tile = (pl.program_id(2) == 0).astype(jnp.int32)
    groups = ((q0, kc0, vc0, kp0, vp0), (q1, kc1, vc1, kp1, vp1), (q2, kc2, vc2, kp2, vp2))
    for g, dil in enumerate(ATT_DILATIONS):
        q_ref, kc_ref, vc_ref, kp_ref, vp_ref = groups[g]
        nb = ATT_TILE // dil // ATT_BLOCK
        for r in range(dil):
            for n in range(nb):
                rows = pl.ds(n * ATT_BLOCK, ATT_BLOCK)
                if n == 0:
                    prev = pl.ds((nb - 1) * ATT_BLOCK, ATT_BLOCK)
                    kp, vp, first = kp_ref[r, prev, :], vp_ref[r, prev, :], first_tile
                else:
                    prev = pl.ds((n - 1) * ATT_BLOCK, ATT_BLOCK)
                    kp, vp, first = kc_ref[r, prev, :], vc_ref[r, prev, :], 0
                num, m, den = _att_pair(q_ref[r, rows, :], kp, kc_ref[r, rows, :], vp, vc_ref[r, rows, :],
                                        bias_ref, g, first)
                if dil == 1:
                    dst = rows
                else:
                    dst = pl.ds(n * ATT_BLOCK * dil + r, ATT_BLOCK, stride=dil)
                num_s[g][dst, :] = num
                m_s[g][dst, :] = m
                den_s[g][dst, :] = den
    m_all = jnp.maximum(jnp.maximum(m_s[0][...], m_s[1][...]), m_s[2][...])
    num = jnp.zeros((ATT_TILE, 2 * ATT_HEAD_DIM), F32)
    den = jnp.zeros((ATT_TILE, 2 * ATT_HEAD_DIM), F32)
    for g in range(ng):
        sc = jnp.exp(m_s[g][...] - m_all)
        num = num + sc * num_s[g][...]
        den = den + sc * den_s[g][...]
    o_ref[...] = (num / den).astype(o_ref.dtype)


def _att_bias_table():
    qi = np.arange(ATT_BLOCK)[:, None]
    ki = np.arange(2 * ATT_BLOCK)[None, :]
    steps = qi + ATT_BLOCK - ki
    valid = (steps >= 0) & (steps <= ATT_BLOCK)
    slopes = np.array([2.0 ** (-8.0 * (h + 1) / ATT_HEADS) for h in range(ATT_HEADS)], np.float32)
    tab = np.empty((len(ATT_DILATIONS), ATT_HEADS, 2, ATT_BLOCK, 2 * ATT_BLOCK), np.float32)
    for g, dil in enumerate(ATT_DILATIONS):
        bias = -slopes[:, None, None] * (steps * dil).astype(np.float32)[None]
        tab[g, :, 0] = np.where(valid[None], bias, NEG_INF)
        tab[g, :, 1] = np.where((valid & (ki >= ATT_BLOCK))[None], bias, NEG_INF)
    return jnp.asarray(tab)


def _attention(qkv, B, S):
    tiles = S // ATT_TILE
    pair = 2 * ATT_HEAD_DIM
    npair = ATT_WIDTH // pair
    in_specs, args = [], []
    for g, dil in enumerate(ATT_DILATIONS):
        blk = (None, dil, ATT_TILE // dil, pair)
        cur = lambda b, hp, t, off: (b * tiles + t, 0, 0, off * npair + hp)
        prv = lambda b, hp, t, off: (b * tiles + jnp.maximum(t - 1, 0), 0, 0, off * npair + hp)
        in_specs += [pl.BlockSpec(blk, functools.partial(cur, off=0)),
                     pl.BlockSpec(blk, functools.partial(cur, off=1)),
                     pl.BlockSpec(blk, functools.partial(cur, off=2)),
                     pl.BlockSpec(blk, functools.partial(prv, off=1)),
                     pl.BlockSpec(blk, functools.partial(prv, off=2))]
        args += [qkv[g]] * 5
    in_specs.append(pl.BlockSpec((len(ATT_DILATIONS), 2, 2, ATT_BLOCK, 2 * ATT_BLOCK),
                                 lambda b, hp, t: (0, hp, 0, 0, 0)))
    args.append(_att_bias_table())
    scratch = [pltpu.VMEM((ATT_TILE, pair), F32) for _ in range(3 * len(ATT_DILATIONS))]
    return pl.pallas_call(
        _att_kernel,
        grid=(B, npair, tiles),
        in_specs=in_specs,
        out_specs=pl.BlockSpec((ATT_TILE, pair), lambda b, hp, t: (b * tiles + t, hp)),
        out_shape=jax.ShapeDtypeStruct((B * S, ATT_WIDTH), BF16),
        scratch_shapes=scratch,
        compiler_params=_cparams("parallel", "parallel", "arbitrary"),
        name="dilated_attention",
    )(*args)


def _att_weights(w_in_l):
    out = []
    width = len(ATT_DILATIONS) * ATT_WIDTH
    for g in range(len(ATT_DILATIONS)):
        cols = [w_in_l[:, part * width + g * ATT_WIDTH: part * width + (g + 1) * ATT_WIDTH] for part in range(3)]
        cols[0] = cols[0] * (ATT_HEAD_DIM ** -0.5)
        out.append(jnp.concatenate(cols, axis=1).astype(BF16))
    return out


def _proj_kernel(x_ref, w_ref, o_ref, *, col_tile):
    xb = x_ref[...].astype(BF16)
    for c in range(w_ref.shape[1] // col_tile):
        cols = slice(c * col_tile, (c + 1) * col_tile)
        o_ref[:, cols] = jnp.dot(xb, w_ref[:, cols], preferred_element_type=F32).astype(o_ref.dtype)


def _proj(x2d, w, col_tile):
    T, N = x2d.shape[0], w.shape[1]
    return pl.pallas_call(
        functools.partial(_proj_kernel, col_tile=col_tile),
        grid=(T // PROJ_ROWS,),
        in_specs=[pl.BlockSpec((PROJ_ROWS, D_MODEL), lambda i: (i, 0)),
                  pl.BlockSpec((D_MODEL, N), lambda i: (0, 0))],
        out_specs=pl.BlockSpec((PROJ_ROWS, N), lambda i: (i, 0)),
        out_shape=jax.ShapeDtypeStruct((T, N), BF16),
        compiler_params=_cparams("parallel"),
        name="proj_hgrn_gates",
    )(x2d, w)


def _split3(v):
    a = v.astype(BF16)
    r = v - a.astype(F32)
    b = r.astype(BF16)
    c = (r - b.astype(F32)).astype(BF16)
    return a, b, c


def _hgrn_kernel(q_ref, f_ref, i_ref, g_ref, lbl_ref, gain_ref, o_ref, state_ref):
    @pl.when(pl.program_id(1) == 0)
    def _():
        state_ref[...] = jnp.zeros_like(state_ref)

    lbl = lbl_ref[...]
    e = jnp.exp(lbl - jnp.max(lbl, axis=0, keepdims=True))
    lb = e[0:1] / jnp.sum(e, axis=0, keepdims=True)
    forget = lb + (1.0 - lb) * jax.nn.sigmoid(f_ref[...].astype(F32))
    log_f = jnp.log(forget)
    key = 1.0 - forget

    row = lax.broadcasted_iota(jnp.int32, (HG_TILE, HG_TILE), 0)
    col = lax.broadcasted_iota(jnp.int32, (HG_TILE, HG_TILE), 1)
    causal = (row >= col) & ((row // HG_CHUNK) == (col // HG_CHUNK))
    tri = jnp.where(causal, 1.0, 0.0).astype(BF16)
    b = sum(jnp.dot(tri, t, preferred_element_type=F32) for t in _split3(log_f))
    eb = jnp.exp(b)
    q_dec = (q_ref[...].astype(F32) * eb).astype(BF16)
    k_inv = key * jnp.exp(-b)
    xi = i_ref[...].astype(F32)
    val = (xi * jax.nn.sigmoid(xi)).astype(BF16)
    k_inv_b = k_inv.astype(BF16)

    n_chunks = HG_TILE // HG_CHUNK
    last_rows = [eb[(c + 1) * HG_CHUNK - 1:(c + 1) * HG_CHUNK, :] for c in range(n_chunks)]
    dec_rows = jnp.concatenate([jnp.broadcast_to(r, (HG_CHUNK, HG_WIDTH)) for r in last_rows], axis=0)
    k_end = (k_inv * dec_rows).astype(BF16)
    chunk_of_row = (lax.broadcasted_iota(jnp.int32, (HG_TILE, HG_DIM), 0) // HG_CHUNK).astype(F32).astype(BF16)
    in_chunk = [chunk_of_row == c for c in range(n_chunks)]
    zero = jnp.zeros((HG_TILE, HG_DIM), BF16)

    def per_chunk_columns(t):
        return jnp.concatenate([jnp.where(m, t, zero) for m in in_chunk], axis=1)

    outs = []
    for h in range(HG_HEADS):
        cols = slice(h * HG_DIM, (h + 1) * HG_DIM)
        qd, ki, vv = q_dec[:, cols], k_inv_b[:, cols], val[:, cols]
        a = lax.dot_general(qd, ki, (((1,), (1,)), ((), ())), preferred_element_type=F32)
        a = jnp.where(causal, a, 0.0).astype(BF16)
        o_intra = jnp.dot(a, vv, preferred_element_type=F32)
        upd = lax.dot_general(vv, per_chunk_columns(k_end[:, cols]), (((0,), (0,)), ((), ())),
                              preferred_element_type=F32)
        st = state_ref[h]
        entering = []
        for c in range(n_chunks):
            entering.append(st.astype(BF16))
            st = st * last_rows[c][:, cols] + upd[:, c * HG_DIM:(c + 1) * HG_DIM]
        state_ref[h] = st
        o_inter = lax.dot_general(per_chunk_columns(qd), jnp.concatenate(entering, axis=1),
                                  (((1,), (1,)), ((), ())), preferred_element_type=F32)
        o = o_intra + o_inter
        o = o * lax.rsqrt(jnp.mean(jnp.square(o), axis=-1, keepdims=True) + RMS_EPS)
        outs.append(o)
    o = jnp.concatenate(outs, axis=1) * gain_ref[...]
    gg = g_ref[...].astype(F32)
    o_ref[...] = (o * (gg * jax.nn.sigmoid(gg))).astype(o_ref.dtype)


def _hgrn(u_hg, lb_logits, gain, B, S):
    tiles = S // HG_TILE
    col = lambda j: pl.BlockSpec((HG_TILE, HG_WIDTH), functools.partial(lambda b, t, j: (b * tiles + t, j), j=j))
    return pl.pallas_call(
        _hgrn_kernel,
        grid=(B, tiles),
        in_specs=[col(0), col(1), col(2), col(3),
                  pl.BlockSpec((2, HG_WIDTH), lambda b, t: (0, 0)),
                  pl.BlockSpec((1, HG_WIDTH), lambda b, t: (0, 0))],
        out_specs=pl.BlockSpec((HG_TILE, HG_WIDTH), lambda b, t: (b * tiles + t, 0)),
        out_shape=jax.ShapeDtypeStruct((B * S, HG_WIDTH), BF16),
        scratch_shapes=[pltpu.VMEM((HG_HEADS, HG_DIM, HG_DIM), F32)],
        compiler_params=_cparams("parallel", "arbitrary"),
        name="hgrn2",
    )(u_hg, u_hg, u_hg, u_hg, lb_logits, gain)


def _load_row_tiles(ref, n, start=0):
    return jnp.concatenate([ref[pl.ds(start + c, n, stride=LANE_CHUNKS), :] for c in range(LANE_CHUNKS)], axis=1)


def _store_row_tiles(ref, val, n):
    for c in range(LANE_CHUNKS):
        ref[pl.ds(c, n, stride=LANE_CHUNKS), :] = val[:, c * LANES:(c + 1) * LANES]


def _layer_norm(z, w, b):
    mu = jnp.mean(z, axis=-1, keepdims=True)
    zc = z - mu
    var = jnp.mean(jnp.square(zc), axis=-1, keepdims=True)
    return zc * lax.rsqrt(var + LN_EPS) * w + b


def _merge_kernel(ya_ref, yh_ref, ga_ref, gh_ref, x_ref, wa_ref, wh_ref, wo_ref, lw_ref, lb_ref, o_ref):
    ma = jnp.dot(ya_ref[...], wa_ref[...], preferred_element_type=F32)
    mh = jnp.dot(yh_ref[...], wh_ref[...], preferred_element_type=F32)
    merged = (jax.nn.sigmoid(ga_ref[...].astype(F32)) * ma + jax.nn.sigmoid(gh_ref[...].astype(F32)) * mh)
    z = DEEPNORM_ALPHA * x_ref[...] + jnp.dot(merged.astype(BF16), wo_ref[...], preferred_element_type=F32)
    _store_row_tiles(o_ref, _layer_norm(z, lw_ref[...], lb_ref[...]), MIX_ROWS)


def _merge(y_att, y_hg, u_hg, x2d, w_a, w_h, w_o, ln_w, ln_b):
    T = x2d.shape[0]
    rows = lambda width, j=0: pl.BlockSpec((MIX_ROWS, width), functools.partial(lambda i, j: (i, j), j=j))
    full = lambda a: pl.BlockSpec(a.shape, lambda i: (0, 0))
    return pl.pallas_call(
        _merge_kernel,
        grid=(T // MIX_ROWS,),
        in_specs=[rows(ATT_WIDTH), rows(HG_WIDTH), rows(D_MODEL, 4), rows(D_MODEL, 5), rows(D_MODEL),
                  full(w_a), full(w_h), full(w_o), full(ln_w), full(ln_b)],
        out_specs=pl.BlockSpec((MIX_ROWS * LANE_CHUNKS, LANES), lambda i: (i, 0)),
        out_shape=jax.ShapeDtypeStruct((T * LANE_CHUNKS, LANES), F32),
        compiler_params=_cparams("parallel"),
        name="merge_ln1",
    )(y_att, y_hg, u_hg, u_hg, x2d, w_a, w_h, w_o, ln_w, ln_b)


def _first_argmax(v, ids, n):
    mx = jnp.max(v, axis=0, keepdims=True)
    return mx, jnp.min(jnp.where(v == mx, ids, n), axis=0, keepdims=True)


def _route_kernel(x1_ref, p_ref, wrt_ref, rb_ref, wsg_ref, wsu_ref, wsd_ref, wpg_ref, wpp_ref,
                  base_ref, idx_ref, gate_ref, rank_ref, cnt_ref, carry_ref):
    @pl.when(pl.program_id(0) % (MOE_TILE // MIX_ROWS) == 0)
    def _():
        carry_ref[...] = jnp.zeros_like(carry_ref)

    x1 = _load_row_tiles(x1_ref, MIX_ROWS)
    x1b = x1.astype(BF16)
    logits = lax.dot_general(wrt_ref[...], x1, (((1,), (1,)), ((), ())), preferred_element_type=F32,
                             precision=lax.Precision.HIGHEST)
    s = jax.nn.sigmoid(logits)
    sel = s + rb_ref[...]
    eid = lax.broadcasted_iota(jnp.int32, (N_EXPERTS, MIX_ROWS), 0)
    neg = -jnp.inf

    grp = sel.reshape(N_GROUPS, GROUP_SIZE, MIX_ROWS)
    mid = lax.broadcasted_iota(jnp.int32, grp.shape, 1)
    m1 = jnp.max(grp, axis=1, keepdims=True)
    i1 = jnp.min(jnp.where(grp == m1, mid, GROUP_SIZE), axis=1, keepdims=True)
    m2 = jnp.max(jnp.where(mid == i1, neg, grp), axis=1, keepdims=True)
    gscore = (m1 + m2).reshape(N_GROUPS, MIX_ROWS)
    gid = lax.broadcasted_iota(jnp.int32, (N_GROUPS, MIX_ROWS), 0)
    gsel = jnp.zeros((N_GROUPS, MIX_ROWS), jnp.bool_)
    for _ in range(TOPK_GROUPS):
        _, gi = _first_argmax(gscore, gid, N_GROUPS)
        hit = gid == gi
        gsel = gsel | hit
        gscore = jnp.where(hit, neg, gscore)
    emask = jnp.broadcast_to(gsel.reshape(N_GROUPS, 1, MIX_ROWS), grp.shape).reshape(N_EXPERTS, MIX_ROWS)
    cand = jnp.where(emask, sel, neg)

    idxs, gates = [], []
    chosen = jnp.zeros((N_EXPERTS, MIX_ROWS), jnp.bool_)
    for _ in range(TOP_K):
        _, ei = _first_argmax(cand, eid, N_EXPERTS)
        hit = eid == ei
        idxs.append(ei)
        gates.append(jnp.sum(jnp.where(hit, s, 0.0), axis=0, keepdims=True))
        chosen = chosen | hit
        cand = jnp.where(hit, neg, cand)
    g = jnp.concatenate(gates, axis=0)
    g = g / jnp.sum(g, axis=0, keepdims=True) * ROUTED_SCALE
    idx_ref[...] = jnp.concatenate(idxs, axis=0)
    gate_ref[...] = g

    onehot = jnp.where(chosen, 1.0, 0.0)
    tr = lax.broadcasted_iota(jnp.int32, (MIX_ROWS, MIX_ROWS), 0)
    tc = lax.broadcasted_iota(jnp.int32, (MIX_ROWS, MIX_ROWS), 1)
    before = jnp.where(tr < tc, 1.0, 0.0).astype(BF16)
    prefix = jnp.dot(onehot.astype(BF16), before, preferred_element_type=F32)
    rankfull = (carry_ref[:, 0:1] + prefix).astype(jnp.int32)
    rank_ref[...] = jnp.concatenate(
        [jnp.sum(jnp.where(eid == ei, rankfull, 0), axis=0, keepdims=True) for ei in idxs], axis=0)
    total = carry_ref[...] + jnp.sum(onehot, axis=1, keepdims=True)
    carry_ref[...] = total
    cnt_ref[...] = total.astype(jnp.int32)

    hg = jnp.dot(x1b, wsg_ref[...], preferred_element_type=F32)
    hu = jnp.dot(x1b, wsu_ref[...], preferred_element_type=F32)
    shared = jnp.dot((hg * jax.nn.sigmoid(hg) * hu).astype(BF16), wsd_ref[...], preferred_element_type=F32)
    ple = (jax.nn.sigmoid(jnp.dot(x1b, wpg_ref[...], preferred_element_type=F32))
           * jnp.dot(p_ref[...].astype(BF16), wpp_ref[...], preferred_element_type=F32))
    _store_row_tiles(base_ref, DEEPNORM_ALPHA * x1 + shared + ple, MIX_ROWS)


def _route(x1, p2d, wr_t, rbias, wsg, wsu, wsd, wpg, wpp):
    T = x1.shape[0] // LANE_CHUNKS
    per_tile = MOE_TILE // MIX_ROWS
    full = lambda a: pl.BlockSpec(a.shape, lambda i: (0, 0))
    tok = pl.BlockSpec((TOP_K, MIX_ROWS), lambda i: (0, i))
    row_tiles = pl.BlockSpec((MIX_ROWS * LANE_CHUNKS, LANES), lambda i: (i, 0))
    return pl.pallas_call(
        _route_kernel,
        grid=(T // MIX_ROWS,),
        in_specs=[row_tiles,
                  pl.BlockSpec((MIX_ROWS, PLE_DIM), lambda i: (i, 0)),
                  full(wr_t), full(rbias), full(wsg), full(wsu), full(wsd), full(wpg), full(wpp)],
        out_specs=[row_tiles, tok, tok, tok,
                   pl.BlockSpec((None, N_EXPERTS, LANES), lambda i: (i // per_tile, 0, 0))],
        out_shape=[jax.ShapeDtypeStruct((T * LANE_CHUNKS, LANES), F32),
                   jax.ShapeDtypeStruct((TOP_K, T), jnp.int32),
                   jax.ShapeDtypeStruct((TOP_K, T), F32),
                   jax.ShapeDtypeStruct((TOP_K, T), jnp.int32),
                   jax.ShapeDtypeStruct((T // MOE_TILE, N_EXPERTS, LANES), jnp.int32)],
        scratch_shapes=[pltpu.VMEM((N_EXPERTS, LANES), F32)],
        compiler_params=_cparams("arbitrary"),
        name="route_shared_ple",
    )(x1, p2d, wr_t, rbias, wsg, wsu, wsd, wpg, wpp)


def _plan_sc_kernel(off_hbm, idx_hbm, rank_hbm, list_hbm, off_v, idx_v, rank_v, list_v):
    n_tokens = idx_hbm.shape[0] // TOP_K
    worker = lax.axis_index("subcore") * SC_CORES + lax.axis_index("core")

    @pl.when(worker < n_tokens // MOE_TILE)
    def _():
        pltpu.sync_copy(off_hbm.at[pl.ds(worker * N_EXPERTS, N_EXPERTS)], off_v)
        lane = lax.iota(jnp.int32, SC_LANES)
        for k in range(TOP_K):
            row = pl.ds(k * n_tokens + worker * MOE_TILE, MOE_TILE)
            pltpu.sync_copy(idx_hbm.at[row], idx_v)
            pltpu.sync_copy(rank_hbm.at[row], rank_v)

            @pl.loop(0, MOE_TILE // SC_LANES)
            def _(i):
                at = i * SC_LANES
                pos = plsc.load_gather(off_v, [idx_v[pl.ds(at, SC_LANES)]]) + rank_v[pl.ds(at, SC_LANES)]
                plsc.store_scatter(list_v, [pos], (lane + at) * TOP_K + k)

        @pl.loop(0, LIST_PAD // SC_LANES)
        def _(i):
            list_v[pl.ds(MOE_TILE * TOP_K + i * SC_LANES, SC_LANES)] = jnp.zeros((SC_LANES,), jnp.int32)

        pltpu.sync_copy(list_v, list_hbm.at[pl.ds(worker * LIST_LEN, LIST_LEN)])


def _plan_sc(off, idx, rank):
    n_tiles = idx.shape[1] // MOE_TILE
    assert n_tiles <= SC_CORES * SC_SUBCORES
    mesh = plsc.VectorSubcoreMesh(core_axis_name="core", subcore_axis_name="subcore",
                                  num_cores=SC_CORES, num_subcores=SC_SUBCORES)
    return pl.kernel(
        _plan_sc_kernel,
        out_type=jax.ShapeDtypeStruct((n_tiles * LIST_LEN,), jnp.int32),
        mesh=mesh,
        scratch_types=[pltpu.VMEM((N_EXPERTS,), jnp.int32), pltpu.VMEM((MOE_TILE,), jnp.int32),
                       pltpu.VMEM((MOE_TILE,), jnp.int32), pltpu.VMEM((LIST_LEN,), jnp.int32)],
        compiler_params=pltpu.CompilerParams(needs_layout_passes=False),
        name="moe_plan_sc",
    )(off, idx.reshape(-1), rank.reshape(-1))


def _moe_tile_kernel(cnt_ref, off_ref, list_ref, gate_ref, wg_ref, wu_ref, wd_ref, lw_ref, lb_ref, x_hbm, base_hbm,
                     o_hbm, x_s, acc_s, xg_a, xg_b, xg_c, y_a, y_b, y_c, stage_s, sem):
    tile, e = pl.program_id(0), pl.program_id(1)
    rows_of = lambda ref, r, n: ref.at[pl.ds(pl.multiple_of(r * LANE_CHUNKS, LANE_CHUNKS), n * LANE_CHUNKS), :]
    tile_rows = pl.ds(pl.multiple_of(tile * (MOE_TILE * LANE_CHUNKS), LANE_CHUNKS), MOE_TILE * LANE_CHUNKS)

    pair = tile * N_EXPERTS + e
    last_pair = pl.num_programs(0) * N_EXPERTS - 1
    n, off = cnt_ref[pair], off_ref[pair]
    first_row = lambda code: pl.multiple_of(code & -LANE_CHUNKS, LANE_CHUNKS)
    tile_at = lambda ref, r: ref.at[pl.ds(r, LANE_CHUNKS), :]

    def gather_group(xg, first, jb):
        at = first + jb * GATHER_GROUP
        rows = [tile_at(x_s, first_row(list_ref[at + u]))[...] for u in range(GATHER_GROUP)]
        rows_of(xg, jb * GATHER_GROUP, GATHER_GROUP)[...] = jnp.concatenate(rows, axis=0)

    def gather_loop(xg, first):
        def body(jb, cc):
            gather_group(xg, first, jb)
            return cc

        lax.fori_loop(0, MOE_CHUNK // GATHER_GROUP, body, 0)

    def swiglu(xg, y):
        xb = _load_row_tiles(xg, MOE_CHUNK).astype(BF16)
        hg = jnp.dot(xb, wg_ref[...], preferred_element_type=F32)
        hu = jnp.dot(xb, wu_ref[...], preferred_element_type=F32)
        act = (hg * jax.nn.sigmoid(hg) * hu).astype(BF16)
        _store_row_tiles(y, jnp.dot(act, wd_ref[...], preferred_element_type=F32), MOE_CHUNK)

    def scatter_group(y, first, j0, live):
        codes = [list_ref[first + j0 + u] for u in range(live)]
        gates = [gate_ref[code] for code in codes]
        dsts = [first_row(code) for code in codes]
        yv = rows_of(y, j0, live)[...]
        vals = [tile_at(acc_s, d)[...] + g * yv[u * LANE_CHUNKS:(u + 1) * LANE_CHUNKS]
                for u, (d, g) in enumerate(zip(dsts, gates))]
        for d, val in reversed(list(zip(dsts, vals))):
            tile_at(acc_s, d)[...] = val

    def scatter_loop(y, first, m):
        def body(jg, cc):
            scatter_group(y, first, jg * SCATTER_GROUP, SCATTER_GROUP)
            return cc

        lax.fori_loop(0, m // SCATTER_GROUP, body, 0)
        for live in range(1, SCATTER_GROUP):
            @pl.when(m % SCATTER_GROUP == live)
            def _(live=live):
                scatter_group(y, first, m - live, live)

    @pl.when(e == 0)
    def _():
        load_x = pltpu.make_async_copy(x_hbm.at[tile_rows, :], x_s, sem.at[0])
        load_base = pltpu.make_async_copy(base_hbm.at[tile_rows, :], acc_s, sem.at[1])
        load_x.start()
        load_base.start()
        y_b[...] = jnp.zeros_like(y_b)
        load_x.wait()
        gather_loop(xg_a, off)
        load_base.wait()

    prev_off = off_ref[jnp.maximum(pair - 1, 0)]
    next_off = off_ref[jnp.minimum(pair + 1, last_pair)]
    live_row = lax.broadcasted_iota(jnp.int32, (MOE_CHUNK, 2 * LANES), 0) < n

    def run_expert(xg_cur, y_cur, xg_nxt, y_prv):
        gathers = [functools.partial(gather_group, xg_nxt, next_off, jb) for jb in range(MOE_CHUNK // GATHER_GROUP)]
        scatters = [functools.partial(scatter_group, y_prv, prev_off, jg * SCATTER_GROUP, SCATTER_GROUP)
                    for jg in range(MOE_CHUNK // SCATTER_GROUP)]
        side = [s for both in zip(gathers, scatters) for s in both]
        pieces = 2 + LANE_CHUNKS // 2
        per_piece = -(-len(side) // pieces)

        def side_work(i):
            for s in side[i * per_piece:(i + 1) * per_piece]:
                s()

        xb = _load_row_tiles(xg_cur, MOE_CHUNK).astype(BF16)
        hg = jnp.dot(xb, wg_ref[...], preferred_element_type=F32)
        side_work(0)
        hu = jnp.dot(xb, wu_ref[...], preferred_element_type=F32)
        side_work(1)
        act = (hg * jax.nn.sigmoid(hg) * hu).astype(BF16)
        for q in range(LANE_CHUNKS // 2):
            out = jnp.dot(act, wd_ref[:, q * 2 * LANES:(q + 1) * 2 * LANES], preferred_element_type=F32)
            out = jnp.where(live_row, out, 0.0)
            for c in range(2):
                y_cur[pl.ds(2 * q + c, MOE_CHUNK, stride=LANE_CHUNKS), :] = out[:, c * LANES:(c + 1) * LANES]
            side_work(2 + q)

    @pl.when(e % 2 == 0)
    def _():
        run_expert(xg_a, y_a, xg_b, y_b)

    @pl.when(e % 2 == 1)
    def _():
        run_expert(xg_b, y_b, xg_a, y_a)

    def extra_chunk(c, carry):
        first = off + c * MOE_CHUNK
        gather_loop(xg_c, first)
        swiglu(xg_c, y_c)
        scatter_loop(y_c, first, jnp.minimum(MOE_CHUNK, n - c * MOE_CHUNK))
        return carry

    lax.fori_loop(1, (n + MOE_CHUNK - 1) // MOE_CHUNK, extra_chunk, 0)

    @pl.when(e == N_EXPERTS - 1)
    def _():
        scatter_loop(y_b, off, jnp.minimum(MOE_CHUNK, n))
        n_pieces = MOE_TILE // LN_ROWS
        store = lambda c: pltpu.make_async_copy(
            stage_s.at[c % 2], o_hbm.at[pl.ds(tile * MOE_TILE + c * LN_ROWS, LN_ROWS), :], sem.at[2 + c % 2])
        for c in range(n_pieces):
            z = _load_row_tiles(acc_s, LN_ROWS, c * LN_ROWS * LANE_CHUNKS)
            if c >= 2:
                store(c - 2).wait()
            stage_s[c % 2] = _layer_norm(z, lw_ref[...], lb_ref[...])
            store(c).start()
        store(n_pieces - 2).wait()
        store(n_pieces - 1).wait()


def _moe_tiles(x1, base, tok_list, gate, cnt, off, wg, wu, wd, ln_w, ln_b):
    T = x1.shape[0] // LANE_CHUNKS
    w_spec = lambda shape: pl.BlockSpec((None,) + shape, lambda i, e, cnt, off: (e, 0, 0))
    vec = pl.BlockSpec((1, D_MODEL), lambda i, e, cnt, off: (0, 0))
    hbm = pl.BlockSpec(memory_space=pl.ANY)
    tile_rows = MOE_TILE * LANE_CHUNKS
    return pl.pallas_call(
        _moe_tile_kernel,
        grid_spec=pltpu.PrefetchScalarGridSpec(
            num_scalar_prefetch=2,
            grid=(T // MOE_TILE, N_EXPERTS),
            in_specs=[pl.BlockSpec((LIST_LEN,), lambda i, e, cnt, off: (i,), memory_space=pltpu.SMEM),
                      pl.BlockSpec((MOE_TILE * TOP_K,), lambda i, e, cnt, off: (i,), memory_space=pltpu.SMEM),
                      w_spec((D_MODEL, EXPERT_FF)), w_spec((D_MODEL, EXPERT_FF)), w_spec((EXPERT_FF, D_MODEL)),
                      vec, vec, hbm, hbm],
            out_specs=hbm,
            scratch_shapes=[pltpu.VMEM((tile_rows, LANES), F32),
                            pltpu.VMEM((tile_rows, LANES), F32)]
                           + [pltpu.VMEM((MOE_CHUNK * LANE_CHUNKS, LANES), F32)] * 6
                           + [pltpu.VMEM((2, LN_ROWS, D_MODEL), F32),
                              pltpu.SemaphoreType.DMA((4,))],
        ),
        out_shape=jax.ShapeDtypeStruct((T, D_MODEL), F32),
        compiler_params=_cparams("arbitrary", "arbitrary"),
        name="moe_tiles_ln2",
    )(cnt, off, tok_list, gate, wg, wu, wd, ln_w, ln_b, x1, base)


def kernel(x, p, w_in, hgrn_lb_logits, hgrn_norm_w, w_branch_att, w_branch_hgrn, w_out, ln1_w, ln1_b, router_w, router_bias, expert_w_gate, expert_w_up, expert_w_down, shared_w_gate, shared_w_up, shared_w_down, ple_gate_w, ple_proj_w, ln2_w, ln2_b):
    B, S, D = x.shape
    T = B * S
    l = 0
    x2d = x.reshape(T, D)
    bf = lambda a: a.astype(BF16)

    ws = _att_weights(w_in[l])
    qkv = [_proj_att(x2d, ws[g], d) for g, d in enumerate(ATT_DILATIONS)]
    y_att = _attention(qkv, B, S)
    u_hg = _proj(x2d, bf(w_in[l][:, 3 * len(ATT_DILATIONS) * ATT_WIDTH:]), 1536)
    y_hg = _hgrn(u_hg, hgrn_lb_logits, hgrn_norm_w[l:l + 1], B, S)
    x1 = _merge(y_att, y_hg, u_hg, x2d, bf(w_branch_att[l]), bf(w_branch_hgrn[l]), bf(w_out[l]),
                ln1_w[l:l + 1], ln1_b[l:l + 1])

    base, idx, gate, rank, counts = _route(
        x1, p[l].reshape(T, PLE_DIM), router_w[l].T, router_bias[l].reshape(N_EXPERTS, 1),
        bf(shared_w_gate[l]), bf(shared_w_up[l]), bf(shared_w_down[l]), bf(ple_gate_w[l]), bf(ple_proj_w[l]))
    cnt = counts[:, :, 0]
    off = jnp.cumsum(cnt, axis=1) - cnt
    cnt, off = cnt.reshape(-1), off.reshape(-1)
    tok_list = _plan_sc(off, idx, rank)
    gate_list = gate.T.reshape(-1)
    out = _moe_tiles(x1, base, tok_list, gate_list, cnt, off, bf(expert_w_gate[l]), bf(expert_w_up[l]),
                     bf(expert_w_down[l]), ln2_w[l:l + 1], ln2_b[l:l + 1])
    return out.reshape(B, S, D)
```

```python
import functools

import jax
import jax.numpy as jnp
import numpy as np
from jax import lax
from jax.experimental import pallas as pl
from jax.experimental.pallas import tpu as pltpu
from jax.experimental.pallas import tpu_sc as plsc

F32 = jnp.float32
BF16 = jnp.bfloat16

D_MODEL = 1024
ATT_HEAD_DIM = 64
ATT_HEADS = 8
ATT_DILATIONS = (1, 4, 16)
ATT_BLOCK = 128
ATT_WIDTH = ATT_HEADS * ATT_HEAD_DIM
ATT_TILE = ATT_BLOCK * max(ATT_DILATIONS)
NEG_INF = -1e30

HG_HEADS = 8
HG_DIM = 128
HG_WIDTH = HG_HEADS * HG_DIM
HG_CHUNK = 32
HG_TILE = 256
RMS_EPS = 1e-6

N_EXPERTS = 64
TOP_K = 8
TOP_K_BITS = 3
N_GROUPS = 8
GROUP_SIZE = N_EXPERTS // N_GROUPS
TOPK_GROUPS = 4
EXPERT_FF = 256
ROUTED_SCALE = 2.5
PLE_DIM = 256
LN_EPS = 1e-5
DEPTH = 1
DEEPNORM_ALPHA = (2.0 * DEPTH) ** 0.25

LANES = 128
LANE_CHUNKS = D_MODEL // LANES
PROJ_ROWS = 512
ATT_PROJ_ROWS = 1024
MIX_ROWS = 512
MOE_TILE = 4096
MOE_CHUNK = 576
LN_ROWS = 256
LIST_PAD = 1024
LIST_LEN = MOE_TILE * TOP_K + LIST_PAD
GATHER_GROUP = 8
SCATTER_GROUP = 8
V7X_VMEM_LIMIT = 56 * 1024 * 1024
SC_CORES, SC_SUBCORES, SC_LANES = 2, 16, 16


def _cparams(*sem):
    return pltpu.CompilerParams(dimension_semantics=sem, vmem_limit_bytes=V7X_VMEM_LIMIT)


def _proj_att_kernel(*refs, dil):
    x_refs, w_ref, o_ref = refs[:LANE_CHUNKS], refs[LANE_CHUNKS], refs[LANE_CHUNKS + 1]
    n = ATT_PROJ_ROWS // dil

    def rows(ref):
        if dil == 1:
            return ref[...]
        return jnp.concatenate([ref[pl.ds(r, n, stride=dil), :] for r in range(dil)], axis=0)

    xp = jnp.concatenate([rows(ref).astype(BF16) for ref in x_refs], axis=1)
    y = jnp.dot(xp, w_ref[...], preferred_element_type=F32)
    o_ref[...] = y.astype(BF16).reshape(dil, n, 3 * ATT_WIDTH)


def _proj_att(x2d, w, dil):
    T = x2d.shape[0]
    per = ATT_TILE // ATT_PROJ_ROWS
    n = ATT_PROJ_ROWS // dil
    out = pl.pallas_call(
        functools.partial(_proj_att_kernel, dil=dil),
        grid=(T // ATT_PROJ_ROWS,),
        in_specs=[pl.BlockSpec((ATT_PROJ_ROWS, LANES), functools.partial(lambda i, c: (i, c), c=c))
                  for c in range(LANE_CHUNKS)]
                 + [pl.BlockSpec((D_MODEL, 3 * ATT_WIDTH), lambda i: (0, 0))],
        out_specs=pl.BlockSpec((None, dil, None, n, 3 * ATT_WIDTH), lambda i: (i // per, 0, i % per, 0, 0)),
        out_shape=jax.ShapeDtypeStruct((T // ATT_TILE, dil, per, n, 3 * ATT_WIDTH), BF16),
        compiler_params=_cparams("parallel"),
        name=f"proj_att_d{dil}",
    )(*([x2d] * LANE_CHUNKS), w)
    return out.reshape(T // ATT_TILE, dil, ATT_TILE // dil, 3 * ATT_WIDTH)


def _att_pair(q2, kp, kc, vp, vc, bias_ref, g, first):
    def head0_lanes(rows, dtype):
        lane = lax.broadcasted_iota(jnp.int32, (rows, 2 * ATT_HEAD_DIM), 1)
        return lane.astype(F32).astype(dtype) < ATT_HEAD_DIM

    lo_q = head0_lanes(ATT_BLOCK, BF16)
    lo_v = head0_lanes(2 * ATT_BLOCK, BF16)
    k2 = jnp.concatenate([kp, kc], axis=0)
    v2 = jnp.concatenate([vp, vc], axis=0)
    zero = jnp.zeros_like(q2)
    ps, ms = [], []
    for hh in range(2):
        qm = jnp.where(lo_q, q2, zero) if hh == 0 else jnp.where(lo_q, zero, q2)
        s = lax.dot_general(qm, k2, (((1,), (1,)), ((), ())), preferred_element_type=F32)
        s = s + bias_ref[g, hh, first]
        m = jnp.max(s, axis=-1, keepdims=True)
        ps.append(jnp.exp(s - m).astype(BF16))
        ms.append(m)
    pcat = jnp.concatenate(ps, axis=1)
    zero_v, one_v = jnp.zeros_like(v2), jnp.ones_like(v2)
    rhs = jnp.concatenate([
        jnp.concatenate([jnp.where(lo_v, v2, zero_v), jnp.where(lo_v, one_v, zero_v)], axis=1),
        jnp.concatenate([jnp.where(lo_v, zero_v, v2), jnp.where(lo_v, zero_v, one_v)], axis=1)], axis=0)
    nd = jnp.dot(pcat, rhs, preferred_element_type=F32)
    m2 = jnp.where(head0_lanes(ATT_BLOCK, F32), ms[0], ms[1])
    return nd[:, :2 * ATT_HEAD_DIM], m2, nd[:, 2 * ATT_HEAD_DIM:]


def _att_kernel(*refs):
    (q0, kc0, vc0, kp0, vp0, q1, kc1, vc1, kp1, vp1, q2, kc2, vc2, kp2, vp2,
     bias_ref, o_ref) = refs[:17]
    ng = len(ATT_DILATIONS)
    num_s, m_s, den_s = refs[17:17 + ng], refs[17 + ng:17 + 2 * ng], refs[17 + 2 * ng:]
    first_tile = (pl.program_id(2) == 0).astype(jnp.int32)
    groups = ((q0, kc0, vc0, kp0, vp0), (q1, kc1, vc1, kp1, vp1), (q2, kc2, vc2, kp2, vp2))
    for g, dil in enumerate(ATT_DILATIONS):
        q_ref, kc_ref, vc_ref, kp_ref, vp_ref = groups[g]
        nb = ATT_TILE // dil // ATT_BLOCK
        for r in range(dil):
            for n in range(nb):
                rows = pl.ds(n * ATT_BLOCK, ATT_BLOCK)
                if n == 0:
                    prev = pl.ds((nb - 1) * ATT_BLOCK, ATT_BLOCK)
                    kp, vp, first = kp_ref[r, prev, :], vp_ref[r, prev, :], first_tile
                else:
                    prev = pl.ds((n - 1) * ATT_BLOCK, ATT_BLOCK)
                    kp, vp, first = kc_ref[r, prev, :], vc_ref[r, prev, :], 0
                num, m, den = _att_pair(q_ref[r, rows, :], kp, kc_ref[r, rows, :], vp, vc_ref[r, rows, :],
                                        bias_ref, g, first)
                if dil == 1:
                    dst = rows
                else:
                    dst = pl.ds(n * ATT_BLOCK * dil + r, ATT_BLOCK, stride=dil)
                num_s[g][dst, :] = num
                m_s[g][dst, :] = m
                den_s[g][dst, :] = den
    m_all = jnp.maximum(jnp.maximum(m_s[0][...], m_s[1][...]), m_s[2][...])
    num = jnp.zeros((ATT_TILE, 2 * ATT_HEAD_DIM), F32)
    den = jnp.zeros((ATT_TILE, 2 * ATT_HEAD_DIM), F32)
    for g in range(ng):
        sc = jnp.exp(m_s[g][...] - m_all)
        num = num + sc * num_s[g][...]
        den = den + sc * den_s[g][...]
    o_ref[...] = (num / den).astype(o_ref.dtype)


def _att_bias_table():
    qi = np.arange(ATT_BLOCK)[:, None]
    ki = np.arange(2 * ATT_BLOCK)[None, :]
    steps = qi + ATT_BLOCK - ki
    valid = (steps >= 0) & (steps <= ATT_BLOCK)
    slopes = np.array([2.0 ** (-8.0 * (h + 1) / ATT_HEADS) for h in range(ATT_HEADS)], np.float32)
    tab = np.empty((len(ATT_DILATIONS), ATT_HEADS, 2, ATT_BLOCK, 2 * ATT_BLOCK), np.float32)
    for g, dil in enumerate(ATT_DILATIONS):
        bias = -slopes[:, None, None] * (steps * dil).astype(np.float32)[None]
        tab[g, :, 0] = np.where(valid[None], bias, NEG_INF)
        tab[g, :, 1] = np.where((valid & (ki >= ATT_BLOCK))[None], bias, NEG_INF)
    return jnp.asarray(tab)


def _attention(qkv, B, S):
    tiles = S // ATT_TILE
    pair = 2 * ATT_HEAD_DIM
    npair = ATT_WIDTH // pair
    in_specs, args = [], []
    for g, dil in enumerate(ATT_DILATIONS):
        blk = (None, dil, ATT_TILE // dil, pair)
        cur = lambda b, hp, t, off: (b * tiles + t, 0, 0, off * npair + hp)
        prv = lambda b, hp, t, off: (b * tiles + jnp.maximum(t - 1, 0), 0, 0, off * npair + hp)
        in_specs += [pl.BlockSpec(blk, functools.partial(cur, off=0)),
                     pl.BlockSpec(blk, functools.partial(cur, off=1)),
                     pl.BlockSpec(blk, functools.partial(cur, off=2)),
                     pl.BlockSpec(blk, functools.partial(prv, off=1)),
                     pl.BlockSpec(blk, functools.partial(prv, off=2))]
        args += [qkv[g]] * 5
    in_specs.append(pl.BlockSpec((len(ATT_DILATIONS), 2, 2, ATT_BLOCK, 2 * ATT_BLOCK),
                                 lambda b, hp, t: (0, hp, 0, 0, 0)))
    args.append(_att_bias_table())
    scratch = [pltpu.VMEM((ATT_TILE, pair), F32) for _ in range(3 * len(ATT_DILATIONS))]
    return pl.pallas_call(
        _att_kernel,
        grid=(B, npair, tiles),
        in_specs=in_specs,
        out_specs=pl.BlockSpec((ATT_TILE, pair), lambda b, hp, t: (b * tiles + t, hp)),
        out_shape=jax.ShapeDtypeStruct((B * S, ATT_WIDTH), BF16),
        scratch_shapes=scratch,
        compiler_params=_cparams("parallel", "parallel", "arbitrary"),
        name="dilated_attention",
    )(*args)


def _att_weights(w_in_l):
    out = []
    width = len(ATT_DILATIONS) * ATT_WIDTH
    for g in range(len(ATT_DILATIONS)):
        cols = [w_in_l[:, part * width + g * ATT_WIDTH: part * width + (g + 1) * ATT_WIDTH] for part in range(3)]
        cols[0] = cols[0] * (ATT_HEAD_DIM ** -0.5)
        out.append(jnp.concatenate(cols, axis=1).astype(BF16))
    return out


def _proj_kernel(x_ref, w_ref, o_ref, *, col_tile):
    xb = x_ref[...].astype(BF16)
    for c in range(w_ref.shape[1] // col_tile):
        cols = slice(c * col_tile, (c + 1) * col_tile)
        o_ref[:, cols] = jnp.dot(xb, w_ref[:, cols], preferred_element_type=F32).astype(o_ref.dtype)


def _proj(x2d, w, col_tile):
    T, N = x2d.shape[0], w.shape[1]
    return pl.pallas_call(
        functools.partial(_proj_kernel, col_tile=col_tile),
        grid=(T // PROJ_ROWS,),
        in_specs=[pl.BlockSpec((PROJ_ROWS, D_MODEL), lambda i: (i, 0)),
                  pl.BlockSpec((D_MODEL, N), lambda i: (0, 0))],
        out_specs=pl.BlockSpec((PROJ_ROWS, N), lambda i: (i, 0)),
        out_shape=jax.ShapeDtypeStruct((T, N), BF16),
        compiler_params=_cparams("parallel"),
        name="proj_hgrn_gates",
    )(x2d, w)


def _split3(v):
    a = v.astype(BF16)
    r = v - a.astype(F32)
    b = r.astype(BF16)
    c = (r - b.astype(F32)).astype(BF16)
    return a, b, c


def _hgrn_kernel(q_ref, f_ref, i_ref, g_ref, lbl_ref, gain_ref, o_ref, state_ref):
    @pl.when(pl.program_id(1) == 0)
    def _():
        state_ref[...] = jnp.zeros_like(state_ref)

    lbl = lbl_ref[...]
    e = jnp.exp(lbl - jnp.max(lbl, axis=0, keepdims=True))
    lb = e[0:1] / jnp.sum(e, axis=0, keepdims=True)
    forget = lb + (1.0 - lb) * jax.nn.sigmoid(f_ref[...].astype(F32))
    log_f = jnp.log(forget)
    key = 1.0 - forget

    row = lax.broadcasted_iota(jnp.int32, (HG_TILE, HG_TILE), 0)
    col = lax.broadcasted_iota(jnp.int32, (HG_TILE, HG_TILE), 1)
    causal = (row >= col) & ((row // HG_CHUNK) == (col // HG_CHUNK))
    tri = jnp.where(causal, 1.0, 0.0).astype(BF16)
    b = sum(jnp.dot(tri, t, preferred_element_type=F32) for t in _split3(log_f))
    eb = jnp.exp(b)
    q_dec = (q_ref[...].astype(F32) * eb).astype(BF16)
    k_inv = key * jnp.exp(-b)
    xi = i_ref[...].astype(F32)
    val = (xi * jax.nn.sigmoid(xi)).astype(BF16)
    k_inv_b = k_inv.astype(BF16)

    n_chunks = HG_TILE // HG_CHUNK
    last_rows = [eb[(c + 1) * HG_CHUNK - 1:(c + 1) * HG_CHUNK, :] for c in range(n_chunks)]
    dec_rows = jnp.concatenate([jnp.broadcast_to(r, (HG_CHUNK, HG_WIDTH)) for r in last_rows], axis=0)
    k_end = (k_inv * dec_rows).astype(BF16)
    chunk_of_row = (lax.broadcasted_iota(jnp.int32, (HG_TILE, HG_DIM), 0) // HG_CHUNK).astype(F32).astype(BF16)
    in_chunk = [chunk_of_row == c for c in range(n_chunks)]
    zero = jnp.zeros((HG_TILE, HG_DIM), BF16)

    def per_chunk_columns(t):
        return jnp.concatenate([jnp.where(m, t, zero) for m in in_chunk], axis=1)

    outs = []
    for h in range(HG_HEADS):
        cols = slice(h * HG_DIM, (h + 1) * HG_DIM)
        qd, ki, vv = q_dec[:, cols], k_inv_b[:, cols], val[:, cols]
        a = lax.dot_general(qd, ki, (((1,), (1,)), ((), ())), preferred_element_type=F32)
        a = jnp.where(causal, a, 0.0).astype(BF16)
        o_intra = jnp.dot(a, vv, preferred_element_type=F32)
        upd = lax.dot_general(vv, per_chunk_columns(k_end[:, cols]), (((0,), (0,)), ((), ())),
                              preferred_element_type=F32)
        st = state_ref[h]
        entering = []
        for c in range(n_chunks):
            entering.append(st.astype(BF16))
            st = st * last_rows[c][:, cols] + upd[:, c * HG_DIM:(c + 1) * HG_DIM]
        state_ref[h] = st
        o_inter = lax.dot_general(per_chunk_columns(qd), jnp.concatenate(entering, axis=1),
                                  (((1,), (1,)), ((), ())), preferred_element_type=F32)
        o = o_intra + o_inter
        o = o * lax.rsqrt(jnp.mean(jnp.square(o), axis=-1, keepdims=True) + RMS_EPS)
        outs.append(o)
    o = jnp.concatenate(outs, axis=1) * gain_ref[...]
    gg = g_ref[...].astype(F32)
    o_ref[...] = (o * (gg * jax.nn.sigmoid(gg))).astype(o_ref.dtype)


def _hgrn(u_hg, lb_logits, gain, B, S):
    tiles = S // HG_TILE
    col = lambda j: pl.BlockSpec((HG_TILE, HG_WIDTH), functools.partial(lambda b, t, j: (b * tiles + t, j), j=j))
    return pl.pallas_call(
        _hgrn_kernel,
        grid=(B, tiles),
        in_specs=[col(0), col(1), col(2), col(3),
                  pl.BlockSpec((2, HG_WIDTH), lambda b, t: (0, 0)),
                  pl.BlockSpec((1, HG_WIDTH), lambda b, t: (0, 0))],
        out_specs=pl.BlockSpec((HG_TILE, HG_WIDTH), lambda b, t: (b * tiles + t, 0)),
        out_shape=jax.ShapeDtypeStruct((B * S, HG_WIDTH), BF16),
        scratch_shapes=[pltpu.VMEM((HG_HEADS, HG_DIM, HG_DIM), F32)],
        compiler_params=_cparams("parallel", "arbitrary"),
        name="hgrn2",
    )(u_hg, u_hg, u_hg, u_hg, lb_logits, gain)


def _load_row_tiles(ref, n, start=0):
    return jnp.concatenate([ref[pl.ds(start + c, n, stride=LANE_CHUNKS), :] for c in range(LANE_CHUNKS)], axis=1)


def _store_row_tiles(ref, val, n):
    for c in range(LANE_CHUNKS):
        ref[pl.ds(c, n, stride=LANE_CHUNKS), :] = val[:, c * LANES:(c + 1) * LANES]


def _layer_norm(z, w, b):
    mu = jnp.mean(z, axis=-1, keepdims=True)
    zc = z - mu
    var = jnp.mean(jnp.square(zc), axis=-1, keepdims=True)
    return zc * lax.rsqrt(var + LN_EPS) * w + b


def _merge_kernel(ya_ref, yh_ref, ga_ref, gh_ref, x_ref, wa_ref, wh_ref, wo_ref, lw_ref, lb_ref, o_ref):
    ma = jnp.dot(ya_ref[...], wa_ref[...], preferred_element_type=F32)
    mh = jnp.dot(yh_ref[...], wh_ref[...], preferred_element_type=F32)
    merged = (jax.nn.sigmoid(ga_ref[...].astype(F32)) * ma + jax.nn.sigmoid(gh_ref[...].astype(F32)) * mh)
    z = DEEPNORM_ALPHA * x_ref[...] + jnp.dot(merged.astype(BF16), wo_ref[...], preferred_element_type=F32)
    _store_row_tiles(o_ref, _layer_norm(z, lw_ref[...], lb_ref[...]), MIX_ROWS)


def _merge(y_att, y_hg, u_hg, x2d, w_a, w_h, w_o, ln_w, ln_b):
    T = x2d.shape[0]
    rows = lambda width, j=0: pl.BlockSpec((MIX_ROWS, width), functools.partial(lambda i, j: (i, j), j=j))
    full = lambda a: pl.BlockSpec(a.shape, lambda i: (0, 0))
    return pl.pallas_call(
        _merge_kernel,
        grid=(T // MIX_ROWS,),
        in_specs=[rows(ATT_WIDTH), rows(HG_WIDTH), rows(D_MODEL, 4), rows(D_MODEL, 5), rows(D_MODEL),
                  full(w_a), full(w_h), full(w_o), full(ln_w), full(ln_b)],
        out_specs=pl.BlockSpec((MIX_ROWS * LANE_CHUNKS, LANES), lambda i: (i, 0)),
        out_shape=jax.ShapeDtypeStruct((T * LANE_CHUNKS, LANES), F32),
        compiler_params=_cparams("parallel"),
        name="merge_ln1",
    )(y_att, y_hg, u_hg, u_hg, x2d, w_a, w_h, w_o, ln_w, ln_b)


def _first_argmax(v, ids, n):
    mx = jnp.max(v, axis=0, keepdims=True)
    return mx, jnp.min(jnp.where(v == mx, ids, n), axis=0, keepdims=True)


def _route_kernel(x1_ref, p_ref, wrt_ref, rb_ref, wsg_ref, wsu_ref, wsd_ref, wpg_ref, wpp_ref,
                  base_ref, idx_ref, gate_ref, rank_ref, cnt_ref, carry_ref):
    @pl.when(pl.program_id(0) % (MOE_TILE // MIX_ROWS) == 0)
    def _():
        carry_ref[...] = jnp.zeros_like(carry_ref)

    x1 = _load_row_tiles(x1_ref, MIX_ROWS)
    x1b = x1.astype(BF16)
    logits = lax.dot_general(wrt_ref[...], x1, (((1,), (1,)), ((), ())), preferred_element_type=F32,
                             precision=lax.Precision.HIGHEST)
    s = jax.nn.sigmoid(logits)
    sel = s + rb_ref[...]
    eid = lax.broadcasted_iota(jnp.int32, (N_EXPERTS, MIX_ROWS), 0)
    neg = -jnp.inf

    grp = sel.reshape(N_GROUPS, GROUP_SIZE, MIX_ROWS)
    mid = lax.broadcasted_iota(jnp.int32, grp.shape, 1)
    m1 = jnp.max(grp, axis=1, keepdims=True)
    i1 = jnp.min(jnp.where(grp == m1, mid, GROUP_SIZE), axis=1, keepdims=True)
    m2 = jnp.max(jnp.where(mid == i1, neg, grp), axis=1, keepdims=True)
    gscore = (m1 + m2).reshape(N_GROUPS, MIX_ROWS)
    gid = lax.broadcasted_iota(jnp.int32, (N_GROUPS, MIX_ROWS), 0)
    gsel = jnp.zeros((N_GROUPS, MIX_ROWS), jnp.bool_)
    for _ in range(TOPK_GROUPS):
        _, gi = _first_argmax(gscore, gid, N_GROUPS)
        hit = gid == gi
        gsel = gsel | hit
        gscore = jnp.where(hit, neg, gscore)
    emask = jnp.broadcast_to(gsel.reshape(N_GROUPS, 1, MIX_ROWS), grp.shape).reshape(N_EXPERTS, MIX_ROWS)
    cand = jnp.where(emask, sel, neg)

    idxs, gates = [], []
    chosen = jnp.zeros((N_EXPERTS, MIX_ROWS), jnp.bool_)
    for _ in range(TOP_K):
        _, ei = _first_argmax(cand, eid, N_EXPERTS)
        hit = eid == ei
        idxs.append(ei)
        gates.append(jnp.sum(jnp.where(hit, s, 0.0), axis=0, keepdims=True))
        chosen = chosen | hit
        cand = jnp.where(hit, neg, cand)
    g = jnp.concatenate(gates, axis=0)
    g = g / jnp.sum(g, axis=0, keepdims=True) * ROUTED_SCALE
    idx_ref[...] = jnp.concatenate(idxs, axis=0)
    gate_ref[...] = g

    onehot = jnp.where(chosen, 1.0, 0.0)
    tr = lax.broadcasted_iota(jnp.int32, (MIX_ROWS, MIX_ROWS), 0)
    tc = lax.broadcasted_iota(jnp.int32, (MIX_ROWS, MIX_ROWS), 1)
    before = jnp.where(tr < tc, 1.0, 0.0).astype(BF16)
    prefix = jnp.dot(onehot.astype(BF16), before, preferred_element_type=F32)
    rankfull = (carry_ref[:, 0:1] + prefix).astype(jnp.int32)
    rank_ref[...] = jnp.concatenate(
        [jnp.sum(jnp.where(eid == ei, rankfull, 0), axis=0, keepdims=True) for ei in idxs], axis=0)
    total = carry_ref[...] + jnp.sum(onehot, axis=1, keepdims=True)
    carry_ref[...] = total
    cnt_ref[...] = total.astype(jnp.int32)

    hg = jnp.dot(x1b, wsg_ref[...], preferred_element_type=F32)
    hu = jnp.dot(x1b, wsu_ref[...], preferred_element_type=F32)
    shared = jnp.dot((hg * jax.nn.sigmoid(hg) * hu).astype(BF16), wsd_ref[...], preferred_element_type=F32)
    ple = (jax.nn.sigmoid(jnp.dot(x1b, wpg_ref[...], preferred_element_type=F32))
           * jnp.dot(p_ref[...].astype(BF16), wpp_ref[...], preferred_element_type=F32))
    _store_row_tiles(base_ref, DEEPNORM_ALPHA * x1 + shared + ple, MIX_ROWS)


def _route(x1, p2d, wr_t, rbias, wsg, wsu, wsd, wpg, wpp):
    T = x1.shape[0] // LANE_CHUNKS
    per_tile = MOE_TILE // MIX_ROWS
    full = lambda a: pl.BlockSpec(a.shape, lambda i: (0, 0))
    tok = pl.BlockSpec((TOP_K, MIX_ROWS), lambda i: (0, i))
    row_tiles = pl.BlockSpec((MIX_ROWS * LANE_CHUNKS, LANES), lambda i: (i, 0))
    return pl.pallas_call(
        _route_kernel,
        grid=(T // MIX_ROWS,),
        in_specs=[row_tiles,
                  pl.BlockSpec((MIX_ROWS, PLE_DIM), lambda i: (i, 0)),
                  full(wr_t), full(rbias), full(wsg), full(wsu), full(wsd), full(wpg), full(wpp)],
        out_specs=[row_tiles, tok, tok, tok,
                   pl.BlockSpec((None, N_EXPERTS, LANES), lambda i: (i // per_tile, 0, 0))],
        out_shape=[jax.ShapeDtypeStruct((T * LANE_CHUNKS, LANES), F32),
                   jax.ShapeDtypeStruct((TOP_K, T), jnp.int32),
                   jax.ShapeDtypeStruct((TOP_K, T), F32),
                   jax.ShapeDtypeStruct((TOP_K, T), jnp.int32),
                   jax.ShapeDtypeStruct((T // MOE_TILE, N_EXPERTS, LANES), jnp.int32)],
        scratch_shapes=[pltpu.VMEM((N_EXPERTS, LANES), F32)],
        compiler_params=_cparams("arbitrary"),
        name="route_shared_ple",
    )(x1, p2d, wr_t, rbias, wsg, wsu, wsd, wpg, wpp)


def _plan_sc_kernel(off_hbm, idx_hbm, rank_hbm, gate_hbm, rows_hbm, gates_hbm,
                    off_v, idx_v, rank_v, gate_v, rows_v, gates_v):
    n_tokens = idx_hbm.shape[0] // TOP_K
    worker = lax.axis_index("subcore") * SC_CORES + lax.axis_index("core")

    @pl.when(worker < n_tokens // MOE_TILE)
    def _():
        pltpu.sync_copy(off_hbm.at[pl.ds(worker * N_EXPERTS, N_EXPERTS)], off_v)
        lane = lax.iota(jnp.int32, SC_LANES)
        for k in range(TOP_K):
            row = pl.ds(k * n_tokens + worker * MOE_TILE, MOE_TILE)
            pltpu.sync_copy(idx_hbm.at[row], idx_v)
            pltpu.sync_copy(rank_hbm.at[row], rank_v)
            pltpu.sync_copy(gate_hbm.at[row], gate_v)

            @pl.loop(0, MOE_TILE // SC_LANES)
            def _(i):
                at = i * SC_LANES
                pos = plsc.load_gather(off_v, [idx_v[pl.ds(at, SC_LANES)]]) + rank_v[pl.ds(at, SC_LANES)]
                plsc.store_scatter(rows_v, [pos], (lane + at) * LANE_CHUNKS)
                plsc.store_scatter(gates_v, [pos], gate_v[pl.ds(at, SC_LANES)])

        @pl.loop(0, LIST_PAD // SC_LANES)
        def _(i):
            tail = pl.ds(MOE_TILE * TOP_K + i * SC_LANES, SC_LANES)
            rows_v[tail] = jnp.zeros((SC_LANES,), jnp.int32)
            gates_v[tail] = jnp.zeros((SC_LANES,), F32)

        out = pl.ds(worker * LIST_LEN, LIST_LEN)
        pltpu.sync_copy(rows_v, rows_hbm.at[out])
        pltpu.sync_copy(gates_v, gates_hbm.at[out])


def _plan_sc(off, idx, rank, gate):
    n_tiles = idx.shape[1] // MOE_TILE
    assert n_tiles <= SC_CORES * SC_SUBCORES
    mesh = plsc.VectorSubcoreMesh(core_axis_name="core", subcore_axis_name="subcore",
                                  num_cores=SC_CORES, num_subcores=SC_SUBCORES)
    return pl.kernel(
        _plan_sc_kernel,
        out_type=(jax.ShapeDtypeStruct((n_tiles * LIST_LEN,), jnp.int32),
                  jax.ShapeDtypeStruct((n_tiles * LIST_LEN,), F32)),
        mesh=mesh,
        scratch_types=[pltpu.VMEM((N_EXPERTS,), jnp.int32), pltpu.VMEM((MOE_TILE,), jnp.int32),
                       pltpu.VMEM((MOE_TILE,), jnp.int32), pltpu.VMEM((MOE_TILE,), F32),
                       pltpu.VMEM((LIST_LEN,), jnp.int32), pltpu.VMEM((LIST_LEN,), F32)],
        compiler_params=pltpu.CompilerParams(needs_layout_passes=False),
        name="moe_plan_sc",
    )(off, idx.reshape(-1), rank.reshape(-1), gate.reshape(-1))


def _moe_tile_kernel(cnt_ref, off_ref, wg_ref, wu_ref, wd_ref, lw_ref, lb_ref, rows_hbm, gates_hbm, x_hbm, base_hbm,
                     o_hbm, x_s, acc_s, xg_a, xg_b, xg_c, y_a, y_b, y_c, stage_s, rows_s, gates_s, sem):
    tile, e = pl.program_id(0), pl.program_id(1)
    rows_of = lambda ref, r, n: ref.at[pl.ds(pl.multiple_of(r * LANE_CHUNKS, LANE_CHUNKS), n * LANE_CHUNKS), :]
    tile_rows = pl.ds(pl.multiple_of(tile * (MOE_TILE * LANE_CHUNKS), LANE_CHUNKS), MOE_TILE * LANE_CHUNKS)

    pair = tile * N_EXPERTS + e
    last_pair = pl.num_programs(0) * N_EXPERTS - 1
    n, off = cnt_ref[pair], off_ref[pair]
    tile_at = lambda ref, r: ref.at[pl.ds(pl.multiple_of(r, LANE_CHUNKS), LANE_CHUNKS), :]

    def gather_group(xg, first, jb):
        at = first + jb * GATHER_GROUP
        rows = [tile_at(x_s, rows_s[at + u])[...] for u in range(GATHER_GROUP)]
        rows_of(xg, jb * GATHER_GROUP, GATHER_GROUP)[...] = jnp.concatenate(rows, axis=0)

    def gather_loop(xg, first):
        def body(jb, cc):
            gather_group(xg, first, jb)
            return cc

        lax.fori_loop(0, MOE_CHUNK // GATHER_GROUP, body, 0)

    def swiglu(xg, y):
        xb = _load_row_tiles(xg, MOE_CHUNK).astype(BF16)
        hg = jnp.dot(xb, wg_ref[...], preferred_element_type=F32)
        hu = jnp.dot(xb, wu_ref[...], preferred_element_type=F32)
        act = (hg * jax.nn.sigmoid(hg) * hu).astype(BF16)
        _store_row_tiles(y, jnp.dot(act, wd_ref[...], preferred_element_type=F32), MOE_CHUNK)

    def scatter_group(y, first, j0, live):
        dsts = [rows_s[first + j0 + u] for u in range(live)]
        gates = [gates_s[first + j0 + u] for u in range(live)]
        yv = rows_of(y, j0, live)[...]
        vals = [tile_at(acc_s, d)[...] + g * yv[u * LANE_CHUNKS:(u + 1) * LANE_CHUNKS]
                for u, (d, g) in enumerate(zip(dsts, gates))]
        for d, val in reversed(list(zip(dsts, vals))):
            tile_at(acc_s, d)[...] = val

    def scatter_loop(y, first, m):
        def body(jg, cc):
            scatter_group(y, first, jg * SCATTER_GROUP, SCATTER_GROUP)
            return cc

        lax.fori_loop(0, m // SCATTER_GROUP, body, 0)
        for live in range(1, SCATTER_GROUP):
            @pl.when(m % SCATTER_GROUP == live)
            def _(live=live):
                scatter_group(y, first, m - live, live)

    @pl.when(e == 0)
    def _():
        plan = pl.ds(pl.multiple_of(tile * LIST_LEN, LIST_PAD), LIST_LEN)
        loads = [pltpu.make_async_copy(rows_hbm.at[plan], rows_s, sem.at[0]),
                 pltpu.make_async_copy(gates_hbm.at[plan], gates_s, sem.at[1]),
                 pltpu.make_async_copy(x_hbm.at[tile_rows, :], x_s, sem.at[2]),
                 pltpu.make_async_copy(base_hbm.at[tile_rows, :], acc_s, sem.at[3])]
        for load in loads:
            load.start()
        y_b[...] = jnp.zeros_like(y_b)
        loads[0].wait()
        loads[2].wait()
        gather_loop(xg_a, off)
        loads[1].wait()
        loads[3].wait()

    prev_off = off_ref[jnp.maximum(pair - 1, 0)]
    next_off = off_ref[jnp.minimum(pair + 1, last_pair)]
    live_row = lax.broadcasted_iota(jnp.int32, (MOE_CHUNK, 2 * LANES), 0) < n

    def run_expert(xg_cur, y_cur, xg_nxt, y_prv):
        gathers = [functools.partial(gather_group, xg_nxt, next_off, jb) for jb in range(MOE_CHUNK // GATHER_GROUP)]
        scatters = [functools.partial(scatter_group, y_prv, prev_off, jg * SCATTER_GROUP, SCATTER_GROUP)
                    for jg in range(MOE_CHUNK // SCATTER_GROUP)]
        side = [s for both in zip(gathers, scatters) for s in both]
        pieces = 2 + LANE_CHUNKS // 2
        per_piece = -(-len(side) // pieces)

        def side_work(i):
            for s in side[i * per_piece:(i + 1) * per_piece]:
                s()

        xb = _load_row_tiles(xg_cur, MOE_CHUNK).astype(BF16)
        hg = jnp.dot(xb, wg_ref[...], preferred_element_type=F32)
        side_work(0)
        hu = jnp.dot(xb, wu_ref[...], preferred_element_type=F32)
        side_work(1)
        act = (hg * jax.nn.sigmoid(hg) * hu).astype(BF16)
        for q in range(LANE_CHUNKS // 2):
            out = jnp.dot(act, wd_ref[:, q * 2 * LANES:(q + 1) * 2 * LANES], preferred_element_type=F32)
            out = jnp.where(live_row, out, 0.0)
            for c in range(2):
                y_cur[pl.ds(2 * q + c, MOE_CHUNK, stride=LANE_CHUNKS), :] = out[:, c * LANES:(c + 1) * LANES]
            side_work(2 + q)

    @pl.when(e % 2 == 0)
    def _():
        run_expert(xg_a, y_a, xg_b, y_b)

    @pl.when(e % 2 == 1)
    def _():
        run_expert(xg_b, y_b, xg_a, y_a)

    def extra_chunk(c, carry):
        first = off + c * MOE_CHUNK
        gather_loop(xg_c, first)
        swiglu(xg_c, y_c)
        scatter_loop(y_c, first, jnp.minimum(MOE_CHUNK, n - c * MOE_CHUNK))
        return carry

    lax.fori_loop(1, (n + MOE_CHUNK - 1) // MOE_CHUNK, extra_chunk, 0)

    @pl.when(e == N_EXPERTS - 1)
    def _():
        scatter_loop(y_b, off, jnp.minimum(MOE_CHUNK, n))
        n_pieces = MOE_TILE // LN_ROWS
        store = lambda c: pltpu.make_async_copy(
            stage_s.at[c % 2], o_hbm.at[pl.ds(tile * MOE_TILE + c * LN_ROWS, LN_ROWS), :], sem.at[4 + c % 2])
        for c in range(n_pieces):
            z = _load_row_tiles(acc_s, LN_ROWS, c * LN_ROWS * LANE_CHUNKS)
            if c >= 2:
                store(c - 2).wait()
            stage_s[c % 2] = _layer_norm(z, lw_ref[...], lb_ref[...])
            store(c).start()
        store(n_pieces - 2).wait()
        store(n_pieces - 1).wait()


def _moe_tiles(x1, base, plan_rows, plan_gates, cnt, off, wg, wu, wd, ln_w, ln_b):
    T = x1.shape[0] // LANE_CHUNKS
    w_spec = lambda shape: pl.BlockSpec((None,) + shape, lambda i, e, cnt, off: (e, 0, 0))
    vec = pl.BlockSpec((1, D_MODEL), lambda i, e, cnt, off: (0, 0))
    hbm = pl.BlockSpec(memory_space=pl.ANY)
    tile_rows = MOE_TILE * LANE_CHUNKS
    return pl.pallas_call(
        _moe_tile_kernel,
        grid_spec=pltpu.PrefetchScalarGridSpec(
            num_scalar_prefetch=2,
            grid=(T // MOE_TILE, N_EXPERTS),
            in_specs=[w_spec((D_MODEL, EXPERT_FF)), w_spec((D_MODEL, EXPERT_FF)), w_spec((EXPERT_FF, D_MODEL)),
                      vec, vec, hbm, hbm, hbm, hbm],
            out_specs=hbm,
            scratch_shapes=[pltpu.VMEM((tile_rows, LANES), F32),
                            pltpu.VMEM((tile_rows, LANES), F32)]
                           + [pltpu.VMEM((MOE_CHUNK * LANE_CHUNKS, LANES), F32)] * 6
                           + [pltpu.VMEM((2, LN_ROWS, D_MODEL), F32),
                              pltpu.SMEM((LIST_LEN,), jnp.int32),
                              pltpu.SMEM((LIST_LEN,), F32),
                              pltpu.SemaphoreType.DMA((6,))],
        ),
        out_shape=jax.ShapeDtypeStruct((T, D_MODEL), F32),
        compiler_params=_cparams("arbitrary", "arbitrary"),
        name="moe_tiles_ln2",
    )(cnt, off, wg, wu, wd, ln_w, ln_b, plan_rows, plan_gates, x1, base)


def kernel(x, p, w_in, hgrn_lb_logits, hgrn_norm_w, w_branch_att, w_branch_hgrn, w_out, ln1_w, ln1_b, router_w, router_bias, expert_w_gate, expert_w_up, expert_w_down, shared_w_gate, shared_w_up, shared_w_down, ple_gate_w, ple_proj_w, ln2_w, ln2_b):
    B, S, D = x.shape
    T = B * S
    l = 0
    x2d = x.reshape(T, D)
    bf = lambda a: a.astype(BF16)

    ws = _att_weights(w_in[l])
    qkv = [_proj_att(x2d, ws[g], d) for g, d in enumerate(ATT_DILATIONS)]
    y_att = _attention(qkv, B, S)
    u_hg = _proj(x2d, bf(w_in[l][:, 3 * len(ATT_DILATIONS) * ATT_WIDTH:]), 1536)
    y_hg = _hgrn(u_hg, hgrn_lb_logits, hgrn_norm_w[l:l + 1], B, S)
    x1 = _merge(y_att, y_hg, u_hg, x2d, bf(w_branch_att[l]), bf(w_branch_hgrn[l]), bf(w_out[l]),
                ln1_w[l:l + 1], ln1_b[l:l + 1])

    base, idx, gate, rank, counts = _route(
        x1, p[l].reshape(T, PLE_DIM), router_w[l].T, router_bias[l].reshape(N_EXPERTS, 1),
        bf(shared_w_gate[l]), bf(shared_w_up[l]), bf(shared_w_down[l]), bf(ple_gate_w[l]), bf(ple_proj_w[l]))
    cnt = counts[:, :, 0]
    off = jnp.cumsum(cnt, axis=1) - cnt
    cnt, off = cnt.reshape(-1), off.reshape(-1)
    plan_rows, plan_gates = _plan_sc(off, idx, rank, gate)
    out = _moe_tiles(x1, base, plan_rows, plan_gates, cnt, off, bf(expert_w_gate[l]), bf(expert_w_up[l]),
                     bf(expert_w_down[l]), ln2_w[l:l + 1], ln2_b[l:l + 1])
    return out.reshape(B, S, D)
```

```python
import functools

import jax
import jax.numpy as jnp
import numpy as np
from jax import lax
from jax.experimental import pallas as pl
from jax.experimental.pallas import tpu as pltpu
from jax.experimental.pallas import tpu_sc as plsc

F32 = jnp.float32
BF16 = jnp.bfloat16

D_MODEL = 1024
ATT_HEAD_DIM = 64
ATT_HEADS = 8
ATT_DILATIONS = (1, 4, 16)
ATT_BLOCK = 128
ATT_WIDTH = ATT_HEADS * ATT_HEAD_DIM
ATT_TILE = ATT_BLOCK * max(ATT_DILATIONS)
NEG_INF = -1e30

HG_HEADS = 8
HG_DIM = 128
HG_WIDTH = HG_HEADS * HG_DIM
HG_CHUNK = 32
HG_TILE = 256
RMS_EPS = 1e-6

N_EXPERTS = 64
TOP_K = 8
TOP_K_BITS = 3
N_GROUPS = 8
GROUP_SIZE = N_EXPERTS // N_GROUPS
TOPK_GROUPS = 4
EXPERT_FF = 256
ROUTED_SCALE = 2.5
PLE_DIM = 256
LN_EPS = 1e-5
DEPTH = 1
DEEPNORM_ALPHA = (2.0 * DEPTH) ** 0.25

LANES = 128
LANE_CHUNKS = D_MODEL // LANES
PROJ_ROWS = 512
ATT_PROJ_ROWS = 1024
MIX_ROWS = 512
MOE_TILE = 4096
MOE_CHUNK = 576
LN_ROWS = 256
LIST_PAD = 1024
LIST_LEN = MOE_TILE * TOP_K + LIST_PAD
GATHER_GROUP = 8
SCATTER_GROUP = 8
V7X_VMEM_LIMIT = 56 * 1024 * 1024
SC_CORES, SC_SUBCORES, SC_LANES = 2, 16, 16


def _cparams(*sem):
    return pltpu.CompilerParams(dimension_semantics=sem, vmem_limit_bytes=V7X_VMEM_LIMIT)


def _proj_att_kernel(*refs, dil):
    x_refs, w_ref, o_ref = refs[:LANE_CHUNKS], refs[LANE_CHUNKS], refs[LANE_CHUNKS + 1]
    n = ATT_PROJ_ROWS // dil

    def rows(ref):
        if dil == 1:
            return ref[...]
        return jnp.concatenate([ref[pl.ds(r, n, stride=dil), :] for r in range(dil)], axis=0)

    xp = jnp.concatenate([rows(ref).astype(BF16) for ref in x_refs], axis=1)
    y = jnp.dot(xp, w_ref[...], preferred_element_type=F32)
    o_ref[...] = y.astype(BF16).reshape(dil, n, 3 * ATT_WIDTH)


def _proj_att(x2d, w, dil):
    T = x2d.shape[0]
    per = ATT_TILE // ATT_PROJ_ROWS
    n = ATT_PROJ_ROWS // dil
    out = pl.pallas_call(
        functools.partial(_proj_att_kernel, dil=dil),
        grid=(T // ATT_PROJ_ROWS,),
        in_specs=[pl.BlockSpec((ATT_PROJ_ROWS, LANES), functools.partial(lambda i, c: (i, c), c=c))
                  for c in range(LANE_CHUNKS)]
                 + [pl.BlockSpec((D_MODEL, 3 * ATT_WIDTH), lambda i: (0, 0))],
        out_specs=pl.BlockSpec((None, dil, None, n, 3 * ATT_WIDTH), lambda i: (i // per, 0, i % per, 0, 0)),
        out_shape=jax.ShapeDtypeStruct((T // ATT_TILE, dil, per, n, 3 * ATT_WIDTH), BF16),
        compiler_params=_cparams("parallel"),
        name=f"proj_att_d{dil}",
    )(*([x2d] * LANE_CHUNKS), w)
    return out.reshape(T // ATT_TILE, dil, ATT_TILE // dil, 3 * ATT_WIDTH)


def _att_pair(q2, kp, kc, vp, vc, bias_ref, g, first):
    def head0_lanes(rows, dtype):
        lane = lax.broadcasted_iota(jnp.int32, (rows, 2 * ATT_HEAD_DIM), 1)
        return lane.astype(F32).astype(dtype) < ATT_HEAD_DIM

    lo_q = head0_lanes(ATT_BLOCK, BF16)
    lo_v = head0_lanes(2 * ATT_BLOCK, BF16)
    k2 = jnp.concatenate([kp, kc], axis=0)
    v2 = jnp.concatenate([vp, vc], axis=0)
    zero = jnp.zeros_like(q2)
    ps, ms = [], []
    for hh in range(2):
        qm = jnp.where(lo_q, q2, zero) if hh == 0 else jnp.where(lo_q, zero, q2)
        s = lax.dot_general(qm, k2, (((1,), (1,)), ((), ())), preferred_element_type=F32)
        s = s + bias_ref[g, hh, first]
        m = jnp.max(s, axis=-1, keepdims=True)
        ps.append(jnp.exp(s - m).astype(BF16))
        ms.append(m)
    pcat = jnp.concatenate(ps, axis=1)
    zero_v, one_v = jnp.zeros_like(v2), jnp.ones_like(v2)
    rhs = jnp.concatenate([
        jnp.concatenate([jnp.where(lo_v, v2, zero_v), jnp.where(lo_v, one_v, zero_v)], axis=1),
        jnp.concatenate([jnp.where(lo_v, zero_v, v2), jnp.where(lo_v, zero_v, one_v)], axis=1)], axis=0)
    nd = jnp.dot(pcat, rhs, preferred_element_type=F32)
    m2 = jnp.where(head0_lanes(ATT_BLOCK, F32), ms[0], ms[1])
    return nd[:, :2 * ATT_HEAD_DIM], m2, nd[:, 2 * ATT_HEAD_DIM:]


def _att_kernel(*refs):
    (q0, kc0, vc0, kp0, vp0, q1, kc1, vc1, kp1, vp1, q2, kc2, vc2, kp2, vp2,
     bias_ref, o_ref) = refs[:17]
    ng = len(ATT_DILATIONS)
    num_s, m_s, den_s = refs[17:17 + ng], refs[17 + ng:17 + 2 * ng], refs[17 + 2 * ng:]
    first_tile = (pl.program_id(2) == 0).astype(jnp.int32)
    groups = ((q0, kc0, vc0, kp0, vp0), (q1, kc1, vc1, kp1, vp1), (q2, kc2, vc2, kp2, vp2))
    for g, dil in enumerate(ATT_DILATIONS):
        q_ref, kc_ref, vc_ref, kp_ref, vp_ref = groups[g]
        nb = ATT_TILE // dil // ATT_BLOCK
        for r in range(dil):
            for n in range(nb):
                rows = pl.ds(n * ATT_BLOCK, ATT_BLOCK)
                if n == 0:
                    prev = pl.ds((nb - 1) * ATT_BLOCK, ATT_BLOCK)
                    kp, vp, first = kp_ref[r, prev, :], vp_ref[r, prev, :], first_tile
                else:
                    prev = pl.ds((n - 1) * ATT_BLOCK, ATT_BLOCK)
                    kp, vp, first = kc_ref[r, prev, :], vc_ref[r, prev, :], 0
                num, m, den = _att_pair(q_ref[r, rows, :], kp, kc_ref[r, rows, :], vp, vc_ref[r, rows, :],
                                        bias_ref, g, first)
                if dil == 1:
                    dst = rows
                else:
                    dst = pl.ds(n * ATT_BLOCK * dil + r, ATT_BLOCK, stride=dil)
                num_s[g][dst, :] = num
                m_s[g][dst, :] = m
                den_s[g][dst, :] = den
    m_all = jnp.maximum(jnp.maximum(m_s[0][...], m_s[1][...]), m_s[2][...])
    num = jnp.zeros((ATT_TILE, 2 * ATT_HEAD_DIM), F32)
    den = jnp.zeros((ATT_TILE, 2 * ATT_HEAD_DIM), F32)
    for g in range(ng):
        sc = jnp.exp(m_s[g][...] - m_all)
        num = num + sc * num_s[g][...]
        den = den + sc * den_s[g][...]
    o_ref[...] = (num / den).astype(o_ref.dtype)


def _att_bias_table():
    qi = np.arange(ATT_BLOCK)[:, None]
    ki = np.arange(2 * ATT_BLOCK)[None, :]
    steps = qi + ATT_BLOCK - ki
    valid = (steps >= 0) & (steps <= ATT_BLOCK)
    slopes = np.array([2.0 ** (-8.0 * (h + 1) / ATT_HEADS) for h in range(ATT_HEADS)], np.float32)
    tab = np.empty((len(ATT_DILATIONS), ATT_HEADS, 2, ATT_BLOCK, 2 * ATT_BLOCK), np.float32)
    for g, dil in enumerate(ATT_DILATIONS):
        bias = -slopes[:, None, None] * (steps * dil).astype(np.float32)[None]
        tab[g, :, 0] = np.where(valid[None], bias, NEG_INF)
        tab[g, :, 1] = np.where((valid & (ki >= ATT_BLOCK))[None], bias, NEG_INF)
    return jnp.asarray(tab)


def _attention(qkv, B, S):
    tiles = S // ATT_TILE
    pair = 2 * ATT_HEAD_DIM
    npair = ATT_WIDTH // pair
    in_specs, args = [], []
    for g, dil in enumerate(ATT_DILATIONS):
        blk = (None, dil, ATT_TILE // dil, pair)
        cur = lambda b, hp, t, off: (b * tiles + t, 0, 0, off * npair + hp)
        prv = lambda b, hp, t, off: (b * tiles + jnp.maximum(t - 1, 0), 0, 0, off * npair + hp)
        in_specs += [pl.BlockSpec(blk, functools.partial(cur, off=0)),
                     pl.BlockSpec(blk, functools.partial(cur, off=1)),
                     pl.BlockSpec(blk, functools.partial(cur, off=2)),
                     pl.BlockSpec(blk, functools.partial(prv, off=1)),
                     pl.BlockSpec(blk, functools.partial(prv, off=2))]
        args += [qkv[g]] * 5
    in_specs.append(pl.BlockSpec((len(ATT_DILATIONS), 2, 2, ATT_BLOCK, 2 * ATT_BLOCK),
                                 lambda b, hp, t: (0, hp, 0, 0, 0)))
    args.append(_att_bias_table())
    scratch = [pltpu.VMEM((ATT_TILE, pair), F32) for _ in range(3 * len(ATT_DILATIONS))]
    return pl.pallas_call(
        _att_kernel,
        grid=(B, npair, tiles),
        in_specs=in_specs,
        out_specs=pl.BlockSpec((ATT_TILE, pair), lambda b, hp, t: (b * tiles + t, hp)),
        out_shape=jax.ShapeDtypeStruct((B * S, ATT_WIDTH), BF16),
        scratch_shapes=scratch,
        compiler_params=_cparams("parallel", "parallel", "arbitrary"),
        name="dilated_attention",
    )(*args)


def _att_weights(w_in_l):
    out = []
    width = len(ATT_DILATIONS) * ATT_WIDTH
    for g in range(len(ATT_DILATIONS)):
        cols = [w_in_l[:, part * width + g * ATT_WIDTH: part * width + (g + 1) * ATT_WIDTH] for part in range(3)]
        cols[0] = cols[0] * (ATT_HEAD_DIM ** -0.5)
        out.append(jnp.concatenate(cols, axis=1).astype(BF16))
    return out


def _proj_kernel(x_ref, w_ref, o_ref, *, col_tile):
    xb = x_ref[...].astype(BF16)
    for c in range(w_ref.shape[1] // col_tile):
        cols = slice(c * col_tile, (c + 1) * col_tile)
        o_ref[:, cols] = jnp.dot(xb, w_ref[:, cols], preferred_element_type=F32).astype(o_ref.dtype)


def _proj(x2d, w, col_tile):
    T, N = x2d.shape[0], w.shape[1]
    return pl.pallas_call(
        functools.partial(_proj_kernel, col_tile=col_tile),
        grid=(T // PROJ_ROWS,),
        in_specs=[pl.BlockSpec((PROJ_ROWS, D_MODEL), lambda i: (i, 0)),
                  pl.BlockSpec((D_MODEL, N), lambda i: (0, 0))],
        out_specs=pl.BlockSpec((PROJ_ROWS, N), lambda i: (i, 0)),
        out_shape=jax.ShapeDtypeStruct((T, N), BF16),
        compiler_params=_cparams("parallel"),
        name="proj_hgrn_gates",
    )(x2d, w)


def _split3(v):
    a = v.astype(BF16)
    r = v - a.astype(F32)
    b = r.astype(BF16)
    c = (r - b.astype(F32)).astype(BF16)
    return a, b, c


def _hgrn_kernel(q_ref, f_ref, i_ref, g_ref, lbl_ref, gain_ref, o_ref, state_ref):
    @pl.when(pl.program_id(1) == 0)
    def _():
        state_ref[...] = jnp.zeros_like(state_ref)

    lbl = lbl_ref[...]
    e = jnp.exp(lbl - jnp.max(lbl, axis=0, keepdims=True))
    lb = e[0:1] / jnp.sum(e, axis=0, keepdims=True)
    forget = lb + (1.0 - lb) * jax.nn.sigmoid(f_ref[...].astype(F32))
    log_f = jnp.log(forget)
    key = 1.0 - forget

    row = lax.broadcasted_iota(jnp.int32, (HG_TILE, HG_TILE), 0)
    col = lax.broadcasted_iota(jnp.int32, (HG_TILE, HG_TILE), 1)
    causal = (row >= col) & ((row // HG_CHUNK) == (col // HG_CHUNK))
    tri = jnp.where(causal, 1.0, 0.0).astype(BF16)
    b = sum(jnp.dot(tri, t, preferred_element_type=F32) for t in _split3(log_f))
    eb = jnp.exp(b)
    q_dec = (q_ref[...].astype(F32) * eb).astype(BF16)
    k_inv = key * jnp.exp(-b)
    xi = i_ref[...].astype(F32)
    val = (xi * jax.nn.sigmoid(xi)).astype(BF16)
    k_inv_b = k_inv.astype(BF16)

    n_chunks = HG_TILE // HG_CHUNK
    last_rows = [eb[(c + 1) * HG_CHUNK - 1:(c + 1) * HG_CHUNK, :] for c in range(n_chunks)]
    dec_rows = jnp.concatenate([jnp.broadcast_to(r, (HG_CHUNK, HG_WIDTH)) for r in last_rows], axis=0)
    k_end = (k_inv * dec_rows).astype(BF16)
    chunk_of_row = (lax.broadcasted_iota(jnp.int32, (HG_TILE, HG_DIM), 0) // HG_CHUNK).astype(F32).astype(BF16)
    in_chunk = [chunk_of_row == c for c in range(n_chunks)]
    zero = jnp.zeros((HG_TILE, HG_DIM), BF16)

    def per_chunk_columns(t):
        return jnp.concatenate([jnp.where(m, t, zero) for m in in_chunk], axis=1)

    outs = []
    for h in range(HG_HEADS):
        cols = slice(h * HG_DIM, (h + 1) * HG_DIM)
        qd, ki, vv = q_dec[:, cols], k_inv_b[:, cols], val[:, cols]
        a = lax.dot_general(qd, ki, (((1,), (1,)), ((), ())), preferred_element_type=F32)
        a = jnp.where(causal, a, 0.0).astype(BF16)
        o_intra = jnp.dot(a, vv, preferred_element_type=F32)
        upd = lax.dot_general(vv, per_chunk_columns(k_end[:, cols]), (((0,), (0,)), ((), ())),
                              preferred_element_type=F32)
        st = state_ref[h]
        entering = []
        for c in range(n_chunks):
            entering.append(st.astype(BF16))
            st = st * last_rows[c][:, cols] + upd[:, c * HG_DIM:(c + 1) * HG_DIM]
        state_ref[h] = st
        o_inter = lax.dot_general(per_chunk_columns(qd), jnp.concatenate(entering, axis=1),
                                  (((1,), (1,)), ((), ())), preferred_element_type=F32)
        o = o_intra + o_inter
        o = o * lax.rsqrt(jnp.mean(jnp.square(o), axis=-1, keepdims=True) + RMS_EPS)
        outs.append(o)
    o = jnp.concatenate(outs, axis=1) * gain_ref[...]
    gg = g_ref[...].astype(F32)
    o_ref[...] = (o * (gg * jax.nn.sigmoid(gg))).astype(o_ref.dtype)


def _hgrn(u_hg, lb_logits, gain, B, S):
    tiles = S // HG_TILE
    col = lambda j: pl.BlockSpec((HG_TILE, HG_WIDTH), functools.partial(lambda b, t, j: (b * tiles + t, j), j=j))
    return pl.pallas_call(
        _hgrn_kernel,
        grid=(B, tiles),
        in_specs=[col(0), col(1), col(2), col(3),
                  pl.BlockSpec((2, HG_WIDTH), lambda b, t: (0, 0)),
                  pl.BlockSpec((1, HG_WIDTH), lambda b, t: (0, 0))],
        out_specs=pl.BlockSpec((HG_TILE, HG_WIDTH), lambda b, t: (b * tiles + t, 0)),
        out_shape=jax.ShapeDtypeStruct((B * S, HG_WIDTH), BF16),
        scratch_shapes=[pltpu.VMEM((HG_HEADS, HG_DIM, HG_DIM), F32)],
        compiler_params=_cparams("parallel", "arbitrary"),
        name="hgrn2",
    )(u_hg, u_hg, u_hg, u_hg, lb_logits, gain)


def _load_row_tiles(ref, n, start=0):
    return jnp.concatenate([ref[pl.ds(start + c, n, stride=LANE_CHUNKS), :] for c in range(LANE_CHUNKS)], axis=1)


def _store_row_tiles(ref, val, n):
    for c in range(LANE_CHUNKS):
        ref[pl.ds(c, n, stride=LANE_CHUNKS), :] = val[:, c * LANES:(c + 1) * LANES]


def _layer_norm(z, w, b):
    mu = jnp.mean(z, axis=-1, keepdims=True)
    zc = z - mu
    var = jnp.mean(jnp.square(zc), axis=-1, keepdims=True)
    return zc * lax.rsqrt(var + LN_EPS) * w + b


def _merge_kernel(ya_ref, yh_ref, ga_ref, gh_ref, x_ref, wa_ref, wh_ref, wo_ref, lw_ref, lb_ref, o_ref):
    ma = jnp.dot(ya_ref[...], wa_ref[...], preferred_element_type=F32)
    mh = jnp.dot(yh_ref[...], wh_ref[...], preferred_element_type=F32)
    merged = (jax.nn.sigmoid(ga_ref[...].astype(F32)) * ma + jax.nn.sigmoid(gh_ref[...].astype(F32)) * mh)
    z = DEEPNORM_ALPHA * x_ref[...] + jnp.dot(merged.astype(BF16), wo_ref[...], preferred_element_type=F32)
    _store_row_tiles(o_ref, _layer_norm(z, lw_ref[...], lb_ref[...]), MIX_ROWS)


def _merge(y_att, y_hg, u_hg, x2d, w_a, w_h, w_o, ln_w, ln_b):
    T = x2d.shape[0]
    rows = lambda width, j=0: pl.BlockSpec((MIX_ROWS, width), functools.partial(lambda i, j: (i, j), j=j))
    full = lambda a: pl.BlockSpec(a.shape, lambda i: (0, 0))
    return pl.pallas_call(
        _merge_kernel,
        grid=(T // MIX_ROWS,),
        in_specs=[rows(ATT_WIDTH), rows(HG_WIDTH), rows(D_MODEL, 4), rows(D_MODEL, 5), rows(D_MODEL),
                  full(w_a), full(w_h), full(w_o), full(ln_w), full(ln_b)],
        out_specs=pl.BlockSpec((MIX_ROWS * LANE_CHUNKS, LANES), lambda i: (i, 0)),
        out_shape=jax.ShapeDtypeStruct((T * LANE_CHUNKS, LANES), F32),
        compiler_params=_cparams("parallel"),
        name="merge_ln1",
    )(y_att, y_hg, u_hg, u_hg, x2d, w_a, w_h, w_o, ln_w, ln_b)


def _first_argmax(v, ids, n):
    mx = jnp.max(v, axis=0, keepdims=True)
    return mx, jnp.min(jnp.where(v == mx, ids, n), axis=0, keepdims=True)


def _route_kernel(x1_ref, p_ref, wrt_ref, rb_ref, wsg_ref, wsu_ref, wsd_ref, wpg_ref, wpp_ref,
                  base_ref, idx_ref, gate_ref, rank_ref, cnt_ref, carry_ref):
    @pl.when(pl.program_id(0) % (MOE_TILE // MIX_ROWS) == 0)
    def _():
        carry_ref[...] = jnp.zeros_like(carry_ref)

    x1 = _load_row_tiles(x1_ref, MIX_ROWS)
    x1b = x1.astype(BF16)
    logits = lax.dot_general(wrt_ref[...], x1, (((1,), (1,)), ((), ())), preferred_element_type=F32,
                             precision=lax.Precision.HIGHEST)
    s = jax.nn.sigmoid(logits)
    sel = s + rb_ref[...]
    eid = lax.broadcasted_iota(jnp.int32, (N_EXPERTS, MIX_ROWS), 0)
    neg = -jnp.inf

    grp = sel.reshape(N_GROUPS, GROUP_SIZE, MIX_ROWS)
    mid = lax.broadcasted_iota(jnp.int32, grp.shape, 1)
    m1 = jnp.max(grp, axis=1, keepdims=True)
    i1 = jnp.min(jnp.where(grp == m1, mid, GROUP_SIZE), axis=1, keepdims=True)
    m2 = jnp.max(jnp.where(mid == i1, neg, grp), axis=1, keepdims=True)
    gscore = (m1 + m2).reshape(N_GROUPS, MIX_ROWS)
    gid = lax.broadcasted_iota(jnp.int32, (N_GROUPS, MIX_ROWS), 0)
    gsel = jnp.zeros((N_GROUPS, MIX_ROWS), jnp.bool_)
    for _ in range(TOPK_GROUPS):
        _, gi = _first_argmax(gscore, gid, N_GROUPS)
        hit = gid == gi
        gsel = gsel | hit
        gscore = jnp.where(hit, neg, gscore)
    emask = jnp.broadcast_to(gsel.reshape(N_GROUPS, 1, MIX_ROWS), grp.shape).reshape(N_EXPERTS, MIX_ROWS)
    cand = jnp.where(emask, sel, neg)

    idxs, gates = [], []
    chosen = jnp.zeros((N_EXPERTS, MIX_ROWS), jnp.bool_)
    for _ in range(TOP_K):
        _, ei = _first_argmax(cand, eid, N_EXPERTS)
        hit = eid == ei
        idxs.append(ei)
        gates.append(jnp.sum(jnp.where(hit, s, 0.0), axis=0, keepdims=True))
        chosen = chosen | hit
        cand = jnp.where(hit, neg, cand)
    g = jnp.concatenate(gates, axis=0)
    g = g / jnp.sum(g, axis=0, keepdims=True) * ROUTED_SCALE
    idx_ref[...] = jnp.concatenate(idxs, axis=0)
    gate_ref[...] = g

    onehot = jnp.where(chosen, 1.0, 0.0)
    tr = lax.broadcasted_iota(jnp.int32, (MIX_ROWS, MIX_ROWS), 0)
    tc = lax.broadcasted_iota(jnp.int32, (MIX_ROWS, MIX_ROWS), 1)
    before = jnp.where(tr < tc, 1.0, 0.0).astype(BF16)
    prefix = jnp.dot(onehot.astype(BF16), before, preferred_element_type=F32)
    rankfull = (carry_ref[:, 0:1] + prefix).astype(jnp.int32)
    rank_ref[...] = jnp.concatenate(
        [jnp.sum(jnp.where(eid == ei, rankfull, 0), axis=0, keepdims=True) for ei in idxs], axis=0)
    total = carry_ref[...] + jnp.sum(onehot, axis=1, keepdims=True)
    carry_ref[...] = total
    cnt_ref[...] = total.astype(jnp.int32)

    hg = jnp.dot(x1b, wsg_ref[...], preferred_element_type=F32)
    hu = jnp.dot(x1b, wsu_ref[...], preferred_element_type=F32)
    shared = jnp.dot((hg * jax.nn.sigmoid(hg) * hu).astype(BF16), wsd_ref[...], preferred_element_type=F32)
    ple = (jax.nn.sigmoid(jnp.dot(x1b, wpg_ref[...], preferred_element_type=F32))
           * jnp.dot(p_ref[...].astype(BF16), wpp_ref[...], preferred_element_type=F32))
    _store_row_tiles(base_ref, DEEPNORM_ALPHA * x1 + shared + ple, MIX_ROWS)


def _route(x1, p2d, wr_t, rbias, wsg, wsu, wsd, wpg, wpp):
    T = x1.shape[0] // LANE_CHUNKS
    per_tile = MOE_TILE // MIX_ROWS
    full = lambda a: pl.BlockSpec(a.shape, lambda i: (0, 0))
    tok = pl.BlockSpec((TOP_K, MIX_ROWS), lambda i: (0, i))
    row_tiles = pl.BlockSpec((MIX_ROWS * LANE_CHUNKS, LANES), lambda i: (i, 0))
    return pl.pallas_call(
        _route_kernel,
        grid=(T // MIX_ROWS,),
        in_specs=[row_tiles,
                  pl.BlockSpec((MIX_ROWS, PLE_DIM), lambda i: (i, 0)),
                  full(wr_t), full(rbias), full(wsg), full(wsu), full(wsd), full(wpg), full(wpp)],
        out_specs=[row_tiles, tok, tok, tok,
                   pl.BlockSpec((None, N_EXPERTS, LANES), lambda i: (i // per_tile, 0, 0))],
        out_shape=[jax.ShapeDtypeStruct((T * LANE_CHUNKS, LANES), F32),
                   jax.ShapeDtypeStruct((TOP_K, T), jnp.int32),
                   jax.ShapeDtypeStruct((TOP_K, T), F32),
                   jax.ShapeDtypeStruct((TOP_K, T), jnp.int32),
                   jax.ShapeDtypeStruct((T // MOE_TILE, N_EXPERTS, LANES), jnp.int32)],
        scratch_shapes=[pltpu.VMEM((N_EXPERTS, LANES), F32)],
        compiler_params=_cparams("arbitrary"),
        name="route_shared_ple",
    )(x1, p2d, wr_t, rbias, wsg, wsu, wsd, wpg, wpp)


def _plan_sc_kernel(off_hbm, idx_hbm, rank_hbm, gate_hbm, rows_hbm, gates_hbm,
                    off_v, idx_v, rank_v, gate_v, rows_v, gates_v):
    n_tokens = idx_hbm.shape[0] // TOP_K
    worker = lax.axis_index("subcore") * SC_CORES + lax.axis_index("core")

    @pl.when(worker < n_tokens // MOE_TILE)
    def _():
        pltpu.sync_copy(off_hbm.at[pl.ds(worker * N_EXPERTS, N_EXPERTS)], off_v)
        lane = lax.iota(jnp.int32, SC_LANES)
        for k in range(TOP_K):
            row = pl.ds(k * n_tokens + worker * MOE_TILE, MOE_TILE)
            pltpu.sync_copy(idx_hbm.at[row], idx_v)
            pltpu.sync_copy(rank_hbm.at[row], rank_v)
            pltpu.sync_copy(gate_hbm.at[row], gate_v)

            @pl.loop(0, MOE_TILE // SC_LANES)
            def _(i):
                at = i * SC_LANES
                pos = plsc.load_gather(off_v, [idx_v[pl.ds(at, SC_LANES)]]) + rank_v[pl.ds(at, SC_LANES)]
                plsc.store_scatter(rows_v, [pos], (lane + at) * LANE_CHUNKS)
                plsc.store_scatter(gates_v, [pos], gate_v[pl.ds(at, SC_LANES)])

        @pl.loop(0, LIST_PAD // SC_LANES)
        def _(i):
            tail = pl.ds(MOE_TILE * TOP_K + i * SC_LANES, SC_LANES)
            rows_v[tail] = jnp.zeros((SC_LANES,), jnp.int32)
            gates_v[tail] = jnp.zeros((SC_LANES,), F32)

        out = pl.ds(worker * LIST_LEN, LIST_LEN)
        pltpu.sync_copy(rows_v, rows_hbm.at[out])
        pltpu.sync_copy(gates_v, gates_hbm.at[out])


def _plan_sc(off, idx, rank, gate):
    n_tiles = idx.shape[1] // MOE_TILE
    assert n_tiles <= SC_CORES * SC_SUBCORES
    mesh = plsc.VectorSubcoreMesh(core_axis_name="core", subcore_axis_name="subcore",
                                  num_cores=SC_CORES, num_subcores=SC_SUBCORES)
    return pl.kernel(
        _plan_sc_kernel,
        out_type=(jax.ShapeDtypeStruct((n_tiles * LIST_LEN,), jnp.int32),
                  jax.ShapeDtypeStruct((n_tiles * LIST_LEN,), F32)),
        mesh=mesh,
        scratch_types=[pltpu.VMEM((N_EXPERTS,), jnp.int32), pltpu.VMEM((MOE_TILE,), jnp.int32),
                       pltpu.VMEM((MOE_TILE,), jnp.int32), pltpu.VMEM((MOE_TILE,), F32),
                       pltpu.VMEM((LIST_LEN,), jnp.int32), pltpu.VMEM((LIST_LEN,), F32)],
        compiler_params=pltpu.CompilerParams(needs_layout_passes=False),
        name="moe_plan_sc",
    )(off, idx.reshape(-1), rank.reshape(-1), gate.reshape(-1))


def _moe_tile_kernel(cnt_ref, off_ref, wg_ref, wu_ref, wd_ref, lw_ref, lb_ref, rows_hbm, gates_hbm, x_hbm, base_hbm,
                     o_hbm, x_s, acc_s, xg_a, xg_b, xg_c, y_a, y_b, y_c, stage_s, rows_s, gates_s, sem):
    tile, e = pl.program_id(0), pl.program_id(1)
    rows_of = lambda ref, r, n: ref.at[pl.ds(pl.multiple_of(r * LANE_CHUNKS, LANE_CHUNKS), n * LANE_CHUNKS), :]
    tile_rows = pl.ds(pl.multiple_of(tile * (MOE_TILE * LANE_CHUNKS), LANE_CHUNKS), MOE_TILE * LANE_CHUNKS)

    pair = tile * N_EXPERTS + e
    last_pair = pl.num_programs(0) * N_EXPERTS - 1
    n, off = cnt_ref[pair], off_ref[pair]
    tile_at = lambda ref, r: ref.at[pl.ds(pl.multiple_of(r, LANE_CHUNKS), LANE_CHUNKS), :]

    def gather_group(xg, first, jb):
        at = first + jb * GATHER_GROUP
        rows = [tile_at(x_s, rows_s[at + u])[...] for u in range(GATHER_GROUP)]
        rows_of(xg, jb * GATHER_GROUP, GATHER_GROUP)[...] = jnp.concatenate(rows, axis=0)

    def gather_loop(xg, first):
        def body(jb, cc):
            gather_group(xg, first, jb)
            return cc

        lax.fori_loop(0, MOE_CHUNK // GATHER_GROUP, body, 0)

    def swiglu(xg, y):
        xb = _load_row_tiles(xg, MOE_CHUNK).astype(BF16)
        hg = jnp.dot(xb, wg_ref[...], preferred_element_type=F32)
        hu = jnp.dot(xb, wu_ref[...], preferred_element_type=F32)
        act = (hg * jax.nn.sigmoid(hg) * hu).astype(BF16)
        _store_row_tiles(y, jnp.dot(act, wd_ref[...], preferred_element_type=F32), MOE_CHUNK)

    def scatter_group(y, first, j0, live):
        dsts = [rows_s[first + j0 + u] for u in range(live)]
        gates = [gates_s[first + j0 + u] for u in range(live)]
        yv = rows_of(y, j0, live)[...]
        vals = [tile_at(acc_s, d)[...] + g * yv[u * LANE_CHUNKS:(u + 1) * LANE_CHUNKS]
                for u, (d, g) in enumerate(zip(dsts, gates))]
        for d, val in reversed(list(zip(dsts, vals))):
            tile_at(acc_s, d)[...] = val

    def scatter_loop(y, first, m):
        def body(jg, cc):
            scatter_group(y, first, jg * SCATTER_GROUP, SCATTER_GROUP)
            return cc

        lax.fori_loop(0, m // SCATTER_GROUP, body, 0)
        for live in range(1, SCATTER_GROUP):
            @pl.when(m % SCATTER_GROUP == live)
            def _(live=live):
                scatter_group(y, first, m - live, live)

    def plan_loads(t):
        plan = pl.ds(pl.multiple_of(t * LIST_LEN, LIST_PAD), LIST_LEN)
        return (pltpu.make_async_copy(rows_hbm.at[plan], rows_s, sem.at[0]),
                pltpu.make_async_copy(gates_hbm.at[plan], gates_s, sem.at[1]))

    def x_load(t):
        rows = pl.ds(pl.multiple_of(t * (MOE_TILE * LANE_CHUNKS), LANE_CHUNKS), MOE_TILE * LANE_CHUNKS)
        return pltpu.make_async_copy(x_hbm.at[rows, :], x_s, sem.at[2])

    @pl.when(e == 0)
    def _():
        @pl.when(tile == 0)
        def _():
            x_load(tile).start()
            for load in plan_loads(tile):
                load.start()

        load_base = pltpu.make_async_copy(base_hbm.at[tile_rows, :], acc_s, sem.at[3])
        load_base.start()
        y_b[...] = jnp.zeros_like(y_b)
        load_rows, load_gates = plan_loads(tile)
        load_rows.wait()
        x_load(tile).wait()
        gather_loop(xg_a, off)
        load_gates.wait()
        load_base.wait()

    prev_off = off_ref[jnp.maximum(pair - 1, 0)]
    next_off = off_ref[jnp.minimum(pair + 1, last_pair)]
    live_row = lax.broadcasted_iota(jnp.int32, (MOE_CHUNK, 2 * LANES), 0) < n

    def run_expert(xg_cur, y_cur, xg_nxt, y_prv):
        gathers = [functools.partial(gather_group, xg_nxt, next_off, jb) for jb in range(MOE_CHUNK // GATHER_GROUP)]
        scatters = [functools.partial(scatter_group, y_prv, prev_off, jg * SCATTER_GROUP, SCATTER_GROUP)
                    for jg in range(MOE_CHUNK // SCATTER_GROUP)]
        side = [s for both in zip(gathers, scatters) for s in both]
        n_down = LANE_CHUNKS // 2
        cost = [D_MODEL, D_MODEL] + [EXPERT_FF] * n_down
        bounds = [round(len(side) * sum(cost[:i]) / sum(cost)) for i in range(len(cost) + 1)]

        def side_work(i):
            for s in side[bounds[i]:bounds[i + 1]]:
                s()

        xb = _load_row_tiles(xg_cur, MOE_CHUNK).astype(BF16)
        side_work(0)
        hg = jnp.dot(xb, wg_ref[...], preferred_element_type=F32)
        side_work(1)
        hu = jnp.dot(xb, wu_ref[...], preferred_element_type=F32)
        act = (hg * jax.nn.sigmoid(hg) * hu).astype(BF16)
        for q in range(n_down):
            side_work(2 + q)
            out = jnp.dot(act, wd_ref[:, q * 2 * LANES:(q + 1) * 2 * LANES], preferred_element_type=F32)
            out = jnp.where(live_row, out, 0.0)
            for c in range(2):
                y_cur[pl.ds(2 * q + c, MOE_CHUNK, stride=LANE_CHUNKS), :] = out[:, c * LANES:(c + 1) * LANES]

    @pl.when(e % 2 == 0)
    def _():
        run_expert(xg_a, y_a, xg_b, y_b)

    @pl.when(e % 2 == 1)
    def _():
        run_expert(xg_b, y_b, xg_a, y_a)

    def extra_chunk(c, carry):
        first = off + c * MOE_CHUNK
        gather_loop(xg_c, first)
        swiglu(xg_c, y_c)
        scatter_loop(y_c, first, jnp.minimum(MOE_CHUNK, n - c * MOE_CHUNK))
        return carry

    lax.fori_loop(1, (n + MOE_CHUNK - 1) // MOE_CHUNK, extra_chunk, 0)

    @pl.when(e == N_EXPERTS - 1)
    def _():
        more_tiles = tile + 1 < pl.num_programs(0)

        @pl.when(more_tiles)
        def _():
            x_load(tile + 1).start()

        scatter_loop(y_b, off, jnp.minimum(MOE_CHUNK, n))

        @pl.when(more_tiles)
        def _():
            for load in plan_loads(tile + 1):
                load.start()

        n_pieces = MOE_TILE // LN_ROWS
        store = lambda c: pltpu.make_async_copy(
            stage_s.at[c % 2], o_hbm.at[pl.ds(tile * MOE_TILE + c * LN_ROWS, LN_ROWS), :], sem.at[4 + c % 2])
        for c in range(n_pieces):
            z = _load_row_tiles(acc_s, LN_ROWS, c * LN_ROWS * LANE_CHUNKS)
            if c >= 2:
                store(c - 2).wait()
            stage_s[c % 2] = _layer_norm(z, lw_ref[...], lb_ref[...])
            store(c).start()
        store(n_pieces - 2).wait()
        store(n_pieces - 1).wait()


def _moe_tiles(x1, base, plan_rows, plan_gates, cnt, off, wg, wu, wd, ln_w, ln_b):
    T = x1.shape[0] // LANE_CHUNKS
    w_spec = lambda shape: pl.BlockSpec((None,) + shape, lambda i, e, cnt, off: (e, 0, 0))
    vec = pl.BlockSpec((1, D_MODEL), lambda i, e, cnt, off: (0, 0))
    hbm = pl.BlockSpec(memory_space=pl.ANY)
    tile_rows = MOE_TILE * LANE_CHUNKS
    return pl.pallas_call(
        _moe_tile_kernel,
        grid_spec=pltpu.PrefetchScalarGridSpec(
            num_scalar_prefetch=2,
            grid=(T // MOE_TILE, N_EXPERTS),
            in_specs=[w_spec((D_MODEL, EXPERT_FF)), w_spec((D_MODEL, EXPERT_FF)), w_spec((EXPERT_FF, D_MODEL)),
                      vec, vec, hbm, hbm, hbm, hbm],
            out_specs=hbm,
            scratch_shapes=[pltpu.VMEM((tile_rows, LANES), F32),
                            pltpu.VMEM((tile_rows, LANES), F32)]
                           + [pltpu.VMEM((MOE_CHUNK * LANE_CHUNKS, LANES), F32)] * 6
                           + [pltpu.VMEM((2, LN_ROWS, D_MODEL), F32),
                              pltpu.SMEM((LIST_LEN,), jnp.int32),
                              pltpu.SMEM((LIST_LEN,), F32),
                              pltpu.SemaphoreType.DMA((6,))],
        ),
        out_shape=jax.ShapeDtypeStruct((T, D_MODEL), F32),
        compiler_params=_cparams("arbitrary", "arbitrary"),
        name="moe_tiles_ln2",
    )(cnt, off, wg, wu, wd, ln_w, ln_b, plan_rows, plan_gates, x1, base)


def kernel(x, p, w_in, hgrn_lb_logits, hgrn_norm_w, w_branch_att, w_branch_hgrn, w_out, ln1_w, ln1_b, router_w, router_bias, expert_w_gate, expert_w_up, expert_w_down, shared_w_gate, shared_w_up, shared_w_down, ple_gate_w, ple_proj_w, ln2_w, ln2_b):
    B, S, D = x.shape
    T = B * S
    l = 0
    x2d = x.reshape(T, D)
    bf = lambda a: a.astype(BF16)

    ws = _att_weights(w_in[l])
    qkv = [_proj_att(x2d, ws[g], d) for g, d in enumerate(ATT_DILATIONS)]
    y_att = _attention(qkv, B, S)
    u_hg = _proj(x2d, bf(w_in[l][:, 3 * len(ATT_DILATIONS) * ATT_WIDTH:]), 1536)
    y_hg = _hgrn(u_hg, hgrn_lb_logits, hgrn_norm_w[l:l + 1], B, S)
    x1 = _merge(y_att, y_hg, u_hg, x2d, bf(w_branch_att[l]), bf(w_branch_hgrn[l]), bf(w_out[l]),
                ln1_w[l:l + 1], ln1_b[l:l + 1])

    base, idx, gate, rank, counts = _route(
        x1, p[l].reshape(T, PLE_DIM), router_w[l].T, router_bias[l].reshape(N_EXPERTS, 1),
        bf(shared_w_gate[l]), bf(shared_w_up[l]), bf(shared_w_down[l]), bf(ple_gate_w[l]), bf(ple_proj_w[l]))
    cnt = counts[:, :, 0]
    off = jnp.cumsum(cnt, axis=1) - cnt
    cnt, off = cnt.reshape(-1), off.reshape(-1)
    plan_rows, plan_gates = _plan_sc(off, idx, rank, gate)
    out = _moe_tiles(x1, base, plan_rows, plan_gates, cnt, off, bf(expert_w_gate[l]), bf(expert_w_up[l]),
                     bf(expert_w_down[l]), ln2_w[l:l + 1], ln2_b[l:l + 1])
    return out.reshape(B, S, D)
```

```python
import functools

import jax
import jax.numpy as jnp
import numpy as np
from jax import lax
from jax.experimental import pallas as pl
from jax.experimental.pallas import tpu as pltpu
from jax.experimental.pallas import tpu_sc as plsc

F32 = jnp.float32
BF16 = jnp.bfloat16

D_MODEL = 1024
ATT_HEAD_DIM = 64
ATT_HEADS = 8
ATT_DILATIONS = (1, 4, 16)
ATT_BLOCK = 128
ATT_WIDTH = ATT_HEADS * ATT_HEAD_DIM
ATT_TILE = ATT_BLOCK * max(ATT_DILATIONS)
NEG_INF = -1e30
LOG2_E = 1.4426950408889634

HG_HEADS = 8
HG_DIM = 128
HG_WIDTH = HG_HEADS * HG_DIM
HG_CHUNK = 32
HG_TILE = 256
RMS_EPS = 1e-6

N_EXPERTS = 64
TOP_K = 8
N_GROUPS = 8
GROUP_SIZE = N_EXPERTS // N_GROUPS
TOPK_GROUPS = 4
EXPERT_FF = 256
ROUTED_SCALE = 2.5
PLE_DIM = 256
LN_EPS = 1e-5
DEPTH = 1
DEEPNORM_ALPHA = (2.0 * DEPTH) ** 0.25

LANES = 128
LANE_CHUNKS = D_MODEL // LANES
PROJ_ROWS = 512
PROJ_COLS = 1536
ATT_PROJ_ROWS = 1024
MIX_ROWS = 512
MOE_TILE = 4096
MOE_CHUNK = 576
LN_ROWS = 256
LIST_PAD = 1024
LIST_LEN = MOE_TILE * TOP_K + LIST_PAD
GATHER_GROUP = 8
SCATTER_GROUP = 8
V7X_VMEM_LIMIT = 56 * 1024 * 1024
SC_CORES, SC_SUBCORES, SC_LANES = 2, 16, 16


def _cparams(*sem):
    return pltpu.CompilerParams(dimension_semantics=sem, vmem_limit_bytes=V7X_VMEM_LIMIT)


def _proj_att_kernel(*refs, dil):
    x_refs, w_ref, o_ref = refs[:LANE_CHUNKS], refs[LANE_CHUNKS], refs[LANE_CHUNKS + 1]
    n = ATT_PROJ_ROWS // dil

    def rows(ref):
        if dil == 1:
            return ref[...]
        return jnp.concatenate([ref[pl.ds(r, n, stride=dil), :] for r in range(dil)], axis=0)

    xp = jnp.concatenate([rows(ref).astype(BF16) for ref in x_refs], axis=1)
    y = jnp.dot(xp, w_ref[...], preferred_element_type=F32)
    o_ref[...] = y.astype(BF16).reshape(dil, n, 3 * ATT_WIDTH)


def _proj_att(x2d, w, dil):
    T = x2d.shape[0]
    per = ATT_TILE // ATT_PROJ_ROWS
    n = ATT_PROJ_ROWS // dil
    out = pl.pallas_call(
        functools.partial(_proj_att_kernel, dil=dil),
        grid=(T // ATT_PROJ_ROWS,),
        in_specs=[pl.BlockSpec((ATT_PROJ_ROWS, LANES), functools.partial(lambda i, c: (i, c), c=c))
                  for c in range(LANE_CHUNKS)]
                 + [pl.BlockSpec((D_MODEL, 3 * ATT_WIDTH), lambda i: (0, 0))],
        out_specs=pl.BlockSpec((None, dil, None, n, 3 * ATT_WIDTH), lambda i: (i // per, 0, i % per, 0, 0)),
        out_shape=jax.ShapeDtypeStruct((T // ATT_TILE, dil, per, n, 3 * ATT_WIDTH), BF16),
        compiler_params=_cparams("parallel"),
        name=f"proj_att_d{dil}",
    )(*([x2d] * LANE_CHUNKS), w)
    return out.reshape(T // ATT_TILE, dil, ATT_TILE // dil, 3 * ATT_WIDTH)


def _att_pair(q2, kp, kc, vp, vc, bias_ref, g, first):
    def head0_lanes(rows, dtype):
        lane = lax.broadcasted_iota(jnp.int32, (rows, 2 * ATT_HEAD_DIM), 1)
        return lane.astype(F32).astype(dtype) < ATT_HEAD_DIM

    lo_q = head0_lanes(ATT_BLOCK, BF16)
    lo_v = head0_lanes(2 * ATT_BLOCK, BF16)
    k2 = jnp.concatenate([kp, kc], axis=0)
    v2 = jnp.concatenate([vp, vc], axis=0)
    zero = jnp.zeros_like(q2)
    ps, ms = [], []
    for hh in range(2):
        qm = jnp.where(lo_q, q2, zero) if hh == 0 else jnp.where(lo_q, zero, q2)
        s = lax.dot_general(qm, k2, (((1,), (1,)), ((), ())), preferred_element_type=F32)
        s = s + bias_ref[g, hh, first]
        m = jnp.max(s, axis=-1, keepdims=True)
        ps.append(jnp.exp2(s - m).astype(BF16))
        ms.append(m)
    pcat = jnp.concatenate(ps, axis=1)
    zero_v, one_v = jnp.zeros_like(v2), jnp.ones_like(v2)
    rhs = jnp.concatenate([
        jnp.concatenate([jnp.where(lo_v, v2, zero_v), jnp.where(lo_v, one_v, zero_v)], axis=1),
        jnp.concatenate([jnp.where(lo_v, zero_v, v2), jnp.where(lo_v, zero_v, one_v)], axis=1)], axis=0)
    nd = jnp.dot(pcat, rhs, preferred_element_type=F32)
    m2 = jnp.where(head0_lanes(ATT_BLOCK, F32), ms[0], ms[1])
    return nd[:, :2 * ATT_HEAD_DIM], m2, nd[:, 2 * ATT_HEAD_DIM:]


def _att_kernel(*refs):
    (q0, kc0, vc0, kp0, vp0, q1, kc1, vc1, kp1, vp1, q2, kc2, vc2, kp2, vp2,
     bias_ref, o_ref) = refs[:17]
    ng = len(ATT_DILATIONS)
    num_s, m_s, den_s = refs[17:17 + ng], refs[17 + ng:17 + 2 * ng], refs[17 + 2 * ng:]
    first_tile = (pl.program_id(2) == 0).astype(jnp.int32)
    groups = ((q0, kc0, vc0, kp0, vp0), (q1, kc1, vc1, kp1, vp1), (q2, kc2, vc2, kp2, vp2))
    for g, dil in enumerate(ATT_DILATIONS):
        q_ref, kc_ref, vc_ref, kp_ref, vp_ref = groups[g]
        nb = ATT_TILE // dil // ATT_BLOCK
        for r in range(dil):
            for n in range(nb):
                rows = pl.ds(n * ATT_BLOCK, ATT_BLOCK)
                if n == 0:
                    prev = pl.ds((nb - 1) * ATT_BLOCK, ATT_BLOCK)
                    kp, vp, first = kp_ref[r, prev, :], vp_ref[r, prev, :], first_tile
                else:
                    prev = pl.ds((n - 1) * ATT_BLOCK, ATT_BLOCK)
                    kp, vp, first = kc_ref[r, prev, :], vc_ref[r, prev, :], 0
                num, m, den = _att_pair(q_ref[r, rows, :], kp, kc_ref[r, rows, :], vp, vc_ref[r, rows, :],
                                        bias_ref, g, first)
                if dil == 1:
                    dst = rows
                else:
                    dst = pl.ds(n * ATT_BLOCK * dil + r, ATT_BLOCK, stride=dil)
                num_s[g][dst, :] = num
                m_s[g][dst, :] = m
                den_s[g][dst, :] = den
    m_all = jnp.maximum(jnp.maximum(m_s[0][...], m_s[1][...]), m_s[2][...])
    num = jnp.zeros((ATT_TILE, 2 * ATT_HEAD_DIM), F32)
    den = jnp.zeros((ATT_TILE, 2 * ATT_HEAD_DIM), F32)
    for g in range(ng):
        sc = jnp.exp2(m_s[g][...] - m_all)
        num = num + sc * num_s[g][...]
        den = den + sc * den_s[g][...]
    o_ref[...] = (num / den).astype(o_ref.dtype)


def _att_bias_table():
    qi = np.arange(ATT_BLOCK)[:, None]
    ki = np.arange(2 * ATT_BLOCK)[None, :]
    steps = qi + ATT_BLOCK - ki
    valid = (steps >= 0) & (steps <= ATT_BLOCK)
    slopes = np.array([2.0 ** (-8.0 * (h + 1) / ATT_HEADS) for h in range(ATT_HEADS)], np.float32)
    tab = np.empty((len(ATT_DILATIONS), ATT_HEADS, 2, ATT_BLOCK, 2 * ATT_BLOCK), np.float32)
    for g, dil in enumerate(ATT_DILATIONS):
        bias = -slopes[:, None, None] * (steps * dil).astype(np.float32)[None] * LOG2_E
        tab[g, :, 0] = np.where(valid[None], bias, NEG_INF)
        tab[g, :, 1] = np.where((valid & (ki >= ATT_BLOCK))[None], bias, NEG_INF)
    return jnp.asarray(tab)


def _attention(qkv, B, S):
    tiles = S // ATT_TILE
    pair = 2 * ATT_HEAD_DIM
    npair = ATT_WIDTH // pair
    in_specs, args = [], []
    for g, dil in enumerate(ATT_DILATIONS):
        blk = (None, dil, ATT_TILE // dil, pair)
        cur = lambda b, hp, t, off: (b * tiles + t, 0, 0, off * npair + hp)
        prv = lambda b, hp, t, off: (b * tiles + jnp.maximum(t - 1, 0), 0, 0, off * npair + hp)
        in_specs += [pl.BlockSpec(blk, functools.partial(cur, off=0)),
                     pl.BlockSpec(blk, functools.partial(cur, off=1)),
                     pl.BlockSpec(blk, functools.partial(cur, off=2)),
                     pl.BlockSpec(blk, functools.partial(prv, off=1)),
                     pl.BlockSpec(blk, functools.partial(prv, off=2))]
        args += [qkv[g]] * 5
    in_specs.append(pl.BlockSpec((len(ATT_DILATIONS), 2, 2, ATT_BLOCK, 2 * ATT_BLOCK),
                                 lambda b, hp, t: (0, hp, 0, 0, 0)))
    args.append(_att_bias_table())
    scratch = [pltpu.VMEM((ATT_TILE, pair), F32) for _ in range(3 * len(ATT_DILATIONS))]
    return pl.pallas_call(
        _att_kernel,
        grid=(B, npair, tiles),
        in_specs=in_specs,
        out_specs=pl.BlockSpec((ATT_TILE, pair), lambda b, hp, t: (b * tiles + t, hp)),
        out_shape=jax.ShapeDtypeStruct((B * S, ATT_WIDTH), BF16),
        scratch_shapes=scratch,
        compiler_params=_cparams("parallel", "parallel", "arbitrary"),
        name="dilated_attention",
    )(*args)


def _att_weights(w_in_l):
    out = []
    width = len(ATT_DILATIONS) * ATT_WIDTH
    for g in range(len(ATT_DILATIONS)):
        cols = [w_in_l[:, part * width + g * ATT_WIDTH: part * width + (g + 1) * ATT_WIDTH] for part in range(3)]
        cols[0] = cols[0] * (ATT_HEAD_DIM ** -0.5 * LOG2_E)
        out.append(jnp.concatenate(cols, axis=1).astype(BF16))
    return out


def _proj_kernel(x_ref, w_ref, o_ref, *, col_tile):
    xb = x_ref[...].astype(BF16)
    for c in range(w_ref.shape[1] // col_tile):
        cols = slice(c * col_tile, (c + 1) * col_tile)
        o_ref[:, cols] = jnp.dot(xb, w_ref[:, cols], preferred_element_type=F32).astype(o_ref.dtype)


def _proj(x2d, w, col_tile):
    T, N = x2d.shape[0], w.shape[1]
    return pl.pallas_call(
        functools.partial(_proj_kernel, col_tile=col_tile),
        grid=(T // PROJ_ROWS,),
        in_specs=[pl.BlockSpec((PROJ_ROWS, D_MODEL), lambda i: (i, 0)),
                  pl.BlockSpec((D_MODEL, N), lambda i: (0, 0))],
        out_specs=pl.BlockSpec((PROJ_ROWS, N), lambda i: (i, 0)),
        out_shape=jax.ShapeDtypeStruct((T, N), BF16),
        compiler_params=_cparams("parallel"),
        name="proj_hgrn_gates",
    )(x2d, w)


def _split3(v):
    a = v.astype(BF16)
    r = v - a.astype(F32)
    b = r.astype(BF16)
    c = (r - b.astype(F32)).astype(BF16)
    return a, b, c


def _hgrn_kernel(q_ref, f_ref, i_ref, g_ref, lbl_ref, gain_ref, o_ref, state_ref):
    @pl.when(pl.program_id(1) == 0)
    def _():
        state_ref[...] = jnp.zeros_like(state_ref)

    lbl = lbl_ref[...]
    e = jnp.exp(lbl - jnp.max(lbl, axis=0, keepdims=True))
    lb = e[0:1] / jnp.sum(e, axis=0, keepdims=True)
    forget = lb + (1.0 - lb) * jax.nn.sigmoid(f_ref[...].astype(F32))
    log_f = jnp.log(forget)
    key = 1.0 - forget

    row = lax.broadcasted_iota(jnp.int32, (HG_TILE, HG_TILE), 0)
    col = lax.broadcasted_iota(jnp.int32, (HG_TILE, HG_TILE), 1)
    causal = (row >= col) & ((row // HG_CHUNK) == (col // HG_CHUNK))
    tri = jnp.where(causal, 1.0, 0.0).astype(BF16)
    b = sum(jnp.dot(tri, t, preferred_element_type=F32) for t in _split3(log_f))
    eb = jnp.exp(b)
    q_dec = (q_ref[...].astype(F32) * eb).astype(BF16)
    k_inv = key * jnp.exp(-b)
    xi = i_ref[...].astype(F32)
    val = (xi * jax.nn.sigmoid(xi)).astype(BF16)
    k_inv_b = k_inv.astype(BF16)

    n_chunks = HG_TILE // HG_CHUNK
    last_rows = [eb[(c + 1) * HG_CHUNK - 1:(c + 1) * HG_CHUNK, :] for c in range(n_chunks)]
    dec_rows = jnp.concatenate([jnp.broadcast_to(r, (HG_CHUNK, HG_WIDTH)) for r in last_rows], axis=0)
    k_end = (k_inv * dec_rows).astype(BF16)
    chunk_of_row = (lax.broadcasted_iota(jnp.int32, (HG_TILE, HG_DIM), 0) // HG_CHUNK).astype(F32).astype(BF16)
    in_chunk = [chunk_of_row == c for c in range(n_chunks)]
    zero = jnp.zeros((HG_TILE, HG_DIM), BF16)

    def per_chunk_columns(t):
        return jnp.concatenate([jnp.where(m, t, zero) for m in in_chunk], axis=1)

    outs = []
    for h in range(HG_HEADS):
        cols = slice(h * HG_DIM, (h + 1) * HG_DIM)
        qd, ki, vv = q_dec[:, cols], k_inv_b[:, cols], val[:, cols]
        a = lax.dot_general(qd, ki, (((1,), (1,)), ((), ())), preferred_element_type=F32)
        a = jnp.where(causal, a, 0.0).astype(BF16)
        o_intra = jnp.dot(a, vv, preferred_element_type=F32)
        upd = lax.dot_general(vv, per_chunk_columns(k_end[:, cols]), (((0,), (0,)), ((), ())),
                              preferred_element_type=F32)
        st = state_ref[h]
        entering = []
        for c in range(n_chunks):
            entering.append(st.astype(BF16))
            st = st * last_rows[c][:, cols] + upd[:, c * HG_DIM:(c + 1) * HG_DIM]
        state_ref[h] = st
        o_inter = lax.dot_general(per_chunk_columns(qd), jnp.concatenate(entering, axis=1),
                                  (((1,), (1,)), ((), ())), preferred_element_type=F32)
        o = o_intra + o_inter
        o = o * lax.rsqrt(jnp.mean(jnp.square(o), axis=-1, keepdims=True) + RMS_EPS)
        outs.append(o)
    o = jnp.concatenate(outs, axis=1) * gain_ref[...]
    gg = g_ref[...].astype(F32)
    o_ref[...] = (o * (gg * jax.nn.sigmoid(gg))).astype(o_ref.dtype)


def _hgrn(u_hg, lb_logits, gain, B, S):
    tiles = S // HG_TILE
    col = lambda j: pl.BlockSpec((HG_TILE, HG_WIDTH), functools.partial(lambda b, t, j: (b * tiles + t, j), j=j))
    return pl.pallas_call(
        _hgrn_kernel,
        grid=(B, tiles),
        in_specs=[col(0), col(1), col(2), col(3),
                  pl.BlockSpec((2, HG_WIDTH), lambda b, t: (0, 0)),
                  pl.BlockSpec((1, HG_WIDTH), lambda b, t: (0, 0))],
        out_specs=pl.BlockSpec((HG_TILE, HG_WIDTH), lambda b, t: (b * tiles + t, 0)),
        out_shape=jax.ShapeDtypeStruct((B * S, HG_WIDTH), BF16),
        scratch_shapes=[pltpu.VMEM((HG_HEADS, HG_DIM, HG_DIM), F32)],
        compiler_params=_cparams("parallel", "arbitrary"),
        name="hgrn2",
    )(u_hg, u_hg, u_hg, u_hg, lb_logits, gain)


def _load_row_tiles(ref, n, start=0):
    return jnp.concatenate([ref[pl.ds(start + c, n, stride=LANE_CHUNKS), :] for c in range(LANE_CHUNKS)], axis=1)


def _store_row_tiles(ref, val, n):
    for c in range(LANE_CHUNKS):
        ref[pl.ds(c, n, stride=LANE_CHUNKS), :] = val[:, c * LANES:(c + 1) * LANES]


def _layer_norm(z, w, b):
    mu = jnp.mean(z, axis=-1, keepdims=True)
    zc = z - mu
    var = jnp.mean(jnp.square(zc), axis=-1, keepdims=True)
    return zc * lax.rsqrt(var + LN_EPS) * w + b


def _merge_kernel(ya_ref, yh_ref, ga_ref, gh_ref, x_ref, wa_ref, wh_ref, wo_ref, lw_ref, lb_ref, o_ref):
    ma = jnp.dot(ya_ref[...], wa_ref[...], preferred_element_type=F32)
    mh = jnp.dot(yh_ref[...], wh_ref[...], preferred_element_type=F32)
    merged = (jax.nn.sigmoid(ga_ref[...].astype(F32)) * ma + jax.nn.sigmoid(gh_ref[...].astype(F32)) * mh)
    z = DEEPNORM_ALPHA * x_ref[...] + jnp.dot(merged.astype(BF16), wo_ref[...], preferred_element_type=F32)
    _store_row_tiles(o_ref, _layer_norm(z, lw_ref[...], lb_ref[...]), MIX_ROWS)


def _merge(y_att, y_hg, u_hg, x2d, w_a, w_h, w_o, ln_w, ln_b):
    T = x2d.shape[0]
    rows = lambda width, j=0: pl.BlockSpec((MIX_ROWS, width), functools.partial(lambda i, j: (i, j), j=j))
    full = lambda a: pl.BlockSpec(a.shape, lambda i: (0, 0))
    return pl.pallas_call(
        _merge_kernel,
        grid=(T // MIX_ROWS,),
        in_specs=[rows(ATT_WIDTH), rows(HG_WIDTH), rows(D_MODEL, 4), rows(D_MODEL, 5), rows(D_MODEL),
                  full(w_a), full(w_h), full(w_o), full(ln_w), full(ln_b)],
        out_specs=pl.BlockSpec((MIX_ROWS * LANE_CHUNKS, LANES), lambda i: (i, 0)),
        out_shape=jax.ShapeDtypeStruct((T * LANE_CHUNKS, LANES), F32),
        compiler_params=_cparams("parallel"),
        name="merge_ln1",
    )(y_att, y_hg, u_hg, u_hg, x2d, w_a, w_h, w_o, ln_w, ln_b)


def _first_argmax(v, ids, n):
    mx = jnp.max(v, axis=0, keepdims=True)
    return mx, jnp.min(jnp.where(v == mx, ids, n), axis=0, keepdims=True)


def _route_kernel(x1_ref, p_ref, wrt_ref, rb_ref, wsg_ref, wsu_ref, wsd_ref, wpg_ref, wpp_ref,
                  base_ref, idx_ref, gate_ref, rank_ref, cnt_ref, carry_ref):
    @pl.when(pl.program_id(0) % (MOE_TILE // MIX_ROWS) == 0)
    def _():
        carry_ref[...] = jnp.zeros_like(carry_ref)

    x1 = _load_row_tiles(x1_ref, MIX_ROWS)
    x1b = x1.astype(BF16)
    logits = lax.dot_general(wrt_ref[...], x1, (((1,), (1,)), ((), ())), preferred_element_type=F32,
                             precision=lax.Precision.HIGHEST)
    s = jax.nn.sigmoid(logits)
    sel = s + rb_ref[...]
    eid = lax.broadcasted_iota(jnp.int32, (N_EXPERTS, MIX_ROWS), 0)
    neg = -jnp.inf

    grp = sel.reshape(N_GROUPS, GROUP_SIZE, MIX_ROWS)
    mid = lax.broadcasted_iota(jnp.int32, grp.shape, 1)
    m1 = jnp.max(grp, axis=1, keepdims=True)
    i1 = jnp.min(jnp.where(grp == m1, mid, GROUP_SIZE), axis=1, keepdims=True)
    m2 = jnp.max(jnp.where(mid == i1, neg, grp), axis=1, keepdims=True)
    gscore = (m1 + m2).reshape(N_GROUPS, MIX_ROWS)
    gid = lax.broadcasted_iota(jnp.int32, (N_GROUPS, MIX_ROWS), 0)
    gsel = jnp.zeros((N_GROUPS, MIX_ROWS), jnp.bool_)
    for _ in range(TOPK_GROUPS):
        _, gi = _first_argmax(gscore, gid, N_GROUPS)
        hit = gid == gi
        gsel = gsel | hit
        gscore = jnp.where(hit, neg, gscore)
    emask = jnp.broadcast_to(gsel.reshape(N_GROUPS, 1, MIX_ROWS), grp.shape).reshape(N_EXPERTS, MIX_ROWS)
    cand = jnp.where(emask, sel, neg)

    idxs, gates = [], []
    chosen = jnp.zeros((N_EXPERTS, MIX_ROWS), jnp.bool_)
    for _ in range(TOP_K):
        _, ei = _first_argmax(cand, eid, N_EXPERTS)
        hit = eid == ei
        idxs.append(ei)
        gates.append(jnp.sum(jnp.where(hit, s, 0.0), axis=0, keepdims=True))
        chosen = chosen | hit
        cand = jnp.where(hit, neg, cand)
    g = jnp.concatenate(gates, axis=0)
    g = g / jnp.sum(g, axis=0, keepdims=True) * ROUTED_SCALE
    idx_ref[...] = jnp.concatenate(idxs, axis=0)
    gate_ref[...] = g

    onehot = jnp.where(chosen, 1.0, 0.0)
    tr = lax.broadcasted_iota(jnp.int32, (MIX_ROWS, MIX_ROWS), 0)
    tc = lax.broadcasted_iota(jnp.int32, (MIX_ROWS, MIX_ROWS), 1)
    before = jnp.where(tr < tc, 1.0, 0.0).astype(BF16)
    prefix = jnp.dot(onehot.astype(BF16), before, preferred_element_type=F32)
    rankfull = (carry_ref[:, 0:1] + prefix).astype(jnp.int32)
    rank_ref[...] = jnp.concatenate(
        [jnp.sum(jnp.where(eid == ei, rankfull, 0), axis=0, keepdims=True) for ei in idxs], axis=0)
    total = carry_ref[...] + jnp.sum(onehot, axis=1, keepdims=True)
    carry_ref[...] = total
    cnt_ref[...] = total.astype(jnp.int32)

    hg = jnp.dot(x1b, wsg_ref[...], preferred_element_type=F32)
    hu = jnp.dot(x1b, wsu_ref[...], preferred_element_type=F32)
    shared = jnp.dot((hg * jax.nn.sigmoid(hg) * hu).astype(BF16), wsd_ref[...], preferred_element_type=F32)
    ple = (jax.nn.sigmoid(jnp.dot(x1b, wpg_ref[...], preferred_element_type=F32))
           * jnp.dot(p_ref[...].astype(BF16), wpp_ref[...], preferred_element_type=F32))
    _store_row_tiles(base_ref, DEEPNORM_ALPHA * x1 + shared + ple, MIX_ROWS)


def _route(x1, p2d, wr_t, rbias, wsg, wsu, wsd, wpg, wpp):
    T = x1.shape[0] // LANE_CHUNKS
    per_tile = MOE_TILE // MIX_ROWS
    full = lambda a: pl.BlockSpec(a.shape, lambda i: (0, 0))
    tok = pl.BlockSpec((TOP_K, MIX_ROWS), lambda i: (0, i))
    row_tiles = pl.BlockSpec((MIX_ROWS * LANE_CHUNKS, LANES), lambda i: (i, 0))
    return pl.pallas_call(
        _route_kernel,
        grid=(T // MIX_ROWS,),
        in_specs=[row_tiles,
                  pl.BlockSpec((MIX_ROWS, PLE_DIM), lambda i: (i, 0)),
                  full(wr_t), full(rbias), full(wsg), full(wsu), full(wsd), full(wpg), full(wpp)],
        out_specs=[row_tiles, tok, tok, tok,
                   pl.BlockSpec((None, N_EXPERTS, LANES), lambda i: (i // per_tile, 0, 0))],
        out_shape=[jax.ShapeDtypeStruct((T * LANE_CHUNKS, LANES), F32),
                   jax.ShapeDtypeStruct((TOP_K, T), jnp.int32),
                   jax.ShapeDtypeStruct((TOP_K, T), F32),
                   jax.ShapeDtypeStruct((TOP_K, T), jnp.int32),
                   jax.ShapeDtypeStruct((T // MOE_TILE, N_EXPERTS, LANES), jnp.int32)],
        scratch_shapes=[pltpu.VMEM((N_EXPERTS, LANES), F32)],
        compiler_params=_cparams("arbitrary"),
        name="route_shared_ple",
    )(x1, p2d, wr_t, rbias, wsg, wsu, wsd, wpg, wpp)


def _plan_sc_kernel(off_hbm, idx_hbm, rank_hbm, gate_hbm, rows_hbm, gates_hbm,
                    off_v, idx_v, rank_v, gate_v, rows_v, gates_v):
    n_tokens = idx_hbm.shape[0] // TOP_K
    worker = lax.axis_index("subcore") * SC_CORES + lax.axis_index("core")

    @pl.when(worker < n_tokens // MOE_TILE)
    def _():
        pltpu.sync_copy(off_hbm.at[pl.ds(worker * N_EXPERTS, N_EXPERTS)], off_v)
        lane = lax.iota(jnp.int32, SC_LANES)
        for k in range(TOP_K):
            row = pl.ds(k * n_tokens + worker * MOE_TILE, MOE_TILE)
            pltpu.sync_copy(idx_hbm.at[row], idx_v)
            pltpu.sync_copy(rank_hbm.at[row], rank_v)
            pltpu.sync_copy(gate_hbm.at[row], gate_v)

            @pl.loop(0, MOE_TILE // SC_LANES)
            def _(i):
                at = i * SC_LANES
                pos = plsc.load_gather(off_v, [idx_v[pl.ds(at, SC_LANES)]]) + rank_v[pl.ds(at, SC_LANES)]
                plsc.store_scatter(rows_v, [pos], (lane + at) * LANE_CHUNKS)
                plsc.store_scatter(gates_v, [pos], gate_v[pl.ds(at, SC_LANES)])

        @pl.loop(0, LIST_PAD // SC_LANES)
        def _(i):
            tail = pl.ds(MOE_TILE * TOP_K + i * SC_LANES, SC_LANES)
            rows_v[tail] = jnp.zeros((SC_LANES,), jnp.int32)
            gates_v[tail] = jnp.zeros((SC_LANES,), F32)

        out = pl.ds(worker * LIST_LEN, LIST_LEN)
        pltpu.sync_copy(rows_v, rows_hbm.at[out])
        pltpu.sync_copy(gates_v, gates_hbm.at[out])


def _plan_sc(off, idx, rank, gate):
    n_tiles = idx.shape[1] // MOE_TILE
    assert n_tiles <= SC_CORES * SC_SUBCORES
    mesh = plsc.VectorSubcoreMesh(core_axis_name="core", subcore_axis_name="subcore",
                                  num_cores=SC_CORES, num_subcores=SC_SUBCORES)
    return pl.kernel(
        _plan_sc_kernel,
        out_type=(jax.ShapeDtypeStruct((n_tiles * LIST_LEN,), jnp.int32),
                  jax.ShapeDtypeStruct((n_tiles * LIST_LEN,), F32)),
        mesh=mesh,
        scratch_types=[pltpu.VMEM((N_EXPERTS,), jnp.int32), pltpu.VMEM((MOE_TILE,), jnp.int32),
                       pltpu.VMEM((MOE_TILE,), jnp.int32), pltpu.VMEM((MOE_TILE,), F32),
                       pltpu.VMEM((LIST_LEN,), jnp.int32), pltpu.VMEM((LIST_LEN,), F32)],
        compiler_params=pltpu.CompilerParams(needs_layout_passes=False),
        name="moe_plan_sc",
    )(off, idx.reshape(-1), rank.reshape(-1), gate.reshape(-1))


def _moe_tile_kernel(cnt_ref, off_ref, wg_ref, wu_ref, wd_ref, lw_ref, lb_ref, rows_hbm, gates_hbm, x_hbm, base_hbm,
                     o_hbm, x_s, acc_s, xg_a, xg_b, xg_c, y_a, y_b, y_c, stage_s, rows_s, gates_s, sem):
    tile, e = pl.program_id(0), pl.program_id(1)
    rows_of = lambda ref, r, n: ref.at[pl.ds(pl.multiple_of(r * LANE_CHUNKS, LANE_CHUNKS), n * LANE_CHUNKS), :]
    tile_rows = pl.ds(pl.multiple_of(tile * (MOE_TILE * LANE_CHUNKS), LANE_CHUNKS), MOE_TILE * LANE_CHUNKS)

    pair = tile * N_EXPERTS + e
    last_pair = pl.num_programs(0) * N_EXPERTS - 1
    n, off = cnt_ref[pair], off_ref[pair]
    tile_at = lambda ref, r: ref.at[pl.ds(pl.multiple_of(r, LANE_CHUNKS), LANE_CHUNKS), :]

    def gather_group(xg, first, jb):
        at = first + jb * GATHER_GROUP
        rows = [tile_at(x_s, rows_s[at + u])[...] for u in range(GATHER_GROUP)]
        rows_of(xg, jb * GATHER_GROUP, GATHER_GROUP)[...] = jnp.concatenate(rows, axis=0)

    def gather_loop(xg, first):
        def body(jb, cc):
            gather_group(xg, first, jb)
            return cc

        lax.fori_loop(0, MOE_CHUNK // GATHER_GROUP, body, 0)

    def swiglu(xg, y):
        xb = _load_row_tiles(xg, MOE_CHUNK).astype(BF16)
        hg = jnp.dot(xb, wg_ref[...], preferred_element_type=F32)
        hu = jnp.dot(xb, wu_ref[...], preferred_element_type=F32)
        act = (hg * jax.nn.sigmoid(hg) * hu).astype(BF16)
        _store_row_tiles(y, jnp.dot(act, wd_ref[...], preferred_element_type=F32), MOE_CHUNK)

    def scatter_group(y, first, j0, live):
        dsts = [rows_s[first + j0 + u] for u in range(live)]
        gates = [gates_s[first + j0 + u] for u in range(live)]
        yv = rows_of(y, j0, live)[...]
        vals = [tile_at(acc_s, d)[...] + g * yv[u * LANE_CHUNKS:(u + 1) * LANE_CHUNKS]
                for u, (d, g) in enumerate(zip(dsts, gates))]
        for d, val in reversed(list(zip(dsts, vals))):
            tile_at(acc_s, d)[...] = val

    def scatter_loop(y, first, m):
        def body(jg, cc):
            scatter_group(y, first, jg * SCATTER_GROUP, SCATTER_GROUP)
            return cc

        lax.fori_loop(0, m // SCATTER_GROUP, body, 0)
        for live in range(1, SCATTER_GROUP):
            @pl.when(m % SCATTER_GROUP == live)
            def _(live=live):
                scatter_group(y, first, m - live, live)

    def plan_loads(t):
        plan = pl.ds(pl.multiple_of(t * LIST_LEN, LIST_PAD), LIST_LEN)
        return (pltpu.make_async_copy(rows_hbm.at[plan], rows_s, sem.at[0]),
                pltpu.make_async_copy(gates_hbm.at[plan], gates_s, sem.at[1]))

    def x_load(t):
        rows = pl.ds(pl.multiple_of(t * (MOE_TILE * LANE_CHUNKS), LANE_CHUNKS), MOE_TILE * LANE_CHUNKS)
        return pltpu.make_async_copy(x_hbm.at[rows, :], x_s, sem.at[2])

    @pl.when(e == 0)
    def _():
        @pl.when(tile == 0)
        def _():
            x_load(tile).start()
            for load in plan_loads(tile):
                load.start()

        load_base = pltpu.make_async_copy(base_hbm.at[tile_rows, :], acc_s, sem.at[3])
        load_base.start()
        y_b[...] = jnp.zeros_like(y_b)
        load_rows, load_gates = plan_loads(tile)
        load_rows.wait()
        x_load(tile).wait()
        gather_loop(xg_a, off)
        load_gates.wait()
        load_base.wait()

    prev_off = off_ref[jnp.maximum(pair - 1, 0)]
    next_off = off_ref[jnp.minimum(pair + 1, last_pair)]
    live_row = lax.broadcasted_iota(jnp.int32, (MOE_CHUNK, 2 * LANES), 0) < n

    def run_expert(xg_cur, y_cur, xg_nxt, y_prv):
        gathers = [functools.partial(gather_group, xg_nxt, next_off, jb) for jb in range(MOE_CHUNK // GATHER_GROUP)]
        scatters = [functools.partial(scatter_group, y_prv, prev_off, jg * SCATTER_GROUP, SCATTER_GROUP)
                    for jg in range(MOE_CHUNK // SCATTER_GROUP)]
        side = [s for both in zip(gathers, scatters) for s in both]
        n_down = LANE_CHUNKS // 2
        cost = [D_MODEL, D_MODEL] + [EXPERT_FF] * n_down
        bounds = [round(len(side) * sum(cost[:i]) / sum(cost)) for i in range(len(cost) + 1)]

        def side_work(i):
            for s in side[bounds[i]:bounds[i + 1]]:
                s()

        xb = _load_row_tiles(xg_cur, MOE_CHUNK).astype(BF16)
        side_work(0)
        hg = jnp.dot(xb, wg_ref[...], preferred_element_type=F32)
        side_work(1)
        hu = jnp.dot(xb, wu_ref[...], preferred_element_type=F32)
        act = (hg * jax.nn.sigmoid(hg) * hu).astype(BF16)
        for q in range(n_down):
            side_work(2 + q)
            out = jnp.dot(act, wd_ref[:, q * 2 * LANES:(q + 1) * 2 * LANES], preferred_element_type=F32)
            out = jnp.where(live_row, out, 0.0)
            for c in range(2):
                y_cur[pl.ds(2 * q + c, MOE_CHUNK, stride=LANE_CHUNKS), :] = out[:, c * LANES:(c + 1) * LANES]

    @pl.when(e % 2 == 0)
    def _():
        run_expert(xg_a, y_a, xg_b, y_b)

    @pl.when(e % 2 == 1)
    def _():
        run_expert(xg_b, y_b, xg_a, y_a)

    def extra_chunk(c, carry):
        first = off + c * MOE_CHUNK
        gather_loop(xg_c, first)
        swiglu(xg_c, y_c)
        scatter_loop(y_c, first, jnp.minimum(MOE_CHUNK, n - c * MOE_CHUNK))
        return carry

    lax.fori_loop(1, (n + MOE_CHUNK - 1) // MOE_CHUNK, extra_chunk, 0)

    @pl.when(e == N_EXPERTS - 1)
    def _():
        more_tiles = tile + 1 < pl.num_programs(0)

        @pl.when(more_tiles)
        def _():
            x_load(tile + 1).start()

        scatter_loop(y_b, off, jnp.minimum(MOE_CHUNK, n))

        @pl.when(more_tiles)
        def _():
            for load in plan_loads(tile + 1):
                load.start()

        n_pieces = MOE_TILE // LN_ROWS
        store = lambda c: pltpu.make_async_copy(
            stage_s.at[c % 2], o_hbm.at[pl.ds(tile * MOE_TILE + c * LN_ROWS, LN_ROWS), :], sem.at[4 + c % 2])
        for c in range(n_pieces):
            z = _load_row_tiles(acc_s, LN_ROWS, c * LN_ROWS * LANE_CHUNKS)
            if c >= 2:
                store(c - 2).wait()
            stage_s[c % 2] = _layer_norm(z, lw_ref[...], lb_ref[...])
            store(c).start()
        store(n_pieces - 2).wait()
        store(n_pieces - 1).wait()


def _moe_tiles(x1, base, plan_rows, plan_gates, cnt, off, wg, wu, wd, ln_w, ln_b):
    T = x1.shape[0] // LANE_CHUNKS
    w_spec = lambda shape: pl.BlockSpec((None,) + shape, lambda i, e, cnt, off: (e, 0, 0))
    vec = pl.BlockSpec((1, D_MODEL), lambda i, e, cnt, off: (0, 0))
    hbm = pl.BlockSpec(memory_space=pl.ANY)
    tile_rows = MOE_TILE * LANE_CHUNKS
    return pl.pallas_call(
        _moe_tile_kernel,
        grid_spec=pltpu.PrefetchScalarGridSpec(
            num_scalar_prefetch=2,
            grid=(T // MOE_TILE, N_EXPERTS),
            in_specs=[w_spec((D_MODEL, EXPERT_FF)), w_spec((D_MODEL, EXPERT_FF)), w_spec((EXPERT_FF, D_MODEL)),
                      vec, vec, hbm, hbm, hbm, hbm],
            out_specs=hbm,
            scratch_shapes=[pltpu.VMEM((tile_rows, LANES), F32),
                            pltpu.VMEM((tile_rows, LANES), F32)]
                           + [pltpu.VMEM((MOE_CHUNK * LANE_CHUNKS, LANES), F32)] * 6
                           + [pltpu.VMEM((2, LN_ROWS, D_MODEL), F32),
                              pltpu.SMEM((LIST_LEN,), jnp.int32),
                              pltpu.SMEM((LIST_LEN,), F32),
                              pltpu.SemaphoreType.DMA((6,))],
        ),
        out_shape=jax.ShapeDtypeStruct((T, D_MODEL), F32),
        compiler_params=_cparams("arbitrary", "arbitrary"),
        name="moe_tiles_ln2",
    )(cnt, off, wg, wu, wd, ln_w, ln_b, plan_rows, plan_gates, x1, base)


def kernel(x, p, w_in, hgrn_lb_logits, hgrn_norm_w, w_branch_att, w_branch_hgrn, w_out, ln1_w, ln1_b, router_w, router_bias, expert_w_gate, expert_w_up, expert_w_down, shared_w_gate, shared_w_up, shared_w_down, ple_gate_w, ple_proj_w, ln2_w, ln2_b):
    B, S, D = x.shape
    T = B * S
    l = 0
    x2d = x.reshape(T, D)
    bf = lambda a: a.astype(BF16)

    ws = _att_weights(w_in[l])
    qkv = [_proj_att(x2d, ws[g], d) for g, d in enumerate(ATT_DILATIONS)]
    y_att = _attention(qkv, B, S)
    u_hg = _proj(x2d, bf(w_in[l][:, 3 * len(ATT_DILATIONS) * ATT_WIDTH:]), PROJ_COLS)
    y_hg = _hgrn(u_hg, hgrn_lb_logits, hgrn_norm_w[l:l + 1], B, S)
    x1 = _merge(y_att, y_hg, u_hg, x2d, bf(w_branch_att[l]), bf(w_branch_hgrn[l]), bf(w_out[l]),
                ln1_w[l:l + 1], ln1_b[l:l + 1])

    base, idx, gate, rank, counts = _route(
        x1, p[l].reshape(T, PLE_DIM), router_w[l].T, router_bias[l].reshape(N_EXPERTS, 1),
        bf(shared_w_gate[l]), bf(shared_w_up[l]), bf(shared_w_down[l]), bf(ple_gate_w[l]), bf(ple_proj_w[l]))
    cnt = counts[:, :, 0]
    off = jnp.cumsum(cnt, axis=1) - cnt
    cnt, off = cnt.reshape(-1), off.reshape(-1)
    plan_rows, plan_gates = _plan_sc(off, idx, rank, gate)
    out = _moe_tiles(x1, base, plan_rows, plan_gates, cnt, off, bf(expert_w_gate[l]), bf(expert_w_up[l]),
                     bf(expert_w_down[l]), ln2_w[l:l + 1], ln2_b[l:l + 1])
    return out.reshape(B, S, D)
```

```python
import functools

import jax
import jax.numpy as jnp
import numpy as np
from jax import lax
from jax.experimental import pallas as pl
from jax.experimental.pallas import tpu as pltpu
from jax.experimental.pallas import tpu_sc as plsc

F32 = jnp.float32
BF16 = jnp.bfloat16

D_MODEL = 1024
ATT_HEAD_DIM = 64
ATT_HEADS = 8
ATT_DILATIONS = (1, 4, 16)
ATT_BLOCK = 128
ATT_WIDTH = ATT_HEADS * ATT_HEAD_DIM
ATT_TILE = ATT_BLOCK * max(ATT_DILATIONS)
NEG_INF = -1e30
LOG2_E = 1.4426950408889634

HG_HEADS = 8
HG_DIM = 128
HG_WIDTH = HG_HEADS * HG_DIM
HG_CHUNK = 32
HG_TILE = 256
RMS_EPS = 1e-6

N_EXPERTS = 64
TOP_K = 8
N_GROUPS = 8
GROUP_SIZE = N_EXPERTS // N_GROUPS
TOPK_GROUPS = 4
EXPERT_FF = 256
ROUTED_SCALE = 2.5
PLE_DIM = 256
LN_EPS = 1e-5
DEPTH = 1
DEEPNORM_ALPHA = (2.0 * DEPTH) ** 0.25

LANES = 128
LANE_CHUNKS = D_MODEL // LANES
PROJ_ROWS = 512
PROJ_COLS = 1536
ATT_PROJ_ROWS = 1024
MIX_ROWS = 512
MOE_TILE = 4096
MOE_CHUNK = 576
LN_ROWS = 256
LIST_PAD = 1024
LIST_LEN = MOE_TILE * TOP_K + LIST_PAD
GATHER_GROUP = 8
SCATTER_GROUP = 8
V7X_VMEM_LIMIT = 56 * 1024 * 1024
SC_CORES, SC_SUBCORES, SC_LANES = 2, 16, 16


def _cparams(*sem):
    return pltpu.CompilerParams(dimension_semantics=sem, vmem_limit_bytes=V7X_VMEM_LIMIT)


def _proj_att_kernel(*refs, dil):
    x_refs, w_ref, o_ref = refs[:LANE_CHUNKS], refs[LANE_CHUNKS], refs[LANE_CHUNKS + 1]
    n = ATT_PROJ_ROWS // dil

    def rows(ref):
        if dil == 1:
            return ref[...]
        return jnp.concatenate([ref[pl.ds(r, n, stride=dil), :] for r in range(dil)], axis=0)

    xp = jnp.concatenate([rows(ref).astype(BF16) for ref in x_refs], axis=1)
    y = jnp.dot(xp, w_ref[...], preferred_element_type=F32)
    o_ref[...] = y.astype(BF16).reshape(dil, n, 3 * ATT_WIDTH)


def _proj_att(x2d, w, dil):
    T = x2d.shape[0]
    per = ATT_TILE // ATT_PROJ_ROWS
    n = ATT_PROJ_ROWS // dil
    out = pl.pallas_call(
        functools.partial(_proj_att_kernel, dil=dil),
        grid=(T // ATT_PROJ_ROWS,),
        in_specs=[pl.BlockSpec((ATT_PROJ_ROWS, LANES), functools.partial(lambda i, c: (i, c), c=c))
                  for c in range(LANE_CHUNKS)]
                 + [pl.BlockSpec((D_MODEL, 3 * ATT_WIDTH), lambda i: (0, 0))],
        out_specs=pl.BlockSpec((None, dil, None, n, 3 * ATT_WIDTH), lambda i: (i // per, 0, i % per, 0, 0)),
        out_shape=jax.ShapeDtypeStruct((T // ATT_TILE, dil, per, n, 3 * ATT_WIDTH), BF16),
        compiler_params=_cparams("parallel"),
        name=f"proj_att_d{dil}",
    )(*([x2d] * LANE_CHUNKS), w)
    return out.reshape(T // ATT_TILE, dil, ATT_TILE // dil, 3 * ATT_WIDTH)


def _att_pair(q2, kp, kc, vp, vc, bias_ref, g, first):
    def head0_lanes(rows, dtype):
        lane = lax.broadcasted_iota(jnp.int32, (rows, 2 * ATT_HEAD_DIM), 1)
        return lane.astype(F32).astype(dtype) < ATT_HEAD_DIM

    lo_q = head0_lanes(ATT_BLOCK, BF16)
    lo_v = head0_lanes(2 * ATT_BLOCK, BF16)
    k2 = jnp.concatenate([kp, kc], axis=0)
    v2 = jnp.concatenate([vp, vc], axis=0)
    zero = jnp.zeros_like(q2)
    ps, ms = [], []
    for hh in range(2):
        qm = jnp.where(lo_q, q2, zero) if hh == 0 else jnp.where(lo_q, zero, q2)
        s = lax.dot_general(qm, k2, (((1,), (1,)), ((), ())), preferred_element_type=F32)
        s = s + bias_ref[g, hh, first]
        m = jnp.max(s, axis=-1, keepdims=True)
        ps.append(jnp.exp2(s - m).astype(BF16))
        ms.append(m)
    pcat = jnp.concatenate(ps, axis=1)
    zero_v, one_v = jnp.zeros_like(v2), jnp.ones_like(v2)
    rhs = jnp.concatenate([
        jnp.concatenate([jnp.where(lo_v, v2, zero_v), jnp.where(lo_v, one_v, zero_v)], axis=1),
        jnp.concatenate([jnp.where(lo_v, zero_v, v2), jnp.where(lo_v, zero_v, one_v)], axis=1)], axis=0)
    nd = jnp.dot(pcat, rhs, preferred_element_type=F32)
    m2 = jnp.where(head0_lanes(ATT_BLOCK, F32), ms[0], ms[1])
    return nd[:, :2 * ATT_HEAD_DIM], m2, nd[:, 2 * ATT_HEAD_DIM:]


def _att_kernel(*refs):
    (q0, kc0, vc0, kp0, vp0, q1, kc1, vc1, kp1, vp1, q2, kc2, vc2, kp2, vp2,
     bias_ref, o_ref) = refs[:17]
    ng = len(ATT_DILATIONS)
    num_s, m_s, den_s = refs[17:17 + ng], refs[17 + ng:17 + 2 * ng], refs[17 + 2 * ng:]
    first_tile = (pl.program_id(2) == 0).astype(jnp.int32)
    groups = ((q0, kc0, vc0, kp0, vp0), (q1, kc1, vc1, kp1, vp1), (q2, kc2, vc2, kp2, vp2))
    for g, dil in enumerate(ATT_DILATIONS):
        q_ref, kc_ref, vc_ref, kp_ref, vp_ref = groups[g]
        nb = ATT_TILE // dil // ATT_BLOCK
        for r in range(dil):
            for n in range(nb):
                rows = pl.ds(n * ATT_BLOCK, ATT_BLOCK)
                if n == 0:
                    prev = pl.ds((nb - 1) * ATT_BLOCK, ATT_BLOCK)
                    kp, vp, first = kp_ref[r, prev, :], vp_ref[r, prev, :], first_tile
                else:
                    prev = pl.ds((n - 1) * ATT_BLOCK, ATT_BLOCK)
                    kp, vp, first = kc_ref[r, prev, :], vc_ref[r, prev, :], 0
                num, m, den = _att_pair(q_ref[r, rows, :], kp, kc_ref[r, rows, :], vp, vc_ref[r, rows, :],
                                        bias_ref, g, first)
                if dil == 1:
                    dst = rows
                else:
                    dst = pl.ds(n * ATT_BLOCK * dil + r, ATT_BLOCK, stride=dil)
                num_s[g][dst, :] = num
                m_s[g][dst, :] = m
                den_s[g][dst, :] = den
    m_all = jnp.maximum(jnp.maximum(m_s[0][...], m_s[1][...]), m_s[2][...])
    num = jnp.zeros((ATT_TILE, 2 * ATT_HEAD_DIM), F32)
    den = jnp.zeros((ATT_TILE, 2 * ATT_HEAD_DIM), F32)
    for g in range(ng):
        sc = jnp.exp2(m_s[g][...] - m_all)
        num = num + sc * num_s[g][...]
        den = den + sc * den_s[g][...]
    o_ref[...] = (num / den).astype(o_ref.dtype)


def _att_bias_table():
    qi = np.arange(ATT_BLOCK)[:, None]
    ki = np.arange(2 * ATT_BLOCK)[None, :]
    steps = qi + ATT_BLOCK - ki
    valid = (steps >= 0) & (steps <= ATT_BLOCK)
    slopes = np.array([2.0 ** (-8.0 * (h + 1) / ATT_HEADS) for h in range(ATT_HEADS)], np.float32)
    tab = np.empty((len(ATT_DILATIONS), ATT_HEADS, 2, ATT_BLOCK, 2 * ATT_BLOCK), np.float32)
    for g, dil in enumerate(ATT_DILATIONS):
        bias = -slopes[:, None, None] * (steps * dil).astype(np.float32)[None] * LOG2_E
        tab[g, :, 0] = np.where(valid[None], bias, NEG_INF)
        tab[g, :, 1] = np.where((valid & (ki >= ATT_BLOCK))[None], bias, NEG_INF)
    return jnp.asarray(tab)


def _attention(qkv, B, S):
    tiles = S // ATT_TILE
    pair = 2 * ATT_HEAD_DIM
    npair = ATT_WIDTH // pair
    in_specs, args = [], []
    for g, dil in enumerate(ATT_DILATIONS):
        blk = (None, dil, ATT_TILE // dil, pair)
        cur = lambda b, hp, t, off: (b * tiles + t, 0, 0, off * npair + hp)
        prv = lambda b, hp, t, off: (b * tiles + jnp.maximum(t - 1, 0), 0, 0, off * npair + hp)
        in_specs += [pl.BlockSpec(blk, functools.partial(cur, off=0)),
                     pl.BlockSpec(blk, functools.partial(cur, off=1)),
                     pl.BlockSpec(blk, functools.partial(cur, off=2)),
                     pl.BlockSpec(blk, functools.partial(prv, off=1)),
                     pl.BlockSpec(blk, functools.partial(prv, off=2))]
        args += [qkv[g]] * 5
    in_specs.append(pl.BlockSpec((len(ATT_DILATIONS), 2, 2, ATT_BLOCK, 2 * ATT_BLOCK),
                                 lambda b, hp, t: (0, hp, 0, 0, 0)))
    args.append(_att_bias_table())
    scratch = [pltpu.VMEM((ATT_TILE, pair), F32) for _ in range(3 * len(ATT_DILATIONS))]
    return pl.pallas_call(
        _att_kernel,
        grid=(B, npair, tiles),
        in_specs=in_specs,
        out_specs=pl.BlockSpec((ATT_TILE, pair), lambda b, hp, t: (b * tiles + t, hp)),
        out_shape=jax.ShapeDtypeStruct((B * S, ATT_WIDTH), BF16),
        scratch_shapes=scratch,
        compiler_params=_cparams("parallel", "parallel", "arbitrary"),
        name="dilated_attention",
    )(*args)


def _att_weights(w_in_l):
    out = []
    width = len(ATT_DILATIONS) * ATT_WIDTH
    for g in range(len(ATT_DILATIONS)):
        cols = [w_in_l[:, part * width + g * ATT_WIDTH: part * width + (g + 1) * ATT_WIDTH] for part in range(3)]
        cols[0] = cols[0] * (ATT_HEAD_DIM ** -0.5 * LOG2_E)
        out.append(jnp.concatenate(cols, axis=1).astype(BF16))
    return out


def _proj_kernel(x_ref, w_ref, o_ref, *, col_tile):
    xb = x_ref[...].astype(BF16)
    for c in range(w_ref.shape[1] // col_tile):
        cols = slice(c * col_tile, (c + 1) * col_tile)
        o_ref[:, cols] = jnp.dot(xb, w_ref[:, cols], preferred_element_type=F32).astype(o_ref.dtype)


def _proj(x2d, w, col_tile):
    T, N = x2d.shape[0], w.shape[1]
    return pl.pallas_call(
        functools.partial(_proj_kernel, col_tile=col_tile),
        grid=(T // PROJ_ROWS,),
        in_specs=[pl.BlockSpec((PROJ_ROWS, D_MODEL), lambda i: (i, 0)),
                  pl.BlockSpec((D_MODEL, N), lambda i: (0, 0))],
        out_specs=pl.BlockSpec((PROJ_ROWS, N), lambda i: (i, 0)),
        out_shape=jax.ShapeDtypeStruct((T, N), BF16),
        compiler_params=_cparams("parallel"),
        name="proj_hgrn_gates",
    )(x2d, w)


def _split2(v):
    a = v.astype(BF16)
    return a, (v - a.astype(F32)).astype(BF16)


def _hgrn_kernel(q_ref, f_ref, i_ref, g_ref, lbl_ref, gain_ref, o_ref, state_ref):
    @pl.when(pl.program_id(1) == 0)
    def _():
        state_ref[...] = jnp.zeros_like(state_ref)

    lbl = lbl_ref[...]
    e = jnp.exp(lbl - jnp.max(lbl, axis=0, keepdims=True))
    lb = e[0:1] / jnp.sum(e, axis=0, keepdims=True)
    forget = lb + (1.0 - lb) * jax.nn.sigmoid(f_ref[...].astype(F32))
    log_f = jnp.log(forget)
    key = 1.0 - forget

    row = lax.broadcasted_iota(jnp.int32, (HG_TILE, HG_TILE), 0)
    col = lax.broadcasted_iota(jnp.int32, (HG_TILE, HG_TILE), 1)
    causal = (row >= col) & ((row // HG_CHUNK) == (col // HG_CHUNK))
    tri = jnp.where(causal, 1.0, 0.0).astype(BF16)
    b = sum(jnp.dot(tri, t, preferred_element_type=F32) for t in _split2(log_f))
    eb = jnp.exp(b)
    q_dec = (q_ref[...].astype(F32) * eb).astype(BF16)
    k_inv = key * jnp.exp(-b)
    xi = i_ref[...].astype(F32)
    val = (xi * jax.nn.sigmoid(xi)).astype(BF16)
    k_inv_b = k_inv.astype(BF16)

    n_chunks = HG_TILE // HG_CHUNK
    last_rows = [eb[(c + 1) * HG_CHUNK - 1:(c + 1) * HG_CHUNK, :] for c in range(n_chunks)]
    dec_rows = jnp.concatenate([jnp.broadcast_to(r, (HG_CHUNK, HG_WIDTH)) for r in last_rows], axis=0)
    k_end = (k_inv * dec_rows).astype(BF16)
    def per_chunk_columns(t):
        blocks = []
        for c in range(n_chunks):
            rows_above, rows_below = c * HG_CHUNK, HG_TILE - (c + 1) * HG_CHUNK
            parts = [t[rows_above:rows_above + HG_CHUNK]]
            if rows_above:
                parts.insert(0, jnp.zeros((rows_above, HG_DIM), t.dtype))
            if rows_below:
                parts.append(jnp.zeros((rows_below, HG_DIM), t.dtype))
            blocks.append(jnp.concatenate(parts, axis=0))
        return jnp.concatenate(blocks, axis=1)

    outs = []
    for h in range(HG_HEADS):
        cols = slice(h * HG_DIM, (h + 1) * HG_DIM)
        qd, ki, vv = q_dec[:, cols], k_inv_b[:, cols], val[:, cols]
        a = lax.dot_general(qd, ki, (((1,), (1,)), ((), ())), preferred_element_type=F32)
        a = jnp.where(causal, a, 0.0).astype(BF16)
        o_intra = jnp.dot(a, vv, preferred_element_type=F32)
        upd = lax.dot_general(vv, per_chunk_columns(k_end[:, cols]), (((0,), (0,)), ((), ())),
                              preferred_element_type=F32)
        st = state_ref[h]
        entering = []
        for c in range(n_chunks):
            entering.append(st.astype(BF16))
            st = st * last_rows[c][:, cols] + upd[:, c * HG_DIM:(c + 1) * HG_DIM]
        state_ref[h] = st
        o_inter = lax.dot_general(per_chunk_columns(qd), jnp.concatenate(entering, axis=1),
                                  (((1,), (1,)), ((), ())), preferred_element_type=F32)
        o = o_intra + o_inter
        o = o * lax.rsqrt(jnp.mean(jnp.square(o), axis=-1, keepdims=True) + RMS_EPS)
        outs.append(o)
    o = jnp.concatenate(outs, axis=1) * gain_ref[...]
    gg = g_ref[...].astype(F32)
    o_ref[...] = (o * (gg * jax.nn.sigmoid(gg))).astype(o_ref.dtype)


def _hgrn(u_hg, lb_logits, gain, B, S):
    tiles = S // HG_TILE
    col = lambda j: pl.BlockSpec((HG_TILE, HG_WIDTH), functools.partial(lambda b, t, j: (b * tiles + t, j), j=j))
    return pl.pallas_call(
        _hgrn_kernel,
        grid=(B, tiles),
        in_specs=[col(0), col(1), col(2), col(3),
                  pl.BlockSpec((2, HG_WIDTH), lambda b, t: (0, 0)),
                  pl.BlockSpec((1, HG_WIDTH), lambda b, t: (0, 0))],
        out_specs=pl.BlockSpec((HG_TILE, HG_WIDTH), lambda b, t: (b * tiles + t, 0)),
        out_shape=jax.ShapeDtypeStruct((B * S, HG_WIDTH), BF16),
        scratch_shapes=[pltpu.VMEM((HG_HEADS, HG_DIM, HG_DIM), F32)],
        compiler_params=_cparams("parallel", "arbitrary"),
        name="hgrn2",
    )(u_hg, u_hg, u_hg, u_hg, lb_logits, gain)


def _load_row_tiles(ref, n, start=0):
    return jnp.concatenate([ref[pl.ds(start + c, n, stride=LANE_CHUNKS), :] for c in range(LANE_CHUNKS)], axis=1)


def _store_row_tiles(ref, val, n):
    for c in range(LANE_CHUNKS):
        ref[pl.ds(c, n, stride=LANE_CHUNKS), :] = val[:, c * LANES:(c + 1) * LANES]


def _layer_norm(z, w, b):
    mu = jnp.mean(z, axis=-1, keepdims=True)
    zc = z - mu
    var = jnp.mean(jnp.square(zc), axis=-1, keepdims=True)
    return zc * lax.rsqrt(var + LN_EPS) * w + b


def _merge_kernel(ya_ref, yh_ref, ga_ref, gh_ref, x_ref, wa_ref, wh_ref, wo_ref, lw_ref, lb_ref, o_ref):
    ma = jnp.dot(ya_ref[...], wa_ref[...], preferred_element_type=F32)
    mh = jnp.dot(yh_ref[...], wh_ref[...], preferred_element_type=F32)
    merged = (jax.nn.sigmoid(ga_ref[...].astype(F32)) * ma + jax.nn.sigmoid(gh_ref[...].astype(F32)) * mh)
    z = DEEPNORM_ALPHA * x_ref[...] + jnp.dot(merged.astype(BF16), wo_ref[...], preferred_element_type=F32)
    _store_row_tiles(o_ref, _layer_norm(z, lw_ref[...], lb_ref[...]), MIX_ROWS)


def _merge(y_att, y_hg, u_hg, x2d, w_a, w_h, w_o, ln_w, ln_b):
    T = x2d.shape[0]
    rows = lambda width, j=0: pl.BlockSpec((MIX_ROWS, width), functools.partial(lambda i, j: (i, j), j=j))
    full = lambda a: pl.BlockSpec(a.shape, lambda i: (0, 0))
    return pl.pallas_call(
        _merge_kernel,
        grid=(T // MIX_ROWS,),
        in_specs=[rows(ATT_WIDTH), rows(HG_WIDTH), rows(D_MODEL, 4), rows(D_MODEL, 5), rows(D_MODEL),
                  full(w_a), full(w_h), full(w_o), full(ln_w), full(ln_b)],
        out_specs=pl.BlockSpec((MIX_ROWS * LANE_CHUNKS, LANES), lambda i: (i, 0)),
        out_shape=jax.ShapeDtypeStruct((T * LANE_CHUNKS, LANES), F32),
        compiler_params=_cparams("parallel"),
        name="merge_ln1",
    )(y_att, y_hg, u_hg, u_hg, x2d, w_a, w_h, w_o, ln_w, ln_b)


def _first_argmax(v, ids, n):
    mx = jnp.max(v, axis=0, keepdims=True)
    return mx, jnp.min(jnp.where(v == mx, ids, n), axis=0, keepdims=True)


def _route_kernel(x1_ref, p_ref, wrt_ref, rb_ref, wsg_ref, wsu_ref, wsd_ref, wpg_ref, wpp_ref,
                  base_ref, idx_ref, gate_ref, rank_ref, cnt_ref, carry_ref):
    @pl.when(pl.program_id(0) % (MOE_TILE // MIX_ROWS) == 0)
    def _():
        carry_ref[...] = jnp.zeros_like(carry_ref)

    x1 = _load_row_tiles(x1_ref, MIX_ROWS)
    x1b = x1.astype(BF16)
    logits = lax.dot_general(wrt_ref[...], x1, (((1,), (1,)), ((), ())), preferred_element_type=F32,
                             precision=lax.Precision.HIGHEST)
    s = jax.nn.sigmoid(logits)
    sel = s + rb_ref[...]
    eid = lax.broadcasted_iota(jnp.int32, (N_EXPERTS, MIX_ROWS), 0)
    neg = -jnp.inf

    grp = sel.reshape(N_GROUPS, GROUP_SIZE, MIX_ROWS)
    mid = lax.broadcasted_iota(jnp.int32, grp.shape, 1)
    m1 = jnp.max(grp, axis=1, keepdims=True)
    i1 = jnp.min(jnp.where(grp == m1, mid, GROUP_SIZE), axis=1, keepdims=True)
    m2 = jnp.max(jnp.where(mid == i1, neg, grp), axis=1, keepdims=True)
    gscore = (m1 + m2).reshape(N_GROUPS, MIX_ROWS)
    gid = lax.broadcasted_iota(jnp.int32, (N_GROUPS, MIX_ROWS), 0)
    gsel = jnp.zeros((N_GROUPS, MIX_ROWS), jnp.bool_)
    for _ in range(TOPK_GROUPS):
        _, gi = _first_argmax(gscore, gid, N_GROUPS)
        hit = gid == gi
        gsel = gsel | hit
        gscore = jnp.where(hit, neg, gscore)
    emask = jnp.broadcast_to(gsel.reshape(N_GROUPS, 1, MIX_ROWS), grp.shape).reshape(N_EXPERTS, MIX_ROWS)
    cand = jnp.where(emask, sel, neg)

    idxs, gates = [], []
    chosen = jnp.zeros((N_EXPERTS, MIX_ROWS), jnp.bool_)
    for _ in range(TOP_K):
        _, ei = _first_argmax(cand, eid, N_EXPERTS)
        hit = eid == ei
        idxs.append(ei)
        gates.append(jnp.sum(jnp.where(hit, s, 0.0), axis=0, keepdims=True))
        chosen = chosen | hit
        cand = jnp.where(hit, neg, cand)
    g = jnp.concatenate(gates, axis=0)
    g = g / jnp.sum(g, axis=0, keepdims=True) * ROUTED_SCALE
    idx_ref[...] = jnp.concatenate(idxs, axis=0)
    gate_ref[...] = g

    onehot = jnp.where(chosen, 1.0, 0.0)
    tr = lax.broadcasted_iota(jnp.int32, (MIX_ROWS, MIX_ROWS), 0)
    tc = lax.broadcasted_iota(jnp.int32, (MIX_ROWS, MIX_ROWS), 1)
    before = jnp.where(tr < tc, 1.0, 0.0).astype(BF16)
    prefix = jnp.dot(onehot.astype(BF16), before, preferred_element_type=F32)
    rankfull = (carry_ref[:, 0:1] + prefix).astype(jnp.int32)
    rank_ref[...] = jnp.concatenate(
        [jnp.sum(jnp.where(eid == ei, rankfull, 0), axis=0, keepdims=True) for ei in idxs], axis=0)
    total = carry_ref[...] + jnp.sum(onehot, axis=1, keepdims=True)
    carry_ref[...] = total
    cnt_ref[...] = total.astype(jnp.int32)

    hg = jnp.dot(x1b, wsg_ref[...], preferred_element_type=F32)
    hu = jnp.dot(x1b, wsu_ref[...], preferred_element_type=F32)
    shared = jnp.dot((hg * jax.nn.sigmoid(hg) * hu).astype(BF16), wsd_ref[...], preferred_element_type=F32)
    ple = (jax.nn.sigmoid(jnp.dot(x1b, wpg_ref[...], preferred_element_type=F32))
           * jnp.dot(p_ref[...].astype(BF16), wpp_ref[...], preferred_element_type=F32))
    _store_row_tiles(base_ref, DEEPNORM_ALPHA * x1 + shared + ple, MIX_ROWS)


def _route(x1, p2d, wr_t, rbias, wsg, wsu, wsd, wpg, wpp):
    T = x1.shape[0] // LANE_CHUNKS
    per_tile = MOE_TILE // MIX_ROWS
    full = lambda a: pl.BlockSpec(a.shape, lambda i: (0, 0))
    tok = pl.BlockSpec((TOP_K, MIX_ROWS), lambda i: (0, i))
    row_tiles = pl.BlockSpec((MIX_ROWS * LANE_CHUNKS, LANES), lambda i: (i, 0))
    return pl.pallas_call(
        _route_kernel,
        grid=(T // MIX_ROWS,),
        in_specs=[row_tiles,
                  pl.BlockSpec((MIX_ROWS, PLE_DIM), lambda i: (i, 0)),
                  full(wr_t), full(rbias), full(wsg), full(wsu), full(wsd), full(wpg), full(wpp)],
        out_specs=[row_tiles, tok, tok, tok,
                   pl.BlockSpec((None, N_EXPERTS, LANES), lambda i: (i // per_tile, 0, 0))],
        out_shape=[jax.ShapeDtypeStruct((T * LANE_CHUNKS, LANES), F32),
                   jax.ShapeDtypeStruct((TOP_K, T), jnp.int32),
                   jax.ShapeDtypeStruct((TOP_K, T), F32),
                   jax.ShapeDtypeStruct((TOP_K, T), jnp.int32),
                   jax.ShapeDtypeStruct((T // MOE_TILE, N_EXPERTS, LANES), jnp.int32)],
        scratch_shapes=[pltpu.VMEM((N_EXPERTS, LANES), F32)],
        compiler_params=_cparams("arbitrary"),
        name="route_shared_ple",
    )(x1, p2d, wr_t, rbias, wsg, wsu, wsd, wpg, wpp)


def _plan_sc_kernel(off_hbm, idx_hbm, rank_hbm, gate_hbm, rows_hbm, gates_hbm,
                    off_v, idx_v, rank_v, gate_v, rows_v, gates_v):
    n_tokens = idx_hbm.shape[0] // TOP_K
    worker = lax.axis_index("subcore") * SC_CORES + lax.axis_index("core")

    @pl.when(worker < n_tokens // MOE_TILE)
    def _():
        pltpu.sync_copy(off_hbm.at[pl.ds(worker * N_EXPERTS, N_EXPERTS)], off_v)
        lane = lax.iota(jnp.int32, SC_LANES)
        for k in range(TOP_K):
            row = pl.ds(k * n_tokens + worker * MOE_TILE, MOE_TILE)
            pltpu.sync_copy(idx_hbm.at[row], idx_v)
            pltpu.sync_copy(rank_hbm.at[row], rank_v)
            pltpu.sync_copy(gate_hbm.at[row], gate_v)

            @pl.loop(0, MOE_TILE // SC_LANES)
            def _(i):
                at = i * SC_LANES
                pos = plsc.load_gather(off_v, [idx_v[pl.ds(at, SC_LANES)]]) + rank_v[pl.ds(at, SC_LANES)]
                plsc.store_scatter(rows_v, [pos], (lane + at) * LANE_CHUNKS)
                plsc.store_scatter(gates_v, [pos], gate_v[pl.ds(at, SC_LANES)])

        @pl.loop(0, LIST_PAD // SC_LANES)
        def _(i):
            tail = pl.ds(MOE_TILE * TOP_K + i * SC_LANES, SC_LANES)
            rows_v[tail] = jnp.zeros((SC_LANES,), jnp.int32)
            gates_v[tail] = jnp.zeros((SC_LANES,), F32)

        out = pl.ds(worker * LIST_LEN, LIST_LEN)
        pltpu.sync_copy(rows_v, rows_hbm.at[out])
        pltpu.sync_copy(gates_v, gates_hbm.at[out])


def _plan_sc(off, idx, rank, gate):
    n_tiles = idx.shape[1] // MOE_TILE
    assert n_tiles <= SC_CORES * SC_SUBCORES
    mesh = plsc.VectorSubcoreMesh(core_axis_name="core", subcore_axis_name="subcore",
                                  num_cores=SC_CORES, num_subcores=SC_SUBCORES)
    return pl.kernel(
        _plan_sc_kernel,
        out_type=(jax.ShapeDtypeStruct((n_tiles * LIST_LEN,), jnp.int32),
                  jax.ShapeDtypeStruct((n_tiles * LIST_LEN,), F32)),
        mesh=mesh,
        scratch_types=[pltpu.VMEM((N_EXPERTS,), jnp.int32), pltpu.VMEM((MOE_TILE,), jnp.int32),
                       pltpu.VMEM((MOE_TILE,), jnp.int32), pltpu.VMEM((MOE_TILE,), F32),
                       pltpu.VMEM((LIST_LEN,), jnp.int32), pltpu.VMEM((LIST_LEN,), F32)],
        compiler_params=pltpu.CompilerParams(needs_layout_passes=False),
        name="moe_plan_sc",
    )(off, idx.reshape(-1), rank.reshape(-1), gate.reshape(-1))


def _moe_tile_kernel(cnt_ref, off_ref, wg_ref, wu_ref, wd_ref, lw_ref, lb_ref, rows_hbm, gates_hbm, x_hbm, base_hbm,
                     o_hbm, x_s, acc_s, xg_a, xg_b, xg_c, y_a, y_b, y_c, stage_s, rows_s, gates_s, sem):
    tile, e = pl.program_id(0), pl.program_id(1)
    rows_of = lambda ref, r, n: ref.at[pl.ds(pl.multiple_of(r * LANE_CHUNKS, LANE_CHUNKS), n * LANE_CHUNKS), :]
    tile_rows = pl.ds(pl.multiple_of(tile * (MOE_TILE * LANE_CHUNKS), LANE_CHUNKS), MOE_TILE * LANE_CHUNKS)

    pair = tile * N_EXPERTS + e
    last_pair = pl.num_programs(0) * N_EXPERTS - 1
    n, off = cnt_ref[pair], off_ref[pair]
    tile_at = lambda ref, r: ref.at[pl.ds(pl.multiple_of(r, LANE_CHUNKS), LANE_CHUNKS), :]

    def gather_group(xg, first, jb):
        at = first + jb * GATHER_GROUP
        rows = [tile_at(x_s, rows_s[at + u])[...] for u in range(GATHER_GROUP)]
        rows_of(xg, jb * GATHER_GROUP, GATHER_GROUP)[...] = jnp.concatenate(rows, axis=0)

    def gather_loop(xg, first):
        def body(jb, cc):
            gather_group(xg, first, jb)
            return cc

        lax.fori_loop(0, MOE_CHUNK // GATHER_GROUP, body, 0)

    def swiglu(xg, y):
        xb = _load_row_tiles(xg, MOE_CHUNK).astype(BF16)
        hg = jnp.dot(xb, wg_ref[...], preferred_element_type=F32)
        hu = jnp.dot(xb, wu_ref[...], preferred_element_type=F32)
        act = (hg * jax.nn.sigmoid(hg) * hu).astype(BF16)
        _store_row_tiles(y, jnp.dot(act, wd_ref[...], preferred_element_type=F32), MOE_CHUNK)

    def scatter_group(y, first, j0, live):
        dsts = [rows_s[first + j0 + u] for u in range(live)]
        gates = [gates_s[first + j0 + u] for u in range(live)]
        yv = rows_of(y, j0, live)[...]
        vals = [tile_at(acc_s, d)[...] + g * yv[u * LANE_CHUNKS:(u + 1) * LANE_CHUNKS]
                for u, (d, g) in enumerate(zip(dsts, gates))]
        for d, val in reversed(list(zip(dsts, vals))):
            tile_at(acc_s, d)[...] = val

    def scatter_loop(y, first, m):
        def body(jg, cc):
            scatter_group(y, first, jg * SCATTER_GROUP, SCATTER_GROUP)
            return cc

        lax.fori_loop(0, m // SCATTER_GROUP, body, 0)
        for live in range(1, SCATTER_GROUP):
            @pl.when(m % SCATTER_GROUP == live)
            def _(live=live):
                scatter_group(y, first, m - live, live)

    def plan_loads(t):
        plan = pl.ds(pl.multiple_of(t * LIST_LEN, LIST_PAD), LIST_LEN)
        return (pltpu.make_async_copy(rows_hbm.at[plan], rows_s, sem.at[0]),
                pltpu.make_async_copy(gates_hbm.at[plan], gates_s, sem.at[1]))

    def x_load(t):
        rows = pl.ds(pl.multiple_of(t * (MOE_TILE * LANE_CHUNKS), LANE_CHUNKS), MOE_TILE * LANE_CHUNKS)
        return pltpu.make_async_copy(x_hbm.at[rows, :], x_s, sem.at[2])

    @pl.when(e == 0)
    def _():
        @pl.when(tile == 0)
        def _():
            x_load(tile).start()
            for load in plan_loads(tile):
                load.start()

        load_base = pltpu.make_async_copy(base_hbm.at[tile_rows, :], acc_s, sem.at[3])
        load_base.start()
        y_b[...] = jnp.zeros_like(y_b)
        load_rows, load_gates = plan_loads(tile)
        load_rows.wait()
        x_load(tile).wait()
        gather_loop(xg_a, off)
        load_gates.wait()
        load_base.wait()

    prev_off = off_ref[jnp.maximum(pair - 1, 0)]
    next_off = off_ref[jnp.minimum(pair + 1, last_pair)]
    live_row = lax.broadcasted_iota(jnp.int32, (MOE_CHUNK, 2 * LANES), 0) < n

    def run_expert(xg_cur, y_cur, xg_nxt, y_prv):
        gathers = [functools.partial(gather_group, xg_nxt, next_off, jb) for jb in range(MOE_CHUNK // GATHER_GROUP)]
        scatters = [functools.partial(scatter_group, y_prv, prev_off, jg * SCATTER_GROUP, SCATTER_GROUP)
                    for jg in range(MOE_CHUNK // SCATTER_GROUP)]
        side = [s for both in zip(gathers, scatters) for s in both]
        n_down = LANE_CHUNKS // 2
        cost = [D_MODEL, D_MODEL] + [EXPERT_FF] * n_down
        bounds = [round(len(side) * sum(cost[:i]) / sum(cost)) for i in range(len(cost) + 1)]

        def side_work(i):
            for s in side[bounds[i]:bounds[i + 1]]:
                s()

        xb = _load_row_tiles(xg_cur, MOE_CHUNK).astype(BF16)
        side_work(0)
        hg = jnp.dot(xb, wg_ref[...], preferred_element_type=F32)
        side_work(1)
        hu = jnp.dot(xb, wu_ref[...], preferred_element_type=F32)
        act = (hg * jax.nn.sigmoid(hg) * hu).astype(BF16)
        for q in range(n_down):
            side_work(2 + q)
            out = jnp.dot(act, wd_ref[:, q * 2 * LANES:(q + 1) * 2 * LANES], preferred_element_type=F32)
            out = jnp.where(live_row, out, 0.0)
            for c in range(2):
                y_cur[pl.ds(2 * q + c, MOE_CHUNK, stride=LANE_CHUNKS), :] = out[:, c * LANES:(c + 1) * LANES]

    @pl.when(e % 2 == 0)
    def _():
        run_expert(xg_a, y_a, xg_b, y_b)

    @pl.when(e % 2 == 1)
    def _():
        run_expert(xg_b, y_b, xg_a, y_a)

    def extra_chunk(c, carry):
        first = off + c * MOE_CHUNK
        gather_loop(xg_c, first)
        swiglu(xg_c, y_c)
        scatter_loop(y_c, first, jnp.minimum(MOE_CHUNK, n - c * MOE_CHUNK))
        return carry

    lax.fori_loop(1, (n + MOE_CHUNK - 1) // MOE_CHUNK, extra_chunk, 0)

    @pl.when(e == N_EXPERTS - 1)
    def _():
        more_tiles = tile + 1 < pl.num_programs(0)

        @pl.when(more_tiles)
        def _():
            x_load(tile + 1).start()

        scatter_loop(y_b, off, jnp.minimum(MOE_CHUNK, n))

        @pl.when(more_tiles)
        def _():
            for load in plan_loads(tile + 1):
                load.start()

        n_pieces = MOE_TILE // LN_ROWS
        store = lambda c: pltpu.make_async_copy(
            stage_s.at[c % 2], o_hbm.at[pl.ds(tile * MOE_TILE + c * LN_ROWS, LN_ROWS), :], sem.at[4 + c % 2])
        for c in range(n_pieces):
            z = _load_row_tiles(acc_s, LN_ROWS, c * LN_ROWS * LANE_CHUNKS)
            if c >= 2:
                store(c - 2).wait()
            stage_s[c % 2] = _layer_norm(z, lw_ref[...], lb_ref[...])
            store(c).start()
        store(n_pieces - 2).wait()
        store(n_pieces - 1).wait()


def _moe_tiles(x1, base, plan_rows, plan_gates, cnt, off, wg, wu, wd, ln_w, ln_b):
    T = x1.shape[0] // LANE_CHUNKS
    w_spec = lambda shape: pl.BlockSpec((None,) + shape, lambda i, e, cnt, off: (e, 0, 0))
    vec = pl.BlockSpec((1, D_MODEL), lambda i, e, cnt, off: (0, 0))
    hbm = pl.BlockSpec(memory_space=pl.ANY)
    tile_rows = MOE_TILE * LANE_CHUNKS
    return pl.pallas_call(
        _moe_tile_kernel,
        grid_spec=pltpu.PrefetchScalarGridSpec(
            num_scalar_prefetch=2,
            grid=(T // MOE_TILE, N_EXPERTS),
            in_specs=[w_spec((D_MODEL, EXPERT_FF)), w_spec((D_MODEL, EXPERT_FF)), w_spec((EXPERT_FF, D_MODEL)),
                      vec, vec, hbm, hbm, hbm, hbm],
            out_specs=hbm,
            scratch_shapes=[pltpu.VMEM((tile_rows, LANES), F32),
                            pltpu.VMEM((tile_rows, LANES), F32)]
                           + [pltpu.VMEM((MOE_CHUNK * LANE_CHUNKS, LANES), F32)] * 6
                           + [pltpu.VMEM((2, LN_ROWS, D_MODEL), F32),
                              pltpu.SMEM((LIST_LEN,), jnp.int32),
                              pltpu.SMEM((LIST_LEN,), F32),
                              pltpu.SemaphoreType.DMA((6,))],
        ),
        out_shape=jax.ShapeDtypeStruct((T, D_MODEL), F32),
        compiler_params=_cparams("arbitrary", "arbitrary"),
        name="moe_tiles_ln2",
    )(cnt, off, wg, wu, wd, ln_w, ln_b, plan_rows, plan_gates, x1, base)


def kernel(x, p, w_in, hgrn_lb_logits, hgrn_norm_w, w_branch_att, w_branch_hgrn, w_out, ln1_w, ln1_b, router_w, router_bias, expert_w_gate, expert_w_up, expert_w_down, shared_w_gate, shared_w_up, shared_w_down, ple_gate_w, ple_proj_w, ln2_w, ln2_b):
    B, S, D = x.shape
    T = B * S
    l = 0
    x2d = x.reshape(T, D)
    bf = lambda a: a.astype(BF16)

    ws = _att_weights(w_in[l])
    qkv = [_proj_att(x2d, ws[g], d) for g, d in enumerate(ATT_DILATIONS)]
    y_att = _attention(qkv, B, S)
    u_hg = _proj(x2d, bf(w_in[l][:, 3 * len(ATT_DILATIONS) * ATT_WIDTH:]), PROJ_COLS)
    y_hg = _hgrn(u_hg, hgrn_lb_logits, hgrn_norm_w[l:l + 1], B, S)
    x1 = _merge(y_att, y_hg, u_hg, x2d, bf(w_branch_att[l]), bf(w_branch_hgrn[l]), bf(w_out[l]),
                ln1_w[l:l + 1], ln1_b[l:l + 1])

    base, idx, gate, rank, counts = _route(
        x1, p[l].reshape(T, PLE_DIM), router_w[l].T, router_bias[l].reshape(N_EXPERTS, 1),
        bf(shared_w_gate[l]), bf(shared_w_up[l]), bf(shared_w_down[l]), bf(ple_gate_w[l]), bf(ple_proj_w[l]))
    cnt = counts[:, :, 0]
    off = jnp.cumsum(cnt, axis=1) - cnt
    cnt, off = cnt.reshape(-1), off.reshape(-1)
    plan_rows, plan_gates = _plan_sc(off, idx, rank, gate)
    out = _moe_tiles(x1, base, plan_rows, plan_gates, cnt, off, bf(expert_w_gate[l]), bf(expert_w_up[l]),
                     bf(expert_w_down[l]), ln2_w[l:l + 1], ln2_b[l:l + 1])
    return out.reshape(B, S, D)
```

```python
import functools

import jax
import jax.numpy as jnp
import numpy as np
from jax import lax
from jax.experimental import pallas as pl
from jax.experimental.pallas import tpu as pltpu
from jax.experimental.pallas import tpu_sc as plsc

F32 = jnp.float32
BF16 = jnp.bfloat16

D_MODEL = 1024
ATT_HEAD_DIM = 64
ATT_HEADS = 8
ATT_DILATIONS = (1, 4, 16)
ATT_BLOCK = 128
ATT_WIDTH = ATT_HEADS * ATT_HEAD_DIM
ATT_TILE = ATT_BLOCK * max(ATT_DILATIONS)
NEG_INF = -1e30
LOG2_E = 1.4426950408889634

HG_HEADS = 8
HG_DIM = 128
HG_WIDTH = HG_HEADS * HG_DIM
HG_CHUNK = 32
HG_TILE = 256
RMS_EPS = 1e-6

N_EXPERTS = 64
TOP_K = 8
N_GROUPS = 8
GROUP_SIZE = N_EXPERTS // N_GROUPS
TOPK_GROUPS = 4
EXPERT_FF = 256
ROUTED_SCALE = 2.5
PLE_DIM = 256
LN_EPS = 1e-5
DEPTH = 1
DEEPNORM_ALPHA = (2.0 * DEPTH) ** 0.25

LANES = 128
LANE_CHUNKS = D_MODEL // LANES
PROJ_ROWS = 512
PROJ_COLS = 1536
ATT_PROJ_ROWS = 1024
MIX_ROWS = 512
MOE_TILE = 4096
MOE_CHUNK = 576
LN_ROWS = 256
LIST_PAD = 1024
LIST_LEN = MOE_TILE * TOP_K + LIST_PAD
GATHER_GROUP = 8
SCATTER_GROUP = 8
V7X_VMEM_LIMIT = 56 * 1024 * 1024
SC_CORES, SC_SUBCORES, SC_LANES = 2, 16, 16


def _cparams(*sem):
    return pltpu.CompilerParams(dimension_semantics=sem, vmem_limit_bytes=V7X_VMEM_LIMIT)


def _proj_att_kernel(*refs, dil):
    x_refs, w_ref, o_ref = refs[:LANE_CHUNKS], refs[LANE_CHUNKS], refs[LANE_CHUNKS + 1]
    n = ATT_PROJ_ROWS // dil

    def rows(ref):
        if dil == 1:
            return ref[...]
        return jnp.concatenate([ref[pl.ds(r, n, stride=dil), :] for r in range(dil)], axis=0)

    xp = jnp.concatenate([rows(ref).astype(BF16) for ref in x_refs], axis=1)
    y = jnp.dot(xp, w_ref[...], preferred_element_type=F32)
    o_ref[...] = y.astype(BF16).reshape(dil, n, 3 * ATT_WIDTH)


def _proj_att(x2d, w, dil):
    T = x2d.shape[0]
    per = ATT_TILE // ATT_PROJ_ROWS
    n = ATT_PROJ_ROWS // dil
    out = pl.pallas_call(
        functools.partial(_proj_att_kernel, dil=dil),
        grid=(T // ATT_PROJ_ROWS,),
        in_specs=[pl.BlockSpec((ATT_PROJ_ROWS, LANES), functools.partial(lambda i, c: (i, c), c=c))
                  for c in range(LANE_CHUNKS)]
                 + [pl.BlockSpec((D_MODEL, 3 * ATT_WIDTH), lambda i: (0, 0))],
        out_specs=pl.BlockSpec((None, dil, None, n, 3 * ATT_WIDTH), lambda i: (i // per, 0, i % per, 0, 0)),
        out_shape=jax.ShapeDtypeStruct((T // ATT_TILE, dil, per, n, 3 * ATT_WIDTH), BF16),
        compiler_params=_cparams("parallel"),
        name=f"proj_att_d{dil}",
    )(*([x2d] * LANE_CHUNKS), w)
    return out.reshape(T // ATT_TILE, dil, ATT_TILE // dil, 3 * ATT_WIDTH)


def _att_pair(q2, kp, kc, vp, vc, bias_ref, g, first):
    def head0_lanes(rows, dtype):
        lane = lax.broadcasted_iota(jnp.int32, (rows, 2 * ATT_HEAD_DIM), 1)
        return lane.astype(F32).astype(dtype) < ATT_HEAD_DIM

    lo_q = head0_lanes(ATT_BLOCK, BF16)
    lo_v = head0_lanes(2 * ATT_BLOCK, BF16)
    k2 = jnp.concatenate([kp, kc], axis=0)
    v2 = jnp.concatenate([vp, vc], axis=0)
    zero = jnp.zeros_like(q2)
    ps, ms = [], []
    for hh in range(2):
        qm = jnp.where(lo_q, q2, zero) if hh == 0 else jnp.where(lo_q, zero, q2)
        s = lax.dot_general(qm, k2, (((1,), (1,)), ((), ())), preferred_element_type=F32)
        s = s + bias_ref[g, hh, first]
        m = jnp.max(s, axis=-1, keepdims=True)
        ps.append(jnp.exp2(s - m).astype(BF16))
        ms.append(m)
    pcat = jnp.concatenate(ps, axis=1)
    zero_v, one_v = jnp.zeros_like(v2), jnp.ones_like(v2)
    rhs = jnp.concatenate([
        jnp.concatenate([jnp.where(lo_v, v2, zero_v), jnp.where(lo_v, one_v, zero_v)], axis=1),
        jnp.concatenate([jnp.where(lo_v, zero_v, v2), jnp.where(lo_v, zero_v, one_v)], axis=1)], axis=0)
    nd = jnp.dot(pcat, rhs, preferred_element_type=F32)
    m2 = jnp.where(head0_lanes(ATT_BLOCK, F32), ms[0], ms[1])
    return nd[:, :2 * ATT_HEAD_DIM], m2, nd[:, 2 * ATT_HEAD_DIM:]


def _att_kernel(*refs):
    (q0, kc0, vc0, kp0, vp0, q1, kc1, vc1, kp1, vp1, q2, kc2, vc2, kp2, vp2,
     bias_ref, o_ref) = refs[:17]
    ng = len(ATT_DILATIONS)
    num_s, m_s, den_s = refs[17:17 + ng], refs[17 + ng:17 + 2 * ng], refs[17 + 2 * ng:]
    first_tile = (pl.program_id(2) == 0).astype(jnp.int32)
    groups = ((q0, kc0, vc0, kp0, vp0), (q1, kc1, vc1, kp1, vp1), (q2, kc2, vc2, kp2, vp2))
    for g, dil in enumerate(ATT_DILATIONS):
        q_ref, kc_ref, vc_ref, kp_ref, vp_ref = groups[g]
        nb = ATT_TILE // dil // ATT_BLOCK
        for r in range(dil):
            for n in range(nb):
                rows = pl.ds(n * ATT_BLOCK, ATT_BLOCK)
                if n == 0:
                    prev = pl.ds((nb - 1) * ATT_BLOCK, ATT_BLOCK)
                    kp, vp, first = kp_ref[r, prev, :], vp_ref[r, prev, :], first_tile
                else:
                    prev = pl.ds((n - 1) * ATT_BLOCK, ATT_BLOCK)
                    kp, vp, first = kc_ref[r, prev, :], vc_ref[r, prev, :], 0
                num, m, den = _att_pair(q_ref[r, rows, :], kp, kc_ref[r, rows, :], vp, vc_ref[r, rows, :],
                                        bias_ref, g, first)
                if dil == 1:
                    dst = rows
                else:
                    dst = pl.ds(n * ATT_BLOCK * dil + r, ATT_BLOCK, stride=dil)
                num_s[g][dst, :] = num
                m_s[g][dst, :] = m
                den_s[g][dst, :] = den
    m_all = jnp.maximum(jnp.maximum(m_s[0][...], m_s[1][...]), m_s[2][...])
    num = jnp.zeros((ATT_TILE, 2 * ATT_HEAD_DIM), F32)
    den = jnp.zeros((ATT_TILE, 2 * ATT_HEAD_DIM), F32)
    for g in range(ng):
        sc = jnp.exp2(m_s[g][...] - m_all)
        num = num + sc * num_s[g][...]
        den = den + sc * den_s[g][...]
    o_ref[...] = (num / den).astype(o_ref.dtype)


def _att_bias_table():
    qi = np.arange(ATT_BLOCK)[:, None]
    ki = np.arange(2 * ATT_BLOCK)[None, :]
    steps = qi + ATT_BLOCK - ki
    valid = (steps >= 0) & (steps <= ATT_BLOCK)
    slopes = np.array([2.0 ** (-8.0 * (h + 1) / ATT_HEADS) for h in range(ATT_HEADS)], np.float32)
    tab = np.empty((len(ATT_DILATIONS), ATT_HEADS, 2, ATT_BLOCK, 2 * ATT_BLOCK), np.float32)
    for g, dil in enumerate(ATT_DILATIONS):
        bias = -slopes[:, None, None] * (steps * dil).astype(np.float32)[None] * LOG2_E
        tab[g, :, 0] = np.where(valid[None], bias, NEG_INF)
        tab[g, :, 1] = np.where((valid & (ki >= ATT_BLOCK))[None], bias, NEG_INF)
    return jnp.asarray(tab)


def _attention(qkv, B, S):
    tiles = S // ATT_TILE
    pair = 2 * ATT_HEAD_DIM
    npair = ATT_WIDTH // pair
    in_specs, args = [], []
    for g, dil in enumerate(ATT_DILATIONS):
        blk = (None, dil, ATT_TILE // dil, pair)
        cur = lambda b, hp, t, off: (b * tiles + t, 0, 0, off * npair + hp)
        prv = lambda b, hp, t, off: (b * tiles + jnp.maximum(t - 1, 0), 0, 0, off * npair + hp)
        in_specs += [pl.BlockSpec(blk, functools.partial(cur, off=0)),
                     pl.BlockSpec(blk, functools.partial(cur, off=1)),
                     pl.BlockSpec(blk, functools.partial(cur, off=2)),
                     pl.BlockSpec(blk, functools.partial(prv, off=1)),
                     pl.BlockSpec(blk, functools.partial(prv, off=2))]
        args += [qkv[g]] * 5
    in_specs.append(pl.BlockSpec((len(ATT_DILATIONS), 2, 2, ATT_BLOCK, 2 * ATT_BLOCK),
                                 lambda b, hp, t: (0, hp, 0, 0, 0)))
    args.append(_att_bias_table())
    scratch = [pltpu.VMEM((ATT_TILE, pair), F32) for _ in range(3 * len(ATT_DILATIONS))]
    return pl.pallas_call(
        _att_kernel,
        grid=(B, npair, tiles),
        in_specs=in_specs,
        out_specs=pl.BlockSpec((ATT_TILE, pair), lambda b, hp, t: (b * tiles + t, hp)),
        out_shape=jax.ShapeDtypeStruct((B * S, ATT_WIDTH), BF16),
        scratch_shapes=scratch,
        compiler_params=_cparams("parallel", "parallel", "arbitrary"),
        name="dilated_attention",
    )(*args)


def _att_weights(w_in_l):
    out = []
    width = len(ATT_DILATIONS) * ATT_WIDTH
    for g in range(len(ATT_DILATIONS)):
        cols = [w_in_l[:, part * width + g * ATT_WIDTH: part * width + (g + 1) * ATT_WIDTH] for part in range(3)]
        cols[0] = cols[0] * (ATT_HEAD_DIM ** -0.5 * LOG2_E)
        out.append(jnp.concatenate(cols, axis=1).astype(BF16))
    return out


def _proj_kernel(x_ref, w_ref, o_ref, *, col_tile):
    xb = x_ref[...].astype(BF16)
    for c in range(w_ref.shape[1] // col_tile):
        cols = slice(c * col_tile, (c + 1) * col_tile)
        o_ref[:, cols] = jnp.dot(xb, w_ref[:, cols], preferred_element_type=F32).astype(o_ref.dtype)


def _proj(x2d, w, col_tile):
    T, N = x2d.shape[0], w.shape[1]
    return pl.pallas_call(
        functools.partial(_proj_kernel, col_tile=col_tile),
        grid=(T // PROJ_ROWS,),
        in_specs=[pl.BlockSpec((PROJ_ROWS, D_MODEL), lambda i: (i, 0)),
                  pl.BlockSpec((D_MODEL, N), lambda i: (0, 0))],
        out_specs=pl.BlockSpec((PROJ_ROWS, N), lambda i: (i, 0)),
        out_shape=jax.ShapeDtypeStruct((T, N), BF16),
        compiler_params=_cparams("parallel"),
        name="proj_hgrn_gates",
    )(x2d, w)


def _split2(v):
    a = v.astype(BF16)
    return a, (v - a.astype(F32)).astype(BF16)


def _hgrn_kernel(q_ref, f_ref, i_ref, g_ref, lbl_ref, gain_ref, o_ref, state_ref):
    @pl.when(pl.program_id(1) == 0)
    def _():
        state_ref[...] = jnp.zeros_like(state_ref)

    lbl = lbl_ref[...]
    e = jnp.exp(lbl - jnp.max(lbl, axis=0, keepdims=True))
    lb = e[0:1] / jnp.sum(e, axis=0, keepdims=True)
    forget = lb + (1.0 - lb) * jax.nn.sigmoid(f_ref[...].astype(F32))
    log_f = jnp.log(forget)
    key = 1.0 - forget

    row = lax.broadcasted_iota(jnp.int32, (HG_TILE, HG_TILE), 0)
    col = lax.broadcasted_iota(jnp.int32, (HG_TILE, HG_TILE), 1)
    causal = (row >= col) & ((row // HG_CHUNK) == (col // HG_CHUNK))
    tri = jnp.where(causal, 1.0, 0.0).astype(BF16)
    b = sum(jnp.dot(tri, t, preferred_element_type=F32) for t in _split2(log_f))
    eb = jnp.exp(b)
    q_dec = (q_ref[...].astype(F32) * eb).astype(BF16)
    k_inv = key * jnp.exp(-b)
    xi = i_ref[...].astype(F32)
    val = (xi * jax.nn.sigmoid(xi)).astype(BF16)
    k_inv_b = k_inv.astype(BF16)

    n_chunks = HG_TILE // HG_CHUNK
    last_rows = [eb[(c + 1) * HG_CHUNK - 1:(c + 1) * HG_CHUNK, :] for c in range(n_chunks)]
    dec_rows = jnp.concatenate([jnp.broadcast_to(r, (HG_CHUNK, HG_WIDTH)) for r in last_rows], axis=0)
    k_end = (k_inv * dec_rows).astype(BF16)
    def per_chunk_columns(t):
        blocks = []
        for c in range(n_chunks):
            rows_above, rows_below = c * HG_CHUNK, HG_TILE - (c + 1) * HG_CHUNK
            parts = [t[rows_above:rows_above + HG_CHUNK]]
            if rows_above:
                parts.insert(0, jnp.zeros((rows_above, HG_DIM), t.dtype))
            if rows_below:
                parts.append(jnp.zeros((rows_below, HG_DIM), t.dtype))
            blocks.append(jnp.concatenate(parts, axis=0))
        return jnp.concatenate(blocks, axis=1)

    outs = []
    for h in range(HG_HEADS):
        cols = slice(h * HG_DIM, (h + 1) * HG_DIM)
        qd, ki, vv = q_dec[:, cols], k_inv_b[:, cols], val[:, cols]
        a = lax.dot_general(qd, ki, (((1,), (1,)), ((), ())), preferred_element_type=F32)
        a = jnp.where(causal, a, 0.0).astype(BF16)
        o_intra = jnp.dot(a, vv, preferred_element_type=F32)
        upd = lax.dot_general(vv, per_chunk_columns(k_end[:, cols]), (((0,), (0,)), ((), ())),
                              preferred_element_type=F32)
        st = state_ref[h]
        entering = []
        for c in range(n_chunks):
            entering.append(st.astype(BF16))
            st = st * last_rows[c][:, cols] + upd[:, c * HG_DIM:(c + 1) * HG_DIM]
        state_ref[h] = st
        o_inter = lax.dot_general(per_chunk_columns(qd), jnp.concatenate(entering, axis=1),
                                  (((1,), (1,)), ((), ())), preferred_element_type=F32)
        o = o_intra + o_inter
        o = o * lax.rsqrt(jnp.mean(jnp.square(o), axis=-1, keepdims=True) + RMS_EPS)
        outs.append(o)
    o = jnp.concatenate(outs, axis=1) * gain_ref[...]
    gg = g_ref[...].astype(F32)
    o_ref[...] = (o * (gg * jax.nn.sigmoid(gg))).astype(o_ref.dtype)


def _hgrn(u_hg, lb_logits, gain, B, S):
    tiles = S // HG_TILE
    col = lambda j: pl.BlockSpec((HG_TILE, HG_WIDTH), functools.partial(lambda b, t, j: (b * tiles + t, j), j=j))
    return pl.pallas_call(
        _hgrn_kernel,
        grid=(B, tiles),
        in_specs=[col(0), col(1), col(2), col(3),
                  pl.BlockSpec((2, HG_WIDTH), lambda b, t: (0, 0)),
                  pl.BlockSpec((1, HG_WIDTH), lambda b, t: (0, 0))],
        out_specs=pl.BlockSpec((HG_TILE, HG_WIDTH), lambda b, t: (b * tiles + t, 0)),
        out_shape=jax.ShapeDtypeStruct((B * S, HG_WIDTH), BF16),
        scratch_shapes=[pltpu.VMEM((HG_HEADS, HG_DIM, HG_DIM), F32)],
        compiler_params=_cparams("parallel", "arbitrary"),
        name="hgrn2",
    )(u_hg, u_hg, u_hg, u_hg, lb_logits, gain)


def _load_row_tiles(ref, n, start=0):
    return jnp.concatenate([ref[pl.ds(start + c, n, stride=LANE_CHUNKS), :] for c in range(LANE_CHUNKS)], axis=1)


def _store_row_tiles(ref, val, n):
    for c in range(LANE_CHUNKS):
        ref[pl.ds(c, n, stride=LANE_CHUNKS), :] = val[:, c * LANES:(c + 1) * LANES]


def _layer_norm(z, w, b):
    mu = jnp.mean(z, axis=-1, keepdims=True)
    zc = z - mu
    var = jnp.mean(jnp.square(zc), axis=-1, keepdims=True)
    return zc * lax.rsqrt(var + LN_EPS) * w + b


def _merge_rows(ya_ref, yh_ref, ga_ref, gh_ref, x_ref, wa_ref, wh_ref, wo_ref, lw_ref, lb_ref):
    ma = jnp.dot(ya_ref[...], wa_ref[...], preferred_element_type=F32)
    mh = jnp.dot(yh_ref[...], wh_ref[...], preferred_element_type=F32)
    merged = (jax.nn.sigmoid(ga_ref[...].astype(F32)) * ma + jax.nn.sigmoid(gh_ref[...].astype(F32)) * mh)
    z = DEEPNORM_ALPHA * x_ref[...] + jnp.dot(merged.astype(BF16), wo_ref[...], preferred_element_type=F32)
    return _layer_norm(z, lw_ref[...], lb_ref[...])


def _first_argmax(v, ids, n):
    mx = jnp.max(v, axis=0, keepdims=True)
    return mx, jnp.min(jnp.where(v == mx, ids, n), axis=0, keepdims=True)


def _mix_kernel(ya_ref, yh_ref, ga_ref, gh_ref, x_ref, wa_ref, wh_ref, wo_ref, lw_ref, lb_ref,
                p_ref, wrt_ref, rb_ref, wsg_ref, wsu_ref, wsd_ref, wpg_ref, wpp_ref,
                x1_ref, base_ref, idx_ref, gate_ref, rank_ref, cnt_ref, carry_ref):
    @pl.when(pl.program_id(0) % (MOE_TILE // MIX_ROWS) == 0)
    def _():
        carry_ref[...] = jnp.zeros_like(carry_ref)

    x1 = _merge_rows(ya_ref, yh_ref, ga_ref, gh_ref, x_ref, wa_ref, wh_ref, wo_ref, lw_ref, lb_ref)
    _store_row_tiles(x1_ref, x1, MIX_ROWS)
    x1b = x1.astype(BF16)
    logits = lax.dot_general(wrt_ref[...], x1, (((1,), (1,)), ((), ())), preferred_element_type=F32,
                             precision=lax.Precision.HIGHEST)
    s = jax.nn.sigmoid(logits)
    sel = s + rb_ref[...]
    eid = lax.broadcasted_iota(jnp.int32, (N_EXPERTS, MIX_ROWS), 0)
    neg = -jnp.inf

    grp = sel.reshape(N_GROUPS, GROUP_SIZE, MIX_ROWS)
    mid = lax.broadcasted_iota(jnp.int32, grp.shape, 1)
    m1 = jnp.max(grp, axis=1, keepdims=True)
    i1 = jnp.min(jnp.where(grp == m1, mid, GROUP_SIZE), axis=1, keepdims=True)
    m2 = jnp.max(jnp.where(mid == i1, neg, grp), axis=1, keepdims=True)
    gscore = (m1 + m2).reshape(N_GROUPS, MIX_ROWS)
    gid = lax.broadcasted_iota(jnp.int32, (N_GROUPS, MIX_ROWS), 0)
    gsel = jnp.zeros((N_GROUPS, MIX_ROWS), jnp.bool_)
    for _ in range(TOPK_GROUPS):
        _, gi = _first_argmax(gscore, gid, N_GROUPS)
        hit = gid == gi
        gsel = gsel | hit
        gscore = jnp.where(hit, neg, gscore)
    emask = jnp.broadcast_to(gsel.reshape(N_GROUPS, 1, MIX_ROWS), grp.shape).reshape(N_EXPERTS, MIX_ROWS)
    cand = jnp.where(emask, sel, neg)

    idxs, gates = [], []
    chosen = jnp.zeros((N_EXPERTS, MIX_ROWS), jnp.bool_)
    for _ in range(TOP_K):
        _, ei = _first_argmax(cand, eid, N_EXPERTS)
        hit = eid == ei
        idxs.append(ei)
        gates.append(jnp.sum(jnp.where(hit, s, 0.0), axis=0, keepdims=True))
        chosen = chosen | hit
        cand = jnp.where(hit, neg, cand)
    g = jnp.concatenate(gates, axis=0)
    g = g / jnp.sum(g, axis=0, keepdims=True) * ROUTED_SCALE
    idx_ref[...] = jnp.concatenate(idxs, axis=0)
    gate_ref[...] = g

    onehot = jnp.where(chosen, 1.0, 0.0)
    tr = lax.broadcasted_iota(jnp.int32, (MIX_ROWS, MIX_ROWS), 0)
    tc = lax.broadcasted_iota(jnp.int32, (MIX_ROWS, MIX_ROWS), 1)
    before = jnp.where(tr < tc, 1.0, 0.0).astype(BF16)
    prefix = jnp.dot(onehot.astype(BF16), before, preferred_element_type=F32)
    rankfull = (carry_ref[:, 0:1] + prefix).astype(jnp.int32)
    rank_ref[...] = jnp.concatenate(
        [jnp.sum(jnp.where(eid == ei, rankfull, 0), axis=0, keepdims=True) for ei in idxs], axis=0)
    total = carry_ref[...] + jnp.sum(onehot, axis=1, keepdims=True)
    carry_ref[...] = total
    cnt_ref[...] = total.astype(jnp.int32)

    hg = jnp.dot(x1b, wsg_ref[...], preferred_element_type=F32)
    hu = jnp.dot(x1b, wsu_ref[...], preferred_element_type=F32)
    shared = jnp.dot((hg * jax.nn.sigmoid(hg) * hu).astype(BF16), wsd_ref[...], preferred_element_type=F32)
    ple = (jax.nn.sigmoid(jnp.dot(x1b, wpg_ref[...], preferred_element_type=F32))
           * jnp.dot(p_ref[...].astype(BF16), wpp_ref[...], preferred_element_type=F32))
    _store_row_tiles(base_ref, DEEPNORM_ALPHA * x1 + shared + ple, MIX_ROWS)


def _mix(y_att, y_hg, u_hg, x2d, merge_weights, p2d, route_weights):
    T = x2d.shape[0]
    per_tile = MOE_TILE // MIX_ROWS
    rows = lambda width, j=0: pl.BlockSpec((MIX_ROWS, width), functools.partial(lambda i, j: (i, j), j=j))
    full = lambda a: pl.BlockSpec(a.shape, lambda i: (0, 0))
    tok = pl.BlockSpec((TOP_K, MIX_ROWS), lambda i: (0, i))
    row_tiles = pl.BlockSpec((MIX_ROWS * LANE_CHUNKS, LANES), lambda i: (i, 0))
    return pl.pallas_call(
        _mix_kernel,
        grid=(T // MIX_ROWS,),
        in_specs=[rows(ATT_WIDTH), rows(HG_WIDTH), rows(D_MODEL, 4), rows(D_MODEL, 5), rows(D_MODEL)]
                 + [full(w) for w in merge_weights] + [rows(PLE_DIM)] + [full(w) for w in route_weights],
        out_specs=[row_tiles, row_tiles, tok, tok, tok,
                   pl.BlockSpec((None, N_EXPERTS, LANES), lambda i: (i // per_tile, 0, 0))],
        out_shape=[jax.ShapeDtypeStruct((T * LANE_CHUNKS, LANES), F32),
                   jax.ShapeDtypeStruct((T * LANE_CHUNKS, LANES), F32),
                   jax.ShapeDtypeStruct((TOP_K, T), jnp.int32),
                   jax.ShapeDtypeStruct((TOP_K, T), F32),
                   jax.ShapeDtypeStruct((TOP_K, T), jnp.int32),
                   jax.ShapeDtypeStruct((T // MOE_TILE, N_EXPERTS, LANES), jnp.int32)],
        scratch_shapes=[pltpu.VMEM((N_EXPERTS, LANES), F32)],
        compiler_params=_cparams("arbitrary"),
        name="merge_route_shared_ple",
    )(y_att, y_hg, u_hg, u_hg, x2d, *merge_weights, p2d, *route_weights)


def _plan_sc_kernel(off_hbm, idx_hbm, rank_hbm, gate_hbm, rows_hbm, gates_hbm,
                    off_v, idx_v, rank_v, gate_v, rows_v, gates_v):
    n_tokens = idx_hbm.shape[0] // TOP_K
    worker = lax.axis_index("subcore") * SC_CORES + lax.axis_index("core")

    @pl.when(worker < n_tokens // MOE_TILE)
    def _():
        pltpu.sync_copy(off_hbm.at[pl.ds(worker * N_EXPERTS, N_EXPERTS)], off_v)
        lane = lax.iota(jnp.int32, SC_LANES)
        for k in range(TOP_K):
            row = pl.ds(k * n_tokens + worker * MOE_TILE, MOE_TILE)
            pltpu.sync_copy(idx_hbm.at[row], idx_v)
            pltpu.sync_copy(rank_hbm.at[row], rank_v)
            pltpu.sync_copy(gate_hbm.at[row], gate_v)

            @pl.loop(0, MOE_TILE // SC_LANES)
            def _(i):
                at = i * SC_LANES
                pos = plsc.load_gather(off_v, [idx_v[pl.ds(at, SC_LANES)]]) + rank_v[pl.ds(at, SC_LANES)]
                plsc.store_scatter(rows_v, [pos], (lane + at) * LANE_CHUNKS)
                plsc.store_scatter(gates_v, [pos], gate_v[pl.ds(at, SC_LANES)])

        @pl.loop(0, LIST_PAD // SC_LANES)
        def _(i):
            tail = pl.ds(MOE_TILE * TOP_K + i * SC_LANES, SC_LANES)
            rows_v[tail] = jnp.zeros((SC_LANES,), jnp.int32)
            gates_v[tail] = jnp.zeros((SC_LANES,), F32)

        out = pl.ds(worker * LIST_LEN, LIST_LEN)
        pltpu.sync_copy(rows_v, rows_hbm.at[out])
        pltpu.sync_copy(gates_v, gates_hbm.at[out])


def _plan_sc(off, idx, rank, gate):
    n_tiles = idx.shape[1] // MOE_TILE
    assert n_tiles <= SC_CORES * SC_SUBCORES
    mesh = plsc.VectorSubcoreMesh(core_axis_name="core", subcore_axis_name="subcore",
                                  num_cores=SC_CORES, num_subcores=SC_SUBCORES)
    return pl.kernel(
        _plan_sc_kernel,
        out_type=(jax.ShapeDtypeStruct((n_tiles * LIST_LEN,), jnp.int32),
                  jax.ShapeDtypeStruct((n_tiles * LIST_LEN,), F32)),
        mesh=mesh,
        scratch_types=[pltpu.VMEM((N_EXPERTS,), jnp.int32), pltpu.VMEM((MOE_TILE,), jnp.int32),
                       pltpu.VMEM((MOE_TILE,), jnp.int32), pltpu.VMEM((MOE_TILE,), F32),
                       pltpu.VMEM((LIST_LEN,), jnp.int32), pltpu.VMEM((LIST_LEN,), F32)],
        compiler_params=pltpu.CompilerParams(needs_layout_passes=False),
        name="moe_plan_sc",
    )(off, idx.reshape(-1), rank.reshape(-1), gate.reshape(-1))


def _moe_tile_kernel(cnt_ref, off_ref, wg_ref, wu_ref, wd_ref, lw_ref, lb_ref, rows_hbm, gates_hbm, x_hbm, base_hbm,
                     o_hbm, x_s, acc_s, xg_a, xg_b, xg_c, y_a, y_b, y_c, stage_s, rows_s, gates_s, sem):
    tile, e = pl.program_id(0), pl.program_id(1)
    rows_of = lambda ref, r, n: ref.at[pl.ds(pl.multiple_of(r * LANE_CHUNKS, LANE_CHUNKS), n * LANE_CHUNKS), :]
    tile_rows = pl.ds(pl.multiple_of(tile * (MOE_TILE * LANE_CHUNKS), LANE_CHUNKS), MOE_TILE * LANE_CHUNKS)

    pair = tile * N_EXPERTS + e
    last_pair = pl.num_programs(0) * N_EXPERTS - 1
    n, off = cnt_ref[pair], off_ref[pair]
    tile_at = lambda ref, r: ref.at[pl.ds(pl.multiple_of(r, LANE_CHUNKS), LANE_CHUNKS), :]

    def gather_group(xg, first, jb):
        at = first + jb * GATHER_GROUP
        rows = [tile_at(x_s, rows_s[at + u])[...] for u in range(GATHER_GROUP)]
        rows_of(xg, jb * GATHER_GROUP, GATHER_GROUP)[...] = jnp.concatenate(rows, axis=0)

    def gather_loop(xg, first):
        def body(jb, cc):
            gather_group(xg, first, jb)
            return cc

        lax.fori_loop(0, MOE_CHUNK // GATHER_GROUP, body, 0)

    def swiglu(xg, y):
        xb = _load_row_tiles(xg, MOE_CHUNK).astype(BF16)
        hg = jnp.dot(xb, wg_ref[...], preferred_element_type=F32)
        hu = jnp.dot(xb, wu_ref[...], preferred_element_type=F32)
        act = (hg * jax.nn.sigmoid(hg) * hu).astype(BF16)
        _store_row_tiles(y, jnp.dot(act, wd_ref[...], preferred_element_type=F32), MOE_CHUNK)

    def scatter_group(y, first, j0, live):
        dsts = [rows_s[first + j0 + u] for u in range(live)]
        gates = [gates_s[first + j0 + u] for u in range(live)]
        yv = rows_of(y, j0, live)[...]
        vals = [tile_at(acc_s, d)[...] + g * yv[u * LANE_CHUNKS:(u + 1) * LANE_CHUNKS]
                for u, (d, g) in enumerate(zip(dsts, gates))]
        for d, val in reversed(list(zip(dsts, vals))):
            tile_at(acc_s, d)[...] = val

    def scatter_loop(y, first, m):
        def body(jg, cc):
            scatter_group(y, first, jg * SCATTER_GROUP, SCATTER_GROUP)
            return cc

        lax.fori_loop(0, m // SCATTER_GROUP, body, 0)
        for live in range(1, SCATTER_GROUP):
            @pl.when(m % SCATTER_GROUP == live)
            def _(live=live):
                scatter_group(y, first, m - live, live)

    def plan_loads(t):
        plan = pl.ds(pl.multiple_of(t * LIST_LEN, LIST_PAD), LIST_LEN)
        return (pltpu.make_async_copy(rows_hbm.at[plan], rows_s, sem.at[0]),
                pltpu.make_async_copy(gates_hbm.at[plan], gates_s, sem.at[1]))

    def x_load(t):
        rows = pl.ds(pl.multiple_of(t * (MOE_TILE * LANE_CHUNKS), LANE_CHUNKS), MOE_TILE * LANE_CHUNKS)
        return pltpu.make_async_copy(x_hbm.at[rows, :], x_s, sem.at[2])

    @pl.when(e == 0)
    def _():
        @pl.when(tile == 0)
        def _():
            x_load(tile).start()
            for load in plan_loads(tile):
                load.start()

        load_base = pltpu.make_async_copy(base_hbm.at[tile_rows, :], acc_s, sem.at[3])
        load_base.start()
        y_b[...] = jnp.zeros_like(y_b)
        load_rows, load_gates = plan_loads(tile)
        load_rows.wait()
        x_load(tile).wait()
        gather_loop(xg_a, off)
        load_gates.wait()
        load_base.wait()

    prev_off = off_ref[jnp.maximum(pair - 1, 0)]
    next_off = off_ref[jnp.minimum(pair + 1, last_pair)]
    live_row = lax.broadcasted_iota(jnp.int32, (MOE_CHUNK, 2 * LANES), 0) < n

    def run_expert(xg_cur, y_cur, xg_nxt, y_prv):
        gathers = [functools.partial(gather_group, xg_nxt, next_off, jb) for jb in range(MOE_CHUNK // GATHER_GROUP)]
        scatters = [functools.partial(scatter_group, y_prv, prev_off, jg * SCATTER_GROUP, SCATTER_GROUP)
                    for jg in range(MOE_CHUNK // SCATTER_GROUP)]
        side = [s for both in zip(gathers, scatters) for s in both]
        n_down = LANE_CHUNKS // 2
        cost = [D_MODEL, D_MODEL] + [EXPERT_FF] * n_down
        bounds = [round(len(side) * sum(cost[:i]) / sum(cost)) for i in range(len(cost) + 1)]

        def side_work(i):
            for s in side[bounds[i]:bounds[i + 1]]:
                s()

        xb = _load_row_tiles(xg_cur, MOE_CHUNK).astype(BF16)
        side_work(0)
        hg = jnp.dot(xb, wg_ref[...], preferred_element_type=F32)
        side_work(1)
        hu = jnp.dot(xb, wu_ref[...], preferred_element_type=F32)
        act = (hg * jax.nn.sigmoid(hg) * hu).astype(BF16)
        for q in range(n_down):
            side_work(2 + q)
            out = jnp.dot(act, wd_ref[:, q * 2 * LANES:(q + 1) * 2 * LANES], preferred_element_type=F32)
            out = jnp.where(live_row, out, 0.0)
            for c in range(2):
                y_cur[pl.ds(2 * q + c, MOE_CHUNK, stride=LANE_CHUNKS), :] = out[:, c * LANES:(c + 1) * LANES]

    @pl.when(e % 2 == 0)
    def _():
        run_expert(xg_a, y_a, xg_b, y_b)

    @pl.when(e % 2 == 1)
    def _():
        run_expert(xg_b, y_b, xg_a, y_a)

    def extra_chunk(c, carry):
        first = off + c * MOE_CHUNK
        gather_loop(xg_c, first)
        swiglu(xg_c, y_c)
        scatter_loop(y_c, first, jnp.minimum(MOE_CHUNK, n - c * MOE_CHUNK))
        return carry

    lax.fori_loop(1, (n + MOE_CHUNK - 1) // MOE_CHUNK, extra_chunk, 0)

    @pl.when(e == N_EXPERTS - 1)
    def _():
        more_tiles = tile + 1 < pl.num_programs(0)

        @pl.when(more_tiles)
        def _():
            x_load(tile + 1).start()

        scatter_loop(y_b, off, jnp.minimum(MOE_CHUNK, n))

        @pl.when(more_tiles)
        def _():
            for load in plan_loads(tile + 1):
                load.start()

        n_pieces = MOE_TILE // LN_ROWS
        store = lambda c: pltpu.make_async_copy(
            stage_s.at[c % 2], o_hbm.at[pl.ds(tile * MOE_TILE + c * LN_ROWS, LN_ROWS), :], sem.at[4 + c % 2])
        for c in range(n_pieces):
            z = _load_row_tiles(acc_s, LN_ROWS, c * LN_ROWS * LANE_CHUNKS)
            if c >= 2:
                store(c - 2).wait()
            stage_s[c % 2] = _layer_norm(z, lw_ref[...], lb_ref[...])
            store(c).start()
        store(n_pieces - 2).wait()
        store(n_pieces - 1).wait()


def _moe_tiles(x1, base, plan_rows, plan_gates, cnt, off, wg, wu, wd, ln_w, ln_b):
    T = x1.shape[0] // LANE_CHUNKS
    w_spec = lambda shape: pl.BlockSpec((None,) + shape, lambda i, e, cnt, off: (e, 0, 0))
    vec = pl.BlockSpec((1, D_MODEL), lambda i, e, cnt, off: (0, 0))
    hbm = pl.BlockSpec(memory_space=pl.ANY)
    tile_rows = MOE_TILE * LANE_CHUNKS
    return pl.pallas_call(
        _moe_tile_kernel,
        grid_spec=pltpu.PrefetchScalarGridSpec(
            num_scalar_prefetch=2,
            grid=(T // MOE_TILE, N_EXPERTS),
            in_specs=[w_spec((D_MODEL, EXPERT_FF)), w_spec((D_MODEL, EXPERT_FF)), w_spec((EXPERT_FF, D_MODEL)),
                      vec, vec, hbm, hbm, hbm, hbm],
            out_specs=hbm,
            scratch_shapes=[pltpu.VMEM((tile_rows, LANES), F32),
                            pltpu.VMEM((tile_rows, LANES), F32)]
                           + [pltpu.VMEM((MOE_CHUNK * LANE_CHUNKS, LANES), F32)] * 6
                           + [pltpu.VMEM((2, LN_ROWS, D_MODEL), F32),
                              pltpu.SMEM((LIST_LEN,), jnp.int32),
                              pltpu.SMEM((LIST_LEN,), F32),
                              pltpu.SemaphoreType.DMA((6,))],
        ),
        out_shape=jax.ShapeDtypeStruct((T, D_MODEL), F32),
        compiler_params=_cparams("arbitrary", "arbitrary"),
        name="moe_tiles_ln2",
    )(cnt, off, wg, wu, wd, ln_w, ln_b, plan_rows, plan_gates, x1, base)


def kernel(x, p, w_in, hgrn_lb_logits, hgrn_norm_w, w_branch_att, w_branch_hgrn, w_out, ln1_w, ln1_b, router_w, router_bias, expert_w_gate, expert_w_up, expert_w_down, shared_w_gate, shared_w_up, shared_w_down, ple_gate_w, ple_proj_w, ln2_w, ln2_b):
    B, S, D = x.shape
    T = B * S
    l = 0
    x2d = x.reshape(T, D)
    bf = lambda a: a.astype(BF16)

    ws = _att_weights(w_in[l])
    qkv = [_proj_att(x2d, ws[g], d) for g, d in enumerate(ATT_DILATIONS)]
    y_att = _attention(qkv, B, S)
    u_hg = _proj(x2d, bf(w_in[l][:, 3 * len(ATT_DILATIONS) * ATT_WIDTH:]), PROJ_COLS)
    y_hg = _hgrn(u_hg, hgrn_lb_logits, hgrn_norm_w[l:l + 1], B, S)
    x1, base, idx, gate, rank, counts = _mix(
        y_att, y_hg, u_hg, x2d,
        (bf(w_branch_att[l]), bf(w_branch_hgrn[l]), bf(w_out[l]), ln1_w[l:l + 1], ln1_b[l:l + 1]),
        p[l].reshape(T, PLE_DIM),
        (router_w[l].T, router_bias[l].reshape(N_EXPERTS, 1), bf(shared_w_gate[l]), bf(shared_w_up[l]),
         bf(shared_w_down[l]), bf(ple_gate_w[l]), bf(ple_proj_w[l])))
    cnt = counts[:, :, 0]
    off = jnp.cumsum(cnt, axis=1) - cnt
    cnt, off = cnt.reshape(-1), off.reshape(-1)
    plan_rows, plan_gates = _plan_sc(off, idx, rank, gate)
    out = _moe_tiles(x1, base, plan_rows, plan_gates, cnt, off, bf(expert_w_gate[l]), bf(expert_w_up[l]),
                     bf(expert_w_down[l]), ln2_w[l:l + 1], ln2_b[l:l + 1])
    return out.reshape(B, S, D)
```

```python
import functools

import jax
import jax.numpy as jnp
import numpy as np
from jax import lax
from jax.experimental import pallas as pl
from jax.experimental.pallas import tpu as pltpu
from jax.experimental.pallas import tpu_sc as plsc

F32 = jnp.float32
BF16 = jnp.bfloat16

D_MODEL = 1024
ATT_HEAD_DIM = 64
ATT_HEADS = 8
ATT_DILATIONS = (1, 4, 16)
ATT_BLOCK = 128
ATT_WIDTH = ATT_HEADS * ATT_HEAD_DIM
ATT_TILE = ATT_BLOCK * max(ATT_DILATIONS)
NEG_INF = -1e30
LOG2_E = 1.4426950408889634

HG_HEADS = 8
HG_DIM = 128
HG_WIDTH = HG_HEADS * HG_DIM
HG_CHUNK = 32
HG_TILE = 256
RMS_EPS = 1e-6

N_EXPERTS = 64
TOP_K = 8
N_GROUPS = 8
GROUP_SIZE = N_EXPERTS // N_GROUPS
TOPK_GROUPS = 4
EXPERT_FF = 256
ROUTED_SCALE = 2.5
PLE_DIM = 256
LN_EPS = 1e-5
DEPTH = 1
DEEPNORM_ALPHA = (2.0 * DEPTH) ** 0.25

LANES = 128
LANE_CHUNKS = D_MODEL // LANES
PROJ_ROWS = 512
PROJ_COLS = 1536
ATT_PROJ_ROWS = 1024
MIX_ROWS = 512
MOE_TILE = 4096
MOE_CHUNK = 576
LN_ROWS = 256
LIST_PAD = 1024
LIST_LEN = MOE_TILE * TOP_K + LIST_PAD
GATHER_GROUP = 8
SCATTER_GROUP = 8
V7X_VMEM_LIMIT = 56 * 1024 * 1024
SC_CORES, SC_SUBCORES, SC_LANES = 2, 16, 16


def _cparams(*sem):
    return pltpu.CompilerParams(dimension_semantics=sem, vmem_limit_bytes=V7X_VMEM_LIMIT)


def _proj_att_kernel(*refs, dil):
    x_refs, w_ref, o_ref = refs[:LANE_CHUNKS], refs[LANE_CHUNKS], refs[LANE_CHUNKS + 1]
    n = ATT_PROJ_ROWS // dil

    def rows(ref):
        if dil == 1:
            return ref[...]
        return jnp.concatenate([ref[pl.ds(r, n, stride=dil), :] for r in range(dil)], axis=0)

    xp = jnp.concatenate([rows(ref).astype(BF16) for ref in x_refs], axis=1)
    y = jnp.dot(xp, w_ref[...], preferred_element_type=F32)
    o_ref[...] = y.astype(BF16).reshape(dil, n, 3 * ATT_WIDTH)


def _proj_att(x2d, w, dil):
    T = x2d.shape[0]
    per = ATT_TILE // ATT_PROJ_ROWS
    n = ATT_PROJ_ROWS // dil
    out = pl.pallas_call(
        functools.partial(_proj_att_kernel, dil=dil),
        grid=(T // ATT_PROJ_ROWS,),
        in_specs=[pl.BlockSpec((ATT_PROJ_ROWS, LANES), functools.partial(lambda i, c: (i, c), c=c))
                  for c in range(LANE_CHUNKS)]
                 + [pl.BlockSpec((D_MODEL, 3 * ATT_WIDTH), lambda i: (0, 0))],
        out_specs=pl.BlockSpec((None, dil, None, n, 3 * ATT_WIDTH), lambda i: (i // per, 0, i % per, 0, 0)),
        out_shape=jax.ShapeDtypeStruct((T // ATT_TILE, dil, per, n, 3 * ATT_WIDTH), BF16),
        compiler_params=_cparams("parallel"),
        name=f"proj_att_d{dil}",
    )(*([x2d] * LANE_CHUNKS), w)
    return out.reshape(T // ATT_TILE, dil, ATT_TILE // dil, 3 * ATT_WIDTH)


def _att_pair(q2, kp, kc, vp, vc, bias_ref, g, first):
    def head0_lanes(rows, dtype):
        lane = lax.broadcasted_iota(jnp.int32, (rows, 2 * ATT_HEAD_DIM), 1)
        return lane.astype(F32).astype(dtype) < ATT_HEAD_DIM

    lo_q = head0_lanes(ATT_BLOCK, BF16)
    lo_v = head0_lanes(2 * ATT_BLOCK, BF16)
    k2 = jnp.concatenate([kp, kc], axis=0)
    v2 = jnp.concatenate([vp, vc], axis=0)
    zero = jnp.zeros_like(q2)
    ps, ms = [], []
    for hh in range(2):
        qm = jnp.where(lo_q, q2, zero) if hh == 0 else jnp.where(lo_q, zero, q2)
        s = lax.dot_general(qm, k2, (((1,), (1,)), ((), ())), preferred_element_type=F32)
        s = s + bias_ref[g, hh, first]
        m = jnp.max(s, axis=-1, keepdims=True)
        ps.append(jnp.exp2(s - m).astype(BF16))
        ms.append(m)
    pcat = jnp.concatenate(ps, axis=1)
    zero_v, one_v = jnp.zeros_like(v2), jnp.ones_like(v2)
    rhs = jnp.concatenate([
        jnp.concatenate([jnp.where(lo_v, v2, zero_v), jnp.where(lo_v, one_v, zero_v)], axis=1),
        jnp.concatenate([jnp.where(lo_v, zero_v, v2), jnp.where(lo_v, zero_v, one_v)], axis=1)], axis=0)
    nd = jnp.dot(pcat, rhs, preferred_element_type=F32)
    m2 = jnp.where(head0_lanes(ATT_BLOCK, F32), ms[0], ms[1])
    return nd[:, :2 * ATT_HEAD_DIM], m2, nd[:, 2 * ATT_HEAD_DIM:]


def _att_kernel(*refs):
    (q0, kc0, vc0, kp0, vp0, q1, kc1, vc1, kp1, vp1, q2, kc2, vc2, kp2, vp2,
     bias_ref, o_ref) = refs[:17]
    ng = len(ATT_DILATIONS)
    num_s, m_s, den_s = refs[17:17 + ng], refs[17 + ng:17 + 2 * ng], refs[17 + 2 * ng:]
    first_tile = (pl.program_id(2) == 0).astype(jnp.int32)
    groups = ((q0, kc0, vc0, kp0, vp0), (q1, kc1, vc1, kp1, vp1), (q2, kc2, vc2, kp2, vp2))
    for g, dil in enumerate(ATT_DILATIONS):
        q_ref, kc_ref, vc_ref, kp_ref, vp_ref = groups[g]
        nb = ATT_TILE // dil // ATT_BLOCK
        for r in range(dil):
            for n in range(nb):
                rows = pl.ds(n * ATT_BLOCK, ATT_BLOCK)
                if n == 0:
                    prev = pl.ds((nb - 1) * ATT_BLOCK, ATT_BLOCK)
                    kp, vp, first = kp_ref[r, prev, :], vp_ref[r, prev, :], first_tile
                else:
                    prev = pl.ds((n - 1) * ATT_BLOCK, ATT_BLOCK)
                    kp, vp, first = kc_ref[r, prev, :], vc_ref[r, prev, :], 0
                num, m, den = _att_pair(q_ref[r, rows, :], kp, kc_ref[r, rows, :], vp, vc_ref[r, rows, :],
                                        bias_ref, g, first)
                if dil == 1:
                    dst = rows
                else:
                    dst = pl.ds(n * ATT_BLOCK * dil + r, ATT_BLOCK, stride=dil)
                num_s[g][dst, :] = num
                m_s[g][dst, :] = m
                den_s[g][dst, :] = den
    m_all = jnp.maximum(jnp.maximum(m_s[0][...], m_s[1][...]), m_s[2][...])
    num = jnp.zeros((ATT_TILE, 2 * ATT_HEAD_DIM), F32)
    den = jnp.zeros((ATT_TILE, 2 * ATT_HEAD_DIM), F32)
    for g in range(ng):
        sc = jnp.exp2(m_s[g][...] - m_all)
        num = num + sc * num_s[g][...]
        den = den + sc * den_s[g][...]
    o_ref[...] = (num / den).astype(o_ref.dtype)


def _att_bias_table():
    qi = np.arange(ATT_BLOCK)[:, None]
    ki = np.arange(2 * ATT_BLOCK)[None, :]
    steps = qi + ATT_BLOCK - ki
    valid = (steps >= 0) & (steps <= ATT_BLOCK)
    slopes = np.array([2.0 ** (-8.0 * (h + 1) / ATT_HEADS) for h in range(ATT_HEADS)], np.float32)
    tab = np.empty((len(ATT_DILATIONS), ATT_HEADS, 2, ATT_BLOCK, 2 * ATT_BLOCK), np.float32)
    for g, dil in enumerate(ATT_DILATIONS):
        bias = -slopes[:, None, None] * (steps * dil).astype(np.float32)[None] * LOG2_E
        tab[g, :, 0] = np.where(valid[None], bias, NEG_INF)
        tab[g, :, 1] = np.where((valid & (ki >= ATT_BLOCK))[None], bias, NEG_INF)
    return jnp.asarray(tab)


def _attention(qkv, B, S):
    tiles = S // ATT_TILE
    pair = 2 * ATT_HEAD_DIM
    npair = ATT_WIDTH // pair
    in_specs, args = [], []
    for g, dil in enumerate(ATT_DILATIONS):
        blk = (None, dil, ATT_TILE // dil, pair)
        cur = lambda b, hp, t, off: (b * tiles + t, 0, 0, off * npair + hp)
        prv = lambda b, hp, t, off: (b * tiles + jnp.maximum(t - 1, 0), 0, 0, off * npair + hp)
        in_specs += [pl.BlockSpec(blk, functools.partial(cur, off=0)),
                     pl.BlockSpec(blk, functools.partial(cur, off=1)),
                     pl.BlockSpec(blk, functools.partial(cur, off=2)),
                     pl.BlockSpec(blk, functools.partial(prv, off=1)),
                     pl.BlockSpec(blk, functools.partial(prv, off=2))]
        args += [qkv[g]] * 5
    in_specs.append(pl.BlockSpec((len(ATT_DILATIONS), 2, 2, ATT_BLOCK, 2 * ATT_BLOCK),
                                 lambda b, hp, t: (0, hp, 0, 0, 0)))
    args.append(_att_bias_table())
    scratch = [pltpu.VMEM((ATT_TILE, pair), F32) for _ in range(3 * len(ATT_DILATIONS))]
    return pl.pallas_call(
        _att_kernel,
        grid=(B, npair, tiles),
        in_specs=in_specs,
        out_specs=pl.BlockSpec((ATT_TILE, pair), lambda b, hp, t: (b * tiles + t, hp)),
        out_shape=jax.ShapeDtypeStruct((B * S, ATT_WIDTH), BF16),
        scratch_shapes=scratch,
        compiler_params=_cparams("parallel", "parallel", "arbitrary"),
        name="dilated_attention",
    )(*args)


def _att_weights(w_in_l):
    out = []
    width = len(ATT_DILATIONS) * ATT_WIDTH
    for g in range(len(ATT_DILATIONS)):
        cols = [w_in_l[:, part * width + g * ATT_WIDTH: part * width + (g + 1) * ATT_WIDTH] for part in range(3)]
        cols[0] = cols[0] * (ATT_HEAD_DIM ** -0.5 * LOG2_E)
        out.append(jnp.concatenate(cols, axis=1).astype(BF16))
    return out


def _proj_kernel(x_ref, w_ref, o_ref, *, col_tile):
    xb = x_ref[...].astype(BF16)
    for c in range(w_ref.shape[1] // col_tile):
        cols = slice(c * col_tile, (c + 1) * col_tile)
        o_ref[:, cols] = jnp.dot(xb, w_ref[:, cols], preferred_element_type=F32).astype(o_ref.dtype)


def _proj(x2d, w, col_tile):
    T, N = x2d.shape[0], w.shape[1]
    return pl.pallas_call(
        functools.partial(_proj_kernel, col_tile=col_tile),
        grid=(T // PROJ_ROWS,),
        in_specs=[pl.BlockSpec((PROJ_ROWS, D_MODEL), lambda i: (i, 0)),
                  pl.BlockSpec((D_MODEL, N), lambda i: (0, 0))],
        out_specs=pl.BlockSpec((PROJ_ROWS, N), lambda i: (i, 0)),
        out_shape=jax.ShapeDtypeStruct((T, N), BF16),
        compiler_params=_cparams("parallel"),
        name="proj_hgrn_gates",
    )(x2d, w)


def _split2(v):
    a = v.astype(BF16)
    return a, (v - a.astype(F32)).astype(BF16)


def _hgrn_kernel(q_ref, f_ref, i_ref, g_ref, lbl_ref, gain_ref, o_ref, state_ref):
    @pl.when(pl.program_id(1) == 0)
    def _():
        state_ref[...] = jnp.zeros_like(state_ref)

    lbl = lbl_ref[...]
    e = jnp.exp(lbl - jnp.max(lbl, axis=0, keepdims=True))
    lb = e[0:1] / jnp.sum(e, axis=0, keepdims=True)
    forget = lb + (1.0 - lb) * jax.nn.sigmoid(f_ref[...].astype(F32))
    log_f = jnp.log(forget)
    key = 1.0 - forget

    row = lax.broadcasted_iota(jnp.int32, (HG_TILE, HG_TILE), 0)
    col = lax.broadcasted_iota(jnp.int32, (HG_TILE, HG_TILE), 1)
    causal = (row >= col) & ((row // HG_CHUNK) == (col // HG_CHUNK))
    tri = jnp.where(causal, 1.0, 0.0).astype(BF16)
    b = sum(jnp.dot(tri, t, preferred_element_type=F32) for t in _split2(log_f))
    eb = jnp.exp(b)
    q_dec = (q_ref[...].astype(F32) * eb).astype(BF16)
    k_inv = key * jnp.exp(-b)
    xi = i_ref[...].astype(F32)
    val = (xi * jax.nn.sigmoid(xi)).astype(BF16)
    k_inv_b = k_inv.astype(BF16)

    n_chunks = HG_TILE // HG_CHUNK
    last_rows = [eb[(c + 1) * HG_CHUNK - 1:(c + 1) * HG_CHUNK, :] for c in range(n_chunks)]
    dec_rows = jnp.concatenate([jnp.broadcast_to(r, (HG_CHUNK, HG_WIDTH)) for r in last_rows], axis=0)
    k_end = (k_inv * dec_rows).astype(BF16)
    def per_chunk_columns(t):
        blocks = []
        for c in range(n_chunks):
            rows_above, rows_below = c * HG_CHUNK, HG_TILE - (c + 1) * HG_CHUNK
            parts = [t[rows_above:rows_above + HG_CHUNK]]
            if rows_above:
                parts.insert(0, jnp.zeros((rows_above, HG_DIM), t.dtype))
            if rows_below:
                parts.append(jnp.zeros((rows_below, HG_DIM), t.dtype))
            blocks.append(jnp.concatenate(parts, axis=0))
        return jnp.concatenate(blocks, axis=1)

    head_cols = [slice(h * HG_DIM, (h + 1) * HG_DIM) for h in range(HG_HEADS)]
    upds = [lax.dot_general(val[:, cols], per_chunk_columns(k_end[:, cols]), (((0,), (0,)), ((), ())),
                            preferred_element_type=F32) for cols in head_cols]
    o_intras = []
    for cols in head_cols:
        a = lax.dot_general(q_dec[:, cols], k_inv_b[:, cols], (((1,), (1,)), ((), ())), preferred_element_type=F32)
        a = jnp.where(causal, a, 0.0).astype(BF16)
        o_intras.append(jnp.dot(a, val[:, cols], preferred_element_type=F32))
    enterings = []
    for h, cols in enumerate(head_cols):
        st = state_ref[h]
        entering = []
        for c in range(n_chunks):
            entering.append(st.astype(BF16))
            st = st * last_rows[c][:, cols] + upds[h][:, c * HG_DIM:(c + 1) * HG_DIM]
        state_ref[h] = st
        enterings.append(jnp.concatenate(entering, axis=1))
    outs = []
    for h, cols in enumerate(head_cols):
        qd = q_dec[:, cols]
        o_inter = lax.dot_general(per_chunk_columns(qd), enterings[h],
                                  (((1,), (1,)), ((), ())), preferred_element_type=F32)
        o = o_intras[h] + o_inter
        o = o * lax.rsqrt(jnp.mean(jnp.square(o), axis=-1, keepdims=True) + RMS_EPS)
        outs.append(o)
    o = jnp.concatenate(outs, axis=1) * gain_ref[...]
    gg = g_ref[...].astype(F32)
    o_ref[...] = (o * (gg * jax.nn.sigmoid(gg))).astype(o_ref.dtype)


def _hgrn(u_hg, lb_logits, gain, B, S):
    tiles = S // HG_TILE
    col = lambda j: pl.BlockSpec((HG_TILE, HG_WIDTH), functools.partial(lambda b, t, j: (b * tiles + t, j), j=j))
    return pl.pallas_call(
        _hgrn_kernel,
        grid=(B, tiles),
        in_specs=[col(0), col(1), col(2), col(3),
                  pl.BlockSpec((2, HG_WIDTH), lambda b, t: (0, 0)),
                  pl.BlockSpec((1, HG_WIDTH), lambda b, t: (0, 0))],
        out_specs=pl.BlockSpec((HG_TILE, HG_WIDTH), lambda b, t: (b * tiles + t, 0)),
        out_shape=jax.ShapeDtypeStruct((B * S, HG_WIDTH), BF16),
        scratch_shapes=[pltpu.VMEM((HG_HEADS, HG_DIM, HG_DIM), F32)],
        compiler_params=_cparams("parallel", "arbitrary"),
        name="hgrn2",
    )(u_hg, u_hg, u_hg, u_hg, lb_logits, gain)


def _load_row_tiles(ref, n, start=0):
    return jnp.concatenate([ref[pl.ds(start + c, n, stride=LANE_CHUNKS), :] for c in range(LANE_CHUNKS)], axis=1)


def _store_row_tiles(ref, val, n):
    for c in range(LANE_CHUNKS):
        ref[pl.ds(c, n, stride=LANE_CHUNKS), :] = val[:, c * LANES:(c + 1) * LANES]


def _layer_norm(z, w, b):
    mu = jnp.mean(z, axis=-1, keepdims=True)
    zc = z - mu
    var = jnp.mean(jnp.square(zc), axis=-1, keepdims=True)
    return zc * lax.rsqrt(var + LN_EPS) * w + b


def _merge_kernel(ya_ref, yh_ref, ga_ref, gh_ref, x_ref, wa_ref, wh_ref, wo_ref, lw_ref, lb_ref, o_ref):
    ma = jnp.dot(ya_ref[...], wa_ref[...], preferred_element_type=F32)
    mh = jnp.dot(yh_ref[...], wh_ref[...], preferred_element_type=F32)
    merged = (jax.nn.sigmoid(ga_ref[...].astype(F32)) * ma + jax.nn.sigmoid(gh_ref[...].astype(F32)) * mh)
    z = DEEPNORM_ALPHA * x_ref[...] + jnp.dot(merged.astype(BF16), wo_ref[...], preferred_element_type=F32)
    _store_row_tiles(o_ref, _layer_norm(z, lw_ref[...], lb_ref[...]), MIX_ROWS)


def _merge(y_att, y_hg, u_hg, x2d, w_a, w_h, w_o, ln_w, ln_b):
    T = x2d.shape[0]
    rows = lambda width, j=0: pl.BlockSpec((MIX_ROWS, width), functools.partial(lambda i, j: (i, j), j=j))
    full = lambda a: pl.BlockSpec(a.shape, lambda i: (0, 0))
    return pl.pallas_call(
        _merge_kernel,
        grid=(T // MIX_ROWS,),
        in_specs=[rows(ATT_WIDTH), rows(HG_WIDTH), rows(D_MODEL, 4), rows(D_MODEL, 5), rows(D_MODEL),
                  full(w_a), full(w_h), full(w_o), full(ln_w), full(ln_b)],
        out_specs=pl.BlockSpec((MIX_ROWS * LANE_CHUNKS, LANES), lambda i: (i, 0)),
        out_shape=jax.ShapeDtypeStruct((T * LANE_CHUNKS, LANES), F32),
        compiler_params=_cparams("parallel"),
        name="merge_ln1",
    )(y_att, y_hg, u_hg, u_hg, x2d, w_a, w_h, w_o, ln_w, ln_b)


def _first_argmax(v, ids, n):
    mx = jnp.max(v, axis=0, keepdims=True)
    return mx, jnp.min(jnp.where(v == mx, ids, n), axis=0, keepdims=True)


def _route_kernel(x1_ref, p_ref, wrt_ref, rb_ref, wsg_ref, wsu_ref, wsd_ref, wpg_ref, wpp_ref,
                  base_ref, idx_ref, gate_ref, rank_ref, cnt_ref, carry_ref):
    @pl.when(pl.program_id(0) % (MOE_TILE // MIX_ROWS) == 0)
    def _():
        carry_ref[...] = jnp.zeros_like(carry_ref)

    x1 = _load_row_tiles(x1_ref, MIX_ROWS)
    x1b = x1.astype(BF16)
    logits = lax.dot_general(wrt_ref[...], x1, (((1,), (1,)), ((), ())), preferred_element_type=F32,
                             precision=lax.Precision.HIGHEST)
    s = jax.nn.sigmoid(logits)
    sel = s + rb_ref[...]
    eid = lax.broadcasted_iota(jnp.int32, (N_EXPERTS, MIX_ROWS), 0)
    neg = -jnp.inf

    grp = sel.reshape(N_GROUPS, GROUP_SIZE, MIX_ROWS)
    mid = lax.broadcasted_iota(jnp.int32, grp.shape, 1)
    m1 = jnp.max(grp, axis=1, keepdims=True)
    i1 = jnp.min(jnp.where(grp == m1, mid, GROUP_SIZE), axis=1, keepdims=True)
    m2 = jnp.max(jnp.where(mid == i1, neg, grp), axis=1, keepdims=True)
    gscore = (m1 + m2).reshape(N_GROUPS, MIX_ROWS)
    gid = lax.broadcasted_iota(jnp.int32, (N_GROUPS, MIX_ROWS), 0)
    gsel = jnp.zeros((N_GROUPS, MIX_ROWS), jnp.bool_)
    for _ in range(TOPK_GROUPS):
        _, gi = _first_argmax(gscore, gid, N_GROUPS)
        hit = gid == gi
        gsel = gsel | hit
        gscore = jnp.where(hit, neg, gscore)
    emask = jnp.broadcast_to(gsel.reshape(N_GROUPS, 1, MIX_ROWS), grp.shape).reshape(N_EXPERTS, MIX_ROWS)
    cand = jnp.where(emask, sel, neg)

    idxs, gates = [], []
    chosen = jnp.zeros((N_EXPERTS, MIX_ROWS), jnp.bool_)
    for _ in range(TOP_K):
        _, ei = _first_argmax(cand, eid, N_EXPERTS)
        hit = eid == ei
        idxs.append(ei)
        gates.append(jnp.sum(jnp.where(hit, s, 0.0), axis=0, keepdims=True))
        chosen = chosen | hit
        cand = jnp.where(hit, neg, cand)
    g = jnp.concatenate(gates, axis=0)
    g = g / jnp.sum(g, axis=0, keepdims=True) * ROUTED_SCALE
    idx_ref[...] = jnp.concatenate(idxs, axis=0)
    gate_ref[...] = g

    onehot = jnp.where(chosen, 1.0, 0.0)
    tr = lax.broadcasted_iota(jnp.int32, (MIX_ROWS, MIX_ROWS), 0)
    tc = lax.broadcasted_iota(jnp.int32, (MIX_ROWS, MIX_ROWS), 1)
    before = jnp.where(tr < tc, 1.0, 0.0).astype(BF16)
    prefix = jnp.dot(onehot.astype(BF16), before, preferred_element_type=F32)
    rankfull = (carry_ref[:, 0:1] + prefix).astype(jnp.int32)
    rank_ref[...] = jnp.concatenate(
        [jnp.sum(jnp.where(eid == ei, rankfull, 0), axis=0, keepdims=True) for ei in idxs], axis=0)
    total = carry_ref[...] + jnp.sum(onehot, axis=1, keepdims=True)
    carry_ref[...] = total
    cnt_ref[...] = total.astype(jnp.int32)

    hg = jnp.dot(x1b, wsg_ref[...], preferred_element_type=F32)
    hu = jnp.dot(x1b, wsu_ref[...], preferred_element_type=F32)
    shared = jnp.dot((hg * jax.nn.sigmoid(hg) * hu).astype(BF16), wsd_ref[...], preferred_element_type=F32)
    ple = (jax.nn.sigmoid(jnp.dot(x1b, wpg_ref[...], preferred_element_type=F32))
           * jnp.dot(p_ref[...].astype(BF16), wpp_ref[...], preferred_element_type=F32))
    _store_row_tiles(base_ref, DEEPNORM_ALPHA * x1 + shared + ple, MIX_ROWS)


def _route(x1, p2d, wr_t, rbias, wsg, wsu, wsd, wpg, wpp):
    T = x1.shape[0] // LANE_CHUNKS
    per_tile = MOE_TILE // MIX_ROWS
    full = lambda a: pl.BlockSpec(a.shape, lambda i: (0, 0))
    tok = pl.BlockSpec((TOP_K, MIX_ROWS), lambda i: (0, i))
    row_tiles = pl.BlockSpec((MIX_ROWS * LANE_CHUNKS, LANES), lambda i: (i, 0))
    return pl.pallas_call(
        _route_kernel,
        grid=(T // MIX_ROWS,),
        in_specs=[row_tiles,
                  pl.BlockSpec((MIX_ROWS, PLE_DIM), lambda i: (i, 0)),
                  full(wr_t), full(rbias), full(wsg), full(wsu), full(wsd), full(wpg), full(wpp)],
        out_specs=[row_tiles, tok, tok, tok,
                   pl.BlockSpec((None, N_EXPERTS, LANES), lambda i: (i // per_tile, 0, 0))],
        out_shape=[jax.ShapeDtypeStruct((T * LANE_CHUNKS, LANES), F32),
                   jax.ShapeDtypeStruct((TOP_K, T), jnp.int32),
                   jax.ShapeDtypeStruct((TOP_K, T), F32),
                   jax.ShapeDtypeStruct((TOP_K, T), jnp.int32),
                   jax.ShapeDtypeStruct((T // MOE_TILE, N_EXPERTS, LANES), jnp.int32)],
        scratch_shapes=[pltpu.VMEM((N_EXPERTS, LANES), F32)],
        compiler_params=_cparams("arbitrary"),
        name="route_shared_ple",
    )(x1, p2d, wr_t, rbias, wsg, wsu, wsd, wpg, wpp)


def _plan_sc_kernel(off_hbm, idx_hbm, rank_hbm, gate_hbm, rows_hbm, gates_hbm,
                    off_v, idx_v, rank_v, gate_v, rows_v, gates_v):
    n_tokens = idx_hbm.shape[0] // TOP_K
    worker = lax.axis_index("subcore") * SC_CORES + lax.axis_index("core")

    @pl.when(worker < n_tokens // MOE_TILE)
    def _():
        pltpu.sync_copy(off_hbm.at[pl.ds(worker * N_EXPERTS, N_EXPERTS)], off_v)
        lane = lax.iota(jnp.int32, SC_LANES)
        for k in range(TOP_K):
            row = pl.ds(k * n_tokens + worker * MOE_TILE, MOE_TILE)
            pltpu.sync_copy(idx_hbm.at[row], idx_v)
            pltpu.sync_copy(rank_hbm.at[row], rank_v)
            pltpu.sync_copy(gate_hbm.at[row], gate_v)

            @pl.loop(0, MOE_TILE // SC_LANES)
            def _(i):
                at = i * SC_LANES
                pos = plsc.load_gather(off_v, [idx_v[pl.ds(at, SC_LANES)]]) + rank_v[pl.ds(at, SC_LANES)]
                plsc.store_scatter(rows_v, [pos], (lane + at) * LANE_CHUNKS)
                plsc.store_scatter(gates_v, [pos], gate_v[pl.ds(at, SC_LANES)])

        @pl.loop(0, LIST_PAD // SC_LANES)
        def _(i):
            tail = pl.ds(MOE_TILE * TOP_K + i * SC_LANES, SC_LANES)
            rows_v[tail] = jnp.zeros((SC_LANES,), jnp.int32)
            gates_v[tail] = jnp.zeros((SC_LANES,), F32)

        out = pl.ds(worker * LIST_LEN, LIST_LEN)
        pltpu.sync_copy(rows_v, rows_hbm.at[out])
        pltpu.sync_copy(gates_v, gates_hbm.at[out])


def _plan_sc(off, idx, rank, gate):
    n_tiles = idx.shape[1] // MOE_TILE
    assert n_tiles <= SC_CORES * SC_SUBCORES
    mesh = plsc.VectorSubcoreMesh(core_axis_name="core", subcore_axis_name="subcore",
                                  num_cores=SC_CORES, num_subcores=SC_SUBCORES)
    return pl.kernel(
        _plan_sc_kernel,
        out_type=(jax.ShapeDtypeStruct((n_tiles * LIST_LEN,), jnp.int32),
                  jax.ShapeDtypeStruct((n_tiles * LIST_LEN,), F32)),
        mesh=mesh,
        scratch_types=[pltpu.VMEM((N_EXPERTS,), jnp.int32), pltpu.VMEM((MOE_TILE,), jnp.int32),
                       pltpu.VMEM((MOE_TILE,), jnp.int32), pltpu.VMEM((MOE_TILE,), F32),
                       pltpu.VMEM((LIST_LEN,), jnp.int32), pltpu.VMEM((LIST_LEN,), F32)],
        compiler_params=pltpu.CompilerParams(needs_layout_passes=False),
        name="moe_plan_sc",
    )(off, idx.reshape(-1), rank.reshape(-1), gate.reshape(-1))


def _moe_tile_kernel(cnt_ref, off_ref, wg_ref, wu_ref, wd_ref, lw_ref, lb_ref, rows_hbm, gates_hbm, x_hbm, base_hbm,
                     o_hbm, x_s, acc_s, xg_a, xg_b, xg_c, y_a, y_b, y_c, stage_s, rows_s, gates_s, sem):
    tile, e = pl.program_id(0), pl.program_id(1)
    rows_of = lambda ref, r, n: ref.at[pl.ds(pl.multiple_of(r * LANE_CHUNKS, LANE_CHUNKS), n * LANE_CHUNKS), :]
    tile_rows = pl.ds(pl.multiple_of(tile * (MOE_TILE * LANE_CHUNKS), LANE_CHUNKS), MOE_TILE * LANE_CHUNKS)

    pair = tile * N_EXPERTS + e
    last_pair = pl.num_programs(0) * N_EXPERTS - 1
    n, off = cnt_ref[pair], off_ref[pair]
    tile_at = lambda ref, r: ref.at[pl.ds(pl.multiple_of(r, LANE_CHUNKS), LANE_CHUNKS), :]

    def gather_group(xg, first, jb):
        at = first + jb * GATHER_GROUP
        rows = [tile_at(x_s, rows_s[at + u])[...] for u in range(GATHER_GROUP)]
        rows_of(xg, jb * GATHER_GROUP, GATHER_GROUP)[...] = jnp.concatenate(rows, axis=0)

    def gather_loop(xg, first):
        def body(jb, cc):
            gather_group(xg, first, jb)
            return cc

        lax.fori_loop(0, MOE_CHUNK // GATHER_GROUP, body, 0)

    def swiglu(xg, y):
        xb = _load_row_tiles(xg, MOE_CHUNK).astype(BF16)
        hg = jnp.dot(xb, wg_ref[...], preferred_element_type=F32)
        hu = jnp.dot(xb, wu_ref[...], preferred_element_type=F32)
        act = (hg * jax.nn.sigmoid(hg) * hu).astype(BF16)
        _store_row_tiles(y, jnp.dot(act, wd_ref[...], preferred_element_type=F32), MOE_CHUNK)

    def scatter_group(y, first, j0, live):
        dsts = [rows_s[first + j0 + u] for u in range(live)]
        gates = [gates_s[first + j0 + u] for u in range(live)]
        yv = rows_of(y, j0, live)[...]
        vals = [tile_at(acc_s, d)[...] + g * yv[u * LANE_CHUNKS:(u + 1) * LANE_CHUNKS]
                for u, (d, g) in enumerate(zip(dsts, gates))]
        for d, val in reversed(list(zip(dsts, vals))):
            tile_at(acc_s, d)[...] = val

    def scatter_loop(y, first, m):
        def body(jg, cc):
            scatter_group(y, first, jg * SCATTER_GROUP, SCATTER_GROUP)
            return cc

        lax.fori_loop(0, m // SCATTER_GROUP, body, 0)
        for live in range(1, SCATTER_GROUP):
            @pl.when(m % SCATTER_GROUP == live)
            def _(live=live):
                scatter_group(y, first, m - live, live)

    def plan_loads(t):
        plan = pl.ds(pl.multiple_of(t * LIST_LEN, LIST_PAD), LIST_LEN)
        return (pltpu.make_async_copy(rows_hbm.at[plan], rows_s, sem.at[0]),
                pltpu.make_async_copy(gates_hbm.at[plan], gates_s, sem.at[1]))

    def x_load(t):
        rows = pl.ds(pl.multiple_of(t * (MOE_TILE * LANE_CHUNKS), LANE_CHUNKS), MOE_TILE * LANE_CHUNKS)
        return pltpu.make_async_copy(x_hbm.at[rows, :], x_s, sem.at[2])

    @pl.when(e == 0)
    def _():
        @pl.when(tile == 0)
        def _():
            x_load(tile).start()
            for load in plan_loads(tile):
                load.start()

        load_base = pltpu.make_async_copy(base_hbm.at[tile_rows, :], acc_s, sem.at[3])
        load_base.start()
        y_b[...] = jnp.zeros_like(y_b)
        load_rows, load_gates = plan_loads(tile)
        load_rows.wait()
        x_load(tile).wait()
        gather_loop(xg_a, off)
        load_gates.wait()
        load_base.wait()

    prev_off = off_ref[jnp.maximum(pair - 1, 0)]
    next_off = off_ref[jnp.minimum(pair + 1, last_pair)]
    live_row = lax.broadcasted_iota(jnp.int32, (MOE_CHUNK, 2 * LANES), 0) < n

    def run_expert(xg_cur, y_cur, xg_nxt, y_prv):
        gathers = [functools.partial(gather_group, xg_nxt, next_off, jb) for jb in range(MOE_CHUNK // GATHER_GROUP)]
        scatters = [functools.partial(scatter_group, y_prv, prev_off, jg * SCATTER_GROUP, SCATTER_GROUP)
                    for jg in range(MOE_CHUNK // SCATTER_GROUP)]
        side = [s for both in zip(gathers, scatters) for s in both]
        n_down = LANE_CHUNKS // 2
        cost = [D_MODEL, D_MODEL] + [EXPERT_FF] * n_down
        bounds = [round(len(side) * sum(cost[:i]) / sum(cost)) for i in range(len(cost) + 1)]

        def side_work(i):
            for s in side[bounds[i]:bounds[i + 1]]:
                s()

        xb = _load_row_tiles(xg_cur, MOE_CHUNK).astype(BF16)
        side_work(0)
        hg = jnp.dot(xb, wg_ref[...], preferred_element_type=F32)
        side_work(1)
        hu = jnp.dot(xb, wu_ref[...], preferred_element_type=F32)
        act = (hg * jax.nn.sigmoid(hg) * hu).astype(BF16)
        for q in range(n_down):
            side_work(2 + q)
            out = jnp.dot(act, wd_ref[:, q * 2 * LANES:(q + 1) * 2 * LANES], preferred_element_type=F32)
            out = jnp.where(live_row, out, 0.0)
            for c in range(2):
                y_cur[pl.ds(2 * q + c, MOE_CHUNK, stride=LANE_CHUNKS), :] = out[:, c * LANES:(c + 1) * LANES]

    @pl.when(e % 2 == 0)
    def _():
        run_expert(xg_a, y_a, xg_b, y_b)

    @pl.when(e % 2 == 1)
    def _():
        run_expert(xg_b, y_b, xg_a, y_a)

    def extra_chunk(c, carry):
        first = off + c * MOE_CHUNK
        gather_loop(xg_c, first)
        swiglu(xg_c, y_c)
        scatter_loop(y_c, first, jnp.minimum(MOE_CHUNK, n - c * MOE_CHUNK))
        return carry

    lax.fori_loop(1, (n + MOE_CHUNK - 1) // MOE_CHUNK, extra_chunk, 0)

    @pl.when(e == N_EXPERTS - 1)
    def _():
        more_tiles = tile + 1 < pl.num_programs(0)

        @pl.when(more_tiles)
        def _():
            x_load(tile + 1).start()

        scatter_loop(y_b, off, jnp.minimum(MOE_CHUNK, n))

        @pl.when(more_tiles)
        def _():
            for load in plan_loads(tile + 1):
                load.start()

        n_pieces = MOE_TILE // LN_ROWS
        store = lambda c: pltpu.make_async_copy(
            stage_s.at[c % 2], o_hbm.at[pl.ds(tile * MOE_TILE + c * LN_ROWS, LN_ROWS), :], sem.at[4 + c % 2])
        for c in range(n_pieces):
            z = _load_row_tiles(acc_s, LN_ROWS, c * LN_ROWS * LANE_CHUNKS)
            if c >= 2:
                store(c - 2).wait()
            stage_s[c % 2] = _layer_norm(z, lw_ref[...], lb_ref[...])
            store(c).start()
        store(n_pieces - 2).wait()
        store(n_pieces - 1).wait()


def _moe_tiles(x1, base, plan_rows, plan_gates, cnt, off, wg, wu, wd, ln_w, ln_b):
    T = x1.shape[0] // LANE_CHUNKS
    w_spec = lambda shape: pl.BlockSpec((None,) + shape, lambda i, e, cnt, off: (e, 0, 0))
    vec = pl.BlockSpec((1, D_MODEL), lambda i, e, cnt, off: (0, 0))
    hbm = pl.BlockSpec(memory_space=pl.ANY)
    tile_rows = MOE_TILE * LANE_CHUNKS
    return pl.pallas_call(
        _moe_tile_kernel,
        grid_spec=pltpu.PrefetchScalarGridSpec(
            num_scalar_prefetch=2,
            grid=(T // MOE_TILE, N_EXPERTS),
            in_specs=[w_spec((D_MODEL, EXPERT_FF)), w_spec((D_MODEL, EXPERT_FF)), w_spec((EXPERT_FF, D_MODEL)),
                      vec, vec, hbm, hbm, hbm, hbm],
            out_specs=hbm,
            scratch_shapes=[pltpu.VMEM((tile_rows, LANES), F32),
                            pltpu.VMEM((tile_rows, LANES), F32)]
                           + [pltpu.VMEM((MOE_CHUNK * LANE_CHUNKS, LANES), F32)] * 6
                           + [pltpu.VMEM((2, LN_ROWS, D_MODEL), F32),
                              pltpu.SMEM((LIST_LEN,), jnp.int32),
                              pltpu.SMEM((LIST_LEN,), F32),
                              pltpu.SemaphoreType.DMA((6,))],
        ),
        out_shape=jax.ShapeDtypeStruct((T, D_MODEL), F32),
        compiler_params=_cparams("arbitrary", "arbitrary"),
        name="moe_tiles_ln2",
    )(cnt, off, wg, wu, wd, ln_w, ln_b, plan_rows, plan_gates, x1, base)


def kernel(x, p, w_in, hgrn_lb_logits, hgrn_norm_w, w_branch_att, w_branch_hgrn, w_out, ln1_w, ln1_b, router_w, router_bias, expert_w_gate, expert_w_up, expert_w_down, shared_w_gate, shared_w_up, shared_w_down, ple_gate_w, ple_proj_w, ln2_w, ln2_b):
    B, S, D = x.shape
    T = B * S
    l = 0
    x2d = x.reshape(T, D)
    bf = lambda a: a.astype(BF16)

    ws = _att_weights(w_in[l])
    qkv = [_proj_att(x2d, ws[g], d) for g, d in enumerate(ATT_DILATIONS)]
    y_att = _attention(qkv, B, S)
    u_hg = _proj(x2d, bf(w_in[l][:, 3 * len(ATT_DILATIONS) * ATT_WIDTH:]), PROJ_COLS)
    y_hg = _hgrn(u_hg, hgrn_lb_logits, hgrn_norm_w[l:l + 1], B, S)
    x1 = _merge(y_att, y_hg, u_hg, x2d, bf(w_branch_att[l]), bf(w_branch_hgrn[l]), bf(w_out[l]),
                ln1_w[l:l + 1], ln1_b[l:l + 1])

    base, idx, gate, rank, counts = _route(
        x1, p[l].reshape(T, PLE_DIM), router_w[l].T, router_bias[l].reshape(N_EXPERTS, 1),
        bf(shared_w_gate[l]), bf(shared_w_up[l]), bf(shared_w_down[l]), bf(ple_gate_w[l]), bf(ple_proj_w[l]))
    cnt = counts[:, :, 0]
    off = jnp.cumsum(cnt, axis=1) - cnt
    cnt, off = cnt.reshape(-1), off.reshape(-1)
    plan_rows, plan_gates = _plan_sc(off, idx, rank, gate)
    out = _moe_tiles(x1, base, plan_rows, plan_gates, cnt, off, bf(expert_w_gate[l]), bf(expert_w_up[l]),
                     bf(expert_w_down[l]), ln2_w[l:l + 1], ln2_b[l:l + 1])
    return out.reshape(B, S, D)
```

```python
import functools

import jax
import jax.numpy as jnp
import numpy as np
from jax import lax
from jax.experimental import pallas as pl
from jax.experimental.pallas import tpu as pltpu
from jax.experimental.pallas import tpu_sc as plsc

F32 = jnp.float32
BF16 = jnp.bfloat16

D_MODEL = 1024
ATT_HEAD_DIM = 64
ATT_HEADS = 8
ATT_DILATIONS = (1, 4, 16)
ATT_BLOCK = 128
ATT_WIDTH = ATT_HEADS * ATT_HEAD_DIM
ATT_TILE = ATT_BLOCK * max(ATT_DILATIONS)
NEG_INF = -1e30
LOG2_E = 1.4426950408889634

HG_HEADS = 8
HG_DIM = 128
HG_WIDTH = HG_HEADS * HG_DIM
HG_CHUNK = 32
HG_TILE = 256
RMS_EPS = 1e-6

N_EXPERTS = 64
TOP_K = 8
N_GROUPS = 8
GROUP_SIZE = N_EXPERTS // N_GROUPS
TOPK_GROUPS = 4
EXPERT_FF = 256
ROUTED_SCALE = 2.5
PLE_DIM = 256
LN_EPS = 1e-5
DEPTH = 1
DEEPNORM_ALPHA = (2.0 * DEPTH) ** 0.25

LANES = 128
LANE_CHUNKS = D_MODEL // LANES
PROJ_ROWS = 512
PROJ_COLS = 1536
ATT_PROJ_ROWS = 1024
MIX_ROWS = 512
MOE_TILE = 4096
MOE_CHUNK = 576
LN_ROWS = 256
LIST_PAD = 1024
LIST_LEN = MOE_TILE * TOP_K + LIST_PAD
GATHER_GROUP = 8
SCATTER_GROUP = 8
V7X_VMEM_LIMIT = 56 * 1024 * 1024
SC_CORES, SC_SUBCORES, SC_LANES = 2, 16, 16


def _cparams(*sem):
    return pltpu.CompilerParams(dimension_semantics=sem, vmem_limit_bytes=V7X_VMEM_LIMIT)


def _proj_att_kernel(*refs, dil):
    x_refs, w_ref, o_ref = refs[:LANE_CHUNKS], refs[LANE_CHUNKS], refs[LANE_CHUNKS + 1]
    n = ATT_PROJ_ROWS // dil

    def rows(ref):
        if dil == 1:
            return ref[...]
        return jnp.concatenate([ref[pl.ds(r, n, stride=dil), :] for r in range(dil)], axis=0)

    xp = jnp.concatenate([rows(ref).astype(BF16) for ref in x_refs], axis=1)
    y = jnp.dot(xp, w_ref[...], preferred_element_type=F32)
    o_ref[...] = y.astype(BF16).reshape(dil, n, 3 * ATT_WIDTH)


def _proj_att(x2d, w, dil):
    T = x2d.shape[0]
    per = ATT_TILE // ATT_PROJ_ROWS
    n = ATT_PROJ_ROWS // dil
    out = pl.pallas_call(
        functools.partial(_proj_att_kernel, dil=dil),
        grid=(T // ATT_PROJ_ROWS,),
        in_specs=[pl.BlockSpec((ATT_PROJ_ROWS, LANES), functools.partial(lambda i, c: (i, c), c=c))
                  for c in range(LANE_CHUNKS)]
                 + [pl.BlockSpec((D_MODEL, 3 * ATT_WIDTH), lambda i: (0, 0))],
        out_specs=pl.BlockSpec((None, dil, None, n, 3 * ATT_WIDTH), lambda i: (i // per, 0, i % per, 0, 0)),
        out_shape=jax.ShapeDtypeStruct((T // ATT_TILE, dil, per, n, 3 * ATT_WIDTH), BF16),
        compiler_params=_cparams("parallel"),
        name=f"proj_att_d{dil}",
    )(*([x2d] * LANE_CHUNKS), w)
    return out.reshape(T // ATT_TILE, dil, ATT_TILE // dil, 3 * ATT_WIDTH)


def _att_pair(q2, kp, kc, vp, vc, bias_ref, g, first):
    def head0_lanes(rows, dtype):
        lane = lax.broadcasted_iota(jnp.int32, (rows, 2 * ATT_HEAD_DIM), 1)
        return lane.astype(F32).astype(dtype) < ATT_HEAD_DIM

    lo_q = head0_lanes(ATT_BLOCK, BF16)
    lo_v = head0_lanes(2 * ATT_BLOCK, BF16)
    k2 = jnp.concatenate([kp, kc], axis=0)
    v2 = jnp.concatenate([vp, vc], axis=0)
    zero = jnp.zeros_like(q2)
    ps, ms = [], []
    for hh in range(2):
        qm = jnp.where(lo_q, q2, zero) if hh == 0 else jnp.where(lo_q, zero, q2)
        s = lax.dot_general(qm, k2, (((1,), (1,)), ((), ())), preferred_element_type=F32)
        s = s + bias_ref[g, hh, first]
        m = jnp.max(s, axis=-1, keepdims=True)
        ps.append(jnp.exp2(s - m).astype(BF16))
        ms.append(m)
    pcat = jnp.concatenate(ps, axis=1)
    zero_v, one_v = jnp.zeros_like(v2), jnp.ones_like(v2)
    rhs = jnp.concatenate([
        jnp.concatenate([jnp.where(lo_v, v2, zero_v), jnp.where(lo_v, one_v, zero_v)], axis=1),
        jnp.concatenate([jnp.where(lo_v, zero_v, v2), jnp.where(lo_v, zero_v, one_v)], axis=1)], axis=0)
    nd = jnp.dot(pcat, rhs, preferred_element_type=F32)
    m2 = jnp.where(head0_lanes(ATT_BLOCK, F32), ms[0], ms[1])
    return nd[:, :2 * ATT_HEAD_DIM], m2, nd[:, 2 * ATT_HEAD_DIM:]


def _att_kernel(*refs):
    (q0, kc0, vc0, kp0, vp0, q1, kc1, vc1, kp1, vp1, q2, kc2, vc2, kp2, vp2,
     bias_ref, o_ref) = refs[:17]
    ng = len(ATT_DILATIONS)
    num_s, m_s, den_s = refs[17:17 + ng], refs[17 + ng:17 + 2 * ng], refs[17 + 2 * ng:]
    first_tile = (pl.program_id(2) == 0).astype(jnp.int32)
    groups = ((q0, kc0, vc0, kp0, vp0), (q1, kc1, vc1, kp1, vp1), (q2, kc2, vc2, kp2, vp2))
    for g, dil in enumerate(ATT_DILATIONS):
        q_ref, kc_ref, vc_ref, kp_ref, vp_ref = groups[g]
        nb = ATT_TILE // dil // ATT_BLOCK
        for r in range(dil):
            for n in range(nb):
                rows = pl.ds(n * ATT_BLOCK, ATT_BLOCK)
                if n == 0:
                    prev = pl.ds((nb - 1) * ATT_BLOCK, ATT_BLOCK)
                    kp, vp, first = kp_ref[r, prev, :], vp_ref[r, prev, :], first_tile
                else:
                    prev = pl.ds((n - 1) * ATT_BLOCK, ATT_BLOCK)
                    kp, vp, first = kc_ref[r, prev, :], vc_ref[r, prev, :], 0
                num, m, den = _att_pair(q_ref[r, rows, :], kp, kc_ref[r, rows, :], vp, vc_ref[r, rows, :],
                                        bias_ref, g, first)
                if dil == 1:
                    dst = rows
                else:
                    dst = pl.ds(n * ATT_BLOCK * dil + r, ATT_BLOCK, stride=dil)
                num_s[g][dst, :] = num
                m_s[g][dst, :] = m
                den_s[g][dst, :] = den
    m_all = jnp.maximum(jnp.maximum(m_s[0][...], m_s[1][...]), m_s[2][...])
    num = jnp.zeros((ATT_TILE, 2 * ATT_HEAD_DIM), F32)
    den = jnp.zeros((ATT_TILE, 2 * ATT_HEAD_DIM), F32)
    for g in range(ng):
        sc = jnp.exp2(m_s[g][...] - m_all)
        num = num + sc * num_s[g][...]
        den = den + sc * den_s[g][...]
    o_ref[...] = (num / den).astype(o_ref.dtype)


def _att_bias_table():
    qi = np.arange(ATT_BLOCK)[:, None]
    ki = np.arange(2 * ATT_BLOCK)[None, :]
    steps = qi + ATT_BLOCK - ki
    valid = (steps >= 0) & (steps <= ATT_BLOCK)
    slopes = np.array([2.0 ** (-8.0 * (h + 1) / ATT_HEADS) for h in range(ATT_HEADS)], np.float32)
    tab = np.empty((len(ATT_DILATIONS), ATT_HEADS, 2, ATT_BLOCK, 2 * ATT_BLOCK), np.float32)
    for g, dil in enumerate(ATT_DILATIONS):
        bias = -slopes[:, None, None] * (steps * dil).astype(np.float32)[None] * LOG2_E
        tab[g, :, 0] = np.where(valid[None], bias, NEG_INF)
        tab[g, :, 1] = np.where((valid & (ki >= ATT_BLOCK))[None], bias, NEG_INF)
    return jnp.asarray(tab)


def _attention(qkv, B, S):
    tiles = S // ATT_TILE
    pair = 2 * ATT_HEAD_DIM
    npair = ATT_WIDTH // pair
    in_specs, args = [], []
    for g, dil in enumerate(ATT_DILATIONS):
        blk = (None, dil, ATT_TILE // dil, pair)
        cur = lambda b, hp, t, off: (b * tiles + t, 0, 0, off * npair + hp)
        prv = lambda b, hp, t, off: (b * tiles + jnp.maximum(t - 1, 0), 0, 0, off * npair + hp)
        in_specs += [pl.BlockSpec(blk, functools.partial(cur, off=0)),
                     pl.BlockSpec(blk, functools.partial(cur, off=1)),
                     pl.BlockSpec(blk, functools.partial(cur, off=2)),
                     pl.BlockSpec(blk, functools.partial(prv, off=1)),
                     pl.BlockSpec(blk, functools.partial(prv, off=2))]
        args += [qkv[g]] * 5
    in_specs.append(pl.BlockSpec((len(ATT_DILATIONS), 2, 2, ATT_BLOCK, 2 * ATT_BLOCK),
                                 lambda b, hp, t: (0, hp, 0, 0, 0)))
    args.append(_att_bias_table())
    scratch = [pltpu.VMEM((ATT_TILE, pair), F32) for _ in range(3 * len(ATT_DILATIONS))]
    return pl.pallas_call(
        _att_kernel,
        grid=(B, npair, tiles),
        in_specs=in_specs,
        out_specs=pl.BlockSpec((ATT_TILE, pair), lambda b, hp, t: (b * tiles + t, hp)),
        out_shape=jax.ShapeDtypeStruct((B * S, ATT_WIDTH), BF16),
        scratch_shapes=scratch,
        compiler_params=_cparams("parallel", "parallel", "arbitrary"),
        name="dilated_attention",
    )(*args)


def _att_weights(w_in_l):
    out = []
    width = len(ATT_DILATIONS) * ATT_WIDTH
    for g in range(len(ATT_DILATIONS)):
        cols = [w_in_l[:, part * width + g * ATT_WIDTH: part * width + (g + 1) * ATT_WIDTH] for part in range(3)]
        cols[0] = cols[0] * (ATT_HEAD_DIM ** -0.5 * LOG2_E)
        out.append(jnp.concatenate(cols, axis=1).astype(BF16))
    return out


def _proj_kernel(x_ref, w_ref, o_ref, *, col_tile):
    xb = x_ref[...].astype(BF16)
    for c in range(w_ref.shape[1] // col_tile):
        cols = slice(c * col_tile, (c + 1) * col_tile)
        o_ref[:, cols] = jnp.dot(xb, w_ref[:, cols], preferred_element_type=F32).astype(o_ref.dtype)


def _proj(x2d, w, col_tile):
    T, N = x2d.shape[0], w.shape[1]
    return pl.pallas_call(
        functools.partial(_proj_kernel, col_tile=col_tile),
        grid=(T // PROJ_ROWS,),
        in_specs=[pl.BlockSpec((PROJ_ROWS, D_MODEL), lambda i: (i, 0)),
                  pl.BlockSpec((D_MODEL, N), lambda i: (0, 0))],
        out_specs=pl.BlockSpec((PROJ_ROWS, N), lambda i: (i, 0)),
        out_shape=jax.ShapeDtypeStruct((T, N), BF16),
        compiler_params=_cparams("parallel"),
        name="proj_hgrn_gates",
    )(x2d, w)


def _split2(v):
    a = v.astype(BF16)
    return a, (v - a.astype(F32)).astype(BF16)


def _hgrn_kernel(q_ref, f_ref, i_ref, g_ref, lbl_ref, gain_ref, o_ref, state_ref):
    @pl.when(pl.program_id(1) == 0)
    def _():
        state_ref[...] = jnp.zeros_like(state_ref)

    lbl = lbl_ref[...]
    e = jnp.exp(lbl - jnp.max(lbl, axis=0, keepdims=True))
    lb = e[0:1] / jnp.sum(e, axis=0, keepdims=True)
    forget = lb + (1.0 - lb) * jax.nn.sigmoid(f_ref[...].astype(F32))
    log_f = jnp.log(forget)
    key = 1.0 - forget

    row = lax.broadcasted_iota(jnp.int32, (HG_TILE, HG_TILE), 0)
    col = lax.broadcasted_iota(jnp.int32, (HG_TILE, HG_TILE), 1)
    causal = (row >= col) & ((row // HG_CHUNK) == (col // HG_CHUNK))
    tri = jnp.where(causal, 1.0, 0.0).astype(BF16)
    b = sum(jnp.dot(tri, t, preferred_element_type=F32) for t in _split2(log_f))
    eb = jnp.exp(b)
    q_dec = (q_ref[...].astype(F32) * eb).astype(BF16)
    k_inv = key * jnp.exp(-b)
    xi = i_ref[...].astype(F32)
    val = (xi * jax.nn.sigmoid(xi)).astype(BF16)
    k_inv_b = k_inv.astype(BF16)

    n_chunks = HG_TILE // HG_CHUNK
    last_rows = [eb[(c + 1) * HG_CHUNK - 1:(c + 1) * HG_CHUNK, :] for c in range(n_chunks)]
    dec_rows = jnp.concatenate([jnp.broadcast_to(r, (HG_CHUNK, HG_WIDTH)) for r in last_rows], axis=0)
    k_end = (k_inv * dec_rows).astype(BF16)
    def per_chunk_columns(t):
        blocks = []
        for c in range(n_chunks):
            rows_above, rows_below = c * HG_CHUNK, HG_TILE - (c + 1) * HG_CHUNK
            parts = [t[rows_above:rows_above + HG_CHUNK]]
            if rows_above:
                parts.insert(0, jnp.zeros((rows_above, HG_DIM), t.dtype))
            if rows_below:
                parts.append(jnp.zeros((rows_below, HG_DIM), t.dtype))
            blocks.append(jnp.concatenate(parts, axis=0))
        return jnp.concatenate(blocks, axis=1)

    head_cols = [slice(h * HG_DIM, (h + 1) * HG_DIM) for h in range(HG_HEADS)]
    upds = [lax.dot_general(val[:, cols], per_chunk_columns(k_end[:, cols]), (((0,), (0,)), ((), ())),
                            preferred_element_type=F32) for cols in head_cols]
    o_intras = []
    for cols in head_cols:
        a = lax.dot_general(q_dec[:, cols], k_inv_b[:, cols], (((1,), (1,)), ((), ())), preferred_element_type=F32)
        a = jnp.where(causal, a, 0.0).astype(BF16)
        o_intras.append(jnp.dot(a, val[:, cols], preferred_element_type=F32))
    enterings = []
    for h, cols in enumerate(head_cols):
        st = state_ref[h]
        entering = []
        for c in range(n_chunks):
            entering.append(st.astype(BF16))
            st = st * last_rows[c][:, cols] + upds[h][:, c * HG_DIM:(c + 1) * HG_DIM]
        state_ref[h] = st
        enterings.append(jnp.concatenate(entering, axis=1))
    outs = []
    for h, cols in enumerate(head_cols):
        qd = q_dec[:, cols]
        o_inter = lax.dot_general(per_chunk_columns(qd), enterings[h],
                                  (((1,), (1,)), ((), ())), preferred_element_type=F32)
        o = o_intras[h] + o_inter
        o = o * lax.rsqrt(jnp.mean(jnp.square(o), axis=-1, keepdims=True) + RMS_EPS)
        outs.append(o)
    o = jnp.concatenate(outs, axis=1) * gain_ref[...]
    gg = g_ref[...].astype(F32)
    o_ref[...] = (o * (gg * jax.nn.sigmoid(gg))).astype(o_ref.dtype)


def _hgrn(u_hg, lb_logits, gain, B, S):
    tiles = S // HG_TILE
    col = lambda j: pl.BlockSpec((HG_TILE, HG_WIDTH), functools.partial(lambda b, t, j: (b * tiles + t, j), j=j))
    return pl.pallas_call(
        _hgrn_kernel,
        grid=(B, tiles),
        in_specs=[col(0), col(1), col(2), col(3),
                  pl.BlockSpec((2, HG_WIDTH), lambda b, t: (0, 0)),
                  pl.BlockSpec((1, HG_WIDTH), lambda b, t: (0, 0))],
        out_specs=pl.BlockSpec((HG_TILE, HG_WIDTH), lambda b, t: (b * tiles + t, 0)),
        out_shape=jax.ShapeDtypeStruct((B * S, HG_WIDTH), BF16),
        scratch_shapes=[pltpu.VMEM((HG_HEADS, HG_DIM, HG_DIM), F32)],
        compiler_params=_cparams("parallel", "arbitrary"),
        name="hgrn2",
    )(u_hg, u_hg, u_hg, u_hg, lb_logits, gain)


def _load_row_tiles(ref, n, start=0):
    return jnp.concatenate([ref[pl.ds(start + c, n, stride=LANE_CHUNKS), :] for c in range(LANE_CHUNKS)], axis=1)


def _store_row_tiles(ref, val, n):
    for c in range(LANE_CHUNKS):
        ref[pl.ds(c, n, stride=LANE_CHUNKS), :] = val[:, c * LANES:(c + 1) * LANES]


def _layer_norm(z, w, b):
    mu = jnp.mean(z, axis=-1, keepdims=True)
    zc = z - mu
    var = jnp.mean(jnp.square(zc), axis=-1, keepdims=True)
    return zc * lax.rsqrt(var + LN_EPS) * w + b


def _merge_kernel(ya_ref, yh_ref, ga_ref, gh_ref, x_ref, wa_ref, wh_ref, wo_ref, lw_ref, lb_ref, o_ref):
    ma = jnp.dot(ya_ref[...], wa_ref[...], preferred_element_type=F32)
    mh = jnp.dot(yh_ref[...], wh_ref[...], preferred_element_type=F32)
    merged = (jax.nn.sigmoid(ga_ref[...].astype(F32)) * ma + jax.nn.sigmoid(gh_ref[...].astype(F32)) * mh)
    z = DEEPNORM_ALPHA * x_ref[...] + jnp.dot(merged.astype(BF16), wo_ref[...], preferred_element_type=F32)
    _store_row_tiles(o_ref, _layer_norm(z, lw_ref[...], lb_ref[...]), MIX_ROWS)


def _merge(y_att, y_hg, u_hg, x2d, w_a, w_h, w_o, ln_w, ln_b):
    T = x2d.shape[0]
    rows = lambda width, j=0: pl.BlockSpec((MIX_ROWS, width), functools.partial(lambda i, j: (i, j), j=j))
    full = lambda a: pl.BlockSpec(a.shape, lambda i: (0, 0))
    return pl.pallas_call(
        _merge_kernel,
        grid=(T // MIX_ROWS,),
        in_specs=[rows(ATT_WIDTH), rows(HG_WIDTH), rows(D_MODEL, 4), rows(D_MODEL, 5), rows(D_MODEL),
                  full(w_a), full(w_h), full(w_o), full(ln_w), full(ln_b)],
        out_specs=pl.BlockSpec((MIX_ROWS * LANE_CHUNKS, LANES), lambda i: (i, 0)),
        out_shape=jax.ShapeDtypeStruct((T * LANE_CHUNKS, LANES), F32),
        compiler_params=_cparams("parallel"),
        name="merge_ln1",
    )(y_att, y_hg, u_hg, u_hg, x2d, w_a, w_h, w_o, ln_w, ln_b)


def _first_argmax(v, ids, n):
    mx = jnp.max(v, axis=0, keepdims=True)
    return mx, jnp.min(jnp.where(v == mx, ids, n), axis=0, keepdims=True)


def _route_kernel(x1_ref, p_ref, wrt_ref, rb_ref, wsg_ref, wsu_ref, wsd_ref, wpg_ref, wpp_ref,
                  base_ref, idx_ref, gate_ref, rank_ref, cnt_ref, carry_ref):
    @pl.when(pl.program_id(0) % (MOE_TILE // MIX_ROWS) == 0)
    def _():
        carry_ref[...] = jnp.zeros_like(carry_ref)

    x1 = _load_row_tiles(x1_ref, MIX_ROWS)
    x1b = x1.astype(BF16)
    logits = lax.dot_general(wrt_ref[...], x1, (((1,), (1,)), ((), ())), preferred_element_type=F32,
                             precision=lax.Precision.HIGHEST)
    hg = jnp.dot(x1b, wsg_ref[...], preferred_element_type=F32)
    hu = jnp.dot(x1b, wsu_ref[...], preferred_element_type=F32)
    ple_gate = jnp.dot(x1b, wpg_ref[...], preferred_element_type=F32)
    ple_proj = jnp.dot(p_ref[...].astype(BF16), wpp_ref[...], preferred_element_type=F32)

    s = jax.nn.sigmoid(logits)
    sel = s + rb_ref[...]
    eid = lax.broadcasted_iota(jnp.int32, (N_EXPERTS, MIX_ROWS), 0)
    neg = -jnp.inf

    grp = sel.reshape(N_GROUPS, GROUP_SIZE, MIX_ROWS)
    mid = lax.broadcasted_iota(jnp.int32, grp.shape, 1)
    m1 = jnp.max(grp, axis=1, keepdims=True)
    i1 = jnp.min(jnp.where(grp == m1, mid, GROUP_SIZE), axis=1, keepdims=True)
    m2 = jnp.max(jnp.where(mid == i1, neg, grp), axis=1, keepdims=True)
    gscore = (m1 + m2).reshape(N_GROUPS, MIX_ROWS)
    gid = lax.broadcasted_iota(jnp.int32, (N_GROUPS, MIX_ROWS), 0)
    gsel = jnp.zeros((N_GROUPS, MIX_ROWS), jnp.bool_)
    for _ in range(TOPK_GROUPS):
        _, gi = _first_argmax(gscore, gid, N_GROUPS)
        hit = gid == gi
        gsel = gsel | hit
        gscore = jnp.where(hit, neg, gscore)
    emask = jnp.broadcast_to(gsel.reshape(N_GROUPS, 1, MIX_ROWS), grp.shape).reshape(N_EXPERTS, MIX_ROWS)
    cand = jnp.where(emask, sel, neg)

    idxs, gates = [], []
    chosen = jnp.zeros((N_EXPERTS, MIX_ROWS), jnp.bool_)
    for _ in range(TOP_K):
        _, ei = _first_argmax(cand, eid, N_EXPERTS)
        hit = eid == ei
        idxs.append(ei)
        gates.append(jnp.sum(jnp.where(hit, s, 0.0), axis=0, keepdims=True))
        chosen = chosen | hit
        cand = jnp.where(hit, neg, cand)
    g = jnp.concatenate(gates, axis=0)
    g = g / jnp.sum(g, axis=0, keepdims=True) * ROUTED_SCALE
    idx_ref[...] = jnp.concatenate(idxs, axis=0)
    gate_ref[...] = g

    onehot = jnp.where(chosen, 1.0, 0.0)
    tr = lax.broadcasted_iota(jnp.int32, (MIX_ROWS, MIX_ROWS), 0)
    tc = lax.broadcasted_iota(jnp.int32, (MIX_ROWS, MIX_ROWS), 1)
    before = jnp.where(tr < tc, 1.0, 0.0).astype(BF16)
    prefix = jnp.dot(onehot.astype(BF16), before, preferred_element_type=F32)
    rankfull = (carry_ref[:, 0:1] + prefix).astype(jnp.int32)
    rank_ref[...] = jnp.concatenate(
        [jnp.sum(jnp.where(eid == ei, rankfull, 0), axis=0, keepdims=True) for ei in idxs], axis=0)
    total = carry_ref[...] + jnp.sum(onehot, axis=1, keepdims=True)
    carry_ref[...] = total
    cnt_ref[...] = total.astype(jnp.int32)

    shared = jnp.dot((hg * jax.nn.sigmoid(hg) * hu).astype(BF16), wsd_ref[...], preferred_element_type=F32)
    ple = jax.nn.sigmoid(ple_gate) * ple_proj
    _store_row_tiles(base_ref, DEEPNORM_ALPHA * x1 + shared + ple, MIX_ROWS)


def _route(x1, p2d, wr_t, rbias, wsg, wsu, wsd, wpg, wpp):
    T = x1.shape[0] // LANE_CHUNKS
    per_tile = MOE_TILE // MIX_ROWS
    full = lambda a: pl.BlockSpec(a.shape, lambda i: (0, 0))
    tok = pl.BlockSpec((TOP_K, MIX_ROWS), lambda i: (0, i))
    row_tiles = pl.BlockSpec((MIX_ROWS * LANE_CHUNKS, LANES), lambda i: (i, 0))
    return pl.pallas_call(
        _route_kernel,
        grid=(T // MIX_ROWS,),
        in_specs=[row_tiles,
                  pl.BlockSpec((MIX_ROWS, PLE_DIM), lambda i: (i, 0)),
                  full(wr_t), full(rbias), full(wsg), full(wsu), full(wsd), full(wpg), full(wpp)],
        out_specs=[row_tiles, tok, tok, tok,
                   pl.BlockSpec((None, N_EXPERTS, LANES), lambda i: (i // per_tile, 0, 0))],
        out_shape=[jax.ShapeDtypeStruct((T * LANE_CHUNKS, LANES), F32),
                   jax.ShapeDtypeStruct((TOP_K, T), jnp.int32),
                   jax.ShapeDtypeStruct((TOP_K, T), F32),
                   jax.ShapeDtypeStruct((TOP_K, T), jnp.int32),
                   jax.ShapeDtypeStruct((T // MOE_TILE, N_EXPERTS, LANES), jnp.int32)],
        scratch_shapes=[pltpu.VMEM((N_EXPERTS, LANES), F32)],
        compiler_params=_cparams("arbitrary"),
        name="route_shared_ple",
    )(x1, p2d, wr_t, rbias, wsg, wsu, wsd, wpg, wpp)


def _plan_sc_kernel(off_hbm, idx_hbm, rank_hbm, gate_hbm, rows_hbm, gates_hbm,
                    off_v, idx_v, rank_v, gate_v, rows_v, gates_v):
    n_tokens = idx_hbm.shape[0] // TOP_K
    worker = lax.axis_index("subcore") * SC_CORES + lax.axis_index("core")

    @pl.when(worker < n_tokens // MOE_TILE)
    def _():
        pltpu.sync_copy(off_hbm.at[pl.ds(worker * N_EXPERTS, N_EXPERTS)], off_v)
        lane = lax.iota(jnp.int32, SC_LANES)
        for k in range(TOP_K):
            row = pl.ds(k * n_tokens + worker * MOE_TILE, MOE_TILE)
            pltpu.sync_copy(idx_hbm.at[row], idx_v)
            pltpu.sync_copy(rank_hbm.at[row], rank_v)
            pltpu.sync_copy(gate_hbm.at[row], gate_v)

            @pl.loop(0, MOE_TILE // SC_LANES)
            def _(i):
                at = i * SC_LANES
                pos = plsc.load_gather(off_v, [idx_v[pl.ds(at, SC_LANES)]]) + rank_v[pl.ds(at, SC_LANES)]
                plsc.store_scatter(rows_v, [pos], (lane + at) * LANE_CHUNKS)
                plsc.store_scatter(gates_v, [pos], gate_v[pl.ds(at, SC_LANES)])

        @pl.loop(0, LIST_PAD // SC_LANES)
        def _(i):
            tail = pl.ds(MOE_TILE * TOP_K + i * SC_LANES, SC_LANES)
            rows_v[tail] = jnp.zeros((SC_LANES,), jnp.int32)
            gates_v[tail] = jnp.zeros((SC_LANES,), F32)

        out = pl.ds(worker * LIST_LEN, LIST_LEN)
        pltpu.sync_copy(rows_v, rows_hbm.at[out])
        pltpu.sync_copy(gates_v, gates_hbm.at[out])


def _plan_sc(off, idx, rank, gate):
    n_tiles = idx.shape[1] // MOE_TILE
    assert n_tiles <= SC_CORES * SC_SUBCORES
    mesh = plsc.VectorSubcoreMesh(core_axis_name="core", subcore_axis_name="subcore",
                                  num_cores=SC_CORES, num_subcores=SC_SUBCORES)
    return pl.kernel(
        _plan_sc_kernel,
        out_type=(jax.ShapeDtypeStruct((n_tiles * LIST_LEN,), jnp.int32),
                  jax.ShapeDtypeStruct((n_tiles * LIST_LEN,), F32)),
        mesh=mesh,
        scratch_types=[pltpu.VMEM((N_EXPERTS,), jnp.int32), pltpu.VMEM((MOE_TILE,), jnp.int32),
                       pltpu.VMEM((MOE_TILE,), jnp.int32), pltpu.VMEM((MOE_TILE,), F32),
                       pltpu.VMEM((LIST_LEN,), jnp.int32), pltpu.VMEM((LIST_LEN,), F32)],
        compiler_params=pltpu.CompilerParams(needs_layout_passes=False),
        name="moe_plan_sc",
    )(off, idx.reshape(-1), rank.reshape(-1), gate.reshape(-1))


def _moe_tile_kernel(cnt_ref, off_ref, wg_ref, wu_ref, wd_ref, lw_ref, lb_ref, rows_hbm, gates_hbm, x_hbm, base_hbm,
                     o_hbm, x_s, acc_s, xg_a, xg_b, xg_c, y_a, y_b, y_c, stage_s, rows_s, gates_s, sem):
    tile, e = pl.program_id(0), pl.program_id(1)
    rows_of = lambda ref, r, n: ref.at[pl.ds(pl.multiple_of(r * LANE_CHUNKS, LANE_CHUNKS), n * LANE_CHUNKS), :]
    tile_rows = pl.ds(pl.multiple_of(tile * (MOE_TILE * LANE_CHUNKS), LANE_CHUNKS), MOE_TILE * LANE_CHUNKS)

    pair = tile * N_EXPERTS + e
    last_pair = pl.num_programs(0) * N_EXPERTS - 1
    n, off = cnt_ref[pair], off_ref[pair]
    tile_at = lambda ref, r: ref.at[pl.ds(pl.multiple_of(r, LANE_CHUNKS), LANE_CHUNKS), :]

    def gather_group(xg, first, jb):
        at = first + jb * GATHER_GROUP
        rows = [tile_at(x_s, rows_s[at + u])[...] for u in range(GATHER_GROUP)]
        rows_of(xg, jb * GATHER_GROUP, GATHER_GROUP)[...] = jnp.concatenate(rows, axis=0)

    def gather_loop(xg, first):
        def body(jb, cc):
            gather_group(xg, first, jb)
            return cc

        lax.fori_loop(0, MOE_CHUNK // GATHER_GROUP, body, 0)

    def swiglu(xg, y):
        xb = _load_row_tiles(xg, MOE_CHUNK).astype(BF16)
        hg = jnp.dot(xb, wg_ref[...], preferred_element_type=F32)
        hu = jnp.dot(xb, wu_ref[...], preferred_element_type=F32)
        act = (hg * jax.nn.sigmoid(hg) * hu).astype(BF16)
        _store_row_tiles(y, jnp.dot(act, wd_ref[...], preferred_element_type=F32), MOE_CHUNK)

    def scatter_group(y, first, j0, live):
        dsts = [rows_s[first + j0 + u] for u in range(live)]
        gates = [gates_s[first + j0 + u] for u in range(live)]
        yv = rows_of(y, j0, live)[...]
        vals = [tile_at(acc_s, d)[...] + g * yv[u * LANE_CHUNKS:(u + 1) * LANE_CHUNKS]
                for u, (d, g) in enumerate(zip(dsts, gates))]
        for d, val in reversed(list(zip(dsts, vals))):
            tile_at(acc_s, d)[...] = val

    def scatter_loop(y, first, m):
        def body(jg, cc):
            scatter_group(y, first, jg * SCATTER_GROUP, SCATTER_GROUP)
            return cc

        lax.fori_loop(0, m // SCATTER_GROUP, body, 0)
        for live in range(1, SCATTER_GROUP):
            @pl.when(m % SCATTER_GROUP == live)
            def _(live=live):
                scatter_group(y, first, m - live, live)

    def plan_loads(t):
        plan = pl.ds(pl.multiple_of(t * LIST_LEN, LIST_PAD), LIST_LEN)
        return (pltpu.make_async_copy(rows_hbm.at[plan], rows_s, sem.at[0]),
                pltpu.make_async_copy(gates_hbm.at[plan], gates_s, sem.at[1]))

    def x_load(t):
        rows = pl.ds(pl.multiple_of(t * (MOE_TILE * LANE_CHUNKS), LANE_CHUNKS), MOE_TILE * LANE_CHUNKS)
        return pltpu.make_async_copy(x_hbm.at[rows, :], x_s, sem.at[2])

    @pl.when(e == 0)
    def _():
        @pl.when(tile == 0)
        def _():
            x_load(tile).start()
            for load in plan_loads(tile):
                load.start()

        load_base = pltpu.make_async_copy(base_hbm.at[tile_rows, :], acc_s, sem.at[3])
        load_base.start()
        y_b[...] = jnp.zeros_like(y_b)
        load_rows, load_gates = plan_loads(tile)
        load_rows.wait()
        x_load(tile).wait()
        gather_loop(xg_a, off)
        load_gates.wait()
        load_base.wait()

    prev_off = off_ref[jnp.maximum(pair - 1, 0)]
    next_off = off_ref[jnp.minimum(pair + 1, last_pair)]
    live_row = lax.broadcasted_iota(jnp.int32, (MOE_CHUNK, 2 * LANES), 0) < n

    def run_expert(xg_cur, y_cur, xg_nxt, y_prv):
        gathers = [functools.partial(gather_group, xg_nxt, next_off, jb) for jb in range(MOE_CHUNK // GATHER_GROUP)]
        scatters = [functools.partial(scatter_group, y_prv, prev_off, jg * SCATTER_GROUP, SCATTER_GROUP)
                    for jg in range(MOE_CHUNK // SCATTER_GROUP)]
        side = [s for both in zip(gathers, scatters) for s in both]
        n_down = LANE_CHUNKS // 2
        cost = [D_MODEL, D_MODEL] + [EXPERT_FF] * n_down
        bounds = [round(len(side) * sum(cost[:i]) / sum(cost)) for i in range(len(cost) + 1)]

        def side_work(i):
            for s in side[bounds[i]:bounds[i + 1]]:
                s()

        xb = _load_row_tiles(xg_cur, MOE_CHUNK).astype(BF16)
        side_work(0)
        hg = jnp.dot(xb, wg_ref[...], preferred_element_type=F32)
        side_work(1)
        hu = jnp.dot(xb, wu_ref[...], preferred_element_type=F32)
        act = (hg * jax.nn.sigmoid(hg) * hu).astype(BF16)
        for q in range(n_down):
            side_work(2 + q)
            out = jnp.dot(act, wd_ref[:, q * 2 * LANES:(q + 1) * 2 * LANES], preferred_element_type=F32)
            out = jnp.where(live_row, out, 0.0)
            for c in range(2):
                y_cur[pl.ds(2 * q + c, MOE_CHUNK, stride=LANE_CHUNKS), :] = out[:, c * LANES:(c + 1) * LANES]

    @pl.when(e % 2 == 0)
    def _():
        run_expert(xg_a, y_a, xg_b, y_b)

    @pl.when(e % 2 == 1)
    def _():
        run_expert(xg_b, y_b, xg_a, y_a)

    def extra_chunk(c, carry):
        first = off + c * MOE_CHUNK
        gather_loop(xg_c, first)
        swiglu(xg_c, y_c)
        scatter_loop(y_c, first, jnp.minimum(MOE_CHUNK, n - c * MOE_CHUNK))
        return carry

    lax.fori_loop(1, (n + MOE_CHUNK - 1) // MOE_CHUNK, extra_chunk, 0)

    @pl.when(e == N_EXPERTS - 1)
    def _():
        more_tiles = tile + 1 < pl.num_programs(0)

        @pl.when(more_tiles)
        def _():
            x_load(tile + 1).start()

        scatter_loop(y_b, off, jnp.minimum(MOE_CHUNK, n))

        @pl.when(more_tiles)
        def _():
            for load in plan_loads(tile + 1):
                load.start()

        n_pieces = MOE_TILE // LN_ROWS
        store = lambda c: pltpu.make_async_copy(
            stage_s.at[c % 2], o_hbm.at[pl.ds(tile * MOE_TILE + c * LN_ROWS, LN_ROWS), :], sem.at[4 + c % 2])
        for c in range(n_pieces):
            z = _load_row_tiles(acc_s, LN_ROWS, c * LN_ROWS * LANE_CHUNKS)
            if c >= 2:
                store(c - 2).wait()
            stage_s[c % 2] = _layer_norm(z, lw_ref[...], lb_ref[...])
            store(c).start()
        store(n_pieces - 2).wait()
        store(n_pieces - 1).wait()


def _moe_tiles(x1, base, plan_rows, plan_gates, cnt, off, wg, wu, wd, ln_w, ln_b):
    T = x1.shape[0] // LANE_CHUNKS
    w_spec = lambda shape: pl.BlockSpec((None,) + shape, lambda i, e, cnt, off: (e, 0, 0))
    vec = pl.BlockSpec((1, D_MODEL), lambda i, e, cnt, off: (0, 0))
    hbm = pl.BlockSpec(memory_space=pl.ANY)
    tile_rows = MOE_TILE * LANE_CHUNKS
    return pl.pallas_call(
        _moe_tile_kernel,
        grid_spec=pltpu.PrefetchScalarGridSpec(
            num_scalar_prefetch=2,
            grid=(T // MOE_TILE, N_EXPERTS),
            in_specs=[w_spec((D_MODEL, EXPERT_FF)), w_spec((D_MODEL, EXPERT_FF)), w_spec((EXPERT_FF, D_MODEL)),
                      vec, vec, hbm, hbm, hbm, hbm],
            out_specs=hbm,
            scratch_shapes=[pltpu.VMEM((tile_rows, LANES), F32),
                            pltpu.VMEM((tile_rows, LANES), F32)]
                           + [pltpu.VMEM((MOE_CHUNK * LANE_CHUNKS, LANES), F32)] * 6
                           + [pltpu.VMEM((2, LN_ROWS, D_MODEL), F32),
                              pltpu.SMEM((LIST_LEN,), jnp.int32),
                              pltpu.SMEM((LIST_LEN,), F32),
                              pltpu.SemaphoreType.DMA((6,))],
        ),
        out_shape=jax.ShapeDtypeStruct((T, D_MODEL), F32),
        compiler_params=_cparams("arbitrary", "arbitrary"),
        name="moe_tiles_ln2",
    )(cnt, off, wg, wu, wd, ln_w, ln_b, plan_rows, plan_gates, x1, base)


def kernel(x, p, w_in, hgrn_lb_logits, hgrn_norm_w, w_branch_att, w_branch_hgrn, w_out, ln1_w, ln1_b, router_w, router_bias, expert_w_gate, expert_w_up, expert_w_down, shared_w_gate, shared_w_up, shared_w_down, ple_gate_w, ple_proj_w, ln2_w, ln2_b):
    B, S, D = x.shape
    T = B * S
    l = 0
    x2d = x.reshape(T, D)
    bf = lambda a: a.astype(BF16)

    ws = _att_weights(w_in[l])
    qkv = [_proj_att(x2d, ws[g], d) for g, d in enumerate(ATT_DILATIONS)]
    y_att = _attention(qkv, B, S)
    u_hg = _proj(x2d, bf(w_in[l][:, 3 * len(ATT_DILATIONS) * ATT_WIDTH:]), PROJ_COLS)
    y_hg = _hgrn(u_hg, hgrn_lb_logits, hgrn_norm_w[l:l + 1], B, S)
    x1 = _merge(y_att, y_hg, u_hg, x2d, bf(w_branch_att[l]), bf(w_branch_hgrn[l]), bf(w_out[l]),
                ln1_w[l:l + 1], ln1_b[l:l + 1])

    base, idx, gate, rank, counts = _route(
        x1, p[l].reshape(T, PLE_DIM), router_w[l].T, router_bias[l].reshape(N_EXPERTS, 1),
        bf(shared_w_gate[l]), bf(shared_w_up[l]), bf(shared_w_down[l]), bf(ple_gate_w[l]), bf(ple_proj_w[l]))
    cnt = counts[:, :, 0]
    off = jnp.cumsum(cnt, axis=1) - cnt
    cnt, off = cnt.reshape(-1), off.reshape(-1)
    plan_rows, plan_gates = _plan_sc(off, idx, rank, gate)
    out = _moe_tiles(x1, base, plan_rows, plan_gates, cnt, off, bf(expert_w_gate[l]), bf(expert_w_up[l]),
                     bf(expert_w_down[l]), ln2_w[l:l + 1], ln2_b[l:l + 1])
    return out.reshape(B, S, D)
```

```python
import functools

import jax
import jax.numpy as jnp
import numpy as np
from jax import lax
from jax.experimental import pallas as pl
from jax.experimental.pallas import tpu as pltpu
from jax.experimental.pallas import tpu_sc as plsc

F32 = jnp.float32
BF16 = jnp.bfloat16

D_MODEL = 1024
ATT_HEAD_DIM = 64
ATT_HEADS = 8
ATT_DILATIONS = (1, 4, 16)
ATT_BLOCK = 128
ATT_WIDTH = ATT_HEADS * ATT_HEAD_DIM
ATT_TILE = ATT_BLOCK * max(ATT_DILATIONS)
NEG_INF = -1e30
LOG2_E = 1.4426950408889634

HG_HEADS = 8
HG_DIM = 128
HG_WIDTH = HG_HEADS * HG_DIM
HG_CHUNK = 32
HG_TILE = 256
RMS_EPS = 1e-6

N_EXPERTS = 64
TOP_K = 8
N_GROUPS = 8
GROUP_SIZE = N_EXPERTS // N_GROUPS
TOPK_GROUPS = 4
EXPERT_FF = 256
ROUTED_SCALE = 2.5
PLE_DIM = 256
LN_EPS = 1e-5
DEPTH = 1
DEEPNORM_ALPHA = (2.0 * DEPTH) ** 0.25

LANES = 128
LANE_CHUNKS = D_MODEL // LANES
PROJ_ROWS = 512
PROJ_COLS = 1536
ATT_PROJ_ROWS = 1024
MIX_ROWS = 512
MOE_TILE = 4096
MOE_CHUNK = 576
GATE_SEG = 640
LN_ROWS = 256
LIST_PAD = 1024
LIST_LEN = MOE_TILE * TOP_K + LIST_PAD
GATHER_GROUP = 8
SCATTER_GROUP = 8
V7X_VMEM_LIMIT = 56 * 1024 * 1024
SC_CORES, SC_SUBCORES, SC_LANES = 2, 16, 16


def _cparams(*sem):
    return pltpu.CompilerParams(dimension_semantics=sem, vmem_limit_bytes=V7X_VMEM_LIMIT)


def _proj_att_kernel(*refs, dil):
    x_refs, w_ref, o_ref = refs[:LANE_CHUNKS], refs[LANE_CHUNKS], refs[LANE_CHUNKS + 1]
    n = ATT_PROJ_ROWS // dil

    def rows(ref):
        if dil == 1:
            return ref[...]
        return jnp.concatenate([ref[pl.ds(r, n, stride=dil), :] for r in range(dil)], axis=0)

    xp = jnp.concatenate([rows(ref).astype(BF16) for ref in x_refs], axis=1)
    y = jnp.dot(xp, w_ref[...], preferred_element_type=F32)
    o_ref[...] = y.astype(BF16).reshape(dil, n, 3 * ATT_WIDTH)


def _proj_att(x2d, w, dil):
    T = x2d.shape[0]
    per = ATT_TILE // ATT_PROJ_ROWS
    n = ATT_PROJ_ROWS // dil
    out = pl.pallas_call(
        functools.partial(_proj_att_kernel, dil=dil),
        grid=(T // ATT_PROJ_ROWS,),
        in_specs=[pl.BlockSpec((ATT_PROJ_ROWS, LANES), functools.partial(lambda i, c: (i, c), c=c))
                  for c in range(LANE_CHUNKS)]
                 + [pl.BlockSpec((D_MODEL, 3 * ATT_WIDTH), lambda i: (0, 0))],
        out_specs=pl.BlockSpec((None, dil, None, n, 3 * ATT_WIDTH), lambda i: (i // per, 0, i % per, 0, 0)),
        out_shape=jax.ShapeDtypeStruct((T // ATT_TILE, dil, per, n, 3 * ATT_WIDTH), BF16),
        compiler_params=_cparams("parallel"),
        name=f"proj_att_d{dil}",
    )(*([x2d] * LANE_CHUNKS), w)
    return out.reshape(T // ATT_TILE, dil, ATT_TILE // dil, 3 * ATT_WIDTH)


def _att_pair(q2, kp, kc, vp, vc, bias_ref, g, first):
    def head0_lanes(rows, dtype):
        lane = lax.broadcasted_iota(jnp.int32, (rows, 2 * ATT_HEAD_DIM), 1)
        return lane.astype(F32).astype(dtype) < ATT_HEAD_DIM

    lo_q = head0_lanes(ATT_BLOCK, BF16)
    lo_v = head0_lanes(2 * ATT_BLOCK, BF16)
    k2 = jnp.concatenate([kp, kc], axis=0)
    v2 = jnp.concatenate([vp, vc], axis=0)
    zero = jnp.zeros_like(q2)
    ps, ms = [], []
    for hh in range(2):
        qm = jnp.where(lo_q, q2, zero) if hh == 0 else jnp.where(lo_q, zero, q2)
        s = lax.dot_general(qm, k2, (((1,), (1,)), ((), ())), preferred_element_type=F32)
        s = s + bias_ref[g, hh, first]
        m = jnp.max(s, axis=-1, keepdims=True)
        ps.append(jnp.exp2(s - m).astype(BF16))
        ms.append(m)
    pcat = jnp.concatenate(ps, axis=1)
    zero_v, one_v = jnp.zeros_like(v2), jnp.ones_like(v2)
    rhs = jnp.concatenate([
        jnp.concatenate([jnp.where(lo_v, v2, zero_v), jnp.where(lo_v, one_v, zero_v)], axis=1),
        jnp.concatenate([jnp.where(lo_v, zero_v, v2), jnp.where(lo_v, zero_v, one_v)], axis=1)], axis=0)
    nd = jnp.dot(pcat, rhs, preferred_element_type=F32)
    m2 = jnp.where(head0_lanes(ATT_BLOCK, F32), ms[0], ms[1])
    return nd[:, :2 * ATT_HEAD_DIM], m2, nd[:, 2 * ATT_HEAD_DIM:]


def _att_kernel(*refs):
    (q0, kc0, vc0, kp0, vp0, q1, kc1, vc1, kp1, vp1, q2, kc2, vc2, kp2, vp2,
     bias_ref, o_ref) = refs[:17]
    ng = len(ATT_DILATIONS)
    num_s, m_s, den_s = refs[17:17 + ng], refs[17 + ng:17 + 2 * ng], refs[17 + 2 * ng:]
    first_tile = (pl.program_id(2) == 0).astype(jnp.int32)
    groups = ((q0, kc0, vc0, kp0, vp0), (q1, kc1, vc1, kp1, vp1), (q2, kc2, vc2, kp2, vp2))
    for g, dil in enumerate(ATT_DILATIONS):
        q_ref, kc_ref, vc_ref, kp_ref, vp_ref = groups[g]
        nb = ATT_TILE // dil // ATT_BLOCK
        for r in range(dil):
            for n in range(nb):
                rows = pl.ds(n * ATT_BLOCK, ATT_BLOCK)
                if n == 0:
                    prev = pl.ds((nb - 1) * ATT_BLOCK, ATT_BLOCK)
                    kp, vp, first = kp_ref[r, prev, :], vp_ref[r, prev, :], first_tile
                else:
                    prev = pl.ds((n - 1) * ATT_BLOCK, ATT_BLOCK)
                    kp, vp, first = kc_ref[r, prev, :], vc_ref[r, prev, :], 0
                num, m, den = _att_pair(q_ref[r, rows, :], kp, kc_ref[r, rows, :], vp, vc_ref[r, rows, :],
                                        bias_ref, g, first)
                if dil == 1:
                    dst = rows
                else:
                    dst = pl.ds(n * ATT_BLOCK * dil + r, ATT_BLOCK, stride=dil)
                num_s[g][dst, :] = num
                m_s[g][dst, :] = m
                den_s[g][dst, :] = den
    m_all = jnp.maximum(jnp.maximum(m_s[0][...], m_s[1][...]), m_s[2][...])
    num = jnp.zeros((ATT_TILE, 2 * ATT_HEAD_DIM), F32)
    den = jnp.zeros((ATT_TILE, 2 * ATT_HEAD_DIM), F32)
    for g in range(ng):
        sc = jnp.exp2(m_s[g][...] - m_all)
        num = num + sc * num_s[g][...]
        den = den + sc * den_s[g][...]
    o_ref[...] = (num / den).astype(o_ref.dtype)


def _att_bias_table():
    qi = np.arange(ATT_BLOCK)[:, None]
    ki = np.arange(2 * ATT_BLOCK)[None, :]
    steps = qi + ATT_BLOCK - ki
    valid = (steps >= 0) & (steps <= ATT_BLOCK)
    slopes = np.array([2.0 ** (-8.0 * (h + 1) / ATT_HEADS) for h in range(ATT_HEADS)], np.float32)
    tab = np.empty((len(ATT_DILATIONS), ATT_HEADS, 2, ATT_BLOCK, 2 * ATT_BLOCK), np.float32)
    for g, dil in enumerate(ATT_DILATIONS):
        bias = -slopes[:, None, None] * (steps * dil).astype(np.float32)[None] * LOG2_E
        tab[g, :, 0] = np.where(valid[None], bias, NEG_INF)
        tab[g, :, 1] = np.where((valid & (ki >= ATT_BLOCK))[None], bias, NEG_INF)
    return jnp.asarray(tab)


def _attention(qkv, B, S):
    tiles = S // ATT_TILE
    pair = 2 * ATT_HEAD_DIM
    npair = ATT_WIDTH // pair
    in_specs, args = [], []
    for g, dil in enumerate(ATT_DILATIONS):
        blk = (None, dil, ATT_TILE // dil, pair)
        cur = lambda b, hp, t, off: (b * tiles + t, 0, 0, off * npair + hp)
        prv = lambda b, hp, t, off: (b * tiles + jnp.maximum(t - 1, 0), 0, 0, off * npair + hp)
        in_specs += [pl.BlockSpec(blk, functools.partial(cur, off=0)),
                     pl.BlockSpec(blk, functools.partial(cur, off=1)),
                     pl.BlockSpec(blk, functools.partial(cur, off=2)),
                     pl.BlockSpec(blk, functools.partial(prv, off=1)),
                     pl.BlockSpec(blk, functools.partial(prv, off=2))]
        args += [qkv[g]] * 5
    in_specs.append(pl.BlockSpec((len(ATT_DILATIONS), 2, 2, ATT_BLOCK, 2 * ATT_BLOCK),
                                 lambda b, hp, t: (0, hp, 0, 0, 0)))
    args.append(_att_bias_table())
    scratch = [pltpu.VMEM((ATT_TILE, pair), F32) for _ in range(3 * len(ATT_DILATIONS))]
    return pl.pallas_call(
        _att_kernel,
        grid=(B, npair, tiles),
        in_specs=in_specs,
        out_specs=pl.BlockSpec((ATT_TILE, pair), lambda b, hp, t: (b * tiles + t, hp)),
        out_shape=jax.ShapeDtypeStruct((B * S, ATT_WIDTH), BF16),
        scratch_shapes=scratch,
        compiler_params=_cparams("parallel", "parallel", "arbitrary"),
        name="dilated_attention",
    )(*args)


def _att_weights(w_in_l):
    out = []
    width = len(ATT_DILATIONS) * ATT_WIDTH
    for g in range(len(ATT_DILATIONS)):
        cols = [w_in_l[:, part * width + g * ATT_WIDTH: part * width + (g + 1) * ATT_WIDTH] for part in range(3)]
        cols[0] = cols[0] * (ATT_HEAD_DIM ** -0.5 * LOG2_E)
        out.append(jnp.concatenate(cols, axis=1).astype(BF16))
    return out


def _proj_kernel(x_ref, w_ref, o_ref, *, col_tile):
    xb = x_ref[...].astype(BF16)
    for c in range(w_ref.shape[1] // col_tile):
        cols = slice(c * col_tile, (c + 1) * col_tile)
        o_ref[:, cols] = jnp.dot(xb, w_ref[:, cols], preferred_element_type=F32).astype(o_ref.dtype)


def _proj(x2d, w, col_tile):
    T, N = x2d.shape[0], w.shape[1]
    return pl.pallas_call(
        functools.partial(_proj_kernel, col_tile=col_tile),
        grid=(T // PROJ_ROWS,),
        in_specs=[pl.BlockSpec((PROJ_ROWS, D_MODEL), lambda i: (i, 0)),
                  pl.BlockSpec((D_MODEL, N), lambda i: (0, 0))],
        out_specs=pl.BlockSpec((PROJ_ROWS, N), lambda i: (i, 0)),
        out_shape=jax.ShapeDtypeStruct((T, N), BF16),
        compiler_params=_cparams("parallel"),
        name="proj_hgrn_gates",
    )(x2d, w)


def _split2(v):
    a = v.astype(BF16)
    return a, (v - a.astype(F32)).astype(BF16)


def _hgrn_kernel(q_ref, f_ref, i_ref, g_ref, lbl_ref, gain_ref, o_ref, state_ref):
    @pl.when(pl.program_id(1) == 0)
    def _():
        state_ref[...] = jnp.zeros_like(state_ref)

    lbl = lbl_ref[...]
    e = jnp.exp(lbl - jnp.max(lbl, axis=0, keepdims=True))
    lb = e[0:1] / jnp.sum(e, axis=0, keepdims=True)
    forget = lb + (1.0 - lb) * jax.nn.sigmoid(f_ref[...].astype(F32))
    log_f = jnp.log(forget)
    key = 1.0 - forget

    row = lax.broadcasted_iota(jnp.int32, (HG_TILE, HG_TILE), 0)
    col = lax.broadcasted_iota(jnp.int32, (HG_TILE, HG_TILE), 1)
    causal = (row >= col) & ((row // HG_CHUNK) == (col // HG_CHUNK))
    tri = jnp.where(causal, 1.0, 0.0).astype(BF16)
    b = sum(jnp.dot(tri, t, preferred_element_type=F32) for t in _split2(log_f))
    eb = jnp.exp(b)
    q_dec = (q_ref[...].astype(F32) * eb).astype(BF16)
    k_inv = key * jnp.exp(-b)
    xi = i_ref[...].astype(F32)
    val = (xi * jax.nn.sigmoid(xi)).astype(BF16)
    k_inv_b = k_inv.astype(BF16)

    n_chunks = HG_TILE // HG_CHUNK
    last_rows = [eb[(c + 1) * HG_CHUNK - 1:(c + 1) * HG_CHUNK, :] for c in range(n_chunks)]
    dec_rows = jnp.concatenate([jnp.broadcast_to(r, (HG_CHUNK, HG_WIDTH)) for r in last_rows], axis=0)
    k_end = (k_inv * dec_rows).astype(BF16)
    def per_chunk_columns(t):
        blocks = []
        for c in range(n_chunks):
            rows_above, rows_below = c * HG_CHUNK, HG_TILE - (c + 1) * HG_CHUNK
            parts = [t[rows_above:rows_above + HG_CHUNK]]
            if rows_above:
                parts.insert(0, jnp.zeros((rows_above, HG_DIM), t.dtype))
            if rows_below:
                parts.append(jnp.zeros((rows_below, HG_DIM), t.dtype))
            blocks.append(jnp.concatenate(parts, axis=0))
        return jnp.concatenate(blocks, axis=1)

    head_cols = [slice(h * HG_DIM, (h + 1) * HG_DIM) for h in range(HG_HEADS)]
    upds = [lax.dot_general(val[:, cols], per_chunk_columns(k_end[:, cols]), (((0,), (0,)), ((), ())),
                            preferred_element_type=F32) for cols in head_cols]
    o_intras = []
    for cols in head_cols:
        a = lax.dot_general(q_dec[:, cols], k_inv_b[:, cols], (((1,), (1,)), ((), ())), preferred_element_type=F32)
        a = jnp.where(causal, a, 0.0).astype(BF16)
        o_intras.append(jnp.dot(a, val[:, cols], preferred_element_type=F32))
    enterings = []
    for h, cols in enumerate(head_cols):
        st = state_ref[h]
        entering = []
        for c in range(n_chunks):
            entering.append(st.astype(BF16))
            st = st * last_rows[c][:, cols] + upds[h][:, c * HG_DIM:(c + 1) * HG_DIM]
        state_ref[h] = st
        enterings.append(jnp.concatenate(entering, axis=1))
    outs = []
    for h, cols in enumerate(head_cols):
        qd = q_dec[:, cols]
        o_inter = lax.dot_general(per_chunk_columns(qd), enterings[h],
                                  (((1,), (1,)), ((), ())), preferred_element_type=F32)
        o = o_intras[h] + o_inter
        o = o * lax.rsqrt(jnp.mean(jnp.square(o), axis=-1, keepdims=True) + RMS_EPS)
        outs.append(o)
    o = jnp.concatenate(outs, axis=1) * gain_ref[...]
    gg = g_ref[...].astype(F32)
    o_ref[...] = (o * (gg * jax.nn.sigmoid(gg))).astype(o_ref.dtype)


def _hgrn(u_hg, lb_logits, gain, B, S):
    tiles = S // HG_TILE
    col = lambda j: pl.BlockSpec((HG_TILE, HG_WIDTH), functools.partial(lambda b, t, j: (b * tiles + t, j), j=j))
    return pl.pallas_call(
        _hgrn_kernel,
        grid=(B, tiles),
        in_specs=[col(0), col(1), col(2), col(3),
                  pl.BlockSpec((2, HG_WIDTH), lambda b, t: (0, 0)),
                  pl.BlockSpec((1, HG_WIDTH), lambda b, t: (0, 0))],
        out_specs=pl.BlockSpec((HG_TILE, HG_WIDTH), lambda b, t: (b * tiles + t, 0)),
        out_shape=jax.ShapeDtypeStruct((B * S, HG_WIDTH), BF16),
        scratch_shapes=[pltpu.VMEM((HG_HEADS, HG_DIM, HG_DIM), F32)],
        compiler_params=_cparams("parallel", "arbitrary"),
        name="hgrn2",
    )(u_hg, u_hg, u_hg, u_hg, lb_logits, gain)


def _load_row_tiles(ref, n, start=0):
    return jnp.concatenate([ref[pl.ds(start + c, n, stride=LANE_CHUNKS), :] for c in range(LANE_CHUNKS)], axis=1)


def _store_row_tiles(ref, val, n):
    for c in range(LANE_CHUNKS):
        ref[pl.ds(c, n, stride=LANE_CHUNKS), :] = val[:, c * LANES:(c + 1) * LANES]


def _layer_norm(z, w, b):
    mu = jnp.mean(z, axis=-1, keepdims=True)
    zc = z - mu
    var = jnp.mean(jnp.square(zc), axis=-1, keepdims=True)
    return zc * lax.rsqrt(var + LN_EPS) * w + b


def _merge_kernel(ya_ref, yh_ref, ga_ref, gh_ref, x_ref, wa_ref, wh_ref, wo_ref, lw_ref, lb_ref, o_ref):
    ma = jnp.dot(ya_ref[...], wa_ref[...], preferred_element_type=F32)
    mh = jnp.dot(yh_ref[...], wh_ref[...], preferred_element_type=F32)
    merged = (jax.nn.sigmoid(ga_ref[...].astype(F32)) * ma + jax.nn.sigmoid(gh_ref[...].astype(F32)) * mh)
    z = DEEPNORM_ALPHA * x_ref[...] + jnp.dot(merged.astype(BF16), wo_ref[...], preferred_element_type=F32)
    _store_row_tiles(o_ref, _layer_norm(z, lw_ref[...], lb_ref[...]), MIX_ROWS)


def _merge(y_att, y_hg, u_hg, x2d, w_a, w_h, w_o, ln_w, ln_b):
    T = x2d.shape[0]
    rows = lambda width, j=0: pl.BlockSpec((MIX_ROWS, width), functools.partial(lambda i, j: (i, j), j=j))
    full = lambda a: pl.BlockSpec(a.shape, lambda i: (0, 0))
    return pl.pallas_call(
        _merge_kernel,
        grid=(T // MIX_ROWS,),
        in_specs=[rows(ATT_WIDTH), rows(HG_WIDTH), rows(D_MODEL, 4), rows(D_MODEL, 5), rows(D_MODEL),
                  full(w_a), full(w_h), full(w_o), full(ln_w), full(ln_b)],
        out_specs=pl.BlockSpec((MIX_ROWS * LANE_CHUNKS, LANES), lambda i: (i, 0)),
        out_shape=jax.ShapeDtypeStruct((T * LANE_CHUNKS, LANES), F32),
        compiler_params=_cparams("parallel"),
        name="merge_ln1",
    )(y_att, y_hg, u_hg, u_hg, x2d, w_a, w_h, w_o, ln_w, ln_b)


def _first_argmax(v, ids, n):
    mx = jnp.max(v, axis=0, keepdims=True)
    return mx, jnp.min(jnp.where(v == mx, ids, n), axis=0, keepdims=True)


def _route_kernel(x1_ref, p_ref, wrt_ref, rb_ref, wsg_ref, wsu_ref, wsd_ref, wpg_ref, wpp_ref,
                  base_ref, idx_ref, gate_ref, rank_ref, cnt_ref, carry_ref):
    @pl.when(pl.program_id(0) % (MOE_TILE // MIX_ROWS) == 0)
    def _():
        carry_ref[...] = jnp.zeros_like(carry_ref)

    x1 = _load_row_tiles(x1_ref, MIX_ROWS)
    x1b = x1.astype(BF16)
    logits = lax.dot_general(wrt_ref[...], x1, (((1,), (1,)), ((), ())), preferred_element_type=F32,
                             precision=lax.Precision.HIGHEST)
    hg = jnp.dot(x1b, wsg_ref[...], preferred_element_type=F32)
    hu = jnp.dot(x1b, wsu_ref[...], preferred_element_type=F32)
    ple_gate = jnp.dot(x1b, wpg_ref[...], preferred_element_type=F32)
    ple_proj = jnp.dot(p_ref[...].astype(BF16), wpp_ref[...], preferred_element_type=F32)

    s = jax.nn.sigmoid(logits)
    sel = s + rb_ref[...]
    eid = lax.broadcasted_iota(jnp.int32, (N_EXPERTS, MIX_ROWS), 0)
    neg = -jnp.inf

    grp = sel.reshape(N_GROUPS, GROUP_SIZE, MIX_ROWS)
    mid = lax.broadcasted_iota(jnp.int32, grp.shape, 1)
    m1 = jnp.max(grp, axis=1, keepdims=True)
    i1 = jnp.min(jnp.where(grp == m1, mid, GROUP_SIZE), axis=1, keepdims=True)
    m2 = jnp.max(jnp.where(mid == i1, neg, grp), axis=1, keepdims=True)
    gscore = (m1 + m2).reshape(N_GROUPS, MIX_ROWS)
    gid = lax.broadcasted_iota(jnp.int32, (N_GROUPS, MIX_ROWS), 0)
    gsel = jnp.zeros((N_GROUPS, MIX_ROWS), jnp.bool_)
    for _ in range(TOPK_GROUPS):
        _, gi = _first_argmax(gscore, gid, N_GROUPS)
        hit = gid == gi
        gsel = gsel | hit
        gscore = jnp.where(hit, neg, gscore)
    emask = jnp.broadcast_to(gsel.reshape(N_GROUPS, 1, MIX_ROWS), grp.shape).reshape(N_EXPERTS, MIX_ROWS)
    cand = jnp.where(emask, sel, neg)

    idxs, gates = [], []
    chosen = jnp.zeros((N_EXPERTS, MIX_ROWS), jnp.bool_)
    for _ in range(TOP_K):
        _, ei = _first_argmax(cand, eid, N_EXPERTS)
        hit = eid == ei
        idxs.append(ei)
        gates.append(jnp.sum(jnp.where(hit, s, 0.0), axis=0, keepdims=True))
        chosen = chosen | hit
        cand = jnp.where(hit, neg, cand)
    g = jnp.concatenate(gates, axis=0)
    g = g / jnp.sum(g, axis=0, keepdims=True) * ROUTED_SCALE
    idx_ref[...] = jnp.concatenate(idxs, axis=0)
    gate_ref[...] = g

    onehot = jnp.where(chosen, 1.0, 0.0)
    tr = lax.broadcasted_iota(jnp.int32, (MIX_ROWS, MIX_ROWS), 0)
    tc = lax.broadcasted_iota(jnp.int32, (MIX_ROWS, MIX_ROWS), 1)
    before = jnp.where(tr < tc, 1.0, 0.0).astype(BF16)
    prefix = jnp.dot(onehot.astype(BF16), before, preferred_element_type=F32)
    rankfull = (carry_ref[:, 0:1] + prefix).astype(jnp.int32)
    rank_ref[...] = jnp.concatenate(
        [jnp.sum(jnp.where(eid == ei, rankfull, 0), axis=0, keepdims=True) for ei in idxs], axis=0)
    total = carry_ref[...] + jnp.sum(onehot, axis=1, keepdims=True)
    carry_ref[...] = total
    cnt_ref[...] = total.astype(jnp.int32)

    shared = jnp.dot((hg * jax.nn.sigmoid(hg) * hu).astype(BF16), wsd_ref[...], preferred_element_type=F32)
    ple = jax.nn.sigmoid(ple_gate) * ple_proj
    _store_row_tiles(base_ref, DEEPNORM_ALPHA * x1 + shared + ple, MIX_ROWS)


def _route(x1, p2d, wr_t, rbias, wsg, wsu, wsd, wpg, wpp):
    T = x1.shape[0] // LANE_CHUNKS
    per_tile = MOE_TILE // MIX_ROWS
    full = lambda a: pl.BlockSpec(a.shape, lambda i: (0, 0))
    tok = pl.BlockSpec((TOP_K, MIX_ROWS), lambda i: (0, i))
    row_tiles = pl.BlockSpec((MIX_ROWS * LANE_CHUNKS, LANES), lambda i: (i, 0))
    return pl.pallas_call(
        _route_kernel,
        grid=(T // MIX_ROWS,),
        in_specs=[row_tiles,
                  pl.BlockSpec((MIX_ROWS, PLE_DIM), lambda i: (i, 0)),
                  full(wr_t), full(rbias), full(wsg), full(wsu), full(wsd), full(wpg), full(wpp)],
        out_specs=[row_tiles, tok, tok, tok,
                   pl.BlockSpec((None, N_EXPERTS, LANES), lambda i: (i // per_tile, 0, 0))],
        out_shape=[jax.ShapeDtypeStruct((T * LANE_CHUNKS, LANES), F32),
                   jax.ShapeDtypeStruct((TOP_K, T), jnp.int32),
                   jax.ShapeDtypeStruct((TOP_K, T), F32),
                   jax.ShapeDtypeStruct((TOP_K, T), jnp.int32),
                   jax.ShapeDtypeStruct((T // MOE_TILE, N_EXPERTS, LANES), jnp.int32)],
        scratch_shapes=[pltpu.VMEM((N_EXPERTS, LANES), F32)],
        compiler_params=_cparams("arbitrary"),
        name="route_shared_ple",
    )(x1, p2d, wr_t, rbias, wsg, wsu, wsd, wpg, wpp)


def _plan_sc_kernel(off_hbm, idx_hbm, rank_hbm, gate_hbm, rows_hbm, gates_hbm, seg_hbm,
                    off_v, idx_v, rank_v, gate_v, rows_v, gates_v, seg_v):
    n_tokens = idx_hbm.shape[0] // TOP_K
    worker = lax.axis_index("subcore") * SC_CORES + lax.axis_index("core")

    @pl.when(worker < n_tokens // MOE_TILE)
    def _():
        pltpu.sync_copy(off_hbm.at[pl.ds(worker * N_EXPERTS, N_EXPERTS)], off_v)
        lane = lax.iota(jnp.int32, SC_LANES)

        @pl.loop(0, N_EXPERTS * GATE_SEG // SC_LANES)
        def _(i):
            seg_v[pl.ds(i * SC_LANES, SC_LANES)] = jnp.zeros((SC_LANES,), F32)

        for k in range(TOP_K):
            row = pl.ds(k * n_tokens + worker * MOE_TILE, MOE_TILE)
            pltpu.sync_copy(idx_hbm.at[row], idx_v)
            pltpu.sync_copy(rank_hbm.at[row], rank_v)
            pltpu.sync_copy(gate_hbm.at[row], gate_v)

            @pl.loop(0, MOE_TILE // SC_LANES)
            def _(i):
                at = i * SC_LANES
                expert, rank, gate = (v[pl.ds(at, SC_LANES)] for v in (idx_v, rank_v, gate_v))
                pos = plsc.load_gather(off_v, [expert]) + rank
                plsc.store_scatter(rows_v, [pos], (lane + at) * LANE_CHUNKS)
                plsc.store_scatter(gates_v, [pos], gate)
                plsc.store_scatter(seg_v, [expert * GATE_SEG + rank], gate, mask=rank < GATE_SEG)

        @pl.loop(0, LIST_PAD // SC_LANES)
        def _(i):
            tail = pl.ds(MOE_TILE * TOP_K + i * SC_LANES, SC_LANES)
            rows_v[tail] = jnp.zeros((SC_LANES,), jnp.int32)
            gates_v[tail] = jnp.zeros((SC_LANES,), F32)

        out = pl.ds(worker * LIST_LEN, LIST_LEN)
        pltpu.sync_copy(rows_v, rows_hbm.at[out])
        pltpu.sync_copy(gates_v, gates_hbm.at[out])
        pltpu.sync_copy(seg_v, seg_hbm.at[pl.ds(worker * (N_EXPERTS * GATE_SEG), N_EXPERTS * GATE_SEG)])


def _plan_sc(off, idx, rank, gate):
    n_tiles = idx.shape[1] // MOE_TILE
    assert n_tiles <= SC_CORES * SC_SUBCORES
    mesh = plsc.VectorSubcoreMesh(core_axis_name="core", subcore_axis_name="subcore",
                                  num_cores=SC_CORES, num_subcores=SC_SUBCORES)
    rows, gates, seg = pl.kernel(
        _plan_sc_kernel,
        out_type=(jax.ShapeDtypeStruct((n_tiles * LIST_LEN,), jnp.int32),
                  jax.ShapeDtypeStruct((n_tiles * LIST_LEN,), F32),
                  jax.ShapeDtypeStruct((n_tiles * N_EXPERTS * GATE_SEG,), F32)),
        mesh=mesh,
        scratch_types=[pltpu.VMEM((N_EXPERTS,), jnp.int32), pltpu.VMEM((MOE_TILE,), jnp.int32),
                       pltpu.VMEM((MOE_TILE,), jnp.int32), pltpu.VMEM((MOE_TILE,), F32),
                       pltpu.VMEM((LIST_LEN,), jnp.int32), pltpu.VMEM((LIST_LEN,), F32),
                       pltpu.VMEM((N_EXPERTS * GATE_SEG,), F32)],
        compiler_params=pltpu.CompilerParams(needs_layout_passes=False),
        name="moe_plan_sc",
    )(off, idx.reshape(-1), rank.reshape(-1), gate.reshape(-1))
    return rows, gates, seg.reshape(n_tiles * N_EXPERTS, 1, GATE_SEG)


def _moe_tile_kernel(cnt_ref, off_ref, wg_ref, wu_ref, wd_ref, lw_ref, lb_ref, seg_ref, rows_hbm, gates_hbm, x_hbm,
                     base_hbm, o_hbm, x_s, acc_s, xg_a, xg_b, xg_c, y_a, y_b, y_c, stage_s, rows_s, gates_s, sem):
    tile, e = pl.program_id(0), pl.program_id(1)
    rows_of = lambda ref, r, n: ref.at[pl.ds(pl.multiple_of(r * LANE_CHUNKS, LANE_CHUNKS), n * LANE_CHUNKS), :]
    tile_rows = pl.ds(pl.multiple_of(tile * (MOE_TILE * LANE_CHUNKS), LANE_CHUNKS), MOE_TILE * LANE_CHUNKS)

    pair = tile * N_EXPERTS + e
    last_pair = pl.num_programs(0) * N_EXPERTS - 1
    n, off = cnt_ref[pair], off_ref[pair]
    tile_at = lambda ref, r: ref.at[pl.ds(pl.multiple_of(r, LANE_CHUNKS), LANE_CHUNKS), :]

    def gather_group(xg, first, jb):
        at = first + jb * GATHER_GROUP
        rows = [tile_at(x_s, rows_s[at + u])[...] for u in range(GATHER_GROUP)]
        rows_of(xg, jb * GATHER_GROUP, GATHER_GROUP)[...] = jnp.concatenate(rows, axis=0)

    def gather_loop(xg, first):
        def body(jb, cc):
            gather_group(xg, first, jb)
            return cc

        lax.fori_loop(0, MOE_CHUNK // GATHER_GROUP, body, 0)

    def swiglu(xg, y):
        xb = _load_row_tiles(xg, MOE_CHUNK).astype(BF16)
        hg = jnp.dot(xb, wg_ref[...], preferred_element_type=F32)
        hu = jnp.dot(xb, wu_ref[...], preferred_element_type=F32)
        act = (hg * jax.nn.sigmoid(hg) * hu).astype(BF16)
        _store_row_tiles(y, jnp.dot(act, wd_ref[...], preferred_element_type=F32), MOE_CHUNK)

    def scatter_group(y, first, j0, live, gated):
        dsts = [rows_s[first + j0 + u] for u in range(live)]
        yv = rows_of(y, j0, live)[...]
        rows = [yv[u * LANE_CHUNKS:(u + 1) * LANE_CHUNKS] for u in range(live)]
        if not gated:
            rows = [gates_s[first + j0 + u] * r for u, r in enumerate(rows)]
        vals = [tile_at(acc_s, d)[...] + r for d, r in zip(dsts, rows)]
        for d, val in reversed(list(zip(dsts, vals))):
            tile_at(acc_s, d)[...] = val

    def scatter_loop(y, first, m, gated):
        def body(jg, cc):
            scatter_group(y, first, jg * SCATTER_GROUP, SCATTER_GROUP, gated)
            return cc

        lax.fori_loop(0, m // SCATTER_GROUP, body, 0)
        for live in range(1, SCATTER_GROUP):
            @pl.when(m % SCATTER_GROUP == live)
            def _(live=live):
                scatter_group(y, first, m - live, live, gated)

    def plan_loads(t):
        plan = pl.ds(pl.multiple_of(t * LIST_LEN, LIST_PAD), LIST_LEN)
        return (pltpu.make_async_copy(rows_hbm.at[plan], rows_s, sem.at[0]),
                pltpu.make_async_copy(gates_hbm.at[plan], gates_s, sem.at[1]))

    def x_load(t):
        rows = pl.ds(pl.multiple_of(t * (MOE_TILE * LANE_CHUNKS), LANE_CHUNKS), MOE_TILE * LANE_CHUNKS)
        return pltpu.make_async_copy(x_hbm.at[rows, :], x_s, sem.at[2])

    @pl.when(e == 0)
    def _():
        @pl.when(tile == 0)
        def _():
            x_load(tile).start()
            for load in plan_loads(tile):
                load.start()

        load_base = pltpu.make_async_copy(base_hbm.at[tile_rows, :], acc_s, sem.at[3])
        load_base.start()
        y_b[...] = jnp.zeros_like(y_b)
        load_rows, load_gates = plan_loads(tile)
        load_rows.wait()
        x_load(tile).wait()
        gather_loop(xg_a, off)
        load_gates.wait()
        load_base.wait()

    prev_off = off_ref[jnp.maximum(pair - 1, 0)]
    next_off = off_ref[jnp.minimum(pair + 1, last_pair)]
    live_row = lax.broadcasted_iota(jnp.int32, (MOE_CHUNK, 2 * LANES), 0) < n

    def run_expert(xg_cur, y_cur, xg_nxt, y_prv):
        gathers = [functools.partial(gather_group, xg_nxt, next_off, jb) for jb in range(MOE_CHUNK // GATHER_GROUP)]
        scatters = [functools.partial(scatter_group, y_prv, prev_off, jg * SCATTER_GROUP, SCATTER_GROUP, True)
                    for jg in range(MOE_CHUNK // SCATTER_GROUP)]
        side = [s for both in zip(gathers, scatters) for s in both]
        n_down = LANE_CHUNKS // 2
        cost = [D_MODEL, D_MODEL] + [EXPERT_FF] * n_down
        bounds = [round(len(side) * sum(cost[:i]) / sum(cost)) for i in range(len(cost) + 1)]

        def side_work(i):
            for s in side[bounds[i]:bounds[i + 1]]:
                s()

        xb = _load_row_tiles(xg_cur, MOE_CHUNK).astype(BF16)
        side_work(0)
        hg = jnp.dot(xb, wg_ref[...], preferred_element_type=F32)
        side_work(1)
        hu = jnp.dot(xb, wu_ref[...], preferred_element_type=F32)
        gate_col = jnp.broadcast_to(seg_ref[...], (LANES, GATE_SEG)).T[:MOE_CHUNK]
        gate_col = jnp.concatenate([gate_col] * (EXPERT_FF // LANES), axis=1)
        act = (hg * jax.nn.sigmoid(hg) * hu * gate_col).astype(BF16)
        for q in range(n_down):
            side_work(2 + q)
            out = jnp.dot(act, wd_ref[:, q * 2 * LANES:(q + 1) * 2 * LANES], preferred_element_type=F32)
            out = jnp.where(live_row, out, 0.0)
            for c in range(2):
                y_cur[pl.ds(2 * q + c, MOE_CHUNK, stride=LANE_CHUNKS), :] = out[:, c * LANES:(c + 1) * LANES]

    @pl.when(e % 2 == 0)
    def _():
        run_expert(xg_a, y_a, xg_b, y_b)

    @pl.when(e % 2 == 1)
    def _():
        run_expert(xg_b, y_b, xg_a, y_a)

    def extra_chunk(c, carry):
        first = off + c * MOE_CHUNK
        gather_loop(xg_c, first)
        swiglu(xg_c, y_c)
        scatter_loop(y_c, first, jnp.minimum(MOE_CHUNK, n - c * MOE_CHUNK), False)
        return carry

    lax.fori_loop(1, (n + MOE_CHUNK - 1) // MOE_CHUNK, extra_chunk, 0)

    @pl.when(e == N_EXPERTS - 1)
    def _():
        more_tiles = tile + 1 < pl.num_programs(0)

        @pl.when(more_tiles)
        def _():
            x_load(tile + 1).start()

        scatter_loop(y_b, off, jnp.minimum(MOE_CHUNK, n), True)

        @pl.when(more_tiles)
        def _():
            for load in plan_loads(tile + 1):
                load.start()

        n_pieces = MOE_TILE // LN_ROWS
        store = lambda c: pltpu.make_async_copy(
            stage_s.at[c % 2], o_hbm.at[pl.ds(tile * MOE_TILE + c * LN_ROWS, LN_ROWS), :], sem.at[4 + c % 2])
        for c in range(n_pieces):
            z = _load_row_tiles(acc_s, LN_ROWS, c * LN_ROWS * LANE_CHUNKS)
            if c >= 2:
                store(c - 2).wait()
            stage_s[c % 2] = _layer_norm(z, lw_ref[...], lb_ref[...])
            store(c).start()
        store(n_pieces - 2).wait()
        store(n_pieces - 1).wait()


def _moe_tiles(x1, base, plan_rows, plan_gates, gate_seg, cnt, off, wg, wu, wd, ln_w, ln_b):
    T = x1.shape[0] // LANE_CHUNKS
    w_spec = lambda shape: pl.BlockSpec((None,) + shape, lambda i, e, cnt, off: (e, 0, 0))
    vec = pl.BlockSpec((1, D_MODEL), lambda i, e, cnt, off: (0, 0))
    hbm = pl.BlockSpec(memory_space=pl.ANY)
    tile_rows = MOE_TILE * LANE_CHUNKS
    return pl.pallas_call(
        _moe_tile_kernel,
        grid_spec=pltpu.PrefetchScalarGridSpec(
            num_scalar_prefetch=2,
            grid=(T // MOE_TILE, N_EXPERTS),
            in_specs=[w_spec((D_MODEL, EXPERT_FF)), w_spec((D_MODEL, EXPERT_FF)), w_spec((EXPERT_FF, D_MODEL)),
                      vec, vec,
                      pl.BlockSpec((None, 1, GATE_SEG), lambda i, e, cnt, off: (i * N_EXPERTS + e, 0, 0)),
                      hbm, hbm, hbm, hbm],
            out_specs=hbm,
            scratch_shapes=[pltpu.VMEM((tile_rows, LANES), F32),
                            pltpu.VMEM((tile_rows, LANES), F32)]
                           + [pltpu.VMEM((MOE_CHUNK * LANE_CHUNKS, LANES), F32)] * 6
                           + [pltpu.VMEM((2, LN_ROWS, D_MODEL), F32),
                              pltpu.SMEM((LIST_LEN,), jnp.int32),
                              pltpu.SMEM((LIST_LEN,), F32),
                              pltpu.SemaphoreType.DMA((6,))],
        ),
        out_shape=jax.ShapeDtypeStruct((T, D_MODEL), F32),
        compiler_params=_cparams("arbitrary", "arbitrary"),
        name="moe_tiles_ln2",
    )(cnt, off, wg, wu, wd, ln_w, ln_b, gate_seg, plan_rows, plan_gates, x1, base)


def kernel(x, p, w_in, hgrn_lb_logits, hgrn_norm_w, w_branch_att, w_branch_hgrn, w_out, ln1_w, ln1_b, router_w, router_bias, expert_w_gate, expert_w_up, expert_w_down, shared_w_gate, shared_w_up, shared_w_down, ple_gate_w, ple_proj_w, ln2_w, ln2_b):
    B, S, D = x.shape
    T = B * S
    l = 0
    x2d = x.reshape(T, D)
    bf = lambda a: a.astype(BF16)

    ws = _att_weights(w_in[l])
    qkv = [_proj_att(x2d, ws[g], d) for g, d in enumerate(ATT_DILATIONS)]
    y_att = _attention(qkv, B, S)
    u_hg = _proj(x2d, bf(w_in[l][:, 3 * len(ATT_DILATIONS) * ATT_WIDTH:]), PROJ_COLS)
    y_hg = _hgrn(u_hg, hgrn_lb_logits, hgrn_norm_w[l:l + 1], B, S)
    x1 = _merge(y_att, y_hg, u_hg, x2d, bf(w_branch_att[l]), bf(w_branch_hgrn[l]), bf(w_out[l]),
                ln1_w[l:l + 1], ln1_b[l:l + 1])

    base, idx, gate, rank, counts = _route(
        x1, p[l].reshape(T, PLE_DIM), router_w[l].T, router_bias[l].reshape(N_EXPERTS, 1),
        bf(shared_w_gate[l]), bf(shared_w_up[l]), bf(shared_w_down[l]), bf(ple_gate_w[l]), bf(ple_proj_w[l]))
    cnt = counts[:, :, 0]
    off = jnp.cumsum(cnt, axis=1) - cnt
    cnt, off = cnt.reshape(-1), off.reshape(-1)
    plan_rows, plan_gates, gate_seg = _plan_sc(off, idx, rank, gate)
    out = _moe_tiles(x1, base, plan_rows, plan_gates, gate_seg, cnt, off, bf(expert_w_gate[l]), bf(expert_w_up[l]),
                     bf(expert_w_down[l]), ln2_w[l:l + 1], ln2_b[l:l + 1])
    return out.reshape(B, S, D)
```

```python
import functools

import jax
import jax.numpy as jnp
import numpy as np
from jax import lax
from jax.experimental import pallas as pl
from jax.experimental.pallas import tpu as pltpu
from jax.experimental.pallas import tpu_sc as plsc

F32 = jnp.float32
BF16 = jnp.bfloat16

D_MODEL = 1024
ATT_HEAD_DIM = 64
ATT_HEADS = 8
ATT_DILATIONS = (1, 4, 16)
ATT_BLOCK = 128
ATT_WIDTH = ATT_HEADS * ATT_HEAD_DIM
ATT_TILE = ATT_BLOCK * max(ATT_DILATIONS)
NEG_INF = -1e30
LOG2_E = 1.4426950408889634

HG_HEADS = 8
HG_DIM = 128
HG_WIDTH = HG_HEADS * HG_DIM
HG_CHUNK = 32
HG_TILE = 256
RMS_EPS = 1e-6

N_EXPERTS = 64
TOP_K = 8
N_GROUPS = 8
GROUP_SIZE = N_EXPERTS // N_GROUPS
TOPK_GROUPS = 4
EXPERT_FF = 256
ROUTED_SCALE = 2.5
PLE_DIM = 256
LN_EPS = 1e-5
DEPTH = 1
DEEPNORM_ALPHA = (2.0 * DEPTH) ** 0.25

LANES = 128
LANE_CHUNKS = D_MODEL // LANES
PROJ_ROWS = 512
PROJ_COLS = 1536
ATT_PROJ_ROWS = 1024
MIX_ROWS = 512
MOE_TILE = 4096
MOE_CHUNK = 576
GATE_SEG = 640
LN_ROWS = 256
LIST_PAD = 1024
LIST_LEN = MOE_TILE * TOP_K + LIST_PAD
GATHER_GROUP = 8
SCATTER_GROUP = 8
V7X_VMEM_LIMIT = 56 * 1024 * 1024
SC_CORES, SC_SUBCORES, SC_LANES = 2, 16, 16


def _cparams(*sem):
    return pltpu.CompilerParams(dimension_semantics=sem, vmem_limit_bytes=V7X_VMEM_LIMIT)


def _proj_att_kernel(*refs, dil):
    x_refs, w_ref, o_ref = refs[:LANE_CHUNKS], refs[LANE_CHUNKS], refs[LANE_CHUNKS + 1]
    n = ATT_PROJ_ROWS // dil

    def rows(ref):
        if dil == 1:
            return ref[...]
        return jnp.concatenate([ref[pl.ds(r, n, stride=dil), :] for r in range(dil)], axis=0)

    xp = jnp.concatenate([rows(ref).astype(BF16) for ref in x_refs], axis=1)
    y = jnp.dot(xp, w_ref[...], preferred_element_type=F32)
    o_ref[...] = y.astype(BF16).reshape(dil, n, 3 * ATT_WIDTH)


def _proj_att(x2d, w, dil):
    T = x2d.shape[0]
    per = ATT_TILE // ATT_PROJ_ROWS
    n = ATT_PROJ_ROWS // dil
    out = pl.pallas_call(
        functools.partial(_proj_att_kernel, dil=dil),
        grid=(T // ATT_PROJ_ROWS,),
        in_specs=[pl.BlockSpec((ATT_PROJ_ROWS, LANES), functools.partial(lambda i, c: (i, c), c=c))
                  for c in range(LANE_CHUNKS)]
                 + [pl.BlockSpec((D_MODEL, 3 * ATT_WIDTH), lambda i: (0, 0))],
        out_specs=pl.BlockSpec((None, dil, None, n, 3 * ATT_WIDTH), lambda i: (i // per, 0, i % per, 0, 0)),
        out_shape=jax.ShapeDtypeStruct((T // ATT_TILE, dil, per, n, 3 * ATT_WIDTH), BF16),
        compiler_params=_cparams("parallel"),
        name=f"proj_att_d{dil}",
    )(*([x2d] * LANE_CHUNKS), w)
    return out.reshape(T // ATT_TILE, dil, ATT_TILE // dil, 3 * ATT_WIDTH)


def _att_pair(q2, kp, kc, vp, vc, bias_ref, g, first):
    def head0_lanes(rows, dtype):
        lane = lax.broadcasted_iota(jnp.int32, (rows, 2 * ATT_HEAD_DIM), 1)
        return lane.astype(F32).astype(dtype) < ATT_HEAD_DIM

    lo_q = head0_lanes(ATT_BLOCK, BF16)
    lo_v = head0_lanes(2 * ATT_BLOCK, BF16)
    k2 = jnp.concatenate([kp, kc], axis=0)
    v2 = jnp.concatenate([vp, vc], axis=0)
    zero = jnp.zeros_like(q2)
    ps, ms = [], []
    for hh in range(2):
        qm = jnp.where(lo_q, q2, zero) if hh == 0 else jnp.where(lo_q, zero, q2)
        s = lax.dot_general(qm, k2, (((1,), (1,)), ((), ())), preferred_element_type=F32)
        s = s + bias_ref[g, hh, first]
        m = jnp.max(s, axis=-1, keepdims=True)
        ps.append(jnp.exp2(s - m).astype(BF16))
        ms.append(m)
    pcat = jnp.concatenate(ps, axis=1)
    zero_v, one_v = jnp.zeros_like(v2), jnp.ones_like(v2)
    rhs = jnp.concatenate([
        jnp.concatenate([jnp.where(lo_v, v2, zero_v), jnp.where(lo_v, one_v, zero_v)], axis=1),
        jnp.concatenate([jnp.where(lo_v, zero_v, v2), jnp.where(lo_v, zero_v, one_v)], axis=1)], axis=0)
    nd = jnp.dot(pcat, rhs, preferred_element_type=F32)
    m2 = jnp.where(head0_lanes(ATT_BLOCK, F32), ms[0], ms[1])
    return nd[:, :2 * ATT_HEAD_DIM], m2, nd[:, 2 * ATT_HEAD_DIM:]


def _att_kernel(*refs):
    (q0, kc0, vc0, kp0, vp0, q1, kc1, vc1, kp1, vp1, q2, kc2, vc2, kp2, vp2,
     bias_ref, o_ref) = refs[:17]
    ng = len(ATT_DILATIONS)
    num_s, m_s, den_s = refs[17:17 + ng], refs[17 + ng:17 + 2 * ng], refs[17 + 2 * ng:]
    first_tile = (pl.program_id(2) == 0).astype(jnp.int32)
    groups = ((q0, kc0, vc0, kp0, vp0), (q1, kc1, vc1, kp1, vp1), (q2, kc2, vc2, kp2, vp2))
    for g, dil in enumerate(ATT_DILATIONS):
        q_ref, kc_ref, vc_ref, kp_ref, vp_ref = groups[g]
        nb = ATT_TILE // dil // ATT_BLOCK
        for r in range(dil):
            for n in range(nb):
                rows = pl.ds(n * ATT_BLOCK, ATT_BLOCK)
                if n == 0:
                    prev = pl.ds((nb - 1) * ATT_BLOCK, ATT_BLOCK)
                    kp, vp, first = kp_ref[r, prev, :], vp_ref[r, prev, :], first_tile
                else:
                    prev = pl.ds((n - 1) * ATT_BLOCK, ATT_BLOCK)
                    kp, vp, first = kc_ref[r, prev, :], vc_ref[r, prev, :], 0
                num, m, den = _att_pair(q_ref[r, rows, :], kp, kc_ref[r, rows, :], vp, vc_ref[r, rows, :],
                                        bias_ref, g, first)
                if dil == 1:
                    dst = rows
                else:
                    dst = pl.ds(n * ATT_BLOCK * dil + r, ATT_BLOCK, stride=dil)
                num_s[g][dst, :] = num
                m_s[g][dst, :] = m
                den_s[g][dst, :] = den
    m_all = jnp.maximum(jnp.maximum(m_s[0][...], m_s[1][...]), m_s[2][...])
    num = jnp.zeros((ATT_TILE, 2 * ATT_HEAD_DIM), F32)
    den = jnp.zeros((ATT_TILE, 2 * ATT_HEAD_DIM), F32)
    for g in range(ng):
        sc = jnp.exp2(m_s[g][...] - m_all)
        num = num + sc * num_s[g][...]
        den = den + sc * den_s[g][...]
    o_ref[...] = (num / den).astype(o_ref.dtype)


def _att_bias_table():
    qi = np.arange(ATT_BLOCK)[:, None]
    ki = np.arange(2 * ATT_BLOCK)[None, :]
    steps = qi + ATT_BLOCK - ki
    valid = (steps >= 0) & (steps <= ATT_BLOCK)
    slopes = np.array([2.0 ** (-8.0 * (h + 1) / ATT_HEADS) for h in range(ATT_HEADS)], np.float32)
    tab = np.empty((len(ATT_DILATIONS), ATT_HEADS, 2, ATT_BLOCK, 2 * ATT_BLOCK), np.float32)
    for g, dil in enumerate(ATT_DILATIONS):
        bias = -slopes[:, None, None] * (steps * dil).astype(np.float32)[None] * LOG2_E
        tab[g, :, 0] = np.where(valid[None], bias, NEG_INF)
        tab[g, :, 1] = np.where((valid & (ki >= ATT_BLOCK))[None], bias, NEG_INF)
    return jnp.asarray(tab)


def _attention(qkv, B, S):
    tiles = S // ATT_TILE
    pair = 2 * ATT_HEAD_DIM
    npair = ATT_WIDTH // pair
    in_specs, args = [], []
    for g, dil in enumerate(ATT_DILATIONS):
        blk = (None, dil, ATT_TILE // dil, pair)
        cur = lambda b, hp, t, off: (b * tiles + t, 0, 0, off * npair + hp)
        prv = lambda b, hp, t, off: (b * tiles + jnp.maximum(t - 1, 0), 0, 0, off * npair + hp)
        in_specs += [pl.BlockSpec(blk, functools.partial(cur, off=0)),
                     pl.BlockSpec(blk, functools.partial(cur, off=1)),
                     pl.BlockSpec(blk, functools.partial(cur, off=2)),
                     pl.BlockSpec(blk, functools.partial(prv, off=1)),
                     pl.BlockSpec(blk, functools.partial(prv, off=2))]
        args += [qkv[g]] * 5
    in_specs.append(pl.BlockSpec((len(ATT_DILATIONS), 2, 2, ATT_BLOCK, 2 * ATT_BLOCK),
                                 lambda b, hp, t: (0, hp, 0, 0, 0)))
    args.append(_att_bias_table())
    scratch = [pltpu.VMEM((ATT_TILE, pair), F32) for _ in range(3 * len(ATT_DILATIONS))]
    return pl.pallas_call(
        _att_kernel,
        grid=(B, npair, tiles),
        in_specs=in_specs,
        out_specs=pl.BlockSpec((ATT_TILE, pair), lambda b, hp, t: (b * tiles + t, hp)),
        out_shape=jax.ShapeDtypeStruct((B * S, ATT_WIDTH), BF16),
        scratch_shapes=scratch,
        compiler_params=_cparams("parallel", "parallel", "arbitrary"),
        name="dilated_attention",
    )(*args)


def _att_weights(w_in_l):
    out = []
    width = len(ATT_DILATIONS) * ATT_WIDTH
    for g in range(len(ATT_DILATIONS)):
        cols = [w_in_l[:, part * width + g * ATT_WIDTH: part * width + (g + 1) * ATT_WIDTH] for part in range(3)]
        cols[0] = cols[0] * (ATT_HEAD_DIM ** -0.5 * LOG2_E)
        out.append(jnp.concatenate(cols, axis=1).astype(BF16))
    return out


def _proj_kernel(x_ref, w_ref, eg_ref, eu_ref, ed_ref, o_ref, egb_ref, eub_ref, edb_ref, *, col_tile):
    egb_ref[...] = eg_ref[...].astype(BF16)
    eub_ref[...] = eu_ref[...].astype(BF16)
    edb_ref[...] = ed_ref[...].astype(BF16)
    xb = x_ref[...].astype(BF16)
    for c in range(w_ref.shape[1] // col_tile):
        cols = slice(c * col_tile, (c + 1) * col_tile)
        o_ref[:, cols] = jnp.dot(xb, w_ref[:, cols], preferred_element_type=F32).astype(o_ref.dtype)


def _proj(x2d, w, col_tile, expert_weights):
    T, N = x2d.shape[0], w.shape[1]
    steps = T // PROJ_ROWS
    n_experts = expert_weights[0].shape[0]
    per_step = max(1, n_experts // steps)
    assert n_experts % per_step == 0 and n_experts // per_step <= steps
    per_expert = lambda ew: pl.BlockSpec(
        (per_step,) + ew.shape[1:], lambda i: (jnp.minimum(i, n_experts // per_step - 1), 0, 0))
    return pl.pallas_call(
        functools.partial(_proj_kernel, col_tile=col_tile),
        grid=(steps,),
        in_specs=[pl.BlockSpec((PROJ_ROWS, D_MODEL), lambda i: (i, 0)),
                  pl.BlockSpec((D_MODEL, N), lambda i: (0, 0))] + [per_expert(ew) for ew in expert_weights],
        out_specs=[pl.BlockSpec((PROJ_ROWS, N), lambda i: (i, 0))] + [per_expert(ew) for ew in expert_weights],
        out_shape=[jax.ShapeDtypeStruct((T, N), BF16)]
                  + [jax.ShapeDtypeStruct(ew.shape, BF16) for ew in expert_weights],
        compiler_params=_cparams("parallel"),
        name="proj_hgrn_gates",
    )(x2d, w, *expert_weights)


def _split2(v):
    a = v.astype(BF16)
    return a, (v - a.astype(F32)).astype(BF16)


def _hgrn_kernel(q_ref, f_ref, i_ref, g_ref, lbl_ref, gain_ref, o_ref, state_ref):
    @pl.when(pl.program_id(1) == 0)
    def _():
        state_ref[...] = jnp.zeros_like(state_ref)

    lbl = lbl_ref[...]
    e = jnp.exp(lbl - jnp.max(lbl, axis=0, keepdims=True))
    lb = e[0:1] / jnp.sum(e, axis=0, keepdims=True)
    forget = lb + (1.0 - lb) * jax.nn.sigmoid(f_ref[...].astype(F32))
    log_f = jnp.log(forget)
    key = 1.0 - forget

    row = lax.broadcasted_iota(jnp.int32, (HG_TILE, HG_TILE), 0)
    col = lax.broadcasted_iota(jnp.int32, (HG_TILE, HG_TILE), 1)
    causal = (row >= col) & ((row // HG_CHUNK) == (col // HG_CHUNK))
    tri = jnp.where(causal, 1.0, 0.0).astype(BF16)
    b = sum(jnp.dot(tri, t, preferred_element_type=F32) for t in _split2(log_f))
    eb = jnp.exp(b)
    q_dec = (q_ref[...].astype(F32) * eb).astype(BF16)
    k_inv = key * jnp.exp(-b)
    xi = i_ref[...].astype(F32)
    val = (xi * jax.nn.sigmoid(xi)).astype(BF16)
    k_inv_b = k_inv.astype(BF16)

    n_chunks = HG_TILE // HG_CHUNK
    last_rows = [eb[(c + 1) * HG_CHUNK - 1:(c + 1) * HG_CHUNK, :] for c in range(n_chunks)]
    dec_rows = jnp.concatenate([jnp.broadcast_to(r, (HG_CHUNK, HG_WIDTH)) for r in last_rows], axis=0)
    k_end = (k_inv * dec_rows).astype(BF16)
    def per_chunk_columns(t):
        blocks = []
        for c in range(n_chunks):
            rows_above, rows_below = c * HG_CHUNK, HG_TILE - (c + 1) * HG_CHUNK
            parts = [t[rows_above:rows_above + HG_CHUNK]]
            if rows_above:
                parts.insert(0, jnp.zeros((rows_above, HG_DIM), t.dtype))
            if rows_below:
                parts.append(jnp.zeros((rows_below, HG_DIM), t.dtype))
            blocks.append(jnp.concatenate(parts, axis=0))
        return jnp.concatenate(blocks, axis=1)

    head_cols = [slice(h * HG_DIM, (h + 1) * HG_DIM) for h in range(HG_HEADS)]
    upds = [lax.dot_general(val[:, cols], per_chunk_columns(k_end[:, cols]), (((0,), (0,)), ((), ())),
                            preferred_element_type=F32) for cols in head_cols]
    o_intras = []
    for cols in head_cols:
        a = lax.dot_general(q_dec[:, cols], k_inv_b[:, cols], (((1,), (1,)), ((), ())), preferred_element_type=F32)
        a = jnp.where(causal, a, 0.0).astype(BF16)
        o_intras.append(jnp.dot(a, val[:, cols], preferred_element_type=F32))
    enterings = []
    for h, cols in enumerate(head_cols):
        st = state_ref[h]
        entering = []
        for c in range(n_chunks):
            entering.append(st.astype(BF16))
            st = st * last_rows[c][:, cols] + upds[h][:, c * HG_DIM:(c + 1) * HG_DIM]
        state_ref[h] = st
        enterings.append(jnp.concatenate(entering, axis=1))
    outs = []
    for h, cols in enumerate(head_cols):
        qd = q_dec[:, cols]
        o_inter = lax.dot_general(per_chunk_columns(qd), enterings[h],
                                  (((1,), (1,)), ((), ())), preferred_element_type=F32)
        o = o_intras[h] + o_inter
        o = o * lax.rsqrt(jnp.mean(jnp.square(o), axis=-1, keepdims=True) + RMS_EPS)
        outs.append(o)
    o = jnp.concatenate(outs, axis=1) * gain_ref[...]
    gg = g_ref[...].astype(F32)
    o_ref[...] = (o * (gg * jax.nn.sigmoid(gg))).astype(o_ref.dtype)


def _hgrn(u_hg, lb_logits, gain, B, S):
    tiles = S // HG_TILE
    col = lambda j: pl.BlockSpec((HG_TILE, HG_WIDTH), functools.partial(lambda b, t, j: (b * tiles + t, j), j=j))
    return pl.pallas_call(
        _hgrn_kernel,
        grid=(B, tiles),
        in_specs=[col(0), col(1), col(2), col(3),
                  pl.BlockSpec((2, HG_WIDTH), lambda b, t: (0, 0)),
                  pl.BlockSpec((1, HG_WIDTH), lambda b, t: (0, 0))],
        out_specs=pl.BlockSpec((HG_TILE, HG_WIDTH), lambda b, t: (b * tiles + t, 0)),
        out_shape=jax.ShapeDtypeStruct((B * S, HG_WIDTH), BF16),
        scratch_shapes=[pltpu.VMEM((HG_HEADS, HG_DIM, HG_DIM), F32)],
        compiler_params=_cparams("parallel", "arbitrary"),
        name="hgrn2",
    )(u_hg, u_hg, u_hg, u_hg, lb_logits, gain)


def _load_row_tiles(ref, n, start=0):
    return jnp.concatenate([ref[pl.ds(start + c, n, stride=LANE_CHUNKS), :] for c in range(LANE_CHUNKS)], axis=1)


def _store_row_tiles(ref, val, n):
    for c in range(LANE_CHUNKS):
        ref[pl.ds(c, n, stride=LANE_CHUNKS), :] = val[:, c * LANES:(c + 1) * LANES]


def _layer_norm(z, w, b):
    mu = jnp.mean(z, axis=-1, keepdims=True)
    zc = z - mu
    var = jnp.mean(jnp.square(zc), axis=-1, keepdims=True)
    return zc * lax.rsqrt(var + LN_EPS) * w + b


def _merge_kernel(ya_ref, yh_ref, ga_ref, gh_ref, x_ref, wa_ref, wh_ref, wo_ref, lw_ref, lb_ref, o_ref):
    ma = jnp.dot(ya_ref[...], wa_ref[...], preferred_element_type=F32)
    mh = jnp.dot(yh_ref[...], wh_ref[...], preferred_element_type=F32)
    merged = (jax.nn.sigmoid(ga_ref[...].astype(F32)) * ma + jax.nn.sigmoid(gh_ref[...].astype(F32)) * mh)
    z = DEEPNORM_ALPHA * x_ref[...] + jnp.dot(merged.astype(BF16), wo_ref[...], preferred_element_type=F32)
    _store_row_tiles(o_ref, _layer_norm(z, lw_ref[...], lb_ref[...]), MIX_ROWS)


def _merge(y_att, y_hg, u_hg, x2d, w_a, w_h, w_o, ln_w, ln_b):
    T = x2d.shape[0]
    rows = lambda width, j=0: pl.BlockSpec((MIX_ROWS, width), functools.partial(lambda i, j: (i, j), j=j))
    full = lambda a: pl.BlockSpec(a.shape, lambda i: (0, 0))
    return pl.pallas_call(
        _merge_kernel,
        grid=(T // MIX_ROWS,),
        in_specs=[rows(ATT_WIDTH), rows(HG_WIDTH), rows(D_MODEL, 4), rows(D_MODEL, 5), rows(D_MODEL),
                  full(w_a), full(w_h), full(w_o), full(ln_w), full(ln_b)],
        out_specs=pl.BlockSpec((MIX_ROWS * LANE_CHUNKS, LANES), lambda i: (i, 0)),
        out_shape=jax.ShapeDtypeStruct((T * LANE_CHUNKS, LANES), F32),
        compiler_params=_cparams("parallel"),
        name="merge_ln1",
    )(y_att, y_hg, u_hg, u_hg, x2d, w_a, w_h, w_o, ln_w, ln_b)


def _first_argmax(v, ids, n):
    mx = jnp.max(v, axis=0, keepdims=True)
    return mx, jnp.min(jnp.where(v == mx, ids, n), axis=0, keepdims=True)


def _route_kernel(x1_ref, p_ref, wrt_ref, rb_ref, wsg_ref, wsu_ref, wsd_ref, wpg_ref, wpp_ref,
                  base_ref, idx_ref, gate_ref, rank_ref, cnt_ref, carry_ref):
    @pl.when(pl.program_id(0) % (MOE_TILE // MIX_ROWS) == 0)
    def _():
        carry_ref[...] = jnp.zeros_like(carry_ref)

    x1 = _load_row_tiles(x1_ref, MIX_ROWS)
    x1b = x1.astype(BF16)
    logits = lax.dot_general(wrt_ref[...], x1, (((1,), (1,)), ((), ())), preferred_element_type=F32,
                             precision=lax.Precision.HIGHEST)
    hg = jnp.dot(x1b, wsg_ref[...], preferred_element_type=F32)
    hu = jnp.dot(x1b, wsu_ref[...], preferred_element_type=F32)
    ple_gate = jnp.dot(x1b, wpg_ref[...], preferred_element_type=F32)
    ple_proj = jnp.dot(p_ref[...].astype(BF16), wpp_ref[...], preferred_element_type=F32)

    s = jax.nn.sigmoid(logits)
    sel = s + rb_ref[...]
    eid = lax.broadcasted_iota(jnp.int32, (N_EXPERTS, MIX_ROWS), 0)
    neg = -jnp.inf

    grp = sel.reshape(N_GROUPS, GROUP_SIZE, MIX_ROWS)
    mid = lax.broadcasted_iota(jnp.int32, grp.shape, 1)
    m1 = jnp.max(grp, axis=1, keepdims=True)
    i1 = jnp.min(jnp.where(grp == m1, mid, GROUP_SIZE), axis=1, keepdims=True)
    m2 = jnp.max(jnp.where(mid == i1, neg, grp), axis=1, keepdims=True)
    gscore = (m1 + m2).reshape(N_GROUPS, MIX_ROWS)
    gid = lax.broadcasted_iota(jnp.int32, (N_GROUPS, MIX_ROWS), 0)
    gsel = jnp.zeros((N_GROUPS, MIX_ROWS), jnp.bool_)
    for _ in range(TOPK_GROUPS):
        _, gi = _first_argmax(gscore, gid, N_GROUPS)
        hit = gid == gi
        gsel = gsel | hit
        gscore = jnp.where(hit, neg, gscore)
    emask = jnp.broadcast_to(gsel.reshape(N_GROUPS, 1, MIX_ROWS), grp.shape).reshape(N_EXPERTS, MIX_ROWS)
    cand = jnp.where(emask, sel, neg)

    idxs, gates = [], []
    chosen = jnp.zeros((N_EXPERTS, MIX_ROWS), jnp.bool_)
    for _ in range(TOP_K):
        _, ei = _first_argmax(cand, eid, N_EXPERTS)
        hit = eid == ei
        idxs.append(ei)
        gates.append(jnp.sum(jnp.where(hit, s, 0.0), axis=0, keepdims=True))
        chosen = chosen | hit
        cand = jnp.where(hit, neg, cand)
    g = jnp.concatenate(gates, axis=0)
    g = g / jnp.sum(g, axis=0, keepdims=True) * ROUTED_SCALE
    idx_ref[...] = jnp.concatenate(idxs, axis=0)
    gate_ref[...] = g

    onehot = jnp.where(chosen, 1.0, 0.0)
    tr = lax.broadcasted_iota(jnp.int32, (MIX_ROWS, MIX_ROWS), 0)
    tc = lax.broadcasted_iota(jnp.int32, (MIX_ROWS, MIX_ROWS), 1)
    before = jnp.where(tr < tc, 1.0, 0.0).astype(BF16)
    prefix = jnp.dot(onehot.astype(BF16), before, preferred_element_type=F32)
    rankfull = (carry_ref[:, 0:1] + prefix).astype(jnp.int32)
    rank_ref[...] = jnp.concatenate(
        [jnp.sum(jnp.where(eid == ei, rankfull, 0), axis=0, keepdims=True) for ei in idxs], axis=0)
    total = carry_ref[...] + jnp.sum(onehot, axis=1, keepdims=True)
    carry_ref[...] = total
    cnt_ref[...] = total.astype(jnp.int32)

    shared = jnp.dot((hg * jax.nn.sigmoid(hg) * hu).astype(BF16), wsd_ref[...], preferred_element_type=F32)
    ple = jax.nn.sigmoid(ple_gate) * ple_proj
    _store_row_tiles(base_ref, DEEPNORM_ALPHA * x1 + shared + ple, MIX_ROWS)


def _route(x1, p2d, wr_t, rbias, wsg, wsu, wsd, wpg, wpp):
    T = x1.shape[0] // LANE_CHUNKS
    per_tile = MOE_TILE // MIX_ROWS
    full = lambda a: pl.BlockSpec(a.shape, lambda i: (0, 0))
    tok = pl.BlockSpec((TOP_K, MIX_ROWS), lambda i: (0, i))
    row_tiles = pl.BlockSpec((MIX_ROWS * LANE_CHUNKS, LANES), lambda i: (i, 0))
    return pl.pallas_call(
        _route_kernel,
        grid=(T // MIX_ROWS,),
        in_specs=[row_tiles,
                  pl.BlockSpec((MIX_ROWS, PLE_DIM), lambda i: (i, 0)),
                  full(wr_t), full(rbias), full(wsg), full(wsu), full(wsd), full(wpg), full(wpp)],
        out_specs=[row_tiles, tok, tok, tok,
                   pl.BlockSpec((None, N_EXPERTS, LANES), lambda i: (i // per_tile, 0, 0))],
        out_shape=[jax.ShapeDtypeStruct((T * LANE_CHUNKS, LANES), F32),
                   jax.ShapeDtypeStruct((TOP_K, T), jnp.int32),
                   jax.ShapeDtypeStruct((TOP_K, T), F32),
                   jax.ShapeDtypeStruct((TOP_K, T), jnp.int32),
                   jax.ShapeDtypeStruct((T // MOE_TILE, N_EXPERTS, LANES), jnp.int32)],
        scratch_shapes=[pltpu.VMEM((N_EXPERTS, LANES), F32)],
        compiler_params=_cparams("arbitrary"),
        name="route_shared_ple",
    )(x1, p2d, wr_t, rbias, wsg, wsu, wsd, wpg, wpp)


def _plan_sc_kernel(off_hbm, idx_hbm, rank_hbm, gate_hbm, rows_hbm, gates_hbm, seg_hbm,
                    off_v, idx_v, rank_v, gate_v, rows_v, gates_v, seg_v):
    n_tokens = idx_hbm.shape[0] // TOP_K
    worker = lax.axis_index("subcore") * SC_CORES + lax.axis_index("core")

    @pl.when(worker < n_tokens // MOE_TILE)
    def _():
        pltpu.sync_copy(off_hbm.at[pl.ds(worker * N_EXPERTS, N_EXPERTS)], off_v)
        lane = lax.iota(jnp.int32, SC_LANES)

        @pl.loop(0, N_EXPERTS * GATE_SEG // SC_LANES)
        def _(i):
            seg_v[pl.ds(i * SC_LANES, SC_LANES)] = jnp.zeros((SC_LANES,), F32)

        for k in range(TOP_K):
            row = pl.ds(k * n_tokens + worker * MOE_TILE, MOE_TILE)
            pltpu.sync_copy(idx_hbm.at[row], idx_v)
            pltpu.sync_copy(rank_hbm.at[row], rank_v)
            pltpu.sync_copy(gate_hbm.at[row], gate_v)

            @pl.loop(0, MOE_TILE // SC_LANES)
            def _(i):
                at = i * SC_LANES
                expert, rank, gate = (v[pl.ds(at, SC_LANES)] for v in (idx_v, rank_v, gate_v))
                pos = plsc.load_gather(off_v, [expert]) + rank
                plsc.store_scatter(rows_v, [pos], (lane + at) * LANE_CHUNKS)
                plsc.store_scatter(gates_v, [pos], gate)
                plsc.store_scatter(seg_v, [expert * GATE_SEG + rank], gate, mask=rank < GATE_SEG)

        @pl.loop(0, LIST_PAD // SC_LANES)
        def _(i):
            tail = pl.ds(MOE_TILE * TOP_K + i * SC_LANES, SC_LANES)
            rows_v[tail] = jnp.zeros((SC_LANES,), jnp.int32)
            gates_v[tail] = jnp.zeros((SC_LANES,), F32)

        out = pl.ds(worker * LIST_LEN, LIST_LEN)
        pltpu.sync_copy(rows_v, rows_hbm.at[out])
        pltpu.sync_copy(gates_v, gates_hbm.at[out])
        pltpu.sync_copy(seg_v, seg_hbm.at[pl.ds(worker * (N_EXPERTS * GATE_SEG), N_EXPERTS * GATE_SEG)])


def _plan_sc(off, idx, rank, gate):
    n_tiles = idx.shape[1] // MOE_TILE
    assert n_tiles <= SC_CORES * SC_SUBCORES
    mesh = plsc.VectorSubcoreMesh(core_axis_name="core", subcore_axis_name="subcore",
                                  num_cores=SC_CORES, num_subcores=SC_SUBCORES)
    rows, gates, seg = pl.kernel(
        _plan_sc_kernel,
        out_type=(jax.ShapeDtypeStruct((n_tiles * LIST_LEN,), jnp.int32),
                  jax.ShapeDtypeStruct((n_tiles * LIST_LEN,), F32),
                  jax.ShapeDtypeStruct((n_tiles * N_EXPERTS * GATE_SEG,), F32)),
        mesh=mesh,
        scratch_types=[pltpu.VMEM((N_EXPERTS,), jnp.int32), pltpu.VMEM((MOE_TILE,), jnp.int32),
                       pltpu.VMEM((MOE_TILE,), jnp.int32), pltpu.VMEM((MOE_TILE,), F32),
                       pltpu.VMEM((LIST_LEN,), jnp.int32), pltpu.VMEM((LIST_LEN,), F32),
                       pltpu.VMEM((N_EXPERTS * GATE_SEG,), F32)],
        compiler_params=pltpu.CompilerParams(needs_layout_passes=False),
        name="moe_plan_sc",
    )(off, idx.reshape(-1), rank.reshape(-1), gate.reshape(-1))
    return rows, gates, seg.reshape(n_tiles * N_EXPERTS, 1, GATE_SEG)


def _moe_tile_kernel(cnt_ref, off_ref, wg_ref, wu_ref, wd_ref, lw_ref, lb_ref, seg_ref, rows_hbm, gates_hbm, x_hbm,
                     base_hbm, o_hbm, x_s, acc_s, xg_a, xg_b, xg_c, y_a, y_b, y_c, stage_s, rows_s, gates_s, sem):
    tile, e = pl.program_id(0), pl.program_id(1)
    rows_of = lambda ref, r, n: ref.at[pl.ds(pl.multiple_of(r * LANE_CHUNKS, LANE_CHUNKS), n * LANE_CHUNKS), :]
    tile_rows = pl.ds(pl.multiple_of(tile * (MOE_TILE * LANE_CHUNKS), LANE_CHUNKS), MOE_TILE * LANE_CHUNKS)

    pair = tile * N_EXPERTS + e
    last_pair = pl.num_programs(0) * N_EXPERTS - 1
    n, off = cnt_ref[pair], off_ref[pair]
    tile_at = lambda ref, r: ref.at[pl.ds(pl.multiple_of(r, LANE_CHUNKS), LANE_CHUNKS), :]

    def gather_group(xg, first, jb):
        at = first + jb * GATHER_GROUP
        rows = [tile_at(x_s, rows_s[at + u])[...] for u in range(GATHER_GROUP)]
        rows_of(xg, jb * GATHER_GROUP, GATHER_GROUP)[...] = jnp.concatenate(rows, axis=0)

    def gather_loop(xg, first):
        def body(jb, cc):
            gather_group(xg, first, jb)
            return cc

        lax.fori_loop(0, MOE_CHUNK // GATHER_GROUP, body, 0)

    def swiglu(xg, y):
        xb = _load_row_tiles(xg, MOE_CHUNK).astype(BF16)
        hg = jnp.dot(xb, wg_ref[...], preferred_element_type=F32)
        hu = jnp.dot(xb, wu_ref[...], preferred_element_type=F32)
        act = (hg * jax.nn.sigmoid(hg) * hu).astype(BF16)
        _store_row_tiles(y, jnp.dot(act, wd_ref[...], preferred_element_type=F32), MOE_CHUNK)

    def scatter_group(y, first, j0, live, gated):
        dsts = [rows_s[first + j0 + u] for u in range(live)]
        yv = rows_of(y, j0, live)[...]
        rows = [yv[u * LANE_CHUNKS:(u + 1) * LANE_CHUNKS] for u in range(live)]
        if not gated:
            rows = [gates_s[first + j0 + u] * r for u, r in enumerate(rows)]
        vals = [tile_at(acc_s, d)[...] + r for d, r in zip(dsts, rows)]
        for d, val in reversed(list(zip(dsts, vals))):
            tile_at(acc_s, d)[...] = val

    def scatter_loop(y, first, m, gated):
        def body(jg, cc):
            scatter_group(y, first, jg * SCATTER_GROUP, SCATTER_GROUP, gated)
            return cc

        lax.fori_loop(0, m // SCATTER_GROUP, body, 0)
        for live in range(1, SCATTER_GROUP):
            @pl.when(m % SCATTER_GROUP == live)
            def _(live=live):
                scatter_group(y, first, m - live, live, gated)

    def plan_loads(t):
        plan = pl.ds(pl.multiple_of(t * LIST_LEN, LIST_PAD), LIST_LEN)
        return (pltpu.make_async_copy(rows_hbm.at[plan], rows_s, sem.at[0]),
                pltpu.make_async_copy(gates_hbm.at[plan], gates_s, sem.at[1]))

    def x_load(t):
        rows = pl.ds(pl.multiple_of(t * (MOE_TILE * LANE_CHUNKS), LANE_CHUNKS), MOE_TILE * LANE_CHUNKS)
        return pltpu.make_async_copy(x_hbm.at[rows, :], x_s, sem.at[2])

    @pl.when(e == 0)
    def _():
        @pl.when(tile == 0)
        def _():
            x_load(tile).start()
            for load in plan_loads(tile):
                load.start()

        load_base = pltpu.make_async_copy(base_hbm.at[tile_rows, :], acc_s, sem.at[3])
        load_base.start()
        y_b[...] = jnp.zeros_like(y_b)
        load_rows, load_gates = plan_loads(tile)
        load_rows.wait()
        x_load(tile).wait()
        gather_loop(xg_a, off)
        load_gates.wait()
        load_base.wait()

    prev_off = off_ref[jnp.maximum(pair - 1, 0)]
    next_off = off_ref[jnp.minimum(pair + 1, last_pair)]
    live_row = lax.broadcasted_iota(jnp.int32, (MOE_CHUNK, 2 * LANES), 0) < n

    def run_expert(xg_cur, y_cur, xg_nxt, y_prv):
        gathers = [functools.partial(gather_group, xg_nxt, next_off, jb) for jb in range(MOE_CHUNK // GATHER_GROUP)]
        scatters = [functools.partial(scatter_group, y_prv, prev_off, jg * SCATTER_GROUP, SCATTER_GROUP, True)
                    for jg in range(MOE_CHUNK // SCATTER_GROUP)]
        side = [s for both in zip(gathers, scatters) for s in both]
        n_down = LANE_CHUNKS // 2
        cost = [D_MODEL, D_MODEL] + [EXPERT_FF] * n_down
        bounds = [round(len(side) * sum(cost[:i]) / sum(cost)) for i in range(len(cost) + 1)]

        def side_work(i):
            for s in side[bounds[i]:bounds[i + 1]]:
                s()

        xb = _load_row_tiles(xg_cur, MOE_CHUNK).astype(BF16)
        side_work(0)
        hg = jnp.dot(xb, wg_ref[...], preferred_element_type=F32)
        side_work(1)
        hu = jnp.dot(xb, wu_ref[...], preferred_element_type=F32)
        gate_col = jnp.broadcast_to(seg_ref[...], (LANES, GATE_SEG)).T[:MOE_CHUNK]
        gate_col = jnp.concatenate([gate_col] * (EXPERT_FF // LANES), axis=1)
        act = (hg * jax.nn.sigmoid(hg) * hu * gate_col).astype(BF16)
        for q in range(n_down):
            side_work(2 + q)
            out = jnp.dot(act, wd_ref[:, q * 2 * LANES:(q + 1) * 2 * LANES], preferred_element_type=F32)
            out = jnp.where(live_row, out, 0.0)
            for c in range(2):
                y_cur[pl.ds(2 * q + c, MOE_CHUNK, stride=LANE_CHUNKS), :] = out[:, c * LANES:(c + 1) * LANES]

    @pl.when(e % 2 == 0)
    def _():
        run_expert(xg_a, y_a, xg_b, y_b)

    @pl.when(e % 2 == 1)
    def _():
        run_expert(xg_b, y_b, xg_a, y_a)

    def extra_chunk(c, carry):
        first = off + c * MOE_CHUNK
        gather_loop(xg_c, first)
        swiglu(xg_c, y_c)
        scatter_loop(y_c, first, jnp.minimum(MOE_CHUNK, n - c * MOE_CHUNK), False)
        return carry

    lax.fori_loop(1, (n + MOE_CHUNK - 1) // MOE_CHUNK, extra_chunk, 0)

    @pl.when(e == N_EXPERTS - 1)
    def _():
        more_tiles = tile + 1 < pl.num_programs(0)

        @pl.when(more_tiles)
        def _():
            x_load(tile + 1).start()

        scatter_loop(y_b, off, jnp.minimum(MOE_CHUNK, n), True)

        @pl.when(more_tiles)
        def _():
            for load in plan_loads(tile + 1):
                load.start()

        n_pieces = MOE_TILE // LN_ROWS
        store = lambda c: pltpu.make_async_copy(
            stage_s.at[c % 2], o_hbm.at[pl.ds(tile * MOE_TILE + c * LN_ROWS, LN_ROWS), :], sem.at[4 + c % 2])
        for c in range(n_pieces):
            z = _load_row_tiles(acc_s, LN_ROWS, c * LN_ROWS * LANE_CHUNKS)
            if c >= 2:
                store(c - 2).wait()
            stage_s[c % 2] = _layer_norm(z, lw_ref[...], lb_ref[...])
            store(c).start()
        store(n_pieces - 2).wait()
        store(n_pieces - 1).wait()


def _moe_tiles(x1, base, plan_rows, plan_gates, gate_seg, cnt, off, wg, wu, wd, ln_w, ln_b):
    T = x1.shape[0] // LANE_CHUNKS
    w_spec = lambda shape: pl.BlockSpec((None,) + shape, lambda i, e, cnt, off: (e, 0, 0))
    vec = pl.BlockSpec((1, D_MODEL), lambda i, e, cnt, off: (0, 0))
    hbm = pl.BlockSpec(memory_space=pl.ANY)
    tile_rows = MOE_TILE * LANE_CHUNKS
    return pl.pallas_call(
        _moe_tile_kernel,
        grid_spec=pltpu.PrefetchScalarGridSpec(
            num_scalar_prefetch=2,
            grid=(T // MOE_TILE, N_EXPERTS),
            in_specs=[w_spec((D_MODEL, EXPERT_FF)), w_spec((D_MODEL, EXPERT_FF)), w_spec((EXPERT_FF, D_MODEL)),
                      vec, vec,
                      pl.BlockSpec((None, 1, GATE_SEG), lambda i, e, cnt, off: (i * N_EXPERTS + e, 0, 0)),
                      hbm, hbm, hbm, hbm],
            out_specs=hbm,
            scratch_shapes=[pltpu.VMEM((tile_rows, LANES), F32),
                            pltpu.VMEM((tile_rows, LANES), F32)]
                           + [pltpu.VMEM((MOE_CHUNK * LANE_CHUNKS, LANES), F32)] * 6
                           + [pltpu.VMEM((2, LN_ROWS, D_MODEL), F32),
                              pltpu.SMEM((LIST_LEN,), jnp.int32),
                              pltpu.SMEM((LIST_LEN,), F32),
                              pltpu.SemaphoreType.DMA((6,))],
        ),
        out_shape=jax.ShapeDtypeStruct((T, D_MODEL), F32),
        compiler_params=_cparams("arbitrary", "arbitrary"),
        name="moe_tiles_ln2",
    )(cnt, off, wg, wu, wd, ln_w, ln_b, gate_seg, plan_rows, plan_gates, x1, base)


def kernel(x, p, w_in, hgrn_lb_logits, hgrn_norm_w, w_branch_att, w_branch_hgrn, w_out, ln1_w, ln1_b, router_w, router_bias, expert_w_gate, expert_w_up, expert_w_down, shared_w_gate, shared_w_up, shared_w_down, ple_gate_w, ple_proj_w, ln2_w, ln2_b):
    B, S, D = x.shape
    T = B * S
    l = 0
    x2d = x.reshape(T, D)
    bf = lambda a: a.astype(BF16)

    ws = _att_weights(w_in[l])
    qkv = [_proj_att(x2d, ws[g], d) for g, d in enumerate(ATT_DILATIONS)]
    y_att = _attention(qkv, B, S)
    u_hg, wg_b, wu_b, wd_b = _proj(x2d, bf(w_in[l][:, 3 * len(ATT_DILATIONS) * ATT_WIDTH:]), PROJ_COLS,
                                   (expert_w_gate[l], expert_w_up[l], expert_w_down[l]))
    y_hg = _hgrn(u_hg, hgrn_lb_logits, hgrn_norm_w[l:l + 1], B, S)
    x1 = _merge(y_att, y_hg, u_hg, x2d, bf(w_branch_att[l]), bf(w_branch_hgrn[l]), bf(w_out[l]),
                ln1_w[l:l + 1], ln1_b[l:l + 1])

    base, idx, gate, rank, counts = _route(
        x1, p[l].reshape(T, PLE_DIM), router_w[l].T, router_bias[l].reshape(N_EXPERTS, 1),
        bf(shared_w_gate[l]), bf(shared_w_up[l]), bf(shared_w_down[l]), bf(ple_gate_w[l]), bf(ple_proj_w[l]))
    cnt = counts[:, :, 0]
    off = jnp.cumsum(cnt, axis=1) - cnt
    cnt, off = cnt.reshape(-1), off.reshape(-1)
    plan_rows, plan_gates, gate_seg = _plan_sc(off, idx, rank, gate)
    out = _moe_tiles(x1, base, plan_rows, plan_gates, gate_seg, cnt, off, wg_b, wu_b, wd_b,
                     ln2_w[l:l + 1], ln2_b[l:l + 1])
    return out.reshape(B, S, D)
```

```python
import functools

import jax
import jax.numpy as jnp
import numpy as np
from jax import lax
from jax.experimental import pallas as pl
from jax.experimental.pallas import tpu as pltpu
from jax.experimental.pallas import tpu_sc as plsc

F32 = jnp.float32
BF16 = jnp.bfloat16

D_MODEL = 1024
ATT_HEAD_DIM = 64
ATT_HEADS = 8
ATT_DILATIONS = (1, 4, 16)
ATT_BLOCK = 128
ATT_WIDTH = ATT_HEADS * ATT_HEAD_DIM
ATT_TILE = ATT_BLOCK * max(ATT_DILATIONS)
NEG_INF = -1e30
LOG2_E = 1.4426950408889634

HG_HEADS = 8
HG_DIM = 128
HG_WIDTH = HG_HEADS * HG_DIM
HG_CHUNK = 32
HG_TILE = 256
RMS_EPS = 1e-6

N_EXPERTS = 64
TOP_K = 8
N_GROUPS = 8
GROUP_SIZE = N_EXPERTS // N_GROUPS
TOPK_GROUPS = 4
EXPERT_FF = 256
ROUTED_SCALE = 2.5
PLE_DIM = 256
LN_EPS = 1e-5
DEPTH = 1
DEEPNORM_ALPHA = (2.0 * DEPTH) ** 0.25

LANES = 128
LANE_CHUNKS = D_MODEL // LANES
PROJ_ROWS = 512
PROJ_COLS = 1536
ATT_PROJ_ROWS = 1024
MIX_ROWS = 512
MOE_TILE = 4096
MOE_CHUNK = 576
GATE_SEG = 640
LN_ROWS = 256
LIST_PAD = 1024
LIST_LEN = MOE_TILE * TOP_K + LIST_PAD
GATHER_GROUP = 8
SCATTER_GROUP = 8
V7X_VMEM_LIMIT = 56 * 1024 * 1024
SC_CORES, SC_SUBCORES, SC_LANES = 2, 16, 16


def _cparams(*sem):
    return pltpu.CompilerParams(dimension_semantics=sem, vmem_limit_bytes=V7X_VMEM_LIMIT)


def _proj_att_kernel(*refs, dil):
    x_refs, w_ref, o_ref = refs[:LANE_CHUNKS], refs[LANE_CHUNKS], refs[LANE_CHUNKS + 1]
    n = ATT_PROJ_ROWS // dil

    def rows(ref):
        if dil == 1:
            return ref[...]
        return jnp.concatenate([ref[pl.ds(r, n, stride=dil), :] for r in range(dil)], axis=0)

    xp = jnp.concatenate([rows(ref).astype(BF16) for ref in x_refs], axis=1)
    y = jnp.dot(xp, w_ref[...], preferred_element_type=F32)
    o_ref[...] = y.astype(BF16).reshape(dil, n, 3 * ATT_WIDTH)


def _proj_att(x2d, w, dil):
    T = x2d.shape[0]
    per = ATT_TILE // ATT_PROJ_ROWS
    n = ATT_PROJ_ROWS // dil
    out = pl.pallas_call(
        functools.partial(_proj_att_kernel, dil=dil),
        grid=(T // ATT_PROJ_ROWS,),
        in_specs=[pl.BlockSpec((ATT_PROJ_ROWS, LANES), functools.partial(lambda i, c: (i, c), c=c))
                  for c in range(LANE_CHUNKS)]
                 + [pl.BlockSpec((D_MODEL, 3 * ATT_WIDTH), lambda i: (0, 0))],
        out_specs=pl.BlockSpec((None, dil, None, n, 3 * ATT_WIDTH), lambda i: (i // per, 0, i % per, 0, 0)),
        out_shape=jax.ShapeDtypeStruct((T // ATT_TILE, dil, per, n, 3 * ATT_WIDTH), BF16),
        compiler_params=_cparams("parallel"),
        name=f"proj_att_d{dil}",
    )(*([x2d] * LANE_CHUNKS), w)
    return out.reshape(T // ATT_TILE, dil, ATT_TILE // dil, 3 * ATT_WIDTH)


def _att_pair(q2, kp, kc, vp, vc, bias_ref, g, first):
    def head0_lanes(rows, dtype):
        lane = lax.broadcasted_iota(jnp.int32, (rows, 2 * ATT_HEAD_DIM), 1)
        return lane.astype(F32).astype(dtype) < ATT_HEAD_DIM

    lo_q = head0_lanes(ATT_BLOCK, BF16)
    lo_v = head0_lanes(2 * ATT_BLOCK, BF16)
    k2 = jnp.concatenate([kp, kc], axis=0)
    v2 = jnp.concatenate([vp, vc], axis=0)
    zero = jnp.zeros_like(q2)
    ps, ms = [], []
    for hh in range(2):
        qm = jnp.where(lo_q, q2, zero) if hh == 0 else jnp.where(lo_q, zero, q2)
        s = lax.dot_general(qm, k2, (((1,), (1,)), ((), ())), preferred_element_type=F32)
        s = s + bias_ref[g, hh, first]
        m = jnp.max(s, axis=-1, keepdims=True)
        ps.append(jnp.exp2(s - m).astype(BF16))
        ms.append(m)
    pcat = jnp.concatenate(ps, axis=1)
    zero_v, one_v = jnp.zeros_like(v2), jnp.ones_like(v2)
    rhs = jnp.concatenate([
        jnp.concatenate([jnp.where(lo_v, v2, zero_v), jnp.where(lo_v, one_v, zero_v)], axis=1),
        jnp.concatenate([jnp.where(lo_v, zero_v, v2), jnp.where(lo_v, zero_v, one_v)], axis=1)], axis=0)
    nd = jnp.dot(pcat, rhs, preferred_element_type=F32)
    m2 = jnp.where(head0_lanes(ATT_BLOCK, F32), ms[0], ms[1])
    return nd[:, :2 * ATT_HEAD_DIM], m2, nd[:, 2 * ATT_HEAD_DIM:]


def _att_kernel(*refs):
    (q0, kc0, vc0, kp0, vp0, q1, kc1, vc1, kp1, vp1, q2, kc2, vc2, kp2, vp2,
     bias_ref, o_ref) = refs[:17]
    ng = len(ATT_DILATIONS)
    num_s, m_s, den_s = refs[17:17 + ng], refs[17 + ng:17 + 2 * ng], refs[17 + 2 * ng:]
    first_tile = (pl.program_id(2) == 0).astype(jnp.int32)
    groups = ((q0, kc0, vc0, kp0, vp0), (q1, kc1, vc1, kp1, vp1), (q2, kc2, vc2, kp2, vp2))
    for g, dil in enumerate(ATT_DILATIONS):
        q_ref, kc_ref, vc_ref, kp_ref, vp_ref = groups[g]
        nb = ATT_TILE // dil // ATT_BLOCK
        for r in range(dil):
            for n in range(nb):
                rows = pl.ds(n * ATT_BLOCK, ATT_BLOCK)
                if n == 0:
                    prev = pl.ds((nb - 1) * ATT_BLOCK, ATT_BLOCK)
                    kp, vp, first = kp_ref[r, prev, :], vp_ref[r, prev, :], first_tile
                else:
                    prev = pl.ds((n - 1) * ATT_BLOCK, ATT_BLOCK)
                    kp, vp, first = kc_ref[r, prev, :], vc_ref[r, prev, :], 0
                num, m, den = _att_pair(q_ref[r, rows, :], kp, kc_ref[r, rows, :], vp, vc_ref[r, rows, :],
                                        bias_ref, g, first)
                if dil == 1:
                    dst = rows
                else:
                    dst = pl.ds(n * ATT_BLOCK * dil + r, ATT_BLOCK, stride=dil)
                num_s[g][dst, :] = num
                m_s[g][dst, :] = m
                den_s[g][dst, :] = den
    m_all = jnp.maximum(jnp.maximum(m_s[0][...], m_s[1][...]), m_s[2][...])
    num = jnp.zeros((ATT_TILE, 2 * ATT_HEAD_DIM), F32)
    den = jnp.zeros((ATT_TILE, 2 * ATT_HEAD_DIM), F32)
    for g in range(ng):
        sc = jnp.exp2(m_s[g][...] - m_all)
        num = num + sc * num_s[g][...]
        den = den + sc * den_s[g][...]
    o_ref[...] = (num / den).astype(o_ref.dtype)


def _att_bias_table():
    qi = np.arange(ATT_BLOCK)[:, None]
    ki = np.arange(2 * ATT_BLOCK)[None, :]
    steps = qi + ATT_BLOCK - ki
    valid = (steps >= 0) & (steps <= ATT_BLOCK)
    slopes = np.array([2.0 ** (-8.0 * (h + 1) / ATT_HEADS) for h in range(ATT_HEADS)], np.float32)
    tab = np.empty((len(ATT_DILATIONS), ATT_HEADS, 2, ATT_BLOCK, 2 * ATT_BLOCK), np.float32)
    for g, dil in enumerate(ATT_DILATIONS):
        bias = -slopes[:, None, None] * (steps * dil).astype(np.float32)[None] * LOG2_E
        tab[g, :, 0] = np.where(valid[None], bias, NEG_INF)
        tab[g, :, 1] = np.where((valid & (ki >= ATT_BLOCK))[None], bias, NEG_INF)
    return jnp.asarray(tab)


def _attention(qkv, B, S):
    tiles = S // ATT_TILE
    pair = 2 * ATT_HEAD_DIM
    npair = ATT_WIDTH // pair
    in_specs, args = [], []
    for g, dil in enumerate(ATT_DILATIONS):
        blk = (None, dil, ATT_TILE // dil, pair)
        cur = lambda b, hp, t, off: (b * tiles + t, 0, 0, off * npair + hp)
        prv = lambda b, hp, t, off: (b * tiles + jnp.maximum(t - 1, 0), 0, 0, off * npair + hp)
        in_specs += [pl.BlockSpec(blk, functools.partial(cur, off=0)),
                     pl.BlockSpec(blk, functools.partial(cur, off=1)),
                     pl.BlockSpec(blk, functools.partial(cur, off=2)),
                     pl.BlockSpec(blk, functools.partial(prv, off=1)),
                     pl.BlockSpec(blk, functools.partial(prv, off=2))]
        args += [qkv[g]] * 5
    in_specs.append(pl.BlockSpec((len(ATT_DILATIONS), 2, 2, ATT_BLOCK, 2 * ATT_BLOCK),
                                 lambda b, hp, t: (0, hp, 0, 0, 0)))
    args.append(_att_bias_table())
    scratch = [pltpu.VMEM((ATT_TILE, pair), F32) for _ in range(3 * len(ATT_DILATIONS))]
    return pl.pallas_call(
        _att_kernel,
        grid=(B, npair, tiles),
        in_specs=in_specs,
        out_specs=pl.BlockSpec((ATT_TILE, pair), lambda b, hp, t: (b * tiles + t, hp)),
        out_shape=jax.ShapeDtypeStruct((B * S, ATT_WIDTH), BF16),
        scratch_shapes=scratch,
        compiler_params=_cparams("parallel", "parallel", "arbitrary"),
        name="dilated_attention",
    )(*args)


def _att_weights(w_in_l):
    out = []
    width = len(ATT_DILATIONS) * ATT_WIDTH
    for g in range(len(ATT_DILATIONS)):
        cols = [w_in_l[:, part * width + g * ATT_WIDTH: part * width + (g + 1) * ATT_WIDTH] for part in range(3)]
        cols[0] = cols[0] * (ATT_HEAD_DIM ** -0.5 * LOG2_E)
        out.append(jnp.concatenate(cols, axis=1).astype(BF16))
    return out


def _proj_kernel(x_ref, w_ref, eg_ref, eu_ref, ed_ref, o_ref, egb_ref, eub_ref, edb_ref, *, col_tile):
    egb_ref[...] = eg_ref[...].astype(BF16)
    eub_ref[...] = eu_ref[...].astype(BF16)
    edb_ref[...] = ed_ref[...].astype(BF16)
    xb = x_ref[...].astype(BF16)
    for c in range(w_ref.shape[1] // col_tile):
        cols = slice(c * col_tile, (c + 1) * col_tile)
        o_ref[:, cols] = jnp.dot(xb, w_ref[:, cols], preferred_element_type=F32).astype(o_ref.dtype)


def _proj(x2d, w, col_tile, expert_weights):
    T, N = x2d.shape[0], w.shape[1]
    steps = T // PROJ_ROWS
    n_experts = expert_weights[0].shape[0]
    per_step = max(1, n_experts // steps)
    assert n_experts % per_step == 0 and n_experts // per_step <= steps
    per_expert = lambda ew: pl.BlockSpec(
        (per_step,) + ew.shape[1:], lambda i: (jnp.minimum(i, n_experts // per_step - 1), 0, 0))
    return pl.pallas_call(
        functools.partial(_proj_kernel, col_tile=col_tile),
        grid=(steps,),
        in_specs=[pl.BlockSpec((PROJ_ROWS, D_MODEL), lambda i: (i, 0)),
                  pl.BlockSpec((D_MODEL, N), lambda i: (0, 0))] + [per_expert(ew) for ew in expert_weights],
        out_specs=[pl.BlockSpec((PROJ_ROWS, N), lambda i: (i, 0))] + [per_expert(ew) for ew in expert_weights],
        out_shape=[jax.ShapeDtypeStruct((T, N), BF16)]
                  + [jax.ShapeDtypeStruct(ew.shape, BF16) for ew in expert_weights],
        compiler_params=_cparams("parallel"),
        name="proj_hgrn_gates",
    )(x2d, w, *expert_weights)


def _split2(v):
    a = v.astype(BF16)
    return a, (v - a.astype(F32)).astype(BF16)


def _hgrn_kernel(q_ref, f_ref, i_ref, g_ref, lbl_ref, gain_ref, o_ref, state_ref):
    @pl.when(pl.program_id(1) == 0)
    def _():
        state_ref[...] = jnp.zeros_like(state_ref)

    lbl = lbl_ref[...]
    e = jnp.exp(lbl - jnp.max(lbl, axis=0, keepdims=True))
    lb = e[0:1] / jnp.sum(e, axis=0, keepdims=True)
    forget = lb + (1.0 - lb) * jax.nn.sigmoid(f_ref[...].astype(F32))
    log_f = jnp.log(forget)
    key = 1.0 - forget

    row = lax.broadcasted_iota(jnp.int32, (HG_TILE, HG_TILE), 0)
    col = lax.broadcasted_iota(jnp.int32, (HG_TILE, HG_TILE), 1)
    causal = (row >= col) & ((row // HG_CHUNK) == (col // HG_CHUNK))
    tri = jnp.where(causal, 1.0, 0.0).astype(BF16)
    b = sum(jnp.dot(tri, t, preferred_element_type=F32) for t in _split2(log_f))
    eb = jnp.exp(b)
    q_dec = (q_ref[...].astype(F32) * eb).astype(BF16)
    k_inv = key * jnp.exp(-b)
    xi = i_ref[...].astype(F32)
    val = (xi * jax.nn.sigmoid(xi)).astype(BF16)
    k_inv_b = k_inv.astype(BF16)

    n_chunks = HG_TILE // HG_CHUNK
    last_rows = [eb[(c + 1) * HG_CHUNK - 1:(c + 1) * HG_CHUNK, :] for c in range(n_chunks)]
    dec_rows = jnp.concatenate([jnp.broadcast_to(r, (HG_CHUNK, HG_WIDTH)) for r in last_rows], axis=0)
    k_end = (k_inv * dec_rows).astype(BF16)
    def per_chunk_columns(t):
        blocks = []
        for c in range(n_chunks):
            rows_above, rows_below = c * HG_CHUNK, HG_TILE - (c + 1) * HG_CHUNK
            parts = [t[rows_above:rows_above + HG_CHUNK]]
            if rows_above:
                parts.insert(0, jnp.zeros((rows_above, HG_DIM), t.dtype))
            if rows_below:
                parts.append(jnp.zeros((rows_below, HG_DIM), t.dtype))
            blocks.append(jnp.concatenate(parts, axis=0))
        return jnp.concatenate(blocks, axis=1)

    head_cols = [slice(h * HG_DIM, (h + 1) * HG_DIM) for h in range(HG_HEADS)]
    upds = [lax.dot_general(val[:, cols], per_chunk_columns(k_end[:, cols]), (((0,), (0,)), ((), ())),
                            preferred_element_type=F32) for cols in head_cols]
    o_intras = []
    for cols in head_cols:
        a = lax.dot_general(q_dec[:, cols], k_inv_b[:, cols], (((1,), (1,)), ((), ())), preferred_element_type=F32)
        a = jnp.where(causal, a, 0.0).astype(BF16)
        o_intras.append(jnp.dot(a, val[:, cols], preferred_element_type=F32))
    enterings = []
    for h, cols in enumerate(head_cols):
        st = state_ref[h]
        entering = []
        for c in range(n_chunks):
            entering.append(st.astype(BF16))
            st = st * last_rows[c][:, cols] + upds[h][:, c * HG_DIM:(c + 1) * HG_DIM]
        state_ref[h] = st
        enterings.append(jnp.concatenate(entering, axis=1))
    outs = []
    for h, cols in enumerate(head_cols):
        qd = q_dec[:, cols]
        o_inter = lax.dot_general(per_chunk_columns(qd), enterings[h],
                                  (((1,), (1,)), ((), ())), preferred_element_type=F32)
        o = o_intras[h] + o_inter
        o = o * lax.rsqrt(jnp.mean(jnp.square(o), axis=-1, keepdims=True) + RMS_EPS)
        outs.append(o)
    o = jnp.concatenate(outs, axis=1) * gain_ref[...]
    gg = g_ref[...].astype(F32)
    o_ref[...] = (o * (gg * jax.nn.sigmoid(gg))).astype(o_ref.dtype)


def _hgrn(u_hg, lb_logits, gain, B, S):
    tiles = S // HG_TILE
    col = lambda j: pl.BlockSpec((HG_TILE, HG_WIDTH), functools.partial(lambda b, t, j: (b * tiles + t, j), j=j))
    return pl.pallas_call(
        _hgrn_kernel,
        grid=(B, tiles),
        in_specs=[col(0), col(1), col(2), col(3),
                  pl.BlockSpec((2, HG_WIDTH), lambda b, t: (0, 0)),
                  pl.BlockSpec((1, HG_WIDTH), lambda b, t: (0, 0))],
        out_specs=pl.BlockSpec((HG_TILE, HG_WIDTH), lambda b, t: (b * tiles + t, 0)),
        out_shape=jax.ShapeDtypeStruct((B * S, HG_WIDTH), BF16),
        scratch_shapes=[pltpu.VMEM((HG_HEADS, HG_DIM, HG_DIM), F32)],
        compiler_params=_cparams("parallel", "arbitrary"),
        name="hgrn2",
    )(u_hg, u_hg, u_hg, u_hg, lb_logits, gain)


def _load_row_tiles(ref, n, start=0):
    return jnp.concatenate([ref[pl.ds(start + c, n, stride=LANE_CHUNKS), :] for c in range(LANE_CHUNKS)], axis=1)


def _store_row_tiles(ref, val, n):
    for c in range(LANE_CHUNKS):
        ref[pl.ds(c, n, stride=LANE_CHUNKS), :] = val[:, c * LANES:(c + 1) * LANES]


def _layer_norm(z, w, b):
    mu = jnp.mean(z, axis=-1, keepdims=True)
    zc = z - mu
    var = jnp.mean(jnp.square(zc), axis=-1, keepdims=True)
    return zc * lax.rsqrt(var + LN_EPS) * w + b


def _merge_kernel(ya_ref, yh_ref, ga_ref, gh_ref, x_ref, wa_ref, wh_ref, wo_ref, lw_ref, lb_ref, o_ref):
    ma = jnp.dot(ya_ref[...], wa_ref[...], preferred_element_type=F32)
    mh = jnp.dot(yh_ref[...], wh_ref[...], preferred_element_type=F32)
    merged = (jax.nn.sigmoid(ga_ref[...].astype(F32)) * ma + jax.nn.sigmoid(gh_ref[...].astype(F32)) * mh)
    z = DEEPNORM_ALPHA * x_ref[...] + jnp.dot(merged.astype(BF16), wo_ref[...], preferred_element_type=F32)
    _store_row_tiles(o_ref, _layer_norm(z, lw_ref[...], lb_ref[...]), MIX_ROWS)


def _merge(y_att, y_hg, u_hg, x2d, w_a, w_h, w_o, ln_w, ln_b):
    T = x2d.shape[0]
    rows = lambda width, j=0: pl.BlockSpec((MIX_ROWS, width), functools.partial(lambda i, j: (i, j), j=j))
    full = lambda a: pl.BlockSpec(a.shape, lambda i: (0, 0))
    return pl.pallas_call(
        _merge_kernel,
        grid=(T // MIX_ROWS,),
        in_specs=[rows(ATT_WIDTH), rows(HG_WIDTH), rows(D_MODEL, 4), rows(D_MODEL, 5), rows(D_MODEL),
                  full(w_a), full(w_h), full(w_o), full(ln_w), full(ln_b)],
        out_specs=pl.BlockSpec((MIX_ROWS * LANE_CHUNKS, LANES), lambda i: (i, 0)),
        out_shape=jax.ShapeDtypeStruct((T * LANE_CHUNKS, LANES), F32),
        compiler_params=_cparams("parallel"),
        name="merge_ln1",
    )(y_att, y_hg, u_hg, u_hg, x2d, w_a, w_h, w_o, ln_w, ln_b)


def _first_argmax(v, ids, n):
    mx = jnp.max(v, axis=0, keepdims=True)
    return mx, jnp.min(jnp.where(v == mx, ids, n), axis=0, keepdims=True)


def _route_kernel(x1_ref, p_ref, wrt_ref, rb_ref, wsg_ref, wsu_ref, wsd_ref, wpg_ref, wpp_ref,
                  base_ref, idx_ref, gate_ref, rank_ref, cnt_ref, carry_ref):
    @pl.when(pl.program_id(0) % (MOE_TILE // MIX_ROWS) == 0)
    def _():
        carry_ref[...] = jnp.zeros_like(carry_ref)

    x1 = _load_row_tiles(x1_ref, MIX_ROWS)
    x1b = x1.astype(BF16)
    x1_lo = (x1 - x1b.astype(F32)).astype(BF16)
    nt = (((1,), (1,)), ((), ()))
    logits = (lax.dot_general(wrt_ref[0], x1b, nt, preferred_element_type=F32)
              + lax.dot_general(wrt_ref[1], x1b, nt, preferred_element_type=F32)
              + lax.dot_general(wrt_ref[0], x1_lo, nt, preferred_element_type=F32))
    hg = jnp.dot(x1b, wsg_ref[...], preferred_element_type=F32)
    hu = jnp.dot(x1b, wsu_ref[...], preferred_element_type=F32)
    ple_gate = jnp.dot(x1b, wpg_ref[...], preferred_element_type=F32)
    ple_proj = jnp.dot(p_ref[...].astype(BF16), wpp_ref[...], preferred_element_type=F32)

    s = jax.nn.sigmoid(logits)
    sel = s + rb_ref[...]
    eid = lax.broadcasted_iota(jnp.int32, (N_EXPERTS, MIX_ROWS), 0)
    neg = -jnp.inf

    grp = sel.reshape(N_GROUPS, GROUP_SIZE, MIX_ROWS)
    mid = lax.broadcasted_iota(jnp.int32, grp.shape, 1)
    m1 = jnp.max(grp, axis=1, keepdims=True)
    i1 = jnp.min(jnp.where(grp == m1, mid, GROUP_SIZE), axis=1, keepdims=True)
    m2 = jnp.max(jnp.where(mid == i1, neg, grp), axis=1, keepdims=True)
    gscore = (m1 + m2).reshape(N_GROUPS, MIX_ROWS)
    gid = lax.broadcasted_iota(jnp.int32, (N_GROUPS, MIX_ROWS), 0)
    gsel = jnp.zeros((N_GROUPS, MIX_ROWS), jnp.bool_)
    for _ in range(TOPK_GROUPS):
        _, gi = _first_argmax(gscore, gid, N_GROUPS)
        hit = gid == gi
        gsel = gsel | hit
        gscore = jnp.where(hit, neg, gscore)
    emask = jnp.broadcast_to(gsel.reshape(N_GROUPS, 1, MIX_ROWS), grp.shape).reshape(N_EXPERTS, MIX_ROWS)
    cand = jnp.where(emask, sel, neg)

    idxs, gates = [], []
    chosen = jnp.zeros((N_EXPERTS, MIX_ROWS), jnp.bool_)
    for _ in range(TOP_K):
        _, ei = _first_argmax(cand, eid, N_EXPERTS)
        hit = eid == ei
        idxs.append(ei)
        gates.append(jnp.sum(jnp.where(hit, s, 0.0), axis=0, keepdims=True))
        chosen = chosen | hit
        cand = jnp.where(hit, neg, cand)
    g = jnp.concatenate(gates, axis=0)
    g = g / jnp.sum(g, axis=0, keepdims=True) * ROUTED_SCALE
    idx_ref[...] = jnp.concatenate(idxs, axis=0)
    gate_ref[...] = g

    onehot = jnp.where(chosen, 1.0, 0.0)
    tr = lax.broadcasted_iota(jnp.int32, (MIX_ROWS, MIX_ROWS), 0)
    tc = lax.broadcasted_iota(jnp.int32, (MIX_ROWS, MIX_ROWS), 1)
    before = jnp.where(tr < tc, 1.0, 0.0).astype(BF16)
    prefix = jnp.dot(onehot.astype(BF16), before, preferred_element_type=F32)
    rankfull = (carry_ref[:, 0:1] + prefix).astype(jnp.int32)
    rank_ref[...] = jnp.concatenate(
        [jnp.sum(jnp.where(eid == ei, rankfull, 0), axis=0, keepdims=True) for ei in idxs], axis=0)
    total = carry_ref[...] + jnp.sum(onehot, axis=1, keepdims=True)
    carry_ref[...] = total
    cnt_ref[...] = total.astype(jnp.int32)

    shared = jnp.dot((hg * jax.nn.sigmoid(hg) * hu).astype(BF16), wsd_ref[...], preferred_element_type=F32)
    ple = jax.nn.sigmoid(ple_gate) * ple_proj
    _store_row_tiles(base_ref, DEEPNORM_ALPHA * x1 + shared + ple, MIX_ROWS)


def _route(x1, p2d, wr_t, rbias, wsg, wsu, wsd, wpg, wpp):
    T = x1.shape[0] // LANE_CHUNKS
    per_tile = MOE_TILE // MIX_ROWS
    full = lambda a: pl.BlockSpec(a.shape, lambda i: (0,) * a.ndim)
    tok = pl.BlockSpec((TOP_K, MIX_ROWS), lambda i: (0, i))
    row_tiles = pl.BlockSpec((MIX_ROWS * LANE_CHUNKS, LANES), lambda i: (i, 0))
    return pl.pallas_call(
        _route_kernel,
        grid=(T // MIX_ROWS,),
        in_specs=[row_tiles,
                  pl.BlockSpec((MIX_ROWS, PLE_DIM), lambda i: (i, 0)),
                  full(wr_t), full(rbias), full(wsg), full(wsu), full(wsd), full(wpg), full(wpp)],
        out_specs=[row_tiles, tok, tok, tok,
                   pl.BlockSpec((None, N_EXPERTS, LANES), lambda i: (i // per_tile, 0, 0))],
        out_shape=[jax.ShapeDtypeStruct((T * LANE_CHUNKS, LANES), F32),
                   jax.ShapeDtypeStruct((TOP_K, T), jnp.int32),
                   jax.ShapeDtypeStruct((TOP_K, T), F32),
                   jax.ShapeDtypeStruct((TOP_K, T), jnp.int32),
                   jax.ShapeDtypeStruct((T // MOE_TILE, N_EXPERTS, LANES), jnp.int32)],
        scratch_shapes=[pltpu.VMEM((N_EXPERTS, LANES), F32)],
        compiler_params=_cparams("arbitrary"),
        name="route_shared_ple",
    )(x1, p2d, wr_t, rbias, wsg, wsu, wsd, wpg, wpp)


def _plan_sc_kernel(off_hbm, idx_hbm, rank_hbm, gate_hbm, rows_hbm, gates_hbm, seg_hbm,
                    off_v, idx_v, rank_v, gate_v, rows_v, gates_v, seg_v):
    n_tokens = idx_hbm.shape[0] // TOP_K
    worker = lax.axis_index("subcore") * SC_CORES + lax.axis_index("core")

    @pl.when(worker < n_tokens // MOE_TILE)
    def _():
        pltpu.sync_copy(off_hbm.at[pl.ds(worker * N_EXPERTS, N_EXPERTS)], off_v)
        lane = lax.iota(jnp.int32, SC_LANES)

        @pl.loop(0, N_EXPERTS * GATE_SEG // SC_LANES)
        def _(i):
            seg_v[pl.ds(i * SC_LANES, SC_LANES)] = jnp.zeros((SC_LANES,), F32)

        for k in range(TOP_K):
            row = pl.ds(k * n_tokens + worker * MOE_TILE, MOE_TILE)
            pltpu.sync_copy(idx_hbm.at[row], idx_v)
            pltpu.sync_copy(rank_hbm.at[row], rank_v)
            pltpu.sync_copy(gate_hbm.at[row], gate_v)

            @pl.loop(0, MOE_TILE // SC_LANES)
            def _(i):
                at = i * SC_LANES
                expert, rank, gate = (v[pl.ds(at, SC_LANES)] for v in (idx_v, rank_v, gate_v))
                pos = plsc.load_gather(off_v, [expert]) + rank
                plsc.store_scatter(rows_v, [pos], (lane + at) * LANE_CHUNKS)
                plsc.store_scatter(gates_v, [pos], gate)
                plsc.store_scatter(seg_v, [expert * GATE_SEG + rank], gate, mask=rank < GATE_SEG)

        @pl.loop(0, LIST_PAD // SC_LANES)
        def _(i):
            tail = pl.ds(MOE_TILE * TOP_K + i * SC_LANES, SC_LANES)
            rows_v[tail] = jnp.zeros((SC_LANES,), jnp.int32)
            gates_v[tail] = jnp.zeros((SC_LANES,), F32)

        out = pl.ds(worker * LIST_LEN, LIST_LEN)
        pltpu.sync_copy(rows_v, rows_hbm.at[out])
        pltpu.sync_copy(gates_v, gates_hbm.at[out])
        pltpu.sync_copy(seg_v, seg_hbm.at[pl.ds(worker * (N_EXPERTS * GATE_SEG), N_EXPERTS * GATE_SEG)])


def _plan_sc(off, idx, rank, gate):
    n_tiles = idx.shape[1] // MOE_TILE
    assert n_tiles <= SC_CORES * SC_SUBCORES
    mesh = plsc.VectorSubcoreMesh(core_axis_name="core", subcore_axis_name="subcore",
                                  num_cores=SC_CORES, num_subcores=SC_SUBCORES)
    rows, gates, seg = pl.kernel(
        _plan_sc_kernel,
        out_type=(jax.ShapeDtypeStruct((n_tiles * LIST_LEN,), jnp.int32),
                  jax.ShapeDtypeStruct((n_tiles * LIST_LEN,), F32),
                  jax.ShapeDtypeStruct((n_tiles * N_EXPERTS * GATE_SEG,), F32)),
        mesh=mesh,
        scratch_types=[pltpu.VMEM((N_EXPERTS,), jnp.int32), pltpu.VMEM((MOE_TILE,), jnp.int32),
                       pltpu.VMEM((MOE_TILE,), jnp.int32), pltpu.VMEM((MOE_TILE,), F32),
                       pltpu.VMEM((LIST_LEN,), jnp.int32), pltpu.VMEM((LIST_LEN,), F32),
                       pltpu.VMEM((N_EXPERTS * GATE_SEG,), F32)],
        compiler_params=pltpu.CompilerParams(needs_layout_passes=False),
        name="moe_plan_sc",
    )(off, idx.reshape(-1), rank.reshape(-1), gate.reshape(-1))
    return rows, gates, seg.reshape(n_tiles * N_EXPERTS, 1, GATE_SEG)


def _moe_tile_kernel(cnt_ref, off_ref, wg_ref, wu_ref, wd_ref, lw_ref, lb_ref, seg_ref, rows_hbm, gates_hbm, x_hbm,
                     base_hbm, o_hbm, x_s, acc_s, xg_a, xg_b, xg_c, y_a, y_b, y_c, stage_s, rows_s, gates_s, sem):
    tile, e = pl.program_id(0), pl.program_id(1)
    rows_of = lambda ref, r, n: ref.at[pl.ds(pl.multiple_of(r * LANE_CHUNKS, LANE_CHUNKS), n * LANE_CHUNKS), :]
    tile_rows = pl.ds(pl.multiple_of(tile * (MOE_TILE * LANE_CHUNKS), LANE_CHUNKS), MOE_TILE * LANE_CHUNKS)

    pair = tile * N_EXPERTS + e
    last_pair = pl.num_programs(0) * N_EXPERTS - 1
    n, off = cnt_ref[pair], off_ref[pair]
    tile_at = lambda ref, r: ref.at[pl.ds(pl.multiple_of(r, LANE_CHUNKS), LANE_CHUNKS), :]

    def gather_group(xg, first, jb):
        at = first + jb * GATHER_GROUP
        rows = [tile_at(x_s, rows_s[at + u])[...] for u in range(GATHER_GROUP)]
        rows_of(xg, jb * GATHER_GROUP, GATHER_GROUP)[...] = jnp.concatenate(rows, axis=0)

    def gather_loop(xg, first):
        def body(jb, cc):
            gather_group(xg, first, jb)
            return cc

        lax.fori_loop(0, MOE_CHUNK // GATHER_GROUP, body, 0)

    def swiglu(xg, y):
        xb = _load_row_tiles(xg, MOE_CHUNK).astype(BF16)
        hg = jnp.dot(xb, wg_ref[...], preferred_element_type=F32)
        hu = jnp.dot(xb, wu_ref[...], preferred_element_type=F32)
        act = (hg * jax.nn.sigmoid(hg) * hu).astype(BF16)
        _store_row_tiles(y, jnp.dot(act, wd_ref[...], preferred_element_type=F32), MOE_CHUNK)

    def scatter_group(y, first, j0, live, gated):
        dsts = [rows_s[first + j0 + u] for u in range(live)]
        yv = rows_of(y, j0, live)[...]
        rows = [yv[u * LANE_CHUNKS:(u + 1) * LANE_CHUNKS] for u in range(live)]
        if not gated:
            rows = [gates_s[first + j0 + u] * r for u, r in enumerate(rows)]
        vals = [tile_at(acc_s, d)[...] + r for d, r in zip(dsts, rows)]
        for d, val in reversed(list(zip(dsts, vals))):
            tile_at(acc_s, d)[...] = val

    def scatter_loop(y, first, m, gated):
        def body(jg, cc):
            scatter_group(y, first, jg * SCATTER_GROUP, SCATTER_GROUP, gated)
            return cc

        lax.fori_loop(0, m // SCATTER_GROUP, body, 0)
        for live in range(1, SCATTER_GROUP):
            @pl.when(m % SCATTER_GROUP == live)
            def _(live=live):
                scatter_group(y, first, m - live, live, gated)

    def plan_loads(t):
        plan = pl.ds(pl.multiple_of(t * LIST_LEN, LIST_PAD), LIST_LEN)
        return (pltpu.make_async_copy(rows_hbm.at[plan], rows_s, sem.at[0]),
                pltpu.make_async_copy(gates_hbm.at[plan], gates_s, sem.at[1]))

    def x_load(t):
        rows = pl.ds(pl.multiple_of(t * (MOE_TILE * LANE_CHUNKS), LANE_CHUNKS), MOE_TILE * LANE_CHUNKS)
        return pltpu.make_async_copy(x_hbm.at[rows, :], x_s, sem.at[2])

    @pl.when(e == 0)
    def _():
        @pl.when(tile == 0)
        def _():
            x_load(tile).start()
            for load in plan_loads(tile):
                load.start()

        load_base = pltpu.make_async_copy(base_hbm.at[tile_rows, :], acc_s, sem.at[3])
        load_base.start()
        y_b[...] = jnp.zeros_like(y_b)
        load_rows, load_gates = plan_loads(tile)
        load_rows.wait()
        x_load(tile).wait()
        gather_loop(xg_a, off)
        load_gates.wait()
        load_base.wait()

    prev_off = off_ref[jnp.maximum(pair - 1, 0)]
    next_off = off_ref[jnp.minimum(pair + 1, last_pair)]
    live_row = lax.broadcasted_iota(jnp.int32, (MOE_CHUNK, 2 * LANES), 0) < n

    def run_expert(xg_cur, y_cur, xg_nxt, y_prv):
        gathers = [functools.partial(gather_group, xg_nxt, next_off, jb) for jb in range(MOE_CHUNK // GATHER_GROUP)]
        scatters = [functools.partial(scatter_group, y_prv, prev_off, jg * SCATTER_GROUP, SCATTER_GROUP, True)
                    for jg in range(MOE_CHUNK // SCATTER_GROUP)]
        side = [s for both in zip(gathers, scatters) for s in both]
        n_down = LANE_CHUNKS // 2
        cost = [D_MODEL, D_MODEL] + [EXPERT_FF] * n_down
        bounds = [round(len(side) * sum(cost[:i]) / sum(cost)) for i in range(len(cost) + 1)]

        def side_work(i):
            for s in side[bounds[i]:bounds[i + 1]]:
                s()

        xb = _load_row_tiles(xg_cur, MOE_CHUNK).astype(BF16)
        side_work(0)
        hg = jnp.dot(xb, wg_ref[...], preferred_element_type=F32)
        side_work(1)
        hu = jnp.dot(xb, wu_ref[...], preferred_element_type=F32)
        gate_col = jnp.broadcast_to(seg_ref[...], (LANES, GATE_SEG)).T[:MOE_CHUNK]
        gate_col = jnp.concatenate([gate_col] * (EXPERT_FF // LANES), axis=1)
        act = (hg * jax.nn.sigmoid(hg) * hu * gate_col).astype(BF16)
        for q in range(n_down):
            side_work(2 + q)
            out = jnp.dot(act, wd_ref[:, q * 2 * LANES:(q + 1) * 2 * LANES], preferred_element_type=F32)
            out = jnp.where(live_row, out, 0.0)
            for c in range(2):
                y_cur[pl.ds(2 * q + c, MOE_CHUNK, stride=LANE_CHUNKS), :] = out[:, c * LANES:(c + 1) * LANES]

    @pl.when(e % 2 == 0)
    def _():
        run_expert(xg_a, y_a, xg_b, y_b)

    @pl.when(e % 2 == 1)
    def _():
        run_expert(xg_b, y_b, xg_a, y_a)

    def extra_chunk(c, carry):
        first = off + c * MOE_CHUNK
        gather_loop(xg_c, first)
        swiglu(xg_c, y_c)
        scatter_loop(y_c, first, jnp.minimum(MOE_CHUNK, n - c * MOE_CHUNK), False)
        return carry

    lax.fori_loop(1, (n + MOE_CHUNK - 1) // MOE_CHUNK, extra_chunk, 0)

    @pl.when(e == N_EXPERTS - 1)
    def _():
        more_tiles = tile + 1 < pl.num_programs(0)

        @pl.when(more_tiles)
        def _():
            x_load(tile + 1).start()

        scatter_loop(y_b, off, jnp.minimum(MOE_CHUNK, n), True)

        @pl.when(more_tiles)
        def _():
            for load in plan_loads(tile + 1):
                load.start()

        n_pieces = MOE_TILE // LN_ROWS
        store = lambda c: pltpu.make_async_copy(
            stage_s.at[c % 2], o_hbm.at[pl.ds(tile * MOE_TILE + c * LN_ROWS, LN_ROWS), :], sem.at[4 + c % 2])
        for c in range(n_pieces):
            z = _load_row_tiles(acc_s, LN_ROWS, c * LN_ROWS * LANE_CHUNKS)
            if c >= 2:
                store(c - 2).wait()
            stage_s[c % 2] = _layer_norm(z, lw_ref[...], lb_ref[...])
            store(c).start()
        store(n_pieces - 2).wait()
        store(n_pieces - 1).wait()


def _moe_tiles(x1, base, plan_rows, plan_gates, gate_seg, cnt, off, wg, wu, wd, ln_w, ln_b):
    T = x1.shape[0] // LANE_CHUNKS
    w_spec = lambda shape: pl.BlockSpec((None,) + shape, lambda i, e, cnt, off: (e, 0, 0))
    vec = pl.BlockSpec((1, D_MODEL), lambda i, e, cnt, off: (0, 0))
    hbm = pl.BlockSpec(memory_space=pl.ANY)
    tile_rows = MOE_TILE * LANE_CHUNKS
    return pl.pallas_call(
        _moe_tile_kernel,
        grid_spec=pltpu.PrefetchScalarGridSpec(
            num_scalar_prefetch=2,
            grid=(T // MOE_TILE, N_EXPERTS),
            in_specs=[w_spec((D_MODEL, EXPERT_FF)), w_spec((D_MODEL, EXPERT_FF)), w_spec((EXPERT_FF, D_MODEL)),
                      vec, vec,
                      pl.BlockSpec((None, 1, GATE_SEG), lambda i, e, cnt, off: (i * N_EXPERTS + e, 0, 0)),
                      hbm, hbm, hbm, hbm],
            out_specs=hbm,
            scratch_shapes=[pltpu.VMEM((tile_rows, LANES), F32),
                            pltpu.VMEM((tile_rows, LANES), F32)]
                           + [pltpu.VMEM((MOE_CHUNK * LANE_CHUNKS, LANES), F32)] * 6
                           + [pltpu.VMEM((2, LN_ROWS, D_MODEL), F32),
                              pltpu.SMEM((LIST_LEN,), jnp.int32),
                              pltpu.SMEM((LIST_LEN,), F32),
                              pltpu.SemaphoreType.DMA((6,))],
        ),
        out_shape=jax.ShapeDtypeStruct((T, D_MODEL), F32),
        compiler_params=_cparams("arbitrary", "arbitrary"),
        name="moe_tiles_ln2",
    )(cnt, off, wg, wu, wd, ln_w, ln_b, gate_seg, plan_rows, plan_gates, x1, base)


def kernel(x, p, w_in, hgrn_lb_logits, hgrn_norm_w, w_branch_att, w_branch_hgrn, w_out, ln1_w, ln1_b, router_w, router_bias, expert_w_gate, expert_w_up, expert_w_down, shared_w_gate, shared_w_up, shared_w_down, ple_gate_w, ple_proj_w, ln2_w, ln2_b):
    B, S, D = x.shape
    T = B * S
    l = 0
    x2d = x.reshape(T, D)
    bf = lambda a: a.astype(BF16)

    ws = _att_weights(w_in[l])
    qkv = [_proj_att(x2d, ws[g], d) for g, d in enumerate(ATT_DILATIONS)]
    y_att = _attention(qkv, B, S)
    u_hg, wg_b, wu_b, wd_b = _proj(x2d, bf(w_in[l][:, 3 * len(ATT_DILATIONS) * ATT_WIDTH:]), PROJ_COLS,
                                   (expert_w_gate[l], expert_w_up[l], expert_w_down[l]))
    y_hg = _hgrn(u_hg, hgrn_lb_logits, hgrn_norm_w[l:l + 1], B, S)
    x1 = _merge(y_att, y_hg, u_hg, x2d, bf(w_branch_att[l]), bf(w_branch_hgrn[l]), bf(w_out[l]),
                ln1_w[l:l + 1], ln1_b[l:l + 1])

    base, idx, gate, rank, counts = _route(
        x1, p[l].reshape(T, PLE_DIM), jnp.stack(_split2(router_w[l].T)), router_bias[l].reshape(N_EXPERTS, 1),
        bf(shared_w_gate[l]), bf(shared_w_up[l]), bf(shared_w_down[l]), bf(ple_gate_w[l]), bf(ple_proj_w[l]))
    cnt = counts[:, :, 0]
    off = jnp.cumsum(cnt, axis=1) - cnt
    cnt, off = cnt.reshape(-1), off.reshape(-1)
    plan_rows, plan_gates, gate_seg = _plan_sc(off, idx, rank, gate)
    out = _moe_tiles(x1, base, plan_rows, plan_gates, gate_seg, cnt, off, wg_b, wu_b, wd_b,
                     ln2_w[l:l + 1], ln2_b[l:l + 1])
    return out.reshape(B, S, D)
```

```python
import functools

import jax
import jax.numpy as jnp
import numpy as np
from jax import lax
from jax.experimental import pallas as pl
from jax.experimental.pallas import tpu as pltpu
from jax.experimental.pallas import tpu_sc as plsc

F32 = jnp.float32
BF16 = jnp.bfloat16

D_MODEL = 1024
ATT_HEAD_DIM = 64
ATT_HEADS = 8
ATT_DILATIONS = (1, 4, 16)
ATT_BLOCK = 128
ATT_WIDTH = ATT_HEADS * ATT_HEAD_DIM
ATT_TILE = ATT_BLOCK * max(ATT_DILATIONS)
NEG_INF = -1e30
LOG2_E = 1.4426950408889634

HG_HEADS = 8
HG_DIM = 128
HG_WIDTH = HG_HEADS * HG_DIM
HG_CHUNK = 32
HG_TILE = 256
RMS_EPS = 1e-6

N_EXPERTS = 64
TOP_K = 8
N_GROUPS = 8
GROUP_SIZE = N_EXPERTS // N_GROUPS
TOPK_GROUPS = 4
EXPERT_FF = 256
ROUTED_SCALE = 2.5
PLE_DIM = 256
LN_EPS = 1e-5
DEPTH = 1
DEEPNORM_ALPHA = (2.0 * DEPTH) ** 0.25

LANES = 128
LANE_CHUNKS = D_MODEL // LANES
PROJ_ROWS = 512
PROJ_COLS = 1536
ATT_PROJ_ROWS = 1024
STRIDE_STEP = 4
MIX_ROWS = 512
MOE_TILE = 4096
MOE_CHUNK = 576
GATE_SEG = 640
LN_ROWS = 256
LIST_PAD = 1024
LIST_LEN = MOE_TILE * TOP_K + LIST_PAD
GATHER_GROUP = 8
SCATTER_GROUP = 8
V7X_VMEM_LIMIT = 56 * 1024 * 1024
SC_CORES, SC_SUBCORES, SC_LANES = 2, 16, 16


def _cparams(*sem):
    return pltpu.CompilerParams(dimension_semantics=sem, vmem_limit_bytes=V7X_VMEM_LIMIT)


def _proj_att_kernel(*refs, dil, n_conv):
    x_refs, (wq_ref, wk_ref, wv_ref) = refs[:LANE_CHUNKS], refs[LANE_CHUNKS:LANE_CHUNKS + 3]
    conv_in = refs[LANE_CHUNKS + 3:LANE_CHUNKS + 3 + n_conv]
    o_ref = refs[LANE_CHUNKS + 3 + n_conv]
    conv_out = refs[LANE_CHUNKS + 4 + n_conv:LANE_CHUNKS + 4 + 2 * n_conv]
    w_ref = refs[LANE_CHUNKS + 4 + 2 * n_conv]
    stage_refs = refs[LANE_CHUNKS + 5 + 2 * n_conv:]
    n = ATT_PROJ_ROWS // dil
    for src, dst in zip(conv_in, conv_out):
        dst[...] = src[...].astype(BF16)

    @pl.when(pl.program_id(0) == 0)
    def _():
        w_ref[:, :ATT_WIDTH] = (wq_ref[...] * (ATT_HEAD_DIM ** -0.5 * LOG2_E)).astype(BF16)
        w_ref[:, ATT_WIDTH:2 * ATT_WIDTH] = wk_ref[...].astype(BF16)
        w_ref[:, 2 * ATT_WIDTH:] = wv_ref[...].astype(BF16)

    def rows(ref, stage):
        if dil == 1:
            return ref[...]
        if dil <= STRIDE_STEP:
            return jnp.concatenate([ref[pl.ds(r, n, stride=dil), :] for r in range(dil)], axis=0)
        assert dil == STRIDE_STEP * STRIDE_STEP
        quarter = ATT_PROJ_ROWS // STRIDE_STEP
        for r1 in range(STRIDE_STEP):
            stage[pl.ds(r1 * quarter, quarter), :] = ref[pl.ds(r1, quarter, stride=STRIDE_STEP), :]
        return jnp.concatenate(
            [stage[pl.ds((r % STRIDE_STEP) * quarter + r // STRIDE_STEP, n, stride=STRIDE_STEP), :]
             for r in range(dil)], axis=0)

    xp = jnp.concatenate([rows(ref, stage_refs[c] if stage_refs else None).astype(BF16)
                          for c, ref in enumerate(x_refs)], axis=1)
    y = jnp.dot(xp, w_ref[...], preferred_element_type=F32)
    o_ref[...] = y.astype(BF16).reshape(dil, n, 3 * ATT_WIDTH)


def _proj_att(x2d, w_in_l, g, dil, conv=None):
    T = x2d.shape[0]
    per = ATT_TILE // ATT_PROJ_ROWS
    n = ATT_PROJ_ROWS // dil
    steps = T // ATT_PROJ_ROWS
    n_groups = len(ATT_DILATIONS)
    conv_args, conv_in, conv_out, conv_shapes = [], [], [], []
    if conv is not None:
        mat, first, blocks = conv
        slab = mat.shape[0] // steps
        assert slab * steps == mat.shape[0] and slab % 16 == 0
        conv_args = [mat] * blocks
        conv_in = [pl.BlockSpec((slab, PROJ_COLS), functools.partial(lambda i, c: (i, c), c=first + c))
                   for c in range(blocks)]
        conv_out = [pl.BlockSpec((slab, PROJ_COLS), lambda i: (i, 0))] * blocks
        conv_shapes = [jax.ShapeDtypeStruct((mat.shape[0], PROJ_COLS), BF16)] * blocks
    out, *converted = pl.pallas_call(
        functools.partial(_proj_att_kernel, dil=dil, n_conv=len(conv_args)),
        grid=(steps,),
        in_specs=[pl.BlockSpec((ATT_PROJ_ROWS, LANES), functools.partial(lambda i, c: (i, c), c=c))
                  for c in range(LANE_CHUNKS)]
                 + [pl.BlockSpec((D_MODEL, ATT_WIDTH), functools.partial(lambda i, c: (0, c), c=part * n_groups + g))
                    for part in range(3)] + conv_in,
        out_specs=[pl.BlockSpec((None, dil, None, n, 3 * ATT_WIDTH), lambda i: (i // per, 0, i % per, 0, 0))]
                  + conv_out,
        out_shape=[jax.ShapeDtypeStruct((T // ATT_TILE, dil, per, n, 3 * ATT_WIDTH), BF16)] + conv_shapes,
        scratch_shapes=[pltpu.VMEM((D_MODEL, 3 * ATT_WIDTH), BF16)]
                       + ([pltpu.VMEM((ATT_PROJ_ROWS, LANES), F32)] * LANE_CHUNKS if dil > STRIDE_STEP else []),
        compiler_params=_cparams("arbitrary"),
        name=f"proj_att_d{dil}",
    )(*([x2d] * LANE_CHUNKS), *([w_in_l] * 3), *conv_args)
    out = out.reshape(T // ATT_TILE, dil, ATT_TILE // dil, 3 * ATT_WIDTH)
    return (out, converted) if conv is not None else out


def _att_pair(q2, kp, kc, vp, vc, bias_ref, g, first):
    def head0_lanes(rows, dtype):
        lane = lax.broadcasted_iota(jnp.int32, (rows, 2 * ATT_HEAD_DIM), 1)
        return lane.astype(F32).astype(dtype) < ATT_HEAD_DIM

    lo_q = head0_lanes(ATT_BLOCK, BF16)
    lo_v = head0_lanes(2 * ATT_BLOCK, BF16)
    k2 = jnp.concatenate([kp, kc], axis=0)
    v2 = jnp.concatenate([vp, vc], axis=0)
    zero = jnp.zeros_like(q2)
    ps, ms = [], []
    for hh in range(2):
        qm = jnp.where(lo_q, q2, zero) if hh == 0 else jnp.where(lo_q, zero, q2)
        s = lax.dot_general(qm, k2, (((1,), (1,)), ((), ())), preferred_element_type=F32)
        s = s + bias_ref[g, hh, first]
        m = jnp.max(s, axis=-1, keepdims=True)
        ps.append(jnp.exp2(s - m).astype(BF16))
        ms.append(m)
    pcat = jnp.concatenate(ps, axis=1)
    zero_v, one_v = jnp.zeros_like(v2), jnp.ones_like(v2)
    rhs = jnp.concatenate([
        jnp.concatenate([jnp.where(lo_v, v2, zero_v), jnp.where(lo_v, one_v, zero_v)], axis=1),
        jnp.concatenate([jnp.where(lo_v, zero_v, v2), jnp.where(lo_v, zero_v, one_v)], axis=1)], axis=0)
    nd = jnp.dot(pcat, rhs, preferred_element_type=F32)
    m2 = jnp.where(head0_lanes(ATT_BLOCK, F32), ms[0], ms[1])
    return nd[:, :2 * ATT_HEAD_DIM], m2, nd[:, 2 * ATT_HEAD_DIM:]


def _att_kernel(*refs):
    (q0, kc0, vc0, kp0, vp0, q1, kc1, vc1, kp1, vp1, q2, kc2, vc2, kp2, vp2,
     bias_ref, o_ref) = refs[:17]
    ng = len(ATT_DILATIONS)
    num_s, m_s, den_s = refs[17:17 + ng], refs[17 + ng:17 + 2 * ng], refs[17 + 2 * ng:]
    first_tile = (pl.program_id(2) == 0).astype(jnp.int32)
    groups = ((q0, kc0, vc0, kp0, vp0), (q1, kc1, vc1, kp1, vp1), (q2, kc2, vc2, kp2, vp2))
    for g, dil in enumerate(ATT_DILATIONS):
        q_ref, kc_ref, vc_ref, kp_ref, vp_ref = groups[g]
        nb = ATT_TILE // dil // ATT_BLOCK
        for r in range(dil):
            for n in range(nb):
                rows = pl.ds(n * ATT_BLOCK, ATT_BLOCK)
                if n == 0:
                    prev = pl.ds((nb - 1) * ATT_BLOCK, ATT_BLOCK)
                    kp, vp, first = kp_ref[r, prev, :], vp_ref[r, prev, :], first_tile
                else:
                    prev = pl.ds((n - 1) * ATT_BLOCK, ATT_BLOCK)
                    kp, vp, first = kc_ref[r, prev, :], vc_ref[r, prev, :], 0
                num, m, den = _att_pair(q_ref[r, rows, :], kp, kc_ref[r, rows, :], vp, vc_ref[r, rows, :],
                                        bias_ref, g, first)
                if dil == 1:
                    dst = rows
                else:
                    dst = pl.ds(n * ATT_BLOCK * dil + r, ATT_BLOCK, stride=dil)
                num_s[g][dst, :] = num
                m_s[g][dst, :] = m
                den_s[g][dst, :] = den
    m_all = jnp.maximum(jnp.maximum(m_s[0][...], m_s[1][...]), m_s[2][...])
    num = jnp.zeros((ATT_TILE, 2 * ATT_HEAD_DIM), F32)
    den = jnp.zeros((ATT_TILE, 2 * ATT_HEAD_DIM), F32)
    for g in range(ng):
        sc = jnp.exp2(m_s[g][...] - m_all)
        num = num + sc * num_s[g][...]
        den = den + sc * den_s[g][...]
    o_ref[...] = (num / den).astype(o_ref.dtype)


def _att_bias_table():
    qi = np.arange(ATT_BLOCK)[:, None]
    ki = np.arange(2 * ATT_BLOCK)[None, :]
    steps = qi + ATT_BLOCK - ki
    valid = (steps >= 0) & (steps <= ATT_BLOCK)
    slopes = np.array([2.0 ** (-8.0 * (h + 1) / ATT_HEADS) for h in range(ATT_HEADS)], np.float32)
    tab = np.empty((len(ATT_DILATIONS), ATT_HEADS, 2, ATT_BLOCK, 2 * ATT_BLOCK), np.float32)
    for g, dil in enumerate(ATT_DILATIONS):
        bias = -slopes[:, None, None] * (steps * dil).astype(np.float32)[None] * LOG2_E
        tab[g, :, 0] = np.where(valid[None], bias, NEG_INF)
        tab[g, :, 1] = np.where((valid & (ki >= ATT_BLOCK))[None], bias, NEG_INF)
    return jnp.asarray(tab)


def _attention(qkv, B, S):
    tiles = S // ATT_TILE
    pair = 2 * ATT_HEAD_DIM
    npair = ATT_WIDTH // pair
    in_specs, args = [], []
    for g, dil in enumerate(ATT_DILATIONS):
        blk = (None, dil, ATT_TILE // dil, pair)
        cur = lambda b, hp, t, off: (b * tiles + t, 0, 0, off * npair + hp)
        prv = lambda b, hp, t, off: (b * tiles + jnp.maximum(t - 1, 0), 0, 0, off * npair + hp)
        in_specs += [pl.BlockSpec(blk, functools.partial(cur, off=0)),
                     pl.BlockSpec(blk, functools.partial(cur, off=1)),
                     pl.BlockSpec(blk, functools.partial(cur, off=2)),
                     pl.BlockSpec(blk, functools.partial(prv, off=1)),
                     pl.BlockSpec(blk, functools.partial(prv, off=2))]
        args += [qkv[g]] * 5
    in_specs.append(pl.BlockSpec((len(ATT_DILATIONS), 2, 2, ATT_BLOCK, 2 * ATT_BLOCK),
                                 lambda b, hp, t: (0, hp, 0, 0, 0)))
    args.append(_att_bias_table())
    scratch = [pltpu.VMEM((ATT_TILE, pair), F32) for _ in range(3 * len(ATT_DILATIONS))]
    return pl.pallas_call(
        _att_kernel,
        grid=(B, npair, tiles),
        in_specs=in_specs,
        out_specs=pl.BlockSpec((ATT_TILE, pair), lambda b, hp, t: (b * tiles + t, hp)),
        out_shape=jax.ShapeDtypeStruct((B * S, ATT_WIDTH), BF16),
        scratch_shapes=scratch,
        compiler_params=_cparams("parallel", "parallel", "arbitrary"),
        name="dilated_attention",
    )(*args)


def _proj_kernel(x_ref, *refs, n_w):
    w_refs = refs[:n_w]
    eg_ref, eu_ref, ed_ref, o_ref, egb_ref, eub_ref, edb_ref = refs[n_w:]
    egb_ref[...] = eg_ref[...].astype(BF16)
    eub_ref[...] = eu_ref[...].astype(BF16)
    edb_ref[...] = ed_ref[...].astype(BF16)
    xb = x_ref[...].astype(BF16)
    col_tile = w_refs[0].shape[1]
    for c, w_ref in enumerate(w_refs):
        cols = slice(c * col_tile, (c + 1) * col_tile)
        o_ref[:, cols] = jnp.dot(xb, w_ref[...], preferred_element_type=F32).astype(o_ref.dtype)


def _proj(x2d, ws, expert_weights):
    T, N = x2d.shape[0], len(ws) * ws[0].shape[1]
    steps = T // PROJ_ROWS
    n_experts = expert_weights[0].shape[0]
    per_step = max(1, n_experts // steps)
    assert n_experts % per_step == 0 and n_experts // per_step <= steps
    per_expert = lambda ew: pl.BlockSpec(
        (per_step,) + ew.shape[1:], lambda i: (jnp.minimum(i, n_experts // per_step - 1), 0, 0))
    return pl.pallas_call(
        functools.partial(_proj_kernel, n_w=len(ws)),
        grid=(steps,),
        in_specs=[pl.BlockSpec((PROJ_ROWS, D_MODEL), lambda i: (i, 0))]
                 + [pl.BlockSpec(w.shape, lambda i: (0, 0)) for w in ws] + [per_expert(ew) for ew in expert_weights],
        out_specs=[pl.BlockSpec((PROJ_ROWS, N), lambda i: (i, 0))] + [per_expert(ew) for ew in expert_weights],
        out_shape=[jax.ShapeDtypeStruct((T, N), BF16)]
                  + [jax.ShapeDtypeStruct(ew.shape, BF16) for ew in expert_weights],
        compiler_params=_cparams("parallel"),
        name="proj_hgrn_gates",
    )(x2d, *ws, *expert_weights)


def _split2(v):
    a = v.astype(BF16)
    return a, (v - a.astype(F32)).astype(BF16)


def _hgrn_kernel(q_ref, f_ref, i_ref, g_ref, lbl_ref, gain_ref, o_ref, state_ref):
    @pl.when(pl.program_id(1) == 0)
    def _():
        state_ref[...] = jnp.zeros_like(state_ref)

    lbl = lbl_ref[...]
    e = jnp.exp(lbl - jnp.max(lbl, axis=0, keepdims=True))
    lb = e[0:1] / jnp.sum(e, axis=0, keepdims=True)
    forget = lb + (1.0 - lb) * jax.nn.sigmoid(f_ref[...].astype(F32))
    log_f = jnp.log(forget)
    key = 1.0 - forget

    row = lax.broadcasted_iota(jnp.int32, (HG_TILE, HG_TILE), 0)
    col = lax.broadcasted_iota(jnp.int32, (HG_TILE, HG_TILE), 1)
    causal = (row >= col) & ((row // HG_CHUNK) == (col // HG_CHUNK))
    tri = jnp.where(causal, 1.0, 0.0).astype(BF16)
    b = sum(jnp.dot(tri, t, preferred_element_type=F32) for t in _split2(log_f))
    eb = jnp.exp(b)
    q_dec = (q_ref[...].astype(F32) * eb).astype(BF16)
    k_inv = key * jnp.exp(-b)
    xi = i_ref[...].astype(F32)
    val = (xi * jax.nn.sigmoid(xi)).astype(BF16)
    k_inv_b = k_inv.astype(BF16)

    n_chunks = HG_TILE // HG_CHUNK
    last_rows = [eb[(c + 1) * HG_CHUNK - 1:(c + 1) * HG_CHUNK, :] for c in range(n_chunks)]
    dec_rows = jnp.concatenate([jnp.broadcast_to(r, (HG_CHUNK, HG_WIDTH)) for r in last_rows], axis=0)
    k_end = (k_inv * dec_rows).astype(BF16)
    def per_chunk_columns(t):
        blocks = []
        for c in range(n_chunks):
            rows_above, rows_below = c * HG_CHUNK, HG_TILE - (c + 1) * HG_CHUNK
            parts = [t[rows_above:rows_above + HG_CHUNK]]
            if rows_above:
                parts.insert(0, jnp.zeros((rows_above, HG_DIM), t.dtype))
            if rows_below:
                parts.append(jnp.zeros((rows_below, HG_DIM), t.dtype))
            blocks.append(jnp.concatenate(parts, axis=0))
        return jnp.concatenate(blocks, axis=1)

    head_cols = [slice(h * HG_DIM, (h + 1) * HG_DIM) for h in range(HG_HEADS)]
    upds = [lax.dot_general(val[:, cols], per_chunk_columns(k_end[:, cols]), (((0,), (0,)), ((), ())),
                            preferred_element_type=F32) for cols in head_cols]
    o_intras = []
    for cols in head_cols:
        a = lax.dot_general(q_dec[:, cols], k_inv_b[:, cols], (((1,), (1,)), ((), ())), preferred_element_type=F32)
        a = jnp.where(causal, a, 0.0).astype(BF16)
        o_intras.append(jnp.dot(a, val[:, cols], preferred_element_type=F32))
    enterings = []
    for h, cols in enumerate(head_cols):
        st = state_ref[h]
        entering = []
        for c in range(n_chunks):
            entering.append(st.astype(BF16))
            st = st * last_rows[c][:, cols] + upds[h][:, c * HG_DIM:(c + 1) * HG_DIM]
        state_ref[h] = st
        enterings.append(jnp.concatenate(entering, axis=1))
    outs = []
    for h, cols in enumerate(head_cols):
        qd = q_dec[:, cols]
        o_inter = lax.dot_general(per_chunk_columns(qd), enterings[h],
                                  (((1,), (1,)), ((), ())), preferred_element_type=F32)
        o = o_intras[h] + o_inter
        o = o * lax.rsqrt(jnp.mean(jnp.square(o), axis=-1, keepdims=True) + RMS_EPS)
        outs.append(o)
    o = jnp.concatenate(outs, axis=1) * gain_ref[...]
    gg = g_ref[...].astype(F32)
    o_ref[...] = (o * (gg * jax.nn.sigmoid(gg))).astype(o_ref.dtype)


def _hgrn(u_hg, lb_logits, gain, B, S):
    tiles = S // HG_TILE
    col = lambda j: pl.BlockSpec((HG_TILE, HG_WIDTH), functools.partial(lambda b, t, j: (b * tiles + t, j), j=j))
    return pl.pallas_call(
        _hgrn_kernel,
        grid=(B, tiles),
        in_specs=[col(0), col(1), col(2), col(3),
                  pl.BlockSpec((2, HG_WIDTH), lambda b, t: (0, 0)),
                  pl.BlockSpec((1, HG_WIDTH), lambda b, t: (0, 0))],
        out_specs=pl.BlockSpec((HG_TILE, HG_WIDTH), lambda b, t: (b * tiles + t, 0)),
        out_shape=jax.ShapeDtypeStruct((B * S, HG_WIDTH), BF16),
        scratch_shapes=[pltpu.VMEM((HG_HEADS, HG_DIM, HG_DIM), F32)],
        compiler_params=_cparams("parallel", "arbitrary"),
        name="hgrn2",
    )(u_hg, u_hg, u_hg, u_hg, lb_logits, gain)


def _load_row_tiles(ref, n, start=0):
    return jnp.concatenate([ref[pl.ds(start + c, n, stride=LANE_CHUNKS), :] for c in range(LANE_CHUNKS)], axis=1)


def _store_row_tiles(ref, val, n):
    for c in range(LANE_CHUNKS):
        ref[pl.ds(c, n, stride=LANE_CHUNKS), :] = val[:, c * LANES:(c + 1) * LANES]


def _layer_norm(z, w, b):
    mu = jnp.mean(z, axis=-1, keepdims=True)
    zc = z - mu
    var = jnp.mean(jnp.square(zc), axis=-1, keepdims=True)
    return zc * lax.rsqrt(var + LN_EPS) * w + b


def _merge_kernel(ya_ref, yh_ref, ga_ref, gh_ref, x_ref, wa_ref, wh_ref, wo_ref, lw_ref, lb_ref, o_ref):
    ma = jnp.dot(ya_ref[...], wa_ref[...], preferred_element_type=F32)
    mh = jnp.dot(yh_ref[...], wh_ref[...], preferred_element_type=F32)
    merged = (jax.nn.sigmoid(ga_ref[...].astype(F32)) * ma + jax.nn.sigmoid(gh_ref[...].astype(F32)) * mh)
    z = DEEPNORM_ALPHA * x_ref[...] + jnp.dot(merged.astype(BF16), wo_ref[...], preferred_element_type=F32)
    _store_row_tiles(o_ref, _layer_norm(z, lw_ref[...], lb_ref[...]), MIX_ROWS)


def _merge(y_att, y_hg, u_hg, x2d, w_a, w_h, w_o, ln_w, ln_b):
    T = x2d.shape[0]
    rows = lambda width, j=0: pl.BlockSpec((MIX_ROWS, width), functools.partial(lambda i, j: (i, j), j=j))
    full = lambda a: pl.BlockSpec(a.shape, lambda i: (0, 0))
    return pl.pallas_call(
        _merge_kernel,
        grid=(T // MIX_ROWS,),
        in_specs=[rows(ATT_WIDTH), rows(HG_WIDTH), rows(D_MODEL, 4), rows(D_MODEL, 5), rows(D_MODEL),
                  full(w_a), full(w_h), full(w_o), full(ln_w), full(ln_b)],
        out_specs=pl.BlockSpec((MIX_ROWS * LANE_CHUNKS, LANES), lambda i: (i, 0)),
        out_shape=jax.ShapeDtypeStruct((T * LANE_CHUNKS, LANES), F32),
        compiler_params=_cparams("parallel"),
        name="merge_ln1",
    )(y_att, y_hg, u_hg, u_hg, x2d, w_a, w_h, w_o, ln_w, ln_b)


def _first_argmax(v, ids, n):
    mx = jnp.max(v, axis=0, keepdims=True)
    return mx, jnp.min(jnp.where(v == mx, ids, n), axis=0, keepdims=True)


def _route_kernel(x1_ref, p_ref, wrt_ref, rb_ref, wsg_ref, wsu_ref, wsd_ref, wpg_ref, wpp_ref,
                  base_ref, idx_ref, gate_ref, rank_ref, cnt_ref, carry_ref):
    @pl.when(pl.program_id(0) % (MOE_TILE // MIX_ROWS) == 0)
    def _():
        carry_ref[...] = jnp.zeros_like(carry_ref)

    x1 = _load_row_tiles(x1_ref, MIX_ROWS)
    x1b = x1.astype(BF16)
    x1_lo = (x1 - x1b.astype(F32)).astype(BF16)
    nt = (((1,), (1,)), ((), ()))
    logits = (lax.dot_general(wrt_ref[0], x1b, nt, preferred_element_type=F32)
              + lax.dot_general(wrt_ref[1], x1b, nt, preferred_element_type=F32)
              + lax.dot_general(wrt_ref[0], x1_lo, nt, preferred_element_type=F32))
    hg = jnp.dot(x1b, wsg_ref[...], preferred_element_type=F32)
    hu = jnp.dot(x1b, wsu_ref[...], preferred_element_type=F32)
    ple_gate = jnp.dot(x1b, wpg_ref[...], preferred_element_type=F32)
    ple_proj = jnp.dot(p_ref[...].astype(BF16), wpp_ref[...], preferred_element_type=F32)

    s = jax.nn.sigmoid(logits)
    sel = s + rb_ref[...]
    eid = lax.broadcasted_iota(jnp.int32, (N_EXPERTS, MIX_ROWS), 0)
    neg = -jnp.inf

    grp = sel.reshape(N_GROUPS, GROUP_SIZE, MIX_ROWS)
    mid = lax.broadcasted_iota(jnp.int32, grp.shape, 1)
    m1 = jnp.max(grp, axis=1, keepdims=True)
    i1 = jnp.min(jnp.where(grp == m1, mid, GROUP_SIZE), axis=1, keepdims=True)
    m2 = jnp.max(jnp.where(mid == i1, neg, grp), axis=1, keepdims=True)
    gscore = (m1 + m2).reshape(N_GROUPS, MIX_ROWS)
    gid = lax.broadcasted_iota(jnp.int32, (N_GROUPS, MIX_ROWS), 0)
    gsel = jnp.zeros((N_GROUPS, MIX_ROWS), jnp.bool_)
    for _ in range(TOPK_GROUPS):
        _, gi = _first_argmax(gscore, gid, N_GROUPS)
        hit = gid == gi
        gsel = gsel | hit
        gscore = jnp.where(hit, neg, gscore)
    emask = jnp.broadcast_to(gsel.reshape(N_GROUPS, 1, MIX_ROWS), grp.shape).reshape(N_EXPERTS, MIX_ROWS)
    cand = jnp.where(emask, sel, neg)

    idxs, gates = [], []
    chosen = jnp.zeros((N_EXPERTS, MIX_ROWS), jnp.bool_)
    for _ in range(TOP_K):
        _, ei = _first_argmax(cand, eid, N_EXPERTS)
        hit = eid == ei
        idxs.append(ei)
        gates.append(jnp.sum(jnp.where(hit, s, 0.0), axis=0, keepdims=True))
        chosen = chosen | hit
        cand = jnp.where(hit, neg, cand)
    g = jnp.concatenate(gates, axis=0)
    g = g / jnp.sum(g, axis=0, keepdims=True) * ROUTED_SCALE
    idx_ref[...] = jnp.concatenate(idxs, axis=0)
    gate_ref[...] = g

    onehot = jnp.where(chosen, 1.0, 0.0)
    tr = lax.broadcasted_iota(jnp.int32, (MIX_ROWS, MIX_ROWS), 0)
    tc = lax.broadcasted_iota(jnp.int32, (MIX_ROWS, MIX_ROWS), 1)
    before = jnp.where(tr < tc, 1.0, 0.0).astype(BF16)
    prefix = jnp.dot(onehot.astype(BF16), before, preferred_element_type=F32)
    rankfull = (carry_ref[:, 0:1] + prefix).astype(jnp.int32)
    rank_ref[...] = jnp.concatenate(
        [jnp.sum(jnp.where(eid == ei, rankfull, 0), axis=0, keepdims=True) for ei in idxs], axis=0)
    total = carry_ref[...] + jnp.sum(onehot, axis=1, keepdims=True)
    carry_ref[...] = total
    cnt_ref[...] = total.astype(jnp.int32)

    shared = jnp.dot((hg * jax.nn.sigmoid(hg) * hu).astype(BF16), wsd_ref[...], preferred_element_type=F32)
    ple = jax.nn.sigmoid(ple_gate) * ple_proj
    _store_row_tiles(base_ref, DEEPNORM_ALPHA * x1 + shared + ple, MIX_ROWS)


def _route(x1, p2d, wr_t, rbias, wsg, wsu, wsd, wpg, wpp):
    T = x1.shape[0] // LANE_CHUNKS
    per_tile = MOE_TILE // MIX_ROWS
    full = lambda a: pl.BlockSpec(a.shape, lambda i: (0,) * a.ndim)
    tok = pl.BlockSpec((TOP_K, MIX_ROWS), lambda i: (0, i))
    row_tiles = pl.BlockSpec((MIX_ROWS * LANE_CHUNKS, LANES), lambda i: (i, 0))
    return pl.pallas_call(
        _route_kernel,
        grid=(T // MIX_ROWS,),
        in_specs=[row_tiles,
                  pl.BlockSpec((MIX_ROWS, PLE_DIM), lambda i: (i, 0)),
                  full(wr_t), full(rbias), full(wsg), full(wsu), full(wsd), full(wpg), full(wpp)],
        out_specs=[row_tiles, tok, tok, tok,
                   pl.BlockSpec((None, N_EXPERTS, LANES), lambda i: (i // per_tile, 0, 0))],
        out_shape=[jax.ShapeDtypeStruct((T * LANE_CHUNKS, LANES), F32),
                   jax.ShapeDtypeStruct((TOP_K, T), jnp.int32),
                   jax.ShapeDtypeStruct((TOP_K, T), F32),
                   jax.ShapeDtypeStruct((TOP_K, T), jnp.int32),
                   jax.ShapeDtypeStruct((T // MOE_TILE, N_EXPERTS, LANES), jnp.int32)],
        scratch_shapes=[pltpu.VMEM((N_EXPERTS, LANES), F32)],
        compiler_params=_cparams("arbitrary"),
        name="route_shared_ple",
    )(x1, p2d, wr_t, rbias, wsg, wsu, wsd, wpg, wpp)


def _plan_sc_kernel(off_hbm, idx_hbm, rank_hbm, gate_hbm, rows_hbm, gates_hbm, seg_hbm,
                    off_v, idx_v, rank_v, gate_v, rows_v, gates_v, seg_v):
    n_tokens = idx_hbm.shape[0] // TOP_K
    worker = lax.axis_index("subcore") * SC_CORES + lax.axis_index("core")

    @pl.when(worker < n_tokens // MOE_TILE)
    def _():
        pltpu.sync_copy(off_hbm.at[pl.ds(worker * N_EXPERTS, N_EXPERTS)], off_v)
        lane = lax.iota(jnp.int32, SC_LANES)

        @pl.loop(0, N_EXPERTS * GATE_SEG // SC_LANES)
        def _(i):
            seg_v[pl.ds(i * SC_LANES, SC_LANES)] = jnp.zeros((SC_LANES,), F32)

        for k in range(TOP_K):
            row = pl.ds(k * n_tokens + worker * MOE_TILE, MOE_TILE)
            pltpu.sync_copy(idx_hbm.at[row], idx_v)
            pltpu.sync_copy(rank_hbm.at[row], rank_v)
            pltpu.sync_copy(gate_hbm.at[row], gate_v)

            @pl.loop(0, MOE_TILE // SC_LANES)
            def _(i):
                at = i * SC_LANES
                expert, rank, gate = (v[pl.ds(at, SC_LANES)] for v in (idx_v, rank_v, gate_v))
                pos = plsc.load_gather(off_v, [expert]) + rank
                plsc.store_scatter(rows_v, [pos], (lane + at) * LANE_CHUNKS)
                plsc.store_scatter(gates_v, [pos], gate)
                plsc.store_scatter(seg_v, [expert * GATE_SEG + rank], gate, mask=rank < GATE_SEG)

        @pl.loop(0, LIST_PAD // SC_LANES)
        def _(i):
            tail = pl.ds(MOE_TILE * TOP_K + i * SC_LANES, SC_LANES)
            rows_v[tail] = jnp.zeros((SC_LANES,), jnp.int32)
            gates_v[tail] = jnp.zeros((SC_LANES,), F32)

        out = pl.ds(worker * LIST_LEN, LIST_LEN)
        pltpu.sync_copy(rows_v, rows_hbm.at[out])
        pltpu.sync_copy(gates_v, gates_hbm.at[out])
        pltpu.sync_copy(seg_v, seg_hbm.at[pl.ds(worker * (N_EXPERTS * GATE_SEG), N_EXPERTS * GATE_SEG)])


def _plan_sc(off, idx, rank, gate):
    n_tiles = idx.shape[1] // MOE_TILE
    assert n_tiles <= SC_CORES * SC_SUBCORES
    mesh = plsc.VectorSubcoreMesh(core_axis_name="core", subcore_axis_name="subcore",
                                  num_cores=SC_CORES, num_subcores=SC_SUBCORES)
    rows, gates, seg = pl.kernel(
        _plan_sc_kernel,
        out_type=(jax.ShapeDtypeStruct((n_tiles * LIST_LEN,), jnp.int32),
                  jax.ShapeDtypeStruct((n_tiles * LIST_LEN,), F32),
                  jax.ShapeDtypeStruct((n_tiles * N_EXPERTS * GATE_SEG,), F32)),
        mesh=mesh,
        scratch_types=[pltpu.VMEM((N_EXPERTS,), jnp.int32), pltpu.VMEM((MOE_TILE,), jnp.int32),
                       pltpu.VMEM((MOE_TILE,), jnp.int32), pltpu.VMEM((MOE_TILE,), F32),
                       pltpu.VMEM((LIST_LEN,), jnp.int32), pltpu.VMEM((LIST_LEN,), F32),
                       pltpu.VMEM((N_EXPERTS * GATE_SEG,), F32)],
        compiler_params=pltpu.CompilerParams(needs_layout_passes=False),
        name="moe_plan_sc",
    )(off, idx.reshape(-1), rank.reshape(-1), gate.reshape(-1))
    return rows, gates, seg.reshape(n_tiles * N_EXPERTS, 1, GATE_SEG)


def _moe_tile_kernel(cnt_ref, off_ref, wg_ref, wu_ref, wd_ref, lw_ref, lb_ref, seg_ref, rows_hbm, gates_hbm, x_hbm,
                     base_hbm, o_hbm, x_s, acc_s, xg_a, xg_b, xg_c, y_a, y_b, y_c, stage_s, rows_s, gates_s, sem):
    tile, e = pl.program_id(0), pl.program_id(1)
    rows_of = lambda ref, r, n: ref.at[pl.ds(pl.multiple_of(r * LANE_CHUNKS, LANE_CHUNKS), n * LANE_CHUNKS), :]
    tile_rows = pl.ds(pl.multiple_of(tile * (MOE_TILE * LANE_CHUNKS), LANE_CHUNKS), MOE_TILE * LANE_CHUNKS)

    pair = tile * N_EXPERTS + e
    last_pair = pl.num_programs(0) * N_EXPERTS - 1
    n, off = cnt_ref[pair], off_ref[pair]
    tile_at = lambda ref, r: ref.at[pl.ds(pl.multiple_of(r, LANE_CHUNKS), LANE_CHUNKS), :]

    def gather_group(xg, first, jb):
        at = first + jb * GATHER_GROUP
        rows = [tile_at(x_s, rows_s[at + u])[...] for u in range(GATHER_GROUP)]
        rows_of(xg, jb * GATHER_GROUP, GATHER_GROUP)[...] = jnp.concatenate(rows, axis=0)

    def gather_loop(xg, first):
        def body(jb, cc):
            gather_group(xg, first, jb)
            return cc

        lax.fori_loop(0, MOE_CHUNK // GATHER_GROUP, body, 0)

    def swiglu(xg, y):
        xb = _load_row_tiles(xg, MOE_CHUNK).astype(BF16)
        hg = jnp.dot(xb, wg_ref[...], preferred_element_type=F32)
        hu = jnp.dot(xb, wu_ref[...], preferred_element_type=F32)
        act = (hg * jax.nn.sigmoid(hg) * hu).astype(BF16)
        _store_row_tiles(y, jnp.dot(act, wd_ref[...], preferred_element_type=F32), MOE_CHUNK)

    def scatter_group(y, first, j0, live, gated):
        dsts = [rows_s[first + j0 + u] for u in range(live)]
        yv = rows_of(y, j0, live)[...]
        rows = [yv[u * LANE_CHUNKS:(u + 1) * LANE_CHUNKS] for u in range(live)]
        if not gated:
            rows = [gates_s[first + j0 + u] * r for u, r in enumerate(rows)]
        vals = [tile_at(acc_s, d)[...] + r for d, r in zip(dsts, rows)]
        for d, val in reversed(list(zip(dsts, vals))):
            tile_at(acc_s, d)[...] = val

    def scatter_loop(y, first, m, gated):
        def body(jg, cc):
            scatter_group(y, first, jg * SCATTER_GROUP, SCATTER_GROUP, gated)
            return cc

        lax.fori_loop(0, m // SCATTER_GROUP, body, 0)
        for live in range(1, SCATTER_GROUP):
            @pl.when(m % SCATTER_GROUP == live)
            def _(live=live):
                scatter_group(y, first, m - live, live, gated)

    def plan_loads(t):
        plan = pl.ds(pl.multiple_of(t * LIST_LEN, LIST_PAD), LIST_LEN)
        return (pltpu.make_async_copy(rows_hbm.at[plan], rows_s, sem.at[0]),
                pltpu.make_async_copy(gates_hbm.at[plan], gates_s, sem.at[1]))

    def x_load(t):
        rows = pl.ds(pl.multiple_of(t * (MOE_TILE * LANE_CHUNKS), LANE_CHUNKS), MOE_TILE * LANE_CHUNKS)
        return pltpu.make_async_copy(x_hbm.at[rows, :], x_s, sem.at[2])

    @pl.when(e == 0)
    def _():
        @pl.when(tile == 0)
        def _():
            x_load(tile).start()
            for load in plan_loads(tile):
                load.start()

        load_base = pltpu.make_async_copy(base_hbm.at[tile_rows, :], acc_s, sem.at[3])
        load_base.start()
        y_b[...] = jnp.zeros_like(y_b)
        load_rows, load_gates = plan_loads(tile)
        load_rows.wait()
        x_load(tile).wait()
        gather_loop(xg_a, off)
        load_gates.wait()
        load_base.wait()

    prev_off = off_ref[jnp.maximum(pair - 1, 0)]
    next_off = off_ref[jnp.minimum(pair + 1, last_pair)]
    live_row = lax.broadcasted_iota(jnp.int32, (MOE_CHUNK, 2 * LANES), 0) < n

    def run_expert(xg_cur, y_cur, xg_nxt, y_prv):
        gathers = [functools.partial(gather_group, xg_nxt, next_off, jb) for jb in range(MOE_CHUNK // GATHER_GROUP)]
        scatters = [functools.partial(scatter_group, y_prv, prev_off, jg * SCATTER_GROUP, SCATTER_GROUP, True)
                    for jg in range(MOE_CHUNK // SCATTER_GROUP)]
        side = [s for both in zip(gathers, scatters) for s in both]
        n_down = LANE_CHUNKS // 2
        cost = [D_MODEL, D_MODEL] + [EXPERT_FF] * n_down
        bounds = [round(len(side) * sum(cost[:i]) / sum(cost)) for i in range(len(cost) + 1)]

        def side_work(i):
            for s in side[bounds[i]:bounds[i + 1]]:
                s()

        xb = _load_row_tiles(xg_cur, MOE_CHUNK).astype(BF16)
        side_work(0)
        hg = jnp.dot(xb, wg_ref[...], preferred_element_type=F32)
        side_work(1)
        hu = jnp.dot(xb, wu_ref[...], preferred_element_type=F32)
        gate_col = jnp.broadcast_to(seg_ref[...], (LANES, GATE_SEG)).T[:MOE_CHUNK]
        gate_col = jnp.concatenate([gate_col] * (EXPERT_FF // LANES), axis=1)
        act = (hg * jax.nn.sigmoid(hg) * hu * gate_col).astype(BF16)
        for q in range(n_down):
            side_work(2 + q)
            out = jnp.dot(act, wd_ref[:, q * 2 * LANES:(q + 1) * 2 * LANES], preferred_element_type=F32)
            out = jnp.where(live_row, out, 0.0)
            for c in range(2):
                y_cur[pl.ds(2 * q + c, MOE_CHUNK, stride=LANE_CHUNKS), :] = out[:, c * LANES:(c + 1) * LANES]

    @pl.when(e % 2 == 0)
    def _():
        run_expert(xg_a, y_a, xg_b, y_b)

    @pl.when(e % 2 == 1)
    def _():
        run_expert(xg_b, y_b, xg_a, y_a)

    def extra_chunk(c, carry):
        first = off + c * MOE_CHUNK
        gather_loop(xg_c, first)
        swiglu(xg_c, y_c)
        scatter_loop(y_c, first, jnp.minimum(MOE_CHUNK, n - c * MOE_CHUNK), False)
        return carry

    lax.fori_loop(1, (n + MOE_CHUNK - 1) // MOE_CHUNK, extra_chunk, 0)

    @pl.when(e == N_EXPERTS - 1)
    def _():
        more_tiles = tile + 1 < pl.num_programs(0)

        @pl.when(more_tiles)
        def _():
            x_load(tile + 1).start()

        scatter_loop(y_b, off, jnp.minimum(MOE_CHUNK, n), True)

        @pl.when(more_tiles)
        def _():
            for load in plan_loads(tile + 1):
                load.start()

        n_pieces = MOE_TILE // LN_ROWS
        store = lambda c: pltpu.make_async_copy(
            stage_s.at[c % 2], o_hbm.at[pl.ds(tile * MOE_TILE + c * LN_ROWS, LN_ROWS), :], sem.at[4 + c % 2])
        for c in range(n_pieces):
            z = _load_row_tiles(acc_s, LN_ROWS, c * LN_ROWS * LANE_CHUNKS)
            if c >= 2:
                store(c - 2).wait()
            stage_s[c % 2] = _layer_norm(z, lw_ref[...], lb_ref[...])
            store(c).start()
        store(n_pieces - 2).wait()
        store(n_pieces - 1).wait()


def _moe_tiles(x1, base, plan_rows, plan_gates, gate_seg, cnt, off, wg, wu, wd, ln_w, ln_b):
    T = x1.shape[0] // LANE_CHUNKS
    w_spec = lambda shape: pl.BlockSpec((None,) + shape, lambda i, e, cnt, off: (e, 0, 0))
    vec = pl.BlockSpec((1, D_MODEL), lambda i, e, cnt, off: (0, 0))
    hbm = pl.BlockSpec(memory_space=pl.ANY)
    tile_rows = MOE_TILE * LANE_CHUNKS
    return pl.pallas_call(
        _moe_tile_kernel,
        grid_spec=pltpu.PrefetchScalarGridSpec(
            num_scalar_prefetch=2,
            grid=(T // MOE_TILE, N_EXPERTS),
            in_specs=[w_spec((D_MODEL, EXPERT_FF)), w_spec((D_MODEL, EXPERT_FF)), w_spec((EXPERT_FF, D_MODEL)),
                      vec, vec,
                      pl.BlockSpec((None, 1, GATE_SEG), lambda i, e, cnt, off: (i * N_EXPERTS + e, 0, 0)),
                      hbm, hbm, hbm, hbm],
            out_specs=hbm,
            scratch_shapes=[pltpu.VMEM((tile_rows, LANES), F32),
                            pltpu.VMEM((tile_rows, LANES), F32)]
                           + [pltpu.VMEM((MOE_CHUNK * LANE_CHUNKS, LANES), F32)] * 6
                           + [pltpu.VMEM((2, LN_ROWS, D_MODEL), F32),
                              pltpu.SMEM((LIST_LEN,), jnp.int32),
                              pltpu.SMEM((LIST_LEN,), F32),
                              pltpu.SemaphoreType.DMA((6,))],
        ),
        out_shape=jax.ShapeDtypeStruct((T, D_MODEL), F32),
        compiler_params=_cparams("arbitrary", "arbitrary"),
        name="moe_tiles_ln2",
    )(cnt, off, wg, wu, wd, ln_w, ln_b, gate_seg, plan_rows, plan_gates, x1, base)


def kernel(x, p, w_in, hgrn_lb_logits, hgrn_norm_w, w_branch_att, w_branch_hgrn, w_out, ln1_w, ln1_b, router_w, router_bias, expert_w_gate, expert_w_up, expert_w_down, shared_w_gate, shared_w_up, shared_w_down, ple_gate_w, ple_proj_w, ln2_w, ln2_b):
    B, S, D = x.shape
    T = B * S
    l = 0
    x2d = x.reshape(T, D)
    bf = lambda a: a.astype(BF16)

    att_cols = 3 * len(ATT_DILATIONS) * ATT_WIDTH
    assert att_cols % PROJ_COLS == 0 and (w_in.shape[2] - att_cols) % PROJ_COLS == 0
    hg_blocks = (w_in[l], att_cols // PROJ_COLS, (w_in.shape[2] - att_cols) // PROJ_COLS)
    qkv = [_proj_att(x2d, w_in[l], g, d, hg_blocks if g == 0 else None) for g, d in enumerate(ATT_DILATIONS)]
    qkv[0], w_hg = qkv[0]
    y_att = _attention(qkv, B, S)
    u_hg, wg_b, wu_b, wd_b = _proj(x2d, w_hg, (expert_w_gate[l], expert_w_up[l], expert_w_down[l]))
    y_hg = _hgrn(u_hg, hgrn_lb_logits, hgrn_norm_w[l:l + 1], B, S)
    x1 = _merge(y_att, y_hg, u_hg, x2d, bf(w_branch_att[l]), bf(w_branch_hgrn[l]), bf(w_out[l]),
                ln1_w[l:l + 1], ln1_b[l:l + 1])

    base, idx, gate, rank, counts = _route(
        x1, p[l].reshape(T, PLE_DIM), jnp.stack(_split2(router_w[l].T)), router_bias[l].reshape(N_EXPERTS, 1),
        bf(shared_w_gate[l]), bf(shared_w_up[l]), bf(shared_w_down[l]), bf(ple_gate_w[l]), bf(ple_proj_w[l]))
    cnt = counts[:, :, 0]
    off = jnp.cumsum(cnt, axis=1) - cnt
    cnt, off = cnt.reshape(-1), off.reshape(-1)
    plan_rows, plan_gates, gate_seg = _plan_sc(off, idx, rank, gate)
    out = _moe_tiles(x1, base, plan_rows, plan_gates, gate_seg, cnt, off, wg_b, wu_b, wd_b,
                     ln2_w[l:l + 1], ln2_b[l:l + 1])
    return out.reshape(B, S, D)
```

```python
import functools

import jax
import jax.numpy as jnp
import numpy as np
from jax import lax
from jax.experimental import pallas as pl
from jax.experimental.pallas import tpu as pltpu
from jax.experimental.pallas import tpu_sc as plsc

F32 = jnp.float32
BF16 = jnp.bfloat16

D_MODEL = 1024
ATT_HEAD_DIM = 64
ATT_HEADS = 8
ATT_DILATIONS = (1, 4, 16)
ATT_BLOCK = 128
ATT_WIDTH = ATT_HEADS * ATT_HEAD_DIM
ATT_TILE = ATT_BLOCK * max(ATT_DILATIONS)
NEG_INF = -1e30
LOG2_E = 1.4426950408889634

HG_HEADS = 8
HG_DIM = 128
HG_WIDTH = HG_HEADS * HG_DIM
HG_CHUNK = 32
HG_TILE = 256
RMS_EPS = 1e-6

N_EXPERTS = 64
TOP_K = 8
N_GROUPS = 8
GROUP_SIZE = N_EXPERTS // N_GROUPS
TOPK_GROUPS = 4
EXPERT_FF = 256
ROUTED_SCALE = 2.5
PLE_DIM = 256
LN_EPS = 1e-5
DEPTH = 1
DEEPNORM_ALPHA = (2.0 * DEPTH) ** 0.25

LANES = 128
LANE_CHUNKS = D_MODEL // LANES
PROJ_ROWS = 512
PROJ_COLS = 1536
ATT_PROJ_ROWS = 1024
STRIDE_STEP = 4
MIX_ROWS = 512
MOE_TILE = 4096
MOE_CHUNK = 576
GATE_SEG = 640
LN_ROWS = 256
LIST_PAD = 1024
LIST_LEN = MOE_TILE * TOP_K + LIST_PAD
GATHER_GROUP = 8
SCATTER_GROUP = 8
V7X_VMEM_LIMIT = 56 * 1024 * 1024
SC_CORES, SC_SUBCORES, SC_LANES = 2, 16, 16


def _cparams(*sem):
    return pltpu.CompilerParams(dimension_semantics=sem, vmem_limit_bytes=V7X_VMEM_LIMIT)


def _proj_att_kernel(*refs, dil, n_conv):
    x_refs, (wq_ref, wk_ref, wv_ref) = refs[:LANE_CHUNKS], refs[LANE_CHUNKS:LANE_CHUNKS + 3]
    conv_in = refs[LANE_CHUNKS + 3:LANE_CHUNKS + 3 + n_conv]
    o_ref = refs[LANE_CHUNKS + 3 + n_conv]
    conv_out = refs[LANE_CHUNKS + 4 + n_conv:LANE_CHUNKS + 4 + 2 * n_conv]
    w_ref = refs[LANE_CHUNKS + 4 + 2 * n_conv]
    stage_refs = refs[LANE_CHUNKS + 5 + 2 * n_conv:]
    n = ATT_PROJ_ROWS // dil
    for src, dst in zip(conv_in, conv_out):
        dst[...] = src[...].astype(BF16)

    @pl.when(pl.program_id(0) == 0)
    def _():
        w_ref[:, :ATT_WIDTH] = (wq_ref[...] * (ATT_HEAD_DIM ** -0.5 * LOG2_E)).astype(BF16)
        w_ref[:, ATT_WIDTH:2 * ATT_WIDTH] = wk_ref[...].astype(BF16)
        w_ref[:, 2 * ATT_WIDTH:] = wv_ref[...].astype(BF16)

    def rows(ref, stage):
        if dil == 1:
            return ref[...]
        if dil <= STRIDE_STEP:
            return jnp.concatenate([ref[pl.ds(r, n, stride=dil), :] for r in range(dil)], axis=0)
        assert dil == STRIDE_STEP * STRIDE_STEP
        quarter = ATT_PROJ_ROWS // STRIDE_STEP
        for r1 in range(STRIDE_STEP):
            stage[pl.ds(r1 * quarter, quarter), :] = ref[pl.ds(r1, quarter, stride=STRIDE_STEP), :]
        return jnp.concatenate(
            [stage[pl.ds((r % STRIDE_STEP) * quarter + r // STRIDE_STEP, n, stride=STRIDE_STEP), :]
             for r in range(dil)], axis=0)

    xp = jnp.concatenate([rows(ref, stage_refs[c] if stage_refs else None).astype(BF16)
                          for c, ref in enumerate(x_refs)], axis=1)
    y = jnp.dot(xp, w_ref[...], preferred_element_type=F32)
    o_ref[...] = y.astype(BF16).reshape(dil, n, 3 * ATT_WIDTH)


def _proj_att(x2d, w_in_l, g, dil, conv=None):
    T = x2d.shape[0]
    per = ATT_TILE // ATT_PROJ_ROWS
    n = ATT_PROJ_ROWS // dil
    steps = T // ATT_PROJ_ROWS
    n_groups = len(ATT_DILATIONS)
    conv_args, conv_in, conv_out, conv_shapes = [], [], [], []
    for mat, width, first, blocks in conv or ():
        slab = mat.shape[0] // steps
        assert slab * steps == mat.shape[0] and slab % 16 == 0
        conv_args += [mat] * blocks
        conv_in += [pl.BlockSpec((slab, width), functools.partial(lambda i, c: (i, c), c=first + c))
                    for c in range(blocks)]
        conv_out += [pl.BlockSpec((slab, width), lambda i: (i, 0))] * blocks
        conv_shapes += [jax.ShapeDtypeStruct((mat.shape[0], width), BF16)] * blocks
    out, *converted = pl.pallas_call(
        functools.partial(_proj_att_kernel, dil=dil, n_conv=len(conv_args)),
        grid=(steps,),
        in_specs=[pl.BlockSpec((ATT_PROJ_ROWS, LANES), functools.partial(lambda i, c: (i, c), c=c))
                  for c in range(LANE_CHUNKS)]
                 + [pl.BlockSpec((D_MODEL, ATT_WIDTH), functools.partial(lambda i, c: (0, c), c=part * n_groups + g))
                    for part in range(3)] + conv_in,
        out_specs=[pl.BlockSpec((None, dil, None, n, 3 * ATT_WIDTH), lambda i: (i // per, 0, i % per, 0, 0))]
                  + conv_out,
        out_shape=[jax.ShapeDtypeStruct((T // ATT_TILE, dil, per, n, 3 * ATT_WIDTH), BF16)] + conv_shapes,
        scratch_shapes=[pltpu.VMEM((D_MODEL, 3 * ATT_WIDTH), BF16)]
                       + ([pltpu.VMEM((ATT_PROJ_ROWS, LANES), F32)] * LANE_CHUNKS if dil > STRIDE_STEP else []),
        compiler_params=_cparams("arbitrary"),
        name=f"proj_att_d{dil}",
    )(*([x2d] * LANE_CHUNKS), *([w_in_l] * 3), *conv_args)
    out = out.reshape(T // ATT_TILE, dil, ATT_TILE // dil, 3 * ATT_WIDTH)
    return (out, converted) if conv is not None else out


def _att_pair(q2, kp, kc, vp, vc, bias_ref, g, first):
    def head0_lanes(rows, dtype):
        lane = lax.broadcasted_iota(jnp.int32, (rows, 2 * ATT_HEAD_DIM), 1)
        return lane.astype(F32).astype(dtype) < ATT_HEAD_DIM

    lo_q = head0_lanes(ATT_BLOCK, BF16)
    lo_v = head0_lanes(2 * ATT_BLOCK, BF16)
    k2 = jnp.concatenate([kp, kc], axis=0)
    v2 = jnp.concatenate([vp, vc], axis=0)
    zero = jnp.zeros_like(q2)
    ps, ms = [], []
    for hh in range(2):
        qm = jnp.where(lo_q, q2, zero) if hh == 0 else jnp.where(lo_q, zero, q2)
        s = lax.dot_general(qm, k2, (((1,), (1,)), ((), ())), preferred_element_type=F32)
        s = s + bias_ref[g, hh, first]
        m = jnp.max(s, axis=-1, keepdims=True)
        ps.append(jnp.exp2(s - m).astype(BF16))
        ms.append(m)
    pcat = jnp.concatenate(ps, axis=1)
    zero_v, one_v = jnp.zeros_like(v2), jnp.ones_like(v2)
    rhs = jnp.concatenate([
        jnp.concatenate([jnp.where(lo_v, v2, zero_v), jnp.where(lo_v, one_v, zero_v)], axis=1),
        jnp.concatenate([jnp.where(lo_v, zero_v, v2), jnp.where(lo_v, zero_v, one_v)], axis=1)], axis=0)
    nd = jnp.dot(pcat, rhs, preferred_element_type=F32)
    m2 = jnp.where(head0_lanes(ATT_BLOCK, F32), ms[0], ms[1])
    return nd[:, :2 * ATT_HEAD_DIM], m2, nd[:, 2 * ATT_HEAD_DIM:]


def _att_kernel(*refs):
    (q0, kc0, vc0, kp0, vp0, q1, kc1, vc1, kp1, vp1, q2, kc2, vc2, kp2, vp2,
     bias_ref, o_ref) = refs[:17]
    ng = len(ATT_DILATIONS)
    num_s, m_s, den_s = refs[17:17 + ng], refs[17 + ng:17 + 2 * ng], refs[17 + 2 * ng:]
    first_tile = (pl.program_id(2) == 0).astype(jnp.int32)
    groups = ((q0, kc0, vc0, kp0, vp0), (q1, kc1, vc1, kp1, vp1), (q2, kc2, vc2, kp2, vp2))
    for g, dil in enumerate(ATT_DILATIONS):
        q_ref, kc_ref, vc_ref, kp_ref, vp_ref = groups[g]
        nb = ATT_TILE // dil // ATT_BLOCK
        for r in range(dil):
            for n in range(nb):
                rows = pl.ds(n * ATT_BLOCK, ATT_BLOCK)
                if n == 0:
                    prev = pl.ds((nb - 1) * ATT_BLOCK, ATT_BLOCK)
                    kp, vp, first = kp_ref[r, prev, :], vp_ref[r, prev, :], first_tile
                else:
                    prev = pl.ds((n - 1) * ATT_BLOCK, ATT_BLOCK)
                    kp, vp, first = kc_ref[r, prev, :], vc_ref[r, prev, :], 0
                num, m, den = _att_pair(q_ref[r, rows, :], kp, kc_ref[r, rows, :], vp, vc_ref[r, rows, :],
                                        bias_ref, g, first)
                if dil == 1:
                    dst = rows
                else:
                    dst = pl.ds(n * ATT_BLOCK * dil + r, ATT_BLOCK, stride=dil)
                num_s[g][dst, :] = num
                m_s[g][dst, :] = m
                den_s[g][dst, :] = den
    m_all = jnp.maximum(jnp.maximum(m_s[0][...], m_s[1][...]), m_s[2][...])
    num = jnp.zeros((ATT_TILE, 2 * ATT_HEAD_DIM), F32)
    den = jnp.zeros((ATT_TILE, 2 * ATT_HEAD_DIM), F32)
    for g in range(ng):
        sc = jnp.exp2(m_s[g][...] - m_all)
        num = num + sc * num_s[g][...]
        den = den + sc * den_s[g][...]
    o_ref[...] = (num / den).astype(o_ref.dtype)


def _att_bias_table():
    qi = np.arange(ATT_BLOCK)[:, None]
    ki = np.arange(2 * ATT_BLOCK)[None, :]
    steps = qi + ATT_BLOCK - ki
    valid = (steps >= 0) & (steps <= ATT_BLOCK)
    slopes = np.array([2.0 ** (-8.0 * (h + 1) / ATT_HEADS) for h in range(ATT_HEADS)], np.float32)
    tab = np.empty((len(ATT_DILATIONS), ATT_HEADS, 2, ATT_BLOCK, 2 * ATT_BLOCK), np.float32)
    for g, dil in enumerate(ATT_DILATIONS):
        bias = -slopes[:, None, None] * (steps * dil).astype(np.float32)[None] * LOG2_E
        tab[g, :, 0] = np.where(valid[None], bias, NEG_INF)
        tab[g, :, 1] = np.where((valid & (ki >= ATT_BLOCK))[None], bias, NEG_INF)
    return jnp.asarray(tab)


def _attention(qkv, B, S):
    tiles = S // ATT_TILE
    pair = 2 * ATT_HEAD_DIM
    npair = ATT_WIDTH // pair
    in_specs, args = [], []
    for g, dil in enumerate(ATT_DILATIONS):
        blk = (None, dil, ATT_TILE // dil, pair)
        cur = lambda b, hp, t, off: (b * tiles + t, 0, 0, off * npair + hp)
        prv = lambda b, hp, t, off: (b * tiles + jnp.maximum(t - 1, 0), 0, 0, off * npair + hp)
        in_specs += [pl.BlockSpec(blk, functools.partial(cur, off=0)),
                     pl.BlockSpec(blk, functools.partial(cur, off=1)),
                     pl.BlockSpec(blk, functools.partial(cur, off=2)),
                     pl.BlockSpec(blk, functools.partial(prv, off=1)),
                     pl.BlockSpec(blk, functools.partial(prv, off=2))]
        args += [qkv[g]] * 5
    in_specs.append(pl.BlockSpec((len(ATT_DILATIONS), 2, 2, ATT_BLOCK, 2 * ATT_BLOCK),
                                 lambda b, hp, t: (0, hp, 0, 0, 0)))
    args.append(_att_bias_table())
    scratch = [pltpu.VMEM((ATT_TILE, pair), F32) for _ in range(3 * len(ATT_DILATIONS))]
    return pl.pallas_call(
        _att_kernel,
        grid=(B, npair, tiles),
        in_specs=in_specs,
        out_specs=pl.BlockSpec((ATT_TILE, pair), lambda b, hp, t: (b * tiles + t, hp)),
        out_shape=jax.ShapeDtypeStruct((B * S, ATT_WIDTH), BF16),
        scratch_shapes=scratch,
        compiler_params=_cparams("parallel", "parallel", "arbitrary"),
        name="dilated_attention",
    )(*args)


def _proj_kernel(x_ref, *refs, n_w):
    w_refs = refs[:n_w]
    eg_ref, eu_ref, ed_ref, o_ref, egb_ref, eub_ref, edb_ref = refs[n_w:]
    egb_ref[...] = eg_ref[...].astype(BF16)
    eub_ref[...] = eu_ref[...].astype(BF16)
    edb_ref[...] = ed_ref[...].astype(BF16)
    xb = x_ref[...].astype(BF16)
    col_tile = w_refs[0].shape[1]
    for c, w_ref in enumerate(w_refs):
        cols = slice(c * col_tile, (c + 1) * col_tile)
        o_ref[:, cols] = jnp.dot(xb, w_ref[...], preferred_element_type=F32).astype(o_ref.dtype)


def _proj(x2d, ws, expert_weights):
    T, N = x2d.shape[0], len(ws) * ws[0].shape[1]
    steps = T // PROJ_ROWS
    n_experts = expert_weights[0].shape[0]
    per_step = max(1, n_experts // steps)
    assert n_experts % per_step == 0 and n_experts // per_step <= steps
    per_expert = lambda ew: pl.BlockSpec(
        (per_step,) + ew.shape[1:], lambda i: (jnp.minimum(i, n_experts // per_step - 1), 0, 0))
    return pl.pallas_call(
        functools.partial(_proj_kernel, n_w=len(ws)),
        grid=(steps,),
        in_specs=[pl.BlockSpec((PROJ_ROWS, D_MODEL), lambda i: (i, 0))]
                 + [pl.BlockSpec(w.shape, lambda i: (0, 0)) for w in ws] + [per_expert(ew) for ew in expert_weights],
        out_specs=[pl.BlockSpec((PROJ_ROWS, N), lambda i: (i, 0))] + [per_expert(ew) for ew in expert_weights],
        out_shape=[jax.ShapeDtypeStruct((T, N), BF16)]
                  + [jax.ShapeDtypeStruct(ew.shape, BF16) for ew in expert_weights],
        compiler_params=_cparams("parallel"),
        name="proj_hgrn_gates",
    )(x2d, *ws, *expert_weights)


def _split2(v):
    a = v.astype(BF16)
    return a, (v - a.astype(F32)).astype(BF16)


def _hgrn_kernel(q_ref, f_ref, i_ref, g_ref, lbl_ref, gain_ref, o_ref, state_ref):
    @pl.when(pl.program_id(1) == 0)
    def _():
        state_ref[...] = jnp.zeros_like(state_ref)

    lbl = lbl_ref[...]
    e = jnp.exp(lbl - jnp.max(lbl, axis=0, keepdims=True))
    lb = e[0:1] / jnp.sum(e, axis=0, keepdims=True)
    forget = lb + (1.0 - lb) * jax.nn.sigmoid(f_ref[...].astype(F32))
    log_f = jnp.log(forget)
    key = 1.0 - forget

    row = lax.broadcasted_iota(jnp.int32, (HG_TILE, HG_TILE), 0)
    col = lax.broadcasted_iota(jnp.int32, (HG_TILE, HG_TILE), 1)
    causal = (row >= col) & ((row // HG_CHUNK) == (col // HG_CHUNK))
    tri = jnp.where(causal, 1.0, 0.0).astype(BF16)
    b = sum(jnp.dot(tri, t, preferred_element_type=F32) for t in _split2(log_f))
    eb = jnp.exp(b)
    q_dec = (q_ref[...].astype(F32) * eb).astype(BF16)
    k_inv = key * jnp.exp(-b)
    xi = i_ref[...].astype(F32)
    val = (xi * jax.nn.sigmoid(xi)).astype(BF16)
    k_inv_b = k_inv.astype(BF16)

    n_chunks = HG_TILE // HG_CHUNK
    last_rows = [eb[(c + 1) * HG_CHUNK - 1:(c + 1) * HG_CHUNK, :] for c in range(n_chunks)]
    dec_rows = jnp.concatenate([jnp.broadcast_to(r, (HG_CHUNK, HG_WIDTH)) for r in last_rows], axis=0)
    k_end = (k_inv * dec_rows).astype(BF16)
    def per_chunk_columns(t):
        blocks = []
        for c in range(n_chunks):
            rows_above, rows_below = c * HG_CHUNK, HG_TILE - (c + 1) * HG_CHUNK
            parts = [t[rows_above:rows_above + HG_CHUNK]]
            if rows_above:
                parts.insert(0, jnp.zeros((rows_above, HG_DIM), t.dtype))
            if rows_below:
                parts.append(jnp.zeros((rows_below, HG_DIM), t.dtype))
            blocks.append(jnp.concatenate(parts, axis=0))
        return jnp.concatenate(blocks, axis=1)

    head_cols = [slice(h * HG_DIM, (h + 1) * HG_DIM) for h in range(HG_HEADS)]
    upds = [lax.dot_general(val[:, cols], per_chunk_columns(k_end[:, cols]), (((0,), (0,)), ((), ())),
                            preferred_element_type=F32) for cols in head_cols]
    o_intras = []
    for cols in head_cols:
        a = lax.dot_general(q_dec[:, cols], k_inv_b[:, cols], (((1,), (1,)), ((), ())), preferred_element_type=F32)
        a = jnp.where(causal, a, 0.0).astype(BF16)
        o_intras.append(jnp.dot(a, val[:, cols], preferred_element_type=F32))
    enterings = []
    for h, cols in enumerate(head_cols):
        st = state_ref[h]
        entering = []
        for c in range(n_chunks):
            entering.append(st.astype(BF16))
            st = st * last_rows[c][:, cols] + upds[h][:, c * HG_DIM:(c + 1) * HG_DIM]
        state_ref[h] = st
        enterings.append(jnp.concatenate(entering, axis=1))
    outs = []
    for h, cols in enumerate(head_cols):
        qd = q_dec[:, cols]
        o_inter = lax.dot_general(per_chunk_columns(qd), enterings[h],
                                  (((1,), (1,)), ((), ())), preferred_element_type=F32)
        o = o_intras[h] + o_inter
        o = o * lax.rsqrt(jnp.mean(jnp.square(o), axis=-1, keepdims=True) + RMS_EPS)
        outs.append(o)
    o = jnp.concatenate(outs, axis=1) * gain_ref[...]
    gg = g_ref[...].astype(F32)
    o_ref[...] = (o * (gg * jax.nn.sigmoid(gg))).astype(o_ref.dtype)


def _hgrn(u_hg, lb_logits, gain, B, S):
    tiles = S // HG_TILE
    col = lambda j: pl.BlockSpec((HG_TILE, HG_WIDTH), functools.partial(lambda b, t, j: (b * tiles + t, j), j=j))
    return pl.pallas_call(
        _hgrn_kernel,
        grid=(B, tiles),
        in_specs=[col(0), col(1), col(2), col(3),
                  pl.BlockSpec((2, HG_WIDTH), lambda b, t: (0, 0)),
                  pl.BlockSpec((1, HG_WIDTH), lambda b, t: (0, 0))],
        out_specs=pl.BlockSpec((HG_TILE, HG_WIDTH), lambda b, t: (b * tiles + t, 0)),
        out_shape=jax.ShapeDtypeStruct((B * S, HG_WIDTH), BF16),
        scratch_shapes=[pltpu.VMEM((HG_HEADS, HG_DIM, HG_DIM), F32)],
        compiler_params=_cparams("parallel", "arbitrary"),
        name="hgrn2",
    )(u_hg, u_hg, u_hg, u_hg, lb_logits, gain)


def _load_row_tiles(ref, n, start=0):
    return jnp.concatenate([ref[pl.ds(start + c, n, stride=LANE_CHUNKS), :] for c in range(LANE_CHUNKS)], axis=1)


def _store_row_tiles(ref, val, n):
    for c in range(LANE_CHUNKS):
        ref[pl.ds(c, n, stride=LANE_CHUNKS), :] = val[:, c * LANES:(c + 1) * LANES]


def _layer_norm(z, w, b):
    mu = jnp.mean(z, axis=-1, keepdims=True)
    zc = z - mu
    var = jnp.mean(jnp.square(zc), axis=-1, keepdims=True)
    return zc * lax.rsqrt(var + LN_EPS) * w + b


def _merge_kernel(ya_ref, yh_ref, ga_ref, gh_ref, x_ref, wa_ref, wh_ref, wo_ref, lw_ref, lb_ref, o_ref):
    ma = jnp.dot(ya_ref[...], wa_ref[...], preferred_element_type=F32)
    mh = jnp.dot(yh_ref[...], wh_ref[...], preferred_element_type=F32)
    merged = (jax.nn.sigmoid(ga_ref[...].astype(F32)) * ma + jax.nn.sigmoid(gh_ref[...].astype(F32)) * mh)
    z = DEEPNORM_ALPHA * x_ref[...] + jnp.dot(merged.astype(BF16), wo_ref[...], preferred_element_type=F32)
    _store_row_tiles(o_ref, _layer_norm(z, lw_ref[...], lb_ref[...]), MIX_ROWS)


def _merge(y_att, y_hg, u_hg, x2d, w_a, w_h, w_o, ln_w, ln_b):
    T = x2d.shape[0]
    rows = lambda width, j=0: pl.BlockSpec((MIX_ROWS, width), functools.partial(lambda i, j: (i, j), j=j))
    full = lambda a: pl.BlockSpec(a.shape, lambda i: (0, 0))
    return pl.pallas_call(
        _merge_kernel,
        grid=(T // MIX_ROWS,),
        in_specs=[rows(ATT_WIDTH), rows(HG_WIDTH), rows(D_MODEL, 4), rows(D_MODEL, 5), rows(D_MODEL),
                  full(w_a), full(w_h), full(w_o), full(ln_w), full(ln_b)],
        out_specs=pl.BlockSpec((MIX_ROWS * LANE_CHUNKS, LANES), lambda i: (i, 0)),
        out_shape=jax.ShapeDtypeStruct((T * LANE_CHUNKS, LANES), F32),
        compiler_params=_cparams("parallel"),
        name="merge_ln1",
    )(y_att, y_hg, u_hg, u_hg, x2d, w_a, w_h, w_o, ln_w, ln_b)


def _first_argmax(v, ids, n):
    mx = jnp.max(v, axis=0, keepdims=True)
    return mx, jnp.min(jnp.where(v == mx, ids, n), axis=0, keepdims=True)


def _route_kernel(x1_ref, p_ref, wrt_ref, rb_ref, wsg_ref, wsu_ref, wsd_ref, wpg_ref, wpp_ref,
                  base_ref, idx_ref, gate_ref, rank_ref, cnt_ref, carry_ref):
    @pl.when(pl.program_id(0) % (MOE_TILE // MIX_ROWS) == 0)
    def _():
        carry_ref[...] = jnp.zeros_like(carry_ref)

    x1 = _load_row_tiles(x1_ref, MIX_ROWS)
    x1b = x1.astype(BF16)
    x1_lo = (x1 - x1b.astype(F32)).astype(BF16)
    nt = (((1,), (1,)), ((), ()))
    logits = (lax.dot_general(wrt_ref[0], x1b, nt, preferred_element_type=F32)
              + lax.dot_general(wrt_ref[1], x1b, nt, preferred_element_type=F32)
              + lax.dot_general(wrt_ref[0], x1_lo, nt, preferred_element_type=F32))
    hg = jnp.dot(x1b, wsg_ref[...], preferred_element_type=F32)
    hu = jnp.dot(x1b, wsu_ref[...], preferred_element_type=F32)
    ple_gate = jnp.dot(x1b, wpg_ref[...], preferred_element_type=F32)
    ple_proj = jnp.dot(p_ref[...].astype(BF16), wpp_ref[...], preferred_element_type=F32)

    s = jax.nn.sigmoid(logits)
    sel = s + rb_ref[...]
    eid = lax.broadcasted_iota(jnp.int32, (N_EXPERTS, MIX_ROWS), 0)
    neg = -jnp.inf

    grp = sel.reshape(N_GROUPS, GROUP_SIZE, MIX_ROWS)
    mid = lax.broadcasted_iota(jnp.int32, grp.shape, 1)
    m1 = jnp.max(grp, axis=1, keepdims=True)
    i1 = jnp.min(jnp.where(grp == m1, mid, GROUP_SIZE), axis=1, keepdims=True)
    m2 = jnp.max(jnp.where(mid == i1, neg, grp), axis=1, keepdims=True)
    gscore = (m1 + m2).reshape(N_GROUPS, MIX_ROWS)
    gid = lax.broadcasted_iota(jnp.int32, (N_GROUPS, MIX_ROWS), 0)
    gsel = jnp.zeros((N_GROUPS, MIX_ROWS), jnp.bool_)
    for _ in range(TOPK_GROUPS):
        _, gi = _first_argmax(gscore, gid, N_GROUPS)
        hit = gid == gi
        gsel = gsel | hit
        gscore = jnp.where(hit, neg, gscore)
    emask = jnp.broadcast_to(gsel.reshape(N_GROUPS, 1, MIX_ROWS), grp.shape).reshape(N_EXPERTS, MIX_ROWS)
    cand = jnp.where(emask, sel, neg)

    idxs, gates = [], []
    chosen = jnp.zeros((N_EXPERTS, MIX_ROWS), jnp.bool_)
    for _ in range(TOP_K):
        _, ei = _first_argmax(cand, eid, N_EXPERTS)
        hit = eid == ei
        idxs.append(ei)
        gates.append(jnp.sum(jnp.where(hit, s, 0.0), axis=0, keepdims=True))
        chosen = chosen | hit
        cand = jnp.where(hit, neg, cand)
    g = jnp.concatenate(gates, axis=0)
    g = g / jnp.sum(g, axis=0, keepdims=True) * ROUTED_SCALE
    idx_ref[...] = jnp.concatenate(idxs, axis=0)
    gate_ref[...] = g

    onehot = jnp.where(chosen, 1.0, 0.0)
    tr = lax.broadcasted_iota(jnp.int32, (MIX_ROWS, MIX_ROWS), 0)
    tc = lax.broadcasted_iota(jnp.int32, (MIX_ROWS, MIX_ROWS), 1)
    before = jnp.where(tr < tc, 1.0, 0.0).astype(BF16)
    prefix = jnp.dot(onehot.astype(BF16), before, preferred_element_type=F32)
    rankfull = (carry_ref[:, 0:1] + prefix).astype(jnp.int32)
    rank_ref[...] = jnp.concatenate(
        [jnp.sum(jnp.where(eid == ei, rankfull, 0), axis=0, keepdims=True) for ei in idxs], axis=0)
    total = carry_ref[...] + jnp.sum(onehot, axis=1, keepdims=True)
    carry_ref[...] = total
    cnt_ref[...] = total.astype(jnp.int32)

    shared = jnp.dot((hg * jax.nn.sigmoid(hg) * hu).astype(BF16), wsd_ref[...], preferred_element_type=F32)
    ple = jax.nn.sigmoid(ple_gate) * ple_proj
    _store_row_tiles(base_ref, DEEPNORM_ALPHA * x1 + shared + ple, MIX_ROWS)


def _route(x1, p2d, wr_t, rbias, wsg, wsu, wsd, wpg, wpp):
    T = x1.shape[0] // LANE_CHUNKS
    per_tile = MOE_TILE // MIX_ROWS
    full = lambda a: pl.BlockSpec(a.shape, lambda i: (0,) * a.ndim)
    tok = pl.BlockSpec((TOP_K, MIX_ROWS), lambda i: (0, i))
    row_tiles = pl.BlockSpec((MIX_ROWS * LANE_CHUNKS, LANES), lambda i: (i, 0))
    return pl.pallas_call(
        _route_kernel,
        grid=(T // MIX_ROWS,),
        in_specs=[row_tiles,
                  pl.BlockSpec((MIX_ROWS, PLE_DIM), lambda i: (i, 0)),
                  full(wr_t), full(rbias), full(wsg), full(wsu), full(wsd), full(wpg), full(wpp)],
        out_specs=[row_tiles, tok, tok, tok,
                   pl.BlockSpec((None, N_EXPERTS, LANES), lambda i: (i // per_tile, 0, 0))],
        out_shape=[jax.ShapeDtypeStruct((T * LANE_CHUNKS, LANES), F32),
                   jax.ShapeDtypeStruct((TOP_K, T), jnp.int32),
                   jax.ShapeDtypeStruct((TOP_K, T), F32),
                   jax.ShapeDtypeStruct((TOP_K, T), jnp.int32),
                   jax.ShapeDtypeStruct((T // MOE_TILE, N_EXPERTS, LANES), jnp.int32)],
        scratch_shapes=[pltpu.VMEM((N_EXPERTS, LANES), F32)],
        compiler_params=_cparams("arbitrary"),
        name="route_shared_ple",
    )(x1, p2d, wr_t, rbias, wsg, wsu, wsd, wpg, wpp)


def _plan_sc_kernel(off_hbm, idx_hbm, rank_hbm, gate_hbm, rows_hbm, gates_hbm, seg_hbm,
                    off_v, idx_v, rank_v, gate_v, rows_v, gates_v, seg_v):
    n_tokens = idx_hbm.shape[0] // TOP_K
    worker = lax.axis_index("subcore") * SC_CORES + lax.axis_index("core")

    @pl.when(worker < n_tokens // MOE_TILE)
    def _():
        pltpu.sync_copy(off_hbm.at[pl.ds(worker * N_EXPERTS, N_EXPERTS)], off_v)
        lane = lax.iota(jnp.int32, SC_LANES)

        @pl.loop(0, N_EXPERTS * GATE_SEG // SC_LANES)
        def _(i):
            seg_v[pl.ds(i * SC_LANES, SC_LANES)] = jnp.zeros((SC_LANES,), F32)

        for k in range(TOP_K):
            row = pl.ds(k * n_tokens + worker * MOE_TILE, MOE_TILE)
            pltpu.sync_copy(idx_hbm.at[row], idx_v)
            pltpu.sync_copy(rank_hbm.at[row], rank_v)
            pltpu.sync_copy(gate_hbm.at[row], gate_v)

            @pl.loop(0, MOE_TILE // SC_LANES)
            def _(i):
                at = i * SC_LANES
                expert, rank, gate = (v[pl.ds(at, SC_LANES)] for v in (idx_v, rank_v, gate_v))
                pos = plsc.load_gather(off_v, [expert]) + rank
                plsc.store_scatter(rows_v, [pos], (lane + at) * LANE_CHUNKS)
                plsc.store_scatter(gates_v, [pos], gate)
                plsc.store_scatter(seg_v, [expert * GATE_SEG + rank], gate, mask=rank < GATE_SEG)

        @pl.loop(0, LIST_PAD // SC_LANES)
        def _(i):
            tail = pl.ds(MOE_TILE * TOP_K + i * SC_LANES, SC_LANES)
            rows_v[tail] = jnp.zeros((SC_LANES,), jnp.int32)
            gates_v[tail] = jnp.zeros((SC_LANES,), F32)

        out = pl.ds(worker * LIST_LEN, LIST_LEN)
        pltpu.sync_copy(rows_v, rows_hbm.at[out])
        pltpu.sync_copy(gates_v, gates_hbm.at[out])
        pltpu.sync_copy(seg_v, seg_hbm.at[pl.ds(worker * (N_EXPERTS * GATE_SEG), N_EXPERTS * GATE_SEG)])


def _plan_sc(off, idx, rank, gate):
    n_tiles = idx.shape[1] // MOE_TILE
    assert n_tiles <= SC_CORES * SC_SUBCORES
    mesh = plsc.VectorSubcoreMesh(core_axis_name="core", subcore_axis_name="subcore",
                                  num_cores=SC_CORES, num_subcores=SC_SUBCORES)
    rows, gates, seg = pl.kernel(
        _plan_sc_kernel,
        out_type=(jax.ShapeDtypeStruct((n_tiles * LIST_LEN,), jnp.int32),
                  jax.ShapeDtypeStruct((n_tiles * LIST_LEN,), F32),
                  jax.ShapeDtypeStruct((n_tiles * N_EXPERTS * GATE_SEG,), F32)),
        mesh=mesh,
        scratch_types=[pltpu.VMEM((N_EXPERTS,), jnp.int32), pltpu.VMEM((MOE_TILE,), jnp.int32),
                       pltpu.VMEM((MOE_TILE,), jnp.int32), pltpu.VMEM((MOE_TILE,), F32),
                       pltpu.VMEM((LIST_LEN,), jnp.int32), pltpu.VMEM((LIST_LEN,), F32),
                       pltpu.VMEM((N_EXPERTS * GATE_SEG,), F32)],
        compiler_params=pltpu.CompilerParams(needs_layout_passes=False),
        name="moe_plan_sc",
    )(off, idx.reshape(-1), rank.reshape(-1), gate.reshape(-1))
    return rows, gates, seg.reshape(n_tiles * N_EXPERTS, 1, GATE_SEG)


def _moe_tile_kernel(cnt_ref, off_ref, wg_ref, wu_ref, wd_ref, lw_ref, lb_ref, seg_ref, rows_hbm, gates_hbm, x_hbm,
                     base_hbm, o_hbm, x_s, acc_s, xg_a, xg_b, xg_c, y_a, y_b, y_c, stage_s, rows_s, gates_s, sem):
    tile, e = pl.program_id(0), pl.program_id(1)
    rows_of = lambda ref, r, n: ref.at[pl.ds(pl.multiple_of(r * LANE_CHUNKS, LANE_CHUNKS), n * LANE_CHUNKS), :]
    tile_rows = pl.ds(pl.multiple_of(tile * (MOE_TILE * LANE_CHUNKS), LANE_CHUNKS), MOE_TILE * LANE_CHUNKS)

    pair = tile * N_EXPERTS + e
    last_pair = pl.num_programs(0) * N_EXPERTS - 1
    n, off = cnt_ref[pair], off_ref[pair]
    tile_at = lambda ref, r: ref.at[pl.ds(pl.multiple_of(r, LANE_CHUNKS), LANE_CHUNKS), :]

    def gather_group(xg, first, jb):
        at = first + jb * GATHER_GROUP
        rows = [tile_at(x_s, rows_s[at + u])[...] for u in range(GATHER_GROUP)]
        rows_of(xg, jb * GATHER_GROUP, GATHER_GROUP)[...] = jnp.concatenate(rows, axis=0)

    def gather_loop(xg, first):
        def body(jb, cc):
            gather_group(xg, first, jb)
            return cc

        lax.fori_loop(0, MOE_CHUNK // GATHER_GROUP, body, 0)

    def swiglu(xg, y):
        xb = _load_row_tiles(xg, MOE_CHUNK).astype(BF16)
        hg = jnp.dot(xb, wg_ref[...], preferred_element_type=F32)
        hu = jnp.dot(xb, wu_ref[...], preferred_element_type=F32)
        act = (hg * jax.nn.sigmoid(hg) * hu).astype(BF16)
        _store_row_tiles(y, jnp.dot(act, wd_ref[...], preferred_element_type=F32), MOE_CHUNK)

    def scatter_group(y, first, j0, live, gated):
        dsts = [rows_s[first + j0 + u] for u in range(live)]
        yv = rows_of(y, j0, live)[...]
        rows = [yv[u * LANE_CHUNKS:(u + 1) * LANE_CHUNKS] for u in range(live)]
        if not gated:
            rows = [gates_s[first + j0 + u] * r for u, r in enumerate(rows)]
        vals = [tile_at(acc_s, d)[...] + r for d, r in zip(dsts, rows)]
        for d, val in reversed(list(zip(dsts, vals))):
            tile_at(acc_s, d)[...] = val

    def scatter_loop(y, first, m, gated):
        def body(jg, cc):
            scatter_group(y, first, jg * SCATTER_GROUP, SCATTER_GROUP, gated)
            return cc

        lax.fori_loop(0, m // SCATTER_GROUP, body, 0)
        for live in range(1, SCATTER_GROUP):
            @pl.when(m % SCATTER_GROUP == live)
            def _(live=live):
                scatter_group(y, first, m - live, live, gated)

    def plan_loads(t):
        plan = pl.ds(pl.multiple_of(t * LIST_LEN, LIST_PAD), LIST_LEN)
        return (pltpu.make_async_copy(rows_hbm.at[plan], rows_s, sem.at[0]),
                pltpu.make_async_copy(gates_hbm.at[plan], gates_s, sem.at[1]))

    def x_load(t):
        rows = pl.ds(pl.multiple_of(t * (MOE_TILE * LANE_CHUNKS), LANE_CHUNKS), MOE_TILE * LANE_CHUNKS)
        return pltpu.make_async_copy(x_hbm.at[rows, :], x_s, sem.at[2])

    @pl.when(e == 0)
    def _():
        @pl.when(tile == 0)
        def _():
            x_load(tile).start()
            for load in plan_loads(tile):
                load.start()

        load_base = pltpu.make_async_copy(base_hbm.at[tile_rows, :], acc_s, sem.at[3])
        load_base.start()
        y_b[...] = jnp.zeros_like(y_b)
        load_rows, load_gates = plan_loads(tile)
        load_rows.wait()
        x_load(tile).wait()
        gather_loop(xg_a, off)
        load_gates.wait()
        load_base.wait()

    prev_off = off_ref[jnp.maximum(pair - 1, 0)]
    next_off = off_ref[jnp.minimum(pair + 1, last_pair)]
    live_row = lax.broadcasted_iota(jnp.int32, (MOE_CHUNK, 2 * LANES), 0) < n

    def run_expert(xg_cur, y_cur, xg_nxt, y_prv):
        gathers = [functools.partial(gather_group, xg_nxt, next_off, jb) for jb in range(MOE_CHUNK // GATHER_GROUP)]
        scatters = [functools.partial(scatter_group, y_prv, prev_off, jg * SCATTER_GROUP, SCATTER_GROUP, True)
                    for jg in range(MOE_CHUNK // SCATTER_GROUP)]
        side = [s for both in zip(gathers, scatters) for s in both]
        n_down = LANE_CHUNKS // 2
        cost = [D_MODEL, D_MODEL] + [EXPERT_FF] * n_down
        bounds = [round(len(side) * sum(cost[:i]) / sum(cost)) for i in range(len(cost) + 1)]

        def side_work(i):
            for s in side[bounds[i]:bounds[i + 1]]:
                s()

        xb = _load_row_tiles(xg_cur, MOE_CHUNK).astype(BF16)
        side_work(0)
        hg = jnp.dot(xb, wg_ref[...], preferred_element_type=F32)
        side_work(1)
        hu = jnp.dot(xb, wu_ref[...], preferred_element_type=F32)
        gate_col = jnp.broadcast_to(seg_ref[...], (LANES, GATE_SEG)).T[:MOE_CHUNK]
        gate_col = jnp.concatenate([gate_col] * (EXPERT_FF // LANES), axis=1)
        act = (hg * jax.nn.sigmoid(hg) * hu * gate_col).astype(BF16)
        for q in range(n_down):
            side_work(2 + q)
            out = jnp.dot(act, wd_ref[:, q * 2 * LANES:(q + 1) * 2 * LANES], preferred_element_type=F32)
            out = jnp.where(live_row, out, 0.0)
            for c in range(2):
                y_cur[pl.ds(2 * q + c, MOE_CHUNK, stride=LANE_CHUNKS), :] = out[:, c * LANES:(c + 1) * LANES]

    @pl.when(e % 2 == 0)
    def _():
        run_expert(xg_a, y_a, xg_b, y_b)

    @pl.when(e % 2 == 1)
    def _():
        run_expert(xg_b, y_b, xg_a, y_a)

    def extra_chunk(c, carry):
        first = off + c * MOE_CHUNK
        gather_loop(xg_c, first)
        swiglu(xg_c, y_c)
        scatter_loop(y_c, first, jnp.minimum(MOE_CHUNK, n - c * MOE_CHUNK), False)
        return carry

    lax.fori_loop(1, (n + MOE_CHUNK - 1) // MOE_CHUNK, extra_chunk, 0)

    @pl.when(e == N_EXPERTS - 1)
    def _():
        more_tiles = tile + 1 < pl.num_programs(0)

        @pl.when(more_tiles)
        def _():
            x_load(tile + 1).start()

        scatter_loop(y_b, off, jnp.minimum(MOE_CHUNK, n), True)

        @pl.when(more_tiles)
        def _():
            for load in plan_loads(tile + 1):
                load.start()

        n_pieces = MOE_TILE // LN_ROWS
        store = lambda c: pltpu.make_async_copy(
            stage_s.at[c % 2], o_hbm.at[pl.ds(tile * MOE_TILE + c * LN_ROWS, LN_ROWS), :], sem.at[4 + c % 2])
        for c in range(n_pieces):
            z = _load_row_tiles(acc_s, LN_ROWS, c * LN_ROWS * LANE_CHUNKS)
            if c >= 2:
                store(c - 2).wait()
            stage_s[c % 2] = _layer_norm(z, lw_ref[...], lb_ref[...])
            store(c).start()
        store(n_pieces - 2).wait()
        store(n_pieces - 1).wait()


def _moe_tiles(x1, base, plan_rows, plan_gates, gate_seg, cnt, off, wg, wu, wd, ln_w, ln_b):
    T = x1.shape[0] // LANE_CHUNKS
    w_spec = lambda shape: pl.BlockSpec((None,) + shape, lambda i, e, cnt, off: (e, 0, 0))
    vec = pl.BlockSpec((1, D_MODEL), lambda i, e, cnt, off: (0, 0))
    hbm = pl.BlockSpec(memory_space=pl.ANY)
    tile_rows = MOE_TILE * LANE_CHUNKS
    return pl.pallas_call(
        _moe_tile_kernel,
        grid_spec=pltpu.PrefetchScalarGridSpec(
            num_scalar_prefetch=2,
            grid=(T // MOE_TILE, N_EXPERTS),
            in_specs=[w_spec((D_MODEL, EXPERT_FF)), w_spec((D_MODEL, EXPERT_FF)), w_spec((EXPERT_FF, D_MODEL)),
                      vec, vec,
                      pl.BlockSpec((None, 1, GATE_SEG), lambda i, e, cnt, off: (i * N_EXPERTS + e, 0, 0)),
                      hbm, hbm, hbm, hbm],
            out_specs=hbm,
            scratch_shapes=[pltpu.VMEM((tile_rows, LANES), F32),
                            pltpu.VMEM((tile_rows, LANES), F32)]
                           + [pltpu.VMEM((MOE_CHUNK * LANE_CHUNKS, LANES), F32)] * 6
                           + [pltpu.VMEM((2, LN_ROWS, D_MODEL), F32),
                              pltpu.SMEM((LIST_LEN,), jnp.int32),
                              pltpu.SMEM((LIST_LEN,), F32),
                              pltpu.SemaphoreType.DMA((6,))],
        ),
        out_shape=jax.ShapeDtypeStruct((T, D_MODEL), F32),
        compiler_params=_cparams("arbitrary", "arbitrary"),
        name="moe_tiles_ln2",
    )(cnt, off, wg, wu, wd, ln_w, ln_b, gate_seg, plan_rows, plan_gates, x1, base)


def kernel(x, p, w_in, hgrn_lb_logits, hgrn_norm_w, w_branch_att, w_branch_hgrn, w_out, ln1_w, ln1_b, router_w, router_bias, expert_w_gate, expert_w_up, expert_w_down, shared_w_gate, shared_w_up, shared_w_down, ple_gate_w, ple_proj_w, ln2_w, ln2_b):
    B, S, D = x.shape
    T = B * S
    l = 0
    x2d = x.reshape(T, D)
    bf = lambda a: a.astype(BF16)

    att_cols = 3 * len(ATT_DILATIONS) * ATT_WIDTH
    assert att_cols % PROJ_COLS == 0 and (w_in.shape[2] - att_cols) % PROJ_COLS == 0
    whole = lambda w: (w, w.shape[1], 0, 1)
    conv = [[(w_in[l], PROJ_COLS, att_cols // PROJ_COLS, (w_in.shape[2] - att_cols) // PROJ_COLS)],
            [whole(w_branch_att[l]), whole(w_branch_hgrn[l]), whole(w_out[l]), whole(ple_gate_w[l])]]
    qkv = [_proj_att(x2d, w_in[l], g, d, conv[g] if g < len(conv) else None) for g, d in enumerate(ATT_DILATIONS)]
    qkv[0], w_hg = qkv[0]
    qkv[1], (w_a_b, w_h_b, w_o_b, w_pg_b) = qkv[1]
    y_att = _attention(qkv, B, S)
    u_hg, wg_b, wu_b, wd_b = _proj(x2d, w_hg, (expert_w_gate[l], expert_w_up[l], expert_w_down[l]))
    y_hg = _hgrn(u_hg, hgrn_lb_logits, hgrn_norm_w[l:l + 1], B, S)
    x1 = _merge(y_att, y_hg, u_hg, x2d, w_a_b, w_h_b, w_o_b, ln1_w[l:l + 1], ln1_b[l:l + 1])

    base, idx, gate, rank, counts = _route(
        x1, p[l].reshape(T, PLE_DIM), jnp.stack(_split2(router_w[l].T)), router_bias[l].reshape(N_EXPERTS, 1),
        bf(shared_w_gate[l]), bf(shared_w_up[l]), bf(shared_w_down[l]), w_pg_b, bf(ple_proj_w[l]))
    cnt = counts[:, :, 0]
    off = jnp.cumsum(cnt, axis=1) - cnt
    cnt, off = cnt.reshape(-1), off.reshape(-1)
    plan_rows, plan_gates, gate_seg = _plan_sc(off, idx, rank, gate)
    out = _moe_tiles(x1, base, plan_rows, plan_gates, gate_seg, cnt, off, wg_b, wu_b, wd_b,
                     ln2_w[l:l + 1], ln2_b[l:l + 1])
    return out.reshape(B, S, D)
```

```python
import functools

import jax
import jax.numpy as jnp
import numpy as np
from jax import lax
from jax.experimental import pallas as pl
from jax.experimental.pallas import tpu as pltpu
from jax.experimental.pallas import tpu_sc as plsc

F32 = jnp.float32
BF16 = jnp.bfloat16

D_MODEL = 1024
ATT_HEAD_DIM = 64
ATT_HEADS = 8
ATT_DILATIONS = (1, 4, 16)
ATT_BLOCK = 128
ATT_WIDTH = ATT_HEADS * ATT_HEAD_DIM
ATT_TILE = ATT_BLOCK * max(ATT_DILATIONS)
NEG_INF = -1e30
LOG2_E = 1.4426950408889634

HG_HEADS = 8
HG_DIM = 128
HG_WIDTH = HG_HEADS * HG_DIM
HG_CHUNK = 32
HG_TILE = 256
RMS_EPS = 1e-6

N_EXPERTS = 64
TOP_K = 8
N_GROUPS = 8
GROUP_SIZE = N_EXPERTS // N_GROUPS
TOPK_GROUPS = 4
EXPERT_FF = 256
ROUTED_SCALE = 2.5
PLE_DIM = 256
LN_EPS = 1e-5
DEPTH = 1
DEEPNORM_ALPHA = (2.0 * DEPTH) ** 0.25

LANES = 128
LANE_CHUNKS = D_MODEL // LANES
PROJ_ROWS = 512
PROJ_COLS = 1536
ATT_PROJ_ROWS = 1024
STRIDE_STEP = 4
MIX_ROWS = 512
MOE_TILE = 4096
MOE_CHUNK = 576
GATE_SEG = 640
LN_ROWS = 256
LIST_PAD = 1024
LIST_LEN = MOE_TILE * TOP_K + LIST_PAD
GATHER_GROUP = 8
SCATTER_GROUP = 8
V7X_VMEM_LIMIT = 56 * 1024 * 1024
SC_CORES, SC_SUBCORES, SC_LANES = 2, 16, 16


def _cparams(*sem):
    return pltpu.CompilerParams(dimension_semantics=sem, vmem_limit_bytes=V7X_VMEM_LIMIT)


def _proj_att_kernel(*refs, dil, n_conv):
    x_refs, (wq_ref, wk_ref, wv_ref) = refs[:LANE_CHUNKS], refs[LANE_CHUNKS:LANE_CHUNKS + 3]
    conv_in = refs[LANE_CHUNKS + 3:LANE_CHUNKS + 3 + n_conv]
    o_ref = refs[LANE_CHUNKS + 3 + n_conv]
    conv_out = refs[LANE_CHUNKS + 4 + n_conv:LANE_CHUNKS + 4 + 2 * n_conv]
    w_ref = refs[LANE_CHUNKS + 4 + 2 * n_conv]
    stage_refs = refs[LANE_CHUNKS + 5 + 2 * n_conv:]
    n = ATT_PROJ_ROWS // dil
    for src, dst in zip(conv_in, conv_out):
        dst[...] = src[...].astype(BF16)

    @pl.when(pl.program_id(0) == 0)
    def _():
        w_ref[:, :ATT_WIDTH] = (wq_ref[...] * (ATT_HEAD_DIM ** -0.5 * LOG2_E)).astype(BF16)
        w_ref[:, ATT_WIDTH:2 * ATT_WIDTH] = wk_ref[...].astype(BF16)
        w_ref[:, 2 * ATT_WIDTH:] = wv_ref[...].astype(BF16)

    def rows(ref, stage):
        if dil == 1:
            return ref[...]
        if dil <= STRIDE_STEP:
            return jnp.concatenate([ref[pl.ds(r, n, stride=dil), :] for r in range(dil)], axis=0)
        assert dil == STRIDE_STEP * STRIDE_STEP
        quarter = ATT_PROJ_ROWS // STRIDE_STEP
        for r1 in range(STRIDE_STEP):
            stage[pl.ds(r1 * quarter, quarter), :] = ref[pl.ds(r1, quarter, stride=STRIDE_STEP), :]
        return jnp.concatenate(
            [stage[pl.ds((r % STRIDE_STEP) * quarter + r // STRIDE_STEP, n, stride=STRIDE_STEP), :]
             for r in range(dil)], axis=0)

    xp = jnp.concatenate([rows(ref, stage_refs[c] if stage_refs else None).astype(BF16)
                          for c, ref in enumerate(x_refs)], axis=1)
    y = jnp.dot(xp, w_ref[...], preferred_element_type=F32)
    o_ref[...] = y.astype(BF16).reshape(dil, n, 3 * ATT_WIDTH)


def _proj_att(x2d, w_in_l, g, dil, conv=None):
    T = x2d.shape[0]
    per = ATT_TILE // ATT_PROJ_ROWS
    n = ATT_PROJ_ROWS // dil
    steps = T // ATT_PROJ_ROWS
    n_groups = len(ATT_DILATIONS)
    conv_args, conv_in, conv_out, conv_shapes = [], [], [], []
    for mat, width, first, blocks in conv or ():
        slab = mat.shape[0] // steps
        assert slab * steps == mat.shape[0] and slab % 16 == 0
        conv_args += [mat] * blocks
        conv_in += [pl.BlockSpec((slab, width), functools.partial(lambda i, c: (i, c), c=first + c))
                    for c in range(blocks)]
        conv_out += [pl.BlockSpec((slab, width), lambda i: (i, 0))] * blocks
        conv_shapes += [jax.ShapeDtypeStruct((mat.shape[0], width), BF16)] * blocks
    out, *converted = pl.pallas_call(
        functools.partial(_proj_att_kernel, dil=dil, n_conv=len(conv_args)),
        grid=(steps,),
        in_specs=[pl.BlockSpec((ATT_PROJ_ROWS, LANES), functools.partial(lambda i, c: (i, c), c=c))
                  for c in range(LANE_CHUNKS)]
                 + [pl.BlockSpec((D_MODEL, ATT_WIDTH), functools.partial(lambda i, c: (0, c), c=part * n_groups + g))
                    for part in range(3)] + conv_in,
        out_specs=[pl.BlockSpec((None, dil, None, n, 3 * ATT_WIDTH), lambda i: (i // per, 0, i % per, 0, 0))]
                  + conv_out,
        out_shape=[jax.ShapeDtypeStruct((T // ATT_TILE, dil, per, n, 3 * ATT_WIDTH), BF16)] + conv_shapes,
        scratch_shapes=[pltpu.VMEM((D_MODEL, 3 * ATT_WIDTH), BF16)]
                       + ([pltpu.VMEM((ATT_PROJ_ROWS, LANES), F32)] * LANE_CHUNKS if dil > STRIDE_STEP else []),
        compiler_params=_cparams("arbitrary"),
        name=f"proj_att_d{dil}",
    )(*([x2d] * LANE_CHUNKS), *([w_in_l] * 3), *conv_args)
    out = out.reshape(T // ATT_TILE, dil, ATT_TILE // dil, 3 * ATT_WIDTH)
    return (out, converted) if conv is not None else out


def _att_pair(q2, kp, kc, vp, vc, bias_ref, g, first):
    def head0_lanes(rows, dtype):
        lane = lax.broadcasted_iota(jnp.int32, (rows, 2 * ATT_HEAD_DIM), 1)
        return lane.astype(F32).astype(dtype) < ATT_HEAD_DIM

    lo_q = head0_lanes(ATT_BLOCK, BF16)
    lo_v = head0_lanes(2 * ATT_BLOCK, BF16)
    k2 = jnp.concatenate([kp, kc], axis=0)
    v2 = jnp.concatenate([vp, vc], axis=0)
    zero = jnp.zeros_like(q2)
    ps, ms = [], []
    for hh in range(2):
        qm = jnp.where(lo_q, q2, zero) if hh == 0 else jnp.where(lo_q, zero, q2)
        s = lax.dot_general(qm, k2, (((1,), (1,)), ((), ())), preferred_element_type=F32)
        s = s + bias_ref[g, hh, first]
        m = jnp.max(s, axis=-1, keepdims=True)
        ps.append(jnp.exp2(s - m).astype(BF16))
        ms.append(m)
    pcat = jnp.concatenate(ps, axis=1)
    zero_v, one_v = jnp.zeros_like(v2), jnp.ones_like(v2)
    rhs = jnp.concatenate([
        jnp.concatenate([jnp.where(lo_v, v2, zero_v), jnp.where(lo_v, one_v, zero_v)], axis=1),
        jnp.concatenate([jnp.where(lo_v, zero_v, v2), jnp.where(lo_v, zero_v, one_v)], axis=1)], axis=0)
    nd = jnp.dot(pcat, rhs, preferred_element_type=F32)
    m2 = jnp.where(head0_lanes(ATT_BLOCK, F32), ms[0], ms[1])
    return nd[:, :2 * ATT_HEAD_DIM], m2, nd[:, 2 * ATT_HEAD_DIM:]


def _att_kernel(*refs):
    (q0, kc0, vc0, kp0, vp0, q1, kc1, vc1, kp1, vp1, q2, kc2, vc2, kp2, vp2,
     bias_ref, o_ref) = refs[:17]
    ng = len(ATT_DILATIONS)
    num_s, m_s, den_s = refs[17:17 + ng], refs[17 + ng:17 + 2 * ng], refs[17 + 2 * ng:]
    first_tile = (pl.program_id(2) == 0).astype(jnp.int32)
    groups = ((q0, kc0, vc0, kp0, vp0), (q1, kc1, vc1, kp1, vp1), (q2, kc2, vc2, kp2, vp2))
    for g, dil in enumerate(ATT_DILATIONS):
        q_ref, kc_ref, vc_ref, kp_ref, vp_ref = groups[g]
        nb = ATT_TILE // dil // ATT_BLOCK
        for r in range(dil):
            for n in range(nb):
                rows = pl.ds(n * ATT_BLOCK, ATT_BLOCK)
                if n == 0:
                    prev = pl.ds((nb - 1) * ATT_BLOCK, ATT_BLOCK)
                    kp, vp, first = kp_ref[r, prev, :], vp_ref[r, prev, :], first_tile
                else:
                    prev = pl.ds((n - 1) * ATT_BLOCK, ATT_BLOCK)
                    kp, vp, first = kc_ref[r, prev, :], vc_ref[r, prev, :], 0
                num, m, den = _att_pair(q_ref[r, rows, :], kp, kc_ref[r, rows, :], vp, vc_ref[r, rows, :],
                                        bias_ref, g, first)
                if dil == 1:
                    dst = rows
                else:
                    dst = pl.ds(n * ATT_BLOCK * dil + r, ATT_BLOCK, stride=dil)
                num_s[g][dst, :] = num
                m_s[g][dst, :] = m
                den_s[g][dst, :] = den
    m_all = jnp.maximum(jnp.maximum(m_s[0][...], m_s[1][...]), m_s[2][...])
    num = jnp.zeros((ATT_TILE, 2 * ATT_HEAD_DIM), F32)
    den = jnp.zeros((ATT_TILE, 2 * ATT_HEAD_DIM), F32)
    for g in range(ng):
        sc = jnp.exp2(m_s[g][...] - m_all)
        num = num + sc * num_s[g][...]
        den = den + sc * den_s[g][...]
    o_ref[...] = (num / den).astype(o_ref.dtype)


def _att_bias_table():
    qi = np.arange(ATT_BLOCK)[:, None]
    ki = np.arange(2 * ATT_BLOCK)[None, :]
    steps = qi + ATT_BLOCK - ki
    valid = (steps >= 0) & (steps <= ATT_BLOCK)
    slopes = np.array([2.0 ** (-8.0 * (h + 1) / ATT_HEADS) for h in range(ATT_HEADS)], np.float32)
    tab = np.empty((len(ATT_DILATIONS), ATT_HEADS, 2, ATT_BLOCK, 2 * ATT_BLOCK), np.float32)
    for g, dil in enumerate(ATT_DILATIONS):
        bias = -slopes[:, None, None] * (steps * dil).astype(np.float32)[None] * LOG2_E
        tab[g, :, 0] = np.where(valid[None], bias, NEG_INF)
        tab[g, :, 1] = np.where((valid & (ki >= ATT_BLOCK))[None], bias, NEG_INF)
    return jnp.asarray(tab)


def _attention(qkv, B, S):
    tiles = S // ATT_TILE
    pair = 2 * ATT_HEAD_DIM
    npair = ATT_WIDTH // pair
    in_specs, args = [], []
    for g, dil in enumerate(ATT_DILATIONS):
        blk = (None, dil, ATT_TILE // dil, pair)
        cur = lambda b, hp, t, off: (b * tiles + t, 0, 0, off * npair + hp)
        prv = lambda b, hp, t, off: (b * tiles + jnp.maximum(t - 1, 0), 0, 0, off * npair + hp)
        in_specs += [pl.BlockSpec(blk, functools.partial(cur, off=0)),
                     pl.BlockSpec(blk, functools.partial(cur, off=1)),
                     pl.BlockSpec(blk, functools.partial(cur, off=2)),
                     pl.BlockSpec(blk, functools.partial(prv, off=1)),
                     pl.BlockSpec(blk, functools.partial(prv, off=2))]
        args += [qkv[g]] * 5
    in_specs.append(pl.BlockSpec((len(ATT_DILATIONS), 2, 2, ATT_BLOCK, 2 * ATT_BLOCK),
                                 lambda b, hp, t: (0, hp, 0, 0, 0)))
    args.append(_att_bias_table())
    scratch = [pltpu.VMEM((ATT_TILE, pair), F32) for _ in range(3 * len(ATT_DILATIONS))]
    return pl.pallas_call(
        _att_kernel,
        grid=(B, npair, tiles),
        in_specs=in_specs,
        out_specs=pl.BlockSpec((ATT_TILE, pair), lambda b, hp, t: (b * tiles + t, hp)),
        out_shape=jax.ShapeDtypeStruct((B * S, ATT_WIDTH), BF16),
        scratch_shapes=scratch,
        compiler_params=_cparams("parallel", "parallel", "arbitrary"),
        name="dilated_attention",
    )(*args)


def _proj_kernel(x_ref, *refs, n_w):
    w_refs = refs[:n_w]
    eg_ref, eu_ref, ed_ref, o_ref, egb_ref, eub_ref, edb_ref = refs[n_w:]
    egb_ref[...] = eg_ref[...].astype(BF16)
    eub_ref[...] = eu_ref[...].astype(BF16)
    edb_ref[...] = ed_ref[...].astype(BF16)
    xb = x_ref[...].astype(BF16)
    col_tile = w_refs[0].shape[1]
    for c, w_ref in enumerate(w_refs):
        cols = slice(c * col_tile, (c + 1) * col_tile)
        o_ref[:, cols] = jnp.dot(xb, w_ref[...], preferred_element_type=F32).astype(o_ref.dtype)


def _proj(x2d, ws, expert_weights):
    T, N = x2d.shape[0], len(ws) * ws[0].shape[1]
    steps = T // PROJ_ROWS
    n_experts = expert_weights[0].shape[0]
    per_step = max(1, n_experts // steps)
    assert n_experts % per_step == 0 and n_experts // per_step <= steps
    per_expert = lambda ew: pl.BlockSpec(
        (per_step,) + ew.shape[1:], lambda i: (jnp.minimum(i, n_experts // per_step - 1), 0, 0))
    return pl.pallas_call(
        functools.partial(_proj_kernel, n_w=len(ws)),
        grid=(steps,),
        in_specs=[pl.BlockSpec((PROJ_ROWS, D_MODEL), lambda i: (i, 0))]
                 + [pl.BlockSpec(w.shape, lambda i: (0, 0)) for w in ws] + [per_expert(ew) for ew in expert_weights],
        out_specs=[pl.BlockSpec((PROJ_ROWS, N), lambda i: (i, 0))] + [per_expert(ew) for ew in expert_weights],
        out_shape=[jax.ShapeDtypeStruct((T, N), BF16)]
                  + [jax.ShapeDtypeStruct(ew.shape, BF16) for ew in expert_weights],
        compiler_params=_cparams("parallel"),
        name="proj_hgrn_gates",
    )(x2d, *ws, *expert_weights)


def _split2(v):
    a = v.astype(BF16)
    return a, (v - a.astype(F32)).astype(BF16)


def _hgrn_kernel(q_ref, f_ref, i_ref, g_ref, lbl_ref, gain_ref, o_ref, state_ref):
    @pl.when(pl.program_id(1) == 0)
    def _():
        state_ref[...] = jnp.zeros_like(state_ref)

    lbl = lbl_ref[...]
    e = jnp.exp(lbl - jnp.max(lbl, axis=0, keepdims=True))
    lb = e[0:1] / jnp.sum(e, axis=0, keepdims=True)
    forget = lb + (1.0 - lb) * jax.nn.sigmoid(f_ref[...].astype(F32))
    log_f = jnp.log(forget)
    key = 1.0 - forget

    row = lax.broadcasted_iota(jnp.int32, (HG_TILE, HG_TILE), 0)
    col = lax.broadcasted_iota(jnp.int32, (HG_TILE, HG_TILE), 1)
    causal = (row >= col) & ((row // HG_CHUNK) == (col // HG_CHUNK))
    tri = jnp.where(causal, 1.0, 0.0).astype(BF16)
    b = sum(jnp.dot(tri, t, preferred_element_type=F32) for t in _split2(log_f))
    eb = jnp.exp(b)
    q_dec = (q_ref[...].astype(F32) * eb).astype(BF16)
    k_inv = key * jnp.exp(-b)
    xi = i_ref[...].astype(F32)
    val = (xi * jax.nn.sigmoid(xi)).astype(BF16)
    k_inv_b = k_inv.astype(BF16)

    n_chunks = HG_TILE // HG_CHUNK
    last_rows = [eb[(c + 1) * HG_CHUNK - 1:(c + 1) * HG_CHUNK, :] for c in range(n_chunks)]
    dec_rows = jnp.concatenate([jnp.broadcast_to(r, (HG_CHUNK, HG_WIDTH)) for r in last_rows], axis=0)
    k_end = (k_inv * dec_rows).astype(BF16)
    def per_chunk_columns(t):
        blocks = []
        for c in range(n_chunks):
            rows_above, rows_below = c * HG_CHUNK, HG_TILE - (c + 1) * HG_CHUNK
            parts = [t[rows_above:rows_above + HG_CHUNK]]
            if rows_above:
                parts.insert(0, jnp.zeros((rows_above, HG_DIM), t.dtype))
            if rows_below:
                parts.append(jnp.zeros((rows_below, HG_DIM), t.dtype))
            blocks.append(jnp.concatenate(parts, axis=0))
        return jnp.concatenate(blocks, axis=1)

    head_cols = [slice(h * HG_DIM, (h + 1) * HG_DIM) for h in range(HG_HEADS)]
    upds = [lax.dot_general(val[:, cols], per_chunk_columns(k_end[:, cols]), (((0,), (0,)), ((), ())),
                            preferred_element_type=F32) for cols in head_cols]
    o_intras = []
    for cols in head_cols:
        a = lax.dot_general(q_dec[:, cols], k_inv_b[:, cols], (((1,), (1,)), ((), ())), preferred_element_type=F32)
        a = jnp.where(causal, a, 0.0).astype(BF16)
        o_intras.append(jnp.dot(a, val[:, cols], preferred_element_type=F32))
    enterings = []
    for h, cols in enumerate(head_cols):
        st = state_ref[h]
        entering = []
        for c in range(n_chunks):
            entering.append(st.astype(BF16))
            st = st * last_rows[c][:, cols] + upds[h][:, c * HG_DIM:(c + 1) * HG_DIM]
        state_ref[h] = st
        enterings.append(jnp.concatenate(entering, axis=1))
    outs = []
    for h, cols in enumerate(head_cols):
        qd = q_dec[:, cols]
        o_inter = lax.dot_general(per_chunk_columns(qd), enterings[h],
                                  (((1,), (1,)), ((), ())), preferred_element_type=F32)
        o = o_intras[h] + o_inter
        o = o * lax.rsqrt(jnp.mean(jnp.square(o), axis=-1, keepdims=True) + RMS_EPS)
        outs.append(o)
    o = jnp.concatenate(outs, axis=1) * gain_ref[...]
    gg = g_ref[...].astype(F32)
    o_ref[...] = (o * (gg * jax.nn.sigmoid(gg))).astype(o_ref.dtype)


def _hgrn(u_hg, lb_logits, gain, B, S):
    tiles = S // HG_TILE
    col = lambda j: pl.BlockSpec((HG_TILE, HG_WIDTH), functools.partial(lambda b, t, j: (b * tiles + t, j), j=j))
    return pl.pallas_call(
        _hgrn_kernel,
        grid=(B, tiles),
        in_specs=[col(0), col(1), col(2), col(3),
                  pl.BlockSpec((2, HG_WIDTH), lambda b, t: (0, 0)),
                  pl.BlockSpec((1, HG_WIDTH), lambda b, t: (0, 0))],
        out_specs=pl.BlockSpec((HG_TILE, HG_WIDTH), lambda b, t: (b * tiles + t, 0)),
        out_shape=jax.ShapeDtypeStruct((B * S, HG_WIDTH), BF16),
        scratch_shapes=[pltpu.VMEM((HG_HEADS, HG_DIM, HG_DIM), F32)],
        compiler_params=_cparams("parallel", "arbitrary"),
        name="hgrn2",
    )(u_hg, u_hg, u_hg, u_hg, lb_logits, gain)


def _load_row_tiles(ref, n, start=0):
    return jnp.concatenate([ref[pl.ds(start + c, n, stride=LANE_CHUNKS), :] for c in range(LANE_CHUNKS)], axis=1)


def _store_row_tiles(ref, val, n):
    for c in range(LANE_CHUNKS):
        ref[pl.ds(c, n, stride=LANE_CHUNKS), :] = val[:, c * LANES:(c + 1) * LANES]


def _layer_norm(z, w, b):
    mu = jnp.mean(z, axis=-1, keepdims=True)
    zc = z - mu
    var = jnp.mean(jnp.square(zc), axis=-1, keepdims=True)
    return zc * lax.rsqrt(var + LN_EPS) * w + b


def _merge_kernel(ya_ref, yh_ref, ga_ref, gh_ref, x_ref, wa_ref, wh_ref, wo_ref, lw_ref, lb_ref, o_ref):
    ma = jnp.dot(ya_ref[...], wa_ref[...], preferred_element_type=F32)
    mh = jnp.dot(yh_ref[...], wh_ref[...], preferred_element_type=F32)
    merged = (jax.nn.sigmoid(ga_ref[...].astype(F32)) * ma + jax.nn.sigmoid(gh_ref[...].astype(F32)) * mh)
    z = DEEPNORM_ALPHA * x_ref[...] + jnp.dot(merged.astype(BF16), wo_ref[...], preferred_element_type=F32)
    _store_row_tiles(o_ref, _layer_norm(z, lw_ref[...], lb_ref[...]), MIX_ROWS)


def _merge(y_att, y_hg, u_hg, x2d, w_a, w_h, w_o, ln_w, ln_b):
    T = x2d.shape[0]
    rows = lambda width, j=0: pl.BlockSpec((MIX_ROWS, width), functools.partial(lambda i, j: (i, j), j=j))
    full = lambda a: pl.BlockSpec(a.shape, lambda i: (0, 0))
    return pl.pallas_call(
        _merge_kernel,
        grid=(T // MIX_ROWS,),
        in_specs=[rows(ATT_WIDTH), rows(HG_WIDTH), rows(D_MODEL, 4), rows(D_MODEL, 5), rows(D_MODEL),
                  full(w_a), full(w_h), full(w_o), full(ln_w), full(ln_b)],
        out_specs=pl.BlockSpec((MIX_ROWS * LANE_CHUNKS, LANES), lambda i: (i, 0)),
        out_shape=jax.ShapeDtypeStruct((T * LANE_CHUNKS, LANES), F32),
        compiler_params=_cparams("parallel"),
        name="merge_ln1",
    )(y_att, y_hg, u_hg, u_hg, x2d, w_a, w_h, w_o, ln_w, ln_b)


def _first_argmax(v, ids, n):
    mx = jnp.max(v, axis=0, keepdims=True)
    return mx, jnp.min(jnp.where(v == mx, ids, n), axis=0, keepdims=True)


def _dense_kernel(x1_ref, p_ref, wsg_ref, wsu_ref, wsd_ref, wpg_ref, wpp_ref, base_ref):
    x1 = _load_row_tiles(x1_ref, MIX_ROWS)
    x1b = x1.astype(BF16)
    hg = jnp.dot(x1b, wsg_ref[...], preferred_element_type=F32)
    hu = jnp.dot(x1b, wsu_ref[...], preferred_element_type=F32)
    ple_gate = jnp.dot(x1b, wpg_ref[...], preferred_element_type=F32)
    ple_proj = jnp.dot(p_ref[...].astype(BF16), wpp_ref[...], preferred_element_type=F32)
    shared = jnp.dot((hg * jax.nn.sigmoid(hg) * hu).astype(BF16), wsd_ref[...], preferred_element_type=F32)
    ple = jax.nn.sigmoid(ple_gate) * ple_proj
    _store_row_tiles(base_ref, DEEPNORM_ALPHA * x1 + shared + ple, MIX_ROWS)


def _dense(x1, p2d, wsg, wsu, wsd, wpg, wpp):
    T = x1.shape[0] // LANE_CHUNKS
    full = lambda a: pl.BlockSpec(a.shape, lambda i: (0,) * a.ndim)
    row_tiles = pl.BlockSpec((MIX_ROWS * LANE_CHUNKS, LANES), lambda i: (i, 0))
    return pl.pallas_call(
        _dense_kernel,
        grid=(T // MIX_ROWS,),
        in_specs=[row_tiles, pl.BlockSpec((MIX_ROWS, PLE_DIM), lambda i: (i, 0)),
                  full(wsg), full(wsu), full(wsd), full(wpg), full(wpp)],
        out_specs=row_tiles,
        out_shape=jax.ShapeDtypeStruct((T * LANE_CHUNKS, LANES), F32),
        compiler_params=_cparams("parallel"),
        name="shared_ple",
    )(x1, p2d, wsg, wsu, wsd, wpg, wpp)


def _route_kernel(x1_ref, wrt_ref, rb_ref, idx_ref, gate_ref, rank_ref, cnt_ref, carry_ref):
    @pl.when(pl.program_id(0) % (MOE_TILE // MIX_ROWS) == 0)
    def _():
        carry_ref[...] = jnp.zeros_like(carry_ref)

    x1 = _load_row_tiles(x1_ref, MIX_ROWS)
    x1b = x1.astype(BF16)
    x1_lo = (x1 - x1b.astype(F32)).astype(BF16)
    nt = (((1,), (1,)), ((), ()))
    logits = (lax.dot_general(wrt_ref[0], x1b, nt, preferred_element_type=F32)
              + lax.dot_general(wrt_ref[1], x1b, nt, preferred_element_type=F32)
              + lax.dot_general(wrt_ref[0], x1_lo, nt, preferred_element_type=F32))

    s = jax.nn.sigmoid(logits)
    sel = s + rb_ref[...]
    eid = lax.broadcasted_iota(jnp.int32, (N_EXPERTS, MIX_ROWS), 0)
    neg = -jnp.inf

    grp = sel.reshape(N_GROUPS, GROUP_SIZE, MIX_ROWS)
    mid = lax.broadcasted_iota(jnp.int32, grp.shape, 1)
    m1 = jnp.max(grp, axis=1, keepdims=True)
    i1 = jnp.min(jnp.where(grp == m1, mid, GROUP_SIZE), axis=1, keepdims=True)
    m2 = jnp.max(jnp.where(mid == i1, neg, grp), axis=1, keepdims=True)
    gscore = (m1 + m2).reshape(N_GROUPS, MIX_ROWS)
    gid = lax.broadcasted_iota(jnp.int32, (N_GROUPS, MIX_ROWS), 0)
    gsel = jnp.zeros((N_GROUPS, MIX_ROWS), jnp.bool_)
    for _ in range(TOPK_GROUPS):
        _, gi = _first_argmax(gscore, gid, N_GROUPS)
        hit = gid == gi
        gsel = gsel | hit
        gscore = jnp.where(hit, neg, gscore)
    emask = jnp.broadcast_to(gsel.reshape(N_GROUPS, 1, MIX_ROWS), grp.shape).reshape(N_EXPERTS, MIX_ROWS)
    cand = jnp.where(emask, sel, neg)

    idxs, gates = [], []
    chosen = jnp.zeros((N_EXPERTS, MIX_ROWS), jnp.bool_)
    for _ in range(TOP_K):
        _, ei = _first_argmax(cand, eid, N_EXPERTS)
        hit = eid == ei
        idxs.append(ei)
        gates.append(jnp.sum(jnp.where(hit, s, 0.0), axis=0, keepdims=True))
        chosen = chosen | hit
        cand = jnp.where(hit, neg, cand)
    g = jnp.concatenate(gates, axis=0)
    g = g / jnp.sum(g, axis=0, keepdims=True) * ROUTED_SCALE
    idx_ref[...] = jnp.concatenate(idxs, axis=0)
    gate_ref[...] = g

    onehot = jnp.where(chosen, 1.0, 0.0)
    tr = lax.broadcasted_iota(jnp.int32, (MIX_ROWS, MIX_ROWS), 0)
    tc = lax.broadcasted_iota(jnp.int32, (MIX_ROWS, MIX_ROWS), 1)
    before = jnp.where(tr < tc, 1.0, 0.0).astype(BF16)
    prefix = jnp.dot(onehot.astype(BF16), before, preferred_element_type=F32)
    rankfull = (carry_ref[:, 0:1] + prefix).astype(jnp.int32)
    rank_ref[...] = jnp.concatenate(
        [jnp.sum(jnp.where(eid == ei, rankfull, 0), axis=0, keepdims=True) for ei in idxs], axis=0)
    total = carry_ref[...] + jnp.sum(onehot, axis=1, keepdims=True)
    carry_ref[...] = total
    cnt_ref[...] = total.astype(jnp.int32)


def _route(x1, wr_t, rbias):
    T = x1.shape[0] // LANE_CHUNKS
    per_tile = MOE_TILE // MIX_ROWS
    full = lambda a: pl.BlockSpec(a.shape, lambda i: (0,) * a.ndim)
    tok = pl.BlockSpec((TOP_K, MIX_ROWS), lambda i: (0, i))
    row_tiles = pl.BlockSpec((MIX_ROWS * LANE_CHUNKS, LANES), lambda i: (i, 0))
    return pl.pallas_call(
        _route_kernel,
        grid=(T // MIX_ROWS,),
        in_specs=[row_tiles, full(wr_t), full(rbias)],
        out_specs=[tok, tok, tok,
                   pl.BlockSpec((None, N_EXPERTS, LANES), lambda i: (i // per_tile, 0, 0))],
        out_shape=[jax.ShapeDtypeStruct((TOP_K, T), jnp.int32),
                   jax.ShapeDtypeStruct((TOP_K, T), F32),
                   jax.ShapeDtypeStruct((TOP_K, T), jnp.int32),
                   jax.ShapeDtypeStruct((T // MOE_TILE, N_EXPERTS, LANES), jnp.int32)],
        scratch_shapes=[pltpu.VMEM((N_EXPERTS, LANES), F32)],
        compiler_params=_cparams("arbitrary"),
        name="route",
    )(x1, wr_t, rbias)


def _plan_sc_kernel(off_hbm, idx_hbm, rank_hbm, gate_hbm, rows_hbm, gates_hbm, seg_hbm,
                    off_v, idx_v, rank_v, gate_v, rows_v, gates_v, seg_v):
    n_tokens = idx_hbm.shape[0] // TOP_K
    worker = lax.axis_index("subcore") * SC_CORES + lax.axis_index("core")

    @pl.when(worker < n_tokens // MOE_TILE)
    def _():
        pltpu.sync_copy(off_hbm.at[pl.ds(worker * N_EXPERTS, N_EXPERTS)], off_v)
        lane = lax.iota(jnp.int32, SC_LANES)

        @pl.loop(0, N_EXPERTS * GATE_SEG // SC_LANES)
        def _(i):
            seg_v[pl.ds(i * SC_LANES, SC_LANES)] = jnp.zeros((SC_LANES,), F32)

        for k in range(TOP_K):
            row = pl.ds(k * n_tokens + worker * MOE_TILE, MOE_TILE)
            pltpu.sync_copy(idx_hbm.at[row], idx_v)
            pltpu.sync_copy(rank_hbm.at[row], rank_v)
            pltpu.sync_copy(gate_hbm.at[row], gate_v)

            @pl.loop(0, MOE_TILE // SC_LANES)
            def _(i):
                at = i * SC_LANES
                expert, rank, gate = (v[pl.ds(at, SC_LANES)] for v in (idx_v, rank_v, gate_v))
                pos = plsc.load_gather(off_v, [expert]) + rank
                plsc.store_scatter(rows_v, [pos], (lane + at) * LANE_CHUNKS)
                plsc.store_scatter(gates_v, [pos], gate)
                plsc.store_scatter(seg_v, [expert * GATE_SEG + rank], gate, mask=rank < GATE_SEG)

        @pl.loop(0, LIST_PAD // SC_LANES)
        def _(i):
            tail = pl.ds(MOE_TILE * TOP_K + i * SC_LANES, SC_LANES)
            rows_v[tail] = jnp.zeros((SC_LANES,), jnp.int32)
            gates_v[tail] = jnp.zeros((SC_LANES,), F32)

        out = pl.ds(worker * LIST_LEN, LIST_LEN)
        pltpu.sync_copy(rows_v, rows_hbm.at[out])
        pltpu.sync_copy(gates_v, gates_hbm.at[out])
        pltpu.sync_copy(seg_v, seg_hbm.at[pl.ds(worker * (N_EXPERTS * GATE_SEG), N_EXPERTS * GATE_SEG)])


def _plan_sc(off, idx, rank, gate):
    n_tiles = idx.shape[1] // MOE_TILE
    assert n_tiles <= SC_CORES * SC_SUBCORES
    mesh = plsc.VectorSubcoreMesh(core_axis_name="core", subcore_axis_name="subcore",
                                  num_cores=SC_CORES, num_subcores=SC_SUBCORES)
    rows, gates, seg = pl.kernel(
        _plan_sc_kernel,
        out_type=(jax.ShapeDtypeStruct((n_tiles * LIST_LEN,), jnp.int32),
                  jax.ShapeDtypeStruct((n_tiles * LIST_LEN,), F32),
                  jax.ShapeDtypeStruct((n_tiles * N_EXPERTS * GATE_SEG,), F32)),
        mesh=mesh,
        scratch_types=[pltpu.VMEM((N_EXPERTS,), jnp.int32), pltpu.VMEM((MOE_TILE,), jnp.int32),
                       pltpu.VMEM((MOE_TILE,), jnp.int32), pltpu.VMEM((MOE_TILE,), F32),
                       pltpu.VMEM((LIST_LEN,), jnp.int32), pltpu.VMEM((LIST_LEN,), F32),
                       pltpu.VMEM((N_EXPERTS * GATE_SEG,), F32)],
        compiler_params=pltpu.CompilerParams(needs_layout_passes=False),
        name="moe_plan_sc",
    )(off, idx.reshape(-1), rank.reshape(-1), gate.reshape(-1))
    return rows, gates, seg.reshape(n_tiles * N_EXPERTS, 1, GATE_SEG)


def _moe_tile_kernel(cnt_ref, off_ref, wg_ref, wu_ref, wd_ref, lw_ref, lb_ref, seg_ref, rows_hbm, gates_hbm, x_hbm,
                     base_hbm, o_hbm, x_s, acc_s, xg_a, xg_b, xg_c, y_a, y_b, y_c, stage_s, rows_s, gates_s, sem):
    tile, e = pl.program_id(0), pl.program_id(1)
    rows_of = lambda ref, r, n: ref.at[pl.ds(pl.multiple_of(r * LANE_CHUNKS, LANE_CHUNKS), n * LANE_CHUNKS), :]
    tile_rows = pl.ds(pl.multiple_of(tile * (MOE_TILE * LANE_CHUNKS), LANE_CHUNKS), MOE_TILE * LANE_CHUNKS)

    pair = tile * N_EXPERTS + e
    last_pair = pl.num_programs(0) * N_EXPERTS - 1
    n, off = cnt_ref[pair], off_ref[pair]
    tile_at = lambda ref, r: ref.at[pl.ds(pl.multiple_of(r, LANE_CHUNKS), LANE_CHUNKS), :]

    def gather_group(xg, first, jb):
        at = first + jb * GATHER_GROUP
        rows = [tile_at(x_s, rows_s[at + u])[...] for u in range(GATHER_GROUP)]
        rows_of(xg, jb * GATHER_GROUP, GATHER_GROUP)[...] = jnp.concatenate(rows, axis=0)

    def gather_loop(xg, first):
        def body(jb, cc):
            gather_group(xg, first, jb)
            return cc

        lax.fori_loop(0, MOE_CHUNK // GATHER_GROUP, body, 0)

    def swiglu(xg, y):
        xb = _load_row_tiles(xg, MOE_CHUNK).astype(BF16)
        hg = jnp.dot(xb, wg_ref[...], preferred_element_type=F32)
        hu = jnp.dot(xb, wu_ref[...], preferred_element_type=F32)
        act = (hg * jax.nn.sigmoid(hg) * hu).astype(BF16)
        _store_row_tiles(y, jnp.dot(act, wd_ref[...], preferred_element_type=F32), MOE_CHUNK)

    def scatter_group(y, first, j0, live, gated):
        dsts = [rows_s[first + j0 + u] for u in range(live)]
        yv = rows_of(y, j0, live)[...]
        rows = [yv[u * LANE_CHUNKS:(u + 1) * LANE_CHUNKS] for u in range(live)]
        if not gated:
            rows = [gates_s[first + j0 + u] * r for u, r in enumerate(rows)]
        vals = [tile_at(acc_s, d)[...] + r for d, r in zip(dsts, rows)]
        for d, val in reversed(list(zip(dsts, vals))):
            tile_at(acc_s, d)[...] = val

    def scatter_loop(y, first, m, gated):
        def body(jg, cc):
            scatter_group(y, first, jg * SCATTER_GROUP, SCATTER_GROUP, gated)
            return cc

        lax.fori_loop(0, m // SCATTER_GROUP, body, 0)
        for live in range(1, SCATTER_GROUP):
            @pl.when(m % SCATTER_GROUP == live)
            def _(live=live):
                scatter_group(y, first, m - live, live, gated)

    def plan_loads(t):
        plan = pl.ds(pl.multiple_of(t * LIST_LEN, LIST_PAD), LIST_LEN)
        return (pltpu.make_async_copy(rows_hbm.at[plan], rows_s, sem.at[0]),
                pltpu.make_async_copy(gates_hbm.at[plan], gates_s, sem.at[1]))

    def x_load(t):
        rows = pl.ds(pl.multiple_of(t * (MOE_TILE * LANE_CHUNKS), LANE_CHUNKS), MOE_TILE * LANE_CHUNKS)
        return pltpu.make_async_copy(x_hbm.at[rows, :], x_s, sem.at[2])

    @pl.when(e == 0)
    def _():
        @pl.when(tile == 0)
        def _():
            x_load(tile).start()
            for load in plan_loads(tile):
                load.start()

        load_base = pltpu.make_async_copy(base_hbm.at[tile_rows, :], acc_s, sem.at[3])
        load_base.start()
        y_b[...] = jnp.zeros_like(y_b)
        load_rows, load_gates = plan_loads(tile)
        load_rows.wait()
        x_load(tile).wait()
        gather_loop(xg_a, off)
        load_gates.wait()
        load_base.wait()

    prev_off = off_ref[jnp.maximum(pair - 1, 0)]
    next_off = off_ref[jnp.minimum(pair + 1, last_pair)]
    live_row = lax.broadcasted_iota(jnp.int32, (MOE_CHUNK, 2 * LANES), 0) < n

    def run_expert(xg_cur, y_cur, xg_nxt, y_prv):
        gathers = [functools.partial(gather_group, xg_nxt, next_off, jb) for jb in range(MOE_CHUNK // GATHER_GROUP)]
        scatters = [functools.partial(scatter_group, y_prv, prev_off, jg * SCATTER_GROUP, SCATTER_GROUP, True)
                    for jg in range(MOE_CHUNK // SCATTER_GROUP)]
        side = [s for both in zip(gathers, scatters) for s in both]
        n_down = LANE_CHUNKS // 2
        cost = [D_MODEL, D_MODEL] + [EXPERT_FF] * n_down
        bounds = [round(len(side) * sum(cost[:i]) / sum(cost)) for i in range(len(cost) + 1)]

        def side_work(i):
            for s in side[bounds[i]:bounds[i + 1]]:
                s()

        xb = _load_row_tiles(xg_cur, MOE_CHUNK).astype(BF16)
        side_work(0)
        hg = jnp.dot(xb, wg_ref[...], preferred_element_type=F32)
        side_work(1)
        hu = jnp.dot(xb, wu_ref[...], preferred_element_type=F32)
        gate_col = jnp.broadcast_to(seg_ref[...], (LANES, GATE_SEG)).T[:MOE_CHUNK]
        gate_col = jnp.concatenate([gate_col] * (EXPERT_FF // LANES), axis=1)
        act = (hg * jax.nn.sigmoid(hg) * hu * gate_col).astype(BF16)
        for q in range(n_down):
            side_work(2 + q)
            out = jnp.dot(act, wd_ref[:, q * 2 * LANES:(q + 1) * 2 * LANES], preferred_element_type=F32)
            out = jnp.where(live_row, out, 0.0)
            for c in range(2):
                y_cur[pl.ds(2 * q + c, MOE_CHUNK, stride=LANE_CHUNKS), :] = out[:, c * LANES:(c + 1) * LANES]

    @pl.when(e % 2 == 0)
    def _():
        run_expert(xg_a, y_a, xg_b, y_b)

    @pl.when(e % 2 == 1)
    def _():
        run_expert(xg_b, y_b, xg_a, y_a)

    def extra_chunk(c, carry):
        first = off + c * MOE_CHUNK
        gather_loop(xg_c, first)
        swiglu(xg_c, y_c)
        scatter_loop(y_c, first, jnp.minimum(MOE_CHUNK, n - c * MOE_CHUNK), False)
        return carry

    lax.fori_loop(1, (n + MOE_CHUNK - 1) // MOE_CHUNK, extra_chunk, 0)

    @pl.when(e == N_EXPERTS - 1)
    def _():
        more_tiles = tile + 1 < pl.num_programs(0)

        @pl.when(more_tiles)
        def _():
            x_load(tile + 1).start()

        scatter_loop(y_b, off, jnp.minimum(MOE_CHUNK, n), True)

        @pl.when(more_tiles)
        def _():
            for load in plan_loads(tile + 1):
                load.start()

        n_pieces = MOE_TILE // LN_ROWS
        store = lambda c: pltpu.make_async_copy(
            stage_s.at[c % 2], o_hbm.at[pl.ds(tile * MOE_TILE + c * LN_ROWS, LN_ROWS), :], sem.at[4 + c % 2])
        for c in range(n_pieces):
            z = _load_row_tiles(acc_s, LN_ROWS, c * LN_ROWS * LANE_CHUNKS)
            if c >= 2:
                store(c - 2).wait()
            stage_s[c % 2] = _layer_norm(z, lw_ref[...], lb_ref[...])
            store(c).start()
        store(n_pieces - 2).wait()
        store(n_pieces - 1).wait()


def _moe_tiles(x1, base, plan_rows, plan_gates, gate_seg, cnt, off, wg, wu, wd, ln_w, ln_b):
    T = x1.shape[0] // LANE_CHUNKS
    w_spec = lambda shape: pl.BlockSpec((None,) + shape, lambda i, e, cnt, off: (e, 0, 0))
    vec = pl.BlockSpec((1, D_MODEL), lambda i, e, cnt, off: (0, 0))
    hbm = pl.BlockSpec(memory_space=pl.ANY)
    tile_rows = MOE_TILE * LANE_CHUNKS
    return pl.pallas_call(
        _moe_tile_kernel,
        grid_spec=pltpu.PrefetchScalarGridSpec(
            num_scalar_prefetch=2,
            grid=(T // MOE_TILE, N_EXPERTS),
            in_specs=[w_spec((D_MODEL, EXPERT_FF)), w_spec((D_MODEL, EXPERT_FF)), w_spec((EXPERT_FF, D_MODEL)),
                      vec, vec,
                      pl.BlockSpec((None, 1, GATE_SEG), lambda i, e, cnt, off: (i * N_EXPERTS + e, 0, 0)),
                      hbm, hbm, hbm, hbm],
            out_specs=hbm,
            scratch_shapes=[pltpu.VMEM((tile_rows, LANES), F32),
                            pltpu.VMEM((tile_rows, LANES), F32)]
                           + [pltpu.VMEM((MOE_CHUNK * LANE_CHUNKS, LANES), F32)] * 6
                           + [pltpu.VMEM((2, LN_ROWS, D_MODEL), F32),
                              pltpu.SMEM((LIST_LEN,), jnp.int32),
                              pltpu.SMEM((LIST_LEN,), F32),
                              pltpu.SemaphoreType.DMA((6,))],
        ),
        out_shape=jax.ShapeDtypeStruct((T, D_MODEL), F32),
        compiler_params=_cparams("arbitrary", "arbitrary"),
        name="moe_tiles_ln2",
    )(cnt, off, wg, wu, wd, ln_w, ln_b, gate_seg, plan_rows, plan_gates, x1, base)


def kernel(x, p, w_in, hgrn_lb_logits, hgrn_norm_w, w_branch_att, w_branch_hgrn, w_out, ln1_w, ln1_b, router_w, router_bias, expert_w_gate, expert_w_up, expert_w_down, shared_w_gate, shared_w_up, shared_w_down, ple_gate_w, ple_proj_w, ln2_w, ln2_b):
    B, S, D = x.shape
    T = B * S
    l = 0
    x2d = x.reshape(T, D)
    bf = lambda a: a.astype(BF16)

    att_cols = 3 * len(ATT_DILATIONS) * ATT_WIDTH
    assert att_cols % PROJ_COLS == 0 and (w_in.shape[2] - att_cols) % PROJ_COLS == 0
    whole = lambda w: (w, w.shape[1], 0, 1)
    conv = [[(w_in[l], PROJ_COLS, att_cols // PROJ_COLS, (w_in.shape[2] - att_cols) // PROJ_COLS)],
            [whole(w_branch_att[l]), whole(w_branch_hgrn[l]), whole(w_out[l]), whole(ple_gate_w[l])]]
    qkv = [_proj_att(x2d, w_in[l], g, d, conv[g] if g < len(conv) else None) for g, d in enumerate(ATT_DILATIONS)]
    qkv[0], w_hg = qkv[0]
    qkv[1], (w_a_b, w_h_b, w_o_b, w_pg_b) = qkv[1]
    y_att = _attention(qkv, B, S)
    u_hg, wg_b, wu_b, wd_b = _proj(x2d, w_hg, (expert_w_gate[l], expert_w_up[l], expert_w_down[l]))
    y_hg = _hgrn(u_hg, hgrn_lb_logits, hgrn_norm_w[l:l + 1], B, S)
    x1 = _merge(y_att, y_hg, u_hg, x2d, w_a_b, w_h_b, w_o_b, ln1_w[l:l + 1], ln1_b[l:l + 1])

    idx, gate, rank, counts = _route(x1, jnp.stack(_split2(router_w[l].T)), router_bias[l].reshape(N_EXPERTS, 1))
    cnt = counts[:, :, 0]
    off = jnp.cumsum(cnt, axis=1) - cnt
    cnt, off = cnt.reshape(-1), off.reshape(-1)
    plan_rows, plan_gates, gate_seg = _plan_sc(off, idx, rank, gate)
    base = _dense(x1, p[l].reshape(T, PLE_DIM), bf(shared_w_gate[l]), bf(shared_w_up[l]), bf(shared_w_down[l]),
                  w_pg_b, bf(ple_proj_w[l]))
    out = _moe_tiles(x1, base, plan_rows, plan_gates, gate_seg, cnt, off, wg_b, wu_b, wd_b,
                     ln2_w[l:l + 1], ln2_b[l:l + 1])
    return out.reshape(B, S, D)
```

```python
import functools

import jax
import jax.numpy as jnp
import numpy as np
from jax import lax
from jax.experimental import pallas as pl
from jax.experimental.pallas import tpu as pltpu
from jax.experimental.pallas import tpu_sc as plsc

F32 = jnp.float32
BF16 = jnp.bfloat16

D_MODEL = 1024
ATT_HEAD_DIM = 64
ATT_HEADS = 8
ATT_DILATIONS = (1, 4, 16)
ATT_BLOCK = 128
ATT_WIDTH = ATT_HEADS * ATT_HEAD_DIM
ATT_TILE = ATT_BLOCK * max(ATT_DILATIONS)
NEG_INF = -1e30
LOG2_E = 1.4426950408889634

HG_HEADS = 8
HG_DIM = 128
HG_WIDTH = HG_HEADS * HG_DIM
HG_CHUNK = 32
HG_TILE = 256
RMS_EPS = 1e-6

N_EXPERTS = 64
TOP_K = 8
N_GROUPS = 8
GROUP_SIZE = N_EXPERTS // N_GROUPS
TOPK_GROUPS = 4
EXPERT_FF = 256
ROUTED_SCALE = 2.5
PLE_DIM = 256
LN_EPS = 1e-5
DEPTH = 1
DEEPNORM_ALPHA = (2.0 * DEPTH) ** 0.25

LANES = 128
LANE_CHUNKS = D_MODEL // LANES
PROJ_ROWS = 512
PROJ_COLS = 1536
ATT_PROJ_ROWS = 1024
STRIDE_STEP = 4
MIX_ROWS = 512
DENSE_ROWS = 1024
MOE_TILE = 4096
MOE_CHUNK = 576
GATE_SEG = 640
LN_ROWS = 256
LIST_PAD = 1024
LIST_LEN = MOE_TILE * TOP_K + LIST_PAD
GATHER_GROUP = 8
SCATTER_GROUP = 8
V7X_VMEM_LIMIT = 56 * 1024 * 1024
SC_CORES, SC_SUBCORES, SC_LANES = 2, 16, 16


def _cparams(*sem):
    return pltpu.CompilerParams(dimension_semantics=sem, vmem_limit_bytes=V7X_VMEM_LIMIT)


def _proj_att_kernel(*refs, dil, n_conv):
    x_refs, (wq_ref, wk_ref, wv_ref) = refs[:LANE_CHUNKS], refs[LANE_CHUNKS:LANE_CHUNKS + 3]
    conv_in = refs[LANE_CHUNKS + 3:LANE_CHUNKS + 3 + n_conv]
    o_ref = refs[LANE_CHUNKS + 3 + n_conv]
    conv_out = refs[LANE_CHUNKS + 4 + n_conv:LANE_CHUNKS + 4 + 2 * n_conv]
    w_ref = refs[LANE_CHUNKS + 4 + 2 * n_conv]
    stage_refs = refs[LANE_CHUNKS + 5 + 2 * n_conv:]
    n = ATT_PROJ_ROWS // dil
    for src, dst in zip(conv_in, conv_out):
        dst[...] = src[...].astype(BF16)

    @pl.when(pl.program_id(0) == 0)
    def _():
        w_ref[:, :ATT_WIDTH] = (wq_ref[...] * (ATT_HEAD_DIM ** -0.5 * LOG2_E)).astype(BF16)
        w_ref[:, ATT_WIDTH:2 * ATT_WIDTH] = wk_ref[...].astype(BF16)
        w_ref[:, 2 * ATT_WIDTH:] = wv_ref[...].astype(BF16)

    def rows(ref, stage):
        if dil == 1:
            return ref[...]
        if dil <= STRIDE_STEP:
            return jnp.concatenate([ref[pl.ds(r, n, stride=dil), :] for r in range(dil)], axis=0)
        assert dil == STRIDE_STEP * STRIDE_STEP
        quarter = ATT_PROJ_ROWS // STRIDE_STEP
        for r1 in range(STRIDE_STEP):
            stage[pl.ds(r1 * quarter, quarter), :] = ref[pl.ds(r1, quarter, stride=STRIDE_STEP), :]
        return jnp.concatenate(
            [stage[pl.ds((r % STRIDE_STEP) * quarter + r // STRIDE_STEP, n, stride=STRIDE_STEP), :]
             for r in range(dil)], axis=0)

    xp = jnp.concatenate([rows(ref, stage_refs[c] if stage_refs else None).astype(BF16)
                          for c, ref in enumerate(x_refs)], axis=1)
    y = jnp.dot(xp, w_ref[...], preferred_element_type=F32)
    o_ref[...] = y.astype(BF16).reshape(dil, n, 3 * ATT_WIDTH)


def _proj_att(x2d, w_in_l, g, dil, conv=None):
    T = x2d.shape[0]
    per = ATT_TILE // ATT_PROJ_ROWS
    n = ATT_PROJ_ROWS // dil
    steps = T // ATT_PROJ_ROWS
    n_groups = len(ATT_DILATIONS)
    conv_args, conv_in, conv_out, conv_shapes = [], [], [], []
    for mat, width, first, blocks in conv or ():
        slab = mat.shape[0] // steps
        assert slab * steps == mat.shape[0] and slab % 16 == 0
        conv_args += [mat] * blocks
        conv_in += [pl.BlockSpec((slab, width), functools.partial(lambda i, c: (i, c), c=first + c))
                    for c in range(blocks)]
        conv_out += [pl.BlockSpec((slab, width), lambda i: (i, 0))] * blocks
        conv_shapes += [jax.ShapeDtypeStruct((mat.shape[0], width), BF16)] * blocks
    out, *converted = pl.pallas_call(
        functools.partial(_proj_att_kernel, dil=dil, n_conv=len(conv_args)),
        grid=(steps,),
        in_specs=[pl.BlockSpec((ATT_PROJ_ROWS, LANES), functools.partial(lambda i, c: (i, c), c=c))
                  for c in range(LANE_CHUNKS)]
                 + [pl.BlockSpec((D_MODEL, ATT_WIDTH), functools.partial(lambda i, c: (0, c), c=part * n_groups + g))
                    for part in range(3)] + conv_in,
        out_specs=[pl.BlockSpec((None, dil, None, n, 3 * ATT_WIDTH), lambda i: (i // per, 0, i % per, 0, 0))]
                  + conv_out,
        out_shape=[jax.ShapeDtypeStruct((T // ATT_TILE, dil, per, n, 3 * ATT_WIDTH), BF16)] + conv_shapes,
        scratch_shapes=[pltpu.VMEM((D_MODEL, 3 * ATT_WIDTH), BF16)]
                       + ([pltpu.VMEM((ATT_PROJ_ROWS, LANES), F32)] * LANE_CHUNKS if dil > STRIDE_STEP else []),
        compiler_params=_cparams("arbitrary"),
        name=f"proj_att_d{dil}",
    )(*([x2d] * LANE_CHUNKS), *([w_in_l] * 3), *conv_args)
    out = out.reshape(T // ATT_TILE, dil, ATT_TILE // dil, 3 * ATT_WIDTH)
    return (out, converted) if conv is not None else out


def _att_pair(q2, kp, kc, vp, vc, bias_ref, g, first):
    def head0_lanes(rows, dtype):
        lane = lax.broadcasted_iota(jnp.int32, (rows, 2 * ATT_HEAD_DIM), 1)
        return lane.astype(F32).astype(dtype) < ATT_HEAD_DIM

    lo_q = head0_lanes(ATT_BLOCK, BF16)
    lo_v = head0_lanes(2 * ATT_BLOCK, BF16)
    k2 = jnp.concatenate([kp, kc], axis=0)
    v2 = jnp.concatenate([vp, vc], axis=0)
    zero = jnp.zeros_like(q2)
    ps, ms = [], []
    for hh in range(2):
        qm = jnp.where(lo_q, q2, zero) if hh == 0 else jnp.where(lo_q, zero, q2)
        s = lax.dot_general(qm, k2, (((1,), (1,)), ((), ())), preferred_element_type=F32)
        s = s + bias_ref[g, hh, first]
        m = jnp.max(s, axis=-1, keepdims=True)
        ps.append(jnp.exp2(s - m).astype(BF16))
        ms.append(m)
    pcat = jnp.concatenate(ps, axis=1)
    zero_v, one_v = jnp.zeros_like(v2), jnp.ones_like(v2)
    rhs = jnp.concatenate([
        jnp.concatenate([jnp.where(lo_v, v2, zero_v), jnp.where(lo_v, one_v, zero_v)], axis=1),
        jnp.concatenate([jnp.where(lo_v, zero_v, v2), jnp.where(lo_v, zero_v, one_v)], axis=1)], axis=0)
    nd = jnp.dot(pcat, rhs, preferred_element_type=F32)
    m2 = jnp.where(head0_lanes(ATT_BLOCK, F32), ms[0], ms[1])
    return nd[:, :2 * ATT_HEAD_DIM], m2, nd[:, 2 * ATT_HEAD_DIM:]


def _att_kernel(*refs):
    (q0, kc0, vc0, kp0, vp0, q1, kc1, vc1, kp1, vp1, q2, kc2, vc2, kp2, vp2,
     bias_ref, o_ref) = refs[:17]
    ng = len(ATT_DILATIONS)
    num_s, m_s, den_s = refs[17:17 + ng], refs[17 + ng:17 + 2 * ng], refs[17 + 2 * ng:]
    first_tile = (pl.program_id(2) == 0).astype(jnp.int32)
    groups = ((q0, kc0, vc0, kp0, vp0), (q1, kc1, vc1, kp1, vp1), (q2, kc2, vc2, kp2, vp2))
    for g, dil in enumerate(ATT_DILATIONS):
        q_ref, kc_ref, vc_ref, kp_ref, vp_ref = groups[g]
        nb = ATT_TILE // dil // ATT_BLOCK
        for r in range(dil):
            for n in range(nb):
                rows = pl.ds(n * ATT_BLOCK, ATT_BLOCK)
                if n == 0:
                    prev = pl.ds((nb - 1) * ATT_BLOCK, ATT_BLOCK)
                    kp, vp, first = kp_ref[r, prev, :], vp_ref[r, prev, :], first_tile
                else:
                    prev = pl.ds((n - 1) * ATT_BLOCK, ATT_BLOCK)
                    kp, vp, first = kc_ref[r, prev, :], vc_ref[r, prev, :], 0
                num, m, den = _att_pair(q_ref[r, rows, :], kp, kc_ref[r, rows, :], vp, vc_ref[r, rows, :],
                                        bias_ref, g, first)
                if dil == 1:
                    dst = rows
                else:
                    dst = pl.ds(n * ATT_BLOCK * dil + r, ATT_BLOCK, stride=dil)
                num_s[g][dst, :] = num
                m_s[g][dst, :] = m
                den_s[g][dst, :] = den
    m_all = jnp.maximum(jnp.maximum(m_s[0][...], m_s[1][...]), m_s[2][...])
    num = jnp.zeros((ATT_TILE, 2 * ATT_HEAD_DIM), F32)
    den = jnp.zeros((ATT_TILE, 2 * ATT_HEAD_DIM), F32)
    for g in range(ng):
        sc = jnp.exp2(m_s[g][...] - m_all)
        num = num + sc * num_s[g][...]
        den = den + sc * den_s[g][...]
    o_ref[...] = (num / den).astype(o_ref.dtype)


def _att_bias_table():
    qi = np.arange(ATT_BLOCK)[:, None]
    ki = np.arange(2 * ATT_BLOCK)[None, :]
    steps = qi + ATT_BLOCK - ki
    valid = (steps >= 0) & (steps <= ATT_BLOCK)
    slopes = np.array([2.0 ** (-8.0 * (h + 1) / ATT_HEADS) for h in range(ATT_HEADS)], np.float32)
    tab = np.empty((len(ATT_DILATIONS), ATT_HEADS, 2, ATT_BLOCK, 2 * ATT_BLOCK), np.float32)
    for g, dil in enumerate(ATT_DILATIONS):
        bias = -slopes[:, None, None] * (steps * dil).astype(np.float32)[None] * LOG2_E
        tab[g, :, 0] = np.where(valid[None], bias, NEG_INF)
        tab[g, :, 1] = np.where((valid & (ki >= ATT_BLOCK))[None], bias, NEG_INF)
    return jnp.asarray(tab)


def _attention(qkv, B, S):
    tiles = S // ATT_TILE
    pair = 2 * ATT_HEAD_DIM
    npair = ATT_WIDTH // pair
    in_specs, args = [], []
    for g, dil in enumerate(ATT_DILATIONS):
        blk = (None, dil, ATT_TILE // dil, pair)
        cur = lambda b, hp, t, off: (b * tiles + t, 0, 0, off * npair + hp)
        prv = lambda b, hp, t, off: (b * tiles + jnp.maximum(t - 1, 0), 0, 0, off * npair + hp)
        in_specs += [pl.BlockSpec(blk, functools.partial(cur, off=0)),
                     pl.BlockSpec(blk, functools.partial(cur, off=1)),
                     pl.BlockSpec(blk, functools.partial(cur, off=2)),
                     pl.BlockSpec(blk, functools.partial(prv, off=1)),
                     pl.BlockSpec(blk, functools.partial(prv, off=2))]
        args += [qkv[g]] * 5
    in_specs.append(pl.BlockSpec((len(ATT_DILATIONS), 2, 2, ATT_BLOCK, 2 * ATT_BLOCK),
                                 lambda b, hp, t: (0, hp, 0, 0, 0)))
    args.append(_att_bias_table())
    scratch = [pltpu.VMEM((ATT_TILE, pair), F32) for _ in range(3 * len(ATT_DILATIONS))]
    return pl.pallas_call(
        _att_kernel,
        grid=(B, npair, tiles),
        in_specs=in_specs,
        out_specs=pl.BlockSpec((ATT_TILE, pair), lambda b, hp, t: (b * tiles + t, hp)),
        out_shape=jax.ShapeDtypeStruct((B * S, ATT_WIDTH), BF16),
        scratch_shapes=scratch,
        compiler_params=_cparams("parallel", "parallel", "arbitrary"),
        name="dilated_attention",
    )(*args)


def _proj_kernel(x_ref, *refs, n_w):
    w_refs = refs[:n_w]
    eg_ref, eu_ref, ed_ref, o_ref, egb_ref, eub_ref, edb_ref = refs[n_w:]
    egb_ref[...] = eg_ref[...].astype(BF16)
    eub_ref[...] = eu_ref[...].astype(BF16)
    edb_ref[...] = ed_ref[...].astype(BF16)
    xb = x_ref[...].astype(BF16)
    col_tile = w_refs[0].shape[1]
    for c, w_ref in enumerate(w_refs):
        cols = slice(c * col_tile, (c + 1) * col_tile)
        o_ref[:, cols] = jnp.dot(xb, w_ref[...], preferred_element_type=F32).astype(o_ref.dtype)


def _proj(x2d, ws, expert_weights):
    T, N = x2d.shape[0], len(ws) * ws[0].shape[1]
    steps = T // PROJ_ROWS
    n_experts = expert_weights[0].shape[0]
    per_step = max(1, n_experts // steps)
    assert n_experts % per_step == 0 and n_experts // per_step <= steps
    per_expert = lambda ew: pl.BlockSpec(
        (per_step,) + ew.shape[1:], lambda i: (jnp.minimum(i, n_experts // per_step - 1), 0, 0))
    return pl.pallas_call(
        functools.partial(_proj_kernel, n_w=len(ws)),
        grid=(steps,),
        in_specs=[pl.BlockSpec((PROJ_ROWS, D_MODEL), lambda i: (i, 0))]
                 + [pl.BlockSpec(w.shape, lambda i: (0, 0)) for w in ws] + [per_expert(ew) for ew in expert_weights],
        out_specs=[pl.BlockSpec((PROJ_ROWS, N), lambda i: (i, 0))] + [per_expert(ew) for ew in expert_weights],
        out_shape=[jax.ShapeDtypeStruct((T, N), BF16)]
                  + [jax.ShapeDtypeStruct(ew.shape, BF16) for ew in expert_weights],
        compiler_params=_cparams("parallel"),
        name="proj_hgrn_gates",
    )(x2d, *ws, *expert_weights)


def _split2(v):
    a = v.astype(BF16)
    return a, (v - a.astype(F32)).astype(BF16)


def _hgrn_kernel(q_ref, f_ref, i_ref, g_ref, lbl_ref, gain_ref, o_ref, state_ref):
    @pl.when(pl.program_id(1) == 0)
    def _():
        state_ref[...] = jnp.zeros_like(state_ref)

    lbl = lbl_ref[...]
    e = jnp.exp(lbl - jnp.max(lbl, axis=0, keepdims=True))
    lb = e[0:1] / jnp.sum(e, axis=0, keepdims=True)
    forget = lb + (1.0 - lb) * jax.nn.sigmoid(f_ref[...].astype(F32))
    log_f = jnp.log(forget)
    key = 1.0 - forget

    row = lax.broadcasted_iota(jnp.int32, (HG_TILE, HG_TILE), 0)
    col = lax.broadcasted_iota(jnp.int32, (HG_TILE, HG_TILE), 1)
    causal = (row >= col) & ((row // HG_CHUNK) == (col // HG_CHUNK))
    tri = jnp.where(causal, 1.0, 0.0).astype(BF16)
    b = sum(jnp.dot(tri, t, preferred_element_type=F32) for t in _split2(log_f))
    eb = jnp.exp(b)
    q_dec = (q_ref[...].astype(F32) * eb).astype(BF16)
    k_inv = key * jnp.exp(-b)
    xi = i_ref[...].astype(F32)
    val = (xi * jax.nn.sigmoid(xi)).astype(BF16)
    k_inv_b = k_inv.astype(BF16)

    n_chunks = HG_TILE // HG_CHUNK
    last_rows = [eb[(c + 1) * HG_CHUNK - 1:(c + 1) * HG_CHUNK, :] for c in range(n_chunks)]
    dec_rows = jnp.concatenate([jnp.broadcast_to(r, (HG_CHUNK, HG_WIDTH)) for r in last_rows], axis=0)
    k_end = (k_inv * dec_rows).astype(BF16)
    def per_chunk_columns(t):
        blocks = []
        for c in range(n_chunks):
            rows_above, rows_below = c * HG_CHUNK, HG_TILE - (c + 1) * HG_CHUNK
            parts = [t[rows_above:rows_above + HG_CHUNK]]
            if rows_above:
                parts.insert(0, jnp.zeros((rows_above, HG_DIM), t.dtype))
            if rows_below:
                parts.append(jnp.zeros((rows_below, HG_DIM), t.dtype))
            blocks.append(jnp.concatenate(parts, axis=0))
        return jnp.concatenate(blocks, axis=1)

    head_cols = [slice(h * HG_DIM, (h + 1) * HG_DIM) for h in range(HG_HEADS)]
    upds = [lax.dot_general(val[:, cols], per_chunk_columns(k_end[:, cols]), (((0,), (0,)), ((), ())),
                            preferred_element_type=F32) for cols in head_cols]
    o_intras = []
    for cols in head_cols:
        a = lax.dot_general(q_dec[:, cols], k_inv_b[:, cols], (((1,), (1,)), ((), ())), preferred_element_type=F32)
        a = jnp.where(causal, a, 0.0).astype(BF16)
        o_intras.append(jnp.dot(a, val[:, cols], preferred_element_type=F32))
    enterings = []
    for h, cols in enumerate(head_cols):
        st = state_ref[h]
        entering = []
        for c in range(n_chunks):
            entering.append(st.astype(BF16))
            st = st * last_rows[c][:, cols] + upds[h][:, c * HG_DIM:(c + 1) * HG_DIM]
        state_ref[h] = st
        enterings.append(jnp.concatenate(entering, axis=1))
    outs = []
    for h, cols in enumerate(head_cols):
        qd = q_dec[:, cols]
        o_inter = lax.dot_general(per_chunk_columns(qd), enterings[h],
                                  (((1,), (1,)), ((), ())), preferred_element_type=F32)
        o = o_intras[h] + o_inter
        o = o * lax.rsqrt(jnp.mean(jnp.square(o), axis=-1, keepdims=True) + RMS_EPS)
        outs.append(o)
    o = jnp.concatenate(outs, axis=1) * gain_ref[...]
    gg = g_ref[...].astype(F32)
    o_ref[...] = (o * (gg * jax.nn.sigmoid(gg))).astype(o_ref.dtype)


def _hgrn(u_hg, lb_logits, gain, B, S):
    tiles = S // HG_TILE
    col = lambda j: pl.BlockSpec((HG_TILE, HG_WIDTH), functools.partial(lambda b, t, j: (b * tiles + t, j), j=j))
    return pl.pallas_call(
        _hgrn_kernel,
        grid=(B, tiles),
        in_specs=[col(0), col(1), col(2), col(3),
                  pl.BlockSpec((2, HG_WIDTH), lambda b, t: (0, 0)),
                  pl.BlockSpec((1, HG_WIDTH), lambda b, t: (0, 0))],
        out_specs=pl.BlockSpec((HG_TILE, HG_WIDTH), lambda b, t: (b * tiles + t, 0)),
        out_shape=jax.ShapeDtypeStruct((B * S, HG_WIDTH), BF16),
        scratch_shapes=[pltpu.VMEM((HG_HEADS, HG_DIM, HG_DIM), F32)],
        compiler_params=_cparams("parallel", "arbitrary"),
        name="hgrn2",
    )(u_hg, u_hg, u_hg, u_hg, lb_logits, gain)


def _load_row_tiles(ref, n, start=0):
    return jnp.concatenate([ref[pl.ds(start + c, n, stride=LANE_CHUNKS), :] for c in range(LANE_CHUNKS)], axis=1)


def _store_row_tiles(ref, val, n):
    for c in range(LANE_CHUNKS):
        ref[pl.ds(c, n, stride=LANE_CHUNKS), :] = val[:, c * LANES:(c + 1) * LANES]


def _layer_norm(z, w, b):
    mu = jnp.mean(z, axis=-1, keepdims=True)
    zc = z - mu
    var = jnp.mean(jnp.square(zc), axis=-1, keepdims=True)
    return zc * lax.rsqrt(var + LN_EPS) * w + b


def _merge_kernel(ya_ref, yh_ref, ga_ref, gh_ref, x_ref, wa_ref, wh_ref, wo_ref, lw_ref, lb_ref, o_ref):
    ma = jnp.dot(ya_ref[...], wa_ref[...], preferred_element_type=F32)
    mh = jnp.dot(yh_ref[...], wh_ref[...], preferred_element_type=F32)
    merged = (jax.nn.sigmoid(ga_ref[...].astype(F32)) * ma + jax.nn.sigmoid(gh_ref[...].astype(F32)) * mh)
    z = DEEPNORM_ALPHA * x_ref[...] + jnp.dot(merged.astype(BF16), wo_ref[...], preferred_element_type=F32)
    _store_row_tiles(o_ref, _layer_norm(z, lw_ref[...], lb_ref[...]), MIX_ROWS)


def _merge(y_att, y_hg, u_hg, x2d, w_a, w_h, w_o, ln_w, ln_b):
    T = x2d.shape[0]
    rows = lambda width, j=0: pl.BlockSpec((MIX_ROWS, width), functools.partial(lambda i, j: (i, j), j=j))
    full = lambda a: pl.BlockSpec(a.shape, lambda i: (0, 0))
    return pl.pallas_call(
        _merge_kernel,
        grid=(T // MIX_ROWS,),
        in_specs=[rows(ATT_WIDTH), rows(HG_WIDTH), rows(D_MODEL, 4), rows(D_MODEL, 5), rows(D_MODEL),
                  full(w_a), full(w_h), full(w_o), full(ln_w), full(ln_b)],
        out_specs=pl.BlockSpec((MIX_ROWS * LANE_CHUNKS, LANES), lambda i: (i, 0)),
        out_shape=jax.ShapeDtypeStruct((T * LANE_CHUNKS, LANES), F32),
        compiler_params=_cparams("parallel"),
        name="merge_ln1",
    )(y_att, y_hg, u_hg, u_hg, x2d, w_a, w_h, w_o, ln_w, ln_b)


def _first_argmax(v, ids, n):
    mx = jnp.max(v, axis=0, keepdims=True)
    return mx, jnp.min(jnp.where(v == mx, ids, n), axis=0, keepdims=True)


def _dense_kernel(x1_ref, p_ref, wsg_ref, wsu_ref, wsd_ref, wpg_ref, wpp_ref, base_ref):
    x1 = _load_row_tiles(x1_ref, DENSE_ROWS)
    x1b = x1.astype(BF16)
    hg = jnp.dot(x1b, wsg_ref[...], preferred_element_type=F32)
    hu = jnp.dot(x1b, wsu_ref[...], preferred_element_type=F32)
    ple_gate = jnp.dot(x1b, wpg_ref[...], preferred_element_type=F32)
    ple_proj = jnp.dot(p_ref[...].astype(BF16), wpp_ref[...], preferred_element_type=F32)
    shared = jnp.dot((hg * jax.nn.sigmoid(hg) * hu).astype(BF16), wsd_ref[...], preferred_element_type=F32)
    ple = jax.nn.sigmoid(ple_gate) * ple_proj
    _store_row_tiles(base_ref, DEEPNORM_ALPHA * x1 + shared + ple, DENSE_ROWS)


def _dense(x1, p2d, wsg, wsu, wsd, wpg, wpp):
    T = x1.shape[0] // LANE_CHUNKS
    full = lambda a: pl.BlockSpec(a.shape, lambda i: (0,) * a.ndim)
    row_tiles = pl.BlockSpec((DENSE_ROWS * LANE_CHUNKS, LANES), lambda i: (i, 0))
    return pl.pallas_call(
        _dense_kernel,
        grid=(T // DENSE_ROWS,),
        in_specs=[row_tiles, pl.BlockSpec((DENSE_ROWS, PLE_DIM), lambda i: (i, 0)),
                  full(wsg), full(wsu), full(wsd), full(wpg), full(wpp)],
        out_specs=row_tiles,
        out_shape=jax.ShapeDtypeStruct((T * LANE_CHUNKS, LANES), F32),
        compiler_params=_cparams("parallel"),
        name="shared_ple",
    )(x1, p2d, wsg, wsu, wsd, wpg, wpp)


def _route_kernel(x1_ref, wrt_ref, rb_ref, idx_ref, gate_ref, rank_ref, cnt_ref, carry_ref):
    @pl.when(pl.program_id(0) % (MOE_TILE // MIX_ROWS) == 0)
    def _():
        carry_ref[...] = jnp.zeros_like(carry_ref)

    x1 = _load_row_tiles(x1_ref, MIX_ROWS)
    x1b = x1.astype(BF16)
    x1_lo = (x1 - x1b.astype(F32)).astype(BF16)
    nt = (((1,), (1,)), ((), ()))
    logits = (lax.dot_general(wrt_ref[0], x1b, nt, preferred_element_type=F32)
              + lax.dot_general(wrt_ref[1], x1b, nt, preferred_element_type=F32)
              + lax.dot_general(wrt_ref[0], x1_lo, nt, preferred_element_type=F32))

    s = jax.nn.sigmoid(logits)
    sel = s + rb_ref[...]
    eid = lax.broadcasted_iota(jnp.int32, (N_EXPERTS, MIX_ROWS), 0)
    neg = -jnp.inf

    grp = sel.reshape(N_GROUPS, GROUP_SIZE, MIX_ROWS)
    mid = lax.broadcasted_iota(jnp.int32, grp.shape, 1)
    m1 = jnp.max(grp, axis=1, keepdims=True)
    i1 = jnp.min(jnp.where(grp == m1, mid, GROUP_SIZE), axis=1, keepdims=True)
    m2 = jnp.max(jnp.where(mid == i1, neg, grp), axis=1, keepdims=True)
    gscore = (m1 + m2).reshape(N_GROUPS, MIX_ROWS)
    gid = lax.broadcasted_iota(jnp.int32, (N_GROUPS, MIX_ROWS), 0)
    gsel = jnp.zeros((N_GROUPS, MIX_ROWS), jnp.bool_)
    for _ in range(TOPK_GROUPS):
        _, gi = _first_argmax(gscore, gid, N_GROUPS)
        hit = gid == gi
        gsel = gsel | hit
        gscore = jnp.where(hit, neg, gscore)
    emask = jnp.broadcast_to(gsel.reshape(N_GROUPS, 1, MIX_ROWS), grp.shape).reshape(N_EXPERTS, MIX_ROWS)
    cand = jnp.where(emask, sel, neg)

    idxs, gates = [], []
    chosen = jnp.zeros((N_EXPERTS, MIX_ROWS), jnp.bool_)
    for _ in range(TOP_K):
        _, ei = _first_argmax(cand, eid, N_EXPERTS)
        hit = eid == ei
        idxs.append(ei)
        gates.append(jnp.sum(jnp.where(hit, s, 0.0), axis=0, keepdims=True))
        chosen = chosen | hit
        cand = jnp.where(hit, neg, cand)
    g = jnp.concatenate(gates, axis=0)
    g = g / jnp.sum(g, axis=0, keepdims=True) * ROUTED_SCALE
    idx_ref[...] = jnp.concatenate(idxs, axis=0)
    gate_ref[...] = g

    onehot = jnp.where(chosen, 1.0, 0.0)
    tr = lax.broadcasted_iota(jnp.int32, (MIX_ROWS, MIX_ROWS), 0)
    tc = lax.broadcasted_iota(jnp.int32, (MIX_ROWS, MIX_ROWS), 1)
    before = jnp.where(tr < tc, 1.0, 0.0).astype(BF16)
    prefix = jnp.dot(onehot.astype(BF16), before, preferred_element_type=F32)
    rankfull = (carry_ref[:, 0:1] + prefix).astype(jnp.int32)
    rank_ref[...] = jnp.concatenate(
        [jnp.sum(jnp.where(eid == ei, rankfull, 0), axis=0, keepdims=True) for ei in idxs], axis=0)
    total = carry_ref[...] + jnp.sum(onehot, axis=1, keepdims=True)
    carry_ref[...] = total
    cnt_ref[...] = total.astype(jnp.int32)


def _route(x1, wr_t, rbias):
    T = x1.shape[0] // LANE_CHUNKS
    per_tile = MOE_TILE // MIX_ROWS
    full = lambda a: pl.BlockSpec(a.shape, lambda i: (0,) * a.ndim)
    tok = pl.BlockSpec((TOP_K, MIX_ROWS), lambda i: (0, i))
    row_tiles = pl.BlockSpec((MIX_ROWS * LANE_CHUNKS, LANES), lambda i: (i, 0))
    return pl.pallas_call(
        _route_kernel,
        grid=(T // MIX_ROWS,),
        in_specs=[row_tiles, full(wr_t), full(rbias)],
        out_specs=[tok, tok, tok,
                   pl.BlockSpec((None, N_EXPERTS, LANES), lambda i: (i // per_tile, 0, 0))],
        out_shape=[jax.ShapeDtypeStruct((TOP_K, T), jnp.int32),
                   jax.ShapeDtypeStruct((TOP_K, T), F32),
                   jax.ShapeDtypeStruct((TOP_K, T), jnp.int32),
                   jax.ShapeDtypeStruct((T // MOE_TILE, N_EXPERTS, LANES), jnp.int32)],
        scratch_shapes=[pltpu.VMEM((N_EXPERTS, LANES), F32)],
        compiler_params=_cparams("arbitrary"),
        name="route",
    )(x1, wr_t, rbias)


def _plan_sc_kernel(off_hbm, idx_hbm, rank_hbm, gate_hbm, rows_hbm, gates_hbm, seg_hbm,
                    off_v, idx_v, rank_v, gate_v, rows_v, gates_v, seg_v):
    n_tokens = idx_hbm.shape[0] // TOP_K
    worker = lax.axis_index("subcore") * SC_CORES + lax.axis_index("core")

    @pl.when(worker < n_tokens // MOE_TILE)
    def _():
        pltpu.sync_copy(off_hbm.at[pl.ds(worker * N_EXPERTS, N_EXPERTS)], off_v)
        lane = lax.iota(jnp.int32, SC_LANES)

        @pl.loop(0, N_EXPERTS * GATE_SEG // SC_LANES)
        def _(i):
            seg_v[pl.ds(i * SC_LANES, SC_LANES)] = jnp.zeros((SC_LANES,), F32)

        for k in range(TOP_K):
            row = pl.ds(k * n_tokens + worker * MOE_TILE, MOE_TILE)
            pltpu.sync_copy(idx_hbm.at[row], idx_v)
            pltpu.sync_copy(rank_hbm.at[row], rank_v)
            pltpu.sync_copy(gate_hbm.at[row], gate_v)

            @pl.loop(0, MOE_TILE // SC_LANES)
            def _(i):
                at = i * SC_LANES
                expert, rank, gate = (v[pl.ds(at, SC_LANES)] for v in (idx_v, rank_v, gate_v))
                pos = plsc.load_gather(off_v, [expert]) + rank
                plsc.store_scatter(rows_v, [pos], (lane + at) * LANE_CHUNKS)
                plsc.store_scatter(gates_v, [pos], gate)
                plsc.store_scatter(seg_v, [expert * GATE_SEG + rank], gate, mask=rank < GATE_SEG)

        @pl.loop(0, LIST_PAD // SC_LANES)
        def _(i):
            tail = pl.ds(MOE_TILE * TOP_K + i * SC_LANES, SC_LANES)
            rows_v[tail] = jnp.zeros((SC_LANES,), jnp.int32)
            gates_v[tail] = jnp.zeros((SC_LANES,), F32)

        out = pl.ds(worker * LIST_LEN, LIST_LEN)
        pltpu.sync_copy(rows_v, rows_hbm.at[out])
        pltpu.sync_copy(gates_v, gates_hbm.at[out])
        pltpu.sync_copy(seg_v, seg_hbm.at[pl.ds(worker * (N_EXPERTS * GATE_SEG), N_EXPERTS * GATE_SEG)])


def _plan_sc(off, idx, rank, gate):
    n_tiles = idx.shape[1] // MOE_TILE
    assert n_tiles <= SC_CORES * SC_SUBCORES
    mesh = plsc.VectorSubcoreMesh(core_axis_name="core", subcore_axis_name="subcore",
                                  num_cores=SC_CORES, num_subcores=SC_SUBCORES)
    rows, gates, seg = pl.kernel(
        _plan_sc_kernel,
        out_type=(jax.ShapeDtypeStruct((n_tiles * LIST_LEN,), jnp.int32),
                  jax.ShapeDtypeStruct((n_tiles * LIST_LEN,), F32),
                  jax.ShapeDtypeStruct((n_tiles * N_EXPERTS * GATE_SEG,), F32)),
        mesh=mesh,
        scratch_types=[pltpu.VMEM((N_EXPERTS,), jnp.int32), pltpu.VMEM((MOE_TILE,), jnp.int32),
                       pltpu.VMEM((MOE_TILE,), jnp.int32), pltpu.VMEM((MOE_TILE,), F32),
                       pltpu.VMEM((LIST_LEN,), jnp.int32), pltpu.VMEM((LIST_LEN,), F32),
                       pltpu.VMEM((N_EXPERTS * GATE_SEG,), F32)],
        compiler_params=pltpu.CompilerParams(needs_layout_passes=False),
        name="moe_plan_sc",
    )(off, idx.reshape(-1), rank.reshape(-1), gate.reshape(-1))
    return rows, gates, seg.reshape(n_tiles * N_EXPERTS, 1, GATE_SEG)


def _moe_tile_kernel(cnt_ref, off_ref, wg_ref, wu_ref, wd_ref, lw_ref, lb_ref, seg_ref, rows_hbm, gates_hbm, x_hbm,
                     base_hbm, o_hbm, x_s, acc_s, xg_a, xg_b, xg_c, y_a, y_b, y_c, stage_s, rows_s, gates_s, sem):
    tile, e = pl.program_id(0), pl.program_id(1)
    rows_of = lambda ref, r, n: ref.at[pl.ds(pl.multiple_of(r * LANE_CHUNKS, LANE_CHUNKS), n * LANE_CHUNKS), :]
    tile_rows = pl.ds(pl.multiple_of(tile * (MOE_TILE * LANE_CHUNKS), LANE_CHUNKS), MOE_TILE * LANE_CHUNKS)

    pair = tile * N_EXPERTS + e
    last_pair = pl.num_programs(0) * N_EXPERTS - 1
    n, off = cnt_ref[pair], off_ref[pair]
    tile_at = lambda ref, r: ref.at[pl.ds(pl.multiple_of(r, LANE_CHUNKS), LANE_CHUNKS), :]

    def gather_group(xg, first, jb):
        at = first + jb * GATHER_GROUP
        rows = [tile_at(x_s, rows_s[at + u])[...] for u in range(GATHER_GROUP)]
        rows_of(xg, jb * GATHER_GROUP, GATHER_GROUP)[...] = jnp.concatenate(rows, axis=0)

    def gather_loop(xg, first):
        def body(jb, cc):
            gather_group(xg, first, jb)
            return cc

        lax.fori_loop(0, MOE_CHUNK // GATHER_GROUP, body, 0)

    def swiglu(xg, y):
        xb = _load_row_tiles(xg, MOE_CHUNK).astype(BF16)
        hg = jnp.dot(xb, wg_ref[...], preferred_element_type=F32)
        hu = jnp.dot(xb, wu_ref[...], preferred_element_type=F32)
        act = (hg * jax.nn.sigmoid(hg) * hu).astype(BF16)
        _store_row_tiles(y, jnp.dot(act, wd_ref[...], preferred_element_type=F32), MOE_CHUNK)

    def scatter_group(y, first, j0, live, gated):
        dsts = [rows_s[first + j0 + u] for u in range(live)]
        yv = rows_of(y, j0, live)[...]
        rows = [yv[u * LANE_CHUNKS:(u + 1) * LANE_CHUNKS] for u in range(live)]
        if not gated:
            rows = [gates_s[first + j0 + u] * r for u, r in enumerate(rows)]
        vals = [tile_at(acc_s, d)[...] + r for d, r in zip(dsts, rows)]
        for d, val in reversed(list(zip(dsts, vals))):
            tile_at(acc_s, d)[...] = val

    def scatter_loop(y, first, m, gated):
        def body(jg, cc):
            scatter_group(y, first, jg * SCATTER_GROUP, SCATTER_GROUP, gated)
            return cc

        lax.fori_loop(0, m // SCATTER_GROUP, body, 0)
        for live in range(1, SCATTER_GROUP):
            @pl.when(m % SCATTER_GROUP == live)
            def _(live=live):
                scatter_group(y, first, m - live, live, gated)

    def plan_loads(t):
        plan = pl.ds(pl.multiple_of(t * LIST_LEN, LIST_PAD), LIST_LEN)
        return (pltpu.make_async_copy(rows_hbm.at[plan], rows_s, sem.at[0]),
                pltpu.make_async_copy(gates_hbm.at[plan], gates_s, sem.at[1]))

    def x_load(t):
        rows = pl.ds(pl.multiple_of(t * (MOE_TILE * LANE_CHUNKS), LANE_CHUNKS), MOE_TILE * LANE_CHUNKS)
        return pltpu.make_async_copy(x_hbm.at[rows, :], x_s, sem.at[2])

    @pl.when(e == 0)
    def _():
        @pl.when(tile == 0)
        def _():
            x_load(tile).start()
            for load in plan_loads(tile):
                load.start()

        load_base = pltpu.make_async_copy(base_hbm.at[tile_rows, :], acc_s, sem.at[3])
        load_base.start()
        y_b[...] = jnp.zeros_like(y_b)
        load_rows, load_gates = plan_loads(tile)
        load_rows.wait()
        x_load(tile).wait()
        gather_loop(xg_a, off)
        load_gates.wait()
        load_base.wait()

    prev_off = off_ref[jnp.maximum(pair - 1, 0)]
    next_off = off_ref[jnp.minimum(pair + 1, last_pair)]
    live_row = lax.broadcasted_iota(jnp.int32, (MOE_CHUNK, 2 * LANES), 0) < n

    def run_expert(xg_cur, y_cur, xg_nxt, y_prv):
        gathers = [functools.partial(gather_group, xg_nxt, next_off, jb) for jb in range(MOE_CHUNK // GATHER_GROUP)]
        scatters = [functools.partial(scatter_group, y_prv, prev_off, jg * SCATTER_GROUP, SCATTER_GROUP, True)
                    for jg in range(MOE_CHUNK // SCATTER_GROUP)]
        side = [s for both in zip(gathers, scatters) for s in both]
        n_down = LANE_CHUNKS // 2
        cost = [D_MODEL, D_MODEL] + [EXPERT_FF] * n_down
        bounds = [round(len(side) * sum(cost[:i]) / sum(cost)) for i in range(len(cost) + 1)]

        def side_work(i):
            for s in side[bounds[i]:bounds[i + 1]]:
                s()

        xb = _load_row_tiles(xg_cur, MOE_CHUNK).astype(BF16)
        side_work(0)
        hg = jnp.dot(xb, wg_ref[...], preferred_element_type=F32)
        side_work(1)
        hu = jnp.dot(xb, wu_ref[...], preferred_element_type=F32)
        gate_col = jnp.broadcast_to(seg_ref[...], (LANES, GATE_SEG)).T[:MOE_CHUNK]
        gate_col = jnp.concatenate([gate_col] * (EXPERT_FF // LANES), axis=1)
        act = (hg * jax.nn.sigmoid(hg) * hu * gate_col).astype(BF16)
        for q in range(n_down):
            side_work(2 + q)
            out = jnp.dot(act, wd_ref[:, q * 2 * LANES:(q + 1) * 2 * LANES], preferred_element_type=F32)
            out = jnp.where(live_row, out, 0.0)
            for c in range(2):
                y_cur[pl.ds(2 * q + c, MOE_CHUNK, stride=LANE_CHUNKS), :] = out[:, c * LANES:(c + 1) * LANES]

    @pl.when(e % 2 == 0)
    def _():
        run_expert(xg_a, y_a, xg_b, y_b)

    @pl.when(e % 2 == 1)
    def _():
        run_expert(xg_b, y_b, xg_a, y_a)

    def extra_chunk(c, carry):
        first = off + c * MOE_CHUNK
        gather_loop(xg_c, first)
        swiglu(xg_c, y_c)
        scatter_loop(y_c, first, jnp.minimum(MOE_CHUNK, n - c * MOE_CHUNK), False)
        return carry

    lax.fori_loop(1, (n + MOE_CHUNK - 1) // MOE_CHUNK, extra_chunk, 0)

    @pl.when(e == N_EXPERTS - 1)
    def _():
        more_tiles = tile + 1 < pl.num_programs(0)

        @pl.when(more_tiles)
        def _():
            x_load(tile + 1).start()

        scatter_loop(y_b, off, jnp.minimum(MOE_CHUNK, n), True)

        @pl.when(more_tiles)
        def _():
            for load in plan_loads(tile + 1):
                load.start()

        n_pieces = MOE_TILE // LN_ROWS
        store = lambda c: pltpu.make_async_copy(
            stage_s.at[c % 2], o_hbm.at[pl.ds(tile * MOE_TILE + c * LN_ROWS, LN_ROWS), :], sem.at[4 + c % 2])
        for c in range(n_pieces):
            z = _load_row_tiles(acc_s, LN_ROWS, c * LN_ROWS * LANE_CHUNKS)
            if c >= 2:
                store(c - 2).wait()
            stage_s[c % 2] = _layer_norm(z, lw_ref[...], lb_ref[...])
            store(c).start()
        store(n_pieces - 2).wait()
        store(n_pieces - 1).wait()


def _moe_tiles(x1, base, plan_rows, plan_gates, gate_seg, cnt, off, wg, wu, wd, ln_w, ln_b):
    T = x1.shape[0] // LANE_CHUNKS
    w_spec = lambda shape: pl.BlockSpec((None,) + shape, lambda i, e, cnt, off: (e, 0, 0))
    vec = pl.BlockSpec((1, D_MODEL), lambda i, e, cnt, off: (0, 0))
    hbm = pl.BlockSpec(memory_space=pl.ANY)
    tile_rows = MOE_TILE * LANE_CHUNKS
    return pl.pallas_call(
        _moe_tile_kernel,
        grid_spec=pltpu.PrefetchScalarGridSpec(
            num_scalar_prefetch=2,
            grid=(T // MOE_TILE, N_EXPERTS),
            in_specs=[w_spec((D_MODEL, EXPERT_FF)), w_spec((D_MODEL, EXPERT_FF)), w_spec((EXPERT_FF, D_MODEL)),
                      vec, vec,
                      pl.BlockSpec((None, 1, GATE_SEG), lambda i, e, cnt, off: (i * N_EXPERTS + e, 0, 0)),
                      hbm, hbm, hbm, hbm],
            out_specs=hbm,
            scratch_shapes=[pltpu.VMEM((tile_rows, LANES), F32),
                            pltpu.VMEM((tile_rows, LANES), F32)]
                           + [pltpu.VMEM((MOE_CHUNK * LANE_CHUNKS, LANES), F32)] * 6
                           + [pltpu.VMEM((2, LN_ROWS, D_MODEL), F32),
                              pltpu.SMEM((LIST_LEN,), jnp.int32),
                              pltpu.SMEM((LIST_LEN,), F32),
                              pltpu.SemaphoreType.DMA((6,))],
        ),
        out_shape=jax.ShapeDtypeStruct((T, D_MODEL), F32),
        compiler_params=_cparams("arbitrary", "arbitrary"),
        name="moe_tiles_ln2",
    )(cnt, off, wg, wu, wd, ln_w, ln_b, gate_seg, plan_rows, plan_gates, x1, base)


def kernel(x, p, w_in, hgrn_lb_logits, hgrn_norm_w, w_branch_att, w_branch_hgrn, w_out, ln1_w, ln1_b, router_w, router_bias, expert_w_gate, expert_w_up, expert_w_down, shared_w_gate, shared_w_up, shared_w_down, ple_gate_w, ple_proj_w, ln2_w, ln2_b):
    B, S, D = x.shape
    T = B * S
    l = 0
    x2d = x.reshape(T, D)
    bf = lambda a: a.astype(BF16)

    att_cols = 3 * len(ATT_DILATIONS) * ATT_WIDTH
    assert att_cols % PROJ_COLS == 0 and (w_in.shape[2] - att_cols) % PROJ_COLS == 0
    whole = lambda w: (w, w.shape[1], 0, 1)
    conv = [[(w_in[l], PROJ_COLS, att_cols // PROJ_COLS, (w_in.shape[2] - att_cols) // PROJ_COLS)],
            [whole(w_branch_att[l]), whole(w_branch_hgrn[l]), whole(w_out[l]), whole(ple_gate_w[l])]]
    qkv = [_proj_att(x2d, w_in[l], g, d, conv[g] if g < len(conv) else None) for g, d in enumerate(ATT_DILATIONS)]
    qkv[0], w_hg = qkv[0]
    qkv[1], (w_a_b, w_h_b, w_o_b, w_pg_b) = qkv[1]
    y_att = _attention(qkv, B, S)
    u_hg, wg_b, wu_b, wd_b = _proj(x2d, w_hg, (expert_w_gate[l], expert_w_up[l], expert_w_down[l]))
    y_hg = _hgrn(u_hg, hgrn_lb_logits, hgrn_norm_w[l:l + 1], B, S)
    x1 = _merge(y_att, y_hg, u_hg, x2d, w_a_b, w_h_b, w_o_b, ln1_w[l:l + 1], ln1_b[l:l + 1])

    idx, gate, rank, counts = _route(x1, jnp.stack(_split2(router_w[l].T)), router_bias[l].reshape(N_EXPERTS, 1))
    cnt = counts[:, :, 0]
    off = jnp.cumsum(cnt, axis=1) - cnt
    cnt, off = cnt.reshape(-1), off.reshape(-1)
    plan_rows, plan_gates, gate_seg = _plan_sc(off, idx, rank, gate)
    base = _dense(x1, p[l].reshape(T, PLE_DIM), bf(shared_w_gate[l]), bf(shared_w_up[l]), bf(shared_w_down[l]),
                  w_pg_b, bf(ple_proj_w[l]))
    out = _moe_tiles(x1, base, plan_rows, plan_gates, gate_seg, cnt, off, wg_b, wu_b, wd_b,
                     ln2_w[l:l + 1], ln2_b[l:l + 1])
    return out.reshape(B, S, D)
```
